```python
import jax, jax.numpy as jnp
from jax import lax
import numpy as np

D_MODEL = 1024
BATCH = 8
SEQ = 8192
DEPTH = 2

CHUNK = 64
PLE_DIM = 256
D_FF = 2816
CONV_W = 4
RET_HEADS = 4
RET_DK = 64
RET_DV = 64
RET_W = RET_HEADS * RET_DV
LRU_W = 384
LRU_BLOCKS = 6
LRU_BLOCK = LRU_W // LRU_BLOCKS
LRU_C = 8.0
GDN_HEADS = 6
GDN_DK = 64
GDN_DV = 64
GDN_W = GDN_HEADS * GDN_DV
D_MIX = RET_W + LRU_W + GDN_W
IN_WIDTHS = (RET_W, RET_W, RET_W, RET_W, LRU_W, LRU_W, GDN_W, GDN_W, GDN_W, GDN_W, GDN_HEADS, GDN_HEADS)
D_IN = sum(IN_WIDTHS)
ROPE_THETA = 10000.0
ALPHA = (2 * DEPTH) ** 0.25
BETA_INIT = (8 * DEPTH) ** -0.25
LN_EPS = 1e-5

kernel_name = 'hybrid_retention_rglru_gdn_macaron_deepnorm'


def layer_norm(x, g, b):
    xf = x.astype(jnp.float32)
    mu = jnp.mean(xf, -1, keepdims=True)
    var = jnp.mean(jnp.square(xf - mu), -1, keepdims=True)
    return ((xf - mu) * lax.rsqrt(var + LN_EPS) * g + b).astype(x.dtype)


def head_norm(t, eps=1e-5):
    mu = jnp.mean(t, -1, keepdims=True)
    return (t - mu) * lax.rsqrt(jnp.var(t, -1, keepdims=True) + eps)


def rms_norm(t, g, eps=1e-6):
    return t * lax.rsqrt(jnp.mean(t * t, -1, keepdims=True) + eps) * g


def l2_normalize(t, eps=1e-6):
    return t * lax.rsqrt(jnp.sum(t * t, -1, keepdims=True) + eps)


def swiglu(x, w_gate, w_up, w_down):
    return (jax.nn.silu(x @ w_gate) * (x @ w_up)) @ w_down


def causal_depthwise_conv(x, w, b=None):
    K = w.shape[0]
    S = x.shape[1]
    xp = jnp.pad(x, ((0, 0), (K - 1, 0), (0, 0)))
    y = sum(w[k] * xp[:, k:k + S] for k in range(K))
    return y if b is None else y + b


def rotary(t, positions):
    half = t.shape[-1] // 2
    inv_freq = ROPE_THETA ** (-jnp.arange(half, dtype=jnp.float32) / half)
    ang = positions.astype(jnp.float32)[..., None] * inv_freq
    cos = jnp.cos(ang)[:, :, None, :]
    sin = jnp.sin(ang)[:, :, None, :]
    t1, t2 = t[..., :half], t[..., half:]
    return jnp.concatenate([t1 * cos - t2 * sin, t2 * cos + t1 * sin], -1)


def retention_chunked(q, k, v, positions):
    B, S, H, _ = q.shape
    N = S // CHUNK
    q = rotary(q, positions) * RET_DK ** -0.5
    k = rotary(k, positions)
    log_gamma = jnp.log1p(-jnp.exp2(-5.0 - jnp.arange(H, dtype=jnp.float32)))
    idx = jnp.arange(CHUNK, dtype=jnp.float32)
    intra = jnp.exp(jnp.abs(idx[:, None] - idx[None, :])[None] * log_gamma[:, None, None])
    cross = jnp.exp((idx + 1.0)[None] * log_gamma[:, None])
    tail = jnp.exp((CHUNK - 1.0 - idx)[None] * log_gamma[:, None])
    chunk_decay = jnp.exp(CHUNK * log_gamma)
    qc = q.reshape(B, N, CHUNK, H, RET_DK)
    kc = k.reshape(B, N, CHUNK, H, RET_DK)
    vc = v.reshape(B, N, CHUNK, H, RET_DV)
    scores = jnp.einsum('bnihd,bnjhd->bnhij', qc, kc) * intra
    o_intra = jnp.einsum('bnhij,bnjhe->bnihe', scores, vc)
    kv = jnp.einsum('bnjhd,hj,bnjhe->nbhde', kc, tail, vc)

    def step(state, kv_n):
        return state * chunk_decay[None, :, None, None] + kv_n, state

    _, states = lax.scan(step, jnp.zeros((B, H, RET_DK, RET_DV), jnp.float32), kv)
    o_cross = jnp.einsum('bnihd,hi,nbhde->bnihe', qc, cross, states)
    return (o_intra + o_cross).reshape(B, S, H, RET_DV)


def rg_lru(x, conv_w, conv_b, w_a, b_a, w_x, b_x, lam):
    B, S, _ = x.shape
    x = causal_depthwise_conv(x, conv_w, conv_b)
    xb = x.reshape(B, S, LRU_BLOCKS, LRU_BLOCK)
    r = jax.nn.sigmoid(jnp.einsum('bsgi,gij->bsgj', xb, w_a).reshape(B, S, LRU_W) + b_a)
    i = jax.nn.sigmoid(jnp.einsum('bsgi,gij->bsgj', xb, w_x).reshape(B, S, LRU_W) + b_x)
    log_a = LRU_C * r * jax.nn.log_sigmoid(lam)
    a = jnp.exp(log_a)
    b = jnp.sqrt(-jnp.expm1(2.0 * log_a)) * (i * x)

    def combine(left, right):
        a1, b1 = left
        a2, b2 = right
        return a1 * a2, a2 * b1 + b2

    _, h = lax.associative_scan(combine, (a, b), axis=1)
    return h


def gated_delta_chunked(q, k, v, g, beta):
    B, S, H, DK = q.shape
    DV = v.shape[-1]
    N = S // CHUNK

    def to_chunks(t):
        return jnp.moveaxis(t.reshape(B, N, CHUNK, H, *t.shape[3:]), 3, 2)

    q = to_chunks(q) * DK ** -0.5
    k = to_chunks(k)
    v = to_chunks(v)
    g = to_chunks(g)
    beta = to_chunks(beta)
    gc = jnp.cumsum(g, axis=-1)
    incl = jnp.tril(jnp.ones((CHUNK, CHUNK), bool))
    strict = jnp.tril(jnp.ones((CHUNK, CHUNK), bool), -1)
    decay = jnp.where(incl, jnp.exp(jnp.minimum(gc[..., :, None] - gc[..., None, :], 0.0)), 0.0)
    kk = jnp.einsum('bnhid,bnhjd->bnhij', k, k)
    a_mat = jnp.where(strict, beta[..., :, None] * kk * decay, 0.0) + jnp.eye(CHUNK, dtype=jnp.float32)
    rhs = jnp.concatenate([v * beta[..., None], k * (beta * jnp.exp(gc))[..., None]], -1)
    sol = lax.linalg.triangular_solve(a_mat, rhs, left_side=True, lower=True, unit_diagonal=True)
    u, w = sol[..., :DV], sol[..., DV:]
    qk = jnp.einsum('bnhid,bnhjd->bnhij', q, k) * decay
    q_dec = q * jnp.exp(gc)[..., None]
    g_last = gc[..., -1]
    k_tail = k * jnp.exp(g_last[..., None] - gc)[..., None]
    xs = (jnp.moveaxis(u, 1, 0), jnp.moveaxis(w, 1, 0), jnp.moveaxis(qk, 1, 0),
          jnp.moveaxis(q_dec, 1, 0), jnp.moveaxis(k_tail, 1, 0), jnp.moveaxis(g_last, 1, 0))

    def step(state, inp):
        u_n, w_n, qk_n, qd_n, kt_n, gl_n = inp
        v_new = u_n - jnp.einsum('bhcd,bhde->bhce', w_n, state)
        o = jnp.einsum('bhcd,bhde->bhce', qd_n, state) + jnp.einsum('bhij,bhje->bhie', qk_n, v_new)
        state = state * jnp.exp(gl_n)[..., None, None] + jnp.einsum('bhcd,bhce->bhde', kt_n, v_new)
        return state, o

    _, o = lax.scan(step, jnp.zeros((B, H, DK, DV), jnp.float32), xs)
    return jnp.transpose(o, (1, 0, 3, 2, 4)).reshape(B, S, H, DV)


def hybrid_mixer(x, positions, w_in, ret_norm_g, lru_conv_w, lru_conv_b, lru_w_a, lru_b_a,
                 lru_w_x, lru_b_x, lru_lambda, gdn_conv_w, gdn_a_log, gdn_dt_bias, gdn_norm_g, w_out):
    B, S, _ = x.shape
    h = (x @ w_in).astype(jnp.float32)
    splits = np.cumsum(IN_WIDTHS)[:-1].tolist()
    q_r, k_r, v_r, g_r, x_l, gate_l, q_g, k_g, v_g, z_g, a_g, b_g = jnp.split(h, splits, axis=-1)

    o_r = retention_chunked(q_r.reshape(B, S, RET_HEADS, RET_DK), k_r.reshape(B, S, RET_HEADS, RET_DK),
                            v_r.reshape(B, S, RET_HEADS, RET_DV), positions)
    o_r = head_norm(o_r).reshape(B, S, RET_W) * ret_norm_g * jax.nn.silu(g_r)

    o_l = rg_lru(x_l, lru_conv_w, lru_conv_b, lru_w_a, lru_b_a, lru_w_x, lru_b_x, lru_lambda) * jax.nn.gelu(gate_l)

    qkv = jax.nn.silu(causal_depthwise_conv(jnp.concatenate([q_g, k_g, v_g], -1), gdn_conv_w))
    q_g, k_g, v_g = jnp.split(qkv, 3, axis=-1)
    q_g = l2_normalize(q_g.reshape(B, S, GDN_HEADS, GDN_DK))
    k_g = l2_normalize(k_g.reshape(B, S, GDN_HEADS, GDN_DK))
    v_g = v_g.reshape(B, S, GDN_HEADS, GDN_DV)
    beta = jax.nn.sigmoid(b_g)
    g = -jnp.exp(gdn_a_log) * jax.nn.softplus(a_g + gdn_dt_bias)
    o_g = gated_delta_chunked(q_g, k_g, v_g, g, beta)
    o_g = (rms_norm(o_g, gdn_norm_g) * jax.nn.silu(z_g.reshape(B, S, GDN_HEADS, GDN_DV))).reshape(B, S, GDN_W)

    o = jnp.concatenate([o_r, o_l, o_g], -1).astype(x.dtype)
    return o @ w_out


def _fwd_setup_inputs(seed: int = 0) -> dict:
    key = jax.random.key(seed)
    counter = [0]

    def nk():
        counter[0] += 1
        return jax.random.fold_in(key, counter[0])

    f32 = jnp.float32
    L = DEPTH

    def nrm(shape, fan_in, scale=1.0):
        return jax.random.normal(nk(), shape, f32) * (scale * fan_in ** -0.5)

    def gain(shape):
        return 1.0 + 0.02 * jax.random.normal(nk(), shape, f32)

    def bias(shape):
        return 0.02 * jax.random.normal(nk(), shape, f32)

    x = jax.random.normal(nk(), (BATCH, SEQ, D_MODEL), f32)
    p = jax.random.normal(nk(), (DEPTH, BATCH, SEQ, PLE_DIM), f32)
    start = jax.random.randint(nk(), (BATCH, 1), 0, 4096, jnp.int32)
    positions = start + jnp.arange(SEQ, dtype=jnp.int32)[None, :]
    a0 = jax.random.uniform(nk(), (L, LRU_W), f32, 0.9, 0.999)
    lru_lambda = jnp.log(a0) - jnp.log1p(-a0)
    gdn_a_log = jnp.log(jax.random.uniform(nk(), (L, GDN_HEADS), f32, 1.0, 16.0))
    dt = jnp.exp(jax.random.uniform(nk(), (L, GDN_HEADS), f32, np.log(1e-3), np.log(1e-1)))
    gdn_dt_bias = dt + jnp.log(-jnp.expm1(-dt))
    return {
        'x': x,
        'p': p,
        'positions': positions,
        'ln_ffn1_g': gain((L, D_MODEL)),
        'ln_ffn1_b': bias((L, D_MODEL)),
        'ffn1_w_gate': nrm((L, D_MODEL, D_FF), D_MODEL),
        'ffn1_w_up': nrm((L, D_MODEL, D_FF), D_MODEL),
        'ffn1_w_down': nrm((L, D_FF, D_MODEL), D_FF, BETA_INIT),
        'w_in': nrm((L, D_MODEL, D_IN), D_MODEL),
        'ret_norm_g': gain((L, RET_W)),
        'lru_conv_w': nrm((L, CONV_W, LRU_W), CONV_W),
        'lru_conv_b': bias((L, LRU_W)),
        'lru_w_a': nrm((L, LRU_BLOCKS, LRU_BLOCK, LRU_BLOCK), LRU_BLOCK),
        'lru_b_a': bias((L, LRU_W)),
        'lru_w_x': nrm((L, LRU_BLOCKS, LRU_BLOCK, LRU_BLOCK), LRU_BLOCK),
        'lru_b_x': bias((L, LRU_W)),
        'lru_lambda': lru_lambda,
        'gdn_conv_w': nrm((L, CONV_W, 3 * GDN_W), CONV_W),
        'gdn_a_log': gdn_a_log,
        'gdn_dt_bias': gdn_dt_bias,
        'gdn_norm_g': gain((L, GDN_DV)),
        'w_out': nrm((L, D_MIX, D_MODEL), D_MIX, BETA_INIT),
        'ln_mix_g': gain((L, D_MODEL)),
        'ln_mix_b': bias((L, D_MODEL)),
        'ffn2_w_gate': nrm((L, D_MODEL, D_FF), D_MODEL),
        'ffn2_w_up': nrm((L, D_MODEL, D_FF), D_MODEL),
        'ffn2_w_down': nrm((L, D_FF, D_MODEL), D_FF, BETA_INIT),
        'ple_w_gate': nrm((L, D_MODEL, D_MODEL), D_MODEL),
        'ple_w_proj': nrm((L, PLE_DIM, D_MODEL), PLE_DIM, BETA_INIT),
        'ln_ffn2_g': gain((L, D_MODEL)),
        'ln_ffn2_b': bias((L, D_MODEL)),
    }


def _fwd_reference(x, p, positions, ln_ffn1_g, ln_ffn1_b, ffn1_w_gate, ffn1_w_up, ffn1_w_down,
              w_in, ret_norm_g, lru_conv_w, lru_conv_b, lru_w_a, lru_b_a, lru_w_x, lru_b_x, lru_lambda,
              gdn_conv_w, gdn_a_log, gdn_dt_bias, gdn_norm_g, w_out, ln_mix_g, ln_mix_b,
              ffn2_w_gate, ffn2_w_up, ffn2_w_down, ple_w_gate, ple_w_proj, ln_ffn2_g, ln_ffn2_b):
    for i in range(DEPTH):
        x = layer_norm(ALPHA * x + 0.5 * swiglu(x, ffn1_w_gate[i], ffn1_w_up[i], ffn1_w_down[i]),
                       ln_ffn1_g[i], ln_ffn1_b[i])
        mix = hybrid_mixer(x, positions, w_in[i], ret_norm_g[i], lru_conv_w[i], lru_conv_b[i],
                           lru_w_a[i], lru_b_a[i], lru_w_x[i], lru_b_x[i], lru_lambda[i],
                           gdn_conv_w[i], gdn_a_log[i], gdn_dt_bias[i], gdn_norm_g[i], w_out[i])
        x = layer_norm(ALPHA * x + mix, ln_mix_g[i], ln_mix_b[i])
        ffn = 0.5 * swiglu(x, ffn2_w_gate[i], ffn2_w_up[i], ffn2_w_down[i])
        ple = jax.nn.sigmoid(x @ ple_w_gate[i]) * (p[i] @ ple_w_proj[i])
        x = layer_norm(ALPHA * x + ffn + ple, ln_ffn2_g[i], ln_ffn2_b[i])
    return x


import jax as _jax
import jax.numpy as _jnp

TWIN_FORMAT = 'train_step'
FWD_PARAMS = ['x', 'p', 'positions', 'ln_ffn1_g', 'ln_ffn1_b', 'ffn1_w_gate', 'ffn1_w_up', 'ffn1_w_down', 'w_in', 'ret_norm_g', 'lru_conv_w', 'lru_conv_b', 'lru_w_a', 'lru_b_a', 'lru_w_x', 'lru_b_x', 'lru_lambda', 'gdn_conv_w', 'gdn_a_log', 'gdn_dt_bias', 'gdn_norm_g', 'w_out', 'ln_mix_g', 'ln_mix_b', 'ffn2_w_gate', 'ffn2_w_up', 'ffn2_w_down', 'ple_w_gate', 'ple_w_proj', 'ln_ffn2_g', 'ln_ffn2_b']
TWIN_WEIGHTS = ['ln_ffn1_g', 'ln_ffn1_b', 'ffn1_w_gate', 'ffn1_w_up', 'ffn1_w_down', 'w_in', 'ret_norm_g', 'lru_conv_w', 'lru_conv_b', 'lru_w_a', 'lru_b_a', 'lru_w_x', 'lru_b_x', 'lru_lambda', 'gdn_conv_w', 'gdn_a_log', 'gdn_dt_bias', 'gdn_norm_g', 'w_out', 'ln_mix_g', 'ln_mix_b', 'ffn2_w_gate', 'ffn2_w_up', 'ffn2_w_down', 'ple_w_gate', 'ple_w_proj', 'ln_ffn2_g', 'ln_ffn2_b']
TWIN_DIFF_INPUT = 'x'
TWIN_INPUTS = ['x', 'p', 'positions', 'ln_ffn1_g', 'ln_ffn1_b', 'ffn1_w_gate', 'ffn1_w_up', 'ffn1_w_down', 'w_in', 'ret_norm_g', 'lru_conv_w', 'lru_conv_b', 'lru_w_a', 'lru_b_a', 'lru_w_x', 'lru_b_x', 'lru_lambda', 'gdn_conv_w', 'gdn_a_log', 'gdn_dt_bias', 'gdn_norm_g', 'w_out', 'ln_mix_g', 'ln_mix_b', 'ffn2_w_gate', 'ffn2_w_up', 'ffn2_w_down', 'ple_w_gate', 'ple_w_proj', 'ln_ffn2_g', 'ln_ffn2_b', 'loss_target', 'm_ln_ffn1_g', 'm_ln_ffn1_b', 'm_ffn1_w_gate', 'm_ffn1_w_up', 'm_ffn1_w_down', 'm_w_in', 'm_ret_norm_g', 'm_lru_conv_w', 'm_lru_conv_b', 'm_lru_w_a', 'm_lru_b_a', 'm_lru_w_x', 'm_lru_b_x', 'm_lru_lambda', 'm_gdn_conv_w', 'm_gdn_a_log', 'm_gdn_dt_bias', 'm_gdn_norm_g', 'm_w_out', 'm_ln_mix_g', 'm_ln_mix_b', 'm_ffn2_w_gate', 'm_ffn2_w_up', 'm_ffn2_w_down', 'm_ple_w_gate', 'm_ple_w_proj', 'm_ln_ffn2_g', 'm_ln_ffn2_b', 'v_ln_ffn1_g', 'v_ln_ffn1_b', 'v_ffn1_w_gate', 'v_ffn1_w_up', 'v_ffn1_w_down', 'v_w_in', 'v_ret_norm_g', 'v_lru_conv_w', 'v_lru_conv_b', 'v_lru_w_a', 'v_lru_b_a', 'v_lru_w_x', 'v_lru_b_x', 'v_lru_lambda', 'v_gdn_conv_w', 'v_gdn_a_log', 'v_gdn_dt_bias', 'v_gdn_norm_g', 'v_w_out', 'v_ln_mix_g', 'v_ln_mix_b', 'v_ffn2_w_gate', 'v_ffn2_w_up', 'v_ffn2_w_down', 'v_ple_w_gate', 'v_ple_w_proj', 'v_ln_ffn2_g', 'v_ln_ffn2_b']
TWIN_OUTPUTS = ['loss', 'grad_x', 'grad_ln_ffn1_g', 'grad_ln_ffn1_b', 'grad_ffn1_w_gate', 'grad_ffn1_w_up', 'grad_ffn1_w_down', 'grad_w_in', 'grad_ret_norm_g', 'grad_lru_conv_w', 'grad_lru_conv_b', 'grad_lru_w_a', 'grad_lru_b_a', 'grad_lru_w_x', 'grad_lru_b_x', 'grad_lru_lambda', 'grad_gdn_conv_w', 'grad_gdn_a_log', 'grad_gdn_dt_bias', 'grad_gdn_norm_g', 'grad_w_out', 'grad_ln_mix_g', 'grad_ln_mix_b', 'grad_ffn2_w_gate', 'grad_ffn2_w_up', 'grad_ffn2_w_down', 'grad_ple_w_gate', 'grad_ple_w_proj', 'grad_ln_ffn2_g', 'grad_ln_ffn2_b', 'delta_ln_ffn1_g', 'delta_ln_ffn1_b', 'delta_ffn1_w_gate', 'delta_ffn1_w_up', 'delta_ffn1_w_down', 'delta_w_in', 'delta_ret_norm_g', 'delta_lru_conv_w', 'delta_lru_conv_b', 'delta_lru_w_a', 'delta_lru_b_a', 'delta_lru_w_x', 'delta_lru_b_x', 'delta_lru_lambda', 'delta_gdn_conv_w', 'delta_gdn_a_log', 'delta_gdn_dt_bias', 'delta_gdn_norm_g', 'delta_w_out', 'delta_ln_mix_g', 'delta_ln_mix_b', 'delta_ffn2_w_gate', 'delta_ffn2_w_up', 'delta_ffn2_w_down', 'delta_ple_w_gate', 'delta_ple_w_proj', 'delta_ln_ffn2_g', 'delta_ln_ffn2_b', 'new_m_ln_ffn1_g', 'new_m_ln_ffn1_b', 'new_m_ffn1_w_gate', 'new_m_ffn1_w_up', 'new_m_ffn1_w_down', 'new_m_w_in', 'new_m_ret_norm_g', 'new_m_lru_conv_w', 'new_m_lru_conv_b', 'new_m_lru_w_a', 'new_m_lru_b_a', 'new_m_lru_w_x', 'new_m_lru_b_x', 'new_m_lru_lambda', 'new_m_gdn_conv_w', 'new_m_gdn_a_log', 'new_m_gdn_dt_bias', 'new_m_gdn_norm_g', 'new_m_w_out', 'new_m_ln_mix_g', 'new_m_ln_mix_b', 'new_m_ffn2_w_gate', 'new_m_ffn2_w_up', 'new_m_ffn2_w_down', 'new_m_ple_w_gate', 'new_m_ple_w_proj', 'new_m_ln_ffn2_g', 'new_m_ln_ffn2_b', 'new_v_ln_ffn1_g', 'new_v_ln_ffn1_b', 'new_v_ffn1_w_gate', 'new_v_ffn1_w_up', 'new_v_ffn1_w_down', 'new_v_w_in', 'new_v_ret_norm_g', 'new_v_lru_conv_w', 'new_v_lru_conv_b', 'new_v_lru_w_a', 'new_v_lru_b_a', 'new_v_lru_w_x', 'new_v_lru_b_x', 'new_v_lru_lambda', 'new_v_gdn_conv_w', 'new_v_gdn_a_log', 'new_v_gdn_dt_bias', 'new_v_gdn_norm_g', 'new_v_w_out', 'new_v_ln_mix_g', 'new_v_ln_mix_b', 'new_v_ffn2_w_gate', 'new_v_ffn2_w_up', 'new_v_ffn2_w_down', 'new_v_ple_w_gate', 'new_v_ple_w_proj', 'new_v_ln_ffn2_g', 'new_v_ln_ffn2_b']
TWIN_LEAF_KINDS = {'loss': 'loss', 'grad_x': 'grad_x', 'grad_ln_ffn1_g': 'grad_w', 'grad_ln_ffn1_b': 'grad_w', 'grad_ffn1_w_gate': 'grad_w', 'grad_ffn1_w_up': 'grad_w', 'grad_ffn1_w_down': 'grad_w', 'grad_w_in': 'grad_w', 'grad_ret_norm_g': 'grad_w', 'grad_lru_conv_w': 'grad_w', 'grad_lru_conv_b': 'grad_w', 'grad_lru_w_a': 'grad_w', 'grad_lru_b_a': 'grad_w', 'grad_lru_w_x': 'grad_w', 'grad_lru_b_x': 'grad_w', 'grad_lru_lambda': 'grad_w', 'grad_gdn_conv_w': 'grad_w', 'grad_gdn_a_log': 'grad_w', 'grad_gdn_dt_bias': 'grad_w', 'grad_gdn_norm_g': 'grad_w', 'grad_w_out': 'grad_w', 'grad_ln_mix_g': 'grad_w', 'grad_ln_mix_b': 'grad_w', 'grad_ffn2_w_gate': 'grad_w', 'grad_ffn2_w_up': 'grad_w', 'grad_ffn2_w_down': 'grad_w', 'grad_ple_w_gate': 'grad_w', 'grad_ple_w_proj': 'grad_w', 'grad_ln_ffn2_g': 'grad_w', 'grad_ln_ffn2_b': 'grad_w', 'delta_ln_ffn1_g': 'delta_w', 'delta_ln_ffn1_b': 'delta_w', 'delta_ffn1_w_gate': 'delta_w', 'delta_ffn1_w_up': 'delta_w', 'delta_ffn1_w_down': 'delta_w', 'delta_w_in': 'delta_w', 'delta_ret_norm_g': 'delta_w', 'delta_lru_conv_w': 'delta_w', 'delta_lru_conv_b': 'delta_w', 'delta_lru_w_a': 'delta_w', 'delta_lru_b_a': 'delta_w', 'delta_lru_w_x': 'delta_w', 'delta_lru_b_x': 'delta_w', 'delta_lru_lambda': 'delta_w', 'delta_gdn_conv_w': 'delta_w', 'delta_gdn_a_log': 'delta_w', 'delta_gdn_dt_bias': 'delta_w', 'delta_gdn_norm_g': 'delta_w', 'delta_w_out': 'delta_w', 'delta_ln_mix_g': 'delta_w', 'delta_ln_mix_b': 'delta_w', 'delta_ffn2_w_gate': 'delta_w', 'delta_ffn2_w_up': 'delta_w', 'delta_ffn2_w_down': 'delta_w', 'delta_ple_w_gate': 'delta_w', 'delta_ple_w_proj': 'delta_w', 'delta_ln_ffn2_g': 'delta_w', 'delta_ln_ffn2_b': 'delta_w', 'new_m_ln_ffn1_g': 'new_m', 'new_m_ln_ffn1_b': 'new_m', 'new_m_ffn1_w_gate': 'new_m', 'new_m_ffn1_w_up': 'new_m', 'new_m_ffn1_w_down': 'new_m', 'new_m_w_in': 'new_m', 'new_m_ret_norm_g': 'new_m', 'new_m_lru_conv_w': 'new_m', 'new_m_lru_conv_b': 'new_m', 'new_m_lru_w_a': 'new_m', 'new_m_lru_b_a': 'new_m', 'new_m_lru_w_x': 'new_m', 'new_m_lru_b_x': 'new_m', 'new_m_lru_lambda': 'new_m', 'new_m_gdn_conv_w': 'new_m', 'new_m_gdn_a_log': 'new_m', 'new_m_gdn_dt_bias': 'new_m', 'new_m_gdn_norm_g': 'new_m', 'new_m_w_out': 'new_m', 'new_m_ln_mix_g': 'new_m', 'new_m_ln_mix_b': 'new_m', 'new_m_ffn2_w_gate': 'new_m', 'new_m_ffn2_w_up': 'new_m', 'new_m_ffn2_w_down': 'new_m', 'new_m_ple_w_gate': 'new_m', 'new_m_ple_w_proj': 'new_m', 'new_m_ln_ffn2_g': 'new_m', 'new_m_ln_ffn2_b': 'new_m', 'new_v_ln_ffn1_g': 'new_v', 'new_v_ln_ffn1_b': 'new_v', 'new_v_ffn1_w_gate': 'new_v', 'new_v_ffn1_w_up': 'new_v', 'new_v_ffn1_w_down': 'new_v', 'new_v_w_in': 'new_v', 'new_v_ret_norm_g': 'new_v', 'new_v_lru_conv_w': 'new_v', 'new_v_lru_conv_b': 'new_v', 'new_v_lru_w_a': 'new_v', 'new_v_lru_b_a': 'new_v', 'new_v_lru_w_x': 'new_v', 'new_v_lru_b_x': 'new_v', 'new_v_lru_lambda': 'new_v', 'new_v_gdn_conv_w': 'new_v', 'new_v_gdn_a_log': 'new_v', 'new_v_gdn_dt_bias': 'new_v', 'new_v_gdn_norm_g': 'new_v', 'new_v_w_out': 'new_v', 'new_v_ln_mix_g': 'new_v', 'new_v_ln_mix_b': 'new_v', 'new_v_ffn2_w_gate': 'new_v', 'new_v_ffn2_w_up': 'new_v', 'new_v_ffn2_w_down': 'new_v', 'new_v_ple_w_gate': 'new_v', 'new_v_ple_w_proj': 'new_v', 'new_v_ln_ffn2_g': 'new_v', 'new_v_ln_ffn2_b': 'new_v'}


def _forward(args):
    return _fwd_reference(*[args[k] for k in FWD_PARAMS])


def _output_shape():
    out = _jax.eval_shape(lambda: _forward(_fwd_setup_inputs(0)))
    return out.shape, out.dtype

N_MICROBATCH = 1
ADAM_LR = 0.001
ADAM_B1 = 0.9
ADAM_B2 = 0.999
ADAM_EPS = 1e-08
ADAM_WD = 0.01
ADAM_STEP = 10
PER_EXAMPLE_BATCH_AXIS = {'x': 0, 'p': 1, 'positions': 0, 'loss_target': 0}
SHARED_INPUTS = []
_WEIGHT_DTYPES = {'ln_ffn1_g': _jnp.float32, 'ln_ffn1_b': _jnp.float32, 'ffn1_w_gate': _jnp.float32, 'ffn1_w_up': _jnp.float32, 'ffn1_w_down': _jnp.float32, 'w_in': _jnp.float32, 'ret_norm_g': _jnp.float32, 'lru_conv_w': _jnp.float32, 'lru_conv_b': _jnp.float32, 'lru_w_a': _jnp.float32, 'lru_b_a': _jnp.float32, 'lru_w_x': _jnp.float32, 'lru_b_x': _jnp.float32, 'lru_lambda': _jnp.float32, 'gdn_conv_w': _jnp.float32, 'gdn_a_log': _jnp.float32, 'gdn_dt_bias': _jnp.float32, 'gdn_norm_g': _jnp.float32, 'w_out': _jnp.float32, 'ln_mix_g': _jnp.float32, 'ln_mix_b': _jnp.float32, 'ffn2_w_gate': _jnp.float32, 'ffn2_w_up': _jnp.float32, 'ffn2_w_down': _jnp.float32, 'ple_w_gate': _jnp.float32, 'ple_w_proj': _jnp.float32, 'ln_ffn2_g': _jnp.float32, 'ln_ffn2_b': _jnp.float32}
MOMENT_SCALE = {'ln_ffn1_g': 1.648656e+00, 'ln_ffn1_b': 7.227539e-01, 'ffn1_w_gate': 1.729982e-02, 'ffn1_w_up': 1.672198e-02, 'ffn1_w_down': 5.537944e-02, 'w_in': 4.937481e-02, 'ret_norm_g': 5.715962e-02, 'lru_conv_w': 3.523877e-02, 'lru_conv_b': 1.921516e-01, 'lru_w_a': 8.290471e-03, 'lru_b_a': 8.093973e-03, 'lru_w_x': 1.470073e-02, 'lru_b_x': 1.356320e-02, 'lru_lambda': 1.682982e-02, 'gdn_conv_w': 5.141812e-02, 'gdn_a_log': 3.341751e-01, 'gdn_dt_bias': 3.156368e-01, 'gdn_norm_g': 1.250204e-01, 'w_out': 9.594440e-02, 'ln_mix_g': 1.735535e+00, 'ln_mix_b': 7.384552e-01, 'ffn2_w_gate': 1.640147e-02, 'ffn2_w_up': 1.592111e-02, 'ffn2_w_down': 5.271708e-02, 'ple_w_gate': 1.875629e-02, 'ple_w_proj': 9.583924e-02, 'ln_ffn2_g': 4.540607e+01, 'ln_ffn2_b': 1.941277e+00}


def _to_microbatches(a, axis):
    t = _jnp.moveaxis(a, axis, 0)
    t = t.reshape((N_MICROBATCH, t.shape[0] // N_MICROBATCH) + t.shape[1:])
    return _jnp.moveaxis(t, 1, axis + 1)


def setup_inputs(seed: int = 0) -> dict:
    inp = _fwd_setup_inputs(seed)
    key = _jax.random.fold_in(_jax.random.key(seed), 7919)
    shape, _ = _output_shape()
    out = dict(inp)
    out["loss_target"] = _jax.random.normal(_jax.random.fold_in(key, 0), shape, _jnp.float32)
    for i, name in enumerate(TWIN_WEIGHTS):
        w = inp[name].astype(_jnp.float32)
        if MOMENT_SCALE is None:
            s = _jnp.sqrt(_jnp.mean(_jnp.square(w)) + 1e-30)
        else:
            s = MOMENT_SCALE[name]
        km, kv = _jax.random.split(_jax.random.fold_in(key, i + 1))
        out[name] = w
        out["m_" + name] = s * _jax.random.normal(km, w.shape, _jnp.float32)
        out["v_" + name] = (s * s) * _jax.random.uniform(kv, w.shape, _jnp.float32, 0.5, 1.5)
    if N_MICROBATCH > 1:
        for name, axis in PER_EXAMPLE_BATCH_AXIS.items():
            out[name] = _to_microbatches(out[name], axis)
    return {'x': out['x'], 'p': out['p'], 'positions': out['positions'], 'ln_ffn1_g': out['ln_ffn1_g'], 'ln_ffn1_b': out['ln_ffn1_b'], 'ffn1_w_gate': out['ffn1_w_gate'], 'ffn1_w_up': out['ffn1_w_up'], 'ffn1_w_down': out['ffn1_w_down'], 'w_in': out['w_in'], 'ret_norm_g': out['ret_norm_g'], 'lru_conv_w': out['lru_conv_w'], 'lru_conv_b': out['lru_conv_b'], 'lru_w_a': out['lru_w_a'], 'lru_b_a': out['lru_b_a'], 'lru_w_x': out['lru_w_x'], 'lru_b_x': out['lru_b_x'], 'lru_lambda': out['lru_lambda'], 'gdn_conv_w': out['gdn_conv_w'], 'gdn_a_log': out['gdn_a_log'], 'gdn_dt_bias': out['gdn_dt_bias'], 'gdn_norm_g': out['gdn_norm_g'], 'w_out': out['w_out'], 'ln_mix_g': out['ln_mix_g'], 'ln_mix_b': out['ln_mix_b'], 'ffn2_w_gate': out['ffn2_w_gate'], 'ffn2_w_up': out['ffn2_w_up'], 'ffn2_w_down': out['ffn2_w_down'], 'ple_w_gate': out['ple_w_gate'], 'ple_w_proj': out['ple_w_proj'], 'ln_ffn2_g': out['ln_ffn2_g'], 'ln_ffn2_b': out['ln_ffn2_b'], 'loss_target': out['loss_target'], 'm_ln_ffn1_g': out['m_ln_ffn1_g'], 'm_ln_ffn1_b': out['m_ln_ffn1_b'], 'm_ffn1_w_gate': out['m_ffn1_w_gate'], 'm_ffn1_w_up': out['m_ffn1_w_up'], 'm_ffn1_w_down': out['m_ffn1_w_down'], 'm_w_in': out['m_w_in'], 'm_ret_norm_g': out['m_ret_norm_g'], 'm_lru_conv_w': out['m_lru_conv_w'], 'm_lru_conv_b': out['m_lru_conv_b'], 'm_lru_w_a': out['m_lru_w_a'], 'm_lru_b_a': out['m_lru_b_a'], 'm_lru_w_x': out['m_lru_w_x'], 'm_lru_b_x': out['m_lru_b_x'], 'm_lru_lambda': out['m_lru_lambda'], 'm_gdn_conv_w': out['m_gdn_conv_w'], 'm_gdn_a_log': out['m_gdn_a_log'], 'm_gdn_dt_bias': out['m_gdn_dt_bias'], 'm_gdn_norm_g': out['m_gdn_norm_g'], 'm_w_out': out['m_w_out'], 'm_ln_mix_g': out['m_ln_mix_g'], 'm_ln_mix_b': out['m_ln_mix_b'], 'm_ffn2_w_gate': out['m_ffn2_w_gate'], 'm_ffn2_w_up': out['m_ffn2_w_up'], 'm_ffn2_w_down': out['m_ffn2_w_down'], 'm_ple_w_gate': out['m_ple_w_gate'], 'm_ple_w_proj': out['m_ple_w_proj'], 'm_ln_ffn2_g': out['m_ln_ffn2_g'], 'm_ln_ffn2_b': out['m_ln_ffn2_b'], 'v_ln_ffn1_g': out['v_ln_ffn1_g'], 'v_ln_ffn1_b': out['v_ln_ffn1_b'], 'v_ffn1_w_gate': out['v_ffn1_w_gate'], 'v_ffn1_w_up': out['v_ffn1_w_up'], 'v_ffn1_w_down': out['v_ffn1_w_down'], 'v_w_in': out['v_w_in'], 'v_ret_norm_g': out['v_ret_norm_g'], 'v_lru_conv_w': out['v_lru_conv_w'], 'v_lru_conv_b': out['v_lru_conv_b'], 'v_lru_w_a': out['v_lru_w_a'], 'v_lru_b_a': out['v_lru_b_a'], 'v_lru_w_x': out['v_lru_w_x'], 'v_lru_b_x': out['v_lru_b_x'], 'v_lru_lambda': out['v_lru_lambda'], 'v_gdn_conv_w': out['v_gdn_conv_w'], 'v_gdn_a_log': out['v_gdn_a_log'], 'v_gdn_dt_bias': out['v_gdn_dt_bias'], 'v_gdn_norm_g': out['v_gdn_norm_g'], 'v_w_out': out['v_w_out'], 'v_ln_mix_g': out['v_ln_mix_g'], 'v_ln_mix_b': out['v_ln_mix_b'], 'v_ffn2_w_gate': out['v_ffn2_w_gate'], 'v_ffn2_w_up': out['v_ffn2_w_up'], 'v_ffn2_w_down': out['v_ffn2_w_down'], 'v_ple_w_gate': out['v_ple_w_gate'], 'v_ple_w_proj': out['v_ple_w_proj'], 'v_ln_ffn2_g': out['v_ln_ffn2_g'], 'v_ln_ffn2_b': out['v_ln_ffn2_b']}


def _loss(weights, diff, rest, loss_target):
    with _jax.named_scope("forward"):
        args = {**rest, TWIN_DIFF_INPUT: diff, **{k: w.astype(_WEIGHT_DTYPES[k]) for k, w in weights.items()}}
        y = _forward(args)
    with _jax.named_scope("loss_head"):
        err = _jnp.square(y.astype(_jnp.float32) - loss_target)
        return 0.5 * _jnp.sum(_jnp.mean(err, axis=-1)) if err.ndim else 0.5 * err


def _adamw(w, g, m, v):
    m = ADAM_B1 * m + (1.0 - ADAM_B1) * g
    v = ADAM_B2 * v + (1.0 - ADAM_B2) * _jnp.square(g)
    m_hat = m / (1.0 - ADAM_B1 ** ADAM_STEP)
    v_hat = v / (1.0 - ADAM_B2 ** ADAM_STEP)
    delta = -ADAM_LR * (m_hat / (_jnp.sqrt(v_hat) + ADAM_EPS) + ADAM_WD * w)
    return delta, m, v


def reference(x, p, positions, ln_ffn1_g, ln_ffn1_b, ffn1_w_gate, ffn1_w_up, ffn1_w_down, w_in, ret_norm_g, lru_conv_w, lru_conv_b, lru_w_a, lru_b_a, lru_w_x, lru_b_x, lru_lambda, gdn_conv_w, gdn_a_log, gdn_dt_bias, gdn_norm_g, w_out, ln_mix_g, ln_mix_b, ffn2_w_gate, ffn2_w_up, ffn2_w_down, ple_w_gate, ple_w_proj, ln_ffn2_g, ln_ffn2_b, loss_target, m_ln_ffn1_g, m_ln_ffn1_b, m_ffn1_w_gate, m_ffn1_w_up, m_ffn1_w_down, m_w_in, m_ret_norm_g, m_lru_conv_w, m_lru_conv_b, m_lru_w_a, m_lru_b_a, m_lru_w_x, m_lru_b_x, m_lru_lambda, m_gdn_conv_w, m_gdn_a_log, m_gdn_dt_bias, m_gdn_norm_g, m_w_out, m_ln_mix_g, m_ln_mix_b, m_ffn2_w_gate, m_ffn2_w_up, m_ffn2_w_down, m_ple_w_gate, m_ple_w_proj, m_ln_ffn2_g, m_ln_ffn2_b, v_ln_ffn1_g, v_ln_ffn1_b, v_ffn1_w_gate, v_ffn1_w_up, v_ffn1_w_down, v_w_in, v_ret_norm_g, v_lru_conv_w, v_lru_conv_b, v_lru_w_a, v_lru_b_a, v_lru_w_x, v_lru_b_x, v_lru_lambda, v_gdn_conv_w, v_gdn_a_log, v_gdn_dt_bias, v_gdn_norm_g, v_w_out, v_ln_mix_g, v_ln_mix_b, v_ffn2_w_gate, v_ffn2_w_up, v_ffn2_w_down, v_ple_w_gate, v_ple_w_proj, v_ln_ffn2_g, v_ln_ffn2_b):
    given = dict(x=x, p=p, positions=positions, ln_ffn1_g=ln_ffn1_g, ln_ffn1_b=ln_ffn1_b, ffn1_w_gate=ffn1_w_gate, ffn1_w_up=ffn1_w_up, ffn1_w_down=ffn1_w_down, w_in=w_in, ret_norm_g=ret_norm_g, lru_conv_w=lru_conv_w, lru_conv_b=lru_conv_b, lru_w_a=lru_w_a, lru_b_a=lru_b_a, lru_w_x=lru_w_x, lru_b_x=lru_b_x, lru_lambda=lru_lambda, gdn_conv_w=gdn_conv_w, gdn_a_log=gdn_a_log, gdn_dt_bias=gdn_dt_bias, gdn_norm_g=gdn_norm_g, w_out=w_out, ln_mix_g=ln_mix_g, ln_mix_b=ln_mix_b, ffn2_w_gate=ffn2_w_gate, ffn2_w_up=ffn2_w_up, ffn2_w_down=ffn2_w_down, ple_w_gate=ple_w_gate, ple_w_proj=ple_w_proj, ln_ffn2_g=ln_ffn2_g, ln_ffn2_b=ln_ffn2_b, loss_target=loss_target, m_ln_ffn1_g=m_ln_ffn1_g, m_ln_ffn1_b=m_ln_ffn1_b, m_ffn1_w_gate=m_ffn1_w_gate, m_ffn1_w_up=m_ffn1_w_up, m_ffn1_w_down=m_ffn1_w_down, m_w_in=m_w_in, m_ret_norm_g=m_ret_norm_g, m_lru_conv_w=m_lru_conv_w, m_lru_conv_b=m_lru_conv_b, m_lru_w_a=m_lru_w_a, m_lru_b_a=m_lru_b_a, m_lru_w_x=m_lru_w_x, m_lru_b_x=m_lru_b_x, m_lru_lambda=m_lru_lambda, m_gdn_conv_w=m_gdn_conv_w, m_gdn_a_log=m_gdn_a_log, m_gdn_dt_bias=m_gdn_dt_bias, m_gdn_norm_g=m_gdn_norm_g, m_w_out=m_w_out, m_ln_mix_g=m_ln_mix_g, m_ln_mix_b=m_ln_mix_b, m_ffn2_w_gate=m_ffn2_w_gate, m_ffn2_w_up=m_ffn2_w_up, m_ffn2_w_down=m_ffn2_w_down, m_ple_w_gate=m_ple_w_gate, m_ple_w_proj=m_ple_w_proj, m_ln_ffn2_g=m_ln_ffn2_g, m_ln_ffn2_b=m_ln_ffn2_b, v_ln_ffn1_g=v_ln_ffn1_g, v_ln_ffn1_b=v_ln_ffn1_b, v_ffn1_w_gate=v_ffn1_w_gate, v_ffn1_w_up=v_ffn1_w_up, v_ffn1_w_down=v_ffn1_w_down, v_w_in=v_w_in, v_ret_norm_g=v_ret_norm_g, v_lru_conv_w=v_lru_conv_w, v_lru_conv_b=v_lru_conv_b, v_lru_w_a=v_lru_w_a, v_lru_b_a=v_lru_b_a, v_lru_w_x=v_lru_w_x, v_lru_b_x=v_lru_b_x, v_lru_lambda=v_lru_lambda, v_gdn_conv_w=v_gdn_conv_w, v_gdn_a_log=v_gdn_a_log, v_gdn_dt_bias=v_gdn_dt_bias, v_gdn_norm_g=v_gdn_norm_g, v_w_out=v_w_out, v_ln_mix_g=v_ln_mix_g, v_ln_mix_b=v_ln_mix_b, v_ffn2_w_gate=v_ffn2_w_gate, v_ffn2_w_up=v_ffn2_w_up, v_ffn2_w_down=v_ffn2_w_down, v_ple_w_gate=v_ple_w_gate, v_ple_w_proj=v_ple_w_proj, v_ln_ffn2_g=v_ln_ffn2_g, v_ln_ffn2_b=v_ln_ffn2_b)
    weights = {n: given[n] for n in TWIN_WEIGHTS}
    shared = {n: given[n] for n in SHARED_INPUTS}
    per_example = {n: given[n] for n in ['x', 'p', 'positions']}
    grad_fn = _jax.value_and_grad(_loss, argnums=(0, 1))

    def one_microbatch(ex, loss_target):
        ex = dict(ex)
        diff = ex.pop(TWIN_DIFF_INPUT)
        return grad_fn(weights, diff, {**shared, **ex}, loss_target)

    if N_MICROBATCH == 1:
        loss, (grad_w, grad_x) = one_microbatch(per_example, given["loss_target"])
    else:
        def body(carry, xs):
            loss_sum, grad_sum = carry
            l_k, (gw_k, gx_k) = one_microbatch(xs[0], xs[1])
            with _jax.named_scope("update"):
                return (loss_sum + l_k, _jax.tree.map(_jnp.add, grad_sum, gw_k)), gx_k

        init = (_jnp.zeros((), _jnp.float32), _jax.tree.map(_jnp.zeros_like, weights))
        (loss, grad_w), grad_x = _jax.lax.scan(body, init, (per_example, given["loss_target"]))
    with _jax.named_scope("update"):
        delta_w, new_m, new_v = {}, {}, {}
        for n in TWIN_WEIGHTS:
            delta_w[n], new_m[n], new_v[n] = _adamw(weights[n], grad_w[n], given["m_" + n], given["v_" + n])
    return (loss, grad_x, *[grad_w[n] for n in TWIN_WEIGHTS], *[delta_w[n] for n in TWIN_WEIGHTS],
            *[new_m[n] for n in TWIN_WEIGHTS], *[new_v[n] for n in TWIN_WEIGHTS])
```

```python
import functools
import math

import jax
import jax.numpy as jnp
from jax import lax
from jax.experimental import pallas as pl
from jax.experimental.pallas import tpu as pltpu

f32 = jnp.float32
bf16 = jnp.bfloat16
HI = lax.Precision.HIGHEST

NDEV = 8
DEPTH = 2
D = 1024
FS = 352
FSP = 384
PLE = 256
CH = 64
RET_H, GDN_H = 4, 6
RET_W, LRU_W, GDN_W = 256, 384, 384
GDN_IN = 1664
GDN_IN_REAL = 1548
D_IN = 3340
ALPHA = 4.0 ** 0.25
LN_EPS = 1e-5
ROPE_THETA = 10000.0
TM = 512
RB_RET, RB_LRU, RB_GDN = 512, 512, 128
VMEM_LIMIT = 56 * 1024 * 1024
ADAM_LR, ADAM_B1, ADAM_B2, ADAM_EPS, ADAM_WD, ADAM_STEP = 0.001, 0.9, 0.999, 1e-08, 0.01, 10

WEIGHTS = ['ln_ffn1_g', 'ln_ffn1_b', 'ffn1_w_gate', 'ffn1_w_up', 'ffn1_w_down', 'w_in', 'ret_norm_g', 'lru_conv_w',
           'lru_conv_b', 'lru_w_a', 'lru_b_a', 'lru_w_x', 'lru_b_x', 'lru_lambda', 'gdn_conv_w', 'gdn_a_log',
           'gdn_dt_bias', 'gdn_norm_g', 'w_out', 'ln_mix_g', 'ln_mix_b', 'ffn2_w_gate', 'ffn2_w_up', 'ffn2_w_down',
           'ple_w_gate', 'ple_w_proj', 'ln_ffn2_g', 'ln_ffn2_b']
BIG = ['ffn1_w_gate', 'ffn1_w_up', 'ffn1_w_down', 'w_in', 'w_out', 'ffn2_w_gate', 'ffn2_w_up', 'ffn2_w_down',
       'ple_w_gate', 'ple_w_proj']
SMALL = [n for n in WEIGHTS if n not in BIG]
CONV_SHARDED = {'lru_conv_w': LRU_W, 'gdn_conv_w': 3 * GDN_W}


def _cparams(sem=None):
    return pltpu.CompilerParams(dimension_semantics=sem, vmem_limit_bytes=VMEM_LIMIT)


def _sigmoid(x):
    return 1.0 / (1.0 + jnp.exp(-x))


def _silu(x):
    return x * _sigmoid(x)


def _dsilu(x):
    s = _sigmoid(x)
    return s * (1.0 + x * (1.0 - s))


def _softplus(x):
    return jnp.maximum(x, 0.0) + jnp.log(1.0 + jnp.exp(-jnp.abs(x)))


def _gelu(x):
    return 0.5 * x * (1.0 + jnp.tanh(0.7978845608028654 * (x + 0.044715 * x * x * x)))


def _dot(a, b):
    return jnp.dot(a.astype(bf16), b.astype(bf16), preferred_element_type=f32)


def _dot_nt(a, b):
    return lax.dot_general(a.astype(bf16), b.astype(bf16), (((1,), (1,)), ((), ())), preferred_element_type=f32)


def _dot_tn(a, b):
    return lax.dot_general(a.astype(bf16), b.astype(bf16), (((0,), (0,)), ((), ())), preferred_element_type=f32)


def _bmm(eq, a, b, hi=False):
    if hi:
        return jnp.einsum(eq, a, b, precision=HI, preferred_element_type=f32)
    return jnp.einsum(eq, a.astype(bf16), b.astype(bf16), preferred_element_type=f32)


def _ln_stats(z):
    mu = jnp.mean(z, -1, keepdims=True)
    zc = z - mu
    rstd = lax.rsqrt(jnp.mean(zc * zc, -1, keepdims=True) + LN_EPS)
    return zc * rstd, rstd


def _ln_bwd(z, g, dout):
    xh, rstd = _ln_stats(z)
    dxh = dout * g
    dz = rstd * (dxh - jnp.mean(dxh, -1, keepdims=True) - xh * jnp.mean(dxh * xh, -1, keepdims=True))
    return dz, jnp.sum(dout * xh, 0, keepdims=True), jnp.sum(dout, 0, keepdims=True)


def _full_spec(shape):
    nd = len(shape)
    return pl.BlockSpec(shape, lambda *_: (0,) * nd)


def _ffn_fwd(x, p384, pd, lg, lb, layer, which, ple=None):
    T = x.shape[0]
    sg, su, sd = 4 * layer + 2 * which, 4 * layer + 2 * which + 1, 2 * layer + which
    has_ple = ple is not None

    def body(*refs):
        if has_ple:
            x_ref, wg_ref, wu_ref, wd_ref, lg_ref, lb_ref, p_ref, wpg_ref, wpp_ref, z_ref, o_ref, acc = refs
        else:
            x_ref, wg_ref, wu_ref, wd_ref, lg_ref, lb_ref, z_ref, o_ref, acc = refs
        f = pl.program_id(1)
        xb = x_ref[...].astype(bf16)

        @pl.when(f == 0)
        def _():
            base = ALPHA * x_ref[...]
            if has_ple:
                gate = _sigmoid(_dot(xb, wpg_ref[...].reshape(D, D)))
                base = base + gate * _dot(p_ref[...], wpp_ref[...])
            acc[...] = base

        g = _dot(xb, wg_ref[...])
        u = _dot(xb, wu_ref[...])
        acc[...] += 0.5 * _dot(_silu(g) * u, wd_ref[...])

        @pl.when(f == NDEV - 1)
        def _():
            z = acc[...]
            z_ref[...] = z
            o_ref[...] = _ln_stats(z)[0] * lg_ref[...] + lb_ref[...]

    row = pl.BlockSpec((TM, D), lambda i, f: (i, 0))
    in_specs = [row,
                pl.BlockSpec((None, None, D, FSP), lambda i, f: (f, sg, 0, 0)),
                pl.BlockSpec((None, None, D, FSP), lambda i, f: (f, su, 0, 0)),
                pl.BlockSpec((None, None, FSP, D), lambda i, f: (f, sd, 0, 0)),
                _full_spec((1, D)), _full_spec((1, D))]
    args = [x, p384, p384, pd, lg, lb]
    if has_ple:
        p, pr, wpp = ple
        in_specs += [pl.BlockSpec((TM, PLE), lambda i, f: (i, 0)),
                     pl.BlockSpec((NDEV, None, 128, D), lambda i, f: (0, 3 * layer + 2, 0, 0)),
                     _full_spec((PLE, D))]
        args += [p, pr, wpp]
    return pl.pallas_call(
        body, grid=(T // TM, NDEV), in_specs=in_specs, out_specs=[row, row],
        out_shape=[jax.ShapeDtypeStruct((T, D), f32)] * 2,
        scratch_shapes=[pltpu.VMEM((TM, D), f32)],
        compiler_params=_cparams(("arbitrary", "arbitrary")), name=f"ffn{which + 1}_fwd")(*args)


def _ffn_bwd(x, z, dout, p384, pd, lg, layer, which, ple=None):
    T = x.shape[0]
    sg, su, sd = 4 * layer + 2 * which, 4 * layer + 2 * which + 1, 2 * layer + which
    has_ple = ple is not None

    def body(*refs):
        if has_ple:
            (x_ref, z_ref, do_ref, wg_ref, wu_ref, wd_ref, lg_ref, p_ref, wpg_ref, wpp_ref,
             dx_ref, dg_ref, du_ref, a_ref, dy_ref, dlg_ref, dlb_ref, dgp_ref, dpj_ref, acc, dyb) = refs
        else:
            (x_ref, z_ref, do_ref, wg_ref, wu_ref, wd_ref, lg_ref,
             dx_ref, dg_ref, du_ref, a_ref, dy_ref, dlg_ref, dlb_ref, acc, dyb) = refs
        i, f = pl.program_id(0), pl.program_id(1)
        xb = x_ref[...].astype(bf16)

        @pl.when(jnp.logical_and(i == 0, f == 0))
        def _():
            dlg_ref[...] = jnp.zeros_like(dlg_ref)
            dlb_ref[...] = jnp.zeros_like(dlb_ref)

        @pl.when(f == 0)
        def _():
            dz, dlg, dlb = _ln_bwd(z_ref[...], lg_ref[...], do_ref[...])
            dlg_ref[...] += dlg
            dlb_ref[...] += dlb
            dy = (0.5 * dz).astype(bf16)
            dyb[...] = dy
            dy_ref[...] = dy
            dx = ALPHA * dz
            if has_ple:
                wpg = wpg_ref[...].reshape(D, D)
                gate = _sigmoid(_dot(xb, wpg))
                proj = _dot(p_ref[...], wpp_ref[...])
                dgp = (dz * proj * gate * (1.0 - gate)).astype(bf16)
                dgp_ref[...] = dgp
                dpj_ref[...] = (dz * gate).astype(bf16)
                dx = dx + _dot_nt(dgp, wpg)
            acc[...] = dx

        g = _dot(xb, wg_ref[...])
        u = _dot(xb, wu_ref[...])
        da = _dot_nt(dyb[...], wd_ref[...])
        dg = (da * u * _dsilu(g)).astype(bf16)
        du = (da * _silu(g)).astype(bf16)
        dg_ref[...] = dg
        du_ref[...] = du
        a_ref[...] = (_silu(g) * u).astype(bf16)
        acc[...] += _dot_nt(dg, wg_ref[...]) + _dot_nt(du, wu_ref[...])

        @pl.when(f == NDEV - 1)
        def _():
            dx_ref[...] = acc[...]

    row = pl.BlockSpec((TM, D), lambda i, f: (i, 0))
    hid = pl.BlockSpec((TM, FSP), lambda i, f: (i, f))
    vec = _full_spec((1, D))
    in_specs = [row, row, row,
                pl.BlockSpec((None, None, D, FSP), lambda i, f: (f, sg, 0, 0)),
                pl.BlockSpec((None, None, D, FSP), lambda i, f: (f, su, 0, 0)),
                pl.BlockSpec((None, None, FSP, D), lambda i, f: (f, sd, 0, 0)),
                vec]
    args = [x, z, dout, p384, p384, pd, lg]
    out_specs = [row, hid, hid, hid, row, vec, vec]
    hshape = jax.ShapeDtypeStruct((T, NDEV * FSP), bf16)
    out_shape = [jax.ShapeDtypeStruct((T, D), f32), hshape, hshape, hshape, jax.ShapeDtypeStruct((T, D), bf16),
                 jax.ShapeDtypeStruct((1, D), f32), jax.ShapeDtypeStruct((1, D), f32)]
    if has_ple:
        p, pr, wpp = ple
        in_specs += [pl.BlockSpec((TM, PLE), lambda i, f: (i, 0)),
                     pl.BlockSpec((NDEV, None, 128, D), lambda i, f: (0, 3 * layer + 2, 0, 0)),
                     _full_spec((PLE, D))]
        args += [p, pr, wpp]
        out_specs += [row, row]
        out_shape += [jax.ShapeDtypeStruct((T, D), bf16)] * 2
    return pl.pallas_call(
        body, grid=(T // TM, NDEV), in_specs=in_specs, out_specs=out_specs, out_shape=out_shape,
        scratch_shapes=[pltpu.VMEM((TM, D), f32), pltpu.VMEM((TM, D), bf16)],
        compiler_params=_cparams(("arbitrary", "arbitrary")), name=f"ffn{which + 1}_bwd")(*args)


def _matmul_tn(a, b, nb, name):
    T, M = a.shape
    N = b.shape[1]
    tk = 512
    nk = T // tk

    def body(a_ref, b_ref, o_ref, acc):
        k = pl.program_id(1)

        @pl.when(k == 0)
        def _():
            acc[...] = jnp.zeros_like(acc)

        acc[...] += _dot_tn(a_ref[...], b_ref[...])

        @pl.when(k == nk - 1)
        def _():
            o_ref[...] = acc[...].astype(bf16)

    return pl.pallas_call(
        body, grid=(N // nb, nk),
        in_specs=[pl.BlockSpec((tk, M), lambda n, k: (k, 0)), pl.BlockSpec((tk, nb), lambda n, k: (k, n))],
        out_specs=pl.BlockSpec((None, M, nb), lambda n, k: (n, 0, 0)),
        out_shape=jax.ShapeDtypeStruct((N // nb, M, nb), bf16),
        scratch_shapes=[pltpu.VMEM((M, nb), f32)],
        compiler_params=_cparams(("arbitrary", "arbitrary")), name=name)(a, b)


def _proj_in(x, pr, pinl, ping, layer):
    T = x.shape[0]

    def body(x_ref, wr_ref, wl_ref, wg_ref, hr_ref, hl_ref, hg_ref):
        xb = x_ref[...].astype(bf16)
        hr_ref[...] = _dot(xb, wr_ref[...].reshape(D, D))
        hl_ref[...] = _dot(xb, wl_ref[...].reshape(D, 2 * LRU_W))
        hg_ref[...] = _dot(xb, wg_ref[...].reshape(D, GDN_IN))

    return pl.pallas_call(
        body, grid=(T // TM,),
        in_specs=[pl.BlockSpec((TM, D), lambda i: (i, 0)),
                  pl.BlockSpec((NDEV, None, 128, D), lambda i: (0, 3 * layer, 0, 0)),
                  pl.BlockSpec((NDEV, None, 128, 2 * LRU_W), lambda i: (0, layer, 0, 0)),
                  pl.BlockSpec((NDEV, None, 128, GDN_IN), lambda i: (0, layer, 0, 0))],
        out_specs=[pl.BlockSpec((TM, D), lambda i: (i, 0)), pl.BlockSpec((TM, 2 * LRU_W), lambda i: (i, 0)),
                   pl.BlockSpec((TM, GDN_IN), lambda i: (i, 0))],
        out_shape=[jax.ShapeDtypeStruct((T, D), f32), jax.ShapeDtypeStruct((T, 2 * LRU_W), f32),
                   jax.ShapeDtypeStruct((T, GDN_IN), f32)],
        compiler_params=_cparams(("arbitrary",)), name="proj_in")(x, pr, pinl, ping)


def _proj_in_bwd(base, dhr, dhl, dhg, pr, pinl, ping, layer):
    T = base.shape[0]

    def body(b_ref, dr_ref, dl_ref, dg_ref, wr_ref, wl_ref, wg_ref, o_ref):
        o_ref[...] = (b_ref[...] + _dot_nt(dr_ref[...], wr_ref[...].reshape(D, D))
                      + _dot_nt(dl_ref[...], wl_ref[...].reshape(D, 2 * LRU_W))
                      + _dot_nt(dg_ref[...], wg_ref[...].reshape(D, GDN_IN)))

    return pl.pallas_call(
        body, grid=(T // TM,),
        in_specs=[pl.BlockSpec((TM, D), lambda i: (i, 0)), pl.BlockSpec((TM, D), lambda i: (i, 0)),
                  pl.BlockSpec((TM, 2 * LRU_W), lambda i: (i, 0)), pl.BlockSpec((TM, GDN_IN), lambda i: (i, 0)),
                  pl.BlockSpec((NDEV, None, 128, D), lambda i: (0, 3 * layer, 0, 0)),
                  pl.BlockSpec((NDEV, None, 128, 2 * LRU_W), lambda i: (0, layer, 0, 0)),
                  pl.BlockSpec((NDEV, None, 128, GDN_IN), lambda i: (0, layer, 0, 0))],
        out_specs=pl.BlockSpec((TM, D), lambda i: (i, 0)),
        out_shape=jax.ShapeDtypeStruct((T, D), f32),
        compiler_params=_cparams(("arbitrary",)), name="proj_in_bwd")(base, dhr, dhl, dhg, pr, pinl, ping)


def _mix_out(x1, o_r, o_l, o_g, pr, lg, lb, layer):
    T = x1.shape[0]

    def body(x_ref, r_ref, l_ref, g_ref, w_ref, lg_ref, lb_ref, z_ref, o_ref):
        w = w_ref[...].reshape(D, D)
        z = (ALPHA * x_ref[...] + _dot(r_ref[...], w[0:RET_W]) + _dot(l_ref[...], w[RET_W:RET_W + LRU_W])
             + _dot(g_ref[...], w[RET_W + LRU_W:D]))
        z_ref[...] = z
        o_ref[...] = _ln_stats(z)[0] * lg_ref[...] + lb_ref[...]

    row = pl.BlockSpec((TM, D), lambda i: (i, 0))
    return pl.pallas_call(
        body, grid=(T // TM,),
        in_specs=[row, pl.BlockSpec((TM, RET_W), lambda i: (i, 0)), pl.BlockSpec((TM, LRU_W), lambda i: (i, 0)),
                  pl.BlockSpec((TM, GDN_W), lambda i: (i, 0)),
                  pl.BlockSpec((NDEV, None, 128, D), lambda i: (0, 3 * layer + 1, 0, 0)),
                  _full_spec((1, D)), _full_spec((1, D))],
        out_specs=[row, row], out_shape=[jax.ShapeDtypeStruct((T, D), f32)] * 2,
        compiler_params=_cparams(("arbitrary",)), name="mix_out")(x1, o_r, o_l, o_g, pr, lg, lb)


def _mix_out_bwd(z, dout, pr, lg, layer):
    T = z.shape[0]

    def body(z_ref, do_ref, w_ref, lg_ref, dxb_ref, dzb_ref, dr_ref, dl_ref, dg_ref, dlg_ref, dlb_ref):
        @pl.when(pl.program_id(0) == 0)
        def _():
            dlg_ref[...] = jnp.zeros_like(dlg_ref)
            dlb_ref[...] = jnp.zeros_like(dlb_ref)

        dz, dlg, dlb = _ln_bwd(z_ref[...], lg_ref[...], do_ref[...])
        dlg_ref[...] += dlg
        dlb_ref[...] += dlb
        dxb_ref[...] = ALPHA * dz
        dzb = dz.astype(bf16)
        dzb_ref[...] = dzb
        w = w_ref[...].reshape(D, D)
        dr_ref[...] = _dot_nt(dzb, w[0:RET_W])
        dl_ref[...] = _dot_nt(dzb, w[RET_W:RET_W + LRU_W])
        dg_ref[...] = _dot_nt(dzb, w[RET_W + LRU_W:D])

    row = pl.BlockSpec((TM, D), lambda i: (i, 0))
    vec = _full_spec((1, D))
    return pl.pallas_call(
        body, grid=(T // TM,),
        in_specs=[row, row, pl.BlockSpec((NDEV, None, 128, D), lambda i: (0, 3 * layer + 1, 0, 0)), vec],
        out_specs=[row, row, pl.BlockSpec((TM, RET_W), lambda i: (i, 0)), pl.BlockSpec((TM, LRU_W), lambda i: (i, 0)),
                   pl.BlockSpec((TM, GDN_W), lambda i: (i, 0)), vec, vec],
        out_shape=[jax.ShapeDtypeStruct((T, D), f32), jax.ShapeDtypeStruct((T, D), bf16),
                   jax.ShapeDtypeStruct((T, RET_W), f32), jax.ShapeDtypeStruct((T, LRU_W), f32),
                   jax.ShapeDtypeStruct((T, GDN_W), f32), jax.ShapeDtypeStruct((1, D), f32),
                   jax.ShapeDtypeStruct((1, D), f32)],
        compiler_params=_cparams(("arbitrary",)), name="mix_out_bwd")(z, dout, pr, lg)


def _loss_grad(y, target):
    T = y.shape[0]

    def body(y_ref, t_ref, dy_ref, l_ref):
        @pl.when(pl.program_id(0) == 0)
        def _():
            l_ref[...] = jnp.zeros_like(l_ref)

        e = y_ref[...] - t_ref[...]
        dy_ref[...] = e * (1.0 / D)
        l_ref[...] += 0.5 * jnp.sum(jnp.sum(e * e, -1, keepdims=True) * (1.0 / D), 0, keepdims=True)

    row = pl.BlockSpec((TM, D), lambda i: (i, 0))
    return pl.pallas_call(
        body, grid=(T // TM,), in_specs=[row, row], out_specs=[row, _full_spec((1, 1))],
        out_shape=[jax.ShapeDtypeStruct((T, D), f32), jax.ShapeDtypeStruct((1, 1), f32)],
        compiler_params=_cparams(("arbitrary",)), name="loss_grad")(y, target)


def _split_heads(x, H):
    n = x.shape[0] // CH
    parts = [x[:, h * CH:(h + 1) * CH].reshape(n, CH, CH) for h in range(H)]
    return jnp.stack(parts, axis=1).reshape(n * H, CH, CH)


def _merge_heads(ref, x, H, col0=0):
    n = x.shape[0] // H
    x4 = x.reshape(n, H, CH, CH)
    for h in range(H):
        ref[:, col0 + h * CH:col0 + (h + 1) * CH] = x4[:, h].reshape(n * CH, CH)


def _conv_fwd(ext, x, tail, w, R):
    ext[0:8, :] = tail
    ext[8:R + 8, :] = x
    y = w[3:4, :] * x
    for k in range(3):
        y = y + w[k:k + 1, :] * ext[5 + k:5 + k + R, :]
    return y


def _conv_bwd(ext, ext2, dy, dy_next, w, R):
    ext2[0:R, :] = dy
    ext2[R:R + 8, :] = dy_next
    dx = w[3:4, :] * dy
    dws = []
    for k in range(3):
        dx = dx + w[k:k + 1, :] * ext2[3 - k:3 - k + R, :]
        dws.append(jnp.sum(dy * ext[5 + k:5 + k + R, :], 0, keepdims=True))
    dws.append(jnp.sum(dy * ext[8:8 + R, :], 0, keepdims=True))
    return dx, jnp.concatenate(dws, axis=0)


def _prev_tail_spec(R, W):
    return pl.BlockSpec((8, W), lambda i: (jnp.maximum(i * (R // 8) - 1, 0), 0))


def _prev_tail_spec_rev(R, W, nb):
    return pl.BlockSpec((8, W), lambda i: (jnp.maximum((nb - 1 - i) * (R // 8) - 1, 0), 0))


def _rope_tables(positions):
    T = positions.shape[0]

    def body(p_ref, c_ref, s_ref):
        lane = lax.broadcasted_iota(jnp.int32, (TM, RET_W), 1)
        fi = (lane % 32).astype(f32)
        inv = jnp.exp(fi * (-math.log(ROPE_THETA) / 32.0))
        ang = p_ref[...].astype(f32) * inv
        c_ref[...] = jnp.cos(ang)
        s_ref[...] = jnp.where(lane % CH < 32, -jnp.sin(ang), jnp.sin(ang))

    row = pl.BlockSpec((TM, RET_W), lambda i: (i, 0))
    return pl.pallas_call(
        body, grid=(T // TM,), in_specs=[pl.BlockSpec((TM, 1), lambda i: (i, 0))], out_specs=[row, row],
        out_shape=[jax.ShapeDtypeStruct((T, RET_W), f32)] * 2,
        compiler_params=_cparams(("arbitrary",)), name="rope_tables")(positions)


def _partner(x):
    lane = lax.broadcasted_iota(jnp.int32, x.shape, 1)
    return jnp.where(lane % CH < 32, pltpu.roll(x, RET_W - 32, 1), pltpu.roll(x, 32, 1))


def _ret_consts():
    ii = lax.broadcasted_iota(jnp.int32, (CH, CH), 0).astype(f32)
    jj = lax.broadcasted_iota(jnp.int32, (CH, CH), 1).astype(f32)
    intra, cross, tail, cd = [], [], [], []
    for h in range(RET_H):
        lg = math.log1p(-(2.0 ** (-5.0 - h)))
        intra.append(jnp.exp(jnp.abs(ii - jj) * lg))
        cross.append(jnp.exp((ii + 1.0) * lg))
        tail.append(jnp.exp((CH - 1.0 - ii) * lg))
        cd.append(jnp.full((CH, CH), math.exp(CH * lg), f32))
    return jnp.stack(intra), jnp.stack(cross), jnp.stack(tail), jnp.stack(cd)


def _ret_chunk(consts, q, k, v, st):
    intra, cross, tail, cd = consts
    s = _bmm('hid,hjd->hij', q, k) * intra
    o = _bmm('hij,hje->hie', s, v) + _bmm('hid,hde->hie', q * cross, st)
    st2 = st * cd + _bmm('hjd,hje->hde', k * tail, v)
    mu = jnp.mean(o, -1, keepdims=True)
    oc = o - mu
    on = oc * lax.rsqrt(jnp.mean(oc * oc, -1, keepdims=True) + 1e-5)
    return on, st2


def _ret_fwd(hr, cosw, sinw, gam):
    T = hr.shape[0]
    R = RB_RET
    nc = R // CH

    def body(h_ref, c_ref, s_ref, g_ref, o_ref, st_ref, st, wide):
        @pl.when(pl.program_id(0) == 0)
        def _():
            st[...] = jnp.zeros_like(st)

        consts = _ret_consts()
        cw, sw = c_ref[...], s_ref[...]
        q, k = h_ref[:, 0:RET_W], h_ref[:, RET_W:2 * RET_W]
        qh = _split_heads((q * cw + _partner(q) * sw) * 0.125, RET_H)
        kh = _split_heads(k * cw + _partner(k) * sw, RET_H)
        vh = _split_heads(h_ref[:, 2 * RET_W:3 * RET_W], RET_H)
        outs = []
        s_cur = st[...]
        for c in range(nc):
            sl = slice(c * RET_H, (c + 1) * RET_H)
            st_ref[c] = s_cur
            on, s_cur = _ret_chunk(consts, qh[sl], kh[sl], vh[sl], s_cur)
            outs.append(on)
        st[...] = s_cur
        _merge_heads(wide, jnp.concatenate(outs, axis=0), RET_H)
        o_ref[...] = wide[...] * g_ref[...] * _silu(h_ref[:, 3 * RET_W:4 * RET_W])

    blk = pl.BlockSpec((R, RET_W), lambda i: (i, 0))
    return pl.pallas_call(
        body, grid=(T // R,),
        in_specs=[pl.BlockSpec((R, D), lambda i: (i, 0)), blk, blk, _full_spec((1, RET_W))],
        out_specs=[blk, pl.BlockSpec((nc, RET_H, CH, CH), lambda i: (i, 0, 0, 0))],
        out_shape=[jax.ShapeDtypeStruct((T, RET_W), f32), jax.ShapeDtypeStruct((T // CH, RET_H, CH, CH), f32)],
        scratch_shapes=[pltpu.VMEM((RET_H, CH, CH), f32), pltpu.VMEM((R, RET_W), f32)],
        compiler_params=_cparams(("arbitrary",)), name="ret_fwd")(hr, cosw, sinw, gam)


def _ret_bwd(hr, cosw, sinw, gam, states, dout):
    T = hr.shape[0]
    R = RB_RET
    nc = R // CH
    nb = T // R

    def body(h_ref, c_ref, s_ref, g_ref, st_ref, do_ref, dh_ref, dgam_ref, dst, wide):
        @pl.when(pl.program_id(0) == 0)
        def _():
            dst[...] = jnp.zeros_like(dst)
            dgam_ref[...] = jnp.zeros_like(dgam_ref)

        consts = _ret_consts()
        cw, sw = c_ref[...], s_ref[...]
        q, k = h_ref[:, 0:RET_W], h_ref[:, RET_W:2 * RET_W]
        gr = h_ref[:, 3 * RET_W:4 * RET_W]
        qh = _split_heads((q * cw + _partner(q) * sw) * 0.125, RET_H)
        kh = _split_heads(k * cw + _partner(k) * sw, RET_H)
        vh = _split_heads(h_ref[:, 2 * RET_W:3 * RET_W], RET_H)
        do = do_ref[...]
        gam = g_ref[...]
        sg = _silu(gr)
        don = _split_heads(do * gam * sg, RET_H)
        ons, dqs, dks, dvs = [None] * nc, [None] * nc, [None] * nc, [None] * nc
        ds = dst[...]
        for c in reversed(range(nc)):
            sl = slice(c * RET_H, (c + 1) * RET_H)
            (on, _), vjp = jax.vjp(functools.partial(_ret_chunk, consts), qh[sl], kh[sl], vh[sl], st_ref[c])
            dqs[c], dks[c], dvs[c], ds = vjp((don[sl], ds))
            ons[c] = on
        dst[...] = ds
        _merge_heads(wide, jnp.concatenate(ons, axis=0), RET_H)
        onw = wide[...]
        dgam_ref[...] += jnp.sum(do * onw * sg, 0, keepdims=True)
        dh_ref[:, 3 * RET_W:4 * RET_W] = do * onw * gam * _dsilu(gr)
        _merge_heads(wide, jnp.concatenate(dqs, axis=0), RET_H)
        u = wide[...] * 0.125
        dh_ref[:, 0:RET_W] = u * cw + _partner(u * sw)
        _merge_heads(wide, jnp.concatenate(dks, axis=0), RET_H)
        u = wide[...]
        dh_ref[:, RET_W:2 * RET_W] = u * cw + _partner(u * sw)
        _merge_heads(dh_ref, jnp.concatenate(dvs, axis=0), RET_H, col0=2 * RET_W)

    blk = pl.BlockSpec((R, RET_W), lambda i: (nb - 1 - i, 0))
    return pl.pallas_call(
        body, grid=(nb,),
        in_specs=[pl.BlockSpec((R, D), lambda i: (nb - 1 - i, 0)), blk, blk, _full_spec((1, RET_W)),
                  pl.BlockSpec((nc, RET_H, CH, CH), lambda i: (nb - 1 - i, 0, 0, 0)), blk],
        out_specs=[pl.BlockSpec((R, D), lambda i: (nb - 1 - i, 0)), _full_spec((1, RET_W))],
        out_shape=[jax.ShapeDtypeStruct((T, D), f32), jax.ShapeDtypeStruct((1, RET_W), f32)],
        scratch_shapes=[pltpu.VMEM((RET_H, CH, CH), f32), pltpu.VMEM((R, RET_W), f32)],
        compiler_params=_cparams(("arbitrary",)), name="ret_bwd")(hr, cosw, sinw, gam, states, dout)


def _lru_ab(xc, wa, ba, wx, bx, lam):
    r = _sigmoid(_dot(xc, wa) + ba)
    i = _sigmoid(_dot(xc, wx) + bx)
    la = 8.0 * r * (-_softplus(-lam))
    a = jnp.exp(la)
    em = jnp.tanh(la) * (jnp.exp(2.0 * la) + 1.0)
    return a, jnp.sqrt(-em) * (i * xc)


def _lru_out(h, gate):
    return h * _gelu(gate)


def _scan_fwd(a, b):
    R = a.shape[0]
    row = lax.broadcasted_iota(jnp.int32, a.shape, 0)
    d = 1
    while d < R:
        m = row >= d
        b = jnp.where(m, a * pltpu.roll(b, d, 0) + b, b)
        a = jnp.where(m, a * pltpu.roll(a, d, 0), a)
        d *= 2
    return a, b


def _scan_bwd(a, b):
    R = a.shape[0]
    row = lax.broadcasted_iota(jnp.int32, a.shape, 0)
    d = 1
    while d < R:
        m = row < R - d
        b = jnp.where(m, a * pltpu.roll(b, R - d, 0) + b, b)
        a = jnp.where(m, a * pltpu.roll(a, R - d, 0), a)
        d *= 2
    return b


def _lru_fwd(hl, cw, cb, wa, ba, wx, bx, lam):
    T = hl.shape[0]
    R = RB_LRU
    W = LRU_W

    def body(h_ref, t_ref, cw_ref, cb_ref, wa_ref, ba_ref, wx_ref, bx_ref, lam_ref, o_ref, hs_ref, carry, ext):
        first = pl.program_id(0) == 0

        @pl.when(first)
        def _():
            carry[...] = jnp.zeros_like(carry)

        tail = jnp.where(first, 0.0, t_ref[:, 0:W])
        xc = _conv_fwd(ext, h_ref[:, 0:W], tail, cw_ref[...], R) + cb_ref[...]
        a, b = _lru_ab(xc, wa_ref[...], ba_ref[...], wx_ref[...], bx_ref[...], lam_ref[...])
        ap, hloc = _scan_fwd(a, b)
        h = hloc + ap * carry[0:1, :]
        carry[...] = jnp.broadcast_to(h[R - 1:R, :], carry.shape)
        hs_ref[...] = h
        o_ref[...] = _lru_out(h, h_ref[:, W:2 * W])

    vec = _full_spec((1, W))
    blk = pl.BlockSpec((R, W), lambda i: (i, 0))
    return pl.pallas_call(
        body, grid=(T // R,),
        in_specs=[pl.BlockSpec((R, 2 * W), lambda i: (i, 0)), _prev_tail_spec(R, 2 * W), _full_spec((4, W)), vec,
                  _full_spec((W, W)), vec, _full_spec((W, W)), vec, vec],
        out_specs=[blk, blk], out_shape=[jax.ShapeDtypeStruct((T, W), f32)] * 2,
        scratch_shapes=[pltpu.VMEM((8, W), f32), pltpu.VMEM((R + 8, W), f32)],
        compiler_params=_cparams(("arbitrary",)), name="lru_fwd")(hl, hl, cw, cb, wa, ba, wx, bx, lam)


def _lru_bwd(hl, hs, cw, cb, wa, ba, wx, bx, lam, dout):
    T = hl.shape[0]
    R = RB_LRU
    W = LRU_W
    nb = T // R

    def body(h_ref, t_ref, hs_ref, hst_ref, cw_ref, cb_ref, wa_ref, ba_ref, wx_ref, bx_ref, lam_ref, do_ref,
             dh_ref, dcw_ref, dcb_ref, dwa_ref, dba_ref, dwx_ref, dbx_ref, dlam_ref, carry_g, carry_dy, ext, ext2):
        i = pl.program_id(0)
        last_blk = i == 0
        first_blk = i == nb - 1

        @pl.when(last_blk)
        def _():
            carry_g[...] = jnp.zeros_like(carry_g)
            carry_dy[...] = jnp.zeros_like(carry_dy)
            for r in (dcw_ref, dcb_ref, dwa_ref, dba_ref, dwx_ref, dbx_ref, dlam_ref):
                r[...] = jnp.zeros_like(r)

        tail = jnp.where(first_blk, 0.0, t_ref[:, 0:W])
        xc = _conv_fwd(ext, h_ref[:, 0:W], tail, cw_ref[...], R) + cb_ref[...]
        (a, _), vjp_ab = jax.vjp(_lru_ab, xc, wa_ref[...], ba_ref[...], wx_ref[...], bx_ref[...], lam_ref[...])
        hs = hs_ref[...]
        _, vjp_out = jax.vjp(_lru_out, hs, h_ref[:, W:2 * W])
        dh, dgate = vjp_out(do_ref[...])
        row = lax.broadcasted_iota(jnp.int32, (R, W), 0)
        dh = jnp.where(row == R - 1, dh + carry_g[0:1, :], dh)
        a_up = jnp.where(row == R - 1, 0.0, pltpu.roll(a, R - 1, 0))
        g = _scan_bwd(a_up, dh)
        carry_g[...] = jnp.broadcast_to(a[0:1, :] * g[0:1, :], carry_g.shape)
        hprev0 = jnp.where(first_blk, 0.0, hst_ref[7:8, :])
        hprev = jnp.where(row == 0, hprev0, pltpu.roll(hs, 1, 0))
        dxc, dwa, dba, dwx, dbx, dlam = vjp_ab((g * hprev, g))
        dwa_ref[...] += dwa
        dba_ref[...] += dba
        dwx_ref[...] += dwx
        dbx_ref[...] += dbx
        dlam_ref[...] += dlam
        dcb_ref[...] += jnp.sum(dxc, 0, keepdims=True)
        dx, dcw = _conv_bwd(ext, ext2, dxc, carry_dy[...], cw_ref[...], R)
        carry_dy[...] = dxc[0:8, :]
        dcw_ref[...] += dcw
        dh_ref[:, 0:W] = dx
        dh_ref[:, W:2 * W] = dgate

    vec = _full_spec((1, W))
    mat = _full_spec((W, W))
    blk = pl.BlockSpec((R, W), lambda i: (nb - 1 - i, 0))
    blk2 = pl.BlockSpec((R, 2 * W), lambda i: (nb - 1 - i, 0))
    return pl.pallas_call(
        body, grid=(nb,),
        in_specs=[blk2, _prev_tail_spec_rev(R, 2 * W, nb), blk, _prev_tail_spec_rev(R, W, nb), _full_spec((4, W)), vec,
                  mat, vec, mat, vec, vec, blk],
        out_specs=[blk2, _full_spec((4, W)), vec, mat, vec, mat, vec, vec],
        out_shape=[jax.ShapeDtypeStruct((T, 2 * W), f32), jax.ShapeDtypeStruct((4, W), f32),
                   jax.ShapeDtypeStruct((1, W), f32), jax.ShapeDtypeStruct((W, W), f32),
                   jax.ShapeDtypeStruct((1, W), f32), jax.ShapeDtypeStruct((W, W), f32),
                   jax.ShapeDtypeStruct((1, W), f32), jax.ShapeDtypeStruct((1, W), f32)],
        scratch_shapes=[pltpu.VMEM((8, W), f32), pltpu.VMEM((8, W), f32), pltpu.VMEM((R + 8, W), f32),
                        pltpu.VMEM((R + 8, W), f32)],
        compiler_params=_cparams(("arbitrary",)), name="lru_bwd")(hl, hl, hs, hs, cw, cb, wa, ba, wx, bx, lam, dout)


def _gdn_local(qs, ks, vs, gb, bb):
    B = qs.shape[0]
    ii = lax.broadcasted_iota(jnp.int32, (B, CH, CH), 1)
    jj = lax.broadcasted_iota(jnp.int32, (B, CH, CH), 2)
    q = qs * lax.rsqrt(jnp.sum(qs * qs, -1, keepdims=True) + 1e-6)
    k = ks * lax.rsqrt(jnp.sum(ks * ks, -1, keepdims=True) + 1e-6)
    tri = jnp.where(ii >= jj, 1.0, 0.0).astype(f32)
    gc = _bmm('bij,bjk->bik', tri, gb, hi=True)
    gct = jnp.swapaxes(gc, 1, 2)
    decay = jnp.where(ii >= jj, jnp.exp(jnp.minimum(gc - gct, 0.0)), 0.0)
    kk = _bmm('bid,bjd->bij', k, k)
    m = -jnp.where(ii > jj, bb * kk * decay, 0.0)
    eye = jnp.where(ii == jj, 1.0, 0.0).astype(f32)
    inv = eye + m
    mp = m
    for _ in range(5):
        mp = _bmm('bij,bjk->bik', mp, mp, hi=True)
        inv = inv + _bmm('bij,bjk->bik', inv, mp, hi=True)
    egc = jnp.exp(gc)
    u = _bmm('bij,bje->bie', inv, vs * bb, hi=True)
    w = _bmm('bij,bje->bie', inv, k * (bb * egc), hi=True)
    qk = _bmm('bid,bjd->bij', q, k) * (0.125 * decay)
    glast = gc[:, CH - 1:CH, :]
    return u, w, qk, q * (0.125 * egc), k * jnp.exp(glast - gc), jnp.exp(jnp.broadcast_to(glast, gc.shape))


def _gdn_step(st, u, w, qk, qd, kt, egl, z, gn):
    vnew = u - _bmm('hcd,hde->hce', w, st)
    o = _bmm('hcd,hde->hce', qd, st) + _bmm('hij,hje->hie', qk, vnew)
    st2 = st * egl + _bmm('hcd,hce->hde', kt, vnew)
    out = o * lax.rsqrt(jnp.mean(o * o, -1, keepdims=True) + 1e-6) * gn * _silu(z)
    return out, st2


def _gdn_scalars(ab, alog, dtb):
    sp = _softplus(ab + dtb)
    return -jnp.exp(alog) * sp, _sigmoid(ab)


def _bcast_heads(blk, lane0, H):
    R = blk.shape[0]
    n = R // CH
    parts = [jnp.broadcast_to(blk[:, lane0 + h:lane0 + h + 1], (R, CH)).reshape(n, CH, CH) for h in range(H)]
    return jnp.stack(parts, axis=1).reshape(n * H, CH, CH)


def _unbcast_heads(x, lane0, H):
    n = x.shape[0] // H
    R = n * CH
    s = jnp.sum(x, axis=2, keepdims=True).reshape(n, H, CH, 1)
    lane = lax.broadcasted_iota(jnp.int32, (R, 128), 1)
    acc = jnp.zeros((R, 128), f32)
    for h in range(H):
        acc = acc + jnp.where(lane == lane0 + h, jnp.broadcast_to(s[:, h].reshape(R, 1), (R, 128)), 0.0)
    return acc


def _gdn_fwd(hg, cw, alog, dtb, gn):
    T = hg.shape[0]
    R = RB_GDN
    nc = R // CH
    W3 = 3 * GDN_W
    H = GDN_H

    def body(h_ref, t_ref, cw_ref, al_ref, dt_ref, gn_ref, o_ref, st_ref, st, ext):
        first = pl.program_id(0) == 0

        @pl.when(first)
        def _():
            st[...] = jnp.zeros_like(st)

        tail = jnp.where(first, 0.0, t_ref[:, 0:W3])
        y = _silu(_conv_fwd(ext, h_ref[:, 0:W3], tail, cw_ref[...], R))
        qs, ks, vs = (_split_heads(y[:, j * GDN_W:(j + 1) * GDN_W], H) for j in range(3))
        zh = _split_heads(h_ref[:, W3:W3 + GDN_W], H)
        g, beta = _gdn_scalars(h_ref[:, W3 + GDN_W:GDN_IN], al_ref[...], dt_ref[...])
        loc = _gdn_local(qs, ks, vs, _bcast_heads(g, 0, H), _bcast_heads(beta, H, H))
        gnv = gn_ref[...]
        outs = []
        s_cur = st[...]
        for c in range(nc):
            sl = slice(c * H, (c + 1) * H)
            st_ref[c] = s_cur
            out, s_cur = _gdn_step(s_cur, *(t[sl] for t in loc), zh[sl], gnv)
            outs.append(out)
        st[...] = s_cur
        _merge_heads(o_ref, jnp.concatenate(outs, axis=0), H)

    return pl.pallas_call(
        body, grid=(T // R,),
        in_specs=[pl.BlockSpec((R, GDN_IN), lambda i: (i, 0)), _prev_tail_spec(R, GDN_IN), _full_spec((4, W3)),
                  _full_spec((1, 128)), _full_spec((1, 128)), _full_spec((1, CH))],
        out_specs=[pl.BlockSpec((R, GDN_W), lambda i: (i, 0)), pl.BlockSpec((nc, H, CH, CH), lambda i: (i, 0, 0, 0))],
        out_shape=[jax.ShapeDtypeStruct((T, GDN_W), f32), jax.ShapeDtypeStruct((T // CH, H, CH, CH), f32)],
        scratch_shapes=[pltpu.VMEM((H, CH, CH), f32), pltpu.VMEM((R + 8, W3), f32)],
        compiler_params=_cparams(("arbitrary",)), name="gdn_fwd")(hg, hg, cw, alog, dtb, gn)


def _gdn_bwd(hg, cw, alog, dtb, gn, states, dout):
    T = hg.shape[0]
    R = RB_GDN
    nc = R // CH
    nb = T // R
    W3 = 3 * GDN_W
    H = GDN_H

    def body(h_ref, t_ref, cw_ref, al_ref, dt_ref, gn_ref, st_ref, do_ref,
             dh_ref, dcw_ref, dal_ref, ddt_ref, dgn_ref, dst, carry_dy, ext, ext2, wide):
        i = pl.program_id(0)
        first_blk = i == nb - 1

        @pl.when(i == 0)
        def _():
            dst[...] = jnp.zeros_like(dst)
            carry_dy[...] = jnp.zeros_like(carry_dy)
            for r in (dcw_ref, dal_ref, ddt_ref, dgn_ref):
                r[...] = jnp.zeros_like(r)

        tail = jnp.where(first_blk, 0.0, t_ref[:, 0:W3])
        ypre = _conv_fwd(ext, h_ref[:, 0:W3], tail, cw_ref[...], R)
        y = _silu(ypre)
        qs, ks, vs = (_split_heads(y[:, j * GDN_W:(j + 1) * GDN_W], H) for j in range(3))
        zh = _split_heads(h_ref[:, W3:W3 + GDN_W], H)
        ab = h_ref[:, W3 + GDN_W:GDN_IN]
        alog, dtb = al_ref[...], dt_ref[...]
        g, beta = _gdn_scalars(ab, alog, dtb)
        loc, vjp_loc = jax.vjp(_gdn_local, qs, ks, vs, _bcast_heads(g, 0, H), _bcast_heads(beta, H, H))
        doh = _split_heads(do_ref[...], H)
        gnv = gn_ref[...]
        dloc = [[None] * nc for _ in range(6)]
        dzs = [None] * nc
        ds = dst[...]
        dgn = jnp.zeros((1, CH), f32)
        for c in reversed(range(nc)):
            sl = slice(c * H, (c + 1) * H)
            _, vjp = jax.vjp(_gdn_step, st_ref[c], *(t[sl] for t in loc), zh[sl], gnv)
            grads = vjp((doh[sl], ds))
            ds = grads[0]
            for j in range(6):
                dloc[j][c] = grads[1 + j]
            dzs[c] = grads[7]
            dgn = dgn + grads[8]
        dst[...] = ds
        dgn_ref[...] += dgn
        dqs, dks, dvs, dgb, dbb = vjp_loc(tuple(jnp.concatenate(d, axis=0) for d in dloc))
        lane = lax.broadcasted_iota(jnp.int32, (R, 128), 1)
        dg = _unbcast_heads(dgb, 0, H)
        dbeta = _unbcast_heads(dbb, H, H)
        da = dg * (-jnp.exp(alog)) * _sigmoid(ab + dtb)
        dh_ref[:, W3 + GDN_W:GDN_IN] = jnp.where(lane < H, da, dbeta * beta * (1.0 - beta))
        ddt_ref[...] += jnp.sum(jnp.where(lane < H, da, 0.0), 0, keepdims=True)
        dal_ref[...] += jnp.sum(jnp.where(lane < H, dg * g, 0.0), 0, keepdims=True)
        _merge_heads(dh_ref, jnp.concatenate(dzs, axis=0), H, col0=W3)
        for j, dpart in enumerate((dqs, dks, dvs)):
            _merge_heads(wide, dpart, H, col0=j * GDN_W)
        dy = wide[...] * _dsilu(ypre)
        dx, dcw = _conv_bwd(ext, ext2, dy, carry_dy[...], cw_ref[...], R)
        carry_dy[...] = dy[0:8, :]
        dcw_ref[...] += dcw
        dh_ref[:, 0:W3] = dx

    blk = pl.BlockSpec((R, GDN_IN), lambda i: (nb - 1 - i, 0))
    return pl.pallas_call(
        body, grid=(nb,),
        in_specs=[blk, _prev_tail_spec_rev(R, GDN_IN, nb), _full_spec((4, W3)), _full_spec((1, 128)),
                  _full_spec((1, 128)), _full_spec((1, CH)),
                  pl.BlockSpec((nc, H, CH, CH), lambda i: (nb - 1 - i, 0, 0, 0)),
                  pl.BlockSpec((R, GDN_W), lambda i: (nb - 1 - i, 0))],
        out_specs=[blk, _full_spec((4, W3)), _full_spec((1, 128)), _full_spec((1, 128)), _full_spec((1, CH))],
        out_shape=[jax.ShapeDtypeStruct((T, GDN_IN), f32), jax.ShapeDtypeStruct((4, W3), f32),
                   jax.ShapeDtypeStruct((1, 128), f32), jax.ShapeDtypeStruct((1, 128), f32),
                   jax.ShapeDtypeStruct((1, CH), f32)],
        scratch_shapes=[pltpu.VMEM((H, CH, CH), f32), pltpu.VMEM((8, W3), f32), pltpu.VMEM((R + 8, W3), f32),
                        pltpu.VMEM((R + 8, W3), f32), pltpu.VMEM((R, W3), f32)],
        compiler_params=_cparams(("arbitrary",)), name="gdn_bwd")(hg, hg, cw, alog, dtb, gn, states, dout)


def _block_diag(w):
    out = jnp.zeros((LRU_W, LRU_W), w.dtype)
    for g in range(w.shape[0]):
        out = lax.dynamic_update_slice(out, w[g], (g * CH, g * CH))
    return out


def _block_diag_t(w):
    return jnp.stack([w[g * CH:(g + 1) * CH, g * CH:(g + 1) * CH] for g in range(LRU_W // CH)])


def _pad_lanes(v, n=128):
    return jnp.pad(v, (0, n - v.shape[0]))[None, :]


def _local_step(x, p, positions, target, G, sm):
    p384, pd, pr, pinl, ping, wpp = G['p384'], G['pd'], G['pr'], G['pinl'], G['ping'], G['wpp']
    cosw, sinw = _rope_tables(positions)
    saved = []
    h = x
    for l in range(DEPTH):
        v = lambda n: sm[n][l][None, :]
        z1, x1 = _ffn_fwd(h, p384, pd, v('ln_ffn1_g'), v('ln_ffn1_b'), l, 0)
        hr, hl, hg = _proj_in(x1, pr, pinl, ping, l)
        o_r, rst = _ret_fwd(hr, cosw, sinw, v('ret_norm_g'))
        lru_args = (sm['lru_conv_w'][l], v('lru_conv_b'), _block_diag(sm['lru_w_a'][l]), v('lru_b_a'),
                    _block_diag(sm['lru_w_x'][l]), v('lru_b_x'), v('lru_lambda'))
        o_l, hs = _lru_fwd(hl, *lru_args)
        gdn_args = (sm['gdn_conv_w'][l], _pad_lanes(sm['gdn_a_log'][l]), _pad_lanes(sm['gdn_dt_bias'][l]),
                    v('gdn_norm_g'))
        o_g, gst = _gdn_fwd(hg, *gdn_args)
        z2, x2 = _mix_out(x1, o_r, o_l, o_g, pr, v('ln_mix_g'), v('ln_mix_b'), l)
        z3, x3 = _ffn_fwd(x2, p384, pd, v('ln_ffn2_g'), v('ln_ffn2_b'), l, 1, ple=(p[l], pr, wpp[l]))
        saved.append((h, z1, x1, hr, hl, hg, o_r, rst, o_l, hs, lru_args, o_g, gst, gdn_args, z2, x2, z3))
        h = x3
    d, loss = _loss_grad(h, target)

    big = {k: [None] * DEPTH for k in ('p384', 'pd', 'pr', 'pinl', 'ping', 'ppp')}
    small = {n: [None] * DEPTH for n in SMALL}
    for l in reversed(range(DEPTH)):
        x0, z1, x1, hr, hl, hg, o_r, rst, o_l, hs, lru_args, o_g, gst, gdn_args, z2, x2, z3 = saved[l]
        v = lambda n: sm[n][l][None, :]
        d2, dg2, du2, a2, dy2, small['ln_ffn2_g'][l], small['ln_ffn2_b'][l], dgp, dpj = _ffn_bwd(
            x2, z3, d, p384, pd, v('ln_ffn2_g'), l, 1, ple=(p[l], pr, wpp[l]))
        dxb, dzb, do_r, do_l, do_g, small['ln_mix_g'][l], small['ln_mix_b'][l] = _mix_out_bwd(
            z2, d2, pr, v('ln_mix_g'), l)
        dhr, small['ret_norm_g'][l] = _ret_bwd(hr, cosw, sinw, v('ret_norm_g'), rst, do_r)
        (dhl, small['lru_conv_w'][l], small['lru_conv_b'][l], dwa, small['lru_b_a'][l], dwx, small['lru_b_x'][l],
         small['lru_lambda'][l]) = _lru_bwd(hl, hs, *lru_args, do_l)
        small['lru_w_a'][l], small['lru_w_x'][l] = _block_diag_t(dwa), _block_diag_t(dwx)
        dhg, small['gdn_conv_w'][l], dal, ddt, small['gdn_norm_g'][l] = _gdn_bwd(hg, *gdn_args, gst, do_g)
        small['gdn_a_log'][l], small['gdn_dt_bias'][l] = dal[:, 0:GDN_H], ddt[:, 0:GDN_H]
        d1 = _proj_in_bwd(dxb, dhr, dhl, dhg, pr, pinl, ping, l)
        d, dg1, du1, a1, dy1, small['ln_ffn1_g'][l], small['ln_ffn1_b'][l] = _ffn_bwd(
            x0, z1, d1, p384, pd, v('ln_ffn1_g'), l, 0)
        rows = lambda m: m.reshape(NDEV, m.shape[1] // NDEV, m.shape[2])
        big['p384'][l] = jnp.stack([_matmul_tn(x0, dg1, FSP, "dw_gate"), _matmul_tn(x0, du1, FSP, "dw_up"),
                                    _matmul_tn(x2, dg2, FSP, "dw_gate"), _matmul_tn(x2, du2, FSP, "dw_up")], axis=1)
        big['pd'][l] = jnp.stack([rows(_matmul_tn(a1, dy1, D, "dw_down")),
                                  rows(_matmul_tn(a2, dy2, D, "dw_down"))], axis=1)
        dwo = jnp.concatenate([_matmul_tn(o_r, dzb, D, "dw_out_r"), _matmul_tn(o_l, dzb, D, "dw_out_l"),
                               _matmul_tn(o_g, dzb, D, "dw_out_g")], axis=1)
        big['pr'][l] = jnp.stack([rows(_matmul_tn(x1, dhr, D, "dw_in_r")), rows(dwo),
                                  rows(_matmul_tn(x2, dgp, D, "dw_ple_gate"))], axis=1)
        big['pinl'][l] = rows(_matmul_tn(x1, dhl, 2 * LRU_W, "dw_in_l"))
        big['ping'][l] = rows(_matmul_tn(x1, dhg, GDN_IN, "dw_in_g"))
        big['ppp'][l] = _matmul_tn(p[l], dpj, 128, "dw_ple_proj")
    big = {'p384': jnp.concatenate(big['p384'], axis=1), 'pd': jnp.concatenate(big['pd'], axis=1),
           'pr': jnp.concatenate(big['pr'], axis=1), 'pinl': jnp.stack(big['pinl'], axis=1),
           'ping': jnp.stack(big['ping'], axis=1), 'ppp': jnp.stack(big['ppp'], axis=1)}
    small = {n: jnp.stack([g.reshape(sm[n].shape[1:]) for g in gs]) for n, gs in small.items()}
    return loss, d, big, small


def _pack_big(ws, dtype=bf16):
    padc = lambda a, n: jnp.pad(a, ((0, 0), (0, 0), (0, n - a.shape[2])))
    padr = lambda a, n: jnp.pad(a, ((0, 0), (0, n - a.shape[1]), (0, 0)))
    per_layer = lambda arrs: jnp.stack(arrs, axis=1).reshape((-1,) + arrs[0].shape[1:])
    w_in = ws['w_in']
    out = {
        'p384': per_layer([padc(ws[n], FSP) for n in ('ffn1_w_gate', 'ffn1_w_up', 'ffn2_w_gate', 'ffn2_w_up')]),
        'pd': per_layer([padr(ws[n], FSP) for n in ('ffn1_w_down', 'ffn2_w_down')]),
        'pr': per_layer([w_in[:, :, 0:D], ws['w_out'], ws['ple_w_gate']]),
        'pinl': w_in[:, :, D:D + 2 * LRU_W],
        'ping': padc(w_in[:, :, D + 2 * LRU_W:D_IN], GDN_IN),
        'ppp': ws['ple_w_proj'],
    }
    return {k: a.astype(dtype) for k, a in out.items()}


def _exchange(arrays, scatter, name):
    n = len(arrays)

    def body(*refs):
        ins, outs = refs[:n], refs[n:2 * n]
        send_sems, recv_sems, local_sems = refs[2 * n:]
        x, y, c = lax.axis_index("x"), lax.axis_index("y"), lax.axis_index("c")
        me = 4 * x + 2 * y + c
        copies = []
        for i in range(n):
            src = ins[i].at[me] if scatter[i] else ins[i]
            cp = pltpu.make_async_copy(src, outs[i].at[me], local_sems.at[i])
            cp.start()
            copies.append(cp)
        sends = []
        for j in range(1, NDEV):
            peer = (me + j) % NDEV
            pid = (peer // 4, (peer // 2) % 2, peer % 2)
            for i in range(n):
                src = ins[i].at[peer] if scatter[i] else ins[i]
                cp = pltpu.make_async_remote_copy(
                    src_ref=src, dst_ref=outs[i].at[me], send_sem=send_sems.at[i, j - 1],
                    recv_sem=recv_sems.at[i, j - 1], device_id=pid, device_id_type=pl.DeviceIdType.MESH)
                cp.start()
                sends.append(cp)
        for j in range(1, NDEV):
            source = (me + NDEV - j) % NDEV
            sid = (source // 4, (source // 2) % 2, source % 2)
            for i in range(n):
                src = ins[i].at[me] if scatter[i] else ins[i]
                pltpu.make_async_remote_copy(
                    src_ref=src, dst_ref=outs[i].at[source], send_sem=send_sems.at[i, j - 1],
                    recv_sem=recv_sems.at[i, j - 1], device_id=sid, device_id_type=pl.DeviceIdType.MESH).wait_recv()
        for cp in sends:
            cp.wait_send()
        for cp in copies:
            cp.wait()

    hbm = pl.BlockSpec(memory_space=pltpu.HBM)
    out_shape = [jax.ShapeDtypeStruct(a.shape if s else (NDEV,) + a.shape, a.dtype) for a, s in zip(arrays, scatter)]
    return pl.pallas_call(
        body, in_specs=[hbm] * n, out_specs=[hbm] * n, out_shape=out_shape,
        scratch_shapes=[pltpu.SemaphoreType.DMA((n, NDEV - 1)), pltpu.SemaphoreType.DMA((n, NDEV - 1)),
                        pltpu.SemaphoreType.DMA((n,))],
        compiler_params=pltpu.CompilerParams(has_side_effects=True), name=name)(*arrays)


def _adam_math(w, g, m, v):
    m2 = ADAM_B1 * m + (1.0 - ADAM_B1) * g
    v2 = ADAM_B2 * v + (1.0 - ADAM_B2) * (g * g)
    m_hat = m2 / (1.0 - ADAM_B1 ** ADAM_STEP)
    v_hat = v2 / (1.0 - ADAM_B2 ** ADAM_STEP)
    return -ADAM_LR * (m_hat / (jnp.sqrt(v_hat) + ADAM_EPS) + ADAM_WD * w), m2, v2


def _adam_big(parts, w, m, v, name):
    L, rows, cols = w.shape

    def body(*refs):
        prefs = refs[:len(parts)]
        w_ref, m_ref, v_ref, g_ref, d_ref, m2_ref, v2_ref = refs[len(parts):]
        c0 = 0
        for pref in prefs:
            acc = pref[0].astype(f32)
            for s in range(1, NDEV):
                acc = acc + pref[s].astype(f32)
            width = min(acc.shape[1], cols - c0)
            g_ref[:, c0:c0 + width] = acc[0:rows, 0:width]
            c0 += width
        d, m2, v2 = _adam_math(w_ref[...], g_ref[...], m_ref[...], v_ref[...])
        d_ref[...] = d
        m2_ref[...] = m2
        v2_ref[...] = v2

    wspec = pl.BlockSpec((None, rows, cols), lambda l: (l, 0, 0))
    in_specs = [pl.BlockSpec((NDEV, None) + a.shape[2:], functools.partial(lambda l, per, first: (0, per * l + first, 0, 0), per=per, first=first))
                for a, per, first in parts]
    return pl.pallas_call(
        body, grid=(L,), in_specs=in_specs + [wspec] * 3, out_specs=[wspec] * 4,
        out_shape=[jax.ShapeDtypeStruct(w.shape, f32)] * 4,
        compiler_params=_cparams(("arbitrary",)), name=name)(*[a for a, _, _ in parts], w, m, v)


def _sum_sources(stacked):
    rows = stacked.shape[1]

    def body(s_ref, o_ref):
        acc = s_ref[0]
        for s in range(1, NDEV):
            acc = acc + s_ref[s]
        o_ref[...] = acc

    return pl.pallas_call(body, out_shape=jax.ShapeDtypeStruct((rows, 128), f32), name="sum_small_grads")(stacked)


def _adam_small(w, g, m, v):
    def body(w_ref, g_ref, m_ref, v_ref, d_ref, m2_ref, v2_ref):
        d, m2, v2 = _adam_math(w_ref[...], g_ref[...], m_ref[...], v_ref[...])
        d_ref[...] = d
        m2_ref[...] = m2
        v2_ref[...] = v2

    return pl.pallas_call(body, out_shape=[jax.ShapeDtypeStruct(w.shape, f32)] * 3, name="adam_small")(w, g, m, v)


def _pack_rows(arrs):
    flat = []
    for a in arrs:
        a = a.reshape(-1)
        flat.append(jnp.pad(a, (0, (-a.shape[0]) % 1024)))
    return jnp.concatenate(flat).reshape(-1, 128)


def _unpack_rows(packed, shapes):
    out, off = [], 0
    flat = packed.reshape(-1)
    for s in shapes:
        n = math.prod(s)
        out.append(flat[off:off + n].reshape(s))
        off += n + (-n) % 1024
    return out


def _gather_conv(gathered, shape):
    L, K, c = shape
    return jnp.transpose(gathered, (1, 2, 0, 3)).reshape(L, K, NDEV * c)


def kernel(x, p, positions, ln_ffn1_g, ln_ffn1_b, ffn1_w_gate, ffn1_w_up, ffn1_w_down, w_in, ret_norm_g, lru_conv_w, lru_conv_b, lru_w_a, lru_b_a, lru_w_x, lru_b_x, lru_lambda, gdn_conv_w, gdn_a_log, gdn_dt_bias, gdn_norm_g, w_out, ln_mix_g, ln_mix_b, ffn2_w_gate, ffn2_w_up, ffn2_w_down, ple_w_gate, ple_w_proj, ln_ffn2_g, ln_ffn2_b, loss_target, m_ln_ffn1_g, m_ln_ffn1_b, m_ffn1_w_gate, m_ffn1_w_up, m_ffn1_w_down, m_w_in, m_ret_norm_g, m_lru_conv_w, m_lru_conv_b, m_lru_w_a, m_lru_b_a, m_lru_w_x, m_lru_b_x, m_lru_lambda, m_gdn_conv_w, m_gdn_a_log, m_gdn_dt_bias, m_gdn_norm_g, m_w_out, m_ln_mix_g, m_ln_mix_b, m_ffn2_w_gate, m_ffn2_w_up, m_ffn2_w_down, m_ple_w_gate, m_ple_w_proj, m_ln_ffn2_g, m_ln_ffn2_b, v_ln_ffn1_g, v_ln_ffn1_b, v_ffn1_w_gate, v_ffn1_w_up, v_ffn1_w_down, v_w_in, v_ret_norm_g, v_lru_conv_w, v_lru_conv_b, v_lru_w_a, v_lru_b_a, v_lru_w_x, v_lru_b_x, v_lru_lambda, v_gdn_conv_w, v_gdn_a_log, v_gdn_dt_bias, v_gdn_norm_g, v_w_out, v_ln_mix_g, v_ln_mix_b, v_ffn2_w_gate, v_ffn2_w_up, v_ffn2_w_down, v_ple_w_gate, v_ple_w_proj, v_ln_ffn2_g, v_ln_ffn2_b):
    args = locals()
    W = {n: args[n] for n in WEIGHTS}
    M = {n: args['m_' + n] for n in WEIGHTS}
    V = {n: args['v_' + n] for n in WEIGHTS}
    me = 4 * lax.axis_index("x") + 2 * lax.axis_index("y") + lax.axis_index("c")

    packed = _pack_big(W)
    conv_pack = _pack_rows([W[n] for n in CONV_SHARDED])
    keys = list(packed)
    gathered = _exchange([packed[k] for k in keys] + [conv_pack], [False] * (len(keys) + 1), "gather_weights")
    G = dict(zip(keys, gathered[:-1]))
    G['wpp'] = jnp.transpose(G.pop('ppp'), (1, 2, 0, 3)).reshape(DEPTH, PLE, D)
    conv_all = gathered[-1]
    sm = {n: W[n] for n in SMALL}
    conv_shards = [_unpack_rows(conv_all[s], [W[n].shape for n in CONV_SHARDED]) for s in range(NDEV)]
    for i, n in enumerate(CONV_SHARDED):
        sm[n] = _gather_conv(jnp.stack([cs[i] for cs in conv_shards]), W[n].shape)

    loss, grad_x, big, small = _local_step(x[0], p[:, 0], positions.reshape(-1, 1), loss_target[0], G, sm)
    loss = lax.psum(loss[0, 0], ("x", "y", "c"))

    small_pack = _pack_rows([small[n] for n in SMALL])
    bkeys = list(big)
    recv = _exchange([big[k] for k in bkeys] + [small_pack], [True] * len(bkeys) + [False], "scatter_grads")
    R = dict(zip(bkeys, recv[:-1]))
    small_sum = _unpack_rows(_sum_sources(recv[-1]), [small[n].shape for n in SMALL])
    grads, delta, new_m, new_v = {}, {}, {}, {}
    for n, g in zip(SMALL, small_sum):
        if n in CONV_SHARDED:
            c = W[n].shape[2]
            g = lax.dynamic_slice_in_dim(g, me * c, c, axis=2)
        grads[n] = g

    big_parts = {
        'ffn1_w_gate': [(R['p384'], 4, 0)], 'ffn1_w_up': [(R['p384'], 4, 1)],
        'ffn2_w_gate': [(R['p384'], 4, 2)], 'ffn2_w_up': [(R['p384'], 4, 3)],
        'ffn1_w_down': [(R['pd'], 2, 0)], 'ffn2_w_down': [(R['pd'], 2, 1)],
        'w_in': [(R['pr'], 3, 0), (R['pinl'], 1, 0), (R['ping'], 1, 0)],
        'w_out': [(R['pr'], 3, 1)], 'ple_w_gate': [(R['pr'], 3, 2)], 'ple_w_proj': [(R['ppp'], 1, 0)],
    }
    for n in BIG:
        grads[n], delta[n], new_m[n], new_v[n] = _adam_big(big_parts[n], W[n], M[n], V[n], "adam_" + n)
    shapes = [W[n].shape for n in SMALL]
    d_s, m_s, v_s = _adam_small(*[_pack_rows([src[n] for n in SMALL]) for src in (W, grads, M, V)])
    for n, dd, mm, vv in zip(SMALL, _unpack_rows(d_s, shapes), _unpack_rows(m_s, shapes), _unpack_rows(v_s, shapes)):
        delta[n], new_m[n], new_v[n] = dd, mm, vv

    return (loss, grad_x[None], *[grads[n] for n in WEIGHTS], *[delta[n] for n in WEIGHTS],
            *[new_m[n] for n in WEIGHTS], *[new_v[n] for n in WEIGHTS])
```

```python
import functools
import math

import jax
import jax.numpy as jnp
from jax import lax
from jax.experimental import pallas as pl
from jax.experimental.pallas import tpu as pltpu

f32 = jnp.float32
bf16 = jnp.bfloat16

NDEV = 8
DEPTH = 2
D = 1024
FS = 352
FSP = 384
FB = 2
NF = NDEV // FB
PLE = 256
CH = 64
RET_H, GDN_H = 4, 6
RET_W, LRU_W, GDN_W = 256, 384, 384
GDN_IN = 1664
GDN_IN_REAL = 1548
D_IN = 3340
ALPHA = 4.0 ** 0.25
LN_EPS = 1e-5
ROPE_THETA = 10000.0
TM = 512
RB_RET, RB_LRU, RB_GDN = 512, 512, 256
VMEM_LIMIT = 56 * 1024 * 1024
ADAM_LR, ADAM_B1, ADAM_B2, ADAM_EPS, ADAM_WD, ADAM_STEP = 0.001, 0.9, 0.999, 1e-08, 0.01, 10

WEIGHTS = ['ln_ffn1_g', 'ln_ffn1_b', 'ffn1_w_gate', 'ffn1_w_up', 'ffn1_w_down', 'w_in', 'ret_norm_g', 'lru_conv_w',
           'lru_conv_b', 'lru_w_a', 'lru_b_a', 'lru_w_x', 'lru_b_x', 'lru_lambda', 'gdn_conv_w', 'gdn_a_log',
           'gdn_dt_bias', 'gdn_norm_g', 'w_out', 'ln_mix_g', 'ln_mix_b', 'ffn2_w_gate', 'ffn2_w_up', 'ffn2_w_down',
           'ple_w_gate', 'ple_w_proj', 'ln_ffn2_g', 'ln_ffn2_b']
BIG = ['ffn1_w_gate', 'ffn1_w_up', 'ffn1_w_down', 'w_in', 'w_out', 'ffn2_w_gate', 'ffn2_w_up', 'ffn2_w_down',
       'ple_w_gate', 'ple_w_proj']
SMALL = [n for n in WEIGHTS if n not in BIG]
CONV_SHARDED = {'lru_conv_w': LRU_W, 'gdn_conv_w': 3 * GDN_W}


def _cparams(sem=None):
    return pltpu.CompilerParams(dimension_semantics=sem, vmem_limit_bytes=VMEM_LIMIT)


def _sigmoid(x):
    return 1.0 / (1.0 + jnp.exp(-x))


def _silu(x):
    return x * _sigmoid(x)


def _dsilu(x):
    s = _sigmoid(x)
    return s * (1.0 + x * (1.0 - s))


def _softplus(x):
    return jnp.maximum(x, 0.0) + jnp.log(1.0 + jnp.exp(-jnp.abs(x)))


def _gelu(x):
    return 0.5 * x * (1.0 + jnp.tanh(0.7978845608028654 * (x + 0.044715 * x * x * x)))


def _dot(a, b):
    return jnp.dot(a.astype(bf16), b.astype(bf16), preferred_element_type=f32)


def _dot_nt(a, b):
    return lax.dot_general(a.astype(bf16), b.astype(bf16), (((1,), (1,)), ((), ())), preferred_element_type=f32)


def _dot_tn(a, b):
    return lax.dot_general(a.astype(bf16), b.astype(bf16), (((0,), (0,)), ((), ())), preferred_element_type=f32)


def _bmm(eq, a, b):
    return jnp.einsum(eq, a.astype(bf16), b.astype(bf16), preferred_element_type=f32)


def _split3(a):
    a1 = a.astype(bf16)
    r = a - a1.astype(f32)
    a2 = r.astype(bf16)
    return a1, a2, (r - a2.astype(f32)).astype(bf16)


def _bmm3(eq, a, b):
    a1, a2, _ = _split3(a)
    b1, b2, _ = _split3(b)
    e = lambda x, y: jnp.einsum(eq, x, y, preferred_element_type=f32)
    return e(a1, b1) + (e(a1, b2) + e(a2, b1))


def _tri_ones(B, upper=False):
    ii = lax.broadcasted_iota(jnp.int32, (B, CH, CH), 1)
    jj = lax.broadcasted_iota(jnp.int32, (B, CH, CH), 2)
    return jnp.where((ii <= jj) if upper else (ii >= jj), 1.0, 0.0).astype(bf16)


def _cumsum_mm(t, x):
    x1, x2, x3 = _split3(x)
    e = lambda y: jnp.einsum('bij,bjk->bik', t, y, preferred_element_type=f32)
    return e(x1) + (e(x2) + e(x3))


@jax.custom_vjp
def _cumsum_rows(x):
    return _cumsum_mm(_tri_ones(x.shape[0]), x)


def _cumsum_rows_fwd(x):
    return _cumsum_rows(x), None


def _cumsum_rows_bwd(_, g):
    return (_cumsum_mm(_tri_ones(g.shape[0], upper=True), g),)


_cumsum_rows.defvjp(_cumsum_rows_fwd, _cumsum_rows_bwd)


@jax.custom_vjp
def _neumann_inv(m):
    ii = lax.broadcasted_iota(jnp.int32, m.shape, 1)
    jj = lax.broadcasted_iota(jnp.int32, m.shape, 2)
    inv = jnp.where(ii == jj, 1.0, 0.0).astype(f32) + m
    mp = m
    for _ in range(5):
        mp = _bmm3('bij,bjk->bik', mp, mp)
        inv = inv + _bmm3('bij,bjk->bik', inv, mp)
    return inv


def _neumann_inv_fwd(m):
    inv = _neumann_inv(m)
    return inv, inv


def _neumann_inv_bwd(inv, g):
    return (_bmm3('bij,bkj->bik', _bmm3('bji,bjk->bik', inv, g), inv),)


_neumann_inv.defvjp(_neumann_inv_fwd, _neumann_inv_bwd)


def _ln_stats(z):
    mu = jnp.mean(z, -1, keepdims=True)
    zc = z - mu
    rstd = lax.rsqrt(jnp.mean(zc * zc, -1, keepdims=True) + LN_EPS)
    return zc * rstd, rstd


def _ln_bwd(z, g, dout):
    xh, rstd = _ln_stats(z)
    dxh = dout * g
    dz = rstd * (dxh - jnp.mean(dxh, -1, keepdims=True) - xh * jnp.mean(dxh * xh, -1, keepdims=True))
    return dz, jnp.sum(dout * xh, 0, keepdims=True), jnp.sum(dout, 0, keepdims=True)


def _full_spec(shape):
    nd = len(shape)
    return pl.BlockSpec(shape, lambda *_: (0,) * nd)


def _hidden_cols(w_ref):
    return jnp.concatenate([w_ref[j] for j in range(FB)], axis=1)


def _ffn_fwd(x, p384, pd, lg, lb, layer, which, ple=None):
    T = x.shape[0]
    sg, su, sd = 4 * layer + 2 * which, 4 * layer + 2 * which + 1, 2 * layer + which
    has_ple = ple is not None

    def body(*refs):
        if has_ple:
            x_ref, wg_ref, wu_ref, wd_ref, lg_ref, lb_ref, p_ref, wpg_ref, wpp_ref, z_ref, o_ref, acc, xb_s = refs
        else:
            x_ref, wg_ref, wu_ref, wd_ref, lg_ref, lb_ref, z_ref, o_ref, acc, xb_s = refs
        f = pl.program_id(1)

        @pl.when(f == 0)
        def _():
            x = x_ref[...]
            xb = x.astype(bf16)
            xb_s[...] = xb
            base = ALPHA * x
            if has_ple:
                gate = _sigmoid(_dot(xb, wpg_ref[...].reshape(D, D)))
                base = base + gate * _dot(p_ref[...], wpp_ref[...])
            acc[...] = base

        xb = xb_s[...]
        g = _dot(xb, _hidden_cols(wg_ref))
        u = _dot(xb, _hidden_cols(wu_ref))
        acc[...] += 0.5 * _dot(_silu(g) * u, wd_ref[...].reshape(FB * FSP, D))

        @pl.when(f == NF - 1)
        def _():
            z = acc[...]
            z_ref[...] = z
            o_ref[...] = _ln_stats(z)[0] * lg_ref[...] + lb_ref[...]

    row = pl.BlockSpec((TM, D), lambda i, f: (i, 0))
    in_specs = [row,
                pl.BlockSpec((FB, None, D, FSP), lambda i, f: (f, sg, 0, 0)),
                pl.BlockSpec((FB, None, D, FSP), lambda i, f: (f, su, 0, 0)),
                pl.BlockSpec((FB, None, FSP, D), lambda i, f: (f, sd, 0, 0)),
                _full_spec((1, D)), _full_spec((1, D))]
    args = [x, p384, p384, pd, lg, lb]
    if has_ple:
        p, pr, wpp = ple
        in_specs += [pl.BlockSpec((TM, PLE), lambda i, f: (i, 0)),
                     pl.BlockSpec((NDEV, None, 128, D), lambda i, f: (0, 3 * layer + 2, 0, 0)),
                     _full_spec((PLE, D))]
        args += [p, pr, wpp]
    return pl.pallas_call(
        body, grid=(T // TM, NF), in_specs=in_specs, out_specs=[row, row],
        out_shape=[jax.ShapeDtypeStruct((T, D), f32)] * 2,
        scratch_shapes=[pltpu.VMEM((TM, D), f32), pltpu.VMEM((TM, D), bf16)],
        compiler_params=_cparams(("arbitrary", "arbitrary")), name=f"ffn{which + 1}_fwd")(*args)


def _ffn_bwd(x, z, dout, p384, pd, lg, layer, which, ple=None):
    T = x.shape[0]
    TMB = TM // 2
    sg, su, sd = 4 * layer + 2 * which, 4 * layer + 2 * which + 1, 2 * layer + which
    has_ple = ple is not None

    def body(*refs):
        if has_ple:
            (x_ref, z_ref, do_ref, wg_ref, wu_ref, wd_ref, lg_ref, p_ref, wpg_ref, wpp_ref,
             dx_ref, dg_ref, du_ref, a_ref, dy_ref, dlg_ref, dlb_ref, dgp_ref, dpj_ref, acc, dyb, xb_s) = refs
        else:
            (x_ref, z_ref, do_ref, wg_ref, wu_ref, wd_ref, lg_ref,
             dx_ref, dg_ref, du_ref, a_ref, dy_ref, dlg_ref, dlb_ref, acc, dyb, xb_s) = refs
        i, f = pl.program_id(0), pl.program_id(1)

        @pl.when(jnp.logical_and(i == 0, f == 0))
        def _():
            dlg_ref[...] = jnp.zeros_like(dlg_ref)
            dlb_ref[...] = jnp.zeros_like(dlb_ref)

        @pl.when(f == 0)
        def _():
            xb = x_ref[...].astype(bf16)
            xb_s[...] = xb
            dz, dlg, dlb = _ln_bwd(z_ref[...], lg_ref[...], do_ref[...])
            dlg_ref[...] += dlg
            dlb_ref[...] += dlb
            dy = (0.5 * dz).astype(bf16)
            dyb[...] = dy
            dy_ref[...] = dy
            dx = ALPHA * dz
            if has_ple:
                wpg = wpg_ref[...].reshape(D, D)
                gate = _sigmoid(_dot(xb, wpg))
                proj = _dot(p_ref[...], wpp_ref[...])
                dgp = (dz * proj * gate * (1.0 - gate)).astype(bf16)
                dgp_ref[...] = dgp
                dpj_ref[...] = (dz * gate).astype(bf16)
                dx = dx + _dot_nt(dgp, wpg)
            acc[...] = dx

        xb = xb_s[...]
        wg, wu = _hidden_cols(wg_ref), _hidden_cols(wu_ref)
        g = _dot(xb, wg)
        u = _dot(xb, wu)
        da = _dot_nt(dyb[...], wd_ref[...].reshape(FB * FSP, D))
        sgm = _sigmoid(g)
        dg = (da * u * (sgm * (1.0 + g * (1.0 - sgm)))).astype(bf16)
        du = (da * (g * sgm)).astype(bf16)
        dg_ref[...] = dg
        du_ref[...] = du
        a_ref[...] = (g * sgm * u).astype(bf16)
        acc[...] += _dot_nt(dg, wg) + _dot_nt(du, wu)

        @pl.when(f == NF - 1)
        def _():
            dx_ref[...] = acc[...]

    row = pl.BlockSpec((TMB, D), lambda i, f: (i, 0))
    hid = pl.BlockSpec((TMB, FB * FSP), lambda i, f: (i, f))
    vec = _full_spec((1, D))
    in_specs = [row, row, row,
                pl.BlockSpec((FB, None, D, FSP), lambda i, f: (f, sg, 0, 0)),
                pl.BlockSpec((FB, None, D, FSP), lambda i, f: (f, su, 0, 0)),
                pl.BlockSpec((FB, None, FSP, D), lambda i, f: (f, sd, 0, 0)),
                vec]
    args = [x, z, dout, p384, p384, pd, lg]
    out_specs = [row, hid, hid, hid, row, vec, vec]
    hshape = jax.ShapeDtypeStruct((T, NDEV * FSP), bf16)
    out_shape = [jax.ShapeDtypeStruct((T, D), f32), hshape, hshape, hshape, jax.ShapeDtypeStruct((T, D), bf16),
                 jax.ShapeDtypeStruct((1, D), f32), jax.ShapeDtypeStruct((1, D), f32)]
    if has_ple:
        p, pr, wpp = ple
        in_specs += [pl.BlockSpec((TMB, PLE), lambda i, f: (i, 0)),
                     pl.BlockSpec((NDEV, None, 128, D), lambda i, f: (0, 3 * layer + 2, 0, 0)),
                     _full_spec((PLE, D))]
        args += [p, pr, wpp]
        out_specs += [row, row]
        out_shape += [jax.ShapeDtypeStruct((T, D), bf16)] * 2
    return pl.pallas_call(
        body, grid=(T // TMB, NF), in_specs=in_specs, out_specs=out_specs, out_shape=out_shape,
        scratch_shapes=[pltpu.VMEM((TMB, D), f32), pltpu.VMEM((TMB, D), bf16), pltpu.VMEM((TMB, D), bf16)],
        compiler_params=_cparams(("arbitrary", "arbitrary")), name=f"ffn{which + 1}_bwd")(*args)


def _matmul_tn(a, b, nb, name, nsub=1):
    T, M = a.shape
    N = b.shape[1]
    tk = min(T, 1024)
    nk = T // tk
    wide = nsub * nb

    def body(a_ref, b_ref, o_ref, acc):
        k = pl.program_id(1)

        @pl.when(k == 0)
        def _():
            acc[...] = jnp.zeros_like(acc)

        acc[...] += _dot_tn(a_ref[...], b_ref[...])

        @pl.when(k == nk - 1)
        def _():
            for j in range(nsub):
                o_ref[j] = acc[:, j * nb:(j + 1) * nb].astype(bf16)

    return pl.pallas_call(
        body, grid=(N // wide, nk),
        in_specs=[pl.BlockSpec((tk, M), lambda n, k: (k, 0)), pl.BlockSpec((tk, wide), lambda n, k: (k, n))],
        out_specs=pl.BlockSpec((nsub, M, nb), lambda n, k: (n, 0, 0)),
        out_shape=jax.ShapeDtypeStruct((N // nb, M, nb), bf16),
        scratch_shapes=[pltpu.VMEM((M, wide), f32)],
        compiler_params=_cparams(("arbitrary", "arbitrary")), name=name)(a, b)


def _proj_in(x, pr, pinl, ping, layer):
    T = x.shape[0]

    def body(x_ref, wr_ref, wl_ref, wg_ref, hr_ref, hl_ref, hg_ref):
        xb = x_ref[...].astype(bf16)
        hr_ref[...] = _dot(xb, wr_ref[...].reshape(D, D))
        hl_ref[...] = _dot(xb, wl_ref[...].reshape(D, 2 * LRU_W))
        hg_ref[...] = _dot(xb, wg_ref[...].reshape(D, GDN_IN))

    return pl.pallas_call(
        body, grid=(T // TM,),
        in_specs=[pl.BlockSpec((TM, D), lambda i: (i, 0)),
                  pl.BlockSpec((NDEV, None, 128, D), lambda i: (0, 3 * layer, 0, 0)),
                  pl.BlockSpec((NDEV, None, 128, 2 * LRU_W), lambda i: (0, layer, 0, 0)),
                  pl.BlockSpec((NDEV, None, 128, GDN_IN), lambda i: (0, layer, 0, 0))],
        out_specs=[pl.BlockSpec((TM, D), lambda i: (i, 0)), pl.BlockSpec((TM, 2 * LRU_W), lambda i: (i, 0)),
                   pl.BlockSpec((TM, GDN_IN), lambda i: (i, 0))],
        out_shape=[jax.ShapeDtypeStruct((T, D), f32), jax.ShapeDtypeStruct((T, 2 * LRU_W), f32),
                   jax.ShapeDtypeStruct((T, GDN_IN), f32)],
        compiler_params=_cparams(("arbitrary",)), name="proj_in")(x, pr, pinl, ping)


def _proj_in_bwd(base, dhr, dhl, dhg, pr, pinl, ping, layer):
    T = base.shape[0]

    def body(b_ref, dr_ref, dl_ref, dg_ref, wr_ref, wl_ref, wg_ref, o_ref):
        o_ref[...] = (b_ref[...] + _dot_nt(dr_ref[...], wr_ref[...].reshape(D, D))
                      + _dot_nt(dl_ref[...], wl_ref[...].reshape(D, 2 * LRU_W))
                      + _dot_nt(dg_ref[...], wg_ref[...].reshape(D, GDN_IN)))

    return pl.pallas_call(
        body, grid=(T // TM,),
        in_specs=[pl.BlockSpec((TM, D), lambda i: (i, 0)), pl.BlockSpec((TM, D), lambda i: (i, 0)),
                  pl.BlockSpec((TM, 2 * LRU_W), lambda i: (i, 0)), pl.BlockSpec((TM, GDN_IN), lambda i: (i, 0)),
                  pl.BlockSpec((NDEV, None, 128, D), lambda i: (0, 3 * layer, 0, 0)),
                  pl.BlockSpec((NDEV, None, 128, 2 * LRU_W), lambda i: (0, layer, 0, 0)),
                  pl.BlockSpec((NDEV, None, 128, GDN_IN), lambda i: (0, layer, 0, 0))],
        out_specs=pl.BlockSpec((TM, D), lambda i: (i, 0)),
        out_shape=jax.ShapeDtypeStruct((T, D), f32),
        compiler_params=_cparams(("arbitrary",)), name="proj_in_bwd")(base, dhr, dhl, dhg, pr, pinl, ping)


def _mix_out(x1, o_r, o_l, o_g, pr, lg, lb, layer):
    T = x1.shape[0]

    def body(x_ref, r_ref, l_ref, g_ref, w_ref, lg_ref, lb_ref, z_ref, o_ref):
        w = w_ref[...].reshape(D, D)
        z = (ALPHA * x_ref[...] + _dot(r_ref[...], w[0:RET_W]) + _dot(l_ref[...], w[RET_W:RET_W + LRU_W])
             + _dot(g_ref[...], w[RET_W + LRU_W:D]))
        z_ref[...] = z
        o_ref[...] = _ln_stats(z)[0] * lg_ref[...] + lb_ref[...]

    row = pl.BlockSpec((TM, D), lambda i: (i, 0))
    return pl.pallas_call(
        body, grid=(T // TM,),
        in_specs=[row, pl.BlockSpec((TM, RET_W), lambda i: (i, 0)), pl.BlockSpec((TM, LRU_W), lambda i: (i, 0)),
                  pl.BlockSpec((TM, GDN_W), lambda i: (i, 0)),
                  pl.BlockSpec((NDEV, None, 128, D), lambda i: (0, 3 * layer + 1, 0, 0)),
                  _full_spec((1, D)), _full_spec((1, D))],
        out_specs=[row, row], out_shape=[jax.ShapeDtypeStruct((T, D), f32)] * 2,
        compiler_params=_cparams(("arbitrary",)), name="mix_out")(x1, o_r, o_l, o_g, pr, lg, lb)


def _mix_out_bwd(z, dout, pr, lg, layer):
    T = z.shape[0]

    def body(z_ref, do_ref, w_ref, lg_ref, dxb_ref, dzb_ref, dr_ref, dl_ref, dg_ref, dlg_ref, dlb_ref):
        @pl.when(pl.program_id(0) == 0)
        def _():
            dlg_ref[...] = jnp.zeros_like(dlg_ref)
            dlb_ref[...] = jnp.zeros_like(dlb_ref)

        dz, dlg, dlb = _ln_bwd(z_ref[...], lg_ref[...], do_ref[...])
        dlg_ref[...] += dlg
        dlb_ref[...] += dlb
        dxb_ref[...] = ALPHA * dz
        dzb = dz.astype(bf16)
        dzb_ref[...] = dzb
        w = w_ref[...].reshape(D, D)
        dr_ref[...] = _dot_nt(dzb, w[0:RET_W])
        dl_ref[...] = _dot_nt(dzb, w[RET_W:RET_W + LRU_W])
        dg_ref[...] = _dot_nt(dzb, w[RET_W + LRU_W:D])

    row = pl.BlockSpec((TM, D), lambda i: (i, 0))
    vec = _full_spec((1, D))
    return pl.pallas_call(
        body, grid=(T // TM,),
        in_specs=[row, row, pl.BlockSpec((NDEV, None, 128, D), lambda i: (0, 3 * layer + 1, 0, 0)), vec],
        out_specs=[row, row, pl.BlockSpec((TM, RET_W), lambda i: (i, 0)), pl.BlockSpec((TM, LRU_W), lambda i: (i, 0)),
                   pl.BlockSpec((TM, GDN_W), lambda i: (i, 0)), vec, vec],
        out_shape=[jax.ShapeDtypeStruct((T, D), f32), jax.ShapeDtypeStruct((T, D), bf16),
                   jax.ShapeDtypeStruct((T, RET_W), f32), jax.ShapeDtypeStruct((T, LRU_W), f32),
                   jax.ShapeDtypeStruct((T, GDN_W), f32), jax.ShapeDtypeStruct((1, D), f32),
                   jax.ShapeDtypeStruct((1, D), f32)],
        compiler_params=_cparams(("arbitrary",)), name="mix_out_bwd")(z, dout, pr, lg)


def _loss_grad(y, target):
    T = y.shape[0]

    def body(y_ref, t_ref, dy_ref, l_ref):
        @pl.when(pl.program_id(0) == 0)
        def _():
            l_ref[...] = jnp.zeros_like(l_ref)

        e = y_ref[...] - t_ref[...]
        dy_ref[...] = e * (1.0 / D)
        l_ref[...] += 0.5 * jnp.sum(jnp.sum(e * e, -1, keepdims=True) * (1.0 / D), 0, keepdims=True)

    row = pl.BlockSpec((TM, D), lambda i: (i, 0))
    return pl.pallas_call(
        body, grid=(T // TM,), in_specs=[row, row], out_specs=[row, _full_spec((1, 1))],
        out_shape=[jax.ShapeDtypeStruct((T, D), f32), jax.ShapeDtypeStruct((1, 1), f32)],
        compiler_params=_cparams(("arbitrary",)), name="loss_grad")(y, target)


def _split_heads(x, H):
    n = x.shape[0] // CH
    parts = [x[:, h * CH:(h + 1) * CH].reshape(n, CH, CH) for h in range(H)]
    return jnp.stack(parts, axis=1).reshape(n * H, CH, CH)


def _merge_heads(ref, x, H, col0=0):
    n = x.shape[0] // H
    x4 = x.reshape(n, H, CH, CH)
    for h in range(H):
        ref[:, col0 + h * CH:col0 + (h + 1) * CH] = x4[:, h].reshape(n * CH, CH)


def _conv_fwd(ext, x, tail, w, R):
    ext[0:8, :] = tail
    ext[8:R + 8, :] = x
    y = w[3:4, :] * x
    for k in range(3):
        y = y + w[k:k + 1, :] * ext[5 + k:5 + k + R, :]
    return y


def _conv_bwd(ext, ext2, dy, dy_next, w, R):
    ext2[0:R, :] = dy
    ext2[R:R + 8, :] = dy_next
    dx = w[3:4, :] * dy
    dws = []
    for k in range(3):
        dx = dx + w[k:k + 1, :] * ext2[3 - k:3 - k + R, :]
        dws.append(jnp.sum(dy * ext[5 + k:5 + k + R, :], 0, keepdims=True))
    dws.append(jnp.sum(dy * ext[8:8 + R, :], 0, keepdims=True))
    return dx, jnp.concatenate(dws, axis=0)


def _prev_tail_spec(R, W):
    return pl.BlockSpec((8, W), lambda i: (jnp.maximum(i * (R // 8) - 1, 0), 0))


def _prev_tail_spec_rev(R, W, nb):
    return pl.BlockSpec((8, W), lambda i: (jnp.maximum((nb - 1 - i) * (R // 8) - 1, 0), 0))


def _rope_tables(positions):
    T = positions.shape[0]

    def body(p_ref, c_ref, s_ref):
        lane = lax.broadcasted_iota(jnp.int32, (TM, RET_W), 1)
        fi = (lane % 32).astype(f32)
        inv = jnp.exp(fi * (-math.log(ROPE_THETA) / 32.0))
        ang = p_ref[...].astype(f32) * inv
        c_ref[...] = jnp.cos(ang)
        s_ref[...] = jnp.where(lane % CH < 32, -jnp.sin(ang), jnp.sin(ang))

    row = pl.BlockSpec((TM, RET_W), lambda i: (i, 0))
    return pl.pallas_call(
        body, grid=(T // TM,), in_specs=[pl.BlockSpec((TM, 1), lambda i: (i, 0))], out_specs=[row, row],
        out_shape=[jax.ShapeDtypeStruct((T, RET_W), f32)] * 2,
        compiler_params=_cparams(("arbitrary",)), name="rope_tables")(positions)


def _partner(x):
    lane = lax.broadcasted_iota(jnp.int32, x.shape, 1)
    return jnp.where(lane % CH < 32, pltpu.roll(x, RET_W - 32, 1), pltpu.roll(x, 32, 1))


def _ret_consts():
    ii = lax.broadcasted_iota(jnp.int32, (CH, CH), 0).astype(f32)
    jj = lax.broadcasted_iota(jnp.int32, (CH, CH), 1).astype(f32)
    intra, cross, tail, cd = [], [], [], []
    for h in range(RET_H):
        lg = math.log1p(-(2.0 ** (-5.0 - h)))
        intra.append(jnp.exp(jnp.abs(ii - jj) * lg))
        cross.append(jnp.exp((ii + 1.0) * lg))
        tail.append(jnp.exp((CH - 1.0 - ii) * lg))
        cd.append(jnp.full((CH, CH), math.exp(CH * lg), f32))
    return jnp.stack(intra), jnp.stack(cross), jnp.stack(tail), jnp.stack(cd)


def _ret_chunk(consts, q, k, v, st):
    intra, cross, tail, cd = consts
    s = _bmm('hid,hjd->hij', q, k) * intra
    o = _bmm('hij,hje->hie', s, v) + _bmm('hid,hde->hie', q * cross, st)
    st2 = st * cd + _bmm('hjd,hje->hde', k * tail, v)
    mu = jnp.mean(o, -1, keepdims=True)
    oc = o - mu
    on = oc * lax.rsqrt(jnp.mean(oc * oc, -1, keepdims=True) + 1e-5)
    return on, st2


def _ret_fwd(hr, cosw, sinw, gam):
    T = hr.shape[0]
    R = RB_RET
    nc = R // CH

    def body(h_ref, c_ref, s_ref, g_ref, o_ref, st_ref, st, wide):
        @pl.when(pl.program_id(0) == 0)
        def _():
            st[...] = jnp.zeros_like(st)

        consts = _ret_consts()
        cw, sw = c_ref[...], s_ref[...]
        q, k = h_ref[:, 0:RET_W], h_ref[:, RET_W:2 * RET_W]
        qh = _split_heads((q * cw + _partner(q) * sw) * 0.125, RET_H)
        kh = _split_heads(k * cw + _partner(k) * sw, RET_H)
        vh = _split_heads(h_ref[:, 2 * RET_W:3 * RET_W], RET_H)
        outs = []
        s_cur = st[...]
        for c in range(nc):
            sl = slice(c * RET_H, (c + 1) * RET_H)
            st_ref[c] = s_cur
            on, s_cur = _ret_chunk(consts, qh[sl], kh[sl], vh[sl], s_cur)
            outs.append(on)
        st[...] = s_cur
        _merge_heads(wide, jnp.concatenate(outs, axis=0), RET_H)
        o_ref[...] = wide[...] * g_ref[...] * _silu(h_ref[:, 3 * RET_W:4 * RET_W])

    blk = pl.BlockSpec((R, RET_W), lambda i: (i, 0))
    return pl.pallas_call(
        body, grid=(T // R,),
        in_specs=[pl.BlockSpec((R, D), lambda i: (i, 0)), blk, blk, _full_spec((1, RET_W))],
        out_specs=[blk, pl.BlockSpec((nc, RET_H, CH, CH), lambda i: (i, 0, 0, 0))],
        out_shape=[jax.ShapeDtypeStruct((T, RET_W), f32), jax.ShapeDtypeStruct((T // CH, RET_H, CH, CH), f32)],
        scratch_shapes=[pltpu.VMEM((RET_H, CH, CH), f32), pltpu.VMEM((R, RET_W), f32)],
        compiler_params=_cparams(("arbitrary",)), name="ret_fwd")(hr, cosw, sinw, gam)


def _ret_bwd(hr, cosw, sinw, gam, states, dout):
    T = hr.shape[0]
    R = RB_RET
    nc = R // CH
    nb = T // R

    def body(h_ref, c_ref, s_ref, g_ref, st_ref, do_ref, dh_ref, dgam_ref, dst, wide):
        @pl.when(pl.program_id(0) == 0)
        def _():
            dst[...] = jnp.zeros_like(dst)
            dgam_ref[...] = jnp.zeros_like(dgam_ref)

        consts = _ret_consts()
        cw, sw = c_ref[...], s_ref[...]
        q, k = h_ref[:, 0:RET_W], h_ref[:, RET_W:2 * RET_W]
        gr = h_ref[:, 3 * RET_W:4 * RET_W]
        qh = _split_heads((q * cw + _partner(q) * sw) * 0.125, RET_H)
        kh = _split_heads(k * cw + _partner(k) * sw, RET_H)
        vh = _split_heads(h_ref[:, 2 * RET_W:3 * RET_W], RET_H)
        do = do_ref[...]
        gam = g_ref[...]
        sg = _silu(gr)
        don = _split_heads(do * gam * sg, RET_H)
        ons, dqs, dks, dvs = [None] * nc, [None] * nc, [None] * nc, [None] * nc
        ds = dst[...]
        for c in reversed(range(nc)):
            sl = slice(c * RET_H, (c + 1) * RET_H)
            (on, _), vjp = jax.vjp(functools.partial(_ret_chunk, consts), qh[sl], kh[sl], vh[sl], st_ref[c])
            dqs[c], dks[c], dvs[c], ds = vjp((don[sl], ds))
            ons[c] = on
        dst[...] = ds
        _merge_heads(wide, jnp.concatenate(ons, axis=0), RET_H)
        onw = wide[...]
        dgam_ref[...] += jnp.sum(do * onw * sg, 0, keepdims=True)
        dh_ref[:, 3 * RET_W:4 * RET_W] = do * onw * gam * _dsilu(gr)
        _merge_heads(wide, jnp.concatenate(dqs, axis=0), RET_H)
        u = wide[...] * 0.125
        dh_ref[:, 0:RET_W] = u * cw + _partner(u * sw)
        _merge_heads(wide, jnp.concatenate(dks, axis=0), RET_H)
        u = wide[...]
        dh_ref[:, RET_W:2 * RET_W] = u * cw + _partner(u * sw)
        _merge_heads(dh_ref, jnp.concatenate(dvs, axis=0), RET_H, col0=2 * RET_W)

    blk = pl.BlockSpec((R, RET_W), lambda i: (nb - 1 - i, 0))
    return pl.pallas_call(
        body, grid=(nb,),
        in_specs=[pl.BlockSpec((R, D), lambda i: (nb - 1 - i, 0)), blk, blk, _full_spec((1, RET_W)),
                  pl.BlockSpec((nc, RET_H, CH, CH), lambda i: (nb - 1 - i, 0, 0, 0)), blk],
        out_specs=[pl.BlockSpec((R, D), lambda i: (nb - 1 - i, 0)), _full_spec((1, RET_W))],
        out_shape=[jax.ShapeDtypeStruct((T, D), f32), jax.ShapeDtypeStruct((1, RET_W), f32)],
        scratch_shapes=[pltpu.VMEM((RET_H, CH, CH), f32), pltpu.VMEM((R, RET_W), f32)],
        compiler_params=_cparams(("arbitrary",)), name="ret_bwd")(hr, cosw, sinw, gam, states, dout)


def _lru_ab(xc, wa, ba, wx, bx, lam):
    r = _sigmoid(_dot(xc, wa) + ba)
    i = _sigmoid(_dot(xc, wx) + bx)
    la = 8.0 * r * (-_softplus(-lam))
    a = jnp.exp(la)
    em = jnp.tanh(la) * (jnp.exp(2.0 * la) + 1.0)
    return a, jnp.sqrt(-em) * (i * xc)


def _lru_out(h, gate):
    return h * _gelu(gate)


def _scan_fwd(a, b):
    R = a.shape[0]
    row = lax.broadcasted_iota(jnp.int32, a.shape, 0)
    d = 1
    while d < R:
        m = row >= d
        b = jnp.where(m, a * pltpu.roll(b, d, 0) + b, b)
        a = jnp.where(m, a * pltpu.roll(a, d, 0), a)
        d *= 2
    return a, b


def _scan_bwd(a, b):
    R = a.shape[0]
    row = lax.broadcasted_iota(jnp.int32, a.shape, 0)
    d = 1
    while d < R:
        m = row < R - d
        b = jnp.where(m, a * pltpu.roll(b, R - d, 0) + b, b)
        a = jnp.where(m, a * pltpu.roll(a, R - d, 0), a)
        d *= 2
    return b


def _lru_fwd(hl, cw, cb, wa, ba, wx, bx, lam):
    T = hl.shape[0]
    R = RB_LRU
    W = LRU_W

    def body(h_ref, t_ref, cw_ref, cb_ref, wa_ref, ba_ref, wx_ref, bx_ref, lam_ref, o_ref, hs_ref, carry, ext):
        first = pl.program_id(0) == 0

        @pl.when(first)
        def _():
            carry[...] = jnp.zeros_like(carry)

        tail = jnp.where(first, 0.0, t_ref[:, 0:W])
        xc = _conv_fwd(ext, h_ref[:, 0:W], tail, cw_ref[...], R) + cb_ref[...]
        a, b = _lru_ab(xc, wa_ref[...], ba_ref[...], wx_ref[...], bx_ref[...], lam_ref[...])
        ap, hloc = _scan_fwd(a, b)
        h = hloc + ap * carry[0:1, :]
        carry[...] = jnp.broadcast_to(h[R - 1:R, :], carry.shape)
        hs_ref[...] = h
        o_ref[...] = _lru_out(h, h_ref[:, W:2 * W])

    vec = _full_spec((1, W))
    blk = pl.BlockSpec((R, W), lambda i: (i, 0))
    return pl.pallas_call(
        body, grid=(T // R,),
        in_specs=[pl.BlockSpec((R, 2 * W), lambda i: (i, 0)), _prev_tail_spec(R, 2 * W), _full_spec((4, W)), vec,
                  _full_spec((W, W)), vec, _full_spec((W, W)), vec, vec],
        out_specs=[blk, blk], out_shape=[jax.ShapeDtypeStruct((T, W), f32)] * 2,
        scratch_shapes=[pltpu.VMEM((8, W), f32), pltpu.VMEM((R + 8, W), f32)],
        compiler_params=_cparams(("arbitrary",)), name="lru_fwd")(hl, hl, cw, cb, wa, ba, wx, bx, lam)


def _lru_bwd(hl, hs, cw, cb, wa, ba, wx, bx, lam, dout):
    T = hl.shape[0]
    R = RB_LRU
    W = LRU_W
    nb = T // R

    def body(h_ref, t_ref, hs_ref, hst_ref, cw_ref, cb_ref, wa_ref, ba_ref, wx_ref, bx_ref, lam_ref, do_ref,
             dh_ref, dcw_ref, dcb_ref, dwa_ref, dba_ref, dwx_ref, dbx_ref, dlam_ref, carry_g, carry_dy, ext, ext2):
        i = pl.program_id(0)
        last_blk = i == 0
        first_blk = i == nb - 1

        @pl.when(last_blk)
        def _():
            carry_g[...] = jnp.zeros_like(carry_g)
            carry_dy[...] = jnp.zeros_like(carry_dy)
            for r in (dcw_ref, dcb_ref, dwa_ref, dba_ref, dwx_ref, dbx_ref, dlam_ref):
                r[...] = jnp.zeros_like(r)

        tail = jnp.where(first_blk, 0.0, t_ref[:, 0:W])
        xc = _conv_fwd(ext, h_ref[:, 0:W], tail, cw_ref[...], R) + cb_ref[...]
        (a, _), vjp_ab = jax.vjp(_lru_ab, xc, wa_ref[...], ba_ref[...], wx_ref[...], bx_ref[...], lam_ref[...])
        hs = hs_ref[...]
        _, vjp_out = jax.vjp(_lru_out, hs, h_ref[:, W:2 * W])
        dh, dgate = vjp_out(do_ref[...])
        row = lax.broadcasted_iota(jnp.int32, (R, W), 0)
        dh = jnp.where(row == R - 1, dh + carry_g[0:1, :], dh)
        a_up = jnp.where(row == R - 1, 0.0, pltpu.roll(a, R - 1, 0))
        g = _scan_bwd(a_up, dh)
        carry_g[...] = jnp.broadcast_to(a[0:1, :] * g[0:1, :], carry_g.shape)
        hprev0 = jnp.where(first_blk, 0.0, hst_ref[7:8, :])
        hprev = jnp.where(row == 0, hprev0, pltpu.roll(hs, 1, 0))
        dxc, dwa, dba, dwx, dbx, dlam = vjp_ab((g * hprev, g))
        dwa_ref[...] += dwa
        dba_ref[...] += dba
        dwx_ref[...] += dwx
        dbx_ref[...] += dbx
        dlam_ref[...] += dlam
        dcb_ref[...] += jnp.sum(dxc, 0, keepdims=True)
        dx, dcw = _conv_bwd(ext, ext2, dxc, carry_dy[...], cw_ref[...], R)
        carry_dy[...] = dxc[0:8, :]
        dcw_ref[...] += dcw
        dh_ref[:, 0:W] = dx
        dh_ref[:, W:2 * W] = dgate

    vec = _full_spec((1, W))
    mat = _full_spec((W, W))
    blk = pl.BlockSpec((R, W), lambda i: (nb - 1 - i, 0))
    blk2 = pl.BlockSpec((R, 2 * W), lambda i: (nb - 1 - i, 0))
    return pl.pallas_call(
        body, grid=(nb,),
        in_specs=[blk2, _prev_tail_spec_rev(R, 2 * W, nb), blk, _prev_tail_spec_rev(R, W, nb), _full_spec((4, W)), vec,
                  mat, vec, mat, vec, vec, blk],
        out_specs=[blk2, _full_spec((4, W)), vec, mat, vec, mat, vec, vec],
        out_shape=[jax.ShapeDtypeStruct((T, 2 * W), f32), jax.ShapeDtypeStruct((4, W), f32),
                   jax.ShapeDtypeStruct((1, W), f32), jax.ShapeDtypeStruct((W, W), f32),
                   jax.ShapeDtypeStruct((1, W), f32), jax.ShapeDtypeStruct((W, W), f32),
                   jax.ShapeDtypeStruct((1, W), f32), jax.ShapeDtypeStruct((1, W), f32)],
        scratch_shapes=[pltpu.VMEM((8, W), f32), pltpu.VMEM((8, W), f32), pltpu.VMEM((R + 8, W), f32),
                        pltpu.VMEM((R + 8, W), f32)],
        compiler_params=_cparams(("arbitrary",)), name="lru_bwd")(hl, hl, hs, hs, cw, cb, wa, ba, wx, bx, lam, dout)


def _gdn_local(qs, ks, vs, gb, bb):
    B = qs.shape[0]
    ii = lax.broadcasted_iota(jnp.int32, (B, CH, CH), 1)
    jj = lax.broadcasted_iota(jnp.int32, (B, CH, CH), 2)
    q = qs * lax.rsqrt(jnp.sum(qs * qs, -1, keepdims=True) + 1e-6)
    k = ks * lax.rsqrt(jnp.sum(ks * ks, -1, keepdims=True) + 1e-6)
    gc = _cumsum_rows(gb)
    gct = jnp.swapaxes(gc, 1, 2)
    decay = jnp.where(ii >= jj, jnp.exp(jnp.minimum(gc - gct, 0.0)), 0.0)
    kk = _bmm('bid,bjd->bij', k, k)
    inv = _neumann_inv(-jnp.where(ii > jj, bb * kk * decay, 0.0))
    egc = jnp.exp(gc)
    u = _bmm3('bij,bje->bie', inv, vs * bb)
    w = _bmm3('bij,bje->bie', inv, k * (bb * egc))
    qk = _bmm('bid,bjd->bij', q, k) * (0.125 * decay)
    glast = gc[:, CH - 1:CH, :]
    return u, w, qk, q * (0.125 * egc), k * jnp.exp(glast - gc), jnp.exp(jnp.broadcast_to(glast, gc.shape))


def _gdn_step(st, u, w, qk, qd, kt, egl, z, gn):
    vnew = u - _bmm('hcd,hde->hce', w, st)
    o = _bmm('hcd,hde->hce', qd, st) + _bmm('hij,hje->hie', qk, vnew)
    st2 = st * egl + _bmm('hcd,hce->hde', kt, vnew)
    out = o * lax.rsqrt(jnp.mean(o * o, -1, keepdims=True) + 1e-6) * gn * _silu(z)
    return out, st2


def _gdn_scalars(ab, alog, dtb):
    sp = _softplus(ab + dtb)
    return -jnp.exp(alog) * sp, _sigmoid(ab)


def _bcast_heads(blk, lane0, H):
    R = blk.shape[0]
    n = R // CH
    parts = [jnp.broadcast_to(blk[:, lane0 + h:lane0 + h + 1], (R, CH)).reshape(n, CH, CH) for h in range(H)]
    return jnp.stack(parts, axis=1).reshape(n * H, CH, CH)


def _unbcast_heads(x, lane0, H):
    n = x.shape[0] // H
    R = n * CH
    s = jnp.sum(x, axis=2, keepdims=True).reshape(n, H, CH, 1)
    lane = lax.broadcasted_iota(jnp.int32, (R, 128), 1)
    acc = jnp.zeros((R, 128), f32)
    for h in range(H):
        acc = acc + jnp.where(lane == lane0 + h, jnp.broadcast_to(s[:, h].reshape(R, 1), (R, 128)), 0.0)
    return acc


def _gdn_fwd(hg, cw, alog, dtb, gn):
    T = hg.shape[0]
    R = RB_GDN
    nc = R // CH
    W3 = 3 * GDN_W
    H = GDN_H

    def body(h_ref, t_ref, cw_ref, al_ref, dt_ref, gn_ref, o_ref, st_ref, st, ext):
        first = pl.program_id(0) == 0

        @pl.when(first)
        def _():
            st[...] = jnp.zeros_like(st)

        tail = jnp.where(first, 0.0, t_ref[:, 0:W3])
        y = _silu(_conv_fwd(ext, h_ref[:, 0:W3], tail, cw_ref[...], R))
        qs, ks, vs = (_split_heads(y[:, j * GDN_W:(j + 1) * GDN_W], H) for j in range(3))
        zh = _split_heads(h_ref[:, W3:W3 + GDN_W], H)
        g, beta = _gdn_scalars(h_ref[:, W3 + GDN_W:GDN_IN], al_ref[...], dt_ref[...])
        loc = _gdn_local(qs, ks, vs, _bcast_heads(g, 0, H), _bcast_heads(beta, H, H))
        gnv = gn_ref[...]
        outs = []
        s_cur = st[...]
        for c in range(nc):
            sl = slice(c * H, (c + 1) * H)
            st_ref[c] = s_cur
            out, s_cur = _gdn_step(s_cur, *(t[sl] for t in loc), zh[sl], gnv)
            outs.append(out)
        st[...] = s_cur
        _merge_heads(o_ref, jnp.concatenate(outs, axis=0), H)

    return pl.pallas_call(
        body, grid=(T // R,),
        in_specs=[pl.BlockSpec((R, GDN_IN), lambda i: (i, 0)), _prev_tail_spec(R, GDN_IN), _full_spec((4, W3)),
                  _full_spec((1, 128)), _full_spec((1, 128)), _full_spec((1, CH))],
        out_specs=[pl.BlockSpec((R, GDN_W), lambda i: (i, 0)), pl.BlockSpec((nc, H, CH, CH), lambda i: (i, 0, 0, 0))],
        out_shape=[jax.ShapeDtypeStruct((T, GDN_W), f32), jax.ShapeDtypeStruct((T // CH, H, CH, CH), f32)],
        scratch_shapes=[pltpu.VMEM((H, CH, CH), f32), pltpu.VMEM((R + 8, W3), f32)],
        compiler_params=_cparams(("arbitrary",)), name="gdn_fwd")(hg, hg, cw, alog, dtb, gn)


def _gdn_bwd(hg, cw, alog, dtb, gn, states, dout):
    T = hg.shape[0]
    R = RB_GDN
    nc = R // CH
    nb = T // R
    W3 = 3 * GDN_W
    H = GDN_H

    def body(h_ref, t_ref, cw_ref, al_ref, dt_ref, gn_ref, st_ref, do_ref,
             dh_ref, dcw_ref, dal_ref, ddt_ref, dgn_ref, dst, carry_dy, ext, ext2, wide):
        i = pl.program_id(0)
        first_blk = i == nb - 1

        @pl.when(i == 0)
        def _():
            dst[...] = jnp.zeros_like(dst)
            carry_dy[...] = jnp.zeros_like(carry_dy)
            for r in (dcw_ref, dal_ref, ddt_ref, dgn_ref):
                r[...] = jnp.zeros_like(r)

        tail = jnp.where(first_blk, 0.0, t_ref[:, 0:W3])
        ypre = _conv_fwd(ext, h_ref[:, 0:W3], tail, cw_ref[...], R)
        y = _silu(ypre)
        qs, ks, vs = (_split_heads(y[:, j * GDN_W:(j + 1) * GDN_W], H) for j in range(3))
        zh = _split_heads(h_ref[:, W3:W3 + GDN_W], H)
        ab = h_ref[:, W3 + GDN_W:GDN_IN]
        alog, dtb = al_ref[...], dt_ref[...]
        g, beta = _gdn_scalars(ab, alog, dtb)
        loc, vjp_loc = jax.vjp(_gdn_local, qs, ks, vs, _bcast_heads(g, 0, H), _bcast_heads(beta, H, H))
        doh = _split_heads(do_ref[...], H)
        gnv = gn_ref[...]
        dloc = [[None] * nc for _ in range(6)]
        dzs = [None] * nc
        ds = dst[...]
        dgn = jnp.zeros((1, CH), f32)
        for c in reversed(range(nc)):
            sl = slice(c * H, (c + 1) * H)
            _, vjp = jax.vjp(_gdn_step, st_ref[c], *(t[sl] for t in loc), zh[sl], gnv)
            grads = vjp((doh[sl], ds))
            ds = grads[0]
            for j in range(6):
                dloc[j][c] = grads[1 + j]
            dzs[c] = grads[7]
            dgn = dgn + grads[8]
        dst[...] = ds
        dgn_ref[...] += dgn
        dqs, dks, dvs, dgb, dbb = vjp_loc(tuple(jnp.concatenate(d, axis=0) for d in dloc))
        lane = lax.broadcasted_iota(jnp.int32, (R, 128), 1)
        dg = _unbcast_heads(dgb, 0, H)
        dbeta = _unbcast_heads(dbb, H, H)
        da = dg * (-jnp.exp(alog)) * _sigmoid(ab + dtb)
        dh_ref[:, W3 + GDN_W:GDN_IN] = jnp.where(lane < H, da, dbeta * beta * (1.0 - beta))
        ddt_ref[...] += jnp.sum(jnp.where(lane < H, da, 0.0), 0, keepdims=True)
        dal_ref[...] += jnp.sum(jnp.where(lane < H, dg * g, 0.0), 0, keepdims=True)
        _merge_heads(dh_ref, jnp.concatenate(dzs, axis=0), H, col0=W3)
        for j, dpart in enumerate((dqs, dks, dvs)):
            _merge_heads(wide, dpart, H, col0=j * GDN_W)
        dy = wide[...] * _dsilu(ypre)
        dx, dcw = _conv_bwd(ext, ext2, dy, carry_dy[...], cw_ref[...], R)
        carry_dy[...] = dy[0:8, :]
        dcw_ref[...] += dcw
        dh_ref[:, 0:W3] = dx

    blk = pl.BlockSpec((R, GDN_IN), lambda i: (nb - 1 - i, 0))
    return pl.pallas_call(
        body, grid=(nb,),
        in_specs=[blk, _prev_tail_spec_rev(R, GDN_IN, nb), _full_spec((4, W3)), _full_spec((1, 128)),
                  _full_spec((1, 128)), _full_spec((1, CH)),
                  pl.BlockSpec((nc, H, CH, CH), lambda i: (nb - 1 - i, 0, 0, 0)),
                  pl.BlockSpec((R, GDN_W), lambda i: (nb - 1 - i, 0))],
        out_specs=[blk, _full_spec((4, W3)), _full_spec((1, 128)), _full_spec((1, 128)), _full_spec((1, CH))],
        out_shape=[jax.ShapeDtypeStruct((T, GDN_IN), f32), jax.ShapeDtypeStruct((4, W3), f32),
                   jax.ShapeDtypeStruct((1, 128), f32), jax.ShapeDtypeStruct((1, 128), f32),
                   jax.ShapeDtypeStruct((1, CH), f32)],
        scratch_shapes=[pltpu.VMEM((H, CH, CH), f32), pltpu.VMEM((8, W3), f32), pltpu.VMEM((R + 8, W3), f32),
                        pltpu.VMEM((R + 8, W3), f32), pltpu.VMEM((R, W3), f32)],
        compiler_params=_cparams(("arbitrary",)), name="gdn_bwd")(hg, hg, cw, alog, dtb, gn, states, dout)


def _block_diag(w):
    out = jnp.zeros((LRU_W, LRU_W), w.dtype)
    for g in range(w.shape[0]):
        out = lax.dynamic_update_slice(out, w[g], (g * CH, g * CH))
    return out


def _block_diag_t(w):
    return jnp.stack([w[g * CH:(g + 1) * CH, g * CH:(g + 1) * CH] for g in range(LRU_W // CH)])


def _pad_lanes(v, n=128):
    return jnp.pad(v, (0, n - v.shape[0]))[None, :]


def _local_step(x, p, positions, target, G, sm):
    p384, pd, pr, pinl, ping, wpp = G['p384'], G['pd'], G['pr'], G['pinl'], G['ping'], G['wpp']
    cosw, sinw = _rope_tables(positions)
    saved = []
    h = x
    for l in range(DEPTH):
        v = lambda n: sm[n][l][None, :]
        z1, x1 = _ffn_fwd(h, p384, pd, v('ln_ffn1_g'), v('ln_ffn1_b'), l, 0)
        hr, hl, hg = _proj_in(x1, pr, pinl, ping, l)
        o_r, rst = _ret_fwd(hr, cosw, sinw, v('ret_norm_g'))
        lru_args = (sm['lru_conv_w'][l], v('lru_conv_b'), _block_diag(sm['lru_w_a'][l]), v('lru_b_a'),
                    _block_diag(sm['lru_w_x'][l]), v('lru_b_x'), v('lru_lambda'))
        o_l, hs = _lru_fwd(hl, *lru_args)
        gdn_args = (sm['gdn_conv_w'][l], _pad_lanes(sm['gdn_a_log'][l]), _pad_lanes(sm['gdn_dt_bias'][l]),
                    v('gdn_norm_g'))
        o_g, gst = _gdn_fwd(hg, *gdn_args)
        z2, x2 = _mix_out(x1, o_r, o_l, o_g, pr, v('ln_mix_g'), v('ln_mix_b'), l)
        z3, x3 = _ffn_fwd(x2, p384, pd, v('ln_ffn2_g'), v('ln_ffn2_b'), l, 1, ple=(p[l], pr, wpp[l]))
        saved.append((h, z1, x1, hr, hl, hg, o_r, rst, o_l, hs, lru_args, o_g, gst, gdn_args, z2, x2, z3))
        h = x3
    d, loss = _loss_grad(h, target)

    big = {k: [None] * DEPTH for k in ('p384', 'pd', 'pr', 'pinl', 'ping', 'ppp')}
    small = {n: [None] * DEPTH for n in SMALL}
    for l in reversed(range(DEPTH)):
        x0, z1, x1, hr, hl, hg, o_r, rst, o_l, hs, lru_args, o_g, gst, gdn_args, z2, x2, z3 = saved[l]
        v = lambda n: sm[n][l][None, :]
        d2, dg2, du2, a2, dy2, small['ln_ffn2_g'][l], small['ln_ffn2_b'][l], dgp, dpj = _ffn_bwd(
            x2, z3, d, p384, pd, v('ln_ffn2_g'), l, 1, ple=(p[l], pr, wpp[l]))
        dxb, dzb, do_r, do_l, do_g, small['ln_mix_g'][l], small['ln_mix_b'][l] = _mix_out_bwd(
            z2, d2, pr, v('ln_mix_g'), l)
        dhr, small['ret_norm_g'][l] = _ret_bwd(hr, cosw, sinw, v('ret_norm_g'), rst, do_r)
        (dhl, small['lru_conv_w'][l], small['lru_conv_b'][l], dwa, small['lru_b_a'][l], dwx, small['lru_b_x'][l],
         small['lru_lambda'][l]) = _lru_bwd(hl, hs, *lru_args, do_l)
        small['lru_w_a'][l], small['lru_w_x'][l] = _block_diag_t(dwa), _block_diag_t(dwx)
        dhg, small['gdn_conv_w'][l], dal, ddt, small['gdn_norm_g'][l] = _gdn_bwd(hg, *gdn_args, gst, do_g)
        small['gdn_a_log'][l], small['gdn_dt_bias'][l] = dal[:, 0:GDN_H], ddt[:, 0:GDN_H]
        d1 = _proj_in_bwd(dxb, dhr, dhl, dhg, pr, pinl, ping, l)
        d, dg1, du1, a1, dy1, small['ln_ffn1_g'][l], small['ln_ffn1_b'][l] = _ffn_bwd(
            x0, z1, d1, p384, pd, v('ln_ffn1_g'), l, 0)
        rows = lambda m: m.reshape(NDEV, m.shape[1] // NDEV, m.shape[2])
        big['p384'][l] = jnp.stack([_matmul_tn(x0, dg1, FSP, "dw_gate", FB), _matmul_tn(x0, du1, FSP, "dw_up", FB),
                                    _matmul_tn(x2, dg2, FSP, "dw_gate", FB), _matmul_tn(x2, du2, FSP, "dw_up", FB)],
                                   axis=1)
        big['pd'][l] = jnp.stack([rows(_matmul_tn(a1, dy1, D, "dw_down")),
                                  rows(_matmul_tn(a2, dy2, D, "dw_down"))], axis=1)
        dwo = jnp.concatenate([_matmul_tn(o_r, dzb, D, "dw_out_r"), _matmul_tn(o_l, dzb, D, "dw_out_l"),
                               _matmul_tn(o_g, dzb, D, "dw_out_g")], axis=1)
        big['pr'][l] = jnp.stack([rows(_matmul_tn(x1, dhr, D, "dw_in_r")), rows(dwo),
                                  rows(_matmul_tn(x2, dgp, D, "dw_ple_gate"))], axis=1)
        big['pinl'][l] = rows(_matmul_tn(x1, dhl, 2 * LRU_W, "dw_in_l"))
        big['ping'][l] = rows(_matmul_tn(x1, dhg, GDN_IN, "dw_in_g"))
        big['ppp'][l] = _matmul_tn(p[l], dpj, 128, "dw_ple_proj")
    big = {'p384': jnp.concatenate(big['p384'], axis=1), 'pd': jnp.concatenate(big['pd'], axis=1),
           'pr': jnp.concatenate(big['pr'], axis=1), 'pinl': jnp.stack(big['pinl'], axis=1),
           'ping': jnp.stack(big['ping'], axis=1), 'ppp': jnp.stack(big['ppp'], axis=1)}
    small = {n: jnp.stack([g.reshape(sm[n].shape[1:]) for g in gs]) for n, gs in small.items()}
    return loss, d, big, small


def _pack_big(ws, dtype=bf16):
    padc = lambda a, n: jnp.pad(a, ((0, 0), (0, 0), (0, n - a.shape[2])))
    padr = lambda a, n: jnp.pad(a, ((0, 0), (0, n - a.shape[1]), (0, 0)))
    per_layer = lambda arrs: jnp.stack(arrs, axis=1).reshape((-1,) + arrs[0].shape[1:])
    w_in = ws['w_in']
    out = {
        'p384': per_layer([padc(ws[n], FSP) for n in ('ffn1_w_gate', 'ffn1_w_up', 'ffn2_w_gate', 'ffn2_w_up')]),
        'pd': per_layer([padr(ws[n], FSP) for n in ('ffn1_w_down', 'ffn2_w_down')]),
        'pr': per_layer([w_in[:, :, 0:D], ws['w_out'], ws['ple_w_gate']]),
        'pinl': w_in[:, :, D:D + 2 * LRU_W],
        'ping': padc(w_in[:, :, D + 2 * LRU_W:D_IN], GDN_IN),
        'ppp': ws['ple_w_proj'],
    }
    return {k: a.astype(dtype) for k, a in out.items()}


def _exchange(arrays, scatter, name):
    n = len(arrays)

    def body(*refs):
        ins, outs = refs[:n], refs[n:2 * n]
        send_sems, recv_sems, local_sems = refs[2 * n:]
        x, y, c = lax.axis_index("x"), lax.axis_index("y"), lax.axis_index("c")
        me = 4 * x + 2 * y + c
        copies = []
        for i in range(n):
            src = ins[i].at[me] if scatter[i] else ins[i]
            cp = pltpu.make_async_copy(src, outs[i].at[me], local_sems.at[i])
            cp.start()
            copies.append(cp)
        sends = []
        for j in range(1, NDEV):
            peer = (me + j) % NDEV
            pid = (peer // 4, (peer // 2) % 2, peer % 2)
            for i in range(n):
                src = ins[i].at[peer] if scatter[i] else ins[i]
                cp = pltpu.make_async_remote_copy(
                    src_ref=src, dst_ref=outs[i].at[me], send_sem=send_sems.at[i, j - 1],
                    recv_sem=recv_sems.at[i, j - 1], device_id=pid, device_id_type=pl.DeviceIdType.MESH)
                cp.start()
                sends.append(cp)
        for j in range(1, NDEV):
            source = (me + NDEV - j) % NDEV
            sid = (source // 4, (source // 2) % 2, source % 2)
            for i in range(n):
                src = ins[i].at[me] if scatter[i] else ins[i]
                pltpu.make_async_remote_copy(
                    src_ref=src, dst_ref=outs[i].at[source], send_sem=send_sems.at[i, j - 1],
                    recv_sem=recv_sems.at[i, j - 1], device_id=sid, device_id_type=pl.DeviceIdType.MESH).wait_recv()
        for cp in sends:
            cp.wait_send()
        for cp in copies:
            cp.wait()

    hbm = pl.BlockSpec(memory_space=pltpu.HBM)
    out_shape = [jax.ShapeDtypeStruct(a.shape if s else (NDEV,) + a.shape, a.dtype) for a, s in zip(arrays, scatter)]
    return pl.pallas_call(
        body, in_specs=[hbm] * n, out_specs=[hbm] * n, out_shape=out_shape,
        scratch_shapes=[pltpu.SemaphoreType.DMA((n, NDEV - 1)), pltpu.SemaphoreType.DMA((n, NDEV - 1)),
                        pltpu.SemaphoreType.DMA((n,))],
        compiler_params=pltpu.CompilerParams(has_side_effects=True), name=name)(*arrays)


def _adam_math(w, g, m, v):
    m2 = ADAM_B1 * m + (1.0 - ADAM_B1) * g
    v2 = ADAM_B2 * v + (1.0 - ADAM_B2) * (g * g)
    m_hat = m2 / (1.0 - ADAM_B1 ** ADAM_STEP)
    v_hat = v2 / (1.0 - ADAM_B2 ** ADAM_STEP)
    return -ADAM_LR * (m_hat / (jnp.sqrt(v_hat) + ADAM_EPS) + ADAM_WD * w), m2, v2


def _adam_big(parts, w, m, v, name):
    L, rows, cols = w.shape

    def body(*refs):
        prefs = refs[:len(parts)]
        w_ref, m_ref, v_ref, g_ref, d_ref, m2_ref, v2_ref = refs[len(parts):]
        c0 = 0
        for pref in prefs:
            acc = pref[0].astype(f32)
            for s in range(1, NDEV):
                acc = acc + pref[s].astype(f32)
            width = min(acc.shape[1], cols - c0)
            g_ref[:, c0:c0 + width] = acc[0:rows, 0:width]
            c0 += width
        d, m2, v2 = _adam_math(w_ref[...], g_ref[...], m_ref[...], v_ref[...])
        d_ref[...] = d
        m2_ref[...] = m2
        v2_ref[...] = v2

    wspec = pl.BlockSpec((None, rows, cols), lambda l: (l, 0, 0))
    in_specs = [pl.BlockSpec((NDEV, None) + a.shape[2:], functools.partial(lambda l, per, first: (0, per * l + first, 0, 0), per=per, first=first))
                for a, per, first in parts]
    return pl.pallas_call(
        body, grid=(L,), in_specs=in_specs + [wspec] * 3, out_specs=[wspec] * 4,
        out_shape=[jax.ShapeDtypeStruct(w.shape, f32)] * 4,
        compiler_params=_cparams(("arbitrary",)), name=name)(*[a for a, _, _ in parts], w, m, v)


def _sum_sources(stacked):
    rows = stacked.shape[1]

    def body(s_ref, o_ref):
        acc = s_ref[0]
        for s in range(1, NDEV):
            acc = acc + s_ref[s]
        o_ref[...] = acc

    return pl.pallas_call(body, out_shape=jax.ShapeDtypeStruct((rows, 128), f32), name="sum_small_grads")(stacked)


def _adam_small(w, g, m, v):
    def body(w_ref, g_ref, m_ref, v_ref, d_ref, m2_ref, v2_ref):
        d, m2, v2 = _adam_math(w_ref[...], g_ref[...], m_ref[...], v_ref[...])
        d_ref[...] = d
        m2_ref[...] = m2
        v2_ref[...] = v2

    return pl.pallas_call(body, out_shape=[jax.ShapeDtypeStruct(w.shape, f32)] * 3, name="adam_small")(w, g, m, v)


def _pack_rows(arrs):
    flat = []
    for a in arrs:
        a = a.reshape(-1)
        flat.append(jnp.pad(a, (0, (-a.shape[0]) % 1024)))
    return jnp.concatenate(flat).reshape(-1, 128)


def _unpack_rows(packed, shapes):
    out, off = [], 0
    flat = packed.reshape(-1)
    for s in shapes:
        n = math.prod(s)
        out.append(flat[off:off + n].reshape(s))
        off += n + (-n) % 1024
    return out


def _gather_conv(gathered, shape):
    L, K, c = shape
    return jnp.transpose(gathered, (1, 2, 0, 3)).reshape(L, K, NDEV * c)


def kernel(x, p, positions, ln_ffn1_g, ln_ffn1_b, ffn1_w_gate, ffn1_w_up, ffn1_w_down, w_in, ret_norm_g, lru_conv_w, lru_conv_b, lru_w_a, lru_b_a, lru_w_x, lru_b_x, lru_lambda, gdn_conv_w, gdn_a_log, gdn_dt_bias, gdn_norm_g, w_out, ln_mix_g, ln_mix_b, ffn2_w_gate, ffn2_w_up, ffn2_w_down, ple_w_gate, ple_w_proj, ln_ffn2_g, ln_ffn2_b, loss_target, m_ln_ffn1_g, m_ln_ffn1_b, m_ffn1_w_gate, m_ffn1_w_up, m_ffn1_w_down, m_w_in, m_ret_norm_g, m_lru_conv_w, m_lru_conv_b, m_lru_w_a, m_lru_b_a, m_lru_w_x, m_lru_b_x, m_lru_lambda, m_gdn_conv_w, m_gdn_a_log, m_gdn_dt_bias, m_gdn_norm_g, m_w_out, m_ln_mix_g, m_ln_mix_b, m_ffn2_w_gate, m_ffn2_w_up, m_ffn2_w_down, m_ple_w_gate, m_ple_w_proj, m_ln_ffn2_g, m_ln_ffn2_b, v_ln_ffn1_g, v_ln_ffn1_b, v_ffn1_w_gate, v_ffn1_w_up, v_ffn1_w_down, v_w_in, v_ret_norm_g, v_lru_conv_w, v_lru_conv_b, v_lru_w_a, v_lru_b_a, v_lru_w_x, v_lru_b_x, v_lru_lambda, v_gdn_conv_w, v_gdn_a_log, v_gdn_dt_bias, v_gdn_norm_g, v_w_out, v_ln_mix_g, v_ln_mix_b, v_ffn2_w_gate, v_ffn2_w_up, v_ffn2_w_down, v_ple_w_gate, v_ple_w_proj, v_ln_ffn2_g, v_ln_ffn2_b):
    args = locals()
    W = {n: args[n] for n in WEIGHTS}
    M = {n: args['m_' + n] for n in WEIGHTS}
    V = {n: args['v_' + n] for n in WEIGHTS}
    me = 4 * lax.axis_index("x") + 2 * lax.axis_index("y") + lax.axis_index("c")

    packed = _pack_big(W)
    conv_pack = _pack_rows([W[n] for n in CONV_SHARDED])
    keys = list(packed)
    gathered = _exchange([packed[k] for k in keys] + [conv_pack], [False] * (len(keys) + 1), "gather_weights")
    G = dict(zip(keys, gathered[:-1]))
    G['wpp'] = jnp.transpose(G.pop('ppp'), (1, 2, 0, 3)).reshape(DEPTH, PLE, D)
    conv_all = gathered[-1]
    sm = {n: W[n] for n in SMALL}
    conv_shards = [_unpack_rows(conv_all[s], [W[n].shape for n in CONV_SHARDED]) for s in range(NDEV)]
    for i, n in enumerate(CONV_SHARDED):
        sm[n] = _gather_conv(jnp.stack([cs[i] for cs in conv_shards]), W[n].shape)

    loss, grad_x, big, small = _local_step(x[0], p[:, 0], positions.reshape(-1, 1), loss_target[0], G, sm)
    loss = lax.psum(loss[0, 0], ("x", "y", "c"))

    small_pack = _pack_rows([small[n] for n in SMALL])
    bkeys = list(big)
    recv = _exchange([big[k] for k in bkeys] + [small_pack], [True] * len(bkeys) + [False], "scatter_grads")
    R = dict(zip(bkeys, recv[:-1]))
    small_sum = _unpack_rows(_sum_sources(recv[-1]), [small[n].shape for n in SMALL])
    grads, delta, new_m, new_v = {}, {}, {}, {}
    for n, g in zip(SMALL, small_sum):
        if n in CONV_SHARDED:
            c = W[n].shape[2]
            g = lax.dynamic_slice_in_dim(g, me * c, c, axis=2)
        grads[n] = g

    big_parts = {
        'ffn1_w_gate': [(R['p384'], 4, 0)], 'ffn1_w_up': [(R['p384'], 4, 1)],
        'ffn2_w_gate': [(R['p384'], 4, 2)], 'ffn2_w_up': [(R['p384'], 4, 3)],
        'ffn1_w_down': [(R['pd'], 2, 0)], 'ffn2_w_down': [(R['pd'], 2, 1)],
        'w_in': [(R['pr'], 3, 0), (R['pinl'], 1, 0), (R['ping'], 1, 0)],
        'w_out': [(R['pr'], 3, 1)], 'ple_w_gate': [(R['pr'], 3, 2)], 'ple_w_proj': [(R['ppp'], 1, 0)],
    }
    for n in BIG:
        grads[n], delta[n], new_m[n], new_v[n] = _adam_big(big_parts[n], W[n], M[n], V[n], "adam_" + n)
    shapes = [W[n].shape for n in SMALL]
    d_s, m_s, v_s = _adam_small(*[_pack_rows([src[n] for n in SMALL]) for src in (W, grads, M, V)])
    for n, dd, mm, vv in zip(SMALL, _unpack_rows(d_s, shapes), _unpack_rows(m_s, shapes), _unpack_rows(v_s, shapes)):
        delta[n], new_m[n], new_v[n] = dd, mm, vv

    return (loss, grad_x[None], *[grads[n] for n in WEIGHTS], *[delta[n] for n in WEIGHTS],
            *[new_m[n] for n in WEIGHTS], *[new_v[n] for n in WEIGHTS])
```

```python
import functools
import math

import jax
import jax.numpy as jnp
from jax import lax
from jax.experimental import pallas as pl
from jax.experimental.pallas import tpu as pltpu

f32 = jnp.float32
bf16 = jnp.bfloat16

NDEV = 8
DEPTH = 2
D = 1024
FS = 352
FSP = 384
FB = 2
NF = NDEV // FB
PLE = 256
CH = 64
RET_H, GDN_H = 4, 6
RET_W, LRU_W, GDN_W = 256, 384, 384
GDN_IN = 1664
GDN_IN_REAL = 1548
D_IN = 3340
ALPHA = 4.0 ** 0.25
LN_EPS = 1e-5
ROPE_THETA = 10000.0
TM = 512
RB_RET, RB_LRU, RB_GDN = 512, 512, 256
VMEM_LIMIT = 56 * 1024 * 1024
ADAM_LR, ADAM_B1, ADAM_B2, ADAM_EPS, ADAM_WD, ADAM_STEP = 0.001, 0.9, 0.999, 1e-08, 0.01, 10

WEIGHTS = ['ln_ffn1_g', 'ln_ffn1_b', 'ffn1_w_gate', 'ffn1_w_up', 'ffn1_w_down', 'w_in', 'ret_norm_g', 'lru_conv_w',
           'lru_conv_b', 'lru_w_a', 'lru_b_a', 'lru_w_x', 'lru_b_x', 'lru_lambda', 'gdn_conv_w', 'gdn_a_log',
           'gdn_dt_bias', 'gdn_norm_g', 'w_out', 'ln_mix_g', 'ln_mix_b', 'ffn2_w_gate', 'ffn2_w_up', 'ffn2_w_down',
           'ple_w_gate', 'ple_w_proj', 'ln_ffn2_g', 'ln_ffn2_b']
BIG = ['ffn1_w_gate', 'ffn1_w_up', 'ffn1_w_down', 'w_in', 'w_out', 'ffn2_w_gate', 'ffn2_w_up', 'ffn2_w_down',
       'ple_w_gate', 'ple_w_proj']
SMALL = [n for n in WEIGHTS if n not in BIG]
CONV_SHARDED = {'lru_conv_w': LRU_W, 'gdn_conv_w': 3 * GDN_W}


def _cparams(sem=None):
    return pltpu.CompilerParams(dimension_semantics=sem, vmem_limit_bytes=VMEM_LIMIT)


def _sigmoid(x):
    return 1.0 / (1.0 + jnp.exp(-x))


def _silu(x):
    return x * _sigmoid(x)


def _dsilu(x):
    s = _sigmoid(x)
    return s * (1.0 + x * (1.0 - s))


def _softplus(x):
    return jnp.maximum(x, 0.0) + jnp.log(1.0 + jnp.exp(-jnp.abs(x)))


def _gelu(x):
    return 0.5 * x * (1.0 + jnp.tanh(0.7978845608028654 * (x + 0.044715 * x * x * x)))


def _dot(a, b):
    return jnp.dot(a.astype(bf16), b.astype(bf16), preferred_element_type=f32)


def _dot_nt(a, b):
    return lax.dot_general(a.astype(bf16), b.astype(bf16), (((1,), (1,)), ((), ())), preferred_element_type=f32)


def _dot_tn(a, b):
    return lax.dot_general(a.astype(bf16), b.astype(bf16), (((0,), (0,)), ((), ())), preferred_element_type=f32)


def _bmm(eq, a, b):
    return jnp.einsum(eq, a.astype(bf16), b.astype(bf16), preferred_element_type=f32)


def _split3(a):
    a1 = a.astype(bf16)
    r = a - a1.astype(f32)
    a2 = r.astype(bf16)
    return a1, a2, (r - a2.astype(f32)).astype(bf16)


def _bmm3(eq, a, b):
    a1, a2, _ = _split3(a)
    b1, b2, _ = _split3(b)
    e = lambda x, y: jnp.einsum(eq, x, y, preferred_element_type=f32)
    return e(a1, b1) + (e(a1, b2) + e(a2, b1))


def _tri_ones(B, upper=False):
    ii = lax.broadcasted_iota(jnp.int32, (B, CH, CH), 1)
    jj = lax.broadcasted_iota(jnp.int32, (B, CH, CH), 2)
    return jnp.where((ii <= jj) if upper else (ii >= jj), 1.0, 0.0).astype(bf16)


def _cumsum_mm(t, x):
    x1, x2, x3 = _split3(x)
    e = lambda y: jnp.einsum('bij,bjk->bik', t, y, preferred_element_type=f32)
    return e(x1) + (e(x2) + e(x3))


@jax.custom_vjp
def _cumsum_rows(x):
    return _cumsum_mm(_tri_ones(x.shape[0]), x)


def _cumsum_rows_fwd(x):
    return _cumsum_rows(x), None


def _cumsum_rows_bwd(_, g):
    return (_cumsum_mm(_tri_ones(g.shape[0], upper=True), g),)


_cumsum_rows.defvjp(_cumsum_rows_fwd, _cumsum_rows_bwd)


@jax.custom_vjp
def _neumann_inv(m):
    ii = lax.broadcasted_iota(jnp.int32, m.shape, 1)
    jj = lax.broadcasted_iota(jnp.int32, m.shape, 2)
    inv = jnp.where(ii == jj, 1.0, 0.0).astype(f32) + m
    mp = m
    for _ in range(5):
        mp = _bmm3('bij,bjk->bik', mp, mp)
        inv = inv + _bmm3('bij,bjk->bik', inv, mp)
    return inv


def _neumann_inv_fwd(m):
    inv = _neumann_inv(m)
    return inv, inv


def _neumann_inv_bwd(inv, g):
    return (_bmm3('bij,bkj->bik', _bmm3('bji,bjk->bik', inv, g), inv),)


_neumann_inv.defvjp(_neumann_inv_fwd, _neumann_inv_bwd)


def _ln_stats(z):
    mu = jnp.mean(z, -1, keepdims=True)
    zc = z - mu
    rstd = lax.rsqrt(jnp.mean(zc * zc, -1, keepdims=True) + LN_EPS)
    return zc * rstd, rstd


def _ln_bwd(z, g, dout):
    xh, rstd = _ln_stats(z)
    dxh = dout * g
    dz = rstd * (dxh - jnp.mean(dxh, -1, keepdims=True) - xh * jnp.mean(dxh * xh, -1, keepdims=True))
    return dz, jnp.sum(dout * xh, 0, keepdims=True), jnp.sum(dout, 0, keepdims=True)


def _full_spec(shape):
    nd = len(shape)
    return pl.BlockSpec(shape, lambda *_: (0,) * nd)


def _hidden_cols(w_ref):
    return jnp.concatenate([w_ref[j] for j in range(FB)], axis=1)


def _ffn_fwd(x, p384, pd, lg, lb, layer, which, ple=None):
    T = x.shape[0]
    sg, su, sd = 4 * layer + 2 * which, 4 * layer + 2 * which + 1, 2 * layer + which
    has_ple = ple is not None

    def body(*refs):
        if has_ple:
            (x_ref, wg_ref, wu_ref, wd_ref, lg_ref, lb_ref, p_ref, wpg_ref, wpp_ref,
             z_ref, o_ref, g_ref, u_ref, acc, xb_s) = refs
        else:
            x_ref, wg_ref, wu_ref, wd_ref, lg_ref, lb_ref, z_ref, o_ref, g_ref, u_ref, acc, xb_s = refs
        f = pl.program_id(1)

        @pl.when(f == 0)
        def _():
            x = x_ref[...]
            xb = x.astype(bf16)
            xb_s[...] = xb
            base = ALPHA * x
            if has_ple:
                gate = _sigmoid(_dot(xb, wpg_ref[...].reshape(D, D)))
                base = base + gate * _dot(p_ref[...], wpp_ref[...])
            acc[...] = base

        xb = xb_s[...]
        g = _dot(xb, _hidden_cols(wg_ref))
        u = _dot(xb, _hidden_cols(wu_ref))
        g_ref[...] = g.astype(bf16)
        u_ref[...] = u.astype(bf16)
        acc[...] += 0.5 * _dot(_silu(g) * u, wd_ref[...].reshape(FB * FSP, D))

        @pl.when(f == NF - 1)
        def _():
            z = acc[...]
            z_ref[...] = z
            o_ref[...] = _ln_stats(z)[0] * lg_ref[...] + lb_ref[...]

    row = pl.BlockSpec((TM, D), lambda i, f: (i, 0))
    in_specs = [row,
                pl.BlockSpec((FB, None, D, FSP), lambda i, f: (f, sg, 0, 0)),
                pl.BlockSpec((FB, None, D, FSP), lambda i, f: (f, su, 0, 0)),
                pl.BlockSpec((FB, None, FSP, D), lambda i, f: (f, sd, 0, 0)),
                _full_spec((1, D)), _full_spec((1, D))]
    args = [x, p384, p384, pd, lg, lb]
    if has_ple:
        p, pr, wpp = ple
        in_specs += [pl.BlockSpec((TM, PLE), lambda i, f: (i, 0)),
                     pl.BlockSpec((NDEV, None, 128, D), lambda i, f: (0, 3 * layer + 2, 0, 0)),
                     _full_spec((PLE, D))]
        args += [p, pr, wpp]
    hid = pl.BlockSpec((TM, FB * FSP), lambda i, f: (i, f))
    hshape = jax.ShapeDtypeStruct((T, NDEV * FSP), bf16)
    return pl.pallas_call(
        body, grid=(T // TM, NF), in_specs=in_specs, out_specs=[row, row, hid, hid],
        out_shape=[jax.ShapeDtypeStruct((T, D), f32)] * 2 + [hshape, hshape],
        scratch_shapes=[pltpu.VMEM((TM, D), f32), pltpu.VMEM((TM, D), bf16)],
        compiler_params=_cparams(("arbitrary", "arbitrary")), name=f"ffn{which + 1}_fwd")(*args)


def _ffn_bwd(x, z, dout, gs, us, p384, pd, lg, layer, which, ple=None):
    T = z.shape[0]
    TMB = TM // 2
    sg, su, sd = 4 * layer + 2 * which, 4 * layer + 2 * which + 1, 2 * layer + which
    has_ple = ple is not None

    def body(*refs):
        if has_ple:
            (z_ref, do_ref, g_ref, u_ref, wg_ref, wu_ref, wd_ref, lg_ref, x_ref, p_ref, wpg_ref, wpp_ref,
             dx_ref, dg_ref, du_ref, a_ref, dy_ref, dlg_ref, dlb_ref, dgp_ref, dpj_ref, acc, dyb) = refs
        else:
            (z_ref, do_ref, g_ref, u_ref, wg_ref, wu_ref, wd_ref, lg_ref,
             dx_ref, dg_ref, du_ref, a_ref, dy_ref, dlg_ref, dlb_ref, acc, dyb) = refs
        i, f = pl.program_id(0), pl.program_id(1)

        @pl.when(jnp.logical_and(i == 0, f == 0))
        def _():
            dlg_ref[...] = jnp.zeros_like(dlg_ref)
            dlb_ref[...] = jnp.zeros_like(dlb_ref)

        @pl.when(f == 0)
        def _():
            dz, dlg, dlb = _ln_bwd(z_ref[...], lg_ref[...], do_ref[...])
            dlg_ref[...] += dlg
            dlb_ref[...] += dlb
            dy = (0.5 * dz).astype(bf16)
            dyb[...] = dy
            dy_ref[...] = dy
            dx = ALPHA * dz
            if has_ple:
                wpg = wpg_ref[...].reshape(D, D)
                gate = _sigmoid(_dot(x_ref[...], wpg))
                proj = _dot(p_ref[...], wpp_ref[...])
                dgp = (dz * proj * gate * (1.0 - gate)).astype(bf16)
                dgp_ref[...] = dgp
                dpj_ref[...] = (dz * gate).astype(bf16)
                dx = dx + _dot_nt(dgp, wpg)
            acc[...] = dx

        g = g_ref[...].astype(f32)
        u = u_ref[...].astype(f32)
        da = _dot_nt(dyb[...], wd_ref[...].reshape(FB * FSP, D))
        sgm = _sigmoid(g)
        dg = (da * u * (sgm * (1.0 + g * (1.0 - sgm)))).astype(bf16)
        du = (da * (g * sgm)).astype(bf16)
        dg_ref[...] = dg
        du_ref[...] = du
        a_ref[...] = (g * sgm * u).astype(bf16)
        acc[...] += _dot_nt(dg, _hidden_cols(wg_ref)) + _dot_nt(du, _hidden_cols(wu_ref))

        @pl.when(f == NF - 1)
        def _():
            dx_ref[...] = acc[...]

    row = pl.BlockSpec((TMB, D), lambda i, f: (i, 0))
    hid = pl.BlockSpec((TMB, FB * FSP), lambda i, f: (i, f))
    vec = _full_spec((1, D))
    in_specs = [row, row, hid, hid,
                pl.BlockSpec((FB, None, D, FSP), lambda i, f: (f, sg, 0, 0)),
                pl.BlockSpec((FB, None, D, FSP), lambda i, f: (f, su, 0, 0)),
                pl.BlockSpec((FB, None, FSP, D), lambda i, f: (f, sd, 0, 0)),
                vec]
    args = [z, dout, gs, us, p384, p384, pd, lg]
    out_specs = [row, hid, hid, hid, row, vec, vec]
    hshape = jax.ShapeDtypeStruct((T, NDEV * FSP), bf16)
    out_shape = [jax.ShapeDtypeStruct((T, D), f32), hshape, hshape, hshape, jax.ShapeDtypeStruct((T, D), bf16),
                 jax.ShapeDtypeStruct((1, D), f32), jax.ShapeDtypeStruct((1, D), f32)]
    if has_ple:
        p, pr, wpp = ple
        in_specs += [row, pl.BlockSpec((TMB, PLE), lambda i, f: (i, 0)),
                     pl.BlockSpec((NDEV, None, 128, D), lambda i, f: (0, 3 * layer + 2, 0, 0)),
                     _full_spec((PLE, D))]
        args += [x, p, pr, wpp]
        out_specs += [row, row]
        out_shape += [jax.ShapeDtypeStruct((T, D), bf16)] * 2
    return pl.pallas_call(
        body, grid=(T // TMB, NF), in_specs=in_specs, out_specs=out_specs, out_shape=out_shape,
        scratch_shapes=[pltpu.VMEM((TMB, D), f32), pltpu.VMEM((TMB, D), bf16)],
        compiler_params=_cparams(("arbitrary", "arbitrary")), name=f"ffn{which + 1}_bwd")(*args)


def _matmul_tn(a, b, nb, name, nsub=1):
    T, M = a.shape
    N = b.shape[1]
    tk = min(T, 1024)
    nk = T // tk
    wide = nsub * nb

    def body(a_ref, b_ref, o_ref, acc):
        k = pl.program_id(1)

        @pl.when(k == 0)
        def _():
            acc[...] = jnp.zeros_like(acc)

        acc[...] += _dot_tn(a_ref[...], b_ref[...])

        @pl.when(k == nk - 1)
        def _():
            for j in range(nsub):
                o_ref[j] = acc[:, j * nb:(j + 1) * nb].astype(bf16)

    return pl.pallas_call(
        body, grid=(N // wide, nk),
        in_specs=[pl.BlockSpec((tk, M), lambda n, k: (k, 0)), pl.BlockSpec((tk, wide), lambda n, k: (k, n))],
        out_specs=pl.BlockSpec((nsub, M, nb), lambda n, k: (n, 0, 0)),
        out_shape=jax.ShapeDtypeStruct((N // nb, M, nb), bf16),
        scratch_shapes=[pltpu.VMEM((M, wide), f32)],
        compiler_params=_cparams(("arbitrary", "arbitrary")), name=name)(a, b)


def _proj_in(x, pr, pinl, ping, layer):
    T = x.shape[0]

    def body(x_ref, wr_ref, wl_ref, wg_ref, hr_ref, hl_ref, hg_ref):
        xb = x_ref[...].astype(bf16)
        hr_ref[...] = _dot(xb, wr_ref[...].reshape(D, D))
        hl_ref[...] = _dot(xb, wl_ref[...].reshape(D, 2 * LRU_W))
        hg_ref[...] = _dot(xb, wg_ref[...].reshape(D, GDN_IN))

    return pl.pallas_call(
        body, grid=(T // TM,),
        in_specs=[pl.BlockSpec((TM, D), lambda i: (i, 0)),
                  pl.BlockSpec((NDEV, None, 128, D), lambda i: (0, 3 * layer, 0, 0)),
                  pl.BlockSpec((NDEV, None, 128, 2 * LRU_W), lambda i: (0, layer, 0, 0)),
                  pl.BlockSpec((NDEV, None, 128, GDN_IN), lambda i: (0, layer, 0, 0))],
        out_specs=[pl.BlockSpec((TM, D), lambda i: (i, 0)), pl.BlockSpec((TM, 2 * LRU_W), lambda i: (i, 0)),
                   pl.BlockSpec((TM, GDN_IN), lambda i: (i, 0))],
        out_shape=[jax.ShapeDtypeStruct((T, D), f32), jax.ShapeDtypeStruct((T, 2 * LRU_W), f32),
                   jax.ShapeDtypeStruct((T, GDN_IN), f32)],
        compiler_params=_cparams(("arbitrary",)), name="proj_in")(x, pr, pinl, ping)


def _proj_in_bwd(base, dhr, dhl, dhg, pr, pinl, ping, layer):
    T = base.shape[0]

    def body(b_ref, dr_ref, dl_ref, dg_ref, wr_ref, wl_ref, wg_ref, o_ref):
        o_ref[...] = (b_ref[...] + _dot_nt(dr_ref[...], wr_ref[...].reshape(D, D))
                      + _dot_nt(dl_ref[...], wl_ref[...].reshape(D, 2 * LRU_W))
                      + _dot_nt(dg_ref[...], wg_ref[...].reshape(D, GDN_IN)))

    return pl.pallas_call(
        body, grid=(T // TM,),
        in_specs=[pl.BlockSpec((TM, D), lambda i: (i, 0)), pl.BlockSpec((TM, D), lambda i: (i, 0)),
                  pl.BlockSpec((TM, 2 * LRU_W), lambda i: (i, 0)), pl.BlockSpec((TM, GDN_IN), lambda i: (i, 0)),
                  pl.BlockSpec((NDEV, None, 128, D), lambda i: (0, 3 * layer, 0, 0)),
                  pl.BlockSpec((NDEV, None, 128, 2 * LRU_W), lambda i: (0, layer, 0, 0)),
                  pl.BlockSpec((NDEV, None, 128, GDN_IN), lambda i: (0, layer, 0, 0))],
        out_specs=pl.BlockSpec((TM, D), lambda i: (i, 0)),
        out_shape=jax.ShapeDtypeStruct((T, D), f32),
        compiler_params=_cparams(("arbitrary",)), name="proj_in_bwd")(base, dhr, dhl, dhg, pr, pinl, ping)


def _mix_out(x1, o_r, o_l, o_g, pr, lg, lb, layer):
    T = x1.shape[0]

    def body(x_ref, r_ref, l_ref, g_ref, w_ref, lg_ref, lb_ref, z_ref, o_ref):
        w = w_ref[...].reshape(D, D)
        z = (ALPHA * x_ref[...] + _dot(r_ref[...], w[0:RET_W]) + _dot(l_ref[...], w[RET_W:RET_W + LRU_W])
             + _dot(g_ref[...], w[RET_W + LRU_W:D]))
        z_ref[...] = z
        o_ref[...] = _ln_stats(z)[0] * lg_ref[...] + lb_ref[...]

    row = pl.BlockSpec((TM, D), lambda i: (i, 0))
    return pl.pallas_call(
        body, grid=(T // TM,),
        in_specs=[row, pl.BlockSpec((TM, RET_W), lambda i: (i, 0)), pl.BlockSpec((TM, LRU_W), lambda i: (i, 0)),
                  pl.BlockSpec((TM, GDN_W), lambda i: (i, 0)),
                  pl.BlockSpec((NDEV, None, 128, D), lambda i: (0, 3 * layer + 1, 0, 0)),
                  _full_spec((1, D)), _full_spec((1, D))],
        out_specs=[row, row], out_shape=[jax.ShapeDtypeStruct((T, D), f32)] * 2,
        compiler_params=_cparams(("arbitrary",)), name="mix_out")(x1, o_r, o_l, o_g, pr, lg, lb)


def _mix_out_bwd(z, dout, pr, lg, layer):
    T = z.shape[0]

    def body(z_ref, do_ref, w_ref, lg_ref, dxb_ref, dzb_ref, dr_ref, dl_ref, dg_ref, dlg_ref, dlb_ref):
        @pl.when(pl.program_id(0) == 0)
        def _():
            dlg_ref[...] = jnp.zeros_like(dlg_ref)
            dlb_ref[...] = jnp.zeros_like(dlb_ref)

        dz, dlg, dlb = _ln_bwd(z_ref[...], lg_ref[...], do_ref[...])
        dlg_ref[...] += dlg
        dlb_ref[...] += dlb
        dxb_ref[...] = ALPHA * dz
        dzb = dz.astype(bf16)
        dzb_ref[...] = dzb
        w = w_ref[...].reshape(D, D)
        dr_ref[...] = _dot_nt(dzb, w[0:RET_W])
        dl_ref[...] = _dot_nt(dzb, w[RET_W:RET_W + LRU_W])
        dg_ref[...] = _dot_nt(dzb, w[RET_W + LRU_W:D])

    row = pl.BlockSpec((TM, D), lambda i: (i, 0))
    vec = _full_spec((1, D))
    return pl.pallas_call(
        body, grid=(T // TM,),
        in_specs=[row, row, pl.BlockSpec((NDEV, None, 128, D), lambda i: (0, 3 * layer + 1, 0, 0)), vec],
        out_specs=[row, row, pl.BlockSpec((TM, RET_W), lambda i: (i, 0)), pl.BlockSpec((TM, LRU_W), lambda i: (i, 0)),
                   pl.BlockSpec((TM, GDN_W), lambda i: (i, 0)), vec, vec],
        out_shape=[jax.ShapeDtypeStruct((T, D), f32), jax.ShapeDtypeStruct((T, D), bf16),
                   jax.ShapeDtypeStruct((T, RET_W), f32), jax.ShapeDtypeStruct((T, LRU_W), f32),
                   jax.ShapeDtypeStruct((T, GDN_W), f32), jax.ShapeDtypeStruct((1, D), f32),
                   jax.ShapeDtypeStruct((1, D), f32)],
        compiler_params=_cparams(("arbitrary",)), name="mix_out_bwd")(z, dout, pr, lg)


def _loss_grad(y, target):
    T = y.shape[0]

    def body(y_ref, t_ref, dy_ref, l_ref):
        @pl.when(pl.program_id(0) == 0)
        def _():
            l_ref[...] = jnp.zeros_like(l_ref)

        e = y_ref[...] - t_ref[...]
        dy_ref[...] = e * (1.0 / D)
        l_ref[...] += 0.5 * jnp.sum(jnp.sum(e * e, -1, keepdims=True) * (1.0 / D), 0, keepdims=True)

    row = pl.BlockSpec((TM, D), lambda i: (i, 0))
    return pl.pallas_call(
        body, grid=(T // TM,), in_specs=[row, row], out_specs=[row, _full_spec((1, 1))],
        out_shape=[jax.ShapeDtypeStruct((T, D), f32), jax.ShapeDtypeStruct((1, 1), f32)],
        compiler_params=_cparams(("arbitrary",)), name="loss_grad")(y, target)


def _split_heads(x, H):
    n = x.shape[0] // CH
    parts = [x[:, h * CH:(h + 1) * CH].reshape(n, CH, CH) for h in range(H)]
    return jnp.stack(parts, axis=1).reshape(n * H, CH, CH)


def _merge_heads(ref, x, H, col0=0):
    n = x.shape[0] // H
    x4 = x.reshape(n, H, CH, CH)
    for h in range(H):
        ref[:, col0 + h * CH:col0 + (h + 1) * CH] = x4[:, h].reshape(n * CH, CH)


def _conv_fwd(ext, x, tail, w, R):
    ext[0:8, :] = tail
    ext[8:R + 8, :] = x
    y = w[3:4, :] * x
    for k in range(3):
        y = y + w[k:k + 1, :] * ext[5 + k:5 + k + R, :]
    return y


def _conv_bwd(ext, ext2, dy, dy_next, w, R):
    ext2[0:R, :] = dy
    ext2[R:R + 8, :] = dy_next
    dx = w[3:4, :] * dy
    dws = []
    for k in range(3):
        dx = dx + w[k:k + 1, :] * ext2[3 - k:3 - k + R, :]
        dws.append(jnp.sum(dy * ext[5 + k:5 + k + R, :], 0, keepdims=True))
    dws.append(jnp.sum(dy * ext[8:8 + R, :], 0, keepdims=True))
    return dx, jnp.concatenate(dws, axis=0)


def _prev_tail_spec(R, W):
    return pl.BlockSpec((8, W), lambda i: (jnp.maximum(i * (R // 8) - 1, 0), 0))


def _prev_tail_spec_rev(R, W, nb):
    return pl.BlockSpec((8, W), lambda i: (jnp.maximum((nb - 1 - i) * (R // 8) - 1, 0), 0))


def _rope_tables(positions):
    T = positions.shape[0]

    def body(p_ref, c_ref, s_ref):
        lane = lax.broadcasted_iota(jnp.int32, (TM, RET_W), 1)
        fi = (lane % 32).astype(f32)
        inv = jnp.exp(fi * (-math.log(ROPE_THETA) / 32.0))
        ang = p_ref[...].astype(f32) * inv
        c_ref[...] = jnp.cos(ang)
        s_ref[...] = jnp.where(lane % CH < 32, -jnp.sin(ang), jnp.sin(ang))

    row = pl.BlockSpec((TM, RET_W), lambda i: (i, 0))
    return pl.pallas_call(
        body, grid=(T // TM,), in_specs=[pl.BlockSpec((TM, 1), lambda i: (i, 0))], out_specs=[row, row],
        out_shape=[jax.ShapeDtypeStruct((T, RET_W), f32)] * 2,
        compiler_params=_cparams(("arbitrary",)), name="rope_tables")(positions)


def _partner(x):
    lane = lax.broadcasted_iota(jnp.int32, x.shape, 1)
    return jnp.where(lane % CH < 32, pltpu.roll(x, RET_W - 32, 1), pltpu.roll(x, 32, 1))


def _ret_consts():
    ii = lax.broadcasted_iota(jnp.int32, (CH, CH), 0).astype(f32)
    jj = lax.broadcasted_iota(jnp.int32, (CH, CH), 1).astype(f32)
    intra, cross, tail, cd = [], [], [], []
    for h in range(RET_H):
        lg = math.log1p(-(2.0 ** (-5.0 - h)))
        intra.append(jnp.exp(jnp.abs(ii - jj) * lg))
        cross.append(jnp.exp((ii + 1.0) * lg))
        tail.append(jnp.exp((CH - 1.0 - ii) * lg))
        cd.append(jnp.full((CH, CH), math.exp(CH * lg), f32))
    return jnp.stack(intra), jnp.stack(cross), jnp.stack(tail), jnp.stack(cd)


def _ret_chunk(consts, q, k, v, st):
    intra, cross, tail, cd = consts
    s = _bmm('hid,hjd->hij', q, k) * intra
    o = _bmm('hij,hje->hie', s, v) + _bmm('hid,hde->hie', q * cross, st)
    st2 = st * cd + _bmm('hjd,hje->hde', k * tail, v)
    mu = jnp.mean(o, -1, keepdims=True)
    oc = o - mu
    on = oc * lax.rsqrt(jnp.mean(oc * oc, -1, keepdims=True) + 1e-5)
    return on, st2


def _ret_fwd(hr, cosw, sinw, gam):
    T = hr.shape[0]
    R = RB_RET
    nc = R // CH

    def body(h_ref, c_ref, s_ref, g_ref, o_ref, st_ref, st, wide):
        @pl.when(pl.program_id(0) == 0)
        def _():
            st[...] = jnp.zeros_like(st)

        consts = _ret_consts()
        cw, sw = c_ref[...], s_ref[...]
        q, k = h_ref[:, 0:RET_W], h_ref[:, RET_W:2 * RET_W]
        qh = _split_heads((q * cw + _partner(q) * sw) * 0.125, RET_H)
        kh = _split_heads(k * cw + _partner(k) * sw, RET_H)
        vh = _split_heads(h_ref[:, 2 * RET_W:3 * RET_W], RET_H)
        outs = []
        s_cur = st[...]
        for c in range(nc):
            sl = slice(c * RET_H, (c + 1) * RET_H)
            st_ref[c] = s_cur
            on, s_cur = _ret_chunk(consts, qh[sl], kh[sl], vh[sl], s_cur)
            outs.append(on)
        st[...] = s_cur
        _merge_heads(wide, jnp.concatenate(outs, axis=0), RET_H)
        o_ref[...] = wide[...] * g_ref[...] * _silu(h_ref[:, 3 * RET_W:4 * RET_W])

    blk = pl.BlockSpec((R, RET_W), lambda i: (i, 0))
    return pl.pallas_call(
        body, grid=(T // R,),
        in_specs=[pl.BlockSpec((R, D), lambda i: (i, 0)), blk, blk, _full_spec((1, RET_W))],
        out_specs=[blk, pl.BlockSpec((nc, RET_H, CH, CH), lambda i: (i, 0, 0, 0))],
        out_shape=[jax.ShapeDtypeStruct((T, RET_W), f32), jax.ShapeDtypeStruct((T // CH, RET_H, CH, CH), f32)],
        scratch_shapes=[pltpu.VMEM((RET_H, CH, CH), f32), pltpu.VMEM((R, RET_W), f32)],
        compiler_params=_cparams(("arbitrary",)), name="ret_fwd")(hr, cosw, sinw, gam)


def _ret_bwd(hr, cosw, sinw, gam, states, dout):
    T = hr.shape[0]
    R = RB_RET
    nc = R // CH
    nb = T // R

    def body(h_ref, c_ref, s_ref, g_ref, st_ref, do_ref, dh_ref, dgam_ref, dst, wide):
        @pl.when(pl.program_id(0) == 0)
        def _():
            dst[...] = jnp.zeros_like(dst)
            dgam_ref[...] = jnp.zeros_like(dgam_ref)

        consts = _ret_consts()
        cw, sw = c_ref[...], s_ref[...]
        q, k = h_ref[:, 0:RET_W], h_ref[:, RET_W:2 * RET_W]
        gr = h_ref[:, 3 * RET_W:4 * RET_W]
        qh = _split_heads((q * cw + _partner(q) * sw) * 0.125, RET_H)
        kh = _split_heads(k * cw + _partner(k) * sw, RET_H)
        vh = _split_heads(h_ref[:, 2 * RET_W:3 * RET_W], RET_H)
        do = do_ref[...]
        gam = g_ref[...]
        sg = _silu(gr)
        don = _split_heads(do * gam * sg, RET_H)
        ons, dqs, dks, dvs = [None] * nc, [None] * nc, [None] * nc, [None] * nc
        ds = dst[...]
        for c in reversed(range(nc)):
            sl = slice(c * RET_H, (c + 1) * RET_H)
            (on, _), vjp = jax.vjp(functools.partial(_ret_chunk, consts), qh[sl], kh[sl], vh[sl], st_ref[c])
            dqs[c], dks[c], dvs[c], ds = vjp((don[sl], ds))
            ons[c] = on
        dst[...] = ds
        _merge_heads(wide, jnp.concatenate(ons, axis=0), RET_H)
        onw = wide[...]
        dgam_ref[...] += jnp.sum(do * onw * sg, 0, keepdims=True)
        dh_ref[:, 3 * RET_W:4 * RET_W] = do * onw * gam * _dsilu(gr)
        _merge_heads(wide, jnp.concatenate(dqs, axis=0), RET_H)
        u = wide[...] * 0.125
        dh_ref[:, 0:RET_W] = u * cw + _partner(u * sw)
        _merge_heads(wide, jnp.concatenate(dks, axis=0), RET_H)
        u = wide[...]
        dh_ref[:, RET_W:2 * RET_W] = u * cw + _partner(u * sw)
        _merge_heads(dh_ref, jnp.concatenate(dvs, axis=0), RET_H, col0=2 * RET_W)

    blk = pl.BlockSpec((R, RET_W), lambda i: (nb - 1 - i, 0))
    return pl.pallas_call(
        body, grid=(nb,),
        in_specs=[pl.BlockSpec((R, D), lambda i: (nb - 1 - i, 0)), blk, blk, _full_spec((1, RET_W)),
                  pl.BlockSpec((nc, RET_H, CH, CH), lambda i: (nb - 1 - i, 0, 0, 0)), blk],
        out_specs=[pl.BlockSpec((R, D), lambda i: (nb - 1 - i, 0)), _full_spec((1, RET_W))],
        out_shape=[jax.ShapeDtypeStruct((T, D), f32), jax.ShapeDtypeStruct((1, RET_W), f32)],
        scratch_shapes=[pltpu.VMEM((RET_H, CH, CH), f32), pltpu.VMEM((R, RET_W), f32)],
        compiler_params=_cparams(("arbitrary",)), name="ret_bwd")(hr, cosw, sinw, gam, states, dout)


def _lru_ab(xc, wa, ba, wx, bx, lam):
    r = _sigmoid(_dot(xc, wa) + ba)
    i = _sigmoid(_dot(xc, wx) + bx)
    la = 8.0 * r * (-_softplus(-lam))
    a = jnp.exp(la)
    em = jnp.tanh(la) * (jnp.exp(2.0 * la) + 1.0)
    return a, jnp.sqrt(-em) * (i * xc)


def _lru_out(h, gate):
    return h * _gelu(gate)


def _scan_fwd(a, b):
    R = a.shape[0]
    row = lax.broadcasted_iota(jnp.int32, a.shape, 0)
    d = 1
    while d < R:
        m = row >= d
        b = jnp.where(m, a * pltpu.roll(b, d, 0) + b, b)
        a = jnp.where(m, a * pltpu.roll(a, d, 0), a)
        d *= 2
    return a, b


def _scan_bwd(a, b):
    R = a.shape[0]
    row = lax.broadcasted_iota(jnp.int32, a.shape, 0)
    d = 1
    while d < R:
        m = row < R - d
        b = jnp.where(m, a * pltpu.roll(b, R - d, 0) + b, b)
        a = jnp.where(m, a * pltpu.roll(a, R - d, 0), a)
        d *= 2
    return b


def _lru_fwd(hl, cw, cb, wa, ba, wx, bx, lam):
    T = hl.shape[0]
    R = RB_LRU
    W = LRU_W

    def body(h_ref, t_ref, cw_ref, cb_ref, wa_ref, ba_ref, wx_ref, bx_ref, lam_ref, o_ref, hs_ref, carry, ext):
        first = pl.program_id(0) == 0

        @pl.when(first)
        def _():
            carry[...] = jnp.zeros_like(carry)

        tail = jnp.where(first, 0.0, t_ref[:, 0:W])
        xc = _conv_fwd(ext, h_ref[:, 0:W], tail, cw_ref[...], R) + cb_ref[...]
        a, b = _lru_ab(xc, wa_ref[...], ba_ref[...], wx_ref[...], bx_ref[...], lam_ref[...])
        ap, hloc = _scan_fwd(a, b)
        h = hloc + ap * carry[0:1, :]
        carry[...] = jnp.broadcast_to(h[R - 1:R, :], carry.shape)
        hs_ref[...] = h
        o_ref[...] = _lru_out(h, h_ref[:, W:2 * W])

    vec = _full_spec((1, W))
    blk = pl.BlockSpec((R, W), lambda i: (i, 0))
    return pl.pallas_call(
        body, grid=(T // R,),
        in_specs=[pl.BlockSpec((R, 2 * W), lambda i: (i, 0)), _prev_tail_spec(R, 2 * W), _full_spec((4, W)), vec,
                  _full_spec((W, W)), vec, _full_spec((W, W)), vec, vec],
        out_specs=[blk, blk], out_shape=[jax.ShapeDtypeStruct((T, W), f32)] * 2,
        scratch_shapes=[pltpu.VMEM((8, W), f32), pltpu.VMEM((R + 8, W), f32)],
        compiler_params=_cparams(("arbitrary",)), name="lru_fwd")(hl, hl, cw, cb, wa, ba, wx, bx, lam)


def _lru_bwd(hl, hs, cw, cb, wa, ba, wx, bx, lam, dout):
    T = hl.shape[0]
    R = RB_LRU
    W = LRU_W
    nb = T // R

    def body(h_ref, t_ref, hs_ref, hst_ref, cw_ref, cb_ref, wa_ref, ba_ref, wx_ref, bx_ref, lam_ref, do_ref,
             dh_ref, dcw_ref, dcb_ref, dwa_ref, dba_ref, dwx_ref, dbx_ref, dlam_ref, carry_g, carry_dy, ext, ext2):
        i = pl.program_id(0)
        last_blk = i == 0
        first_blk = i == nb - 1

        @pl.when(last_blk)
        def _():
            carry_g[...] = jnp.zeros_like(carry_g)
            carry_dy[...] = jnp.zeros_like(carry_dy)
            for r in (dcw_ref, dcb_ref, dwa_ref, dba_ref, dwx_ref, dbx_ref, dlam_ref):
                r[...] = jnp.zeros_like(r)

        tail = jnp.where(first_blk, 0.0, t_ref[:, 0:W])
        xc = _conv_fwd(ext, h_ref[:, 0:W], tail, cw_ref[...], R) + cb_ref[...]
        (a, _), vjp_ab = jax.vjp(_lru_ab, xc, wa_ref[...], ba_ref[...], wx_ref[...], bx_ref[...], lam_ref[...])
        hs = hs_ref[...]
        _, vjp_out = jax.vjp(_lru_out, hs, h_ref[:, W:2 * W])
        dh, dgate = vjp_out(do_ref[...])
        row = lax.broadcasted_iota(jnp.int32, (R, W), 0)
        dh = jnp.where(row == R - 1, dh + carry_g[0:1, :], dh)
        a_up = jnp.where(row == R - 1, 0.0, pltpu.roll(a, R - 1, 0))
        g = _scan_bwd(a_up, dh)
        carry_g[...] = jnp.broadcast_to(a[0:1, :] * g[0:1, :], carry_g.shape)
        hprev0 = jnp.where(first_blk, 0.0, hst_ref[7:8, :])
        hprev = jnp.where(row == 0, hprev0, pltpu.roll(hs, 1, 0))
        dxc, dwa, dba, dwx, dbx, dlam = vjp_ab((g * hprev, g))
        dwa_ref[...] += dwa
        dba_ref[...] += dba
        dwx_ref[...] += dwx
        dbx_ref[...] += dbx
        dlam_ref[...] += dlam
        dcb_ref[...] += jnp.sum(dxc, 0, keepdims=True)
        dx, dcw = _conv_bwd(ext, ext2, dxc, carry_dy[...], cw_ref[...], R)
        carry_dy[...] = dxc[0:8, :]
        dcw_ref[...] += dcw
        dh_ref[:, 0:W] = dx
        dh_ref[:, W:2 * W] = dgate

    vec = _full_spec((1, W))
    mat = _full_spec((W, W))
    blk = pl.BlockSpec((R, W), lambda i: (nb - 1 - i, 0))
    blk2 = pl.BlockSpec((R, 2 * W), lambda i: (nb - 1 - i, 0))
    return pl.pallas_call(
        body, grid=(nb,),
        in_specs=[blk2, _prev_tail_spec_rev(R, 2 * W, nb), blk, _prev_tail_spec_rev(R, W, nb), _full_spec((4, W)), vec,
                  mat, vec, mat, vec, vec, blk],
        out_specs=[blk2, _full_spec((4, W)), vec, mat, vec, mat, vec, vec],
        out_shape=[jax.ShapeDtypeStruct((T, 2 * W), f32), jax.ShapeDtypeStruct((4, W), f32),
                   jax.ShapeDtypeStruct((1, W), f32), jax.ShapeDtypeStruct((W, W), f32),
                   jax.ShapeDtypeStruct((1, W), f32), jax.ShapeDtypeStruct((W, W), f32),
                   jax.ShapeDtypeStruct((1, W), f32), jax.ShapeDtypeStruct((1, W), f32)],
        scratch_shapes=[pltpu.VMEM((8, W), f32), pltpu.VMEM((8, W), f32), pltpu.VMEM((R + 8, W), f32),
                        pltpu.VMEM((R + 8, W), f32)],
        compiler_params=_cparams(("arbitrary",)), name="lru_bwd")(hl, hl, hs, hs, cw, cb, wa, ba, wx, bx, lam, dout)


def _gdn_local(qs, ks, vs, gb, bb):
    B = qs.shape[0]
    ii = lax.broadcasted_iota(jnp.int32, (B, CH, CH), 1)
    jj = lax.broadcasted_iota(jnp.int32, (B, CH, CH), 2)
    q = qs * lax.rsqrt(jnp.sum(qs * qs, -1, keepdims=True) + 1e-6)
    k = ks * lax.rsqrt(jnp.sum(ks * ks, -1, keepdims=True) + 1e-6)
    gc = _cumsum_rows(gb)
    gct = jnp.swapaxes(gc, 1, 2)
    decay = jnp.where(ii >= jj, jnp.exp(jnp.minimum(gc - gct, 0.0)), 0.0)
    kk = _bmm('bid,bjd->bij', k, k)
    inv = _neumann_inv(-jnp.where(ii > jj, bb * kk * decay, 0.0))
    egc = jnp.exp(gc)
    u = _bmm3('bij,bje->bie', inv, vs * bb)
    w = _bmm3('bij,bje->bie', inv, k * (bb * egc))
    qk = _bmm('bid,bjd->bij', q, k) * (0.125 * decay)
    glast = gc[:, CH - 1:CH, :]
    return u, w, qk, q * (0.125 * egc), k * jnp.exp(glast - gc), jnp.exp(jnp.broadcast_to(glast, gc.shape))


def _gdn_step(st, u, w, qk, qd, kt, egl, z, gn):
    vnew = u - _bmm('hcd,hde->hce', w, st)
    o = _bmm('hcd,hde->hce', qd, st) + _bmm('hij,hje->hie', qk, vnew)
    st2 = st * egl + _bmm('hcd,hce->hde', kt, vnew)
    out = o * lax.rsqrt(jnp.mean(o * o, -1, keepdims=True) + 1e-6) * gn * _silu(z)
    return out, st2


def _gdn_scalars(ab, alog, dtb):
    sp = _softplus(ab + dtb)
    return -jnp.exp(alog) * sp, _sigmoid(ab)


def _bcast_heads(blk, lane0, H):
    R = blk.shape[0]
    n = R // CH
    parts = [jnp.broadcast_to(blk[:, lane0 + h:lane0 + h + 1], (R, CH)).reshape(n, CH, CH) for h in range(H)]
    return jnp.stack(parts, axis=1).reshape(n * H, CH, CH)


def _unbcast_heads(x, lane0, H):
    n = x.shape[0] // H
    R = n * CH
    s = jnp.sum(x, axis=2, keepdims=True).reshape(n, H, CH, 1)
    lane = lax.broadcasted_iota(jnp.int32, (R, 128), 1)
    acc = jnp.zeros((R, 128), f32)
    for h in range(H):
        acc = acc + jnp.where(lane == lane0 + h, jnp.broadcast_to(s[:, h].reshape(R, 1), (R, 128)), 0.0)
    return acc


def _gdn_fwd(hg, cw, alog, dtb, gn):
    T = hg.shape[0]
    R = RB_GDN
    nc = R // CH
    W3 = 3 * GDN_W
    H = GDN_H

    def body(h_ref, t_ref, cw_ref, al_ref, dt_ref, gn_ref, o_ref, st_ref, st, ext):
        first = pl.program_id(0) == 0

        @pl.when(first)
        def _():
            st[...] = jnp.zeros_like(st)

        tail = jnp.where(first, 0.0, t_ref[:, 0:W3])
        y = _silu(_conv_fwd(ext, h_ref[:, 0:W3], tail, cw_ref[...], R))
        qs, ks, vs = (_split_heads(y[:, j * GDN_W:(j + 1) * GDN_W], H) for j in range(3))
        zh = _split_heads(h_ref[:, W3:W3 + GDN_W], H)
        g, beta = _gdn_scalars(h_ref[:, W3 + GDN_W:GDN_IN], al_ref[...], dt_ref[...])
        loc = _gdn_local(qs, ks, vs, _bcast_heads(g, 0, H), _bcast_heads(beta, H, H))
        gnv = gn_ref[...]
        outs = []
        s_cur = st[...]
        for c in range(nc):
            sl = slice(c * H, (c + 1) * H)
            st_ref[c] = s_cur
            out, s_cur = _gdn_step(s_cur, *(t[sl] for t in loc), zh[sl], gnv)
            outs.append(out)
        st[...] = s_cur
        _merge_heads(o_ref, jnp.concatenate(outs, axis=0), H)

    return pl.pallas_call(
        body, grid=(T // R,),
        in_specs=[pl.BlockSpec((R, GDN_IN), lambda i: (i, 0)), _prev_tail_spec(R, GDN_IN), _full_spec((4, W3)),
                  _full_spec((1, 128)), _full_spec((1, 128)), _full_spec((1, CH))],
        out_specs=[pl.BlockSpec((R, GDN_W), lambda i: (i, 0)), pl.BlockSpec((nc, H, CH, CH), lambda i: (i, 0, 0, 0))],
        out_shape=[jax.ShapeDtypeStruct((T, GDN_W), f32), jax.ShapeDtypeStruct((T // CH, H, CH, CH), f32)],
        scratch_shapes=[pltpu.VMEM((H, CH, CH), f32), pltpu.VMEM((R + 8, W3), f32)],
        compiler_params=_cparams(("arbitrary",)), name="gdn_fwd")(hg, hg, cw, alog, dtb, gn)


def _gdn_bwd(hg, cw, alog, dtb, gn, states, dout):
    T = hg.shape[0]
    R = RB_GDN
    nc = R // CH
    nb = T // R
    W3 = 3 * GDN_W
    H = GDN_H

    def body(h_ref, t_ref, cw_ref, al_ref, dt_ref, gn_ref, st_ref, do_ref,
             dh_ref, dcw_ref, dal_ref, ddt_ref, dgn_ref, dst, carry_dy, ext, ext2, wide):
        i = pl.program_id(0)
        first_blk = i == nb - 1

        @pl.when(i == 0)
        def _():
            dst[...] = jnp.zeros_like(dst)
            carry_dy[...] = jnp.zeros_like(carry_dy)
            for r in (dcw_ref, dal_ref, ddt_ref, dgn_ref):
                r[...] = jnp.zeros_like(r)

        tail = jnp.where(first_blk, 0.0, t_ref[:, 0:W3])
        ypre = _conv_fwd(ext, h_ref[:, 0:W3], tail, cw_ref[...], R)
        y = _silu(ypre)
        qs, ks, vs = (_split_heads(y[:, j * GDN_W:(j + 1) * GDN_W], H) for j in range(3))
        zh = _split_heads(h_ref[:, W3:W3 + GDN_W], H)
        ab = h_ref[:, W3 + GDN_W:GDN_IN]
        alog, dtb = al_ref[...], dt_ref[...]
        g, beta = _gdn_scalars(ab, alog, dtb)
        loc, vjp_loc = jax.vjp(_gdn_local, qs, ks, vs, _bcast_heads(g, 0, H), _bcast_heads(beta, H, H))
        doh = _split_heads(do_ref[...], H)
        gnv = gn_ref[...]
        dloc = [[None] * nc for _ in range(6)]
        dzs = [None] * nc
        ds = dst[...]
        dgn = jnp.zeros((1, CH), f32)
        for c in reversed(range(nc)):
            sl = slice(c * H, (c + 1) * H)
            _, vjp = jax.vjp(_gdn_step, st_ref[c], *(t[sl] for t in loc), zh[sl], gnv)
            grads = vjp((doh[sl], ds))
            ds = grads[0]
            for j in range(6):
                dloc[j][c] = grads[1 + j]
            dzs[c] = grads[7]
            dgn = dgn + grads[8]
        dst[...] = ds
        dgn_ref[...] += dgn
        dqs, dks, dvs, dgb, dbb = vjp_loc(tuple(jnp.concatenate(d, axis=0) for d in dloc))
        lane = lax.broadcasted_iota(jnp.int32, (R, 128), 1)
        dg = _unbcast_heads(dgb, 0, H)
        dbeta = _unbcast_heads(dbb, H, H)
        da = dg * (-jnp.exp(alog)) * _sigmoid(ab + dtb)
        dh_ref[:, W3 + GDN_W:GDN_IN] = jnp.where(lane < H, da, dbeta * beta * (1.0 - beta))
        ddt_ref[...] += jnp.sum(jnp.where(lane < H, da, 0.0), 0, keepdims=True)
        dal_ref[...] += jnp.sum(jnp.where(lane < H, dg * g, 0.0), 0, keepdims=True)
        _merge_heads(dh_ref, jnp.concatenate(dzs, axis=0), H, col0=W3)
        for j, dpart in enumerate((dqs, dks, dvs)):
            _merge_heads(wide, dpart, H, col0=j * GDN_W)
        dy = wide[...] * _dsilu(ypre)
        dx, dcw = _conv_bwd(ext, ext2, dy, carry_dy[...], cw_ref[...], R)
        carry_dy[...] = dy[0:8, :]
        dcw_ref[...] += dcw
        dh_ref[:, 0:W3] = dx

    blk = pl.BlockSpec((R, GDN_IN), lambda i: (nb - 1 - i, 0))
    return pl.pallas_call(
        body, grid=(nb,),
        in_specs=[blk, _prev_tail_spec_rev(R, GDN_IN, nb), _full_spec((4, W3)), _full_spec((1, 128)),
                  _full_spec((1, 128)), _full_spec((1, CH)),
                  pl.BlockSpec((nc, H, CH, CH), lambda i: (nb - 1 - i, 0, 0, 0)),
                  pl.BlockSpec((R, GDN_W), lambda i: (nb - 1 - i, 0))],
        out_specs=[blk, _full_spec((4, W3)), _full_spec((1, 128)), _full_spec((1, 128)), _full_spec((1, CH))],
        out_shape=[jax.ShapeDtypeStruct((T, GDN_IN), f32), jax.ShapeDtypeStruct((4, W3), f32),
                   jax.ShapeDtypeStruct((1, 128), f32), jax.ShapeDtypeStruct((1, 128), f32),
                   jax.ShapeDtypeStruct((1, CH), f32)],
        scratch_shapes=[pltpu.VMEM((H, CH, CH), f32), pltpu.VMEM((8, W3), f32), pltpu.VMEM((R + 8, W3), f32),
                        pltpu.VMEM((R + 8, W3), f32), pltpu.VMEM((R, W3), f32)],
        compiler_params=_cparams(("arbitrary",)), name="gdn_bwd")(hg, hg, cw, alog, dtb, gn, states, dout)


def _block_diag(w):
    out = jnp.zeros((LRU_W, LRU_W), w.dtype)
    for g in range(w.shape[0]):
        out = lax.dynamic_update_slice(out, w[g], (g * CH, g * CH))
    return out


def _block_diag_t(w):
    return jnp.stack([w[g * CH:(g + 1) * CH, g * CH:(g + 1) * CH] for g in range(LRU_W // CH)])


def _pad_lanes(v, n=128):
    return jnp.pad(v, (0, n - v.shape[0]))[None, :]


def _local_step(x, p, positions, target, G, sm):
    p384, pd, pr, pinl, ping, wpp = G['p384'], G['pd'], G['pr'], G['pinl'], G['ping'], G['wpp']
    cosw, sinw = _rope_tables(positions)
    saved = []
    h = x
    for l in range(DEPTH):
        v = lambda n: sm[n][l][None, :]
        z1, x1, g1, u1 = _ffn_fwd(h, p384, pd, v('ln_ffn1_g'), v('ln_ffn1_b'), l, 0)
        hr, hl, hg = _proj_in(x1, pr, pinl, ping, l)
        o_r, rst = _ret_fwd(hr, cosw, sinw, v('ret_norm_g'))
        lru_args = (sm['lru_conv_w'][l], v('lru_conv_b'), _block_diag(sm['lru_w_a'][l]), v('lru_b_a'),
                    _block_diag(sm['lru_w_x'][l]), v('lru_b_x'), v('lru_lambda'))
        o_l, hs = _lru_fwd(hl, *lru_args)
        gdn_args = (sm['gdn_conv_w'][l], _pad_lanes(sm['gdn_a_log'][l]), _pad_lanes(sm['gdn_dt_bias'][l]),
                    v('gdn_norm_g'))
        o_g, gst = _gdn_fwd(hg, *gdn_args)
        z2, x2 = _mix_out(x1, o_r, o_l, o_g, pr, v('ln_mix_g'), v('ln_mix_b'), l)
        z3, x3, g2, u2 = _ffn_fwd(x2, p384, pd, v('ln_ffn2_g'), v('ln_ffn2_b'), l, 1, ple=(p[l], pr, wpp[l]))
        saved.append((h, z1, x1, hr, hl, hg, o_r, rst, o_l, hs, lru_args, o_g, gst, gdn_args, z2, x2, z3,
                      g1, u1, g2, u2))
        h = x3
    d, loss = _loss_grad(h, target)

    big = {k: [None] * DEPTH for k in ('p384', 'pd', 'pr', 'pinl', 'ping', 'ppp')}
    small = {n: [None] * DEPTH for n in SMALL}
    for l in reversed(range(DEPTH)):
        (x0, z1, x1, hr, hl, hg, o_r, rst, o_l, hs, lru_args, o_g, gst, gdn_args, z2, x2, z3,
         g1, u1, g2, u2) = saved[l]
        v = lambda n: sm[n][l][None, :]
        d2, dg2, du2, a2, dy2, small['ln_ffn2_g'][l], small['ln_ffn2_b'][l], dgp, dpj = _ffn_bwd(
            x2, z3, d, g2, u2, p384, pd, v('ln_ffn2_g'), l, 1, ple=(p[l], pr, wpp[l]))
        dxb, dzb, do_r, do_l, do_g, small['ln_mix_g'][l], small['ln_mix_b'][l] = _mix_out_bwd(
            z2, d2, pr, v('ln_mix_g'), l)
        dhr, small['ret_norm_g'][l] = _ret_bwd(hr, cosw, sinw, v('ret_norm_g'), rst, do_r)
        (dhl, small['lru_conv_w'][l], small['lru_conv_b'][l], dwa, small['lru_b_a'][l], dwx, small['lru_b_x'][l],
         small['lru_lambda'][l]) = _lru_bwd(hl, hs, *lru_args, do_l)
        small['lru_w_a'][l], small['lru_w_x'][l] = _block_diag_t(dwa), _block_diag_t(dwx)
        dhg, small['gdn_conv_w'][l], dal, ddt, small['gdn_norm_g'][l] = _gdn_bwd(hg, *gdn_args, gst, do_g)
        small['gdn_a_log'][l], small['gdn_dt_bias'][l] = dal[:, 0:GDN_H], ddt[:, 0:GDN_H]
        d1 = _proj_in_bwd(dxb, dhr, dhl, dhg, pr, pinl, ping, l)
        d, dg1, du1, a1, dy1, small['ln_ffn1_g'][l], small['ln_ffn1_b'][l] = _ffn_bwd(
            x0, z1, d1, g1, u1, p384, pd, v('ln_ffn1_g'), l, 0)
        rows = lambda m: m.reshape(NDEV, m.shape[1] // NDEV, m.shape[2])
        big['p384'][l] = jnp.stack([_matmul_tn(x0, dg1, FSP, "dw_gate", FB), _matmul_tn(x0, du1, FSP, "dw_up", FB),
                                    _matmul_tn(x2, dg2, FSP, "dw_gate", FB), _matmul_tn(x2, du2, FSP, "dw_up", FB)],
                                   axis=1)
        big['pd'][l] = jnp.stack([rows(_matmul_tn(a1, dy1, D, "dw_down")),
                                  rows(_matmul_tn(a2, dy2, D, "dw_down"))], axis=1)
        dwo = jnp.concatenate([_matmul_tn(o_r, dzb, D, "dw_out_r"), _matmul_tn(o_l, dzb, D, "dw_out_l"),
                               _matmul_tn(o_g, dzb, D, "dw_out_g")], axis=1)
        big['pr'][l] = jnp.stack([rows(_matmul_tn(x1, dhr, D, "dw_in_r")), rows(dwo),
                                  rows(_matmul_tn(x2, dgp, D, "dw_ple_gate"))], axis=1)
        big['pinl'][l] = rows(_matmul_tn(x1, dhl, 2 * LRU_W, "dw_in_l"))
        big['ping'][l] = rows(_matmul_tn(x1, dhg, GDN_IN, "dw_in_g"))
        big['ppp'][l] = _matmul_tn(p[l], dpj, 128, "dw_ple_proj")
    big = {'p384': jnp.concatenate(big['p384'], axis=1), 'pd': jnp.concatenate(big['pd'], axis=1),
           'pr': jnp.concatenate(big['pr'], axis=1), 'pinl': jnp.stack(big['pinl'], axis=1),
           'ping': jnp.stack(big['ping'], axis=1), 'ppp': jnp.stack(big['ppp'], axis=1)}
    small = {n: jnp.stack([g.reshape(sm[n].shape[1:]) for g in gs]) for n, gs in small.items()}
    return loss, d, big, small


def _pack_big(ws, dtype=bf16):
    padc = lambda a, n: jnp.pad(a, ((0, 0), (0, 0), (0, n - a.shape[2])))
    padr = lambda a, n: jnp.pad(a, ((0, 0), (0, n - a.shape[1]), (0, 0)))
    per_layer = lambda arrs: jnp.stack(arrs, axis=1).reshape((-1,) + arrs[0].shape[1:])
    w_in = ws['w_in']
    out = {
        'p384': per_layer([padc(ws[n], FSP) for n in ('ffn1_w_gate', 'ffn1_w_up', 'ffn2_w_gate', 'ffn2_w_up')]),
        'pd': per_layer([padr(ws[n], FSP) for n in ('ffn1_w_down', 'ffn2_w_down')]),
        'pr': per_layer([w_in[:, :, 0:D], ws['w_out'], ws['ple_w_gate']]),
        'pinl': w_in[:, :, D:D + 2 * LRU_W],
        'ping': padc(w_in[:, :, D + 2 * LRU_W:D_IN], GDN_IN),
        'ppp': ws['ple_w_proj'],
    }
    return {k: a.astype(dtype) for k, a in out.items()}


def _exchange(arrays, scatter, name):
    n = len(arrays)

    def body(*refs):
        ins, outs = refs[:n], refs[n:2 * n]
        send_sems, recv_sems, local_sems = refs[2 * n:]
        x, y, c = lax.axis_index("x"), lax.axis_index("y"), lax.axis_index("c")
        me = 4 * x + 2 * y + c
        copies = []
        for i in range(n):
            src = ins[i].at[me] if scatter[i] else ins[i]
            cp = pltpu.make_async_copy(src, outs[i].at[me], local_sems.at[i])
            cp.start()
            copies.append(cp)
        sends = []
        for j in range(1, NDEV):
            peer = (me + j) % NDEV
            pid = (peer // 4, (peer // 2) % 2, peer % 2)
            for i in range(n):
                src = ins[i].at[peer] if scatter[i] else ins[i]
                cp = pltpu.make_async_remote_copy(
                    src_ref=src, dst_ref=outs[i].at[me], send_sem=send_sems.at[i, j - 1],
                    recv_sem=recv_sems.at[i, j - 1], device_id=pid, device_id_type=pl.DeviceIdType.MESH)
                cp.start()
                sends.append(cp)
        for j in range(1, NDEV):
            source = (me + NDEV - j) % NDEV
            sid = (source // 4, (source // 2) % 2, source % 2)
            for i in range(n):
                src = ins[i].at[me] if scatter[i] else ins[i]
                pltpu.make_async_remote_copy(
                    src_ref=src, dst_ref=outs[i].at[source], send_sem=send_sems.at[i, j - 1],
                    recv_sem=recv_sems.at[i, j - 1], device_id=sid, device_id_type=pl.DeviceIdType.MESH).wait_recv()
        for cp in sends:
            cp.wait_send()
        for cp in copies:
            cp.wait()

    hbm = pl.BlockSpec(memory_space=pltpu.HBM)
    out_shape = [jax.ShapeDtypeStruct(a.shape if s else (NDEV,) + a.shape, a.dtype) for a, s in zip(arrays, scatter)]
    return pl.pallas_call(
        body, in_specs=[hbm] * n, out_specs=[hbm] * n, out_shape=out_shape,
        scratch_shapes=[pltpu.SemaphoreType.DMA((n, NDEV - 1)), pltpu.SemaphoreType.DMA((n, NDEV - 1)),
                        pltpu.SemaphoreType.DMA((n,))],
        compiler_params=pltpu.CompilerParams(has_side_effects=True), name=name)(*arrays)


def _gather_two_level(arrays, name):
    n = len(arrays)

    def body(*refs):
        ins, outs = refs[:n], refs[n:2 * n]
        send_sems, recv_sems, local_sems = refs[2 * n:]
        x, y, c = lax.axis_index("x"), lax.axis_index("y"), lax.axis_index("c")
        me, sibling = (x, y, c), (x, y, 1 - c)
        chips = [(1 - x, y), (x, 1 - y), (1 - x, 1 - y)]
        slot = lambda d: 4 * d[0] + 2 * d[1] + d[2]

        def copy(i, k, block, to, src=None):
            return pltpu.make_async_remote_copy(
                src_ref=outs[i].at[slot(block)] if src is None else src, dst_ref=outs[i].at[slot(block)],
                send_sem=send_sems.at[i, k], recv_sem=recv_sems.at[i, k], device_id=to,
                device_id_type=pl.DeviceIdType.MESH)

        mine, first, passed = [], [], []
        for i in range(n):
            cp = pltpu.make_async_copy(ins[i], outs[i].at[slot(me)], local_sems.at[i])
            cp.start()
            mine.append(cp)
            first.append(copy(i, 0, me, sibling, src=ins[i]))
            first += [copy(i, 1 + j, me, (*chip, c), src=ins[i]) for j, chip in enumerate(chips)]
        for cp in first:
            cp.start()
        for i in range(n):
            for j, chip in enumerate(chips):
                copy(i, 1 + j, (*chip, c), me).wait_recv()
                cp = copy(i, 4 + j, (*chip, c), sibling)
                cp.start()
                passed.append(cp)
        for i in range(n):
            copy(i, 0, sibling, me).wait_recv()
            for j, chip in enumerate(chips):
                copy(i, 4 + j, (*chip, 1 - c), me).wait_recv()
        for cp in first + passed:
            cp.wait_send()
        for cp in mine:
            cp.wait()

    hbm = pl.BlockSpec(memory_space=pltpu.HBM)
    return pl.pallas_call(
        body, in_specs=[hbm] * n, out_specs=[hbm] * n,
        out_shape=[jax.ShapeDtypeStruct((NDEV,) + a.shape, a.dtype) for a in arrays],
        scratch_shapes=[pltpu.SemaphoreType.DMA((n, NDEV - 1)), pltpu.SemaphoreType.DMA((n, NDEV - 1)),
                        pltpu.SemaphoreType.DMA((n,))],
        compiler_params=pltpu.CompilerParams(has_side_effects=True), name=name)(*arrays)


def _adam_math(w, g, m, v):
    m2 = ADAM_B1 * m + (1.0 - ADAM_B1) * g
    v2 = ADAM_B2 * v + (1.0 - ADAM_B2) * (g * g)
    m_hat = m2 / (1.0 - ADAM_B1 ** ADAM_STEP)
    v_hat = v2 / (1.0 - ADAM_B2 ** ADAM_STEP)
    return -ADAM_LR * (m_hat / (jnp.sqrt(v_hat) + ADAM_EPS) + ADAM_WD * w), m2, v2


def _adam_big(parts, w, m, v, name):
    L, rows, cols = w.shape

    def body(*refs):
        prefs = refs[:len(parts)]
        w_ref, m_ref, v_ref, g_ref, d_ref, m2_ref, v2_ref = refs[len(parts):]
        c0 = 0
        for pref in prefs:
            acc = pref[0].astype(f32)
            for s in range(1, NDEV):
                acc = acc + pref[s].astype(f32)
            width = min(acc.shape[1], cols - c0)
            g_ref[:, c0:c0 + width] = acc[0:rows, 0:width]
            c0 += width
        d, m2, v2 = _adam_math(w_ref[...], g_ref[...], m_ref[...], v_ref[...])
        d_ref[...] = d
        m2_ref[...] = m2
        v2_ref[...] = v2

    wspec = pl.BlockSpec((None, rows, cols), lambda l: (l, 0, 0))
    in_specs = [pl.BlockSpec((NDEV, None) + a.shape[2:], functools.partial(lambda l, per, first: (0, per * l + first, 0, 0), per=per, first=first))
                for a, per, first in parts]
    return pl.pallas_call(
        body, grid=(L,), in_specs=in_specs + [wspec] * 3, out_specs=[wspec] * 4,
        out_shape=[jax.ShapeDtypeStruct(w.shape, f32)] * 4,
        compiler_params=_cparams(("arbitrary",)), name=name)(*[a for a, _, _ in parts], w, m, v)


def _sum_sources(stacked):
    rows = stacked.shape[1]

    def body(s_ref, o_ref):
        acc = s_ref[0]
        for s in range(1, NDEV):
            acc = acc + s_ref[s]
        o_ref[...] = acc

    return pl.pallas_call(body, out_shape=jax.ShapeDtypeStruct((rows, 128), f32), name="sum_small_grads")(stacked)


def _adam_small(w, g, m, v):
    def body(w_ref, g_ref, m_ref, v_ref, d_ref, m2_ref, v2_ref):
        d, m2, v2 = _adam_math(w_ref[...], g_ref[...], m_ref[...], v_ref[...])
        d_ref[...] = d
        m2_ref[...] = m2
        v2_ref[...] = v2

    return pl.pallas_call(body, out_shape=[jax.ShapeDtypeStruct(w.shape, f32)] * 3, name="adam_small")(w, g, m, v)


def _pack_rows(arrs):
    flat = []
    for a in arrs:
        a = a.reshape(-1)
        flat.append(jnp.pad(a, (0, (-a.shape[0]) % 1024)))
    return jnp.concatenate(flat).reshape(-1, 128)


def _unpack_rows(packed, shapes):
    out, off = [], 0
    flat = packed.reshape(-1)
    for s in shapes:
        n = math.prod(s)
        out.append(flat[off:off + n].reshape(s))
        off += n + (-n) % 1024
    return out


def _gather_conv(gathered, shape):
    L, K, c = shape
    return jnp.transpose(gathered, (1, 2, 0, 3)).reshape(L, K, NDEV * c)


def kernel(x, p, positions, ln_ffn1_g, ln_ffn1_b, ffn1_w_gate, ffn1_w_up, ffn1_w_down, w_in, ret_norm_g, lru_conv_w, lru_conv_b, lru_w_a, lru_b_a, lru_w_x, lru_b_x, lru_lambda, gdn_conv_w, gdn_a_log, gdn_dt_bias, gdn_norm_g, w_out, ln_mix_g, ln_mix_b, ffn2_w_gate, ffn2_w_up, ffn2_w_down, ple_w_gate, ple_w_proj, ln_ffn2_g, ln_ffn2_b, loss_target, m_ln_ffn1_g, m_ln_ffn1_b, m_ffn1_w_gate, m_ffn1_w_up, m_ffn1_w_down, m_w_in, m_ret_norm_g, m_lru_conv_w, m_lru_conv_b, m_lru_w_a, m_lru_b_a, m_lru_w_x, m_lru_b_x, m_lru_lambda, m_gdn_conv_w, m_gdn_a_log, m_gdn_dt_bias, m_gdn_norm_g, m_w_out, m_ln_mix_g, m_ln_mix_b, m_ffn2_w_gate, m_ffn2_w_up, m_ffn2_w_down, m_ple_w_gate, m_ple_w_proj, m_ln_ffn2_g, m_ln_ffn2_b, v_ln_ffn1_g, v_ln_ffn1_b, v_ffn1_w_gate, v_ffn1_w_up, v_ffn1_w_down, v_w_in, v_ret_norm_g, v_lru_conv_w, v_lru_conv_b, v_lru_w_a, v_lru_b_a, v_lru_w_x, v_lru_b_x, v_lru_lambda, v_gdn_conv_w, v_gdn_a_log, v_gdn_dt_bias, v_gdn_norm_g, v_w_out, v_ln_mix_g, v_ln_mix_b, v_ffn2_w_gate, v_ffn2_w_up, v_ffn2_w_down, v_ple_w_gate, v_ple_w_proj, v_ln_ffn2_g, v_ln_ffn2_b):
    args = locals()
    W = {n: args[n] for n in WEIGHTS}
    M = {n: args['m_' + n] for n in WEIGHTS}
    V = {n: args['v_' + n] for n in WEIGHTS}
    me = 4 * lax.axis_index("x") + 2 * lax.axis_index("y") + lax.axis_index("c")

    packed = _pack_big(W)
    conv_pack = _pack_rows([W[n] for n in CONV_SHARDED])
    keys = list(packed)
    gathered = _gather_two_level([packed[k] for k in keys] + [conv_pack], "gather_weights")
    G = dict(zip(keys, gathered[:-1]))
    G['wpp'] = jnp.transpose(G.pop('ppp'), (1, 2, 0, 3)).reshape(DEPTH, PLE, D)
    conv_all = gathered[-1]
    sm = {n: W[n] for n in SMALL}
    conv_shards = [_unpack_rows(conv_all[s], [W[n].shape for n in CONV_SHARDED]) for s in range(NDEV)]
    for i, n in enumerate(CONV_SHARDED):
        sm[n] = _gather_conv(jnp.stack([cs[i] for cs in conv_shards]), W[n].shape)

    loss, grad_x, big, small = _local_step(x[0], p[:, 0], positions.reshape(-1, 1), loss_target[0], G, sm)
    loss = lax.psum(loss[0, 0], ("x", "y", "c"))

    small_pack = _pack_rows([small[n] for n in SMALL])
    bkeys = list(big)
    recv = _exchange([big[k] for k in bkeys] + [small_pack], [True] * len(bkeys) + [False], "scatter_grads")
    R = dict(zip(bkeys, recv[:-1]))
    small_sum = _unpack_rows(_sum_sources(recv[-1]), [small[n].shape for n in SMALL])
    grads, delta, new_m, new_v = {}, {}, {}, {}
    for n, g in zip(SMALL, small_sum):
        if n in CONV_SHARDED:
            c = W[n].shape[2]
            g = lax.dynamic_slice_in_dim(g, me * c, c, axis=2)
        grads[n] = g

    big_parts = {
        'ffn1_w_gate': [(R['p384'], 4, 0)], 'ffn1_w_up': [(R['p384'], 4, 1)],
        'ffn2_w_gate': [(R['p384'], 4, 2)], 'ffn2_w_up': [(R['p384'], 4, 3)],
        'ffn1_w_down': [(R['pd'], 2, 0)], 'ffn2_w_down': [(R['pd'], 2, 1)],
        'w_in': [(R['pr'], 3, 0), (R['pinl'], 1, 0), (R['ping'], 1, 0)],
        'w_out': [(R['pr'], 3, 1)], 'ple_w_gate': [(R['pr'], 3, 2)], 'ple_w_proj': [(R['ppp'], 1, 0)],
    }
    for n in BIG:
        grads[n], delta[n], new_m[n], new_v[n] = _adam_big(big_parts[n], W[n], M[n], V[n], "adam_" + n)
    shapes = [W[n].shape for n in SMALL]
    d_s, m_s, v_s = _adam_small(*[_pack_rows([src[n] for n in SMALL]) for src in (W, grads, M, V)])
    for n, dd, mm, vv in zip(SMALL, _unpack_rows(d_s, shapes), _unpack_rows(m_s, shapes), _unpack_rows(v_s, shapes)):
        delta[n], new_m[n], new_v[n] = dd, mm, vv

    return (loss, grad_x[None], *[grads[n] for n in WEIGHTS], *[delta[n] for n in WEIGHTS],
            *[new_m[n] for n in WEIGHTS], *[new_v[n] for n in WEIGHTS])
```

```python
import functools
import math

import jax
import jax.numpy as jnp
from jax import lax
from jax.experimental import pallas as pl
from jax.experimental.pallas import tpu as pltpu

f32 = jnp.float32
bf16 = jnp.bfloat16

NDEV = 8
DEPTH = 2
D = 1024
FS = 352
FSP = 384
FB = 2
NF = NDEV // FB
PLE = 256
CH = 64
RET_H, GDN_H = 4, 6
RET_W, LRU_W, GDN_W = 256, 384, 384
GDN_IN = 1664
GDN_IN_REAL = 1548
D_IN = 3340
ALPHA = 4.0 ** 0.25
LN_EPS = 1e-5
ROPE_THETA = 10000.0
TM = 512
RB_RET, RB_LRU, RB_GDN = 512, 512, 256
VMEM_LIMIT = 56 * 1024 * 1024
ADAM_LR, ADAM_B1, ADAM_B2, ADAM_EPS, ADAM_WD, ADAM_STEP = 0.001, 0.9, 0.999, 1e-08, 0.01, 10

WEIGHTS = ['ln_ffn1_g', 'ln_ffn1_b', 'ffn1_w_gate', 'ffn1_w_up', 'ffn1_w_down', 'w_in', 'ret_norm_g', 'lru_conv_w',
           'lru_conv_b', 'lru_w_a', 'lru_b_a', 'lru_w_x', 'lru_b_x', 'lru_lambda', 'gdn_conv_w', 'gdn_a_log',
           'gdn_dt_bias', 'gdn_norm_g', 'w_out', 'ln_mix_g', 'ln_mix_b', 'ffn2_w_gate', 'ffn2_w_up', 'ffn2_w_down',
           'ple_w_gate', 'ple_w_proj', 'ln_ffn2_g', 'ln_ffn2_b']
BIG = ['ffn1_w_gate', 'ffn1_w_up', 'ffn1_w_down', 'w_in', 'w_out', 'ffn2_w_gate', 'ffn2_w_up', 'ffn2_w_down',
       'ple_w_gate', 'ple_w_proj']
SMALL = [n for n in WEIGHTS if n not in BIG]
CONV_SHARDED = {'lru_conv_w': LRU_W, 'gdn_conv_w': 3 * GDN_W}


def _cparams(sem=None):
    return pltpu.CompilerParams(dimension_semantics=sem, vmem_limit_bytes=VMEM_LIMIT)


def _sigmoid(x):
    return 1.0 / (1.0 + jnp.exp(-x))


def _silu(x):
    return x * _sigmoid(x)


def _dsilu(x):
    s = _sigmoid(x)
    return s * (1.0 + x * (1.0 - s))


def _softplus(x):
    return jnp.maximum(x, 0.0) + jnp.log(1.0 + jnp.exp(-jnp.abs(x)))


def _gelu(x):
    return 0.5 * x * (1.0 + jnp.tanh(0.7978845608028654 * (x + 0.044715 * x * x * x)))


def _dot(a, b):
    return jnp.dot(a.astype(bf16), b.astype(bf16), preferred_element_type=f32)


def _dot_nt(a, b):
    return lax.dot_general(a.astype(bf16), b.astype(bf16), (((1,), (1,)), ((), ())), preferred_element_type=f32)


def _dot_tn(a, b):
    return lax.dot_general(a.astype(bf16), b.astype(bf16), (((0,), (0,)), ((), ())), preferred_element_type=f32)


def _bmm(eq, a, b):
    return jnp.einsum(eq, a.astype(bf16), b.astype(bf16), preferred_element_type=f32)


def _split3(a):
    a1 = a.astype(bf16)
    r = a - a1.astype(f32)
    a2 = r.astype(bf16)
    return a1, a2, (r - a2.astype(f32)).astype(bf16)


def _bmm3(eq, a, b):
    a1, a2, _ = _split3(a)
    b1, b2, _ = _split3(b)
    e = lambda x, y: jnp.einsum(eq, x, y, preferred_element_type=f32)
    return e(a1, b1) + (e(a1, b2) + e(a2, b1))


def _tri_ones(B, upper=False):
    ii = lax.broadcasted_iota(jnp.int32, (B, CH, CH), 1)
    jj = lax.broadcasted_iota(jnp.int32, (B, CH, CH), 2)
    return jnp.where((ii <= jj) if upper else (ii >= jj), 1.0, 0.0).astype(bf16)


def _cumsum_mm(t, x):
    x1, x2, x3 = _split3(x)
    e = lambda y: jnp.einsum('bij,bjk->bik', t, y, preferred_element_type=f32)
    return e(x1) + (e(x2) + e(x3))


@jax.custom_vjp
def _cumsum_rows(x):
    return _cumsum_mm(_tri_ones(x.shape[0]), x)


def _cumsum_rows_fwd(x):
    return _cumsum_rows(x), None


def _cumsum_rows_bwd(_, g):
    return (_cumsum_mm(_tri_ones(g.shape[0], upper=True), g),)


_cumsum_rows.defvjp(_cumsum_rows_fwd, _cumsum_rows_bwd)


@jax.custom_vjp
def _neumann_inv(m):
    ii = lax.broadcasted_iota(jnp.int32, m.shape, 1)
    jj = lax.broadcasted_iota(jnp.int32, m.shape, 2)
    inv = jnp.where(ii == jj, 1.0, 0.0).astype(f32) + m
    mp = m
    for _ in range(5):
        mp = _bmm3('bij,bjk->bik', mp, mp)
        inv = inv + _bmm3('bij,bjk->bik', inv, mp)
    return inv


def _neumann_inv_fwd(m):
    inv = _neumann_inv(m)
    return inv, inv


def _neumann_inv_bwd(inv, g):
    return (_bmm3('bij,bkj->bik', _bmm3('bji,bjk->bik', inv, g), inv),)


_neumann_inv.defvjp(_neumann_inv_fwd, _neumann_inv_bwd)


def _ln_stats(z):
    mu = jnp.mean(z, -1, keepdims=True)
    zc = z - mu
    rstd = lax.rsqrt(jnp.mean(zc * zc, -1, keepdims=True) + LN_EPS)
    return zc * rstd, rstd


def _ln_bwd(z, g, dout):
    xh, rstd = _ln_stats(z)
    dxh = dout * g
    dz = rstd * (dxh - jnp.mean(dxh, -1, keepdims=True) - xh * jnp.mean(dxh * xh, -1, keepdims=True))
    return dz, jnp.sum(dout * xh, 0, keepdims=True), jnp.sum(dout, 0, keepdims=True)


def _full_spec(shape):
    nd = len(shape)
    return pl.BlockSpec(shape, lambda *_: (0,) * nd)


def _ffn_fwd(x, p384, pd, lg, lb, layer, which, ple=None):
    T = x.shape[0]
    sg, su, sd = 4 * layer + 2 * which, 4 * layer + 2 * which + 1, 2 * layer + which
    has_ple = ple is not None

    def body(*refs):
        if has_ple:
            (x_ref, wg_ref, wu_ref, wd_ref, lg_ref, lb_ref, p_ref, wpg_ref, wpp_ref,
             z_ref, o_ref, g_ref, u_ref, acc, xb_s) = refs
        else:
            x_ref, wg_ref, wu_ref, wd_ref, lg_ref, lb_ref, z_ref, o_ref, g_ref, u_ref, acc, xb_s = refs
        f = pl.program_id(1)

        @pl.when(f == 0)
        def _():
            x = x_ref[...]
            xb = x.astype(bf16)
            xb_s[...] = xb
            base = ALPHA * x
            if has_ple:
                gate = _sigmoid(_dot(xb, wpg_ref[...].reshape(D, D)))
                base = base + gate * _dot(p_ref[...], wpp_ref[...])
            acc[...] = base

        xb = xb_s[...]
        g = _dot(xb, wg_ref[...])
        u = _dot(xb, wu_ref[...])
        g_ref[...] = g.astype(bf16)
        u_ref[...] = u.astype(bf16)
        acc[...] += 0.5 * _dot(_silu(g) * u, wd_ref[...].reshape(FB * FSP, D))

        @pl.when(f == NF - 1)
        def _():
            z = acc[...]
            z_ref[...] = z
            o_ref[...] = _ln_stats(z)[0] * lg_ref[...] + lb_ref[...]

    row = pl.BlockSpec((TM, D), lambda i, f: (i, 0))
    in_specs = [row,
                pl.BlockSpec((None, D, FB * FSP), lambda i, f: (sg, 0, f)),
                pl.BlockSpec((None, D, FB * FSP), lambda i, f: (su, 0, f)),
                pl.BlockSpec((FB, None, FSP, D), lambda i, f: (f, sd, 0, 0)),
                _full_spec((1, D)), _full_spec((1, D))]
    args = [x, p384, p384, pd, lg, lb]
    if has_ple:
        p, pr, wpp = ple
        in_specs += [pl.BlockSpec((TM, PLE), lambda i, f: (i, 0)),
                     pl.BlockSpec((NDEV, None, 128, D), lambda i, f: (0, 3 * layer + 2, 0, 0)),
                     _full_spec((PLE, D))]
        args += [p, pr, wpp]
    hid = pl.BlockSpec((TM, FB * FSP), lambda i, f: (i, f))
    hshape = jax.ShapeDtypeStruct((T, NDEV * FSP), bf16)
    return pl.pallas_call(
        body, grid=(T // TM, NF), in_specs=in_specs, out_specs=[row, row, hid, hid],
        out_shape=[jax.ShapeDtypeStruct((T, D), f32)] * 2 + [hshape, hshape],
        scratch_shapes=[pltpu.VMEM((TM, D), f32), pltpu.VMEM((TM, D), bf16)],
        compiler_params=_cparams(("arbitrary", "arbitrary")), name=f"ffn{which + 1}_fwd")(*args)


def _ffn_bwd(x, z, dout, gs, us, p384, pd, lg, layer, which, ple=None):
    T = z.shape[0]
    TMB = TM // 2
    sg, su, sd = 4 * layer + 2 * which, 4 * layer + 2 * which + 1, 2 * layer + which
    has_ple = ple is not None

    def body(*refs):
        if has_ple:
            (z_ref, do_ref, g_ref, u_ref, wg_ref, wu_ref, wd_ref, lg_ref, x_ref, p_ref, wpg_ref, wpp_ref,
             dx_ref, dg_ref, du_ref, a_ref, dy_ref, dlg_ref, dlb_ref, dgp_ref, dpj_ref, acc, dyb) = refs
        else:
            (z_ref, do_ref, g_ref, u_ref, wg_ref, wu_ref, wd_ref, lg_ref,
             dx_ref, dg_ref, du_ref, a_ref, dy_ref, dlg_ref, dlb_ref, acc, dyb) = refs
        i, f = pl.program_id(0), pl.program_id(1)

        @pl.when(jnp.logical_and(i == 0, f == 0))
        def _():
            dlg_ref[...] = jnp.zeros_like(dlg_ref)
            dlb_ref[...] = jnp.zeros_like(dlb_ref)

        @pl.when(f == 0)
        def _():
            dz, dlg, dlb = _ln_bwd(z_ref[...], lg_ref[...], do_ref[...])
            dlg_ref[...] += dlg
            dlb_ref[...] += dlb
            dy = (0.5 * dz).astype(bf16)
            dyb[...] = dy
            dy_ref[...] = dy
            dx = ALPHA * dz
            if has_ple:
                wpg = wpg_ref[...].reshape(D, D)
                gate = _sigmoid(_dot(x_ref[...], wpg))
                proj = _dot(p_ref[...], wpp_ref[...])
                dgp = (dz * proj * gate * (1.0 - gate)).astype(bf16)
                dgp_ref[...] = dgp
                dpj_ref[...] = (dz * gate).astype(bf16)
                dx = dx + _dot_nt(dgp, wpg)
            acc[...] = dx

        g = g_ref[...].astype(f32)
        u = u_ref[...].astype(f32)
        da = _dot_nt(dyb[...], wd_ref[...].reshape(FB * FSP, D))
        sgm = _sigmoid(g)
        dg = (da * u * (sgm * (1.0 + g * (1.0 - sgm)))).astype(bf16)
        du = (da * (g * sgm)).astype(bf16)
        dg_ref[...] = dg
        du_ref[...] = du
        a_ref[...] = (g * sgm * u).astype(bf16)
        acc[...] += _dot_nt(dg, wg_ref[...]) + _dot_nt(du, wu_ref[...])

        @pl.when(f == NF - 1)
        def _():
            dx_ref[...] = acc[...]

    row = pl.BlockSpec((TMB, D), lambda i, f: (i, 0))
    hid = pl.BlockSpec((TMB, FB * FSP), lambda i, f: (i, f))
    vec = _full_spec((1, D))
    in_specs = [row, row, hid, hid,
                pl.BlockSpec((None, D, FB * FSP), lambda i, f: (sg, 0, f)),
                pl.BlockSpec((None, D, FB * FSP), lambda i, f: (su, 0, f)),
                pl.BlockSpec((FB, None, FSP, D), lambda i, f: (f, sd, 0, 0)),
                vec]
    args = [z, dout, gs, us, p384, p384, pd, lg]
    out_specs = [row, hid, hid, hid, row, vec, vec]
    hshape = jax.ShapeDtypeStruct((T, NDEV * FSP), bf16)
    out_shape = [jax.ShapeDtypeStruct((T, D), f32), hshape, hshape, hshape, jax.ShapeDtypeStruct((T, D), bf16),
                 jax.ShapeDtypeStruct((1, D), f32), jax.ShapeDtypeStruct((1, D), f32)]
    if has_ple:
        p, pr, wpp = ple
        in_specs += [row, pl.BlockSpec((TMB, PLE), lambda i, f: (i, 0)),
                     pl.BlockSpec((NDEV, None, 128, D), lambda i, f: (0, 3 * layer + 2, 0, 0)),
                     _full_spec((PLE, D))]
        args += [x, p, pr, wpp]
        out_specs += [row, row]
        out_shape += [jax.ShapeDtypeStruct((T, D), bf16)] * 2
    return pl.pallas_call(
        body, grid=(T // TMB, NF), in_specs=in_specs, out_specs=out_specs, out_shape=out_shape,
        scratch_shapes=[pltpu.VMEM((TMB, D), f32), pltpu.VMEM((TMB, D), bf16)],
        compiler_params=_cparams(("arbitrary", "arbitrary")), name=f"ffn{which + 1}_bwd")(*args)


def _matmul_tn(a, b, nb, name, nsub=1):
    T, M = a.shape
    N = b.shape[1]
    tk = min(T, 1024)
    nk = T // tk
    wide = nsub * nb

    def body(a_ref, b_ref, o_ref, acc):
        k = pl.program_id(1)

        @pl.when(k == 0)
        def _():
            acc[...] = jnp.zeros_like(acc)

        acc[...] += _dot_tn(a_ref[...], b_ref[...])

        @pl.when(k == nk - 1)
        def _():
            for j in range(nsub):
                o_ref[j] = acc[:, j * nb:(j + 1) * nb].astype(bf16)

    return pl.pallas_call(
        body, grid=(N // wide, nk),
        in_specs=[pl.BlockSpec((tk, M), lambda n, k: (k, 0)), pl.BlockSpec((tk, wide), lambda n, k: (k, n))],
        out_specs=pl.BlockSpec((nsub, M, nb), lambda n, k: (n, 0, 0)),
        out_shape=jax.ShapeDtypeStruct((N // nb, M, nb), bf16),
        scratch_shapes=[pltpu.VMEM((M, wide), f32)],
        compiler_params=_cparams(("arbitrary", "arbitrary")), name=name)(a, b)


def _proj_in(x, pr, pinl, ping, layer):
    T = x.shape[0]

    def body(x_ref, wr_ref, wl_ref, wg_ref, hr_ref, hl_ref, hg_ref):
        xb = x_ref[...].astype(bf16)
        hr_ref[...] = _dot(xb, wr_ref[...].reshape(D, D))
        hl_ref[...] = _dot(xb, wl_ref[...].reshape(D, 2 * LRU_W))
        hg_ref[...] = _dot(xb, wg_ref[...].reshape(D, GDN_IN))

    return pl.pallas_call(
        body, grid=(T // TM,),
        in_specs=[pl.BlockSpec((TM, D), lambda i: (i, 0)),
                  pl.BlockSpec((NDEV, None, 128, D), lambda i: (0, 3 * layer, 0, 0)),
                  pl.BlockSpec((NDEV, None, 128, 2 * LRU_W), lambda i: (0, layer, 0, 0)),
                  pl.BlockSpec((NDEV, None, 128, GDN_IN), lambda i: (0, layer, 0, 0))],
        out_specs=[pl.BlockSpec((TM, D), lambda i: (i, 0)), pl.BlockSpec((TM, 2 * LRU_W), lambda i: (i, 0)),
                   pl.BlockSpec((TM, GDN_IN), lambda i: (i, 0))],
        out_shape=[jax.ShapeDtypeStruct((T, D), f32), jax.ShapeDtypeStruct((T, 2 * LRU_W), f32),
                   jax.ShapeDtypeStruct((T, GDN_IN), f32)],
        compiler_params=_cparams(("arbitrary",)), name="proj_in")(x, pr, pinl, ping)


def _proj_in_bwd(base, dhr, dhl, dhg, pr, pinl, ping, layer):
    T = base.shape[0]

    def body(b_ref, dr_ref, dl_ref, dg_ref, wr_ref, wl_ref, wg_ref, o_ref):
        o_ref[...] = (b_ref[...] + _dot_nt(dr_ref[...], wr_ref[...].reshape(D, D))
                      + _dot_nt(dl_ref[...], wl_ref[...].reshape(D, 2 * LRU_W))
                      + _dot_nt(dg_ref[...], wg_ref[...].reshape(D, GDN_IN)))

    return pl.pallas_call(
        body, grid=(T // TM,),
        in_specs=[pl.BlockSpec((TM, D), lambda i: (i, 0)), pl.BlockSpec((TM, D), lambda i: (i, 0)),
                  pl.BlockSpec((TM, 2 * LRU_W), lambda i: (i, 0)), pl.BlockSpec((TM, GDN_IN), lambda i: (i, 0)),
                  pl.BlockSpec((NDEV, None, 128, D), lambda i: (0, 3 * layer, 0, 0)),
                  pl.BlockSpec((NDEV, None, 128, 2 * LRU_W), lambda i: (0, layer, 0, 0)),
                  pl.BlockSpec((NDEV, None, 128, GDN_IN), lambda i: (0, layer, 0, 0))],
        out_specs=pl.BlockSpec((TM, D), lambda i: (i, 0)),
        out_shape=jax.ShapeDtypeStruct((T, D), f32),
        compiler_params=_cparams(("arbitrary",)), name="proj_in_bwd")(base, dhr, dhl, dhg, pr, pinl, ping)


def _mix_out(x1, o_r, o_l, o_g, pr, lg, lb, layer):
    T = x1.shape[0]

    def body(x_ref, r_ref, l_ref, g_ref, w_ref, lg_ref, lb_ref, z_ref, o_ref):
        w = w_ref[...].reshape(D, D)
        z = (ALPHA * x_ref[...] + _dot(r_ref[...], w[0:RET_W]) + _dot(l_ref[...], w[RET_W:RET_W + LRU_W])
             + _dot(g_ref[...], w[RET_W + LRU_W:D]))
        z_ref[...] = z
        o_ref[...] = _ln_stats(z)[0] * lg_ref[...] + lb_ref[...]

    row = pl.BlockSpec((TM, D), lambda i: (i, 0))
    return pl.pallas_call(
        body, grid=(T // TM,),
        in_specs=[row, pl.BlockSpec((TM, RET_W), lambda i: (i, 0)), pl.BlockSpec((TM, LRU_W), lambda i: (i, 0)),
                  pl.BlockSpec((TM, GDN_W), lambda i: (i, 0)),
                  pl.BlockSpec((NDEV, None, 128, D), lambda i: (0, 3 * layer + 1, 0, 0)),
                  _full_spec((1, D)), _full_spec((1, D))],
        out_specs=[row, row], out_shape=[jax.ShapeDtypeStruct((T, D), f32)] * 2,
        compiler_params=_cparams(("arbitrary",)), name="mix_out")(x1, o_r, o_l, o_g, pr, lg, lb)


def _mix_out_bwd(z, dout, pr, lg, layer):
    T = z.shape[0]

    def body(z_ref, do_ref, w_ref, lg_ref, dxb_ref, dzb_ref, dr_ref, dl_ref, dg_ref, dlg_ref, dlb_ref):
        @pl.when(pl.program_id(0) == 0)
        def _():
            dlg_ref[...] = jnp.zeros_like(dlg_ref)
            dlb_ref[...] = jnp.zeros_like(dlb_ref)

        dz, dlg, dlb = _ln_bwd(z_ref[...], lg_ref[...], do_ref[...])
        dlg_ref[...] += dlg
        dlb_ref[...] += dlb
        dxb_ref[...] = ALPHA * dz
        dzb = dz.astype(bf16)
        dzb_ref[...] = dzb
        w = w_ref[...].reshape(D, D)
        dr_ref[...] = _dot_nt(dzb, w[0:RET_W])
        dl_ref[...] = _dot_nt(dzb, w[RET_W:RET_W + LRU_W])
        dg_ref[...] = _dot_nt(dzb, w[RET_W + LRU_W:D])

    row = pl.BlockSpec((TM, D), lambda i: (i, 0))
    vec = _full_spec((1, D))
    return pl.pallas_call(
        body, grid=(T // TM,),
        in_specs=[row, row, pl.BlockSpec((NDEV, None, 128, D), lambda i: (0, 3 * layer + 1, 0, 0)), vec],
        out_specs=[row, row, pl.BlockSpec((TM, RET_W), lambda i: (i, 0)), pl.BlockSpec((TM, LRU_W), lambda i: (i, 0)),
                   pl.BlockSpec((TM, GDN_W), lambda i: (i, 0)), vec, vec],
        out_shape=[jax.ShapeDtypeStruct((T, D), f32), jax.ShapeDtypeStruct((T, D), bf16),
                   jax.ShapeDtypeStruct((T, RET_W), f32), jax.ShapeDtypeStruct((T, LRU_W), f32),
                   jax.ShapeDtypeStruct((T, GDN_W), f32), jax.ShapeDtypeStruct((1, D), f32),
                   jax.ShapeDtypeStruct((1, D), f32)],
        compiler_params=_cparams(("arbitrary",)), name="mix_out_bwd")(z, dout, pr, lg)


def _loss_grad(y, target):
    T = y.shape[0]

    def body(y_ref, t_ref, dy_ref, l_ref):
        @pl.when(pl.program_id(0) == 0)
        def _():
            l_ref[...] = jnp.zeros_like(l_ref)

        e = y_ref[...] - t_ref[...]
        dy_ref[...] = e * (1.0 / D)
        l_ref[...] += 0.5 * jnp.sum(jnp.sum(e * e, -1, keepdims=True) * (1.0 / D), 0, keepdims=True)

    row = pl.BlockSpec((TM, D), lambda i: (i, 0))
    return pl.pallas_call(
        body, grid=(T // TM,), in_specs=[row, row], out_specs=[row, _full_spec((1, 1))],
        out_shape=[jax.ShapeDtypeStruct((T, D), f32), jax.ShapeDtypeStruct((1, 1), f32)],
        compiler_params=_cparams(("arbitrary",)), name="loss_grad")(y, target)


def _split_heads(x, H):
    n = x.shape[0] // CH
    parts = [x[:, h * CH:(h + 1) * CH].reshape(n, CH, CH) for h in range(H)]
    return jnp.stack(parts, axis=1).reshape(n * H, CH, CH)


def _merge_heads(ref, x, H, col0=0):
    n = x.shape[0] // H
    x4 = x.reshape(n, H, CH, CH)
    for h in range(H):
        ref[:, col0 + h * CH:col0 + (h + 1) * CH] = x4[:, h].reshape(n * CH, CH)


def _conv_fwd(ext, x, tail, w, R):
    ext[0:8, :] = tail
    ext[8:R + 8, :] = x
    y = w[3:4, :] * x
    for k in range(3):
        y = y + w[k:k + 1, :] * ext[5 + k:5 + k + R, :]
    return y


def _conv_bwd(ext, ext2, dy, dy_next, w, R):
    ext2[0:R, :] = dy
    ext2[R:R + 8, :] = dy_next
    dx = w[3:4, :] * dy
    dws = []
    for k in range(3):
        dx = dx + w[k:k + 1, :] * ext2[3 - k:3 - k + R, :]
        dws.append(jnp.sum(dy * ext[5 + k:5 + k + R, :], 0, keepdims=True))
    dws.append(jnp.sum(dy * ext[8:8 + R, :], 0, keepdims=True))
    return dx, jnp.concatenate(dws, axis=0)


def _prev_tail_spec(R, W):
    return pl.BlockSpec((8, W), lambda i: (jnp.maximum(i * (R // 8) - 1, 0), 0))


def _prev_tail_spec_rev(R, W, nb):
    return pl.BlockSpec((8, W), lambda i: (jnp.maximum((nb - 1 - i) * (R // 8) - 1, 0), 0))


def _rope_tables(positions):
    T = positions.shape[0]

    def body(p_ref, c_ref, s_ref):
        lane = lax.broadcasted_iota(jnp.int32, (TM, RET_W), 1)
        fi = (lane % 32).astype(f32)
        inv = jnp.exp(fi * (-math.log(ROPE_THETA) / 32.0))
        ang = p_ref[...].astype(f32) * inv
        c_ref[...] = jnp.cos(ang)
        s_ref[...] = jnp.where(lane % CH < 32, -jnp.sin(ang), jnp.sin(ang))

    row = pl.BlockSpec((TM, RET_W), lambda i: (i, 0))
    return pl.pallas_call(
        body, grid=(T // TM,), in_specs=[pl.BlockSpec((TM, 1), lambda i: (i, 0))], out_specs=[row, row],
        out_shape=[jax.ShapeDtypeStruct((T, RET_W), f32)] * 2,
        compiler_params=_cparams(("arbitrary",)), name="rope_tables")(positions)


def _partner(x):
    lane = lax.broadcasted_iota(jnp.int32, x.shape, 1)
    return jnp.where(lane % CH < 32, pltpu.roll(x, RET_W - 32, 1), pltpu.roll(x, 32, 1))


def _ret_consts():
    ii = lax.broadcasted_iota(jnp.int32, (CH, CH), 0).astype(f32)
    jj = lax.broadcasted_iota(jnp.int32, (CH, CH), 1).astype(f32)
    intra, cross, tail, cd = [], [], [], []
    for h in range(RET_H):
        lg = math.log1p(-(2.0 ** (-5.0 - h)))
        intra.append(jnp.exp(jnp.abs(ii - jj) * lg))
        cross.append(jnp.exp((ii + 1.0) * lg))
        tail.append(jnp.exp((CH - 1.0 - ii) * lg))
        cd.append(jnp.full((CH, CH), math.exp(CH * lg), f32))
    return jnp.stack(intra), jnp.stack(cross), jnp.stack(tail), jnp.stack(cd)


def _ret_chunk(consts, q, k, v, st):
    intra, cross, tail, cd = consts
    s = _bmm('hid,hjd->hij', q, k) * intra
    o = _bmm('hij,hje->hie', s, v) + _bmm('hid,hde->hie', q * cross, st)
    st2 = st * cd + _bmm('hjd,hje->hde', k * tail, v)
    mu = jnp.mean(o, -1, keepdims=True)
    oc = o - mu
    on = oc * lax.rsqrt(jnp.mean(oc * oc, -1, keepdims=True) + 1e-5)
    return on, st2


def _ret_fwd(hr, cosw, sinw, gam):
    T = hr.shape[0]
    R = RB_RET
    nc = R // CH

    def body(h_ref, c_ref, s_ref, g_ref, o_ref, st_ref, st, wide):
        @pl.when(pl.program_id(0) == 0)
        def _():
            st[...] = jnp.zeros_like(st)

        consts = _ret_consts()
        cw, sw = c_ref[...], s_ref[...]
        q, k = h_ref[:, 0:RET_W], h_ref[:, RET_W:2 * RET_W]
        qh = _split_heads((q * cw + _partner(q) * sw) * 0.125, RET_H)
        kh = _split_heads(k * cw + _partner(k) * sw, RET_H)
        vh = _split_heads(h_ref[:, 2 * RET_W:3 * RET_W], RET_H)
        outs = []
        s_cur = st[...]
        for c in range(nc):
            sl = slice(c * RET_H, (c + 1) * RET_H)
            st_ref[c] = s_cur
            on, s_cur = _ret_chunk(consts, qh[sl], kh[sl], vh[sl], s_cur)
            outs.append(on)
        st[...] = s_cur
        _merge_heads(wide, jnp.concatenate(outs, axis=0), RET_H)
        o_ref[...] = wide[...] * g_ref[...] * _silu(h_ref[:, 3 * RET_W:4 * RET_W])

    blk = pl.BlockSpec((R, RET_W), lambda i: (i, 0))
    return pl.pallas_call(
        body, grid=(T // R,),
        in_specs=[pl.BlockSpec((R, D), lambda i: (i, 0)), blk, blk, _full_spec((1, RET_W))],
        out_specs=[blk, pl.BlockSpec((nc, RET_H, CH, CH), lambda i: (i, 0, 0, 0))],
        out_shape=[jax.ShapeDtypeStruct((T, RET_W), f32), jax.ShapeDtypeStruct((T // CH, RET_H, CH, CH), f32)],
        scratch_shapes=[pltpu.VMEM((RET_H, CH, CH), f32), pltpu.VMEM((R, RET_W), f32)],
        compiler_params=_cparams(("arbitrary",)), name="ret_fwd")(hr, cosw, sinw, gam)


def _ret_bwd(hr, cosw, sinw, gam, states, dout):
    T = hr.shape[0]
    R = RB_RET
    nc = R // CH
    nb = T // R

    def body(h_ref, c_ref, s_ref, g_ref, st_ref, do_ref, dh_ref, dgam_ref, dst, wide):
        @pl.when(pl.program_id(0) == 0)
        def _():
            dst[...] = jnp.zeros_like(dst)
            dgam_ref[...] = jnp.zeros_like(dgam_ref)

        consts = _ret_consts()
        cw, sw = c_ref[...], s_ref[...]
        q, k = h_ref[:, 0:RET_W], h_ref[:, RET_W:2 * RET_W]
        gr = h_ref[:, 3 * RET_W:4 * RET_W]
        qh = _split_heads((q * cw + _partner(q) * sw) * 0.125, RET_H)
        kh = _split_heads(k * cw + _partner(k) * sw, RET_H)
        vh = _split_heads(h_ref[:, 2 * RET_W:3 * RET_W], RET_H)
        do = do_ref[...]
        gam = g_ref[...]
        sg = _silu(gr)
        don = _split_heads(do * gam * sg, RET_H)
        ons, dqs, dks, dvs = [None] * nc, [None] * nc, [None] * nc, [None] * nc
        ds = dst[...]
        for c in reversed(range(nc)):
            sl = slice(c * RET_H, (c + 1) * RET_H)
            (on, _), vjp = jax.vjp(functools.partial(_ret_chunk, consts), qh[sl], kh[sl], vh[sl], st_ref[c])
            dqs[c], dks[c], dvs[c], ds = vjp((don[sl], ds))
            ons[c] = on
        dst[...] = ds
        _merge_heads(wide, jnp.concatenate(ons, axis=0), RET_H)
        onw = wide[...]
        dgam_ref[...] += jnp.sum(do * onw * sg, 0, keepdims=True)
        dh_ref[:, 3 * RET_W:4 * RET_W] = do * onw * gam * _dsilu(gr)
        _merge_heads(wide, jnp.concatenate(dqs, axis=0), RET_H)
        u = wide[...] * 0.125
        dh_ref[:, 0:RET_W] = u * cw + _partner(u * sw)
        _merge_heads(wide, jnp.concatenate(dks, axis=0), RET_H)
        u = wide[...]
        dh_ref[:, RET_W:2 * RET_W] = u * cw + _partner(u * sw)
        _merge_heads(dh_ref, jnp.concatenate(dvs, axis=0), RET_H, col0=2 * RET_W)

    blk = pl.BlockSpec((R, RET_W), lambda i: (nb - 1 - i, 0))
    return pl.pallas_call(
        body, grid=(nb,),
        in_specs=[pl.BlockSpec((R, D), lambda i: (nb - 1 - i, 0)), blk, blk, _full_spec((1, RET_W)),
                  pl.BlockSpec((nc, RET_H, CH, CH), lambda i: (nb - 1 - i, 0, 0, 0)), blk],
        out_specs=[pl.BlockSpec((R, D), lambda i: (nb - 1 - i, 0)), _full_spec((1, RET_W))],
        out_shape=[jax.ShapeDtypeStruct((T, D), f32), jax.ShapeDtypeStruct((1, RET_W), f32)],
        scratch_shapes=[pltpu.VMEM((RET_H, CH, CH), f32), pltpu.VMEM((R, RET_W), f32)],
        compiler_params=_cparams(("arbitrary",)), name="ret_bwd")(hr, cosw, sinw, gam, states, dout)


def _lru_ab(xc, wa, ba, wx, bx, lam):
    r = _sigmoid(_dot(xc, wa) + ba)
    i = _sigmoid(_dot(xc, wx) + bx)
    la = 8.0 * r * (-_softplus(-lam))
    a = jnp.exp(la)
    em = jnp.tanh(la) * (jnp.exp(2.0 * la) + 1.0)
    return a, jnp.sqrt(-em) * (i * xc)


def _lru_out(h, gate):
    return h * _gelu(gate)


def _scan_fwd(a, b):
    R = a.shape[0]
    row = lax.broadcasted_iota(jnp.int32, a.shape, 0)
    d = 1
    while d < R:
        m = row >= d
        b = jnp.where(m, a * pltpu.roll(b, d, 0) + b, b)
        a = jnp.where(m, a * pltpu.roll(a, d, 0), a)
        d *= 2
    return a, b


def _scan_bwd(a, b):
    R = a.shape[0]
    row = lax.broadcasted_iota(jnp.int32, a.shape, 0)
    d = 1
    while d < R:
        m = row < R - d
        b = jnp.where(m, a * pltpu.roll(b, R - d, 0) + b, b)
        a = jnp.where(m, a * pltpu.roll(a, R - d, 0), a)
        d *= 2
    return b


def _lru_fwd(hl, cw, cb, wa, ba, wx, bx, lam):
    T = hl.shape[0]
    R = RB_LRU
    W = LRU_W

    def body(h_ref, t_ref, cw_ref, cb_ref, wa_ref, ba_ref, wx_ref, bx_ref, lam_ref, o_ref, hs_ref, carry, ext):
        first = pl.program_id(0) == 0

        @pl.when(first)
        def _():
            carry[...] = jnp.zeros_like(carry)

        tail = jnp.where(first, 0.0, t_ref[:, 0:W])
        xc = _conv_fwd(ext, h_ref[:, 0:W], tail, cw_ref[...], R) + cb_ref[...]
        a, b = _lru_ab(xc, wa_ref[...], ba_ref[...], wx_ref[...], bx_ref[...], lam_ref[...])
        ap, hloc = _scan_fwd(a, b)
        h = hloc + ap * carry[0:1, :]
        carry[...] = jnp.broadcast_to(h[R - 1:R, :], carry.shape)
        hs_ref[...] = h
        o_ref[...] = _lru_out(h, h_ref[:, W:2 * W])

    vec = _full_spec((1, W))
    blk = pl.BlockSpec((R, W), lambda i: (i, 0))
    return pl.pallas_call(
        body, grid=(T // R,),
        in_specs=[pl.BlockSpec((R, 2 * W), lambda i: (i, 0)), _prev_tail_spec(R, 2 * W), _full_spec((4, W)), vec,
                  _full_spec((W, W)), vec, _full_spec((W, W)), vec, vec],
        out_specs=[blk, blk], out_shape=[jax.ShapeDtypeStruct((T, W), f32)] * 2,
        scratch_shapes=[pltpu.VMEM((8, W), f32), pltpu.VMEM((R + 8, W), f32)],
        compiler_params=_cparams(("arbitrary",)), name="lru_fwd")(hl, hl, cw, cb, wa, ba, wx, bx, lam)


def _lru_bwd(hl, hs, cw, cb, wa, ba, wx, bx, lam, dout):
    T = hl.shape[0]
    R = RB_LRU
    W = LRU_W
    nb = T // R

    def body(h_ref, t_ref, hs_ref, hst_ref, cw_ref, cb_ref, wa_ref, ba_ref, wx_ref, bx_ref, lam_ref, do_ref,
             dh_ref, dcw_ref, dcb_ref, dwa_ref, dba_ref, dwx_ref, dbx_ref, dlam_ref, carry_g, carry_dy, ext, ext2):
        i = pl.program_id(0)
        last_blk = i == 0
        first_blk = i == nb - 1

        @pl.when(last_blk)
        def _():
            carry_g[...] = jnp.zeros_like(carry_g)
            carry_dy[...] = jnp.zeros_like(carry_dy)
            for r in (dcw_ref, dcb_ref, dwa_ref, dba_ref, dwx_ref, dbx_ref, dlam_ref):
                r[...] = jnp.zeros_like(r)

        tail = jnp.where(first_blk, 0.0, t_ref[:, 0:W])
        xc = _conv_fwd(ext, h_ref[:, 0:W], tail, cw_ref[...], R) + cb_ref[...]
        (a, _), vjp_ab = jax.vjp(_lru_ab, xc, wa_ref[...], ba_ref[...], wx_ref[...], bx_ref[...], lam_ref[...])
        hs = hs_ref[...]
        _, vjp_out = jax.vjp(_lru_out, hs, h_ref[:, W:2 * W])
        dh, dgate = vjp_out(do_ref[...])
        row = lax.broadcasted_iota(jnp.int32, (R, W), 0)
        dh = jnp.where(row == R - 1, dh + carry_g[0:1, :], dh)
        a_up = jnp.where(row == R - 1, 0.0, pltpu.roll(a, R - 1, 0))
        g = _scan_bwd(a_up, dh)
        carry_g[...] = jnp.broadcast_to(a[0:1, :] * g[0:1, :], carry_g.shape)
        hprev0 = jnp.where(first_blk, 0.0, hst_ref[7:8, :])
        hprev = jnp.where(row == 0, hprev0, pltpu.roll(hs, 1, 0))
        dxc, dwa, dba, dwx, dbx, dlam = vjp_ab((g * hprev, g))
        dwa_ref[...] += dwa
        dba_ref[...] += dba
        dwx_ref[...] += dwx
        dbx_ref[...] += dbx
        dlam_ref[...] += dlam
        dcb_ref[...] += jnp.sum(dxc, 0, keepdims=True)
        dx, dcw = _conv_bwd(ext, ext2, dxc, carry_dy[...], cw_ref[...], R)
        carry_dy[...] = dxc[0:8, :]
        dcw_ref[...] += dcw
        dh_ref[:, 0:W] = dx
        dh_ref[:, W:2 * W] = dgate

    vec = _full_spec((1, W))
    mat = _full_spec((W, W))
    blk = pl.BlockSpec((R, W), lambda i: (nb - 1 - i, 0))
    blk2 = pl.BlockSpec((R, 2 * W), lambda i: (nb - 1 - i, 0))
    return pl.pallas_call(
        body, grid=(nb,),
        in_specs=[blk2, _prev_tail_spec_rev(R, 2 * W, nb), blk, _prev_tail_spec_rev(R, W, nb), _full_spec((4, W)), vec,
                  mat, vec, mat, vec, vec, blk],
        out_specs=[blk2, _full_spec((4, W)), vec, mat, vec, mat, vec, vec],
        out_shape=[jax.ShapeDtypeStruct((T, 2 * W), f32), jax.ShapeDtypeStruct((4, W), f32),
                   jax.ShapeDtypeStruct((1, W), f32), jax.ShapeDtypeStruct((W, W), f32),
                   jax.ShapeDtypeStruct((1, W), f32), jax.ShapeDtypeStruct((W, W), f32),
                   jax.ShapeDtypeStruct((1, W), f32), jax.ShapeDtypeStruct((1, W), f32)],
        scratch_shapes=[pltpu.VMEM((8, W), f32), pltpu.VMEM((8, W), f32), pltpu.VMEM((R + 8, W), f32),
                        pltpu.VMEM((R + 8, W), f32)],
        compiler_params=_cparams(("arbitrary",)), name="lru_bwd")(hl, hl, hs, hs, cw, cb, wa, ba, wx, bx, lam, dout)


def _gdn_local(qs, ks, vs, gb, bb):
    B = qs.shape[0]
    ii = lax.broadcasted_iota(jnp.int32, (B, CH, CH), 1)
    jj = lax.broadcasted_iota(jnp.int32, (B, CH, CH), 2)
    q = qs * lax.rsqrt(jnp.sum(qs * qs, -1, keepdims=True) + 1e-6)
    k = ks * lax.rsqrt(jnp.sum(ks * ks, -1, keepdims=True) + 1e-6)
    gc = _cumsum_rows(gb)
    gct = jnp.swapaxes(gc, 1, 2)
    decay = jnp.where(ii >= jj, jnp.exp(jnp.minimum(gc - gct, 0.0)), 0.0)
    kk = _bmm('bid,bjd->bij', k, k)
    inv = _neumann_inv(-jnp.where(ii > jj, bb * kk * decay, 0.0))
    egc = jnp.exp(gc)
    u = _bmm3('bij,bje->bie', inv, vs * bb)
    w = _bmm3('bij,bje->bie', inv, k * (bb * egc))
    qk = _bmm('bid,bjd->bij', q, k) * (0.125 * decay)
    glast = gc[:, CH - 1:CH, :]
    return u, w, qk, q * (0.125 * egc), k * jnp.exp(glast - gc), jnp.exp(jnp.broadcast_to(glast, gc.shape))


def _gdn_step(st, u, w, qk, qd, kt, egl, z, gn):
    vnew = u - _bmm('hcd,hde->hce', w, st)
    o = _bmm('hcd,hde->hce', qd, st) + _bmm('hij,hje->hie', qk, vnew)
    st2 = st * egl + _bmm('hcd,hce->hde', kt, vnew)
    out = o * lax.rsqrt(jnp.mean(o * o, -1, keepdims=True) + 1e-6) * gn * _silu(z)
    return out, st2


def _gdn_scalars(ab, alog, dtb):
    sp = _softplus(ab + dtb)
    return -jnp.exp(alog) * sp, _sigmoid(ab)


def _bcast_heads(blk, lane0, H):
    R = blk.shape[0]
    n = R // CH
    parts = [jnp.broadcast_to(blk[:, lane0 + h:lane0 + h + 1], (R, CH)).reshape(n, CH, CH) for h in range(H)]
    return jnp.stack(parts, axis=1).reshape(n * H, CH, CH)


def _unbcast_heads(x, lane0, H):
    n = x.shape[0] // H
    R = n * CH
    s = jnp.sum(x, axis=2, keepdims=True).reshape(n, H, CH, 1)
    lane = lax.broadcasted_iota(jnp.int32, (R, 128), 1)
    acc = jnp.zeros((R, 128), f32)
    for h in range(H):
        acc = acc + jnp.where(lane == lane0 + h, jnp.broadcast_to(s[:, h].reshape(R, 1), (R, 128)), 0.0)
    return acc


def _gdn_fwd(hg, cw, alog, dtb, gn):
    T = hg.shape[0]
    R = RB_GDN
    nc = R // CH
    W3 = 3 * GDN_W
    H = GDN_H

    def body(h_ref, t_ref, cw_ref, al_ref, dt_ref, gn_ref, o_ref, st_ref, st, ext):
        first = pl.program_id(0) == 0

        @pl.when(first)
        def _():
            st[...] = jnp.zeros_like(st)

        tail = jnp.where(first, 0.0, t_ref[:, 0:W3])
        y = _silu(_conv_fwd(ext, h_ref[:, 0:W3], tail, cw_ref[...], R))
        qs, ks, vs = (_split_heads(y[:, j * GDN_W:(j + 1) * GDN_W], H) for j in range(3))
        zh = _split_heads(h_ref[:, W3:W3 + GDN_W], H)
        g, beta = _gdn_scalars(h_ref[:, W3 + GDN_W:GDN_IN], al_ref[...], dt_ref[...])
        loc = _gdn_local(qs, ks, vs, _bcast_heads(g, 0, H), _bcast_heads(beta, H, H))
        gnv = gn_ref[...]
        outs = []
        s_cur = st[...]
        for c in range(nc):
            sl = slice(c * H, (c + 1) * H)
            st_ref[c] = s_cur
            out, s_cur = _gdn_step(s_cur, *(t[sl] for t in loc), zh[sl], gnv)
            outs.append(out)
        st[...] = s_cur
        _merge_heads(o_ref, jnp.concatenate(outs, axis=0), H)

    return pl.pallas_call(
        body, grid=(T // R,),
        in_specs=[pl.BlockSpec((R, GDN_IN), lambda i: (i, 0)), _prev_tail_spec(R, GDN_IN), _full_spec((4, W3)),
                  _full_spec((1, 128)), _full_spec((1, 128)), _full_spec((1, CH))],
        out_specs=[pl.BlockSpec((R, GDN_W), lambda i: (i, 0)), pl.BlockSpec((nc, H, CH, CH), lambda i: (i, 0, 0, 0))],
        out_shape=[jax.ShapeDtypeStruct((T, GDN_W), f32), jax.ShapeDtypeStruct((T // CH, H, CH, CH), f32)],
        scratch_shapes=[pltpu.VMEM((H, CH, CH), f32), pltpu.VMEM((R + 8, W3), f32)],
        compiler_params=_cparams(("arbitrary",)), name="gdn_fwd")(hg, hg, cw, alog, dtb, gn)


def _gdn_bwd(hg, cw, alog, dtb, gn, states, dout):
    T = hg.shape[0]
    R = RB_GDN
    nc = R // CH
    nb = T // R
    W3 = 3 * GDN_W
    H = GDN_H

    def body(h_ref, t_ref, cw_ref, al_ref, dt_ref, gn_ref, st_ref, do_ref,
             dh_ref, dcw_ref, dal_ref, ddt_ref, dgn_ref, dst, carry_dy, ext, ext2, wide):
        i = pl.program_id(0)
        first_blk = i == nb - 1

        @pl.when(i == 0)
        def _():
            dst[...] = jnp.zeros_like(dst)
            carry_dy[...] = jnp.zeros_like(carry_dy)
            for r in (dcw_ref, dal_ref, ddt_ref, dgn_ref):
                r[...] = jnp.zeros_like(r)

        tail = jnp.where(first_blk, 0.0, t_ref[:, 0:W3])
        ypre = _conv_fwd(ext, h_ref[:, 0:W3], tail, cw_ref[...], R)
        y = _silu(ypre)
        qs, ks, vs = (_split_heads(y[:, j * GDN_W:(j + 1) * GDN_W], H) for j in range(3))
        zh = _split_heads(h_ref[:, W3:W3 + GDN_W], H)
        ab = h_ref[:, W3 + GDN_W:GDN_IN]
        alog, dtb = al_ref[...], dt_ref[...]
        g, beta = _gdn_scalars(ab, alog, dtb)
        loc, vjp_loc = jax.vjp(_gdn_local, qs, ks, vs, _bcast_heads(g, 0, H), _bcast_heads(beta, H, H))
        doh = _split_heads(do_ref[...], H)
        gnv = gn_ref[...]
        dloc = [[None] * nc for _ in range(6)]
        dzs = [None] * nc
        ds = dst[...]
        dgn = jnp.zeros((1, CH), f32)
        for c in reversed(range(nc)):
            sl = slice(c * H, (c + 1) * H)
            _, vjp = jax.vjp(_gdn_step, st_ref[c], *(t[sl] for t in loc), zh[sl], gnv)
            grads = vjp((doh[sl], ds))
            ds = grads[0]
            for j in range(6):
                dloc[j][c] = grads[1 + j]
            dzs[c] = grads[7]
            dgn = dgn + grads[8]
        dst[...] = ds
        dgn_ref[...] += dgn
        dqs, dks, dvs, dgb, dbb = vjp_loc(tuple(jnp.concatenate(d, axis=0) for d in dloc))
        lane = lax.broadcasted_iota(jnp.int32, (R, 128), 1)
        dg = _unbcast_heads(dgb, 0, H)
        dbeta = _unbcast_heads(dbb, H, H)
        da = dg * (-jnp.exp(alog)) * _sigmoid(ab + dtb)
        dh_ref[:, W3 + GDN_W:GDN_IN] = jnp.where(lane < H, da, dbeta * beta * (1.0 - beta))
        ddt_ref[...] += jnp.sum(jnp.where(lane < H, da, 0.0), 0, keepdims=True)
        dal_ref[...] += jnp.sum(jnp.where(lane < H, dg * g, 0.0), 0, keepdims=True)
        _merge_heads(dh_ref, jnp.concatenate(dzs, axis=0), H, col0=W3)
        for j, dpart in enumerate((dqs, dks, dvs)):
            _merge_heads(wide, dpart, H, col0=j * GDN_W)
        dy = wide[...] * _dsilu(ypre)
        dx, dcw = _conv_bwd(ext, ext2, dy, carry_dy[...], cw_ref[...], R)
        carry_dy[...] = dy[0:8, :]
        dcw_ref[...] += dcw
        dh_ref[:, 0:W3] = dx

    blk = pl.BlockSpec((R, GDN_IN), lambda i: (nb - 1 - i, 0))
    return pl.pallas_call(
        body, grid=(nb,),
        in_specs=[blk, _prev_tail_spec_rev(R, GDN_IN, nb), _full_spec((4, W3)), _full_spec((1, 128)),
                  _full_spec((1, 128)), _full_spec((1, CH)),
                  pl.BlockSpec((nc, H, CH, CH), lambda i: (nb - 1 - i, 0, 0, 0)),
                  pl.BlockSpec((R, GDN_W), lambda i: (nb - 1 - i, 0))],
        out_specs=[blk, _full_spec((4, W3)), _full_spec((1, 128)), _full_spec((1, 128)), _full_spec((1, CH))],
        out_shape=[jax.ShapeDtypeStruct((T, GDN_IN), f32), jax.ShapeDtypeStruct((4, W3), f32),
                   jax.ShapeDtypeStruct((1, 128), f32), jax.ShapeDtypeStruct((1, 128), f32),
                   jax.ShapeDtypeStruct((1, CH), f32)],
        scratch_shapes=[pltpu.VMEM((H, CH, CH), f32), pltpu.VMEM((8, W3), f32), pltpu.VMEM((R + 8, W3), f32),
                        pltpu.VMEM((R + 8, W3), f32), pltpu.VMEM((R, W3), f32)],
        compiler_params=_cparams(("arbitrary",)), name="gdn_bwd")(hg, hg, cw, alog, dtb, gn, states, dout)


def _block_diag(w):
    out = jnp.zeros((LRU_W, LRU_W), w.dtype)
    for g in range(w.shape[0]):
        out = lax.dynamic_update_slice(out, w[g], (g * CH, g * CH))
    return out


def _block_diag_t(w):
    return jnp.stack([w[g * CH:(g + 1) * CH, g * CH:(g + 1) * CH] for g in range(LRU_W // CH)])


def _pad_lanes(v, n=128):
    return jnp.pad(v, (0, n - v.shape[0]))[None, :]


def _local_step(x, p, positions, target, G, sm):
    pd, pr, pinl, ping, wpp = G['pd'], G['pr'], G['pinl'], G['ping'], G['wpp']
    p384 = jnp.transpose(G['p384'], (1, 2, 0, 3)).reshape(-1, D, NDEV * FSP)
    cosw, sinw = _rope_tables(positions)
    saved = []
    h = x
    for l in range(DEPTH):
        v = lambda n: sm[n][l][None, :]
        z1, x1, g1, u1 = _ffn_fwd(h, p384, pd, v('ln_ffn1_g'), v('ln_ffn1_b'), l, 0)
        hr, hl, hg = _proj_in(x1, pr, pinl, ping, l)
        o_r, rst = _ret_fwd(hr, cosw, sinw, v('ret_norm_g'))
        lru_args = (sm['lru_conv_w'][l], v('lru_conv_b'), _block_diag(sm['lru_w_a'][l]), v('lru_b_a'),
                    _block_diag(sm['lru_w_x'][l]), v('lru_b_x'), v('lru_lambda'))
        o_l, hs = _lru_fwd(hl, *lru_args)
        gdn_args = (sm['gdn_conv_w'][l], _pad_lanes(sm['gdn_a_log'][l]), _pad_lanes(sm['gdn_dt_bias'][l]),
                    v('gdn_norm_g'))
        o_g, gst = _gdn_fwd(hg, *gdn_args)
        z2, x2 = _mix_out(x1, o_r, o_l, o_g, pr, v('ln_mix_g'), v('ln_mix_b'), l)
        z3, x3, g2, u2 = _ffn_fwd(x2, p384, pd, v('ln_ffn2_g'), v('ln_ffn2_b'), l, 1, ple=(p[l], pr, wpp[l]))
        saved.append((h, z1, x1, hr, hl, hg, o_r, rst, o_l, hs, lru_args, o_g, gst, gdn_args, z2, x2, z3,
                      g1, u1, g2, u2))
        h = x3
    d, loss = _loss_grad(h, target)

    big = {k: [None] * DEPTH for k in ('p384', 'pd', 'pr', 'pinl', 'ping', 'ppp')}
    small = {n: [None] * DEPTH for n in SMALL}
    for l in reversed(range(DEPTH)):
        (x0, z1, x1, hr, hl, hg, o_r, rst, o_l, hs, lru_args, o_g, gst, gdn_args, z2, x2, z3,
         g1, u1, g2, u2) = saved[l]
        v = lambda n: sm[n][l][None, :]
        d2, dg2, du2, a2, dy2, small['ln_ffn2_g'][l], small['ln_ffn2_b'][l], dgp, dpj = _ffn_bwd(
            x2, z3, d, g2, u2, p384, pd, v('ln_ffn2_g'), l, 1, ple=(p[l], pr, wpp[l]))
        dxb, dzb, do_r, do_l, do_g, small['ln_mix_g'][l], small['ln_mix_b'][l] = _mix_out_bwd(
            z2, d2, pr, v('ln_mix_g'), l)
        dhr, small['ret_norm_g'][l] = _ret_bwd(hr, cosw, sinw, v('ret_norm_g'), rst, do_r)
        (dhl, small['lru_conv_w'][l], small['lru_conv_b'][l], dwa, small['lru_b_a'][l], dwx, small['lru_b_x'][l],
         small['lru_lambda'][l]) = _lru_bwd(hl, hs, *lru_args, do_l)
        small['lru_w_a'][l], small['lru_w_x'][l] = _block_diag_t(dwa), _block_diag_t(dwx)
        dhg, small['gdn_conv_w'][l], dal, ddt, small['gdn_norm_g'][l] = _gdn_bwd(hg, *gdn_args, gst, do_g)
        small['gdn_a_log'][l], small['gdn_dt_bias'][l] = dal[:, 0:GDN_H], ddt[:, 0:GDN_H]
        d1 = _proj_in_bwd(dxb, dhr, dhl, dhg, pr, pinl, ping, l)
        d, dg1, du1, a1, dy1, small['ln_ffn1_g'][l], small['ln_ffn1_b'][l] = _ffn_bwd(
            x0, z1, d1, g1, u1, p384, pd, v('ln_ffn1_g'), l, 0)
        rows = lambda m: m.reshape(NDEV, m.shape[1] // NDEV, m.shape[2])
        big['p384'][l] = jnp.stack([_matmul_tn(x0, dg1, FSP, "dw_gate", FB), _matmul_tn(x0, du1, FSP, "dw_up", FB),
                                    _matmul_tn(x2, dg2, FSP, "dw_gate", FB), _matmul_tn(x2, du2, FSP, "dw_up", FB)],
                                   axis=1)
        big['pd'][l] = jnp.stack([rows(_matmul_tn(a1, dy1, D, "dw_down")),
                                  rows(_matmul_tn(a2, dy2, D, "dw_down"))], axis=1)
        dwo = jnp.concatenate([_matmul_tn(o_r, dzb, D, "dw_out_r"), _matmul_tn(o_l, dzb, D, "dw_out_l"),
                               _matmul_tn(o_g, dzb, D, "dw_out_g")], axis=1)
        big['pr'][l] = jnp.stack([rows(_matmul_tn(x1, dhr, D, "dw_in_r")), rows(dwo),
                                  rows(_matmul_tn(x2, dgp, D, "dw_ple_gate"))], axis=1)
        big['pinl'][l] = rows(_matmul_tn(x1, dhl, 2 * LRU_W, "dw_in_l"))
        big['ping'][l] = rows(_matmul_tn(x1, dhg, GDN_IN, "dw_in_g"))
        big['ppp'][l] = _matmul_tn(p[l], dpj, 128, "dw_ple_proj")
    big = {'p384': jnp.concatenate(big['p384'], axis=1), 'pd': jnp.concatenate(big['pd'], axis=1),
           'pr': jnp.concatenate(big['pr'], axis=1), 'pinl': jnp.stack(big['pinl'], axis=1),
           'ping': jnp.stack(big['ping'], axis=1), 'ppp': jnp.stack(big['ppp'], axis=1)}
    small = {n: jnp.stack([g.reshape(sm[n].shape[1:]) for g in gs]) for n, gs in small.items()}
    return loss, d, big, small


def _pack_big(ws, dtype=bf16):
    padc = lambda a, n: jnp.pad(a, ((0, 0), (0, 0), (0, n - a.shape[2])))
    padr = lambda a, n: jnp.pad(a, ((0, 0), (0, n - a.shape[1]), (0, 0)))
    per_layer = lambda arrs: jnp.stack(arrs, axis=1).reshape((-1,) + arrs[0].shape[1:])
    w_in = ws['w_in']
    out = {
        'p384': per_layer([padc(ws[n], FSP) for n in ('ffn1_w_gate', 'ffn1_w_up', 'ffn2_w_gate', 'ffn2_w_up')]),
        'pd': per_layer([padr(ws[n], FSP) for n in ('ffn1_w_down', 'ffn2_w_down')]),
        'pr': per_layer([w_in[:, :, 0:D], ws['w_out'], ws['ple_w_gate']]),
        'pinl': w_in[:, :, D:D + 2 * LRU_W],
        'ping': padc(w_in[:, :, D + 2 * LRU_W:D_IN], GDN_IN),
        'ppp': ws['ple_w_proj'],
    }
    return {k: a.astype(dtype) for k, a in out.items()}


def _exchange(arrays, scatter, name):
    n = len(arrays)

    def body(*refs):
        ins, outs = refs[:n], refs[n:2 * n]
        send_sems, recv_sems, local_sems = refs[2 * n:]
        x, y, c = lax.axis_index("x"), lax.axis_index("y"), lax.axis_index("c")
        me = 4 * x + 2 * y + c
        copies = []
        for i in range(n):
            src = ins[i].at[me] if scatter[i] else ins[i]
            cp = pltpu.make_async_copy(src, outs[i].at[me], local_sems.at[i])
            cp.start()
            copies.append(cp)
        sends = []
        for j in range(1, NDEV):
            peer = (me + j) % NDEV
            pid = (peer // 4, (peer // 2) % 2, peer % 2)
            for i in range(n):
                src = ins[i].at[peer] if scatter[i] else ins[i]
                cp = pltpu.make_async_remote_copy(
                    src_ref=src, dst_ref=outs[i].at[me], send_sem=send_sems.at[i, j - 1],
                    recv_sem=recv_sems.at[i, j - 1], device_id=pid, device_id_type=pl.DeviceIdType.MESH)
                cp.start()
                sends.append(cp)
        for j in range(1, NDEV):
            source = (me + NDEV - j) % NDEV
            sid = (source // 4, (source // 2) % 2, source % 2)
            for i in range(n):
                src = ins[i].at[me] if scatter[i] else ins[i]
                pltpu.make_async_remote_copy(
                    src_ref=src, dst_ref=outs[i].at[source], send_sem=send_sems.at[i, j - 1],
                    recv_sem=recv_sems.at[i, j - 1], device_id=sid, device_id_type=pl.DeviceIdType.MESH).wait_recv()
        for cp in sends:
            cp.wait_send()
        for cp in copies:
            cp.wait()

    hbm = pl.BlockSpec(memory_space=pltpu.HBM)
    out_shape = [jax.ShapeDtypeStruct(a.shape if s else (NDEV,) + a.shape, a.dtype) for a, s in zip(arrays, scatter)]
    return pl.pallas_call(
        body, in_specs=[hbm] * n, out_specs=[hbm] * n, out_shape=out_shape,
        scratch_shapes=[pltpu.SemaphoreType.DMA((n, NDEV - 1)), pltpu.SemaphoreType.DMA((n, NDEV - 1)),
                        pltpu.SemaphoreType.DMA((n,))],
        compiler_params=pltpu.CompilerParams(has_side_effects=True), name=name)(*arrays)


def _gather_two_level(arrays, name):
    n = len(arrays)

    def body(*refs):
        ins, outs = refs[:n], refs[n:2 * n]
        send_sems, recv_sems, local_sems = refs[2 * n:]
        x, y, c = lax.axis_index("x"), lax.axis_index("y"), lax.axis_index("c")
        me, sibling = (x, y, c), (x, y, 1 - c)
        chips = [(1 - x, y), (x, 1 - y), (1 - x, 1 - y)]
        slot = lambda d: 4 * d[0] + 2 * d[1] + d[2]

        def copy(i, k, block, to, src=None):
            return pltpu.make_async_remote_copy(
                src_ref=outs[i].at[slot(block)] if src is None else src, dst_ref=outs[i].at[slot(block)],
                send_sem=send_sems.at[i, k], recv_sem=recv_sems.at[i, k], device_id=to,
                device_id_type=pl.DeviceIdType.MESH)

        mine, first, passed = [], [], []
        for i in range(n):
            cp = pltpu.make_async_copy(ins[i], outs[i].at[slot(me)], local_sems.at[i])
            cp.start()
            mine.append(cp)
            first.append(copy(i, 0, me, sibling, src=ins[i]))
            first += [copy(i, 1 + j, me, (*chip, c), src=ins[i]) for j, chip in enumerate(chips)]
        for cp in first:
            cp.start()
        for i in range(n):
            for j, chip in enumerate(chips):
                copy(i, 1 + j, (*chip, c), me).wait_recv()
                cp = copy(i, 4 + j, (*chip, c), sibling)
                cp.start()
                passed.append(cp)
        for i in range(n):
            copy(i, 0, sibling, me).wait_recv()
            for j, chip in enumerate(chips):
                copy(i, 4 + j, (*chip, 1 - c), me).wait_recv()
        for cp in first + passed:
            cp.wait_send()
        for cp in mine:
            cp.wait()

    hbm = pl.BlockSpec(memory_space=pltpu.HBM)
    return pl.pallas_call(
        body, in_specs=[hbm] * n, out_specs=[hbm] * n,
        out_shape=[jax.ShapeDtypeStruct((NDEV,) + a.shape, a.dtype) for a in arrays],
        scratch_shapes=[pltpu.SemaphoreType.DMA((n, NDEV - 1)), pltpu.SemaphoreType.DMA((n, NDEV - 1)),
                        pltpu.SemaphoreType.DMA((n,))],
        compiler_params=pltpu.CompilerParams(has_side_effects=True), name=name)(*arrays)


def _scatter_pairs(arrays, name):
    n = len(arrays)

    def body(*refs):
        ins, owns, gots = refs[:n], refs[n:2 * n], refs[2 * n:3 * n]
        send_sems, recv_sems, local_sems = refs[3 * n:]
        x, y, c = lax.axis_index("x"), lax.axis_index("y"), lax.axis_index("c")
        sends, keeps = [], []
        for i in range(n):
            for q in range(4):
                cp = pltpu.make_async_copy(ins[i].at[2 * q + c], owns[i].at[q], local_sems.at[i, q])
                cp.start()
                keeps.append(cp)
                cp = pltpu.make_async_remote_copy(
                    src_ref=ins[i].at[2 * q + 1 - c], dst_ref=gots[i].at[q], send_sem=send_sems.at[i, q],
                    recv_sem=recv_sems.at[i, q], device_id=(x, y, 1 - c), device_id_type=pl.DeviceIdType.MESH)
                cp.start()
                sends.append(cp)
        for cp in sends:
            cp.wait_recv()
        for cp in sends:
            cp.wait_send()
        for cp in keeps:
            cp.wait()

    hbm = pl.BlockSpec(memory_space=pltpu.HBM)
    half = [jax.ShapeDtypeStruct((4,) + a.shape[1:], a.dtype) for a in arrays]
    out = pl.pallas_call(
        body, in_specs=[hbm] * n, out_specs=[hbm] * (2 * n), out_shape=half + half,
        scratch_shapes=[pltpu.SemaphoreType.DMA((n, 4)), pltpu.SemaphoreType.DMA((n, 4)),
                        pltpu.SemaphoreType.DMA((n, 4))],
        compiler_params=pltpu.CompilerParams(has_side_effects=True), name=name)(*arrays)
    return out[:n], out[n:]


def _pair_sum(own, got, name):
    def body(a_ref, b_ref, o_ref):
        o_ref[...] = (a_ref[...].astype(f32) + b_ref[...].astype(f32)).astype(bf16)

    spec = pl.BlockSpec((None, None) + own.shape[2:], lambda q, s: (q, s, 0, 0))
    return pl.pallas_call(
        body, grid=own.shape[:2], in_specs=[spec, spec], out_specs=spec,
        out_shape=jax.ShapeDtypeStruct(own.shape, bf16),
        compiler_params=_cparams(("arbitrary", "arbitrary")), name=name)(own, got)


def _scatter_chips(arrays, name):
    n = len(arrays)

    def body(*refs):
        ins, outs = refs[:n], refs[n:2 * n]
        send_sems, recv_sems, local_sems = refs[2 * n:]
        x, y, c = lax.axis_index("x"), lax.axis_index("y"), lax.axis_index("c")
        chip = 2 * x + y
        keeps, sends = [], []
        for i in range(n):
            cp = pltpu.make_async_copy(ins[i].at[chip], outs[i].at[chip], local_sems.at[i])
            cp.start()
            keeps.append(cp)
        for j in range(1, 4):
            peer = (chip + j) % 4
            for i in range(n):
                cp = pltpu.make_async_remote_copy(
                    src_ref=ins[i].at[peer], dst_ref=outs[i].at[chip], send_sem=send_sems.at[i, j - 1],
                    recv_sem=recv_sems.at[i, j - 1], device_id=(peer // 2, peer % 2, c),
                    device_id_type=pl.DeviceIdType.MESH)
                cp.start()
                sends.append(cp)
        for j in range(1, 4):
            source = (chip + 4 - j) % 4
            for i in range(n):
                pltpu.make_async_remote_copy(
                    src_ref=ins[i].at[chip], dst_ref=outs[i].at[source], send_sem=send_sems.at[i, j - 1],
                    recv_sem=recv_sems.at[i, j - 1], device_id=(source // 2, source % 2, c),
                    device_id_type=pl.DeviceIdType.MESH).wait_recv()
        for cp in sends:
            cp.wait_send()
        for cp in keeps:
            cp.wait()

    hbm = pl.BlockSpec(memory_space=pltpu.HBM)
    return pl.pallas_call(
        body, in_specs=[hbm] * n, out_specs=[hbm] * n,
        out_shape=[jax.ShapeDtypeStruct(a.shape, a.dtype) for a in arrays],
        scratch_shapes=[pltpu.SemaphoreType.DMA((n, 3)), pltpu.SemaphoreType.DMA((n, 3)),
                        pltpu.SemaphoreType.DMA((n,))],
        compiler_params=pltpu.CompilerParams(has_side_effects=True), name=name)(*arrays)


def _adam_math(w, g, m, v):
    m2 = ADAM_B1 * m + (1.0 - ADAM_B1) * g
    v2 = ADAM_B2 * v + (1.0 - ADAM_B2) * (g * g)
    m_hat = m2 / (1.0 - ADAM_B1 ** ADAM_STEP)
    v_hat = v2 / (1.0 - ADAM_B2 ** ADAM_STEP)
    return -ADAM_LR * (m_hat / (jnp.sqrt(v_hat) + ADAM_EPS) + ADAM_WD * w), m2, v2


def _adam_big(parts, w, m, v, name):
    L, rows, cols = w.shape

    def body(*refs):
        prefs = refs[:len(parts)]
        w_ref, m_ref, v_ref, g_ref, d_ref, m2_ref, v2_ref = refs[len(parts):]
        c0 = 0
        for pref in prefs:
            acc = pref[0].astype(f32)
            for s in range(1, pref.shape[0]):
                acc = acc + pref[s].astype(f32)
            width = min(acc.shape[1], cols - c0)
            g_ref[:, c0:c0 + width] = acc[0:rows, 0:width]
            c0 += width
        d, m2, v2 = _adam_math(w_ref[...], g_ref[...], m_ref[...], v_ref[...])
        d_ref[...] = d
        m2_ref[...] = m2
        v2_ref[...] = v2

    wspec = pl.BlockSpec((None, rows, cols), lambda l: (l, 0, 0))
    in_specs = [pl.BlockSpec((a.shape[0], None) + a.shape[2:],
                             functools.partial(lambda l, per, first: (0, per * l + first, 0, 0), per=per, first=first))
                for a, per, first in parts]
    return pl.pallas_call(
        body, grid=(L,), in_specs=in_specs + [wspec] * 3, out_specs=[wspec] * 4,
        out_shape=[jax.ShapeDtypeStruct(w.shape, f32)] * 4,
        compiler_params=_cparams(("arbitrary",)), name=name)(*[a for a, _, _ in parts], w, m, v)


def _sum_sources(stacked):
    rows = stacked.shape[1]

    def body(s_ref, o_ref):
        acc = s_ref[0]
        for s in range(1, NDEV):
            acc = acc + s_ref[s]
        o_ref[...] = acc

    return pl.pallas_call(body, out_shape=jax.ShapeDtypeStruct((rows, 128), f32), name="sum_small_grads")(stacked)


def _adam_small(w, g, m, v):
    def body(w_ref, g_ref, m_ref, v_ref, d_ref, m2_ref, v2_ref):
        d, m2, v2 = _adam_math(w_ref[...], g_ref[...], m_ref[...], v_ref[...])
        d_ref[...] = d
        m2_ref[...] = m2
        v2_ref[...] = v2

    return pl.pallas_call(body, out_shape=[jax.ShapeDtypeStruct(w.shape, f32)] * 3, name="adam_small")(w, g, m, v)


def _pack_rows(arrs):
    flat = []
    for a in arrs:
        a = a.reshape(-1)
        flat.append(jnp.pad(a, (0, (-a.shape[0]) % 1024)))
    return jnp.concatenate(flat).reshape(-1, 128)


def _unpack_rows(packed, shapes):
    out, off = [], 0
    flat = packed.reshape(-1)
    for s in shapes:
        n = math.prod(s)
        out.append(flat[off:off + n].reshape(s))
        off += n + (-n) % 1024
    return out


def _gather_conv(gathered, shape):
    L, K, c = shape
    return jnp.transpose(gathered, (1, 2, 0, 3)).reshape(L, K, NDEV * c)


def kernel(x, p, positions, ln_ffn1_g, ln_ffn1_b, ffn1_w_gate, ffn1_w_up, ffn1_w_down, w_in, ret_norm_g, lru_conv_w, lru_conv_b, lru_w_a, lru_b_a, lru_w_x, lru_b_x, lru_lambda, gdn_conv_w, gdn_a_log, gdn_dt_bias, gdn_norm_g, w_out, ln_mix_g, ln_mix_b, ffn2_w_gate, ffn2_w_up, ffn2_w_down, ple_w_gate, ple_w_proj, ln_ffn2_g, ln_ffn2_b, loss_target, m_ln_ffn1_g, m_ln_ffn1_b, m_ffn1_w_gate, m_ffn1_w_up, m_ffn1_w_down, m_w_in, m_ret_norm_g, m_lru_conv_w, m_lru_conv_b, m_lru_w_a, m_lru_b_a, m_lru_w_x, m_lru_b_x, m_lru_lambda, m_gdn_conv_w, m_gdn_a_log, m_gdn_dt_bias, m_gdn_norm_g, m_w_out, m_ln_mix_g, m_ln_mix_b, m_ffn2_w_gate, m_ffn2_w_up, m_ffn2_w_down, m_ple_w_gate, m_ple_w_proj, m_ln_ffn2_g, m_ln_ffn2_b, v_ln_ffn1_g, v_ln_ffn1_b, v_ffn1_w_gate, v_ffn1_w_up, v_ffn1_w_down, v_w_in, v_ret_norm_g, v_lru_conv_w, v_lru_conv_b, v_lru_w_a, v_lru_b_a, v_lru_w_x, v_lru_b_x, v_lru_lambda, v_gdn_conv_w, v_gdn_a_log, v_gdn_dt_bias, v_gdn_norm_g, v_w_out, v_ln_mix_g, v_ln_mix_b, v_ffn2_w_gate, v_ffn2_w_up, v_ffn2_w_down, v_ple_w_gate, v_ple_w_proj, v_ln_ffn2_g, v_ln_ffn2_b):
    args = locals()
    W = {n: args[n] for n in WEIGHTS}
    M = {n: args['m_' + n] for n in WEIGHTS}
    V = {n: args['v_' + n] for n in WEIGHTS}
    me = 4 * lax.axis_index("x") + 2 * lax.axis_index("y") + lax.axis_index("c")

    packed = _pack_big(W)
    conv_pack = _pack_rows([W[n] for n in CONV_SHARDED])
    keys = list(packed)
    gathered = _gather_two_level([packed[k] for k in keys] + [conv_pack], "gather_weights")
    G = dict(zip(keys, gathered[:-1]))
    G['wpp'] = jnp.transpose(G.pop('ppp'), (1, 2, 0, 3)).reshape(DEPTH, PLE, D)
    conv_all = gathered[-1]
    sm = {n: W[n] for n in SMALL}
    conv_shards = [_unpack_rows(conv_all[s], [W[n].shape for n in CONV_SHARDED]) for s in range(NDEV)]
    for i, n in enumerate(CONV_SHARDED):
        sm[n] = _gather_conv(jnp.stack([cs[i] for cs in conv_shards]), W[n].shape)

    loss, grad_x, big, small = _local_step(x[0], p[:, 0], positions.reshape(-1, 1), loss_target[0], G, sm)
    loss = lax.psum(loss[0, 0], ("x", "y", "c"))

    small_pack = _pack_rows([small[n] for n in SMALL])
    bkeys = list(big)
    owns, gots = _scatter_pairs([big[k] for k in bkeys], "scatter_pairs")
    pair = [_pair_sum(o, g, "pair_sum_" + k) for k, o, g in zip(bkeys, owns, gots)]
    R = dict(zip(bkeys, _scatter_chips(pair, "scatter_chips")))
    small_all = _exchange([small_pack], [False], "gather_small_grads")[0]
    small_sum = _unpack_rows(_sum_sources(small_all), [small[n].shape for n in SMALL])
    grads, delta, new_m, new_v = {}, {}, {}, {}
    for n, g in zip(SMALL, small_sum):
        if n in CONV_SHARDED:
            c = W[n].shape[2]
            g = lax.dynamic_slice_in_dim(g, me * c, c, axis=2)
        grads[n] = g

    big_parts = {
        'ffn1_w_gate': [(R['p384'], 4, 0)], 'ffn1_w_up': [(R['p384'], 4, 1)],
        'ffn2_w_gate': [(R['p384'], 4, 2)], 'ffn2_w_up': [(R['p384'], 4, 3)],
        'ffn1_w_down': [(R['pd'], 2, 0)], 'ffn2_w_down': [(R['pd'], 2, 1)],
        'w_in': [(R['pr'], 3, 0), (R['pinl'], 1, 0), (R['ping'], 1, 0)],
        'w_out': [(R['pr'], 3, 1)], 'ple_w_gate': [(R['pr'], 3, 2)], 'ple_w_proj': [(R['ppp'], 1, 0)],
    }
    for n in BIG:
        grads[n], delta[n], new_m[n], new_v[n] = _adam_big(big_parts[n], W[n], M[n], V[n], "adam_" + n)
    shapes = [W[n].shape for n in SMALL]
    d_s, m_s, v_s = _adam_small(*[_pack_rows([src[n] for n in SMALL]) for src in (W, grads, M, V)])
    for n, dd, mm, vv in zip(SMALL, _unpack_rows(d_s, shapes), _unpack_rows(m_s, shapes), _unpack_rows(v_s, shapes)):
        delta[n], new_m[n], new_v[n] = dd, mm, vv

    return (loss, grad_x[None], *[grads[n] for n in WEIGHTS], *[delta[n] for n in WEIGHTS],
            *[new_m[n] for n in WEIGHTS], *[new_v[n] for n in WEIGHTS])
```

```python
import functools
import math

import jax
import jax.numpy as jnp
from jax import lax
from jax.experimental import pallas as pl
from jax.experimental.pallas import tpu as pltpu

f32 = jnp.float32
bf16 = jnp.bfloat16

NDEV = 8
DEPTH = 2
D = 1024
FS = 352
FSP = 384
FB = 2
NF = NDEV // FB
PLE = 256
CH = 64
RET_H, GDN_H = 4, 6
RET_W, LRU_W, GDN_W = 256, 384, 384
GDN_IN = 1664
GDN_IN_REAL = 1548
D_IN = 3340
ALPHA = 4.0 ** 0.25
LN_EPS = 1e-5
ROPE_THETA = 10000.0
TM = 512
RB_RET, RB_LRU, RB_GDN = 512, 512, 256
VMEM_LIMIT = 56 * 1024 * 1024
ADAM_LR, ADAM_B1, ADAM_B2, ADAM_EPS, ADAM_WD, ADAM_STEP = 0.001, 0.9, 0.999, 1e-08, 0.01, 10

WEIGHTS = ['ln_ffn1_g', 'ln_ffn1_b', 'ffn1_w_gate', 'ffn1_w_up', 'ffn1_w_down', 'w_in', 'ret_norm_g', 'lru_conv_w',
           'lru_conv_b', 'lru_w_a', 'lru_b_a', 'lru_w_x', 'lru_b_x', 'lru_lambda', 'gdn_conv_w', 'gdn_a_log',
           'gdn_dt_bias', 'gdn_norm_g', 'w_out', 'ln_mix_g', 'ln_mix_b', 'ffn2_w_gate', 'ffn2_w_up', 'ffn2_w_down',
           'ple_w_gate', 'ple_w_proj', 'ln_ffn2_g', 'ln_ffn2_b']
BIG = ['ffn1_w_gate', 'ffn1_w_up', 'ffn1_w_down', 'w_in', 'w_out', 'ffn2_w_gate', 'ffn2_w_up', 'ffn2_w_down',
       'ple_w_gate', 'ple_w_proj']
SMALL = [n for n in WEIGHTS if n not in BIG]
CONV_SHARDED = {'lru_conv_w': LRU_W, 'gdn_conv_w': 3 * GDN_W}


def _cparams(sem=None):
    return pltpu.CompilerParams(dimension_semantics=sem, vmem_limit_bytes=VMEM_LIMIT)


def _sigmoid(x):
    return 1.0 / (1.0 + jnp.exp(-x))


def _silu(x):
    return x * _sigmoid(x)


def _dsilu(x):
    s = _sigmoid(x)
    return s * (1.0 + x * (1.0 - s))


def _softplus(x):
    return jnp.maximum(x, 0.0) + jnp.log(1.0 + jnp.exp(-jnp.abs(x)))


def _gelu(x):
    return 0.5 * x * (1.0 + jnp.tanh(0.7978845608028654 * (x + 0.044715 * x * x * x)))


def _dot(a, b):
    return jnp.dot(a.astype(bf16), b.astype(bf16), preferred_element_type=f32)


def _dot_nt(a, b):
    return lax.dot_general(a.astype(bf16), b.astype(bf16), (((1,), (1,)), ((), ())), preferred_element_type=f32)


def _dot_tn(a, b):
    return lax.dot_general(a.astype(bf16), b.astype(bf16), (((0,), (0,)), ((), ())), preferred_element_type=f32)


def _bmm(eq, a, b):
    return jnp.einsum(eq, a.astype(bf16), b.astype(bf16), preferred_element_type=f32)


def _split3(a):
    a1 = a.astype(bf16)
    r = a - a1.astype(f32)
    a2 = r.astype(bf16)
    return a1, a2, (r - a2.astype(f32)).astype(bf16)


def _bmm3(eq, a, b):
    a1, a2, _ = _split3(a)
    b1, b2, _ = _split3(b)
    e = lambda x, y: jnp.einsum(eq, x, y, preferred_element_type=f32)
    return e(a1, b1) + (e(a1, b2) + e(a2, b1))


def _tri_ones(B, upper=False):
    ii = lax.broadcasted_iota(jnp.int32, (B, CH, CH), 1)
    jj = lax.broadcasted_iota(jnp.int32, (B, CH, CH), 2)
    return jnp.where((ii <= jj) if upper else (ii >= jj), 1.0, 0.0).astype(bf16)


def _cumsum_mm(t, x):
    x1, x2, x3 = _split3(x)
    e = lambda y: jnp.einsum('bij,bjk->bik', t, y, preferred_element_type=f32)
    return e(x1) + (e(x2) + e(x3))


@jax.custom_vjp
def _cumsum_rows(x):
    return _cumsum_mm(_tri_ones(x.shape[0]), x)


def _cumsum_rows_fwd(x):
    return _cumsum_rows(x), None


def _cumsum_rows_bwd(_, g):
    return (_cumsum_mm(_tri_ones(g.shape[0], upper=True), g),)


_cumsum_rows.defvjp(_cumsum_rows_fwd, _cumsum_rows_bwd)


@jax.custom_vjp
def _neumann_inv(m):
    ii = lax.broadcasted_iota(jnp.int32, m.shape, 1)
    jj = lax.broadcasted_iota(jnp.int32, m.shape, 2)
    inv = jnp.where(ii == jj, 1.0, 0.0).astype(f32) + m
    mp = m
    for _ in range(5):
        mp = _bmm3('bij,bjk->bik', mp, mp)
        inv = inv + _bmm3('bij,bjk->bik', inv, mp)
    return inv


def _neumann_inv_fwd(m):
    inv = _neumann_inv(m)
    return inv, inv


def _neumann_inv_bwd(inv, g):
    return (_bmm3('bij,bkj->bik', _bmm3('bji,bjk->bik', inv, g), inv),)


_neumann_inv.defvjp(_neumann_inv_fwd, _neumann_inv_bwd)


def _ln_stats(z):
    mu = jnp.mean(z, -1, keepdims=True)
    zc = z - mu
    rstd = lax.rsqrt(jnp.mean(zc * zc, -1, keepdims=True) + LN_EPS)
    return zc * rstd, rstd


def _ln_bwd(z, g, dout):
    xh, rstd = _ln_stats(z)
    dxh = dout * g
    dz = rstd * (dxh - jnp.mean(dxh, -1, keepdims=True) - xh * jnp.mean(dxh * xh, -1, keepdims=True))
    return dz, jnp.sum(dout * xh, 0, keepdims=True), jnp.sum(dout, 0, keepdims=True)


def _full_spec(shape):
    nd = len(shape)
    return pl.BlockSpec(shape, lambda *_: (0,) * nd)


def _ffn_fwd(x, p384, pd, lg, lb, layer, which, ple=None):
    T = x.shape[0]
    sg, su, sd = 4 * layer + 2 * which, 4 * layer + 2 * which + 1, 2 * layer + which
    has_ple = ple is not None

    def body(*refs):
        if has_ple:
            (x_ref, wg_ref, wu_ref, wd_ref, lg_ref, lb_ref, p_ref, wpg_ref, wpp_ref,
             z_ref, o_ref, g_ref, u_ref, acc, xb_s) = refs
        else:
            x_ref, wg_ref, wu_ref, wd_ref, lg_ref, lb_ref, z_ref, o_ref, g_ref, u_ref, acc, xb_s = refs
        f = pl.program_id(1)

        @pl.when(f == 0)
        def _():
            x = x_ref[...]
            xb = x.astype(bf16)
            xb_s[...] = xb
            base = ALPHA * x
            if has_ple:
                gate = _sigmoid(_dot(xb, wpg_ref[...].reshape(D, D)))
                base = base + gate * _dot(p_ref[...], wpp_ref[...])
            acc[...] = base

        xb = xb_s[...]
        g = _dot(xb, wg_ref[...])
        u = _dot(xb, wu_ref[...])
        g_ref[...] = g.astype(bf16)
        u_ref[...] = u.astype(bf16)
        acc[...] += 0.5 * _dot(_silu(g) * u, wd_ref[...].reshape(FB * FSP, D))

        @pl.when(f == NF - 1)
        def _():
            z = acc[...]
            z_ref[...] = z
            o_ref[...] = _ln_stats(z)[0] * lg_ref[...] + lb_ref[...]

    row = pl.BlockSpec((TM, D), lambda i, f: (i, 0))
    in_specs = [row,
                pl.BlockSpec((None, D, FB * FSP), lambda i, f: (sg, 0, f)),
                pl.BlockSpec((None, D, FB * FSP), lambda i, f: (su, 0, f)),
                pl.BlockSpec((FB, None, FSP, D), lambda i, f: (f, sd, 0, 0)),
                _full_spec((1, D)), _full_spec((1, D))]
    args = [x, p384, p384, pd, lg, lb]
    if has_ple:
        p, pr, wpp = ple
        in_specs += [pl.BlockSpec((TM, PLE), lambda i, f: (i, 0)),
                     pl.BlockSpec((NDEV, None, 128, D), lambda i, f: (0, 3 * layer + 2, 0, 0)),
                     _full_spec((PLE, D))]
        args += [p, pr, wpp]
    hid = pl.BlockSpec((TM, FB * FSP), lambda i, f: (i, f))
    hshape = jax.ShapeDtypeStruct((T, NDEV * FSP), bf16)
    return pl.pallas_call(
        body, grid=(T // TM, NF), in_specs=in_specs, out_specs=[row, row, hid, hid],
        out_shape=[jax.ShapeDtypeStruct((T, D), f32)] * 2 + [hshape, hshape],
        scratch_shapes=[pltpu.VMEM((TM, D), f32), pltpu.VMEM((TM, D), bf16)],
        compiler_params=_cparams(("arbitrary", "arbitrary")), name=f"ffn{which + 1}_fwd")(*args)


def _ffn_bwd(x, z, dout, gs, us, p384, pd, lg, layer, which, ple=None):
    T = z.shape[0]
    TMB = TM // 2
    sg, su, sd = 4 * layer + 2 * which, 4 * layer + 2 * which + 1, 2 * layer + which
    has_ple = ple is not None

    def body(*refs):
        if has_ple:
            (z_ref, do_ref, g_ref, u_ref, wg_ref, wu_ref, wd_ref, lg_ref, x_ref, p_ref, wpg_ref, wpp_ref,
             dx_ref, dg_ref, du_ref, a_ref, dy_ref, dlg_ref, dlb_ref, dgp_ref, dpj_ref, acc, dyb) = refs
        else:
            (z_ref, do_ref, g_ref, u_ref, wg_ref, wu_ref, wd_ref, lg_ref,
             dx_ref, dg_ref, du_ref, a_ref, dy_ref, dlg_ref, dlb_ref, acc, dyb) = refs
        i, f = pl.program_id(0), pl.program_id(1)

        @pl.when(jnp.logical_and(i == 0, f == 0))
        def _():
            dlg_ref[...] = jnp.zeros_like(dlg_ref)
            dlb_ref[...] = jnp.zeros_like(dlb_ref)

        @pl.when(f == 0)
        def _():
            dz, dlg, dlb = _ln_bwd(z_ref[...], lg_ref[...], do_ref[...])
            dlg_ref[...] += dlg
            dlb_ref[...] += dlb
            dy = (0.5 * dz).astype(bf16)
            dyb[...] = dy
            dy_ref[...] = dy
            dx = ALPHA * dz
            if has_ple:
                wpg = wpg_ref[...].reshape(D, D)
                gate = _sigmoid(_dot(x_ref[...], wpg))
                proj = _dot(p_ref[...], wpp_ref[...])
                dgp = (dz * proj * gate * (1.0 - gate)).astype(bf16)
                dgp_ref[...] = dgp
                dpj_ref[...] = (dz * gate).astype(bf16)
                dx = dx + _dot_nt(dgp, wpg)
            acc[...] = dx

        g = g_ref[...].astype(f32)
        u = u_ref[...].astype(f32)
        da = _dot_nt(dyb[...], wd_ref[...].reshape(FB * FSP, D))
        sgm = _sigmoid(g)
        dg = (da * u * (sgm * (1.0 + g * (1.0 - sgm)))).astype(bf16)
        du = (da * (g * sgm)).astype(bf16)
        dg_ref[...] = dg
        du_ref[...] = du
        a_ref[...] = (g * sgm * u).astype(bf16)
        acc[...] += _dot_nt(dg, wg_ref[...]) + _dot_nt(du, wu_ref[...])

        @pl.when(f == NF - 1)
        def _():
            dx_ref[...] = acc[...]

    row = pl.BlockSpec((TMB, D), lambda i, f: (i, 0))
    hid = pl.BlockSpec((TMB, FB * FSP), lambda i, f: (i, f))
    vec = _full_spec((1, D))
    in_specs = [row, row, hid, hid,
                pl.BlockSpec((None, D, FB * FSP), lambda i, f: (sg, 0, f)),
                pl.BlockSpec((None, D, FB * FSP), lambda i, f: (su, 0, f)),
                pl.BlockSpec((FB, None, FSP, D), lambda i, f: (f, sd, 0, 0)),
                vec]
    args = [z, dout, gs, us, p384, p384, pd, lg]
    out_specs = [row, hid, hid, hid, row, vec, vec]
    hshape = jax.ShapeDtypeStruct((T, NDEV * FSP), bf16)
    out_shape = [jax.ShapeDtypeStruct((T, D), f32), hshape, hshape, hshape, jax.ShapeDtypeStruct((T, D), bf16),
                 jax.ShapeDtypeStruct((1, D), f32), jax.ShapeDtypeStruct((1, D), f32)]
    if has_ple:
        p, pr, wpp = ple
        in_specs += [row, pl.BlockSpec((TMB, PLE), lambda i, f: (i, 0)),
                     pl.BlockSpec((NDEV, None, 128, D), lambda i, f: (0, 3 * layer + 2, 0, 0)),
                     _full_spec((PLE, D))]
        args += [x, p, pr, wpp]
        out_specs += [row, row]
        out_shape += [jax.ShapeDtypeStruct((T, D), bf16)] * 2
    return pl.pallas_call(
        body, grid=(T // TMB, NF), in_specs=in_specs, out_specs=out_specs, out_shape=out_shape,
        scratch_shapes=[pltpu.VMEM((TMB, D), f32), pltpu.VMEM((TMB, D), bf16)],
        compiler_params=_cparams(("arbitrary", "arbitrary")), name=f"ffn{which + 1}_bwd")(*args)


def _matmul_tn(a, b, nb, name, nsub=1):
    T, M = a.shape
    N = b.shape[1]
    tk = min(T, 1024)
    nk = T // tk
    wide = nsub * nb

    def body(a_ref, b_ref, o_ref, acc):
        k = pl.program_id(1)

        @pl.when(k == 0)
        def _():
            acc[...] = jnp.zeros_like(acc)

        acc[...] += _dot_tn(a_ref[...], b_ref[...])

        @pl.when(k == nk - 1)
        def _():
            for j in range(nsub):
                o_ref[j] = acc[:, j * nb:(j + 1) * nb].astype(bf16)

    return pl.pallas_call(
        body, grid=(N // wide, nk),
        in_specs=[pl.BlockSpec((tk, M), lambda n, k: (k, 0)), pl.BlockSpec((tk, wide), lambda n, k: (k, n))],
        out_specs=pl.BlockSpec((nsub, M, nb), lambda n, k: (n, 0, 0)),
        out_shape=jax.ShapeDtypeStruct((N // nb, M, nb), bf16),
        scratch_shapes=[pltpu.VMEM((M, wide), f32)],
        compiler_params=_cparams(("arbitrary", "arbitrary")), name=name)(a, b)


def _proj_in(x, pr, pinl, ping, layer):
    T = x.shape[0]

    def body(x_ref, wr_ref, wl_ref, wg_ref, hr_ref, hl_ref, hg_ref):
        xb = x_ref[...].astype(bf16)
        hr_ref[...] = _dot(xb, wr_ref[...].reshape(D, D))
        hl_ref[...] = _dot(xb, wl_ref[...].reshape(D, 2 * LRU_W))
        hg_ref[...] = _dot(xb, wg_ref[...].reshape(D, GDN_IN))

    return pl.pallas_call(
        body, grid=(T // TM,),
        in_specs=[pl.BlockSpec((TM, D), lambda i: (i, 0)),
                  pl.BlockSpec((NDEV, None, 128, D), lambda i: (0, 3 * layer, 0, 0)),
                  pl.BlockSpec((NDEV, None, 128, 2 * LRU_W), lambda i: (0, layer, 0, 0)),
                  pl.BlockSpec((NDEV, None, 128, GDN_IN), lambda i: (0, layer, 0, 0))],
        out_specs=[pl.BlockSpec((TM, D), lambda i: (i, 0)), pl.BlockSpec((TM, 2 * LRU_W), lambda i: (i, 0)),
                   pl.BlockSpec((TM, GDN_IN), lambda i: (i, 0))],
        out_shape=[jax.ShapeDtypeStruct((T, D), f32), jax.ShapeDtypeStruct((T, 2 * LRU_W), f32),
                   jax.ShapeDtypeStruct((T, GDN_IN), f32)],
        compiler_params=_cparams(("arbitrary",)), name="proj_in")(x, pr, pinl, ping)


def _proj_in_bwd(base, dhr, dhl, dhg, pr, pinl, ping, layer):
    T = base.shape[0]

    def body(b_ref, dr_ref, dl_ref, dg_ref, wr_ref, wl_ref, wg_ref, o_ref):
        o_ref[...] = (b_ref[...] + _dot_nt(dr_ref[...], wr_ref[...].reshape(D, D))
                      + _dot_nt(dl_ref[...], wl_ref[...].reshape(D, 2 * LRU_W))
                      + _dot_nt(dg_ref[...], wg_ref[...].reshape(D, GDN_IN)))

    return pl.pallas_call(
        body, grid=(T // TM,),
        in_specs=[pl.BlockSpec((TM, D), lambda i: (i, 0)), pl.BlockSpec((TM, D), lambda i: (i, 0)),
                  pl.BlockSpec((TM, 2 * LRU_W), lambda i: (i, 0)), pl.BlockSpec((TM, GDN_IN), lambda i: (i, 0)),
                  pl.BlockSpec((NDEV, None, 128, D), lambda i: (0, 3 * layer, 0, 0)),
                  pl.BlockSpec((NDEV, None, 128, 2 * LRU_W), lambda i: (0, layer, 0, 0)),
                  pl.BlockSpec((NDEV, None, 128, GDN_IN), lambda i: (0, layer, 0, 0))],
        out_specs=pl.BlockSpec((TM, D), lambda i: (i, 0)),
        out_shape=jax.ShapeDtypeStruct((T, D), f32),
        compiler_params=_cparams(("arbitrary",)), name="proj_in_bwd")(base, dhr, dhl, dhg, pr, pinl, ping)


def _mix_out(x1, o_r, o_l, o_g, pr, lg, lb, layer):
    T = x1.shape[0]

    def body(x_ref, r_ref, l_ref, g_ref, w_ref, lg_ref, lb_ref, z_ref, o_ref):
        w = w_ref[...].reshape(D, D)
        z = (ALPHA * x_ref[...] + _dot(r_ref[...], w[0:RET_W]) + _dot(l_ref[...], w[RET_W:RET_W + LRU_W])
             + _dot(g_ref[...], w[RET_W + LRU_W:D]))
        z_ref[...] = z
        o_ref[...] = _ln_stats(z)[0] * lg_ref[...] + lb_ref[...]

    row = pl.BlockSpec((TM, D), lambda i: (i, 0))
    return pl.pallas_call(
        body, grid=(T // TM,),
        in_specs=[row, pl.BlockSpec((TM, RET_W), lambda i: (i, 0)), pl.BlockSpec((TM, LRU_W), lambda i: (i, 0)),
                  pl.BlockSpec((TM, GDN_W), lambda i: (i, 0)),
                  pl.BlockSpec((NDEV, None, 128, D), lambda i: (0, 3 * layer + 1, 0, 0)),
                  _full_spec((1, D)), _full_spec((1, D))],
        out_specs=[row, row], out_shape=[jax.ShapeDtypeStruct((T, D), f32)] * 2,
        compiler_params=_cparams(("arbitrary",)), name="mix_out")(x1, o_r, o_l, o_g, pr, lg, lb)


def _mix_out_bwd(z, dout, pr, lg, layer):
    T = z.shape[0]

    def body(z_ref, do_ref, w_ref, lg_ref, dxb_ref, dzb_ref, dr_ref, dl_ref, dg_ref, dlg_ref, dlb_ref):
        @pl.when(pl.program_id(0) == 0)
        def _():
            dlg_ref[...] = jnp.zeros_like(dlg_ref)
            dlb_ref[...] = jnp.zeros_like(dlb_ref)

        dz, dlg, dlb = _ln_bwd(z_ref[...], lg_ref[...], do_ref[...])
        dlg_ref[...] += dlg
        dlb_ref[...] += dlb
        dxb_ref[...] = ALPHA * dz
        dzb = dz.astype(bf16)
        dzb_ref[...] = dzb
        w = w_ref[...].reshape(D, D)
        dr_ref[...] = _dot_nt(dzb, w[0:RET_W])
        dl_ref[...] = _dot_nt(dzb, w[RET_W:RET_W + LRU_W])
        dg_ref[...] = _dot_nt(dzb, w[RET_W + LRU_W:D])

    row = pl.BlockSpec((TM, D), lambda i: (i, 0))
    vec = _full_spec((1, D))
    return pl.pallas_call(
        body, grid=(T // TM,),
        in_specs=[row, row, pl.BlockSpec((NDEV, None, 128, D), lambda i: (0, 3 * layer + 1, 0, 0)), vec],
        out_specs=[row, row, pl.BlockSpec((TM, RET_W), lambda i: (i, 0)), pl.BlockSpec((TM, LRU_W), lambda i: (i, 0)),
                   pl.BlockSpec((TM, GDN_W), lambda i: (i, 0)), vec, vec],
        out_shape=[jax.ShapeDtypeStruct((T, D), f32), jax.ShapeDtypeStruct((T, D), bf16),
                   jax.ShapeDtypeStruct((T, RET_W), f32), jax.ShapeDtypeStruct((T, LRU_W), f32),
                   jax.ShapeDtypeStruct((T, GDN_W), f32), jax.ShapeDtypeStruct((1, D), f32),
                   jax.ShapeDtypeStruct((1, D), f32)],
        compiler_params=_cparams(("arbitrary",)), name="mix_out_bwd")(z, dout, pr, lg)


def _loss_grad(y, target):
    T = y.shape[0]

    def body(y_ref, t_ref, dy_ref, l_ref):
        @pl.when(pl.program_id(0) == 0)
        def _():
            l_ref[...] = jnp.zeros_like(l_ref)

        e = y_ref[...] - t_ref[...]
        dy_ref[...] = e * (1.0 / D)
        l_ref[...] += 0.5 * jnp.sum(jnp.sum(e * e, -1, keepdims=True) * (1.0 / D), 0, keepdims=True)

    row = pl.BlockSpec((TM, D), lambda i: (i, 0))
    return pl.pallas_call(
        body, grid=(T // TM,), in_specs=[row, row], out_specs=[row, _full_spec((1, 1))],
        out_shape=[jax.ShapeDtypeStruct((T, D), f32), jax.ShapeDtypeStruct((1, 1), f32)],
        compiler_params=_cparams(("arbitrary",)), name="loss_grad")(y, target)


def _split_heads(x, H):
    n = x.shape[0] // CH
    parts = [x[:, h * CH:(h + 1) * CH].reshape(n, CH, CH) for h in range(H)]
    return jnp.stack(parts, axis=1).reshape(n * H, CH, CH)


def _merge_heads(ref, x, H, col0=0):
    n = x.shape[0] // H
    x4 = x.reshape(n, H, CH, CH)
    for h in range(H):
        ref[:, col0 + h * CH:col0 + (h + 1) * CH] = x4[:, h].reshape(n * CH, CH)


def _conv_fwd(ext, x, tail, w, R):
    ext[0:8, :] = tail
    ext[8:R + 8, :] = x
    y = w[3:4, :] * x
    for k in range(3):
        y = y + w[k:k + 1, :] * ext[5 + k:5 + k + R, :]
    return y


def _conv_bwd(ext, ext2, dy, dy_next, w, R):
    ext2[0:R, :] = dy
    ext2[R:R + 8, :] = dy_next
    dx = w[3:4, :] * dy
    dws = []
    for k in range(3):
        dx = dx + w[k:k + 1, :] * ext2[3 - k:3 - k + R, :]
        dws.append(jnp.sum(dy * ext[5 + k:5 + k + R, :], 0, keepdims=True))
    dws.append(jnp.sum(dy * ext[8:8 + R, :], 0, keepdims=True))
    return dx, jnp.concatenate(dws, axis=0)


def _prev_tail_spec(R, W):
    return pl.BlockSpec((8, W), lambda i: (jnp.maximum(i * (R // 8) - 1, 0), 0))


def _prev_tail_spec_rev(R, W, nb):
    return pl.BlockSpec((8, W), lambda i: (jnp.maximum((nb - 1 - i) * (R // 8) - 1, 0), 0))


def _rope_tables(positions):
    T = positions.shape[0]

    def body(p_ref, c_ref, s_ref):
        lane = lax.broadcasted_iota(jnp.int32, (TM, RET_W), 1)
        fi = (lane % 32).astype(f32)
        inv = jnp.exp(fi * (-math.log(ROPE_THETA) / 32.0))
        ang = p_ref[...].astype(f32) * inv
        c_ref[...] = jnp.cos(ang)
        s_ref[...] = jnp.where(lane % CH < 32, -jnp.sin(ang), jnp.sin(ang))

    row = pl.BlockSpec((TM, RET_W), lambda i: (i, 0))
    return pl.pallas_call(
        body, grid=(T // TM,), in_specs=[pl.BlockSpec((TM, 1), lambda i: (i, 0))], out_specs=[row, row],
        out_shape=[jax.ShapeDtypeStruct((T, RET_W), f32)] * 2,
        compiler_params=_cparams(("arbitrary",)), name="rope_tables")(positions)


def _partner(x):
    lane = lax.broadcasted_iota(jnp.int32, x.shape, 1)
    return jnp.where(lane % CH < 32, pltpu.roll(x, RET_W - 32, 1), pltpu.roll(x, 32, 1))


def _ret_consts():
    ii = lax.broadcasted_iota(jnp.int32, (CH, CH), 0).astype(f32)
    jj = lax.broadcasted_iota(jnp.int32, (CH, CH), 1).astype(f32)
    intra, cross, tail, cd = [], [], [], []
    for h in range(RET_H):
        lg = math.log1p(-(2.0 ** (-5.0 - h)))
        intra.append(jnp.exp(jnp.abs(ii - jj) * lg))
        cross.append(jnp.exp((ii + 1.0) * lg))
        tail.append(jnp.exp((CH - 1.0 - ii) * lg))
        cd.append(jnp.full((CH, CH), math.exp(CH * lg), f32))
    return jnp.stack(intra), jnp.stack(cross), jnp.stack(tail), jnp.stack(cd)


def _ret_chunk(consts, q, k, v, st):
    intra, cross, tail, cd = consts
    s = _bmm('hid,hjd->hij', q, k) * intra
    o = _bmm('hij,hje->hie', s, v) + _bmm('hid,hde->hie', q * cross, st)
    st2 = st * cd + _bmm('hjd,hje->hde', k * tail, v)
    mu = jnp.mean(o, -1, keepdims=True)
    oc = o - mu
    on = oc * lax.rsqrt(jnp.mean(oc * oc, -1, keepdims=True) + 1e-5)
    return on, st2


def _ret_fwd(hr, cosw, sinw, gam):
    T = hr.shape[0]
    R = RB_RET
    nc = R // CH

    def body(h_ref, c_ref, s_ref, g_ref, o_ref, st_ref, st, wide):
        @pl.when(pl.program_id(0) == 0)
        def _():
            st[...] = jnp.zeros_like(st)

        consts = _ret_consts()
        cw, sw = c_ref[...], s_ref[...]
        q, k = h_ref[:, 0:RET_W], h_ref[:, RET_W:2 * RET_W]
        qh = _split_heads((q * cw + _partner(q) * sw) * 0.125, RET_H)
        kh = _split_heads(k * cw + _partner(k) * sw, RET_H)
        vh = _split_heads(h_ref[:, 2 * RET_W:3 * RET_W], RET_H)
        outs = []
        s_cur = st[...]
        for c in range(nc):
            sl = slice(c * RET_H, (c + 1) * RET_H)
            st_ref[c] = s_cur
            on, s_cur = _ret_chunk(consts, qh[sl], kh[sl], vh[sl], s_cur)
            outs.append(on)
        st[...] = s_cur
        _merge_heads(wide, jnp.concatenate(outs, axis=0), RET_H)
        o_ref[...] = wide[...] * g_ref[...] * _silu(h_ref[:, 3 * RET_W:4 * RET_W])

    blk = pl.BlockSpec((R, RET_W), lambda i: (i, 0))
    return pl.pallas_call(
        body, grid=(T // R,),
        in_specs=[pl.BlockSpec((R, D), lambda i: (i, 0)), blk, blk, _full_spec((1, RET_W))],
        out_specs=[blk, pl.BlockSpec((nc, RET_H, CH, CH), lambda i: (i, 0, 0, 0))],
        out_shape=[jax.ShapeDtypeStruct((T, RET_W), f32), jax.ShapeDtypeStruct((T // CH, RET_H, CH, CH), f32)],
        scratch_shapes=[pltpu.VMEM((RET_H, CH, CH), f32), pltpu.VMEM((R, RET_W), f32)],
        compiler_params=_cparams(("arbitrary",)), name="ret_fwd")(hr, cosw, sinw, gam)


def _ret_bwd(hr, cosw, sinw, gam, states, dout):
    T = hr.shape[0]
    R = RB_RET
    nc = R // CH
    nb = T // R

    def body(h_ref, c_ref, s_ref, g_ref, st_ref, do_ref, dh_ref, dgam_ref, dst, wide):
        @pl.when(pl.program_id(0) == 0)
        def _():
            dst[...] = jnp.zeros_like(dst)
            dgam_ref[...] = jnp.zeros_like(dgam_ref)

        consts = _ret_consts()
        cw, sw = c_ref[...], s_ref[...]
        q, k = h_ref[:, 0:RET_W], h_ref[:, RET_W:2 * RET_W]
        gr = h_ref[:, 3 * RET_W:4 * RET_W]
        qh = _split_heads((q * cw + _partner(q) * sw) * 0.125, RET_H)
        kh = _split_heads(k * cw + _partner(k) * sw, RET_H)
        vh = _split_heads(h_ref[:, 2 * RET_W:3 * RET_W], RET_H)
        do = do_ref[...]
        gam = g_ref[...]
        sg = _silu(gr)
        don = _split_heads(do * gam * sg, RET_H)
        ons, dqs, dks, dvs = [None] * nc, [None] * nc, [None] * nc, [None] * nc
        ds = dst[...]
        for c in reversed(range(nc)):
            sl = slice(c * RET_H, (c + 1) * RET_H)
            (on, _), vjp = jax.vjp(functools.partial(_ret_chunk, consts), qh[sl], kh[sl], vh[sl], st_ref[c])
            dqs[c], dks[c], dvs[c], ds = vjp((don[sl], ds))
            ons[c] = on
        dst[...] = ds
        _merge_heads(wide, jnp.concatenate(ons, axis=0), RET_H)
        onw = wide[...]
        dgam_ref[...] += jnp.sum(do * onw * sg, 0, keepdims=True)
        dh_ref[:, 3 * RET_W:4 * RET_W] = do * onw * gam * _dsilu(gr)
        _merge_heads(wide, jnp.concatenate(dqs, axis=0), RET_H)
        u = wide[...] * 0.125
        dh_ref[:, 0:RET_W] = u * cw + _partner(u * sw)
        _merge_heads(wide, jnp.concatenate(dks, axis=0), RET_H)
        u = wide[...]
        dh_ref[:, RET_W:2 * RET_W] = u * cw + _partner(u * sw)
        _merge_heads(dh_ref, jnp.concatenate(dvs, axis=0), RET_H, col0=2 * RET_W)

    blk = pl.BlockSpec((R, RET_W), lambda i: (nb - 1 - i, 0))
    return pl.pallas_call(
        body, grid=(nb,),
        in_specs=[pl.BlockSpec((R, D), lambda i: (nb - 1 - i, 0)), blk, blk, _full_spec((1, RET_W)),
                  pl.BlockSpec((nc, RET_H, CH, CH), lambda i: (nb - 1 - i, 0, 0, 0)), blk],
        out_specs=[pl.BlockSpec((R, D), lambda i: (nb - 1 - i, 0)), _full_spec((1, RET_W))],
        out_shape=[jax.ShapeDtypeStruct((T, D), f32), jax.ShapeDtypeStruct((1, RET_W), f32)],
        scratch_shapes=[pltpu.VMEM((RET_H, CH, CH), f32), pltpu.VMEM((R, RET_W), f32)],
        compiler_params=_cparams(("arbitrary",)), name="ret_bwd")(hr, cosw, sinw, gam, states, dout)


def _lru_ab(xc, wa, ba, wx, bx, lam):
    r = _sigmoid(_dot(xc, wa) + ba)
    i = _sigmoid(_dot(xc, wx) + bx)
    la = 8.0 * r * (-_softplus(-lam))
    a = jnp.exp(la)
    em = jnp.tanh(la) * (jnp.exp(2.0 * la) + 1.0)
    return a, jnp.sqrt(-em) * (i * xc)


def _lru_out(h, gate):
    return h * _gelu(gate)


def _scan_fwd(a, b):
    R = a.shape[0]
    row = lax.broadcasted_iota(jnp.int32, a.shape, 0)
    d = 1
    while d < R:
        m = row >= d
        b = jnp.where(m, a * pltpu.roll(b, d, 0) + b, b)
        a = jnp.where(m, a * pltpu.roll(a, d, 0), a)
        d *= 2
    return a, b


def _scan_bwd(a, b):
    R = a.shape[0]
    row = lax.broadcasted_iota(jnp.int32, a.shape, 0)
    d = 1
    while d < R:
        m = row < R - d
        b = jnp.where(m, a * pltpu.roll(b, R - d, 0) + b, b)
        a = jnp.where(m, a * pltpu.roll(a, R - d, 0), a)
        d *= 2
    return b


def _lru_fwd(hl, cw, cb, wa, ba, wx, bx, lam):
    T = hl.shape[0]
    R = RB_LRU
    W = LRU_W

    def body(h_ref, t_ref, cw_ref, cb_ref, wa_ref, ba_ref, wx_ref, bx_ref, lam_ref, o_ref, hs_ref, carry, ext):
        first = pl.program_id(0) == 0

        @pl.when(first)
        def _():
            carry[...] = jnp.zeros_like(carry)

        tail = jnp.where(first, 0.0, t_ref[:, 0:W])
        xc = _conv_fwd(ext, h_ref[:, 0:W], tail, cw_ref[...], R) + cb_ref[...]
        a, b = _lru_ab(xc, wa_ref[...], ba_ref[...], wx_ref[...], bx_ref[...], lam_ref[...])
        ap, hloc = _scan_fwd(a, b)
        h = hloc + ap * carry[0:1, :]
        carry[...] = jnp.broadcast_to(h[R - 1:R, :], carry.shape)
        hs_ref[...] = h
        o_ref[...] = _lru_out(h, h_ref[:, W:2 * W])

    vec = _full_spec((1, W))
    blk = pl.BlockSpec((R, W), lambda i: (i, 0))
    return pl.pallas_call(
        body, grid=(T // R,),
        in_specs=[pl.BlockSpec((R, 2 * W), lambda i: (i, 0)), _prev_tail_spec(R, 2 * W), _full_spec((4, W)), vec,
                  _full_spec((W, W)), vec, _full_spec((W, W)), vec, vec],
        out_specs=[blk, blk], out_shape=[jax.ShapeDtypeStruct((T, W), f32)] * 2,
        scratch_shapes=[pltpu.VMEM((8, W), f32), pltpu.VMEM((R + 8, W), f32)],
        compiler_params=_cparams(("arbitrary",)), name="lru_fwd")(hl, hl, cw, cb, wa, ba, wx, bx, lam)


def _lru_bwd(hl, hs, cw, cb, wa, ba, wx, bx, lam, dout):
    T = hl.shape[0]
    R = RB_LRU
    W = LRU_W
    nb = T // R

    def body(h_ref, t_ref, hs_ref, hst_ref, cw_ref, cb_ref, wa_ref, ba_ref, wx_ref, bx_ref, lam_ref, do_ref,
             dh_ref, dcw_ref, dcb_ref, dwa_ref, dba_ref, dwx_ref, dbx_ref, dlam_ref, carry_g, carry_dy, ext, ext2):
        i = pl.program_id(0)
        last_blk = i == 0
        first_blk = i == nb - 1

        @pl.when(last_blk)
        def _():
            carry_g[...] = jnp.zeros_like(carry_g)
            carry_dy[...] = jnp.zeros_like(carry_dy)
            for r in (dcw_ref, dcb_ref, dwa_ref, dba_ref, dwx_ref, dbx_ref, dlam_ref):
                r[...] = jnp.zeros_like(r)

        tail = jnp.where(first_blk, 0.0, t_ref[:, 0:W])
        xc = _conv_fwd(ext, h_ref[:, 0:W], tail, cw_ref[...], R) + cb_ref[...]
        (a, _), vjp_ab = jax.vjp(_lru_ab, xc, wa_ref[...], ba_ref[...], wx_ref[...], bx_ref[...], lam_ref[...])
        hs = hs_ref[...]
        _, vjp_out = jax.vjp(_lru_out, hs, h_ref[:, W:2 * W])
        dh, dgate = vjp_out(do_ref[...])
        row = lax.broadcasted_iota(jnp.int32, (R, W), 0)
        dh = jnp.where(row == R - 1, dh + carry_g[0:1, :], dh)
        a_up = jnp.where(row == R - 1, 0.0, pltpu.roll(a, R - 1, 0))
        g = _scan_bwd(a_up, dh)
        carry_g[...] = jnp.broadcast_to(a[0:1, :] * g[0:1, :], carry_g.shape)
        hprev0 = jnp.where(first_blk, 0.0, hst_ref[7:8, :])
        hprev = jnp.where(row == 0, hprev0, pltpu.roll(hs, 1, 0))
        dxc, dwa, dba, dwx, dbx, dlam = vjp_ab((g * hprev, g))
        dwa_ref[...] += dwa
        dba_ref[...] += dba
        dwx_ref[...] += dwx
        dbx_ref[...] += dbx
        dlam_ref[...] += dlam
        dcb_ref[...] += jnp.sum(dxc, 0, keepdims=True)
        dx, dcw = _conv_bwd(ext, ext2, dxc, carry_dy[...], cw_ref[...], R)
        carry_dy[...] = dxc[0:8, :]
        dcw_ref[...] += dcw
        dh_ref[:, 0:W] = dx
        dh_ref[:, W:2 * W] = dgate

    vec = _full_spec((1, W))
    mat = _full_spec((W, W))
    blk = pl.BlockSpec((R, W), lambda i: (nb - 1 - i, 0))
    blk2 = pl.BlockSpec((R, 2 * W), lambda i: (nb - 1 - i, 0))
    return pl.pallas_call(
        body, grid=(nb,),
        in_specs=[blk2, _prev_tail_spec_rev(R, 2 * W, nb), blk, _prev_tail_spec_rev(R, W, nb), _full_spec((4, W)), vec,
                  mat, vec, mat, vec, vec, blk],
        out_specs=[blk2, _full_spec((4, W)), vec, mat, vec, mat, vec, vec],
        out_shape=[jax.ShapeDtypeStruct((T, 2 * W), f32), jax.ShapeDtypeStruct((4, W), f32),
                   jax.ShapeDtypeStruct((1, W), f32), jax.ShapeDtypeStruct((W, W), f32),
                   jax.ShapeDtypeStruct((1, W), f32), jax.ShapeDtypeStruct((W, W), f32),
                   jax.ShapeDtypeStruct((1, W), f32), jax.ShapeDtypeStruct((1, W), f32)],
        scratch_shapes=[pltpu.VMEM((8, W), f32), pltpu.VMEM((8, W), f32), pltpu.VMEM((R + 8, W), f32),
                        pltpu.VMEM((R + 8, W), f32)],
        compiler_params=_cparams(("arbitrary",)), name="lru_bwd")(hl, hl, hs, hs, cw, cb, wa, ba, wx, bx, lam, dout)


def _gdn_local(qs, ks, vs, gb, bb):
    B = qs.shape[0]
    ii = lax.broadcasted_iota(jnp.int32, (B, CH, CH), 1)
    jj = lax.broadcasted_iota(jnp.int32, (B, CH, CH), 2)
    q = qs * lax.rsqrt(jnp.sum(qs * qs, -1, keepdims=True) + 1e-6)
    k = ks * lax.rsqrt(jnp.sum(ks * ks, -1, keepdims=True) + 1e-6)
    gc = _cumsum_rows(gb)
    gct = jnp.swapaxes(gc, 1, 2)
    decay = jnp.where(ii >= jj, jnp.exp(jnp.minimum(gc - gct, 0.0)), 0.0)
    kk = _bmm('bid,bjd->bij', k, k)
    inv = _neumann_inv(-jnp.where(ii > jj, bb * kk * decay, 0.0))
    egc = jnp.exp(gc)
    u = _bmm3('bij,bje->bie', inv, vs * bb)
    w = _bmm3('bij,bje->bie', inv, k * (bb * egc))
    qk = _bmm('bid,bjd->bij', q, k) * (0.125 * decay)
    glast = gc[:, CH - 1:CH, :]
    return u, w, qk, q * (0.125 * egc), k * jnp.exp(glast - gc), jnp.exp(jnp.broadcast_to(glast, gc.shape))


def _gdn_step(st, u, w, qk, qd, kt, egl, z, gn):
    vnew = u - _bmm('hcd,hde->hce', w, st)
    o = _bmm('hcd,hde->hce', qd, st) + _bmm('hij,hje->hie', qk, vnew)
    st2 = st * egl + _bmm('hcd,hce->hde', kt, vnew)
    out = o * lax.rsqrt(jnp.mean(o * o, -1, keepdims=True) + 1e-6) * gn * _silu(z)
    return out, st2


def _gdn_scalars(ab, alog, dtb):
    sp = _softplus(ab + dtb)
    return -jnp.exp(alog) * sp, _sigmoid(ab)


def _bcast_heads(blk, lane0, H):
    R = blk.shape[0]
    n = R // CH
    parts = [jnp.broadcast_to(blk[:, lane0 + h:lane0 + h + 1], (R, CH)).reshape(n, CH, CH) for h in range(H)]
    return jnp.stack(parts, axis=1).reshape(n * H, CH, CH)


def _unbcast_heads(x, lane0, H):
    n = x.shape[0] // H
    R = n * CH
    s = jnp.sum(x, axis=2, keepdims=True).reshape(n, H, CH, 1)
    lane = lax.broadcasted_iota(jnp.int32, (R, 128), 1)
    acc = jnp.zeros((R, 128), f32)
    for h in range(H):
        acc = acc + jnp.where(lane == lane0 + h, jnp.broadcast_to(s[:, h].reshape(R, 1), (R, 128)), 0.0)
    return acc


def _gdn_fwd(hg, cw, alog, dtb, gn):
    T = hg.shape[0]
    R = RB_GDN
    nc = R // CH
    W3 = 3 * GDN_W
    H = GDN_H

    def body(h_ref, t_ref, cw_ref, al_ref, dt_ref, gn_ref, o_ref, st_ref, st, ext):
        first = pl.program_id(0) == 0

        @pl.when(first)
        def _():
            st[...] = jnp.zeros_like(st)

        tail = jnp.where(first, 0.0, t_ref[:, 0:W3])
        y = _silu(_conv_fwd(ext, h_ref[:, 0:W3], tail, cw_ref[...], R))
        qs, ks, vs = (_split_heads(y[:, j * GDN_W:(j + 1) * GDN_W], H) for j in range(3))
        zh = _split_heads(h_ref[:, W3:W3 + GDN_W], H)
        g, beta = _gdn_scalars(h_ref[:, W3 + GDN_W:GDN_IN], al_ref[...], dt_ref[...])
        loc = _gdn_local(qs, ks, vs, _bcast_heads(g, 0, H), _bcast_heads(beta, H, H))
        gnv = gn_ref[...]
        outs = []
        s_cur = st[...]
        for c in range(nc):
            sl = slice(c * H, (c + 1) * H)
            st_ref[c] = s_cur
            out, s_cur = _gdn_step(s_cur, *(t[sl] for t in loc), zh[sl], gnv)
            outs.append(out)
        st[...] = s_cur
        _merge_heads(o_ref, jnp.concatenate(outs, axis=0), H)

    return pl.pallas_call(
        body, grid=(T // R,),
        in_specs=[pl.BlockSpec((R, GDN_IN), lambda i: (i, 0)), _prev_tail_spec(R, GDN_IN), _full_spec((4, W3)),
                  _full_spec((1, 128)), _full_spec((1, 128)), _full_spec((1, CH))],
        out_specs=[pl.BlockSpec((R, GDN_W), lambda i: (i, 0)), pl.BlockSpec((nc, H, CH, CH), lambda i: (i, 0, 0, 0))],
        out_shape=[jax.ShapeDtypeStruct((T, GDN_W), f32), jax.ShapeDtypeStruct((T // CH, H, CH, CH), f32)],
        scratch_shapes=[pltpu.VMEM((H, CH, CH), f32), pltpu.VMEM((R + 8, W3), f32)],
        compiler_params=_cparams(("arbitrary",)), name="gdn_fwd")(hg, hg, cw, alog, dtb, gn)


def _gdn_bwd(hg, cw, alog, dtb, gn, states, dout):
    T = hg.shape[0]
    R = RB_GDN
    nc = R // CH
    nb = T // R
    W3 = 3 * GDN_W
    H = GDN_H

    def body(h_ref, t_ref, cw_ref, al_ref, dt_ref, gn_ref, st_ref, do_ref,
             dh_ref, dcw_ref, dal_ref, ddt_ref, dgn_ref, dst, carry_dy, ext, ext2, wide):
        i = pl.program_id(0)
        first_blk = i == nb - 1

        @pl.when(i == 0)
        def _():
            dst[...] = jnp.zeros_like(dst)
            carry_dy[...] = jnp.zeros_like(carry_dy)
            for r in (dcw_ref, dal_ref, ddt_ref, dgn_ref):
                r[...] = jnp.zeros_like(r)

        tail = jnp.where(first_blk, 0.0, t_ref[:, 0:W3])
        ypre = _conv_fwd(ext, h_ref[:, 0:W3], tail, cw_ref[...], R)
        y = _silu(ypre)
        qs, ks, vs = (_split_heads(y[:, j * GDN_W:(j + 1) * GDN_W], H) for j in range(3))
        zh = _split_heads(h_ref[:, W3:W3 + GDN_W], H)
        ab = h_ref[:, W3 + GDN_W:GDN_IN]
        alog, dtb = al_ref[...], dt_ref[...]
        g, beta = _gdn_scalars(ab, alog, dtb)
        loc, vjp_loc = jax.vjp(_gdn_local, qs, ks, vs, _bcast_heads(g, 0, H), _bcast_heads(beta, H, H))
        doh = _split_heads(do_ref[...], H)
        gnv = gn_ref[...]
        dloc = [[None] * nc for _ in range(6)]
        dzs = [None] * nc
        ds = dst[...]
        dgn = jnp.zeros((1, CH), f32)
        for c in reversed(range(nc)):
            sl = slice(c * H, (c + 1) * H)
            _, vjp = jax.vjp(_gdn_step, st_ref[c], *(t[sl] for t in loc), zh[sl], gnv)
            grads = vjp((doh[sl], ds))
            ds = grads[0]
            for j in range(6):
                dloc[j][c] = grads[1 + j]
            dzs[c] = grads[7]
            dgn = dgn + grads[8]
        dst[...] = ds
        dgn_ref[...] += dgn
        dqs, dks, dvs, dgb, dbb = vjp_loc(tuple(jnp.concatenate(d, axis=0) for d in dloc))
        lane = lax.broadcasted_iota(jnp.int32, (R, 128), 1)
        dg = _unbcast_heads(dgb, 0, H)
        dbeta = _unbcast_heads(dbb, H, H)
        da = dg * (-jnp.exp(alog)) * _sigmoid(ab + dtb)
        dh_ref[:, W3 + GDN_W:GDN_IN] = jnp.where(lane < H, da, dbeta * beta * (1.0 - beta))
        ddt_ref[...] += jnp.sum(jnp.where(lane < H, da, 0.0), 0, keepdims=True)
        dal_ref[...] += jnp.sum(jnp.where(lane < H, dg * g, 0.0), 0, keepdims=True)
        _merge_heads(dh_ref, jnp.concatenate(dzs, axis=0), H, col0=W3)
        for j, dpart in enumerate((dqs, dks, dvs)):
            _merge_heads(wide, dpart, H, col0=j * GDN_W)
        dy = wide[...] * _dsilu(ypre)
        dx, dcw = _conv_bwd(ext, ext2, dy, carry_dy[...], cw_ref[...], R)
        carry_dy[...] = dy[0:8, :]
        dcw_ref[...] += dcw
        dh_ref[:, 0:W3] = dx

    blk = pl.BlockSpec((R, GDN_IN), lambda i: (nb - 1 - i, 0))
    return pl.pallas_call(
        body, grid=(nb,),
        in_specs=[blk, _prev_tail_spec_rev(R, GDN_IN, nb), _full_spec((4, W3)), _full_spec((1, 128)),
                  _full_spec((1, 128)), _full_spec((1, CH)),
                  pl.BlockSpec((nc, H, CH, CH), lambda i: (nb - 1 - i, 0, 0, 0)),
                  pl.BlockSpec((R, GDN_W), lambda i: (nb - 1 - i, 0))],
        out_specs=[blk, _full_spec((4, W3)), _full_spec((1, 128)), _full_spec((1, 128)), _full_spec((1, CH))],
        out_shape=[jax.ShapeDtypeStruct((T, GDN_IN), f32), jax.ShapeDtypeStruct((4, W3), f32),
                   jax.ShapeDtypeStruct((1, 128), f32), jax.ShapeDtypeStruct((1, 128), f32),
                   jax.ShapeDtypeStruct((1, CH), f32)],
        scratch_shapes=[pltpu.VMEM((H, CH, CH), f32), pltpu.VMEM((8, W3), f32), pltpu.VMEM((R + 8, W3), f32),
                        pltpu.VMEM((R + 8, W3), f32), pltpu.VMEM((R, W3), f32)],
        compiler_params=_cparams(("arbitrary",)), name="gdn_bwd")(hg, hg, cw, alog, dtb, gn, states, dout)


def _block_diag(w):
    out = jnp.zeros((LRU_W, LRU_W), w.dtype)
    for g in range(w.shape[0]):
        out = lax.dynamic_update_slice(out, w[g], (g * CH, g * CH))
    return out


def _block_diag_t(w):
    return jnp.stack([w[g * CH:(g + 1) * CH, g * CH:(g + 1) * CH] for g in range(LRU_W // CH)])


def _pad_lanes(v, n=128):
    return jnp.pad(v, (0, n - v.shape[0]))[None, :]


def _local_step(x, p, positions, target, G, sm):
    pd, pr, pinl, ping, wpp = G['pd'], G['pr'], G['pinl'], G['ping'], G['wpp']
    p384 = jnp.transpose(G['p384'], (1, 2, 0, 3)).reshape(-1, D, NDEV * FSP)
    cosw, sinw = _rope_tables(positions)
    saved = []
    h = x
    for l in range(DEPTH):
        v = lambda n: sm[n][l][None, :]
        z1, x1, g1, u1 = _ffn_fwd(h, p384, pd, v('ln_ffn1_g'), v('ln_ffn1_b'), l, 0)
        hr, hl, hg = _proj_in(x1, pr, pinl, ping, l)
        o_r, rst = _ret_fwd(hr, cosw, sinw, v('ret_norm_g'))
        lru_args = (sm['lru_conv_w'][l], v('lru_conv_b'), _block_diag(sm['lru_w_a'][l]), v('lru_b_a'),
                    _block_diag(sm['lru_w_x'][l]), v('lru_b_x'), v('lru_lambda'))
        o_l, hs = _lru_fwd(hl, *lru_args)
        gdn_args = (sm['gdn_conv_w'][l], _pad_lanes(sm['gdn_a_log'][l]), _pad_lanes(sm['gdn_dt_bias'][l]),
                    v('gdn_norm_g'))
        o_g, gst = _gdn_fwd(hg, *gdn_args)
        z2, x2 = _mix_out(x1, o_r, o_l, o_g, pr, v('ln_mix_g'), v('ln_mix_b'), l)
        z3, x3, g2, u2 = _ffn_fwd(x2, p384, pd, v('ln_ffn2_g'), v('ln_ffn2_b'), l, 1, ple=(p[l], pr, wpp[l]))
        saved.append((h, z1, x1, hr, hl, hg, o_r, rst, o_l, hs, lru_args, o_g, gst, gdn_args, z2, x2, z3,
                      g1, u1, g2, u2))
        h = x3
    d, loss = _loss_grad(h, target)

    big = {k: [None] * DEPTH for k in ('p384', 'pd', 'pr', 'pinl', 'ping', 'ppp')}
    small = {n: [None] * DEPTH for n in SMALL}
    for l in reversed(range(DEPTH)):
        (x0, z1, x1, hr, hl, hg, o_r, rst, o_l, hs, lru_args, o_g, gst, gdn_args, z2, x2, z3,
         g1, u1, g2, u2) = saved[l]
        v = lambda n: sm[n][l][None, :]
        d2, dg2, du2, a2, dy2, small['ln_ffn2_g'][l], small['ln_ffn2_b'][l], dgp, dpj = _ffn_bwd(
            x2, z3, d, g2, u2, p384, pd, v('ln_ffn2_g'), l, 1, ple=(p[l], pr, wpp[l]))
        dxb, dzb, do_r, do_l, do_g, small['ln_mix_g'][l], small['ln_mix_b'][l] = _mix_out_bwd(
            z2, d2, pr, v('ln_mix_g'), l)
        dhr, small['ret_norm_g'][l] = _ret_bwd(hr, cosw, sinw, v('ret_norm_g'), rst, do_r)
        (dhl, small['lru_conv_w'][l], small['lru_conv_b'][l], dwa, small['lru_b_a'][l], dwx, small['lru_b_x'][l],
         small['lru_lambda'][l]) = _lru_bwd(hl, hs, *lru_args, do_l)
        small['lru_w_a'][l], small['lru_w_x'][l] = _block_diag_t(dwa), _block_diag_t(dwx)
        dhg, small['gdn_conv_w'][l], dal, ddt, small['gdn_norm_g'][l] = _gdn_bwd(hg, *gdn_args, gst, do_g)
        small['gdn_a_log'][l], small['gdn_dt_bias'][l] = dal[:, 0:GDN_H], ddt[:, 0:GDN_H]
        d1 = _proj_in_bwd(dxb, dhr, dhl, dhg, pr, pinl, ping, l)
        d, dg1, du1, a1, dy1, small['ln_ffn1_g'][l], small['ln_ffn1_b'][l] = _ffn_bwd(
            x0, z1, d1, g1, u1, p384, pd, v('ln_ffn1_g'), l, 0)
        rows = lambda m: m.reshape(NDEV, m.shape[1] // NDEV, m.shape[2])
        big['p384'][l] = jnp.stack([_matmul_tn(x0, dg1, FSP, "dw_gate", FB), _matmul_tn(x0, du1, FSP, "dw_up", FB),
                                    _matmul_tn(x2, dg2, FSP, "dw_gate", FB), _matmul_tn(x2, du2, FSP, "dw_up", FB)],
                                   axis=1)
        big['pd'][l] = jnp.stack([rows(_matmul_tn(a1, dy1, D, "dw_down")),
                                  rows(_matmul_tn(a2, dy2, D, "dw_down"))], axis=1)
        dwo = jnp.concatenate([_matmul_tn(o_r, dzb, D, "dw_out_r"), _matmul_tn(o_l, dzb, D, "dw_out_l"),
                               _matmul_tn(o_g, dzb, D, "dw_out_g")], axis=1)
        big['pr'][l] = jnp.stack([rows(_matmul_tn(x1, dhr, D, "dw_in_r")), rows(dwo),
                                  rows(_matmul_tn(x2, dgp, D, "dw_ple_gate"))], axis=1)
        big['pinl'][l] = rows(_matmul_tn(x1, dhl, 2 * LRU_W, "dw_in_l"))
        big['ping'][l] = rows(_matmul_tn(x1, dhg, GDN_IN, "dw_in_g"))
        big['ppp'][l] = _matmul_tn(p[l], dpj, 128, "dw_ple_proj")
    big = {'p384': jnp.concatenate(big['p384'], axis=1), 'pd': jnp.concatenate(big['pd'], axis=1),
           'pr': jnp.concatenate(big['pr'], axis=1), 'pinl': jnp.stack(big['pinl'], axis=1),
           'ping': jnp.stack(big['ping'], axis=1), 'ppp': jnp.stack(big['ppp'], axis=1)}
    small = {n: jnp.stack([g.reshape(sm[n].shape[1:]) for g in gs]) for n, gs in small.items()}
    return loss, d, big, small


def _pack_big(ws, dtype=bf16):
    padc = lambda a, n: jnp.pad(a, ((0, 0), (0, 0), (0, n - a.shape[2])))
    padr = lambda a, n: jnp.pad(a, ((0, 0), (0, n - a.shape[1]), (0, 0)))
    per_layer = lambda arrs: jnp.stack(arrs, axis=1).reshape((-1,) + arrs[0].shape[1:])
    w_in = ws['w_in']
    out = {
        'p384': per_layer([padc(ws[n], FSP) for n in ('ffn1_w_gate', 'ffn1_w_up', 'ffn2_w_gate', 'ffn2_w_up')]),
        'pd': per_layer([padr(ws[n], FSP) for n in ('ffn1_w_down', 'ffn2_w_down')]),
        'pr': per_layer([w_in[:, :, 0:D], ws['w_out'], ws['ple_w_gate']]),
        'pinl': w_in[:, :, D:D + 2 * LRU_W],
        'ping': padc(w_in[:, :, D + 2 * LRU_W:D_IN], GDN_IN),
        'ppp': ws['ple_w_proj'],
    }
    return {k: a.astype(dtype) for k, a in out.items()}


def _exchange(arrays, scatter, name):
    n = len(arrays)

    def body(*refs):
        ins, outs = refs[:n], refs[n:2 * n]
        send_sems, recv_sems, local_sems = refs[2 * n:]
        x, y, c = lax.axis_index("x"), lax.axis_index("y"), lax.axis_index("c")
        me = 4 * x + 2 * y + c
        copies = []
        for i in range(n):
            src = ins[i].at[me] if scatter[i] else ins[i]
            cp = pltpu.make_async_copy(src, outs[i].at[me], local_sems.at[i])
            cp.start()
            copies.append(cp)
        sends = []
        for j in range(1, NDEV):
            peer = (me + j) % NDEV
            pid = (peer // 4, (peer // 2) % 2, peer % 2)
            for i in range(n):
                src = ins[i].at[peer] if scatter[i] else ins[i]
                cp = pltpu.make_async_remote_copy(
                    src_ref=src, dst_ref=outs[i].at[me], send_sem=send_sems.at[i, j - 1],
                    recv_sem=recv_sems.at[i, j - 1], device_id=pid, device_id_type=pl.DeviceIdType.MESH)
                cp.start()
                sends.append(cp)
        for j in range(1, NDEV):
            source = (me + NDEV - j) % NDEV
            sid = (source // 4, (source // 2) % 2, source % 2)
            for i in range(n):
                src = ins[i].at[me] if scatter[i] else ins[i]
                pltpu.make_async_remote_copy(
                    src_ref=src, dst_ref=outs[i].at[source], send_sem=send_sems.at[i, j - 1],
                    recv_sem=recv_sems.at[i, j - 1], device_id=sid, device_id_type=pl.DeviceIdType.MESH).wait_recv()
        for cp in sends:
            cp.wait_send()
        for cp in copies:
            cp.wait()

    hbm = pl.BlockSpec(memory_space=pltpu.HBM)
    out_shape = [jax.ShapeDtypeStruct(a.shape if s else (NDEV,) + a.shape, a.dtype) for a, s in zip(arrays, scatter)]
    return pl.pallas_call(
        body, in_specs=[hbm] * n, out_specs=[hbm] * n, out_shape=out_shape,
        scratch_shapes=[pltpu.SemaphoreType.DMA((n, NDEV - 1)), pltpu.SemaphoreType.DMA((n, NDEV - 1)),
                        pltpu.SemaphoreType.DMA((n,))],
        compiler_params=pltpu.CompilerParams(has_side_effects=True), name=name)(*arrays)


def _gather_two_level(arrays, name):
    n = len(arrays)

    def body(*refs):
        ins, outs = refs[:n], refs[n:2 * n]
        send_sems, recv_sems, local_sems = refs[2 * n:]
        x, y, c = lax.axis_index("x"), lax.axis_index("y"), lax.axis_index("c")
        me, sibling = (x, y, c), (x, y, 1 - c)
        chips = [(1 - x, y), (x, 1 - y), (1 - x, 1 - y)]
        slot = lambda d: 4 * d[0] + 2 * d[1] + d[2]

        def copy(i, k, block, to, src=None):
            return pltpu.make_async_remote_copy(
                src_ref=outs[i].at[slot(block)] if src is None else src, dst_ref=outs[i].at[slot(block)],
                send_sem=send_sems.at[i, k], recv_sem=recv_sems.at[i, k], device_id=to,
                device_id_type=pl.DeviceIdType.MESH)

        mine, first, passed = [], [], []
        for i in range(n):
            cp = pltpu.make_async_copy(ins[i], outs[i].at[slot(me)], local_sems.at[i])
            cp.start()
            mine.append(cp)
            first.append(copy(i, 0, me, sibling, src=ins[i]))
            first += [copy(i, 1 + j, me, (*chip, c), src=ins[i]) for j, chip in enumerate(chips)]
        for cp in first:
            cp.start()
        for i in range(n):
            for j, chip in enumerate(chips):
                copy(i, 1 + j, (*chip, c), me).wait_recv()
                cp = copy(i, 4 + j, (*chip, c), sibling)
                cp.start()
                passed.append(cp)
        for i in range(n):
            copy(i, 0, sibling, me).wait_recv()
            for j, chip in enumerate(chips):
                copy(i, 4 + j, (*chip, 1 - c), me).wait_recv()
        for cp in first + passed:
            cp.wait_send()
        for cp in mine:
            cp.wait()

    hbm = pl.BlockSpec(memory_space=pltpu.HBM)
    return pl.pallas_call(
        body, in_specs=[hbm] * n, out_specs=[hbm] * n,
        out_shape=[jax.ShapeDtypeStruct((NDEV,) + a.shape, a.dtype) for a in arrays],
        scratch_shapes=[pltpu.SemaphoreType.DMA((n, NDEV - 1)), pltpu.SemaphoreType.DMA((n, NDEV - 1)),
                        pltpu.SemaphoreType.DMA((n,))],
        compiler_params=pltpu.CompilerParams(has_side_effects=True), name=name)(*arrays)


def _scatter_pairs(arrays, name):
    n = len(arrays)

    def body(*refs):
        ins, gots = refs[:n], refs[n:2 * n]
        send_sems, recv_sems = refs[2 * n:]
        x, y, c = lax.axis_index("x"), lax.axis_index("y"), lax.axis_index("c")
        sends = []
        for i in range(n):
            for q in range(4):
                cp = pltpu.make_async_remote_copy(
                    src_ref=ins[i].at[2 * q + 1 - c], dst_ref=gots[i].at[q], send_sem=send_sems.at[i, q],
                    recv_sem=recv_sems.at[i, q], device_id=(x, y, 1 - c), device_id_type=pl.DeviceIdType.MESH)
                cp.start()
                sends.append(cp)
        for cp in sends:
            cp.wait_recv()
        for cp in sends:
            cp.wait_send()

    hbm = pl.BlockSpec(memory_space=pltpu.HBM)
    return pl.pallas_call(
        body, in_specs=[hbm] * n, out_specs=[hbm] * n,
        out_shape=[jax.ShapeDtypeStruct((4,) + a.shape[1:], a.dtype) for a in arrays],
        scratch_shapes=[pltpu.SemaphoreType.DMA((n, 4)), pltpu.SemaphoreType.DMA((n, 4))],
        compiler_params=pltpu.CompilerParams(has_side_effects=True), name=name)(*arrays)


def _pair_sum(own, got, name):
    def body(a_ref, b_ref, o_ref):
        o_ref[...] = (a_ref[...].astype(f32) + b_ref[...].astype(f32)).astype(bf16)

    spec = pl.BlockSpec((None, None) + own.shape[2:], lambda q, s: (q, s, 0, 0))
    return pl.pallas_call(
        body, grid=own.shape[:2], in_specs=[spec, spec], out_specs=spec,
        out_shape=jax.ShapeDtypeStruct(own.shape, bf16),
        compiler_params=_cparams(("arbitrary", "arbitrary")), name=name)(own, got)


def _scatter_chips(arrays, name):
    n = len(arrays)

    def body(*refs):
        ins, outs = refs[:n], refs[n:2 * n]
        send_sems, recv_sems, local_sems = refs[2 * n:]
        x, y, c = lax.axis_index("x"), lax.axis_index("y"), lax.axis_index("c")
        chip = 2 * x + y
        keeps, sends = [], []
        for i in range(n):
            cp = pltpu.make_async_copy(ins[i].at[chip], outs[i].at[chip], local_sems.at[i])
            cp.start()
            keeps.append(cp)
        for j in range(1, 4):
            peer = (chip + j) % 4
            for i in range(n):
                cp = pltpu.make_async_remote_copy(
                    src_ref=ins[i].at[peer], dst_ref=outs[i].at[chip], send_sem=send_sems.at[i, j - 1],
                    recv_sem=recv_sems.at[i, j - 1], device_id=(peer // 2, peer % 2, c),
                    device_id_type=pl.DeviceIdType.MESH)
                cp.start()
                sends.append(cp)
        for j in range(1, 4):
            source = (chip + 4 - j) % 4
            for i in range(n):
                pltpu.make_async_remote_copy(
                    src_ref=ins[i].at[chip], dst_ref=outs[i].at[source], send_sem=send_sems.at[i, j - 1],
                    recv_sem=recv_sems.at[i, j - 1], device_id=(source // 2, source % 2, c),
                    device_id_type=pl.DeviceIdType.MESH).wait_recv()
        for cp in sends:
            cp.wait_send()
        for cp in keeps:
            cp.wait()

    hbm = pl.BlockSpec(memory_space=pltpu.HBM)
    return pl.pallas_call(
        body, in_specs=[hbm] * n, out_specs=[hbm] * n,
        out_shape=[jax.ShapeDtypeStruct(a.shape, a.dtype) for a in arrays],
        scratch_shapes=[pltpu.SemaphoreType.DMA((n, 3)), pltpu.SemaphoreType.DMA((n, 3)),
                        pltpu.SemaphoreType.DMA((n,))],
        compiler_params=pltpu.CompilerParams(has_side_effects=True), name=name)(*arrays)


def _adam_math(w, g, m, v):
    m2 = ADAM_B1 * m + (1.0 - ADAM_B1) * g
    v2 = ADAM_B2 * v + (1.0 - ADAM_B2) * (g * g)
    m_hat = m2 / (1.0 - ADAM_B1 ** ADAM_STEP)
    v_hat = v2 / (1.0 - ADAM_B2 ** ADAM_STEP)
    return -ADAM_LR * (m_hat / (jnp.sqrt(v_hat) + ADAM_EPS) + ADAM_WD * w), m2, v2


def _adam_big(parts, w, m, v, name):
    L, rows, cols = w.shape

    def body(*refs):
        prefs = refs[:len(parts)]
        w_ref, m_ref, v_ref, g_ref, d_ref, m2_ref, v2_ref = refs[len(parts):]
        c0 = 0
        for pref in prefs:
            acc = pref[0].astype(f32)
            for s in range(1, pref.shape[0]):
                acc = acc + pref[s].astype(f32)
            width = min(acc.shape[1], cols - c0)
            g_ref[:, c0:c0 + width] = acc[0:rows, 0:width]
            c0 += width
        d, m2, v2 = _adam_math(w_ref[...], g_ref[...], m_ref[...], v_ref[...])
        d_ref[...] = d
        m2_ref[...] = m2
        v2_ref[...] = v2

    wspec = pl.BlockSpec((None, rows, cols), lambda l: (l, 0, 0))
    in_specs = [pl.BlockSpec((a.shape[0], None) + a.shape[2:],
                             functools.partial(lambda l, per, first: (0, per * l + first, 0, 0), per=per, first=first))
                for a, per, first in parts]
    return pl.pallas_call(
        body, grid=(L,), in_specs=in_specs + [wspec] * 3, out_specs=[wspec] * 4,
        out_shape=[jax.ShapeDtypeStruct(w.shape, f32)] * 4,
        compiler_params=_cparams(("arbitrary",)), name=name)(*[a for a, _, _ in parts], w, m, v)


def _sum_sources(stacked):
    rows = stacked.shape[1]

    def body(s_ref, o_ref):
        acc = s_ref[0]
        for s in range(1, NDEV):
            acc = acc + s_ref[s]
        o_ref[...] = acc

    return pl.pallas_call(body, out_shape=jax.ShapeDtypeStruct((rows, 128), f32), name="sum_small_grads")(stacked)


def _adam_small(w, g, m, v):
    def body(w_ref, g_ref, m_ref, v_ref, d_ref, m2_ref, v2_ref):
        d, m2, v2 = _adam_math(w_ref[...], g_ref[...], m_ref[...], v_ref[...])
        d_ref[...] = d
        m2_ref[...] = m2
        v2_ref[...] = v2

    return pl.pallas_call(body, out_shape=[jax.ShapeDtypeStruct(w.shape, f32)] * 3, name="adam_small")(w, g, m, v)


def _pack_rows(arrs):
    flat = []
    for a in arrs:
        a = a.reshape(-1)
        flat.append(jnp.pad(a, (0, (-a.shape[0]) % 1024)))
    return jnp.concatenate(flat).reshape(-1, 128)


def _unpack_rows(packed, shapes):
    out, off = [], 0
    flat = packed.reshape(-1)
    for s in shapes:
        n = math.prod(s)
        out.append(flat[off:off + n].reshape(s))
        off += n + (-n) % 1024
    return out


def _gather_conv(gathered, shape):
    L, K, c = shape
    return jnp.transpose(gathered, (1, 2, 0, 3)).reshape(L, K, NDEV * c)


def kernel(x, p, positions, ln_ffn1_g, ln_ffn1_b, ffn1_w_gate, ffn1_w_up, ffn1_w_down, w_in, ret_norm_g, lru_conv_w, lru_conv_b, lru_w_a, lru_b_a, lru_w_x, lru_b_x, lru_lambda, gdn_conv_w, gdn_a_log, gdn_dt_bias, gdn_norm_g, w_out, ln_mix_g, ln_mix_b, ffn2_w_gate, ffn2_w_up, ffn2_w_down, ple_w_gate, ple_w_proj, ln_ffn2_g, ln_ffn2_b, loss_target, m_ln_ffn1_g, m_ln_ffn1_b, m_ffn1_w_gate, m_ffn1_w_up, m_ffn1_w_down, m_w_in, m_ret_norm_g, m_lru_conv_w, m_lru_conv_b, m_lru_w_a, m_lru_b_a, m_lru_w_x, m_lru_b_x, m_lru_lambda, m_gdn_conv_w, m_gdn_a_log, m_gdn_dt_bias, m_gdn_norm_g, m_w_out, m_ln_mix_g, m_ln_mix_b, m_ffn2_w_gate, m_ffn2_w_up, m_ffn2_w_down, m_ple_w_gate, m_ple_w_proj, m_ln_ffn2_g, m_ln_ffn2_b, v_ln_ffn1_g, v_ln_ffn1_b, v_ffn1_w_gate, v_ffn1_w_up, v_ffn1_w_down, v_w_in, v_ret_norm_g, v_lru_conv_w, v_lru_conv_b, v_lru_w_a, v_lru_b_a, v_lru_w_x, v_lru_b_x, v_lru_lambda, v_gdn_conv_w, v_gdn_a_log, v_gdn_dt_bias, v_gdn_norm_g, v_w_out, v_ln_mix_g, v_ln_mix_b, v_ffn2_w_gate, v_ffn2_w_up, v_ffn2_w_down, v_ple_w_gate, v_ple_w_proj, v_ln_ffn2_g, v_ln_ffn2_b):
    args = locals()
    W = {n: args[n] for n in WEIGHTS}
    M = {n: args['m_' + n] for n in WEIGHTS}
    V = {n: args['v_' + n] for n in WEIGHTS}
    me = 4 * lax.axis_index("x") + 2 * lax.axis_index("y") + lax.axis_index("c")

    packed = _pack_big(W)
    conv_pack = _pack_rows([W[n] for n in CONV_SHARDED])
    keys = list(packed)
    gathered = _gather_two_level([packed[k] for k in keys] + [conv_pack], "gather_weights")
    G = dict(zip(keys, gathered[:-1]))
    G['wpp'] = jnp.transpose(G.pop('ppp'), (1, 2, 0, 3)).reshape(DEPTH, PLE, D)
    conv_all = gathered[-1]
    sm = {n: W[n] for n in SMALL}
    conv_shards = [_unpack_rows(conv_all[s], [W[n].shape for n in CONV_SHARDED]) for s in range(NDEV)]
    for i, n in enumerate(CONV_SHARDED):
        sm[n] = _gather_conv(jnp.stack([cs[i] for cs in conv_shards]), W[n].shape)

    loss, grad_x, big, small = _local_step(x[0], p[:, 0], positions.reshape(-1, 1), loss_target[0], G, sm)
    loss = lax.psum(loss[0, 0], ("x", "y", "c"))

    small_pack = _pack_rows([small[n] for n in SMALL])
    bkeys = list(big)
    gots = _scatter_pairs([big[k] for k in bkeys], "scatter_pairs")
    core = lax.axis_index("c")
    owns = [lax.dynamic_index_in_dim(big[k].reshape((4, 2) + big[k].shape[1:]), core, axis=1, keepdims=False)
            for k in bkeys]
    pair = [_pair_sum(o, g, "pair_sum_" + k) for k, o, g in zip(bkeys, owns, gots)]
    R = dict(zip(bkeys, _scatter_chips(pair, "scatter_chips")))
    small_all = _exchange([small_pack], [False], "gather_small_grads")[0]
    small_sum = _unpack_rows(_sum_sources(small_all), [small[n].shape for n in SMALL])
    grads, delta, new_m, new_v = {}, {}, {}, {}
    for n, g in zip(SMALL, small_sum):
        if n in CONV_SHARDED:
            c = W[n].shape[2]
            g = lax.dynamic_slice_in_dim(g, me * c, c, axis=2)
        grads[n] = g

    big_parts = {
        'ffn1_w_gate': [(R['p384'], 4, 0)], 'ffn1_w_up': [(R['p384'], 4, 1)],
        'ffn2_w_gate': [(R['p384'], 4, 2)], 'ffn2_w_up': [(R['p384'], 4, 3)],
        'ffn1_w_down': [(R['pd'], 2, 0)], 'ffn2_w_down': [(R['pd'], 2, 1)],
        'w_in': [(R['pr'], 3, 0), (R['pinl'], 1, 0), (R['ping'], 1, 0)],
        'w_out': [(R['pr'], 3, 1)], 'ple_w_gate': [(R['pr'], 3, 2)], 'ple_w_proj': [(R['ppp'], 1, 0)],
    }
    for n in BIG:
        grads[n], delta[n], new_m[n], new_v[n] = _adam_big(big_parts[n], W[n], M[n], V[n], "adam_" + n)
    shapes = [W[n].shape for n in SMALL]
    d_s, m_s, v_s = _adam_small(*[_pack_rows([src[n] for n in SMALL]) for src in (W, grads, M, V)])
    for n, dd, mm, vv in zip(SMALL, _unpack_rows(d_s, shapes), _unpack_rows(m_s, shapes), _unpack_rows(v_s, shapes)):
        delta[n], new_m[n], new_v[n] = dd, mm, vv

    return (loss, grad_x[None], *[grads[n] for n in WEIGHTS], *[delta[n] for n in WEIGHTS],
            *[new_m[n] for n in WEIGHTS], *[new_v[n] for n in WEIGHTS])
```

```python
import functools
import math

import jax
import jax.numpy as jnp
from jax import lax
from jax.experimental import pallas as pl
from jax.experimental.pallas import tpu as pltpu

f32 = jnp.float32
bf16 = jnp.bfloat16

NDEV = 8
DEPTH = 2
D = 1024
FS = 352
FSP = 384
FB = 2
NF = NDEV // FB
PLE = 256
CH = 64
RET_H, GDN_H = 4, 6
RET_W, LRU_W, GDN_W = 256, 384, 384
GDN_IN = 1664
GDN_IN_REAL = 1548
D_IN = 3340
ALPHA = 4.0 ** 0.25
LN_EPS = 1e-5
ROPE_THETA = 10000.0
TM = 512
RB_RET, RB_LRU, RB_GDN = 512, 512, 256
VMEM_LIMIT = 56 * 1024 * 1024
ADAM_LR, ADAM_B1, ADAM_B2, ADAM_EPS, ADAM_WD, ADAM_STEP = 0.001, 0.9, 0.999, 1e-08, 0.01, 10

WEIGHTS = ['ln_ffn1_g', 'ln_ffn1_b', 'ffn1_w_gate', 'ffn1_w_up', 'ffn1_w_down', 'w_in', 'ret_norm_g', 'lru_conv_w',
           'lru_conv_b', 'lru_w_a', 'lru_b_a', 'lru_w_x', 'lru_b_x', 'lru_lambda', 'gdn_conv_w', 'gdn_a_log',
           'gdn_dt_bias', 'gdn_norm_g', 'w_out', 'ln_mix_g', 'ln_mix_b', 'ffn2_w_gate', 'ffn2_w_up', 'ffn2_w_down',
           'ple_w_gate', 'ple_w_proj', 'ln_ffn2_g', 'ln_ffn2_b']
BIG = ['ffn1_w_gate', 'ffn1_w_up', 'ffn1_w_down', 'w_in', 'w_out', 'ffn2_w_gate', 'ffn2_w_up', 'ffn2_w_down',
       'ple_w_gate', 'ple_w_proj']
SMALL = [n for n in WEIGHTS if n not in BIG]
CONV_SHARDED = {'lru_conv_w': LRU_W, 'gdn_conv_w': 3 * GDN_W}


def _cparams(sem=None):
    return pltpu.CompilerParams(dimension_semantics=sem, vmem_limit_bytes=VMEM_LIMIT)


def _sigmoid(x):
    return 1.0 / (1.0 + jnp.exp(-x))


def _silu(x):
    return x * _sigmoid(x)


def _dsilu(x):
    s = _sigmoid(x)
    return s * (1.0 + x * (1.0 - s))


def _softplus(x):
    return jnp.maximum(x, 0.0) + jnp.log(1.0 + jnp.exp(-jnp.abs(x)))


def _gelu(x):
    return 0.5 * x * (1.0 + jnp.tanh(0.7978845608028654 * (x + 0.044715 * x * x * x)))


def _dot(a, b):
    return jnp.dot(a.astype(bf16), b.astype(bf16), preferred_element_type=f32)


def _dot_nt(a, b):
    return lax.dot_general(a.astype(bf16), b.astype(bf16), (((1,), (1,)), ((), ())), preferred_element_type=f32)


def _dot_tn(a, b):
    return lax.dot_general(a.astype(bf16), b.astype(bf16), (((0,), (0,)), ((), ())), preferred_element_type=f32)


def _bmm(eq, a, b):
    return jnp.einsum(eq, a.astype(bf16), b.astype(bf16), preferred_element_type=f32)


def _split3(a):
    a1 = a.astype(bf16)
    r = a - a1.astype(f32)
    a2 = r.astype(bf16)
    return a1, a2, (r - a2.astype(f32)).astype(bf16)


def _bmm3(eq, a, b):
    a1, a2, _ = _split3(a)
    b1, b2, _ = _split3(b)
    e = lambda x, y: jnp.einsum(eq, x, y, preferred_element_type=f32)
    return e(a1, b1) + (e(a1, b2) + e(a2, b1))


def _tri_ones(B, upper=False):
    ii = lax.broadcasted_iota(jnp.int32, (B, CH, CH), 1)
    jj = lax.broadcasted_iota(jnp.int32, (B, CH, CH), 2)
    return jnp.where((ii <= jj) if upper else (ii >= jj), 1.0, 0.0).astype(bf16)


def _cumsum_mm(t, x):
    x1, x2, x3 = _split3(x)
    e = lambda y: jnp.einsum('bij,bjk->bik', t, y, preferred_element_type=f32)
    return e(x1) + (e(x2) + e(x3))


def _chunk_cumsum(x, reverse=False):
    n = x.shape[0] // CH
    return _cumsum_mm(_tri_ones(n, upper=reverse), x.reshape(n, CH, 128)).reshape(x.shape)


@jax.custom_vjp
def _neumann_inv(m):
    ii = lax.broadcasted_iota(jnp.int32, m.shape, 1)
    jj = lax.broadcasted_iota(jnp.int32, m.shape, 2)
    inv = jnp.where(ii == jj, 1.0, 0.0).astype(f32) + m
    mp = m
    for _ in range(5):
        mp = _bmm3('bij,bjk->bik', mp, mp)
        inv = inv + _bmm3('bij,bjk->bik', inv, mp)
    return inv


def _neumann_inv_fwd(m):
    inv = _neumann_inv(m)
    return inv, inv


def _neumann_inv_bwd(inv, g):
    return (_bmm3('bij,bkj->bik', _bmm3('bji,bjk->bik', inv, g), inv),)


_neumann_inv.defvjp(_neumann_inv_fwd, _neumann_inv_bwd)


def _ln_stats(z):
    mu = jnp.mean(z, -1, keepdims=True)
    zc = z - mu
    rstd = lax.rsqrt(jnp.mean(zc * zc, -1, keepdims=True) + LN_EPS)
    return zc * rstd, rstd


def _ln_bwd(z, g, dout):
    xh, rstd = _ln_stats(z)
    dxh = dout * g
    dz = rstd * (dxh - jnp.mean(dxh, -1, keepdims=True) - xh * jnp.mean(dxh * xh, -1, keepdims=True))
    return dz, jnp.sum(dout * xh, 0, keepdims=True), jnp.sum(dout, 0, keepdims=True)


def _full_spec(shape):
    nd = len(shape)
    return pl.BlockSpec(shape, lambda *_: (0,) * nd)


def _ffn_fwd(x, p384, pd, lg, lb, layer, which, ple=None):
    T = x.shape[0]
    sg, su, sd = 4 * layer + 2 * which, 4 * layer + 2 * which + 1, 2 * layer + which
    has_ple = ple is not None

    def body(*refs):
        if has_ple:
            (x_ref, wg_ref, wu_ref, wd_ref, lg_ref, lb_ref, p_ref, wpg_ref, wpp_ref,
             z_ref, o_ref, g_ref, u_ref, acc, xb_s) = refs
        else:
            x_ref, wg_ref, wu_ref, wd_ref, lg_ref, lb_ref, z_ref, o_ref, g_ref, u_ref, acc, xb_s = refs
        f = pl.program_id(1)

        @pl.when(f == 0)
        def _():
            x = x_ref[...]
            xb = x.astype(bf16)
            xb_s[...] = xb
            base = ALPHA * x
            if has_ple:
                gate = _sigmoid(_dot(xb, wpg_ref[...].reshape(D, D)))
                base = base + gate * _dot(p_ref[...], wpp_ref[...])
            acc[...] = base

        xb = xb_s[...]
        g = _dot(xb, wg_ref[...])
        u = _dot(xb, wu_ref[...])
        g_ref[...] = g.astype(bf16)
        u_ref[...] = u.astype(bf16)
        acc[...] += 0.5 * _dot(_silu(g) * u, wd_ref[...].reshape(FB * FSP, D))

        @pl.when(f == NF - 1)
        def _():
            z = acc[...]
            z_ref[...] = z
            o_ref[...] = _ln_stats(z)[0] * lg_ref[...] + lb_ref[...]

    row = pl.BlockSpec((TM, D), lambda i, f: (i, 0))
    in_specs = [row,
                pl.BlockSpec((None, D, FB * FSP), lambda i, f: (sg, 0, f)),
                pl.BlockSpec((None, D, FB * FSP), lambda i, f: (su, 0, f)),
                pl.BlockSpec((FB, None, FSP, D), lambda i, f: (f, sd, 0, 0)),
                _full_spec((1, D)), _full_spec((1, D))]
    args = [x, p384, p384, pd, lg, lb]
    if has_ple:
        p, pr, wpp = ple
        in_specs += [pl.BlockSpec((TM, PLE), lambda i, f: (i, 0)),
                     pl.BlockSpec((NDEV, None, 128, D), lambda i, f: (0, 3 * layer + 2, 0, 0)),
                     _full_spec((PLE, D))]
        args += [p, pr, wpp]
    hid = pl.BlockSpec((TM, FB * FSP), lambda i, f: (i, f))
    hshape = jax.ShapeDtypeStruct((T, NDEV * FSP), bf16)
    return pl.pallas_call(
        body, grid=(T // TM, NF), in_specs=in_specs, out_specs=[row, row, hid, hid],
        out_shape=[jax.ShapeDtypeStruct((T, D), f32)] * 2 + [hshape, hshape],
        scratch_shapes=[pltpu.VMEM((TM, D), f32), pltpu.VMEM((TM, D), bf16)],
        compiler_params=_cparams(("arbitrary", "arbitrary")), name=f"ffn{which + 1}_fwd")(*args)


def _ffn_bwd(z, dout, gs, us, p384, pd, lg, layer, which):
    T = z.shape[0]
    TMB = TM
    sg, su, sd = 4 * layer + 2 * which, 4 * layer + 2 * which + 1, 2 * layer + which

    def body(z_ref, do_ref, g_ref, u_ref, wg_ref, wu_ref, wd_ref, lg_ref,
             dx_ref, dg_ref, du_ref, a_ref, dy_ref, dlg_ref, dlb_ref, acc, dyb):
        i, f = pl.program_id(0), pl.program_id(1)

        @pl.when(jnp.logical_and(i == 0, f == 0))
        def _():
            dlg_ref[...] = jnp.zeros_like(dlg_ref)
            dlb_ref[...] = jnp.zeros_like(dlb_ref)

        @pl.when(f == 0)
        def _():
            dz, dlg, dlb = _ln_bwd(z_ref[...], lg_ref[...], do_ref[...])
            dlg_ref[...] += dlg
            dlb_ref[...] += dlb
            dy = (0.5 * dz).astype(bf16)
            dyb[...] = dy
            dy_ref[...] = dy
            acc[...] = ALPHA * dz

        g = g_ref[...].astype(f32)
        u = u_ref[...].astype(f32)
        da = _dot_nt(dyb[...], wd_ref[...].reshape(FB * FSP, D))
        sgm = _sigmoid(g)
        dg = (da * u * (sgm * (1.0 + g * (1.0 - sgm)))).astype(bf16)
        du = (da * (g * sgm)).astype(bf16)
        dg_ref[...] = dg
        du_ref[...] = du
        a_ref[...] = (g * sgm * u).astype(bf16)
        acc[...] += _dot_nt(dg, wg_ref[...]) + _dot_nt(du, wu_ref[...])

        @pl.when(f == NF - 1)
        def _():
            dx_ref[...] = acc[...]

    row = pl.BlockSpec((TMB, D), lambda i, f: (i, 0))
    hid = pl.BlockSpec((TMB, FB * FSP), lambda i, f: (i, f))
    vec = _full_spec((1, D))
    in_specs = [row, row, hid, hid,
                pl.BlockSpec((None, D, FB * FSP), lambda i, f: (sg, 0, f)),
                pl.BlockSpec((None, D, FB * FSP), lambda i, f: (su, 0, f)),
                pl.BlockSpec((FB, None, FSP, D), lambda i, f: (f, sd, 0, 0)),
                vec]
    args = [z, dout, gs, us, p384, p384, pd, lg]
    out_specs = [row, hid, hid, hid, row, vec, vec]
    hshape = jax.ShapeDtypeStruct((T, NDEV * FSP), bf16)
    out_shape = [jax.ShapeDtypeStruct((T, D), f32), hshape, hshape, hshape, jax.ShapeDtypeStruct((T, D), bf16),
                 jax.ShapeDtypeStruct((1, D), f32), jax.ShapeDtypeStruct((1, D), f32)]
    return pl.pallas_call(
        body, grid=(T // TMB, NF), in_specs=in_specs, out_specs=out_specs, out_shape=out_shape,
        scratch_shapes=[pltpu.VMEM((TMB, D), f32), pltpu.VMEM((TMB, D), bf16)],
        compiler_params=_cparams(("arbitrary", "arbitrary")), name=f"ffn{which + 1}_bwd")(*args)


def _ple_bwd(x, p, dy, dx_ffn, pr, wpp, layer):
    T = x.shape[0]

    def body(x_ref, p_ref, dy_ref, dxf_ref, wpg_ref, wpp_ref, dx_ref, dgp_ref, dpj_ref):
        dz = 2.0 * dy_ref[...].astype(f32)
        wpg = wpg_ref[...].reshape(D, D)
        gate = _sigmoid(_dot(x_ref[...], wpg))
        proj = _dot(p_ref[...], wpp_ref[...])
        dgp = (dz * proj * gate * (1.0 - gate)).astype(bf16)
        dgp_ref[...] = dgp
        dpj_ref[...] = (dz * gate).astype(bf16)
        dx_ref[...] = dxf_ref[...] + _dot_nt(dgp, wpg)

    row = pl.BlockSpec((TM, D), lambda i: (i, 0))
    return pl.pallas_call(
        body, grid=(T // TM,),
        in_specs=[row, pl.BlockSpec((TM, PLE), lambda i: (i, 0)), row, row,
                  pl.BlockSpec((NDEV, None, 128, D), lambda i: (0, 3 * layer + 2, 0, 0)), _full_spec((PLE, D))],
        out_specs=[row, row, row],
        out_shape=[jax.ShapeDtypeStruct((T, D), f32), jax.ShapeDtypeStruct((T, D), bf16),
                   jax.ShapeDtypeStruct((T, D), bf16)],
        compiler_params=_cparams(("arbitrary",)), name="ple_bwd")(x, p, dy, dx_ffn, pr, wpp)


def _matmul_tn(a, b, nb, name, nsub=1):
    T, M = a.shape
    N = b.shape[1]
    tk = min(T, 1024)
    nk = T // tk
    wide = nsub * nb

    def body(a_ref, b_ref, o_ref, acc):
        k = pl.program_id(1)

        @pl.when(k == 0)
        def _():
            acc[...] = jnp.zeros_like(acc)

        acc[...] += _dot_tn(a_ref[...], b_ref[...])

        @pl.when(k == nk - 1)
        def _():
            for j in range(nsub):
                o_ref[j] = acc[:, j * nb:(j + 1) * nb].astype(bf16)

    return pl.pallas_call(
        body, grid=(N // wide, nk),
        in_specs=[pl.BlockSpec((tk, M), lambda n, k: (k, 0)), pl.BlockSpec((tk, wide), lambda n, k: (k, n))],
        out_specs=pl.BlockSpec((nsub, M, nb), lambda n, k: (n, 0, 0)),
        out_shape=jax.ShapeDtypeStruct((N // nb, M, nb), bf16),
        scratch_shapes=[pltpu.VMEM((M, wide), f32)],
        compiler_params=_cparams(("arbitrary", "arbitrary")), name=name)(a, b)


def _proj_in(x, pr, pinl, ping, layer):
    T = x.shape[0]

    def body(x_ref, wr_ref, wl_ref, wg_ref, hr_ref, hl_ref, hg_ref):
        xb = x_ref[...].astype(bf16)
        hr_ref[...] = _dot(xb, wr_ref[...].reshape(D, D))
        hl_ref[...] = _dot(xb, wl_ref[...].reshape(D, 2 * LRU_W))
        hg_ref[...] = _dot(xb, wg_ref[...].reshape(D, GDN_IN))

    return pl.pallas_call(
        body, grid=(T // TM,),
        in_specs=[pl.BlockSpec((TM, D), lambda i: (i, 0)),
                  pl.BlockSpec((NDEV, None, 128, D), lambda i: (0, 3 * layer, 0, 0)),
                  pl.BlockSpec((NDEV, None, 128, 2 * LRU_W), lambda i: (0, layer, 0, 0)),
                  pl.BlockSpec((NDEV, None, 128, GDN_IN), lambda i: (0, layer, 0, 0))],
        out_specs=[pl.BlockSpec((TM, D), lambda i: (i, 0)), pl.BlockSpec((TM, 2 * LRU_W), lambda i: (i, 0)),
                   pl.BlockSpec((TM, GDN_IN), lambda i: (i, 0))],
        out_shape=[jax.ShapeDtypeStruct((T, D), f32), jax.ShapeDtypeStruct((T, 2 * LRU_W), f32),
                   jax.ShapeDtypeStruct((T, GDN_IN), f32)],
        compiler_params=_cparams(("arbitrary",)), name="proj_in")(x, pr, pinl, ping)


def _proj_in_bwd(base, dhr, dhl, dhg, pr, pinl, ping, layer):
    T = base.shape[0]

    def body(b_ref, dr_ref, dl_ref, dg_ref, wr_ref, wl_ref, wg_ref, o_ref):
        o_ref[...] = (b_ref[...] + _dot_nt(dr_ref[...], wr_ref[...].reshape(D, D))
                      + _dot_nt(dl_ref[...], wl_ref[...].reshape(D, 2 * LRU_W))
                      + _dot_nt(dg_ref[...], wg_ref[...].reshape(D, GDN_IN)))

    return pl.pallas_call(
        body, grid=(T // TM,),
        in_specs=[pl.BlockSpec((TM, D), lambda i: (i, 0)), pl.BlockSpec((TM, D), lambda i: (i, 0)),
                  pl.BlockSpec((TM, 2 * LRU_W), lambda i: (i, 0)), pl.BlockSpec((TM, GDN_IN), lambda i: (i, 0)),
                  pl.BlockSpec((NDEV, None, 128, D), lambda i: (0, 3 * layer, 0, 0)),
                  pl.BlockSpec((NDEV, None, 128, 2 * LRU_W), lambda i: (0, layer, 0, 0)),
                  pl.BlockSpec((NDEV, None, 128, GDN_IN), lambda i: (0, layer, 0, 0))],
        out_specs=pl.BlockSpec((TM, D), lambda i: (i, 0)),
        out_shape=jax.ShapeDtypeStruct((T, D), f32),
        compiler_params=_cparams(("arbitrary",)), name="proj_in_bwd")(base, dhr, dhl, dhg, pr, pinl, ping)


def _mix_out(x1, o_r, o_l, o_g, pr, lg, lb, layer):
    T = x1.shape[0]

    def body(x_ref, r_ref, l_ref, g_ref, w_ref, lg_ref, lb_ref, z_ref, o_ref):
        w = w_ref[...].reshape(D, D)
        z = (ALPHA * x_ref[...] + _dot(r_ref[...], w[0:RET_W]) + _dot(l_ref[...], w[RET_W:RET_W + LRU_W])
             + _dot(g_ref[...], w[RET_W + LRU_W:D]))
        z_ref[...] = z
        o_ref[...] = _ln_stats(z)[0] * lg_ref[...] + lb_ref[...]

    row = pl.BlockSpec((TM, D), lambda i: (i, 0))
    return pl.pallas_call(
        body, grid=(T // TM,),
        in_specs=[row, pl.BlockSpec((TM, RET_W), lambda i: (i, 0)), pl.BlockSpec((TM, LRU_W), lambda i: (i, 0)),
                  pl.BlockSpec((TM, GDN_W), lambda i: (i, 0)),
                  pl.BlockSpec((NDEV, None, 128, D), lambda i: (0, 3 * layer + 1, 0, 0)),
                  _full_spec((1, D)), _full_spec((1, D))],
        out_specs=[row, row], out_shape=[jax.ShapeDtypeStruct((T, D), f32)] * 2,
        compiler_params=_cparams(("arbitrary",)), name="mix_out")(x1, o_r, o_l, o_g, pr, lg, lb)


def _mix_out_bwd(z, dout, pr, lg, layer):
    T = z.shape[0]

    def body(z_ref, do_ref, w_ref, lg_ref, dxb_ref, dzb_ref, dr_ref, dl_ref, dg_ref, dlg_ref, dlb_ref):
        @pl.when(pl.program_id(0) == 0)
        def _():
            dlg_ref[...] = jnp.zeros_like(dlg_ref)
            dlb_ref[...] = jnp.zeros_like(dlb_ref)

        dz, dlg, dlb = _ln_bwd(z_ref[...], lg_ref[...], do_ref[...])
        dlg_ref[...] += dlg
        dlb_ref[...] += dlb
        dxb_ref[...] = ALPHA * dz
        dzb = dz.astype(bf16)
        dzb_ref[...] = dzb
        w = w_ref[...].reshape(D, D)
        dr_ref[...] = _dot_nt(dzb, w[0:RET_W])
        dl_ref[...] = _dot_nt(dzb, w[RET_W:RET_W + LRU_W])
        dg_ref[...] = _dot_nt(dzb, w[RET_W + LRU_W:D])

    row = pl.BlockSpec((TM, D), lambda i: (i, 0))
    vec = _full_spec((1, D))
    return pl.pallas_call(
        body, grid=(T // TM,),
        in_specs=[row, row, pl.BlockSpec((NDEV, None, 128, D), lambda i: (0, 3 * layer + 1, 0, 0)), vec],
        out_specs=[row, row, pl.BlockSpec((TM, RET_W), lambda i: (i, 0)), pl.BlockSpec((TM, LRU_W), lambda i: (i, 0)),
                   pl.BlockSpec((TM, GDN_W), lambda i: (i, 0)), vec, vec],
        out_shape=[jax.ShapeDtypeStruct((T, D), f32), jax.ShapeDtypeStruct((T, D), bf16),
                   jax.ShapeDtypeStruct((T, RET_W), f32), jax.ShapeDtypeStruct((T, LRU_W), f32),
                   jax.ShapeDtypeStruct((T, GDN_W), f32), jax.ShapeDtypeStruct((1, D), f32),
                   jax.ShapeDtypeStruct((1, D), f32)],
        compiler_params=_cparams(("arbitrary",)), name="mix_out_bwd")(z, dout, pr, lg)


def _loss_grad(y, target):
    T = y.shape[0]

    def body(y_ref, t_ref, dy_ref, l_ref):
        @pl.when(pl.program_id(0) == 0)
        def _():
            l_ref[...] = jnp.zeros_like(l_ref)

        e = y_ref[...] - t_ref[...]
        dy_ref[...] = e * (1.0 / D)
        l_ref[...] += 0.5 * jnp.sum(jnp.sum(e * e, -1, keepdims=True) * (1.0 / D), 0, keepdims=True)

    row = pl.BlockSpec((TM, D), lambda i: (i, 0))
    return pl.pallas_call(
        body, grid=(T // TM,), in_specs=[row, row], out_specs=[row, _full_spec((1, 1))],
        out_shape=[jax.ShapeDtypeStruct((T, D), f32), jax.ShapeDtypeStruct((1, 1), f32)],
        compiler_params=_cparams(("arbitrary",)), name="loss_grad")(y, target)


def _split_heads(x, H):
    n = x.shape[0] // CH
    parts = [x[:, h * CH:(h + 1) * CH].reshape(n, CH, CH) for h in range(H)]
    return jnp.stack(parts, axis=1).reshape(n * H, CH, CH)


def _merge_heads(ref, x, H, col0=0):
    n = x.shape[0] // H
    x4 = x.reshape(n, H, CH, CH)
    for h in range(H):
        ref[:, col0 + h * CH:col0 + (h + 1) * CH] = x4[:, h].reshape(n * CH, CH)


def _conv_fwd(ext, x, tail, w, R):
    ext[0:8, :] = tail
    ext[8:R + 8, :] = x
    y = w[3:4, :] * x
    for k in range(3):
        y = y + w[k:k + 1, :] * ext[5 + k:5 + k + R, :]
    return y


def _conv_bwd(ext, ext2, dy, dy_next, w, R):
    ext2[0:R, :] = dy
    ext2[R:R + 8, :] = dy_next
    dx = w[3:4, :] * dy
    dws = []
    for k in range(3):
        dx = dx + w[k:k + 1, :] * ext2[3 - k:3 - k + R, :]
        dws.append(jnp.sum(dy * ext[5 + k:5 + k + R, :], 0, keepdims=True))
    dws.append(jnp.sum(dy * ext[8:8 + R, :], 0, keepdims=True))
    return dx, jnp.concatenate(dws, axis=0)


def _prev_tail_spec(R, W):
    return pl.BlockSpec((8, W), lambda i: (jnp.maximum(i * (R // 8) - 1, 0), 0))


def _prev_tail_spec_rev(R, W, nb):
    return pl.BlockSpec((8, W), lambda i: (jnp.maximum((nb - 1 - i) * (R // 8) - 1, 0), 0))


def _rope_tables(positions):
    T = positions.shape[0]

    def body(p_ref, c_ref, s_ref):
        lane = lax.broadcasted_iota(jnp.int32, (TM, RET_W), 1)
        fi = (lane % 32).astype(f32)
        inv = jnp.exp(fi * (-math.log(ROPE_THETA) / 32.0))
        ang = p_ref[...].astype(f32) * inv
        c_ref[...] = jnp.cos(ang)
        s_ref[...] = jnp.where(lane % CH < 32, -jnp.sin(ang), jnp.sin(ang))

    row = pl.BlockSpec((TM, RET_W), lambda i: (i, 0))
    return pl.pallas_call(
        body, grid=(T // TM,), in_specs=[pl.BlockSpec((TM, 1), lambda i: (i, 0))], out_specs=[row, row],
        out_shape=[jax.ShapeDtypeStruct((T, RET_W), f32)] * 2,
        compiler_params=_cparams(("arbitrary",)), name="rope_tables")(positions)


def _partner(x):
    lane = lax.broadcasted_iota(jnp.int32, x.shape, 1)
    return jnp.where(lane % CH < 32, pltpu.roll(x, RET_W - 32, 1), pltpu.roll(x, 32, 1))


def _ret_consts():
    ii = lax.broadcasted_iota(jnp.int32, (CH, CH), 0).astype(f32)
    jj = lax.broadcasted_iota(jnp.int32, (CH, CH), 1).astype(f32)
    intra, cross, tail, cd = [], [], [], []
    for h in range(RET_H):
        lg = math.log1p(-(2.0 ** (-5.0 - h)))
        intra.append(jnp.exp(jnp.abs(ii - jj) * lg))
        cross.append(jnp.exp((ii + 1.0) * lg))
        tail.append(jnp.exp((CH - 1.0 - ii) * lg))
        cd.append(jnp.full((CH, CH), math.exp(CH * lg), f32))
    return jnp.stack(intra), jnp.stack(cross), jnp.stack(tail), jnp.stack(cd)


def _ret_chunk(consts, q, k, v, st):
    intra, cross, tail, cd = consts
    s = _bmm('hid,hjd->hij', q, k) * intra
    o = _bmm('hij,hje->hie', s, v) + _bmm('hid,hde->hie', q * cross, st)
    st2 = st * cd + _bmm('hjd,hje->hde', k * tail, v)
    mu = jnp.mean(o, -1, keepdims=True)
    oc = o - mu
    on = oc * lax.rsqrt(jnp.mean(oc * oc, -1, keepdims=True) + 1e-5)
    return on, st2


def _ret_fwd(hr, cosw, sinw, gam):
    T = hr.shape[0]
    R = RB_RET
    nc = R // CH

    def body(h_ref, c_ref, s_ref, g_ref, o_ref, st_ref, st, wide):
        @pl.when(pl.program_id(0) == 0)
        def _():
            st[...] = jnp.zeros_like(st)

        consts = _ret_consts()
        cw, sw = c_ref[...], s_ref[...]
        q, k = h_ref[:, 0:RET_W], h_ref[:, RET_W:2 * RET_W]
        qh = _split_heads((q * cw + _partner(q) * sw) * 0.125, RET_H)
        kh = _split_heads(k * cw + _partner(k) * sw, RET_H)
        vh = _split_heads(h_ref[:, 2 * RET_W:3 * RET_W], RET_H)
        outs = []
        s_cur = st[...]
        for c in range(nc):
            sl = slice(c * RET_H, (c + 1) * RET_H)
            st_ref[c] = s_cur
            on, s_cur = _ret_chunk(consts, qh[sl], kh[sl], vh[sl], s_cur)
            outs.append(on)
        st[...] = s_cur
        _merge_heads(wide, jnp.concatenate(outs, axis=0), RET_H)
        o_ref[...] = wide[...] * g_ref[...] * _silu(h_ref[:, 3 * RET_W:4 * RET_W])

    blk = pl.BlockSpec((R, RET_W), lambda i: (i, 0))
    return pl.pallas_call(
        body, grid=(T // R,),
        in_specs=[pl.BlockSpec((R, D), lambda i: (i, 0)), blk, blk, _full_spec((1, RET_W))],
        out_specs=[blk, pl.BlockSpec((nc, RET_H, CH, CH), lambda i: (i, 0, 0, 0))],
        out_shape=[jax.ShapeDtypeStruct((T, RET_W), f32), jax.ShapeDtypeStruct((T // CH, RET_H, CH, CH), f32)],
        scratch_shapes=[pltpu.VMEM((RET_H, CH, CH), f32), pltpu.VMEM((R, RET_W), f32)],
        compiler_params=_cparams(("arbitrary",)), name="ret_fwd")(hr, cosw, sinw, gam)


def _ret_bwd(hr, cosw, sinw, gam, states, dout):
    T = hr.shape[0]
    R = RB_RET
    nc = R // CH
    nb = T // R

    def body(h_ref, c_ref, s_ref, g_ref, st_ref, do_ref, dh_ref, dgam_ref, dst, wide):
        @pl.when(pl.program_id(0) == 0)
        def _():
            dst[...] = jnp.zeros_like(dst)
            dgam_ref[...] = jnp.zeros_like(dgam_ref)

        consts = _ret_consts()
        cw, sw = c_ref[...], s_ref[...]
        q, k = h_ref[:, 0:RET_W], h_ref[:, RET_W:2 * RET_W]
        gr = h_ref[:, 3 * RET_W:4 * RET_W]
        qh = _split_heads((q * cw + _partner(q) * sw) * 0.125, RET_H)
        kh = _split_heads(k * cw + _partner(k) * sw, RET_H)
        vh = _split_heads(h_ref[:, 2 * RET_W:3 * RET_W], RET_H)
        do = do_ref[...]
        gam = g_ref[...]
        sg = _silu(gr)
        don = _split_heads(do * gam * sg, RET_H)
        ons, dqs, dks, dvs = [None] * nc, [None] * nc, [None] * nc, [None] * nc
        ds = dst[...]
        for c in reversed(range(nc)):
            sl = slice(c * RET_H, (c + 1) * RET_H)
            (on, _), vjp = jax.vjp(functools.partial(_ret_chunk, consts), qh[sl], kh[sl], vh[sl], st_ref[c])
            dqs[c], dks[c], dvs[c], ds = vjp((don[sl], ds))
            ons[c] = on
        dst[...] = ds
        _merge_heads(wide, jnp.concatenate(ons, axis=0), RET_H)
        onw = wide[...]
        dgam_ref[...] += jnp.sum(do * onw * sg, 0, keepdims=True)
        dh_ref[:, 3 * RET_W:4 * RET_W] = do * onw * gam * _dsilu(gr)
        _merge_heads(wide, jnp.concatenate(dqs, axis=0), RET_H)
        u = wide[...] * 0.125
        dh_ref[:, 0:RET_W] = u * cw + _partner(u * sw)
        _merge_heads(wide, jnp.concatenate(dks, axis=0), RET_H)
        u = wide[...]
        dh_ref[:, RET_W:2 * RET_W] = u * cw + _partner(u * sw)
        _merge_heads(dh_ref, jnp.concatenate(dvs, axis=0), RET_H, col0=2 * RET_W)

    blk = pl.BlockSpec((R, RET_W), lambda i: (nb - 1 - i, 0))
    return pl.pallas_call(
        body, grid=(nb,),
        in_specs=[pl.BlockSpec((R, D), lambda i: (nb - 1 - i, 0)), blk, blk, _full_spec((1, RET_W)),
                  pl.BlockSpec((nc, RET_H, CH, CH), lambda i: (nb - 1 - i, 0, 0, 0)), blk],
        out_specs=[pl.BlockSpec((R, D), lambda i: (nb - 1 - i, 0)), _full_spec((1, RET_W))],
        out_shape=[jax.ShapeDtypeStruct((T, D), f32), jax.ShapeDtypeStruct((1, RET_W), f32)],
        scratch_shapes=[pltpu.VMEM((RET_H, CH, CH), f32), pltpu.VMEM((R, RET_W), f32)],
        compiler_params=_cparams(("arbitrary",)), name="ret_bwd")(hr, cosw, sinw, gam, states, dout)


def _lru_ab(xc, wa, ba, wx, bx, lam):
    r = _sigmoid(_dot(xc, wa) + ba)
    i = _sigmoid(_dot(xc, wx) + bx)
    la = 8.0 * r * (-_softplus(-lam))
    a = jnp.exp(la)
    em = jnp.tanh(la) * (jnp.exp(2.0 * la) + 1.0)
    return a, jnp.sqrt(-em) * (i * xc)


def _lru_out(h, gate):
    return h * _gelu(gate)


def _scan_fwd(a, b):
    R = a.shape[0]
    row = lax.broadcasted_iota(jnp.int32, a.shape, 0)
    d = 1
    while d < R:
        m = row >= d
        b = jnp.where(m, a * pltpu.roll(b, d, 0) + b, b)
        a = jnp.where(m, a * pltpu.roll(a, d, 0), a)
        d *= 2
    return a, b


def _scan_bwd(a, b):
    R = a.shape[0]
    row = lax.broadcasted_iota(jnp.int32, a.shape, 0)
    d = 1
    while d < R:
        m = row < R - d
        b = jnp.where(m, a * pltpu.roll(b, R - d, 0) + b, b)
        a = jnp.where(m, a * pltpu.roll(a, R - d, 0), a)
        d *= 2
    return b


def _lru_fwd(hl, cw, cb, wa, ba, wx, bx, lam):
    T = hl.shape[0]
    R = RB_LRU
    W = LRU_W

    def body(h_ref, t_ref, cw_ref, cb_ref, wa_ref, ba_ref, wx_ref, bx_ref, lam_ref, o_ref, hs_ref, carry, ext):
        first = pl.program_id(0) == 0

        @pl.when(first)
        def _():
            carry[...] = jnp.zeros_like(carry)

        tail = jnp.where(first, 0.0, t_ref[:, 0:W])
        xc = _conv_fwd(ext, h_ref[:, 0:W], tail, cw_ref[...], R) + cb_ref[...]
        a, b = _lru_ab(xc, wa_ref[...], ba_ref[...], wx_ref[...], bx_ref[...], lam_ref[...])
        ap, hloc = _scan_fwd(a, b)
        h = hloc + ap * carry[0:1, :]
        carry[...] = jnp.broadcast_to(h[R - 1:R, :], carry.shape)
        hs_ref[...] = h
        o_ref[...] = _lru_out(h, h_ref[:, W:2 * W])

    vec = _full_spec((1, W))
    blk = pl.BlockSpec((R, W), lambda i: (i, 0))
    return pl.pallas_call(
        body, grid=(T // R,),
        in_specs=[pl.BlockSpec((R, 2 * W), lambda i: (i, 0)), _prev_tail_spec(R, 2 * W), _full_spec((4, W)), vec,
                  _full_spec((W, W)), vec, _full_spec((W, W)), vec, vec],
        out_specs=[blk, blk], out_shape=[jax.ShapeDtypeStruct((T, W), f32)] * 2,
        scratch_shapes=[pltpu.VMEM((8, W), f32), pltpu.VMEM((R + 8, W), f32)],
        compiler_params=_cparams(("arbitrary",)), name="lru_fwd")(hl, hl, cw, cb, wa, ba, wx, bx, lam)


def _lru_bwd(hl, hs, cw, cb, wa, ba, wx, bx, lam, dout):
    T = hl.shape[0]
    R = RB_LRU
    W = LRU_W
    nb = T // R

    def body(h_ref, t_ref, hs_ref, hst_ref, cw_ref, cb_ref, wa_ref, ba_ref, wx_ref, bx_ref, lam_ref, do_ref,
             dh_ref, dcw_ref, dcb_ref, dwa_ref, dba_ref, dwx_ref, dbx_ref, dlam_ref, carry_g, carry_dy, ext, ext2):
        i = pl.program_id(0)
        last_blk = i == 0
        first_blk = i == nb - 1

        @pl.when(last_blk)
        def _():
            carry_g[...] = jnp.zeros_like(carry_g)
            carry_dy[...] = jnp.zeros_like(carry_dy)
            for r in (dcw_ref, dcb_ref, dwa_ref, dba_ref, dwx_ref, dbx_ref, dlam_ref):
                r[...] = jnp.zeros_like(r)

        tail = jnp.where(first_blk, 0.0, t_ref[:, 0:W])
        xc = _conv_fwd(ext, h_ref[:, 0:W], tail, cw_ref[...], R) + cb_ref[...]
        (a, _), vjp_ab = jax.vjp(_lru_ab, xc, wa_ref[...], ba_ref[...], wx_ref[...], bx_ref[...], lam_ref[...])
        hs = hs_ref[...]
        _, vjp_out = jax.vjp(_lru_out, hs, h_ref[:, W:2 * W])
        dh, dgate = vjp_out(do_ref[...])
        row = lax.broadcasted_iota(jnp.int32, (R, W), 0)
        dh = jnp.where(row == R - 1, dh + carry_g[0:1, :], dh)
        a_up = jnp.where(row == R - 1, 0.0, pltpu.roll(a, R - 1, 0))
        g = _scan_bwd(a_up, dh)
        carry_g[...] = jnp.broadcast_to(a[0:1, :] * g[0:1, :], carry_g.shape)
        hprev0 = jnp.where(first_blk, 0.0, hst_ref[7:8, :])
        hprev = jnp.where(row == 0, hprev0, pltpu.roll(hs, 1, 0))
        dxc, dwa, dba, dwx, dbx, dlam = vjp_ab((g * hprev, g))
        dwa_ref[...] += dwa
        dba_ref[...] += dba
        dwx_ref[...] += dwx
        dbx_ref[...] += dbx
        dlam_ref[...] += dlam
        dcb_ref[...] += jnp.sum(dxc, 0, keepdims=True)
        dx, dcw = _conv_bwd(ext, ext2, dxc, carry_dy[...], cw_ref[...], R)
        carry_dy[...] = dxc[0:8, :]
        dcw_ref[...] += dcw
        dh_ref[:, 0:W] = dx
        dh_ref[:, W:2 * W] = dgate

    vec = _full_spec((1, W))
    mat = _full_spec((W, W))
    blk = pl.BlockSpec((R, W), lambda i: (nb - 1 - i, 0))
    blk2 = pl.BlockSpec((R, 2 * W), lambda i: (nb - 1 - i, 0))
    return pl.pallas_call(
        body, grid=(nb,),
        in_specs=[blk2, _prev_tail_spec_rev(R, 2 * W, nb), blk, _prev_tail_spec_rev(R, W, nb), _full_spec((4, W)), vec,
                  mat, vec, mat, vec, vec, blk],
        out_specs=[blk2, _full_spec((4, W)), vec, mat, vec, mat, vec, vec],
        out_shape=[jax.ShapeDtypeStruct((T, 2 * W), f32), jax.ShapeDtypeStruct((4, W), f32),
                   jax.ShapeDtypeStruct((1, W), f32), jax.ShapeDtypeStruct((W, W), f32),
                   jax.ShapeDtypeStruct((1, W), f32), jax.ShapeDtypeStruct((W, W), f32),
                   jax.ShapeDtypeStruct((1, W), f32), jax.ShapeDtypeStruct((1, W), f32)],
        scratch_shapes=[pltpu.VMEM((8, W), f32), pltpu.VMEM((8, W), f32), pltpu.VMEM((R + 8, W), f32),
                        pltpu.VMEM((R + 8, W), f32)],
        compiler_params=_cparams(("arbitrary",)), name="lru_bwd")(hl, hl, hs, hs, cw, cb, wa, ba, wx, bx, lam, dout)


def _gdn_local(qs, ks, vs, gc, bb):
    B = qs.shape[0]
    ii = lax.broadcasted_iota(jnp.int32, (B, CH, CH), 1)
    jj = lax.broadcasted_iota(jnp.int32, (B, CH, CH), 2)
    q = qs * lax.rsqrt(jnp.sum(qs * qs, -1, keepdims=True) + 1e-6)
    k = ks * lax.rsqrt(jnp.sum(ks * ks, -1, keepdims=True) + 1e-6)
    gct = jnp.swapaxes(gc, 1, 2)
    decay = jnp.where(ii >= jj, jnp.exp(jnp.minimum(gc - gct, 0.0)), 0.0)
    kk = _bmm('bid,bjd->bij', k, k)
    inv = _neumann_inv(-jnp.where(ii > jj, bb * kk * decay, 0.0))
    egc = jnp.exp(gc)
    u = _bmm3('bij,bje->bie', inv, vs * bb)
    w = _bmm3('bij,bje->bie', inv, k * (bb * egc))
    qk = _bmm('bid,bjd->bij', q, k) * (0.125 * decay)
    glast = gc[:, CH - 1:CH, :]
    return u, w, qk, q * (0.125 * egc), k * jnp.exp(glast - gc), jnp.exp(jnp.broadcast_to(glast, gc.shape))


def _gdn_step(st, u, w, qk, qd, kt, egl, z, gn):
    vnew = u - _bmm('hcd,hde->hce', w, st)
    o = _bmm('hcd,hde->hce', qd, st) + _bmm('hij,hje->hie', qk, vnew)
    st2 = st * egl + _bmm('hcd,hce->hde', kt, vnew)
    out = o * lax.rsqrt(jnp.mean(o * o, -1, keepdims=True) + 1e-6) * gn * _silu(z)
    return out, st2


def _gdn_scalars(ab, alog, dtb):
    sp = _softplus(ab + dtb)
    return -jnp.exp(alog) * sp, _sigmoid(ab)


def _bcast_heads(blk, lane0, H):
    R = blk.shape[0]
    n = R // CH
    parts = [jnp.broadcast_to(blk[:, lane0 + h:lane0 + h + 1], (R, CH)).reshape(n, CH, CH) for h in range(H)]
    return jnp.stack(parts, axis=1).reshape(n * H, CH, CH)


def _unbcast_heads(x, lane0, H):
    n = x.shape[0] // H
    R = n * CH
    s = jnp.sum(x, axis=2, keepdims=True).reshape(n, H, CH, 1)
    lane = lax.broadcasted_iota(jnp.int32, (R, 128), 1)
    acc = jnp.zeros((R, 128), f32)
    for h in range(H):
        acc = acc + jnp.where(lane == lane0 + h, jnp.broadcast_to(s[:, h].reshape(R, 1), (R, 128)), 0.0)
    return acc


def _gdn_fwd(hg, cw, alog, dtb, gn):
    T = hg.shape[0]
    R = RB_GDN
    nc = R // CH
    W3 = 3 * GDN_W
    H = GDN_H

    def body(h_ref, t_ref, cw_ref, al_ref, dt_ref, gn_ref, o_ref, st_ref, st, ext):
        first = pl.program_id(0) == 0

        @pl.when(first)
        def _():
            st[...] = jnp.zeros_like(st)

        tail = jnp.where(first, 0.0, t_ref[:, 0:W3])
        y = _silu(_conv_fwd(ext, h_ref[:, 0:W3], tail, cw_ref[...], R))
        qs, ks, vs = (_split_heads(y[:, j * GDN_W:(j + 1) * GDN_W], H) for j in range(3))
        zh = _split_heads(h_ref[:, W3:W3 + GDN_W], H)
        g, beta = _gdn_scalars(h_ref[:, W3 + GDN_W:GDN_IN], al_ref[...], dt_ref[...])
        loc = _gdn_local(qs, ks, vs, _bcast_heads(_chunk_cumsum(g), 0, H), _bcast_heads(beta, H, H))
        gnv = gn_ref[...]
        outs = []
        s_cur = st[...]
        for c in range(nc):
            sl = slice(c * H, (c + 1) * H)
            st_ref[c] = s_cur
            out, s_cur = _gdn_step(s_cur, *(t[sl] for t in loc), zh[sl], gnv)
            outs.append(out)
        st[...] = s_cur
        _merge_heads(o_ref, jnp.concatenate(outs, axis=0), H)

    return pl.pallas_call(
        body, grid=(T // R,),
        in_specs=[pl.BlockSpec((R, GDN_IN), lambda i: (i, 0)), _prev_tail_spec(R, GDN_IN), _full_spec((4, W3)),
                  _full_spec((1, 128)), _full_spec((1, 128)), _full_spec((1, CH))],
        out_specs=[pl.BlockSpec((R, GDN_W), lambda i: (i, 0)), pl.BlockSpec((nc, H, CH, CH), lambda i: (i, 0, 0, 0))],
        out_shape=[jax.ShapeDtypeStruct((T, GDN_W), f32), jax.ShapeDtypeStruct((T // CH, H, CH, CH), f32)],
        scratch_shapes=[pltpu.VMEM((H, CH, CH), f32), pltpu.VMEM((R + 8, W3), f32)],
        compiler_params=_cparams(("arbitrary",)), name="gdn_fwd")(hg, hg, cw, alog, dtb, gn)


def _gdn_bwd(hg, cw, alog, dtb, gn, states, dout):
    T = hg.shape[0]
    R = RB_GDN
    nc = R // CH
    nb = T // R
    W3 = 3 * GDN_W
    H = GDN_H

    def body(h_ref, t_ref, cw_ref, al_ref, dt_ref, gn_ref, st_ref, do_ref,
             dh_ref, dcw_ref, dal_ref, ddt_ref, dgn_ref, dst, carry_dy, ext, ext2, wide):
        i = pl.program_id(0)
        first_blk = i == nb - 1

        @pl.when(i == 0)
        def _():
            dst[...] = jnp.zeros_like(dst)
            carry_dy[...] = jnp.zeros_like(carry_dy)
            for r in (dcw_ref, dal_ref, ddt_ref, dgn_ref):
                r[...] = jnp.zeros_like(r)

        tail = jnp.where(first_blk, 0.0, t_ref[:, 0:W3])
        ypre = _conv_fwd(ext, h_ref[:, 0:W3], tail, cw_ref[...], R)
        y = _silu(ypre)
        qs, ks, vs = (_split_heads(y[:, j * GDN_W:(j + 1) * GDN_W], H) for j in range(3))
        zh = _split_heads(h_ref[:, W3:W3 + GDN_W], H)
        ab = h_ref[:, W3 + GDN_W:GDN_IN]
        alog, dtb = al_ref[...], dt_ref[...]
        g, beta = _gdn_scalars(ab, alog, dtb)
        loc, vjp_loc = jax.vjp(_gdn_local, qs, ks, vs, _bcast_heads(_chunk_cumsum(g), 0, H),
                               _bcast_heads(beta, H, H))
        doh = _split_heads(do_ref[...], H)
        gnv = gn_ref[...]
        dloc = [[None] * nc for _ in range(6)]
        dzs = [None] * nc
        ds = dst[...]
        dgn = jnp.zeros((1, CH), f32)
        for c in reversed(range(nc)):
            sl = slice(c * H, (c + 1) * H)
            _, vjp = jax.vjp(_gdn_step, st_ref[c], *(t[sl] for t in loc), zh[sl], gnv)
            grads = vjp((doh[sl], ds))
            ds = grads[0]
            for j in range(6):
                dloc[j][c] = grads[1 + j]
            dzs[c] = grads[7]
            dgn = dgn + grads[8]
        dst[...] = ds
        dgn_ref[...] += dgn
        dqs, dks, dvs, dgb, dbb = vjp_loc(tuple(jnp.concatenate(d, axis=0) for d in dloc))
        lane = lax.broadcasted_iota(jnp.int32, (R, 128), 1)
        dg = _chunk_cumsum(_unbcast_heads(dgb, 0, H), reverse=True)
        dbeta = _unbcast_heads(dbb, H, H)
        da = dg * (-jnp.exp(alog)) * _sigmoid(ab + dtb)
        dh_ref[:, W3 + GDN_W:GDN_IN] = jnp.where(lane < H, da, dbeta * beta * (1.0 - beta))
        ddt_ref[...] += jnp.sum(jnp.where(lane < H, da, 0.0), 0, keepdims=True)
        dal_ref[...] += jnp.sum(jnp.where(lane < H, dg * g, 0.0), 0, keepdims=True)
        _merge_heads(dh_ref, jnp.concatenate(dzs, axis=0), H, col0=W3)
        for j, dpart in enumerate((dqs, dks, dvs)):
            _merge_heads(wide, dpart, H, col0=j * GDN_W)
        dy = wide[...] * _dsilu(ypre)
        dx, dcw = _conv_bwd(ext, ext2, dy, carry_dy[...], cw_ref[...], R)
        carry_dy[...] = dy[0:8, :]
        dcw_ref[...] += dcw
        dh_ref[:, 0:W3] = dx

    blk = pl.BlockSpec((R, GDN_IN), lambda i: (nb - 1 - i, 0))
    return pl.pallas_call(
        body, grid=(nb,),
        in_specs=[blk, _prev_tail_spec_rev(R, GDN_IN, nb), _full_spec((4, W3)), _full_spec((1, 128)),
                  _full_spec((1, 128)), _full_spec((1, CH)),
                  pl.BlockSpec((nc, H, CH, CH), lambda i: (nb - 1 - i, 0, 0, 0)),
                  pl.BlockSpec((R, GDN_W), lambda i: (nb - 1 - i, 0))],
        out_specs=[blk, _full_spec((4, W3)), _full_spec((1, 128)), _full_spec((1, 128)), _full_spec((1, CH))],
        out_shape=[jax.ShapeDtypeStruct((T, GDN_IN), f32), jax.ShapeDtypeStruct((4, W3), f32),
                   jax.ShapeDtypeStruct((1, 128), f32), jax.ShapeDtypeStruct((1, 128), f32),
                   jax.ShapeDtypeStruct((1, CH), f32)],
        scratch_shapes=[pltpu.VMEM((H, CH, CH), f32), pltpu.VMEM((8, W3), f32), pltpu.VMEM((R + 8, W3), f32),
                        pltpu.VMEM((R + 8, W3), f32), pltpu.VMEM((R, W3), f32)],
        compiler_params=_cparams(("arbitrary",)), name="gdn_bwd")(hg, hg, cw, alog, dtb, gn, states, dout)


def _block_diag(w):
    out = jnp.zeros((LRU_W, LRU_W), w.dtype)
    for g in range(w.shape[0]):
        out = lax.dynamic_update_slice(out, w[g], (g * CH, g * CH))
    return out


def _block_diag_t(w):
    return jnp.stack([w[g * CH:(g + 1) * CH, g * CH:(g + 1) * CH] for g in range(LRU_W // CH)])


def _pad_lanes(v, n=128):
    return jnp.pad(v, (0, n - v.shape[0]))[None, :]


def _local_step(x, p, positions, target, G, sm):
    pd, pr, pinl, ping, wpp = G['pd'], G['pr'], G['pinl'], G['ping'], G['wpp']
    p384 = jnp.transpose(G['p384'], (1, 2, 0, 3)).reshape(-1, D, NDEV * FSP)
    cosw, sinw = _rope_tables(positions)
    saved = []
    h = x
    for l in range(DEPTH):
        v = lambda n: sm[n][l][None, :]
        z1, x1, g1, u1 = _ffn_fwd(h, p384, pd, v('ln_ffn1_g'), v('ln_ffn1_b'), l, 0)
        hr, hl, hg = _proj_in(x1, pr, pinl, ping, l)
        o_r, rst = _ret_fwd(hr, cosw, sinw, v('ret_norm_g'))
        lru_args = (sm['lru_conv_w'][l], v('lru_conv_b'), _block_diag(sm['lru_w_a'][l]), v('lru_b_a'),
                    _block_diag(sm['lru_w_x'][l]), v('lru_b_x'), v('lru_lambda'))
        o_l, hs = _lru_fwd(hl, *lru_args)
        gdn_args = (sm['gdn_conv_w'][l], _pad_lanes(sm['gdn_a_log'][l]), _pad_lanes(sm['gdn_dt_bias'][l]),
                    v('gdn_norm_g'))
        o_g, gst = _gdn_fwd(hg, *gdn_args)
        z2, x2 = _mix_out(x1, o_r, o_l, o_g, pr, v('ln_mix_g'), v('ln_mix_b'), l)
        z3, x3, g2, u2 = _ffn_fwd(x2, p384, pd, v('ln_ffn2_g'), v('ln_ffn2_b'), l, 1, ple=(p[l], pr, wpp[l]))
        saved.append((h, z1, x1, hr, hl, hg, o_r, rst, o_l, hs, lru_args, o_g, gst, gdn_args, z2, x2, z3,
                      g1, u1, g2, u2))
        h = x3
    d, loss = _loss_grad(h, target)

    big = {k: [None] * DEPTH for k in ('p384', 'pd', 'pr', 'pinl', 'ping', 'ppp')}
    small = {n: [None] * DEPTH for n in SMALL}
    for l in reversed(range(DEPTH)):
        (x0, z1, x1, hr, hl, hg, o_r, rst, o_l, hs, lru_args, o_g, gst, gdn_args, z2, x2, z3,
         g1, u1, g2, u2) = saved[l]
        v = lambda n: sm[n][l][None, :]
        d2, dg2, du2, a2, dy2, small['ln_ffn2_g'][l], small['ln_ffn2_b'][l] = _ffn_bwd(
            z3, d, g2, u2, p384, pd, v('ln_ffn2_g'), l, 1)
        d2, dgp, dpj = _ple_bwd(x2, p[l], dy2, d2, pr, wpp[l], l)
        dxb, dzb, do_r, do_l, do_g, small['ln_mix_g'][l], small['ln_mix_b'][l] = _mix_out_bwd(
            z2, d2, pr, v('ln_mix_g'), l)
        dhr, small['ret_norm_g'][l] = _ret_bwd(hr, cosw, sinw, v('ret_norm_g'), rst, do_r)
        (dhl, small['lru_conv_w'][l], small['lru_conv_b'][l], dwa, small['lru_b_a'][l], dwx, small['lru_b_x'][l],
         small['lru_lambda'][l]) = _lru_bwd(hl, hs, *lru_args, do_l)
        small['lru_w_a'][l], small['lru_w_x'][l] = _block_diag_t(dwa), _block_diag_t(dwx)
        dhg, small['gdn_conv_w'][l], dal, ddt, small['gdn_norm_g'][l] = _gdn_bwd(hg, *gdn_args, gst, do_g)
        small['gdn_a_log'][l], small['gdn_dt_bias'][l] = dal[:, 0:GDN_H], ddt[:, 0:GDN_H]
        d1 = _proj_in_bwd(dxb, dhr, dhl, dhg, pr, pinl, ping, l)
        d, dg1, du1, a1, dy1, small['ln_ffn1_g'][l], small['ln_ffn1_b'][l] = _ffn_bwd(
            z1, d1, g1, u1, p384, pd, v('ln_ffn1_g'), l, 0)
        rows = lambda m: m.reshape(NDEV, m.shape[1] // NDEV, m.shape[2])
        big['p384'][l] = jnp.stack([_matmul_tn(x0, dg1, FSP, "dw_gate", FB), _matmul_tn(x0, du1, FSP, "dw_up", FB),
                                    _matmul_tn(x2, dg2, FSP, "dw_gate", FB), _matmul_tn(x2, du2, FSP, "dw_up", FB)],
                                   axis=1)
        big['pd'][l] = jnp.stack([rows(_matmul_tn(a1, dy1, D, "dw_down")),
                                  rows(_matmul_tn(a2, dy2, D, "dw_down"))], axis=1)
        dwo = jnp.concatenate([_matmul_tn(o_r, dzb, D, "dw_out_r"), _matmul_tn(o_l, dzb, D, "dw_out_l"),
                               _matmul_tn(o_g, dzb, D, "dw_out_g")], axis=1)
        big['pr'][l] = jnp.stack([rows(_matmul_tn(x1, dhr, D, "dw_in_r")), rows(dwo),
                                  rows(_matmul_tn(x2, dgp, D, "dw_ple_gate"))], axis=1)
        big['pinl'][l] = rows(_matmul_tn(x1, dhl, 2 * LRU_W, "dw_in_l"))
        big['ping'][l] = rows(_matmul_tn(x1, dhg, GDN_IN, "dw_in_g"))
        big['ppp'][l] = _matmul_tn(p[l], dpj, 128, "dw_ple_proj")
    big = {'p384': jnp.concatenate(big['p384'], axis=1), 'pd': jnp.concatenate(big['pd'], axis=1),
           'pr': jnp.concatenate(big['pr'], axis=1), 'pinl': jnp.stack(big['pinl'], axis=1),
           'ping': jnp.stack(big['ping'], axis=1), 'ppp': jnp.stack(big['ppp'], axis=1)}
    small = {n: jnp.stack([g.reshape(sm[n].shape[1:]) for g in gs]) for n, gs in small.items()}
    return loss, d, big, small


def _pack_big(ws, dtype=bf16):
    padc = lambda a, n: jnp.pad(a, ((0, 0), (0, 0), (0, n - a.shape[2])))
    padr = lambda a, n: jnp.pad(a, ((0, 0), (0, n - a.shape[1]), (0, 0)))
    per_layer = lambda arrs: jnp.stack(arrs, axis=1).reshape((-1,) + arrs[0].shape[1:])
    w_in = ws['w_in']
    out = {
        'p384': per_layer([padc(ws[n], FSP) for n in ('ffn1_w_gate', 'ffn1_w_up', 'ffn2_w_gate', 'ffn2_w_up')]),
        'pd': per_layer([padr(ws[n], FSP) for n in ('ffn1_w_down', 'ffn2_w_down')]),
        'pr': per_layer([w_in[:, :, 0:D], ws['w_out'], ws['ple_w_gate']]),
        'pinl': w_in[:, :, D:D + 2 * LRU_W],
        'ping': padc(w_in[:, :, D + 2 * LRU_W:D_IN], GDN_IN),
        'ppp': ws['ple_w_proj'],
    }
    return {k: a.astype(dtype) for k, a in out.items()}


def _exchange(arrays, scatter, name):
    n = len(arrays)

    def body(*refs):
        ins, outs = refs[:n], refs[n:2 * n]
        send_sems, recv_sems, local_sems = refs[2 * n:]
        x, y, c = lax.axis_index("x"), lax.axis_index("y"), lax.axis_index("c")
        me = 4 * x + 2 * y + c
        copies = []
        for i in range(n):
            src = ins[i].at[me] if scatter[i] else ins[i]
            cp = pltpu.make_async_copy(src, outs[i].at[me], local_sems.at[i])
            cp.start()
            copies.append(cp)
        sends = []
        for j in range(1, NDEV):
            peer = (me + j) % NDEV
            pid = (peer // 4, (peer // 2) % 2, peer % 2)
            for i in range(n):
                src = ins[i].at[peer] if scatter[i] else ins[i]
                cp = pltpu.make_async_remote_copy(
                    src_ref=src, dst_ref=outs[i].at[me], send_sem=send_sems.at[i, j - 1],
                    recv_sem=recv_sems.at[i, j - 1], device_id=pid, device_id_type=pl.DeviceIdType.MESH)
                cp.start()
                sends.append(cp)
        for j in range(1, NDEV):
            source = (me + NDEV - j) % NDEV
            sid = (source // 4, (source // 2) % 2, source % 2)
            for i in range(n):
                src = ins[i].at[me] if scatter[i] else ins[i]
                pltpu.make_async_remote_copy(
                    src_ref=src, dst_ref=outs[i].at[source], send_sem=send_sems.at[i, j - 1],
                    recv_sem=recv_sems.at[i, j - 1], device_id=sid, device_id_type=pl.DeviceIdType.MESH).wait_recv()
        for cp in sends:
            cp.wait_send()
        for cp in copies:
            cp.wait()

    hbm = pl.BlockSpec(memory_space=pltpu.HBM)
    out_shape = [jax.ShapeDtypeStruct(a.shape if s else (NDEV,) + a.shape, a.dtype) for a, s in zip(arrays, scatter)]
    return pl.pallas_call(
        body, in_specs=[hbm] * n, out_specs=[hbm] * n, out_shape=out_shape,
        scratch_shapes=[pltpu.SemaphoreType.DMA((n, NDEV - 1)), pltpu.SemaphoreType.DMA((n, NDEV - 1)),
                        pltpu.SemaphoreType.DMA((n,))],
        compiler_params=pltpu.CompilerParams(has_side_effects=True), name=name)(*arrays)


def _gather_two_level(arrays, name):
    n = len(arrays)

    def body(*refs):
        ins, outs = refs[:n], refs[n:2 * n]
        send_sems, recv_sems, local_sems = refs[2 * n:]
        x, y, c = lax.axis_index("x"), lax.axis_index("y"), lax.axis_index("c")
        me, sibling = (x, y, c), (x, y, 1 - c)
        chips = [(1 - x, y), (x, 1 - y), (1 - x, 1 - y)]
        slot = lambda d: 4 * d[0] + 2 * d[1] + d[2]

        def copy(i, k, block, to, src=None):
            return pltpu.make_async_remote_copy(
                src_ref=outs[i].at[slot(block)] if src is None else src, dst_ref=outs[i].at[slot(block)],
                send_sem=send_sems.at[i, k], recv_sem=recv_sems.at[i, k], device_id=to,
                device_id_type=pl.DeviceIdType.MESH)

        mine, first, passed = [], [], []
        for i in range(n):
            cp = pltpu.make_async_copy(ins[i], outs[i].at[slot(me)], local_sems.at[i])
            cp.start()
            mine.append(cp)
            first.append(copy(i, 0, me, sibling, src=ins[i]))
            first += [copy(i, 1 + j, me, (*chip, c), src=ins[i]) for j, chip in enumerate(chips)]
        for cp in first:
            cp.start()
        for i in range(n):
            for j, chip in enumerate(chips):
                copy(i, 1 + j, (*chip, c), me).wait_recv()
                cp = copy(i, 4 + j, (*chip, c), sibling)
                cp.start()
                passed.append(cp)
        for i in range(n):
            copy(i, 0, sibling, me).wait_recv()
            for j, chip in enumerate(chips):
                copy(i, 4 + j, (*chip, 1 - c), me).wait_recv()
        for cp in first + passed:
            cp.wait_send()
        for cp in mine:
            cp.wait()

    hbm = pl.BlockSpec(memory_space=pltpu.HBM)
    return pl.pallas_call(
        body, in_specs=[hbm] * n, out_specs=[hbm] * n,
        out_shape=[jax.ShapeDtypeStruct((NDEV,) + a.shape, a.dtype) for a in arrays],
        scratch_shapes=[pltpu.SemaphoreType.DMA((n, NDEV - 1)), pltpu.SemaphoreType.DMA((n, NDEV - 1)),
                        pltpu.SemaphoreType.DMA((n,))],
        compiler_params=pltpu.CompilerParams(has_side_effects=True), name=name)(*arrays)


def _scatter_pairs(arrays, name):
    n = len(arrays)

    def body(*refs):
        ins, gots = refs[:n], refs[n:2 * n]
        send_sems, recv_sems = refs[2 * n:]
        x, y, c = lax.axis_index("x"), lax.axis_index("y"), lax.axis_index("c")
        sends = []
        for i in range(n):
            for q in range(4):
                cp = pltpu.make_async_remote_copy(
                    src_ref=ins[i].at[2 * q + 1 - c], dst_ref=gots[i].at[q], send_sem=send_sems.at[i, q],
                    recv_sem=recv_sems.at[i, q], device_id=(x, y, 1 - c), device_id_type=pl.DeviceIdType.MESH)
                cp.start()
                sends.append(cp)
        for cp in sends:
            cp.wait_recv()
        for cp in sends:
            cp.wait_send()

    hbm = pl.BlockSpec(memory_space=pltpu.HBM)
    return pl.pallas_call(
        body, in_specs=[hbm] * n, out_specs=[hbm] * n,
        out_shape=[jax.ShapeDtypeStruct((4,) + a.shape[1:], a.dtype) for a in arrays],
        scratch_shapes=[pltpu.SemaphoreType.DMA((n, 4)), pltpu.SemaphoreType.DMA((n, 4))],
        compiler_params=pltpu.CompilerParams(has_side_effects=True), name=name)(*arrays)


def _pair_sum(own, got, name):
    def body(a_ref, b_ref, o_ref):
        o_ref[...] = (a_ref[...].astype(f32) + b_ref[...].astype(f32)).astype(bf16)

    spec = pl.BlockSpec((None, None) + own.shape[2:], lambda q, s: (q, s, 0, 0))
    return pl.pallas_call(
        body, grid=own.shape[:2], in_specs=[spec, spec], out_specs=spec,
        out_shape=jax.ShapeDtypeStruct(own.shape, bf16),
        compiler_params=_cparams(("arbitrary", "arbitrary")), name=name)(own, got)


def _scatter_chips(arrays, name):
    n = len(arrays)

    def body(*refs):
        ins, outs = refs[:n], refs[n:2 * n]
        send_sems, recv_sems, local_sems = refs[2 * n:]
        x, y, c = lax.axis_index("x"), lax.axis_index("y"), lax.axis_index("c")
        chip = 2 * x + y
        keeps, sends = [], []
        for i in range(n):
            cp = pltpu.make_async_copy(ins[i].at[chip], outs[i].at[chip], local_sems.at[i])
            cp.start()
            keeps.append(cp)
        for j in range(1, 4):
            peer = (chip + j) % 4
            for i in range(n):
                cp = pltpu.make_async_remote_copy(
                    src_ref=ins[i].at[peer], dst_ref=outs[i].at[chip], send_sem=send_sems.at[i, j - 1],
                    recv_sem=recv_sems.at[i, j - 1], device_id=(peer // 2, peer % 2, c),
                    device_id_type=pl.DeviceIdType.MESH)
                cp.start()
                sends.append(cp)
        for j in range(1, 4):
            source = (chip + 4 - j) % 4
            for i in range(n):
                pltpu.make_async_remote_copy(
                    src_ref=ins[i].at[chip], dst_ref=outs[i].at[source], send_sem=send_sems.at[i, j - 1],
                    recv_sem=recv_sems.at[i, j - 1], device_id=(source // 2, source % 2, c),
                    device_id_type=pl.DeviceIdType.MESH).wait_recv()
        for cp in sends:
            cp.wait_send()
        for cp in keeps:
            cp.wait()

    hbm = pl.BlockSpec(memory_space=pltpu.HBM)
    return pl.pallas_call(
        body, in_specs=[hbm] * n, out_specs=[hbm] * n,
        out_shape=[jax.ShapeDtypeStruct(a.shape, a.dtype) for a in arrays],
        scratch_shapes=[pltpu.SemaphoreType.DMA((n, 3)), pltpu.SemaphoreType.DMA((n, 3)),
                        pltpu.SemaphoreType.DMA((n,))],
        compiler_params=pltpu.CompilerParams(has_side_effects=True), name=name)(*arrays)


def _adam_math(w, g, m, v):
    m2 = ADAM_B1 * m + (1.0 - ADAM_B1) * g
    v2 = ADAM_B2 * v + (1.0 - ADAM_B2) * (g * g)
    m_hat = m2 / (1.0 - ADAM_B1 ** ADAM_STEP)
    v_hat = v2 / (1.0 - ADAM_B2 ** ADAM_STEP)
    return -ADAM_LR * (m_hat / (jnp.sqrt(v_hat) + ADAM_EPS) + ADAM_WD * w), m2, v2


def _adam_big(parts, w, m, v, name):
    L, rows, cols = w.shape

    def body(*refs):
        prefs = refs[:len(parts)]
        w_ref, m_ref, v_ref, g_ref, d_ref, m2_ref, v2_ref = refs[len(parts):]
        c0 = 0
        for pref in prefs:
            acc = pref[0].astype(f32)
            for s in range(1, pref.shape[0]):
                acc = acc + pref[s].astype(f32)
            width = min(acc.shape[1], cols - c0)
            g_ref[:, c0:c0 + width] = acc[0:rows, 0:width]
            c0 += width
        d, m2, v2 = _adam_math(w_ref[...], g_ref[...], m_ref[...], v_ref[...])
        d_ref[...] = d
        m2_ref[...] = m2
        v2_ref[...] = v2

    wspec = pl.BlockSpec((None, rows, cols), lambda l: (l, 0, 0))
    in_specs = [pl.BlockSpec((a.shape[0], None) + a.shape[2:],
                             functools.partial(lambda l, per, first: (0, per * l + first, 0, 0), per=per, first=first))
                for a, per, first in parts]
    return pl.pallas_call(
        body, grid=(L,), in_specs=in_specs + [wspec] * 3, out_specs=[wspec] * 4,
        out_shape=[jax.ShapeDtypeStruct(w.shape, f32)] * 4,
        compiler_params=_cparams(("arbitrary",)), name=name)(*[a for a, _, _ in parts], w, m, v)


def _sum_sources(stacked):
    rows = stacked.shape[1]

    def body(s_ref, o_ref):
        acc = s_ref[0]
        for s in range(1, NDEV):
            acc = acc + s_ref[s]
        o_ref[...] = acc

    return pl.pallas_call(body, out_shape=jax.ShapeDtypeStruct((rows, 128), f32), name="sum_small_grads")(stacked)


def _adam_small(w, g, m, v):
    def body(w_ref, g_ref, m_ref, v_ref, d_ref, m2_ref, v2_ref):
        d, m2, v2 = _adam_math(w_ref[...], g_ref[...], m_ref[...], v_ref[...])
        d_ref[...] = d
        m2_ref[...] = m2
        v2_ref[...] = v2

    return pl.pallas_call(body, out_shape=[jax.ShapeDtypeStruct(w.shape, f32)] * 3, name="adam_small")(w, g, m, v)


def _pack_rows(arrs):
    flat = []
    for a in arrs:
        a = a.reshape(-1)
        flat.append(jnp.pad(a, (0, (-a.shape[0]) % 1024)))
    return jnp.concatenate(flat).reshape(-1, 128)


def _unpack_rows(packed, shapes):
    out, off = [], 0
    flat = packed.reshape(-1)
    for s in shapes:
        n = math.prod(s)
        out.append(flat[off:off + n].reshape(s))
        off += n + (-n) % 1024
    return out


def _gather_conv(gathered, shape):
    L, K, c = shape
    return jnp.transpose(gathered, (1, 2, 0, 3)).reshape(L, K, NDEV * c)


def kernel(x, p, positions, ln_ffn1_g, ln_ffn1_b, ffn1_w_gate, ffn1_w_up, ffn1_w_down, w_in, ret_norm_g, lru_conv_w, lru_conv_b, lru_w_a, lru_b_a, lru_w_x, lru_b_x, lru_lambda, gdn_conv_w, gdn_a_log, gdn_dt_bias, gdn_norm_g, w_out, ln_mix_g, ln_mix_b, ffn2_w_gate, ffn2_w_up, ffn2_w_down, ple_w_gate, ple_w_proj, ln_ffn2_g, ln_ffn2_b, loss_target, m_ln_ffn1_g, m_ln_ffn1_b, m_ffn1_w_gate, m_ffn1_w_up, m_ffn1_w_down, m_w_in, m_ret_norm_g, m_lru_conv_w, m_lru_conv_b, m_lru_w_a, m_lru_b_a, m_lru_w_x, m_lru_b_x, m_lru_lambda, m_gdn_conv_w, m_gdn_a_log, m_gdn_dt_bias, m_gdn_norm_g, m_w_out, m_ln_mix_g, m_ln_mix_b, m_ffn2_w_gate, m_ffn2_w_up, m_ffn2_w_down, m_ple_w_gate, m_ple_w_proj, m_ln_ffn2_g, m_ln_ffn2_b, v_ln_ffn1_g, v_ln_ffn1_b, v_ffn1_w_gate, v_ffn1_w_up, v_ffn1_w_down, v_w_in, v_ret_norm_g, v_lru_conv_w, v_lru_conv_b, v_lru_w_a, v_lru_b_a, v_lru_w_x, v_lru_b_x, v_lru_lambda, v_gdn_conv_w, v_gdn_a_log, v_gdn_dt_bias, v_gdn_norm_g, v_w_out, v_ln_mix_g, v_ln_mix_b, v_ffn2_w_gate, v_ffn2_w_up, v_ffn2_w_down, v_ple_w_gate, v_ple_w_proj, v_ln_ffn2_g, v_ln_ffn2_b):
    args = locals()
    W = {n: args[n] for n in WEIGHTS}
    M = {n: args['m_' + n] for n in WEIGHTS}
    V = {n: args['v_' + n] for n in WEIGHTS}
    me = 4 * lax.axis_index("x") + 2 * lax.axis_index("y") + lax.axis_index("c")

    packed = _pack_big(W)
    conv_pack = _pack_rows([W[n] for n in CONV_SHARDED])
    keys = list(packed)
    gathered = _gather_two_level([packed[k] for k in keys] + [conv_pack], "gather_weights")
    G = dict(zip(keys, gathered[:-1]))
    G['wpp'] = jnp.transpose(G.pop('ppp'), (1, 2, 0, 3)).reshape(DEPTH, PLE, D)
    conv_all = gathered[-1]
    sm = {n: W[n] for n in SMALL}
    conv_shards = [_unpack_rows(conv_all[s], [W[n].shape for n in CONV_SHARDED]) for s in range(NDEV)]
    for i, n in enumerate(CONV_SHARDED):
        sm[n] = _gather_conv(jnp.stack([cs[i] for cs in conv_shards]), W[n].shape)

    loss, grad_x, big, small = _local_step(x[0], p[:, 0], positions.reshape(-1, 1), loss_target[0], G, sm)
    loss = lax.psum(loss[0, 0], ("x", "y", "c"))

    small_pack = _pack_rows([small[n] for n in SMALL])
    bkeys = list(big)
    gots = _scatter_pairs([big[k] for k in bkeys], "scatter_pairs")
    core = lax.axis_index("c")
    owns = [lax.dynamic_index_in_dim(big[k].reshape((4, 2) + big[k].shape[1:]), core, axis=1, keepdims=False)
            for k in bkeys]
    pair = [_pair_sum(o, g, "pair_sum_" + k) for k, o, g in zip(bkeys, owns, gots)]
    R = dict(zip(bkeys, _scatter_chips(pair, "scatter_chips")))
    small_all = _exchange([small_pack], [False], "gather_small_grads")[0]
    small_sum = _unpack_rows(_sum_sources(small_all), [small[n].shape for n in SMALL])
    grads, delta, new_m, new_v = {}, {}, {}, {}
    for n, g in zip(SMALL, small_sum):
        if n in CONV_SHARDED:
            c = W[n].shape[2]
            g = lax.dynamic_slice_in_dim(g, me * c, c, axis=2)
        grads[n] = g

    big_parts = {
        'ffn1_w_gate': [(R['p384'], 4, 0)], 'ffn1_w_up': [(R['p384'], 4, 1)],
        'ffn2_w_gate': [(R['p384'], 4, 2)], 'ffn2_w_up': [(R['p384'], 4, 3)],
        'ffn1_w_down': [(R['pd'], 2, 0)], 'ffn2_w_down': [(R['pd'], 2, 1)],
        'w_in': [(R['pr'], 3, 0), (R['pinl'], 1, 0), (R['ping'], 1, 0)],
        'w_out': [(R['pr'], 3, 1)], 'ple_w_gate': [(R['pr'], 3, 2)], 'ple_w_proj': [(R['ppp'], 1, 0)],
    }
    for n in BIG:
        grads[n], delta[n], new_m[n], new_v[n] = _adam_big(big_parts[n], W[n], M[n], V[n], "adam_" + n)
    shapes = [W[n].shape for n in SMALL]
    d_s, m_s, v_s = _adam_small(*[_pack_rows([src[n] for n in SMALL]) for src in (W, grads, M, V)])
    for n, dd, mm, vv in zip(SMALL, _unpack_rows(d_s, shapes), _unpack_rows(m_s, shapes), _unpack_rows(v_s, shapes)):
        delta[n], new_m[n], new_v[n] = dd, mm, vv

    return (loss, grad_x[None], *[grads[n] for n in WEIGHTS], *[delta[n] for n in WEIGHTS],
            *[new_m[n] for n in WEIGHTS], *[new_v[n] for n in WEIGHTS])
```

```python
import functools
import math

import jax
import jax.numpy as jnp
from jax import lax
from jax.experimental import pallas as pl
from jax.experimental.pallas import tpu as pltpu

f32 = jnp.float32
bf16 = jnp.bfloat16

NDEV = 8
DEPTH = 2
D = 1024
FS = 352
FSP = 384
FB = 2
NF = NDEV // FB
PLE = 256
CH = 64
RET_H, GDN_H = 4, 6
RET_W, LRU_W, GDN_W = 256, 384, 384
GDN_IN = 1664
GDN_IN_REAL = 1548
D_IN = 3340
ALPHA = 4.0 ** 0.25
LN_EPS = 1e-5
ROPE_THETA = 10000.0
TM = 512
RB_RET, RB_LRU, RB_GDN = 512, 512, 256
VMEM_LIMIT = 56 * 1024 * 1024
ADAM_LR, ADAM_B1, ADAM_B2, ADAM_EPS, ADAM_WD, ADAM_STEP = 0.001, 0.9, 0.999, 1e-08, 0.01, 10

WEIGHTS = ['ln_ffn1_g', 'ln_ffn1_b', 'ffn1_w_gate', 'ffn1_w_up', 'ffn1_w_down', 'w_in', 'ret_norm_g', 'lru_conv_w',
           'lru_conv_b', 'lru_w_a', 'lru_b_a', 'lru_w_x', 'lru_b_x', 'lru_lambda', 'gdn_conv_w', 'gdn_a_log',
           'gdn_dt_bias', 'gdn_norm_g', 'w_out', 'ln_mix_g', 'ln_mix_b', 'ffn2_w_gate', 'ffn2_w_up', 'ffn2_w_down',
           'ple_w_gate', 'ple_w_proj', 'ln_ffn2_g', 'ln_ffn2_b']
BIG = ['ffn1_w_gate', 'ffn1_w_up', 'ffn1_w_down', 'w_in', 'w_out', 'ffn2_w_gate', 'ffn2_w_up', 'ffn2_w_down',
       'ple_w_gate', 'ple_w_proj']
SMALL = [n for n in WEIGHTS if n not in BIG]
PACKS = ('p384', 'pd', 'pr', 'pinl', 'ping', 'ppp')
CONV_SHARDED = {'lru_conv_w': LRU_W, 'gdn_conv_w': 3 * GDN_W}


def _cparams(sem=None):
    return pltpu.CompilerParams(dimension_semantics=sem, vmem_limit_bytes=VMEM_LIMIT)


def _sigmoid(x):
    return 1.0 / (1.0 + jnp.exp(-x))


def _silu(x):
    return x * _sigmoid(x)


def _dsilu(x):
    s = _sigmoid(x)
    return s * (1.0 + x * (1.0 - s))


def _softplus(x):
    return jnp.maximum(x, 0.0) + jnp.log(1.0 + jnp.exp(-jnp.abs(x)))


def _gelu(x):
    return 0.5 * x * (1.0 + jnp.tanh(0.7978845608028654 * (x + 0.044715 * x * x * x)))


def _dot(a, b):
    return jnp.dot(a.astype(bf16), b.astype(bf16), preferred_element_type=f32)


def _dot_nt(a, b):
    return lax.dot_general(a.astype(bf16), b.astype(bf16), (((1,), (1,)), ((), ())), preferred_element_type=f32)


def _dot_tn(a, b):
    return lax.dot_general(a.astype(bf16), b.astype(bf16), (((0,), (0,)), ((), ())), preferred_element_type=f32)


def _bmm(eq, a, b):
    return jnp.einsum(eq, a.astype(bf16), b.astype(bf16), preferred_element_type=f32)


def _split3(a):
    a1 = a.astype(bf16)
    r = a - a1.astype(f32)
    a2 = r.astype(bf16)
    return a1, a2, (r - a2.astype(f32)).astype(bf16)


def _bmm3(eq, a, b):
    a1, a2, _ = _split3(a)
    b1, b2, _ = _split3(b)
    e = lambda x, y: jnp.einsum(eq, x, y, preferred_element_type=f32)
    return e(a1, b1) + (e(a1, b2) + e(a2, b1))


def _tri_ones(B, upper=False):
    ii = lax.broadcasted_iota(jnp.int32, (B, CH, CH), 1)
    jj = lax.broadcasted_iota(jnp.int32, (B, CH, CH), 2)
    return jnp.where((ii <= jj) if upper else (ii >= jj), 1.0, 0.0).astype(bf16)


def _cumsum_mm(t, x):
    x1, x2, x3 = _split3(x)
    e = lambda y: jnp.einsum('bij,bjk->bik', t, y, preferred_element_type=f32)
    return e(x1) + (e(x2) + e(x3))


def _chunk_cumsum(x, reverse=False):
    n = x.shape[0] // CH
    return _cumsum_mm(_tri_ones(n, upper=reverse), x.reshape(n, CH, 128)).reshape(x.shape)


@jax.custom_vjp
def _neumann_inv(m):
    ii = lax.broadcasted_iota(jnp.int32, m.shape, 1)
    jj = lax.broadcasted_iota(jnp.int32, m.shape, 2)
    inv = jnp.where(ii == jj, 1.0, 0.0).astype(f32) + m
    mp = m
    for _ in range(5):
        mp = _bmm3('bij,bjk->bik', mp, mp)
        inv = inv + _bmm3('bij,bjk->bik', inv, mp)
    return inv


def _neumann_inv_fwd(m):
    inv = _neumann_inv(m)
    return inv, inv


def _neumann_inv_bwd(inv, g):
    return (_bmm3('bij,bkj->bik', _bmm3('bji,bjk->bik', inv, g), inv),)


_neumann_inv.defvjp(_neumann_inv_fwd, _neumann_inv_bwd)


def _ln_stats(z):
    mu = jnp.mean(z, -1, keepdims=True)
    zc = z - mu
    rstd = lax.rsqrt(jnp.mean(zc * zc, -1, keepdims=True) + LN_EPS)
    return zc * rstd, rstd


def _ln_bwd(z, g, dout):
    xh, rstd = _ln_stats(z)
    dxh = dout * g
    dz = rstd * (dxh - jnp.mean(dxh, -1, keepdims=True) - xh * jnp.mean(dxh * xh, -1, keepdims=True))
    return dz, jnp.sum(dout * xh, 0, keepdims=True), jnp.sum(dout, 0, keepdims=True)


def _full_spec(shape):
    nd = len(shape)
    return pl.BlockSpec(shape, lambda *_: (0,) * nd)


def _ffn_fwd(x, p384, pd, lg, lb, layer, which, ple=None):
    T = x.shape[0]
    sg, su, sd = 4 * layer + 2 * which, 4 * layer + 2 * which + 1, 2 * layer + which
    has_ple = ple is not None

    def body(*refs):
        if has_ple:
            (x_ref, wg_ref, wu_ref, wd_ref, lg_ref, lb_ref, p_ref, wpg_ref, wpp_ref,
             z_ref, o_ref, g_ref, u_ref, acc, xb_s) = refs
        else:
            x_ref, wg_ref, wu_ref, wd_ref, lg_ref, lb_ref, z_ref, o_ref, g_ref, u_ref, acc, xb_s = refs
        f = pl.program_id(1)

        @pl.when(f == 0)
        def _():
            x = x_ref[...]
            xb = x.astype(bf16)
            xb_s[...] = xb
            base = ALPHA * x
            if has_ple:
                gate = _sigmoid(_dot(xb, wpg_ref[...].reshape(D, D)))
                base = base + gate * _dot(p_ref[...], wpp_ref[...])
            acc[...] = base

        xb = xb_s[...]
        g = _dot(xb, wg_ref[...])
        u = _dot(xb, wu_ref[...])
        g_ref[...] = g.astype(bf16)
        u_ref[...] = u.astype(bf16)
        acc[...] += 0.5 * _dot(_silu(g) * u, wd_ref[...].reshape(FB * FSP, D))

        @pl.when(f == NF - 1)
        def _():
            z = acc[...]
            z_ref[...] = z
            o_ref[...] = _ln_stats(z)[0] * lg_ref[...] + lb_ref[...]

    row = pl.BlockSpec((TM, D), lambda i, f: (i, 0))
    in_specs = [row,
                pl.BlockSpec((None, D, FB * FSP), lambda i, f: (sg, 0, f)),
                pl.BlockSpec((None, D, FB * FSP), lambda i, f: (su, 0, f)),
                pl.BlockSpec((FB, None, FSP, D), lambda i, f: (f, sd, 0, 0)),
                _full_spec((1, D)), _full_spec((1, D))]
    args = [x, p384, p384, pd, lg, lb]
    if has_ple:
        p, pr, wpp = ple
        in_specs += [pl.BlockSpec((TM, PLE), lambda i, f: (i, 0)),
                     pl.BlockSpec((NDEV, None, 128, D), lambda i, f: (0, 3 * layer + 2, 0, 0)),
                     _full_spec((PLE, D))]
        args += [p, pr, wpp]
    hid = pl.BlockSpec((TM, FB * FSP), lambda i, f: (i, f))
    hshape = jax.ShapeDtypeStruct((T, NDEV * FSP), bf16)
    return pl.pallas_call(
        body, grid=(T // TM, NF), in_specs=in_specs, out_specs=[row, row, hid, hid],
        out_shape=[jax.ShapeDtypeStruct((T, D), f32)] * 2 + [hshape, hshape],
        scratch_shapes=[pltpu.VMEM((TM, D), f32), pltpu.VMEM((TM, D), bf16)],
        compiler_params=_cparams(("arbitrary", "arbitrary")), name=f"ffn{which + 1}_fwd")(*args)


def _ffn_bwd(z, dout, gs, us, p384, pd, lg, layer, which):
    T = z.shape[0]
    TMB = TM
    sg, su, sd = 4 * layer + 2 * which, 4 * layer + 2 * which + 1, 2 * layer + which

    def body(z_ref, do_ref, g_ref, u_ref, wg_ref, wu_ref, wd_ref, lg_ref,
             dx_ref, dg_ref, du_ref, a_ref, dy_ref, dlg_ref, dlb_ref, acc, dyb):
        i, f = pl.program_id(0), pl.program_id(1)

        @pl.when(jnp.logical_and(i == 0, f == 0))
        def _():
            dlg_ref[...] = jnp.zeros_like(dlg_ref)
            dlb_ref[...] = jnp.zeros_like(dlb_ref)

        @pl.when(f == 0)
        def _():
            dz, dlg, dlb = _ln_bwd(z_ref[...], lg_ref[...], do_ref[...])
            dlg_ref[...] += dlg
            dlb_ref[...] += dlb
            dy = (0.5 * dz).astype(bf16)
            dyb[...] = dy
            dy_ref[...] = dy
            acc[...] = ALPHA * dz

        g = g_ref[...].astype(f32)
        u = u_ref[...].astype(f32)
        da = _dot_nt(dyb[...], wd_ref[...].reshape(FB * FSP, D))
        sgm = _sigmoid(g)
        dg = (da * u * (sgm * (1.0 + g * (1.0 - sgm)))).astype(bf16)
        du = (da * (g * sgm)).astype(bf16)
        dg_ref[...] = dg
        du_ref[...] = du
        a_ref[...] = (g * sgm * u).astype(bf16)
        acc[...] += _dot_nt(dg, wg_ref[...]) + _dot_nt(du, wu_ref[...])

        @pl.when(f == NF - 1)
        def _():
            dx_ref[...] = acc[...]

    row = pl.BlockSpec((TMB, D), lambda i, f: (i, 0))
    hid = pl.BlockSpec((TMB, FB * FSP), lambda i, f: (i, f))
    vec = _full_spec((1, D))
    in_specs = [row, row, hid, hid,
                pl.BlockSpec((None, D, FB * FSP), lambda i, f: (sg, 0, f)),
                pl.BlockSpec((None, D, FB * FSP), lambda i, f: (su, 0, f)),
                pl.BlockSpec((FB, None, FSP, D), lambda i, f: (f, sd, 0, 0)),
                vec]
    args = [z, dout, gs, us, p384, p384, pd, lg]
    out_specs = [row, hid, hid, hid, row, vec, vec]
    hshape = jax.ShapeDtypeStruct((T, NDEV * FSP), bf16)
    out_shape = [jax.ShapeDtypeStruct((T, D), f32), hshape, hshape, hshape, jax.ShapeDtypeStruct((T, D), bf16),
                 jax.ShapeDtypeStruct((1, D), f32), jax.ShapeDtypeStruct((1, D), f32)]
    return pl.pallas_call(
        body, grid=(T // TMB, NF), in_specs=in_specs, out_specs=out_specs, out_shape=out_shape,
        scratch_shapes=[pltpu.VMEM((TMB, D), f32), pltpu.VMEM((TMB, D), bf16)],
        compiler_params=_cparams(("arbitrary", "arbitrary")), name=f"ffn{which + 1}_bwd")(*args)


def _ple_bwd(x, p, dy, dx_ffn, pr, wpp, layer):
    T = x.shape[0]

    def body(x_ref, p_ref, dy_ref, dxf_ref, wpg_ref, wpp_ref, dx_ref, dgp_ref, dpj_ref):
        dz = 2.0 * dy_ref[...].astype(f32)
        wpg = wpg_ref[...].reshape(D, D)
        gate = _sigmoid(_dot(x_ref[...], wpg))
        proj = _dot(p_ref[...], wpp_ref[...])
        dgp = (dz * proj * gate * (1.0 - gate)).astype(bf16)
        dgp_ref[...] = dgp
        dpj_ref[...] = (dz * gate).astype(bf16)
        dx_ref[...] = dxf_ref[...] + _dot_nt(dgp, wpg)

    row = pl.BlockSpec((TM, D), lambda i: (i, 0))
    return pl.pallas_call(
        body, grid=(T // TM,),
        in_specs=[row, pl.BlockSpec((TM, PLE), lambda i: (i, 0)), row, row,
                  pl.BlockSpec((NDEV, None, 128, D), lambda i: (0, 3 * layer + 2, 0, 0)), _full_spec((PLE, D))],
        out_specs=[row, row, row],
        out_shape=[jax.ShapeDtypeStruct((T, D), f32), jax.ShapeDtypeStruct((T, D), bf16),
                   jax.ShapeDtypeStruct((T, D), bf16)],
        compiler_params=_cparams(("arbitrary",)), name="ple_bwd")(x, p, dy, dx_ffn, pr, wpp)


def _matmul_tn(a, b, nb, name, nsub=1):
    T, M = a.shape
    N = b.shape[1]
    tk = min(T, 1024)
    nk = T // tk
    wide = nsub * nb

    def body(a_ref, b_ref, o_ref, acc):
        k = pl.program_id(1)

        @pl.when(k == 0)
        def _():
            acc[...] = jnp.zeros_like(acc)

        acc[...] += _dot_tn(a_ref[...], b_ref[...])

        @pl.when(k == nk - 1)
        def _():
            for j in range(nsub):
                o_ref[j] = acc[:, j * nb:(j + 1) * nb].astype(bf16)

    return pl.pallas_call(
        body, grid=(N // wide, nk),
        in_specs=[pl.BlockSpec((tk, M), lambda n, k: (k, 0)), pl.BlockSpec((tk, wide), lambda n, k: (k, n))],
        out_specs=pl.BlockSpec((nsub, M, nb), lambda n, k: (n, 0, 0)),
        out_shape=jax.ShapeDtypeStruct((N // nb, M, nb), bf16),
        scratch_shapes=[pltpu.VMEM((M, wide), f32)],
        compiler_params=_cparams(("arbitrary", "arbitrary")), name=name)(a, b)


def _proj_in(x, pr, pinl, ping, layer):
    T = x.shape[0]

    def body(x_ref, wr_ref, wl_ref, wg_ref, hr_ref, hl_ref, hg_ref):
        xb = x_ref[...].astype(bf16)
        hr_ref[...] = _dot(xb, wr_ref[...].reshape(D, D))
        hl_ref[...] = _dot(xb, wl_ref[...].reshape(D, 2 * LRU_W))
        hg_ref[...] = _dot(xb, wg_ref[...].reshape(D, GDN_IN))

    return pl.pallas_call(
        body, grid=(T // TM,),
        in_specs=[pl.BlockSpec((TM, D), lambda i: (i, 0)),
                  pl.BlockSpec((NDEV, None, 128, D), lambda i: (0, 3 * layer, 0, 0)),
                  pl.BlockSpec((NDEV, None, 128, 2 * LRU_W), lambda i: (0, layer, 0, 0)),
                  pl.BlockSpec((NDEV, None, 128, GDN_IN), lambda i: (0, layer, 0, 0))],
        out_specs=[pl.BlockSpec((TM, D), lambda i: (i, 0)), pl.BlockSpec((TM, 2 * LRU_W), lambda i: (i, 0)),
                   pl.BlockSpec((TM, GDN_IN), lambda i: (i, 0))],
        out_shape=[jax.ShapeDtypeStruct((T, D), f32), jax.ShapeDtypeStruct((T, 2 * LRU_W), f32),
                   jax.ShapeDtypeStruct((T, GDN_IN), f32)],
        compiler_params=_cparams(("arbitrary",)), name="proj_in")(x, pr, pinl, ping)


def _proj_in_bwd(base, dhr, dhl, dhg, pr, pinl, ping, layer):
    T = base.shape[0]

    def body(b_ref, dr_ref, dl_ref, dg_ref, wr_ref, wl_ref, wg_ref, o_ref):
        o_ref[...] = (b_ref[...] + _dot_nt(dr_ref[...], wr_ref[...].reshape(D, D))
                      + _dot_nt(dl_ref[...], wl_ref[...].reshape(D, 2 * LRU_W))
                      + _dot_nt(dg_ref[...], wg_ref[...].reshape(D, GDN_IN)))

    return pl.pallas_call(
        body, grid=(T // TM,),
        in_specs=[pl.BlockSpec((TM, D), lambda i: (i, 0)), pl.BlockSpec((TM, D), lambda i: (i, 0)),
                  pl.BlockSpec((TM, 2 * LRU_W), lambda i: (i, 0)), pl.BlockSpec((TM, GDN_IN), lambda i: (i, 0)),
                  pl.BlockSpec((NDEV, None, 128, D), lambda i: (0, 3 * layer, 0, 0)),
                  pl.BlockSpec((NDEV, None, 128, 2 * LRU_W), lambda i: (0, layer, 0, 0)),
                  pl.BlockSpec((NDEV, None, 128, GDN_IN), lambda i: (0, layer, 0, 0))],
        out_specs=pl.BlockSpec((TM, D), lambda i: (i, 0)),
        out_shape=jax.ShapeDtypeStruct((T, D), f32),
        compiler_params=_cparams(("arbitrary",)), name="proj_in_bwd")(base, dhr, dhl, dhg, pr, pinl, ping)


def _mix_out(x1, o_r, o_l, o_g, pr, lg, lb, layer):
    T = x1.shape[0]

    def body(x_ref, r_ref, l_ref, g_ref, w_ref, lg_ref, lb_ref, z_ref, o_ref):
        w = w_ref[...].reshape(D, D)
        z = (ALPHA * x_ref[...] + _dot(r_ref[...], w[0:RET_W]) + _dot(l_ref[...], w[RET_W:RET_W + LRU_W])
             + _dot(g_ref[...], w[RET_W + LRU_W:D]))
        z_ref[...] = z
        o_ref[...] = _ln_stats(z)[0] * lg_ref[...] + lb_ref[...]

    row = pl.BlockSpec((TM, D), lambda i: (i, 0))
    return pl.pallas_call(
        body, grid=(T // TM,),
        in_specs=[row, pl.BlockSpec((TM, RET_W), lambda i: (i, 0)), pl.BlockSpec((TM, LRU_W), lambda i: (i, 0)),
                  pl.BlockSpec((TM, GDN_W), lambda i: (i, 0)),
                  pl.BlockSpec((NDEV, None, 128, D), lambda i: (0, 3 * layer + 1, 0, 0)),
                  _full_spec((1, D)), _full_spec((1, D))],
        out_specs=[row, row], out_shape=[jax.ShapeDtypeStruct((T, D), f32)] * 2,
        compiler_params=_cparams(("arbitrary",)), name="mix_out")(x1, o_r, o_l, o_g, pr, lg, lb)


def _mix_out_bwd(z, dout, pr, lg, layer):
    T = z.shape[0]

    def body(z_ref, do_ref, w_ref, lg_ref, dxb_ref, dzb_ref, dr_ref, dl_ref, dg_ref, dlg_ref, dlb_ref):
        @pl.when(pl.program_id(0) == 0)
        def _():
            dlg_ref[...] = jnp.zeros_like(dlg_ref)
            dlb_ref[...] = jnp.zeros_like(dlb_ref)

        dz, dlg, dlb = _ln_bwd(z_ref[...], lg_ref[...], do_ref[...])
        dlg_ref[...] += dlg
        dlb_ref[...] += dlb
        dxb_ref[...] = ALPHA * dz
        dzb = dz.astype(bf16)
        dzb_ref[...] = dzb
        w = w_ref[...].reshape(D, D)
        dr_ref[...] = _dot_nt(dzb, w[0:RET_W])
        dl_ref[...] = _dot_nt(dzb, w[RET_W:RET_W + LRU_W])
        dg_ref[...] = _dot_nt(dzb, w[RET_W + LRU_W:D])

    row = pl.BlockSpec((TM, D), lambda i: (i, 0))
    vec = _full_spec((1, D))
    return pl.pallas_call(
        body, grid=(T // TM,),
        in_specs=[row, row, pl.BlockSpec((NDEV, None, 128, D), lambda i: (0, 3 * layer + 1, 0, 0)), vec],
        out_specs=[row, row, pl.BlockSpec((TM, RET_W), lambda i: (i, 0)), pl.BlockSpec((TM, LRU_W), lambda i: (i, 0)),
                   pl.BlockSpec((TM, GDN_W), lambda i: (i, 0)), vec, vec],
        out_shape=[jax.ShapeDtypeStruct((T, D), f32), jax.ShapeDtypeStruct((T, D), bf16),
                   jax.ShapeDtypeStruct((T, RET_W), f32), jax.ShapeDtypeStruct((T, LRU_W), f32),
                   jax.ShapeDtypeStruct((T, GDN_W), f32), jax.ShapeDtypeStruct((1, D), f32),
                   jax.ShapeDtypeStruct((1, D), f32)],
        compiler_params=_cparams(("arbitrary",)), name="mix_out_bwd")(z, dout, pr, lg)


def _loss_grad(y, target):
    T = y.shape[0]

    def body(y_ref, t_ref, dy_ref, l_ref):
        @pl.when(pl.program_id(0) == 0)
        def _():
            l_ref[...] = jnp.zeros_like(l_ref)

        e = y_ref[...] - t_ref[...]
        dy_ref[...] = e * (1.0 / D)
        l_ref[...] += 0.5 * jnp.sum(jnp.sum(e * e, -1, keepdims=True) * (1.0 / D), 0, keepdims=True)

    row = pl.BlockSpec((TM, D), lambda i: (i, 0))
    return pl.pallas_call(
        body, grid=(T // TM,), in_specs=[row, row], out_specs=[row, _full_spec((1, 1))],
        out_shape=[jax.ShapeDtypeStruct((T, D), f32), jax.ShapeDtypeStruct((1, 1), f32)],
        compiler_params=_cparams(("arbitrary",)), name="loss_grad")(y, target)


def _split_heads(x, H):
    n = x.shape[0] // CH
    parts = [x[:, h * CH:(h + 1) * CH].reshape(n, CH, CH) for h in range(H)]
    return jnp.stack(parts, axis=1).reshape(n * H, CH, CH)


def _merge_heads(ref, x, H, col0=0):
    n = x.shape[0] // H
    x4 = x.reshape(n, H, CH, CH)
    for h in range(H):
        ref[:, col0 + h * CH:col0 + (h + 1) * CH] = x4[:, h].reshape(n * CH, CH)


def _conv_fwd(ext, x, tail, w, R):
    ext[0:8, :] = tail
    ext[8:R + 8, :] = x
    y = w[3:4, :] * x
    for k in range(3):
        y = y + w[k:k + 1, :] * ext[5 + k:5 + k + R, :]
    return y


def _conv_bwd(ext, ext2, dy, dy_next, w, R):
    ext2[0:R, :] = dy
    ext2[R:R + 8, :] = dy_next
    dx = w[3:4, :] * dy
    dws = []
    for k in range(3):
        dx = dx + w[k:k + 1, :] * ext2[3 - k:3 - k + R, :]
        dws.append(jnp.sum(dy * ext[5 + k:5 + k + R, :], 0, keepdims=True))
    dws.append(jnp.sum(dy * ext[8:8 + R, :], 0, keepdims=True))
    return dx, jnp.concatenate(dws, axis=0)


def _prev_tail_spec(R, W):
    return pl.BlockSpec((8, W), lambda i: (jnp.maximum(i * (R // 8) - 1, 0), 0))


def _prev_tail_spec_rev(R, W, nb):
    return pl.BlockSpec((8, W), lambda i: (jnp.maximum((nb - 1 - i) * (R // 8) - 1, 0), 0))


def _rope_tables(positions):
    T = positions.shape[0]

    def body(p_ref, c_ref, s_ref):
        lane = lax.broadcasted_iota(jnp.int32, (TM, RET_W), 1)
        fi = (lane % 32).astype(f32)
        inv = jnp.exp(fi * (-math.log(ROPE_THETA) / 32.0))
        ang = p_ref[...].astype(f32) * inv
        c_ref[...] = jnp.cos(ang)
        s_ref[...] = jnp.where(lane % CH < 32, -jnp.sin(ang), jnp.sin(ang))

    row = pl.BlockSpec((TM, RET_W), lambda i: (i, 0))
    return pl.pallas_call(
        body, grid=(T // TM,), in_specs=[pl.BlockSpec((TM, 1), lambda i: (i, 0))], out_specs=[row, row],
        out_shape=[jax.ShapeDtypeStruct((T, RET_W), f32)] * 2,
        compiler_params=_cparams(("arbitrary",)), name="rope_tables")(positions)


def _partner(x):
    lane = lax.broadcasted_iota(jnp.int32, x.shape, 1)
    return jnp.where(lane % CH < 32, pltpu.roll(x, RET_W - 32, 1), pltpu.roll(x, 32, 1))


def _ret_consts():
    ii = lax.broadcasted_iota(jnp.int32, (CH, CH), 0).astype(f32)
    jj = lax.broadcasted_iota(jnp.int32, (CH, CH), 1).astype(f32)
    intra, cross, tail, cd = [], [], [], []
    for h in range(RET_H):
        lg = math.log1p(-(2.0 ** (-5.0 - h)))
        intra.append(jnp.exp(jnp.abs(ii - jj) * lg))
        cross.append(jnp.exp((ii + 1.0) * lg))
        tail.append(jnp.exp((CH - 1.0 - ii) * lg))
        cd.append(jnp.full((CH, CH), math.exp(CH * lg), f32))
    return jnp.stack(intra), jnp.stack(cross), jnp.stack(tail), jnp.stack(cd)


def _ret_chunk(consts, q, k, v, st):
    intra, cross, tail, cd = consts
    s = _bmm('hid,hjd->hij', q, k) * intra
    o = _bmm('hij,hje->hie', s, v) + _bmm('hid,hde->hie', q * cross, st)
    st2 = st * cd + _bmm('hjd,hje->hde', k * tail, v)
    mu = jnp.mean(o, -1, keepdims=True)
    oc = o - mu
    on = oc * lax.rsqrt(jnp.mean(oc * oc, -1, keepdims=True) + 1e-5)
    return on, st2


def _ret_fwd(hr, cosw, sinw, gam):
    T = hr.shape[0]
    R = RB_RET
    nc = R // CH

    def body(h_ref, c_ref, s_ref, g_ref, o_ref, st_ref, st, wide):
        @pl.when(pl.program_id(0) == 0)
        def _():
            st[...] = jnp.zeros_like(st)

        consts = _ret_consts()
        cw, sw = c_ref[...], s_ref[...]
        q, k = h_ref[:, 0:RET_W], h_ref[:, RET_W:2 * RET_W]
        qh = _split_heads((q * cw + _partner(q) * sw) * 0.125, RET_H)
        kh = _split_heads(k * cw + _partner(k) * sw, RET_H)
        vh = _split_heads(h_ref[:, 2 * RET_W:3 * RET_W], RET_H)
        outs = []
        s_cur = st[...]
        for c in range(nc):
            sl = slice(c * RET_H, (c + 1) * RET_H)
            st_ref[c] = s_cur
            on, s_cur = _ret_chunk(consts, qh[sl], kh[sl], vh[sl], s_cur)
            outs.append(on)
        st[...] = s_cur
        _merge_heads(wide, jnp.concatenate(outs, axis=0), RET_H)
        o_ref[...] = wide[...] * g_ref[...] * _silu(h_ref[:, 3 * RET_W:4 * RET_W])

    blk = pl.BlockSpec((R, RET_W), lambda i: (i, 0))
    return pl.pallas_call(
        body, grid=(T // R,),
        in_specs=[pl.BlockSpec((R, D), lambda i: (i, 0)), blk, blk, _full_spec((1, RET_W))],
        out_specs=[blk, pl.BlockSpec((nc, RET_H, CH, CH), lambda i: (i, 0, 0, 0))],
        out_shape=[jax.ShapeDtypeStruct((T, RET_W), f32), jax.ShapeDtypeStruct((T // CH, RET_H, CH, CH), f32)],
        scratch_shapes=[pltpu.VMEM((RET_H, CH, CH), f32), pltpu.VMEM((R, RET_W), f32)],
        compiler_params=_cparams(("arbitrary",)), name="ret_fwd")(hr, cosw, sinw, gam)


def _ret_bwd(hr, cosw, sinw, gam, states, dout):
    T = hr.shape[0]
    R = RB_RET
    nc = R // CH
    nb = T // R

    def body(h_ref, c_ref, s_ref, g_ref, st_ref, do_ref, dh_ref, dgam_ref, dst, wide):
        @pl.when(pl.program_id(0) == 0)
        def _():
            dst[...] = jnp.zeros_like(dst)
            dgam_ref[...] = jnp.zeros_like(dgam_ref)

        consts = _ret_consts()
        cw, sw = c_ref[...], s_ref[...]
        q, k = h_ref[:, 0:RET_W], h_ref[:, RET_W:2 * RET_W]
        gr = h_ref[:, 3 * RET_W:4 * RET_W]
        qh = _split_heads((q * cw + _partner(q) * sw) * 0.125, RET_H)
        kh = _split_heads(k * cw + _partner(k) * sw, RET_H)
        vh = _split_heads(h_ref[:, 2 * RET_W:3 * RET_W], RET_H)
        do = do_ref[...]
        gam = g_ref[...]
        sg = _silu(gr)
        don = _split_heads(do * gam * sg, RET_H)
        ons, dqs, dks, dvs = [None] * nc, [None] * nc, [None] * nc, [None] * nc
        ds = dst[...]
        for c in reversed(range(nc)):
            sl = slice(c * RET_H, (c + 1) * RET_H)
            (on, _), vjp = jax.vjp(functools.partial(_ret_chunk, consts), qh[sl], kh[sl], vh[sl], st_ref[c])
            dqs[c], dks[c], dvs[c], ds = vjp((don[sl], ds))
            ons[c] = on
        dst[...] = ds
        _merge_heads(wide, jnp.concatenate(ons, axis=0), RET_H)
        onw = wide[...]
        dgam_ref[...] += jnp.sum(do * onw * sg, 0, keepdims=True)
        dh_ref[:, 3 * RET_W:4 * RET_W] = do * onw * gam * _dsilu(gr)
        _merge_heads(wide, jnp.concatenate(dqs, axis=0), RET_H)
        u = wide[...] * 0.125
        dh_ref[:, 0:RET_W] = u * cw + _partner(u * sw)
        _merge_heads(wide, jnp.concatenate(dks, axis=0), RET_H)
        u = wide[...]
        dh_ref[:, RET_W:2 * RET_W] = u * cw + _partner(u * sw)
        _merge_heads(dh_ref, jnp.concatenate(dvs, axis=0), RET_H, col0=2 * RET_W)

    blk = pl.BlockSpec((R, RET_W), lambda i: (nb - 1 - i, 0))
    return pl.pallas_call(
        body, grid=(nb,),
        in_specs=[pl.BlockSpec((R, D), lambda i: (nb - 1 - i, 0)), blk, blk, _full_spec((1, RET_W)),
                  pl.BlockSpec((nc, RET_H, CH, CH), lambda i: (nb - 1 - i, 0, 0, 0)), blk],
        out_specs=[pl.BlockSpec((R, D), lambda i: (nb - 1 - i, 0)), _full_spec((1, RET_W))],
        out_shape=[jax.ShapeDtypeStruct((T, D), f32), jax.ShapeDtypeStruct((1, RET_W), f32)],
        scratch_shapes=[pltpu.VMEM((RET_H, CH, CH), f32), pltpu.VMEM((R, RET_W), f32)],
        compiler_params=_cparams(("arbitrary",)), name="ret_bwd")(hr, cosw, sinw, gam, states, dout)


def _lru_ab(xc, wa, ba, wx, bx, lam):
    r = _sigmoid(_dot(xc, wa) + ba)
    i = _sigmoid(_dot(xc, wx) + bx)
    la = 8.0 * r * (-_softplus(-lam))
    a = jnp.exp(la)
    em = jnp.tanh(la) * (jnp.exp(2.0 * la) + 1.0)
    return a, jnp.sqrt(-em) * (i * xc)


def _lru_out(h, gate):
    return h * _gelu(gate)


def _scan_fwd(a, b):
    R = a.shape[0]
    row = lax.broadcasted_iota(jnp.int32, a.shape, 0)
    d = 1
    while d < R:
        m = row >= d
        b = jnp.where(m, a * pltpu.roll(b, d, 0) + b, b)
        a = jnp.where(m, a * pltpu.roll(a, d, 0), a)
        d *= 2
    return a, b


def _scan_bwd(a, b):
    R = a.shape[0]
    row = lax.broadcasted_iota(jnp.int32, a.shape, 0)
    d = 1
    while d < R:
        m = row < R - d
        b = jnp.where(m, a * pltpu.roll(b, R - d, 0) + b, b)
        a = jnp.where(m, a * pltpu.roll(a, R - d, 0), a)
        d *= 2
    return b


def _lru_fwd(hl, cw, cb, wa, ba, wx, bx, lam):
    T = hl.shape[0]
    R = RB_LRU
    W = LRU_W

    def body(h_ref, t_ref, cw_ref, cb_ref, wa_ref, ba_ref, wx_ref, bx_ref, lam_ref, o_ref, hs_ref, carry, ext):
        first = pl.program_id(0) == 0

        @pl.when(first)
        def _():
            carry[...] = jnp.zeros_like(carry)

        tail = jnp.where(first, 0.0, t_ref[:, 0:W])
        xc = _conv_fwd(ext, h_ref[:, 0:W], tail, cw_ref[...], R) + cb_ref[...]
        a, b = _lru_ab(xc, wa_ref[...], ba_ref[...], wx_ref[...], bx_ref[...], lam_ref[...])
        ap, hloc = _scan_fwd(a, b)
        h = hloc + ap * carry[0:1, :]
        carry[...] = jnp.broadcast_to(h[R - 1:R, :], carry.shape)
        hs_ref[...] = h
        o_ref[...] = _lru_out(h, h_ref[:, W:2 * W])

    vec = _full_spec((1, W))
    blk = pl.BlockSpec((R, W), lambda i: (i, 0))
    return pl.pallas_call(
        body, grid=(T // R,),
        in_specs=[pl.BlockSpec((R, 2 * W), lambda i: (i, 0)), _prev_tail_spec(R, 2 * W), _full_spec((4, W)), vec,
                  _full_spec((W, W)), vec, _full_spec((W, W)), vec, vec],
        out_specs=[blk, blk], out_shape=[jax.ShapeDtypeStruct((T, W), f32)] * 2,
        scratch_shapes=[pltpu.VMEM((8, W), f32), pltpu.VMEM((R + 8, W), f32)],
        compiler_params=_cparams(("arbitrary",)), name="lru_fwd")(hl, hl, cw, cb, wa, ba, wx, bx, lam)


def _lru_bwd(hl, hs, cw, cb, wa, ba, wx, bx, lam, dout):
    T = hl.shape[0]
    R = RB_LRU
    W = LRU_W
    nb = T // R

    def body(h_ref, t_ref, hs_ref, hst_ref, cw_ref, cb_ref, wa_ref, ba_ref, wx_ref, bx_ref, lam_ref, do_ref,
             dh_ref, dcw_ref, dcb_ref, dwa_ref, dba_ref, dwx_ref, dbx_ref, dlam_ref, carry_g, carry_dy, ext, ext2):
        i = pl.program_id(0)
        last_blk = i == 0
        first_blk = i == nb - 1

        @pl.when(last_blk)
        def _():
            carry_g[...] = jnp.zeros_like(carry_g)
            carry_dy[...] = jnp.zeros_like(carry_dy)
            for r in (dcw_ref, dcb_ref, dwa_ref, dba_ref, dwx_ref, dbx_ref, dlam_ref):
                r[...] = jnp.zeros_like(r)

        tail = jnp.where(first_blk, 0.0, t_ref[:, 0:W])
        xc = _conv_fwd(ext, h_ref[:, 0:W], tail, cw_ref[...], R) + cb_ref[...]
        (a, _), vjp_ab = jax.vjp(_lru_ab, xc, wa_ref[...], ba_ref[...], wx_ref[...], bx_ref[...], lam_ref[...])
        hs = hs_ref[...]
        _, vjp_out = jax.vjp(_lru_out, hs, h_ref[:, W:2 * W])
        dh, dgate = vjp_out(do_ref[...])
        row = lax.broadcasted_iota(jnp.int32, (R, W), 0)
        dh = jnp.where(row == R - 1, dh + carry_g[0:1, :], dh)
        a_up = jnp.where(row == R - 1, 0.0, pltpu.roll(a, R - 1, 0))
        g = _scan_bwd(a_up, dh)
        carry_g[...] = jnp.broadcast_to(a[0:1, :] * g[0:1, :], carry_g.shape)
        hprev0 = jnp.where(first_blk, 0.0, hst_ref[7:8, :])
        hprev = jnp.where(row == 0, hprev0, pltpu.roll(hs, 1, 0))
        dxc, dwa, dba, dwx, dbx, dlam = vjp_ab((g * hprev, g))
        dwa_ref[...] += dwa
        dba_ref[...] += dba
        dwx_ref[...] += dwx
        dbx_ref[...] += dbx
        dlam_ref[...] += dlam
        dcb_ref[...] += jnp.sum(dxc, 0, keepdims=True)
        dx, dcw = _conv_bwd(ext, ext2, dxc, carry_dy[...], cw_ref[...], R)
        carry_dy[...] = dxc[0:8, :]
        dcw_ref[...] += dcw
        dh_ref[:, 0:W] = dx
        dh_ref[:, W:2 * W] = dgate

    vec = _full_spec((1, W))
    mat = _full_spec((W, W))
    blk = pl.BlockSpec((R, W), lambda i: (nb - 1 - i, 0))
    blk2 = pl.BlockSpec((R, 2 * W), lambda i: (nb - 1 - i, 0))
    return pl.pallas_call(
        body, grid=(nb,),
        in_specs=[blk2, _prev_tail_spec_rev(R, 2 * W, nb), blk, _prev_tail_spec_rev(R, W, nb), _full_spec((4, W)), vec,
                  mat, vec, mat, vec, vec, blk],
        out_specs=[blk2, _full_spec((4, W)), vec, mat, vec, mat, vec, vec],
        out_shape=[jax.ShapeDtypeStruct((T, 2 * W), f32), jax.ShapeDtypeStruct((4, W), f32),
                   jax.ShapeDtypeStruct((1, W), f32), jax.ShapeDtypeStruct((W, W), f32),
                   jax.ShapeDtypeStruct((1, W), f32), jax.ShapeDtypeStruct((W, W), f32),
                   jax.ShapeDtypeStruct((1, W), f32), jax.ShapeDtypeStruct((1, W), f32)],
        scratch_shapes=[pltpu.VMEM((8, W), f32), pltpu.VMEM((8, W), f32), pltpu.VMEM((R + 8, W), f32),
                        pltpu.VMEM((R + 8, W), f32)],
        compiler_params=_cparams(("arbitrary",)), name="lru_bwd")(hl, hl, hs, hs, cw, cb, wa, ba, wx, bx, lam, dout)


def _gdn_local(qs, ks, vs, gc, bb):
    B = qs.shape[0]
    ii = lax.broadcasted_iota(jnp.int32, (B, CH, CH), 1)
    jj = lax.broadcasted_iota(jnp.int32, (B, CH, CH), 2)
    q = qs * lax.rsqrt(jnp.sum(qs * qs, -1, keepdims=True) + 1e-6)
    k = ks * lax.rsqrt(jnp.sum(ks * ks, -1, keepdims=True) + 1e-6)
    gct = jnp.swapaxes(gc, 1, 2)
    decay = jnp.where(ii >= jj, jnp.exp(jnp.minimum(gc - gct, 0.0)), 0.0)
    kk = _bmm('bid,bjd->bij', k, k)
    inv = _neumann_inv(-jnp.where(ii > jj, bb * kk * decay, 0.0))
    egc = jnp.exp(gc)
    u = _bmm3('bij,bje->bie', inv, vs * bb)
    w = _bmm3('bij,bje->bie', inv, k * (bb * egc))
    qk = _bmm('bid,bjd->bij', q, k) * (0.125 * decay)
    glast = gc[:, CH - 1:CH, :]
    return u, w, qk, q * (0.125 * egc), k * jnp.exp(glast - gc), jnp.exp(jnp.broadcast_to(glast, gc.shape))


def _gdn_step(st, u, w, qk, qd, kt, egl, z, gn):
    vnew = u - _bmm('hcd,hde->hce', w, st)
    o = _bmm('hcd,hde->hce', qd, st) + _bmm('hij,hje->hie', qk, vnew)
    st2 = st * egl + _bmm('hcd,hce->hde', kt, vnew)
    out = o * lax.rsqrt(jnp.mean(o * o, -1, keepdims=True) + 1e-6) * gn * _silu(z)
    return out, st2


def _gdn_scalars(ab, alog, dtb):
    sp = _softplus(ab + dtb)
    return -jnp.exp(alog) * sp, _sigmoid(ab)


def _bcast_heads(blk, lane0, H):
    R = blk.shape[0]
    n = R // CH
    parts = [jnp.broadcast_to(blk[:, lane0 + h:lane0 + h + 1], (R, CH)).reshape(n, CH, CH) for h in range(H)]
    return jnp.stack(parts, axis=1).reshape(n * H, CH, CH)


def _unbcast_heads(x, lane0, H):
    n = x.shape[0] // H
    R = n * CH
    s = jnp.sum(x, axis=2, keepdims=True).reshape(n, H, CH, 1)
    lane = lax.broadcasted_iota(jnp.int32, (R, 128), 1)
    acc = jnp.zeros((R, 128), f32)
    for h in range(H):
        acc = acc + jnp.where(lane == lane0 + h, jnp.broadcast_to(s[:, h].reshape(R, 1), (R, 128)), 0.0)
    return acc


def _gdn_fwd(hg, cw, alog, dtb, gn):
    T = hg.shape[0]
    R = RB_GDN
    nc = R // CH
    W3 = 3 * GDN_W
    H = GDN_H

    def body(h_ref, t_ref, cw_ref, al_ref, dt_ref, gn_ref, o_ref, st_ref, st, ext):
        first = pl.program_id(0) == 0

        @pl.when(first)
        def _():
            st[...] = jnp.zeros_like(st)

        tail = jnp.where(first, 0.0, t_ref[:, 0:W3])
        y = _silu(_conv_fwd(ext, h_ref[:, 0:W3], tail, cw_ref[...], R))
        qs, ks, vs = (_split_heads(y[:, j * GDN_W:(j + 1) * GDN_W], H) for j in range(3))
        zh = _split_heads(h_ref[:, W3:W3 + GDN_W], H)
        g, beta = _gdn_scalars(h_ref[:, W3 + GDN_W:GDN_IN], al_ref[...], dt_ref[...])
        loc = _gdn_local(qs, ks, vs, _bcast_heads(_chunk_cumsum(g), 0, H), _bcast_heads(beta, H, H))
        gnv = gn_ref[...]
        outs = []
        s_cur = st[...]
        for c in range(nc):
            sl = slice(c * H, (c + 1) * H)
            st_ref[c] = s_cur
            out, s_cur = _gdn_step(s_cur, *(t[sl] for t in loc), zh[sl], gnv)
            outs.append(out)
        st[...] = s_cur
        _merge_heads(o_ref, jnp.concatenate(outs, axis=0), H)

    return pl.pallas_call(
        body, grid=(T // R,),
        in_specs=[pl.BlockSpec((R, GDN_IN), lambda i: (i, 0)), _prev_tail_spec(R, GDN_IN), _full_spec((4, W3)),
                  _full_spec((1, 128)), _full_spec((1, 128)), _full_spec((1, CH))],
        out_specs=[pl.BlockSpec((R, GDN_W), lambda i: (i, 0)), pl.BlockSpec((nc, H, CH, CH), lambda i: (i, 0, 0, 0))],
        out_shape=[jax.ShapeDtypeStruct((T, GDN_W), f32), jax.ShapeDtypeStruct((T // CH, H, CH, CH), f32)],
        scratch_shapes=[pltpu.VMEM((H, CH, CH), f32), pltpu.VMEM((R + 8, W3), f32)],
        compiler_params=_cparams(("arbitrary",)), name="gdn_fwd")(hg, hg, cw, alog, dtb, gn)


def _gdn_bwd(hg, cw, alog, dtb, gn, states, dout):
    T = hg.shape[0]
    R = RB_GDN
    nc = R // CH
    nb = T // R
    W3 = 3 * GDN_W
    H = GDN_H

    def body(h_ref, t_ref, cw_ref, al_ref, dt_ref, gn_ref, st_ref, do_ref,
             dh_ref, dcw_ref, dal_ref, ddt_ref, dgn_ref, dst, carry_dy, ext, ext2, wide):
        i = pl.program_id(0)
        first_blk = i == nb - 1

        @pl.when(i == 0)
        def _():
            dst[...] = jnp.zeros_like(dst)
            carry_dy[...] = jnp.zeros_like(carry_dy)
            for r in (dcw_ref, dal_ref, ddt_ref, dgn_ref):
                r[...] = jnp.zeros_like(r)

        tail = jnp.where(first_blk, 0.0, t_ref[:, 0:W3])
        ypre = _conv_fwd(ext, h_ref[:, 0:W3], tail, cw_ref[...], R)
        y = _silu(ypre)
        qs, ks, vs = (_split_heads(y[:, j * GDN_W:(j + 1) * GDN_W], H) for j in range(3))
        zh = _split_heads(h_ref[:, W3:W3 + GDN_W], H)
        ab = h_ref[:, W3 + GDN_W:GDN_IN]
        alog, dtb = al_ref[...], dt_ref[...]
        g, beta = _gdn_scalars(ab, alog, dtb)
        loc, vjp_loc = jax.vjp(_gdn_local, qs, ks, vs, _bcast_heads(_chunk_cumsum(g), 0, H),
                               _bcast_heads(beta, H, H))
        doh = _split_heads(do_ref[...], H)
        gnv = gn_ref[...]
        dloc = [[None] * nc for _ in range(6)]
        dzs = [None] * nc
        ds = dst[...]
        dgn = jnp.zeros((1, CH), f32)
        for c in reversed(range(nc)):
            sl = slice(c * H, (c + 1) * H)
            _, vjp = jax.vjp(_gdn_step, st_ref[c], *(t[sl] for t in loc), zh[sl], gnv)
            grads = vjp((doh[sl], ds))
            ds = grads[0]
            for j in range(6):
                dloc[j][c] = grads[1 + j]
            dzs[c] = grads[7]
            dgn = dgn + grads[8]
        dst[...] = ds
        dgn_ref[...] += dgn
        dqs, dks, dvs, dgb, dbb = vjp_loc(tuple(jnp.concatenate(d, axis=0) for d in dloc))
        lane = lax.broadcasted_iota(jnp.int32, (R, 128), 1)
        dg = _chunk_cumsum(_unbcast_heads(dgb, 0, H), reverse=True)
        dbeta = _unbcast_heads(dbb, H, H)
        da = dg * (-jnp.exp(alog)) * _sigmoid(ab + dtb)
        dh_ref[:, W3 + GDN_W:GDN_IN] = jnp.where(lane < H, da, dbeta * beta * (1.0 - beta))
        ddt_ref[...] += jnp.sum(jnp.where(lane < H, da, 0.0), 0, keepdims=True)
        dal_ref[...] += jnp.sum(jnp.where(lane < H, dg * g, 0.0), 0, keepdims=True)
        _merge_heads(dh_ref, jnp.concatenate(dzs, axis=0), H, col0=W3)
        for j, dpart in enumerate((dqs, dks, dvs)):
            _merge_heads(wide, dpart, H, col0=j * GDN_W)
        dy = wide[...] * _dsilu(ypre)
        dx, dcw = _conv_bwd(ext, ext2, dy, carry_dy[...], cw_ref[...], R)
        carry_dy[...] = dy[0:8, :]
        dcw_ref[...] += dcw
        dh_ref[:, 0:W3] = dx

    blk = pl.BlockSpec((R, GDN_IN), lambda i: (nb - 1 - i, 0))
    return pl.pallas_call(
        body, grid=(nb,),
        in_specs=[blk, _prev_tail_spec_rev(R, GDN_IN, nb), _full_spec((4, W3)), _full_spec((1, 128)),
                  _full_spec((1, 128)), _full_spec((1, CH)),
                  pl.BlockSpec((nc, H, CH, CH), lambda i: (nb - 1 - i, 0, 0, 0)),
                  pl.BlockSpec((R, GDN_W), lambda i: (nb - 1 - i, 0))],
        out_specs=[blk, _full_spec((4, W3)), _full_spec((1, 128)), _full_spec((1, 128)), _full_spec((1, CH))],
        out_shape=[jax.ShapeDtypeStruct((T, GDN_IN), f32), jax.ShapeDtypeStruct((4, W3), f32),
                   jax.ShapeDtypeStruct((1, 128), f32), jax.ShapeDtypeStruct((1, 128), f32),
                   jax.ShapeDtypeStruct((1, CH), f32)],
        scratch_shapes=[pltpu.VMEM((H, CH, CH), f32), pltpu.VMEM((8, W3), f32), pltpu.VMEM((R + 8, W3), f32),
                        pltpu.VMEM((R + 8, W3), f32), pltpu.VMEM((R, W3), f32)],
        compiler_params=_cparams(("arbitrary",)), name="gdn_bwd")(hg, hg, cw, alog, dtb, gn, states, dout)


def _block_diag(w):
    out = jnp.zeros((LRU_W, LRU_W), w.dtype)
    for g in range(w.shape[0]):
        out = lax.dynamic_update_slice(out, w[g], (g * CH, g * CH))
    return out


def _block_diag_t(w):
    return jnp.stack([w[g * CH:(g + 1) * CH, g * CH:(g + 1) * CH] for g in range(LRU_W // CH)])


def _pad_lanes(v, n=128):
    return jnp.pad(v, (0, n - v.shape[0]))[None, :]


def _local_step(x, p, positions, target, fetch, emit, sm):
    cosw, sinw = _rope_tables(positions)
    saved = []
    h = x
    for l in range(DEPTH):
        v = lambda n: sm[n][l][None, :]
        G, tok = fetch(l, h)
        pd, pr, pinl, ping, wpp = G['pd'], G['pr'], G['pinl'], G['ping'], G['wpp']
        p384 = jnp.transpose(G['p384'], (1, 2, 0, 3)).reshape(-1, D, NDEV * FSP)
        wts = (p384, pd, pr, pinl, ping, wpp)
        z1, x1, g1, u1 = _ffn_fwd(h, p384, pd, v('ln_ffn1_g') + tok, v('ln_ffn1_b'), 0, 0)
        hr, hl, hg = _proj_in(x1, pr, pinl, ping, 0)
        o_r, rst = _ret_fwd(hr, cosw, sinw, v('ret_norm_g'))
        lru_args = (sm['lru_conv_w'][l], v('lru_conv_b'), _block_diag(sm['lru_w_a'][l]), v('lru_b_a'),
                    _block_diag(sm['lru_w_x'][l]), v('lru_b_x'), v('lru_lambda'))
        o_l, hs = _lru_fwd(hl, *lru_args)
        gdn_args = (sm['gdn_conv_w'][l], _pad_lanes(sm['gdn_a_log'][l]), _pad_lanes(sm['gdn_dt_bias'][l]),
                    v('gdn_norm_g'))
        o_g, gst = _gdn_fwd(hg, *gdn_args)
        z2, x2 = _mix_out(x1, o_r, o_l, o_g, pr, v('ln_mix_g'), v('ln_mix_b'), 0)
        z3, x3, g2, u2 = _ffn_fwd(x2, p384, pd, v('ln_ffn2_g'), v('ln_ffn2_b'), 0, 1, ple=(p[l], pr, wpp))
        saved.append((h, z1, x1, hr, hl, hg, o_r, rst, o_l, hs, lru_args, o_g, gst, gdn_args, z2, x2, z3,
                      g1, u1, g2, u2, wts))
        h = x3
    d, loss = _loss_grad(h, target)

    small = {n: [None] * DEPTH for n in SMALL}
    tok = 0.0
    for l in reversed(range(DEPTH)):
        (x0, z1, x1, hr, hl, hg, o_r, rst, o_l, hs, lru_args, o_g, gst, gdn_args, z2, x2, z3,
         g1, u1, g2, u2, wts) = saved[l]
        p384, pd, pr, pinl, ping, wpp = wts
        v = lambda n: sm[n][l][None, :]
        d2, dg2, du2, a2, dy2, small['ln_ffn2_g'][l], small['ln_ffn2_b'][l] = _ffn_bwd(
            z3, d, g2, u2, p384, pd, v('ln_ffn2_g') + tok, 0, 1)
        d2, dgp, dpj = _ple_bwd(x2, p[l], dy2, d2, pr, wpp, 0)
        dxb, dzb, do_r, do_l, do_g, small['ln_mix_g'][l], small['ln_mix_b'][l] = _mix_out_bwd(
            z2, d2, pr, v('ln_mix_g'), 0)
        dhr, small['ret_norm_g'][l] = _ret_bwd(hr, cosw, sinw, v('ret_norm_g'), rst, do_r)
        (dhl, small['lru_conv_w'][l], small['lru_conv_b'][l], dwa, small['lru_b_a'][l], dwx, small['lru_b_x'][l],
         small['lru_lambda'][l]) = _lru_bwd(hl, hs, *lru_args, do_l)
        small['lru_w_a'][l], small['lru_w_x'][l] = _block_diag_t(dwa), _block_diag_t(dwx)
        dhg, small['gdn_conv_w'][l], dal, ddt, small['gdn_norm_g'][l] = _gdn_bwd(hg, *gdn_args, gst, do_g)
        small['gdn_a_log'][l], small['gdn_dt_bias'][l] = dal[:, 0:GDN_H], ddt[:, 0:GDN_H]
        d1 = _proj_in_bwd(dxb, dhr, dhl, dhg, pr, pinl, ping, 0)
        d, dg1, du1, a1, dy1, small['ln_ffn1_g'][l], small['ln_ffn1_b'][l] = _ffn_bwd(
            z1, d1, g1, u1, p384, pd, v('ln_ffn1_g'), 0, 0)
        rows = lambda m: m.reshape(NDEV, m.shape[1] // NDEV, m.shape[2])
        dwo = jnp.concatenate([_matmul_tn(o_r, dzb, D, "dw_out_r"), _matmul_tn(o_l, dzb, D, "dw_out_l"),
                               _matmul_tn(o_g, dzb, D, "dw_out_g")], axis=1)
        tok = emit(l, {
            'p384': jnp.stack([_matmul_tn(x0, dg1, FSP, "dw_gate", FB), _matmul_tn(x0, du1, FSP, "dw_up", FB),
                               _matmul_tn(x2, dg2, FSP, "dw_gate", FB), _matmul_tn(x2, du2, FSP, "dw_up", FB)],
                              axis=1),
            'pd': jnp.stack([rows(_matmul_tn(a1, dy1, D, "dw_down")), rows(_matmul_tn(a2, dy2, D, "dw_down"))],
                            axis=1),
            'pr': jnp.stack([rows(_matmul_tn(x1, dhr, D, "dw_in_r")), rows(dwo),
                             rows(_matmul_tn(x2, dgp, D, "dw_ple_gate"))], axis=1),
            'pinl': rows(_matmul_tn(x1, dhl, 2 * LRU_W, "dw_in_l"))[:, None],
            'ping': rows(_matmul_tn(x1, dhg, GDN_IN, "dw_in_g"))[:, None],
            'ppp': _matmul_tn(p[l], dpj, 128, "dw_ple_proj")[:, None]})
    small = {n: jnp.stack([g.reshape(sm[n].shape[1:]) for g in gs]) for n, gs in small.items()}
    return loss, d, small


def _pack_big(ws, dtype=bf16):
    padc = lambda a, n: jnp.pad(a, ((0, 0), (0, 0), (0, n - a.shape[2])))
    padr = lambda a, n: jnp.pad(a, ((0, 0), (0, n - a.shape[1]), (0, 0)))
    per_layer = lambda arrs: jnp.stack(arrs, axis=1).reshape((-1,) + arrs[0].shape[1:])
    w_in = ws['w_in']
    out = {
        'p384': per_layer([padc(ws[n], FSP) for n in ('ffn1_w_gate', 'ffn1_w_up', 'ffn2_w_gate', 'ffn2_w_up')]),
        'pd': per_layer([padr(ws[n], FSP) for n in ('ffn1_w_down', 'ffn2_w_down')]),
        'pr': per_layer([w_in[:, :, 0:D], ws['w_out'], ws['ple_w_gate']]),
        'pinl': w_in[:, :, D:D + 2 * LRU_W],
        'ping': padc(w_in[:, :, D + 2 * LRU_W:D_IN], GDN_IN),
        'ppp': ws['ple_w_proj'],
    }
    return {k: a.astype(dtype) for k, a in out.items()}


def _exchange(arrays, scatter, name):
    n = len(arrays)

    def body(*refs):
        ins, outs = refs[:n], refs[n:2 * n]
        send_sems, recv_sems, local_sems = refs[2 * n:]
        x, y, c = lax.axis_index("x"), lax.axis_index("y"), lax.axis_index("c")
        me = 4 * x + 2 * y + c
        copies = []
        for i in range(n):
            src = ins[i].at[me] if scatter[i] else ins[i]
            cp = pltpu.make_async_copy(src, outs[i].at[me], local_sems.at[i])
            cp.start()
            copies.append(cp)
        sends = []
        for j in range(1, NDEV):
            peer = (me + j) % NDEV
            pid = (peer // 4, (peer // 2) % 2, peer % 2)
            for i in range(n):
                src = ins[i].at[peer] if scatter[i] else ins[i]
                cp = pltpu.make_async_remote_copy(
                    src_ref=src, dst_ref=outs[i].at[me], send_sem=send_sems.at[i, j - 1],
                    recv_sem=recv_sems.at[i, j - 1], device_id=pid, device_id_type=pl.DeviceIdType.MESH)
                cp.start()
                sends.append(cp)
        for j in range(1, NDEV):
            source = (me + NDEV - j) % NDEV
            sid = (source // 4, (source // 2) % 2, source % 2)
            for i in range(n):
                src = ins[i].at[me] if scatter[i] else ins[i]
                pltpu.make_async_remote_copy(
                    src_ref=src, dst_ref=outs[i].at[source], send_sem=send_sems.at[i, j - 1],
                    recv_sem=recv_sems.at[i, j - 1], device_id=sid, device_id_type=pl.DeviceIdType.MESH).wait_recv()
        for cp in sends:
            cp.wait_send()
        for cp in copies:
            cp.wait()

    hbm = pl.BlockSpec(memory_space=pltpu.HBM)
    out_shape = [jax.ShapeDtypeStruct(a.shape if s else (NDEV,) + a.shape, a.dtype) for a, s in zip(arrays, scatter)]
    return pl.pallas_call(
        body, in_specs=[hbm] * n, out_specs=[hbm] * n, out_shape=out_shape,
        scratch_shapes=[pltpu.SemaphoreType.DMA((n, NDEV - 1)), pltpu.SemaphoreType.DMA((n, NDEV - 1)),
                        pltpu.SemaphoreType.DMA((n,))],
        compiler_params=pltpu.CompilerParams(has_side_effects=True), name=name)(*arrays)


def _gather_two_level(arrays, name):
    n = len(arrays)

    def body(*refs):
        ins, outs = refs[:n], refs[n:2 * n]
        send_sems, recv_sems, local_sems = refs[2 * n:]
        x, y, c = lax.axis_index("x"), lax.axis_index("y"), lax.axis_index("c")
        me, sibling = (x, y, c), (x, y, 1 - c)
        chips = [(1 - x, y), (x, 1 - y), (1 - x, 1 - y)]
        slot = lambda d: 4 * d[0] + 2 * d[1] + d[2]

        def copy(i, k, block, to, src=None):
            return pltpu.make_async_remote_copy(
                src_ref=outs[i].at[slot(block)] if src is None else src, dst_ref=outs[i].at[slot(block)],
                send_sem=send_sems.at[i, k], recv_sem=recv_sems.at[i, k], device_id=to,
                device_id_type=pl.DeviceIdType.MESH)

        mine, first, passed = [], [], []
        for i in range(n):
            cp = pltpu.make_async_copy(ins[i], outs[i].at[slot(me)], local_sems.at[i])
            cp.start()
            mine.append(cp)
            first.append(copy(i, 0, me, sibling, src=ins[i]))
            first += [copy(i, 1 + j, me, (*chip, c), src=ins[i]) for j, chip in enumerate(chips)]
        for cp in first:
            cp.start()
        for i in range(n):
            for j, chip in enumerate(chips):
                copy(i, 1 + j, (*chip, c), me).wait_recv()
                cp = copy(i, 4 + j, (*chip, c), sibling)
                cp.start()
                passed.append(cp)
        for i in range(n):
            copy(i, 0, sibling, me).wait_recv()
            for j, chip in enumerate(chips):
                copy(i, 4 + j, (*chip, 1 - c), me).wait_recv()
        for cp in first + passed:
            cp.wait_send()
        for cp in mine:
            cp.wait()

    hbm = pl.BlockSpec(memory_space=pltpu.HBM)
    return pl.pallas_call(
        body, in_specs=[hbm] * n, out_specs=[hbm] * n,
        out_shape=[jax.ShapeDtypeStruct((NDEV,) + a.shape, a.dtype) for a in arrays],
        scratch_shapes=[pltpu.SemaphoreType.DMA((n, NDEV - 1)), pltpu.SemaphoreType.DMA((n, NDEV - 1)),
                        pltpu.SemaphoreType.DMA((n,))],
        compiler_params=pltpu.CompilerParams(has_side_effects=True), name=name)(*arrays)


def _scatter_pairs(arrays, name):
    n = len(arrays)

    def body(*refs):
        ins, gots = refs[:n], refs[n:2 * n]
        send_sems, recv_sems = refs[2 * n:]
        x, y, c = lax.axis_index("x"), lax.axis_index("y"), lax.axis_index("c")
        sends = []
        for i in range(n):
            for q in range(4):
                cp = pltpu.make_async_remote_copy(
                    src_ref=ins[i].at[2 * q + 1 - c], dst_ref=gots[i].at[q], send_sem=send_sems.at[i, q],
                    recv_sem=recv_sems.at[i, q], device_id=(x, y, 1 - c), device_id_type=pl.DeviceIdType.MESH)
                cp.start()
                sends.append(cp)
        for cp in sends:
            cp.wait_recv()
        for cp in sends:
            cp.wait_send()

    hbm = pl.BlockSpec(memory_space=pltpu.HBM)
    return pl.pallas_call(
        body, in_specs=[hbm] * n, out_specs=[hbm] * n,
        out_shape=[jax.ShapeDtypeStruct((4,) + a.shape[1:], a.dtype) for a in arrays],
        scratch_shapes=[pltpu.SemaphoreType.DMA((n, 4)), pltpu.SemaphoreType.DMA((n, 4))],
        compiler_params=pltpu.CompilerParams(has_side_effects=True), name=name)(*arrays)


def _pair_sum(own, got, name):
    def body(a_ref, b_ref, o_ref):
        o_ref[...] = (a_ref[...].astype(f32) + b_ref[...].astype(f32)).astype(bf16)

    spec = pl.BlockSpec((None, None) + own.shape[2:], lambda q, s: (q, s, 0, 0))
    return pl.pallas_call(
        body, grid=own.shape[:2], in_specs=[spec, spec], out_specs=spec,
        out_shape=jax.ShapeDtypeStruct(own.shape, bf16),
        compiler_params=_cparams(("arbitrary", "arbitrary")), name=name)(own, got)


def _scatter_chips(arrays, name):
    n = len(arrays)

    def body(*refs):
        ins, outs = refs[:n], refs[n:2 * n]
        send_sems, recv_sems, local_sems = refs[2 * n:]
        x, y, c = lax.axis_index("x"), lax.axis_index("y"), lax.axis_index("c")
        chip = 2 * x + y
        keeps, sends = [], []
        for i in range(n):
            cp = pltpu.make_async_copy(ins[i].at[chip], outs[i].at[chip], local_sems.at[i])
            cp.start()
            keeps.append(cp)
        for j in range(1, 4):
            peer = (chip + j) % 4
            for i in range(n):
                cp = pltpu.make_async_remote_copy(
                    src_ref=ins[i].at[peer], dst_ref=outs[i].at[chip], send_sem=send_sems.at[i, j - 1],
                    recv_sem=recv_sems.at[i, j - 1], device_id=(peer // 2, peer % 2, c),
                    device_id_type=pl.DeviceIdType.MESH)
                cp.start()
                sends.append(cp)
        for j in range(1, 4):
            source = (chip + 4 - j) % 4
            for i in range(n):
                pltpu.make_async_remote_copy(
                    src_ref=ins[i].at[chip], dst_ref=outs[i].at[source], send_sem=send_sems.at[i, j - 1],
                    recv_sem=recv_sems.at[i, j - 1], device_id=(source // 2, source % 2, c),
                    device_id_type=pl.DeviceIdType.MESH).wait_recv()
        for cp in sends:
            cp.wait_send()
        for cp in keeps:
            cp.wait()

    hbm = pl.BlockSpec(memory_space=pltpu.HBM)
    return pl.pallas_call(
        body, in_specs=[hbm] * n, out_specs=[hbm] * n,
        out_shape=[jax.ShapeDtypeStruct(a.shape, a.dtype) for a in arrays],
        scratch_shapes=[pltpu.SemaphoreType.DMA((n, 3)), pltpu.SemaphoreType.DMA((n, 3)),
                        pltpu.SemaphoreType.DMA((n,))],
        compiler_params=pltpu.CompilerParams(has_side_effects=True), name=name)(*arrays)


def _gather_plan(srcs, lands, x, y, c):
    me = 4 * x + 2 * y + c
    sends, arrivals = [], []
    for j in range(1, NDEV):
        peer, source = (me + j) % NDEV, (me + NDEV - j) % NDEV
        for i in range(len(srcs)):
            k = i * (NDEV - 1) + j - 1
            sends.append((srcs[i], lands[i].at[me], (peer // 4, (peer // 2) % 2, peer % 2), k))
            arrivals.append((srcs[i], lands[i].at[source], (source // 4, (source // 2) % 2, source % 2), k))
    return sends, arrivals


def _chips_plan(srcs, lands, x, y, c):
    chip = 2 * x + y
    sends, arrivals = [], []
    for j in range(1, 4):
        peer, source = (chip + j) % 4, (chip + 4 - j) % 4
        for i in range(len(srcs)):
            k = i * 3 + j - 1
            sends.append((srcs[i].at[peer], lands[i].at[chip], (peer // 2, peer % 2, c), k))
            arrivals.append((srcs[i].at[chip], lands[i].at[source], (source // 2, source % 2, c), k))
    return sends, arrivals


def _remote(entry, send_sems, recv_sems):
    src, dst, dev, k = entry
    return pltpu.make_async_remote_copy(src_ref=src, dst_ref=dst, send_sem=send_sems.at[k], recv_sem=recv_sems.at[k],
                                        device_id=dev, device_id_type=pl.DeviceIdType.MESH)


_HBM = pl.BlockSpec(memory_space=pltpu.HBM)
_SEM = pl.BlockSpec(memory_space=pltpu.SEMAPHORE)


def _split_start(arrays, land_shapes, plan, npeer, name):
    n = len(arrays)

    def body(*refs):
        srcs, lands = refs[:n], refs[n:2 * n]
        send_sems, recv_sems, token = refs[2 * n], refs[2 * n + 1], refs[-1]
        sends, _ = plan(srcs, lands, lax.axis_index("x"), lax.axis_index("y"), lax.axis_index("c"))
        for entry in sends:
            _remote(entry, send_sems, recv_sems).start()
        token[...] = jnp.zeros_like(token)

    lands = [lax.empty(s, a.dtype) for s, a in zip(land_shapes, arrays)]
    thru = [pltpu.HBM(a.shape, a.dtype) for a in arrays + lands]
    out = pl.pallas_call(
        body, name=name, in_specs=[_HBM] * (2 * n),
        out_specs=(_SEM, _SEM, *([_HBM] * (2 * n)), pl.BlockSpec(memory_space=pltpu.VMEM)),
        out_shape=(pltpu.SemaphoreType.DMA((n * npeer,)), pltpu.SemaphoreType.DMA((n * npeer,)), *thru,
                   jax.ShapeDtypeStruct((8, 128), f32)),
        input_output_aliases={i: 2 + i for i in range(2 * n)},
        compiler_params=pltpu.CompilerParams(has_side_effects=pltpu.SideEffectType.DATAFLOW_SIDE_EFFECTING),
    )(*[pltpu.with_memory_space_constraint(a, pltpu.HBM) for a in arrays + lands])
    return out[0], out[1], list(out[2:2 + n]), list(out[2 + n:2 + 2 * n]), out[-1]


def _split_wait(send_sems, recv_sems, srcs, lands, after, plan, name):
    n = len(srcs)

    def body(*refs):
        s_refs, l_refs = refs[:n], refs[n:2 * n]
        ssem, rsem = refs[2 * n], refs[2 * n + 1]
        sends, arrivals = plan(s_refs, l_refs, lax.axis_index("x"), lax.axis_index("y"), lax.axis_index("c"))
        for entry in sends:
            _remote(entry, ssem, rsem).wait_send()
        for entry in arrivals:
            _remote(entry, ssem, rsem).wait_recv()

    out = pl.pallas_call(
        body, name=name, in_specs=[_HBM] * (2 * n) + [_SEM, _SEM, pl.BlockSpec(memory_space=pl.ANY)],
        out_specs=[_HBM] * (2 * n), out_shape=[pltpu.HBM(a.shape, a.dtype) for a in srcs + lands],
        input_output_aliases={i: i for i in range(2 * n)},
        compiler_params=pltpu.CompilerParams(has_side_effects=pltpu.SideEffectType.DATAFLOW_SIDE_EFFECTING),
    )(*srcs, *lands, send_sems, recv_sems, after)
    return list(out[:n]), list(out[n:])


def _adam_math(w, g, m, v):
    m2 = ADAM_B1 * m + (1.0 - ADAM_B1) * g
    v2 = ADAM_B2 * v + (1.0 - ADAM_B2) * (g * g)
    m_hat = m2 / (1.0 - ADAM_B1 ** ADAM_STEP)
    v_hat = v2 / (1.0 - ADAM_B2 ** ADAM_STEP)
    return -ADAM_LR * (m_hat / (jnp.sqrt(v_hat) + ADAM_EPS) + ADAM_WD * w), m2, v2


def _adam_big(parts, w, m, v, name):
    L, rows, cols = w.shape
    flat = [(a, slot) for layer_parts in parts for a, slot in layer_parts]
    per = len(parts[0])

    def body(*refs):
        prefs = refs[:len(flat)]
        w_ref, m_ref, v_ref, g_ref, d_ref, m2_ref, v2_ref = refs[len(flat):]
        for li in range(L):
            @pl.when(pl.program_id(0) == li)
            def _():
                c0 = 0
                for pref in prefs[li * per:(li + 1) * per]:
                    acc = pref[0].astype(f32)
                    for s in range(1, pref.shape[0]):
                        acc = acc + pref[s].astype(f32)
                    width = min(acc.shape[1], cols - c0)
                    g_ref[:, c0:c0 + width] = acc[0:rows, 0:width]
                    c0 += width

        d, m2, v2 = _adam_math(w_ref[...], g_ref[...], m_ref[...], v_ref[...])
        d_ref[...] = d
        m2_ref[...] = m2
        v2_ref[...] = v2

    wspec = pl.BlockSpec((None, rows, cols), lambda l: (l, 0, 0))
    in_specs = [pl.BlockSpec((a.shape[0], None) + a.shape[2:], functools.partial(lambda l, slot: (0, slot, 0, 0), slot=slot))
                for a, slot in flat]
    return pl.pallas_call(
        body, grid=(L,), in_specs=in_specs + [wspec] * 3, out_specs=[wspec] * 4,
        out_shape=[jax.ShapeDtypeStruct(w.shape, f32)] * 4,
        compiler_params=_cparams(("arbitrary",)), name=name)(*[a for a, _ in flat], w, m, v)


def _sum_sources(stacked):
    rows = stacked.shape[1]

    def body(s_ref, o_ref):
        acc = s_ref[0]
        for s in range(1, NDEV):
            acc = acc + s_ref[s]
        o_ref[...] = acc

    return pl.pallas_call(body, out_shape=jax.ShapeDtypeStruct((rows, 128), f32), name="sum_small_grads")(stacked)


def _adam_small(w, g, m, v):
    def body(w_ref, g_ref, m_ref, v_ref, d_ref, m2_ref, v2_ref):
        d, m2, v2 = _adam_math(w_ref[...], g_ref[...], m_ref[...], v_ref[...])
        d_ref[...] = d
        m2_ref[...] = m2
        v2_ref[...] = v2

    return pl.pallas_call(body, out_shape=[jax.ShapeDtypeStruct(w.shape, f32)] * 3, name="adam_small")(w, g, m, v)


def _pack_rows(arrs):
    flat = []
    for a in arrs:
        a = a.reshape(-1)
        flat.append(jnp.pad(a, (0, (-a.shape[0]) % 1024)))
    return jnp.concatenate(flat).reshape(-1, 128)


def _unpack_rows(packed, shapes):
    out, off = [], 0
    flat = packed.reshape(-1)
    for s in shapes:
        n = math.prod(s)
        out.append(flat[off:off + n].reshape(s))
        off += n + (-n) % 1024
    return out


def _gather_conv(gathered, shape):
    L, K, c = shape
    return jnp.transpose(gathered, (1, 2, 0, 3)).reshape(L, K, NDEV * c)


def kernel(x, p, positions, ln_ffn1_g, ln_ffn1_b, ffn1_w_gate, ffn1_w_up, ffn1_w_down, w_in, ret_norm_g, lru_conv_w, lru_conv_b, lru_w_a, lru_b_a, lru_w_x, lru_b_x, lru_lambda, gdn_conv_w, gdn_a_log, gdn_dt_bias, gdn_norm_g, w_out, ln_mix_g, ln_mix_b, ffn2_w_gate, ffn2_w_up, ffn2_w_down, ple_w_gate, ple_w_proj, ln_ffn2_g, ln_ffn2_b, loss_target, m_ln_ffn1_g, m_ln_ffn1_b, m_ffn1_w_gate, m_ffn1_w_up, m_ffn1_w_down, m_w_in, m_ret_norm_g, m_lru_conv_w, m_lru_conv_b, m_lru_w_a, m_lru_b_a, m_lru_w_x, m_lru_b_x, m_lru_lambda, m_gdn_conv_w, m_gdn_a_log, m_gdn_dt_bias, m_gdn_norm_g, m_w_out, m_ln_mix_g, m_ln_mix_b, m_ffn2_w_gate, m_ffn2_w_up, m_ffn2_w_down, m_ple_w_gate, m_ple_w_proj, m_ln_ffn2_g, m_ln_ffn2_b, v_ln_ffn1_g, v_ln_ffn1_b, v_ffn1_w_gate, v_ffn1_w_up, v_ffn1_w_down, v_w_in, v_ret_norm_g, v_lru_conv_w, v_lru_conv_b, v_lru_w_a, v_lru_b_a, v_lru_w_x, v_lru_b_x, v_lru_lambda, v_gdn_conv_w, v_gdn_a_log, v_gdn_dt_bias, v_gdn_norm_g, v_w_out, v_ln_mix_g, v_ln_mix_b, v_ffn2_w_gate, v_ffn2_w_up, v_ffn2_w_down, v_ple_w_gate, v_ple_w_proj, v_ln_ffn2_g, v_ln_ffn2_b):
    args = locals()
    W = {n: args[n] for n in WEIGHTS}
    M = {n: args['m_' + n] for n in WEIGHTS}
    V = {n: args['v_' + n] for n in WEIGHTS}
    me = 4 * lax.axis_index("x") + 2 * lax.axis_index("y") + lax.axis_index("c")

    core = lax.axis_index("c")
    chip = 2 * lax.axis_index("x") + lax.axis_index("y")

    packed = _pack_big(W)
    layer_pack = lambda l: [packed[k][l * (packed[k].shape[0] // DEPTH):(l + 1) * (packed[k].shape[0] // DEPTH)]
                            for k in PACKS]
    conv_pack = _pack_rows([W[n] for n in CONV_SHARDED])
    g0 = _gather_two_level(layer_pack(0) + [conv_pack], "gather_weights")
    g0, src1 = lax.optimization_barrier((g0, layer_pack(1)))
    gather1 = _split_start(src1, [(NDEV,) + a.shape for a in src1], _gather_plan, NDEV - 1, "gather_start")

    def as_weights(arrs):
        G = dict(zip(PACKS, arrs))
        G['wpp'] = jnp.transpose(G.pop('ppp'), (1, 2, 0, 3)).reshape(PLE, D)
        return G

    def fetch(l, after):
        if l == 0:
            return as_weights(g0[:-1]), gather1[4][0, 0]
        srcs, lands = _split_wait(gather1[0], gather1[1], gather1[2], gather1[3], after, _gather_plan, "gather_wait")
        return as_weights([lax.dynamic_update_slice_in_dim(ld, s[None], me, axis=0)
                           for s, ld in zip(srcs, lands)]), 0.0

    conv_all = g0[-1]
    sm = {n: W[n] for n in SMALL}
    conv_shards = [_unpack_rows(conv_all[s], [W[n].shape for n in CONV_SHARDED]) for s in range(NDEV)]
    for i, n in enumerate(CONV_SHARDED):
        sm[n] = _gather_conv(jnp.stack([cs[i] for cs in conv_shards]), W[n].shape)

    received = [None] * DEPTH
    scatter1 = []

    def emit(l, grads_l):
        arrs = [grads_l[k] for k in PACKS]
        gots = _scatter_pairs(arrs, "scatter_pairs")
        owns = [lax.dynamic_index_in_dim(a.reshape((4, 2) + a.shape[1:]), core, axis=1, keepdims=False) for a in arrs]
        pair = [_pair_sum(o, g, "pair_sum_" + k) for k, o, g in zip(PACKS, owns, gots)]
        if l == 0:
            received[l] = _scatter_chips(pair, "scatter_chips")
            return 0.0
        scatter1.extend(_split_start(pair, [a.shape for a in pair], _chips_plan, 3, "scatter_start"))
        return scatter1[4][0, 0]

    loss, grad_x, small = _local_step(x[0], p[:, 0], positions.reshape(-1, 1), loss_target[0], fetch, emit, sm)
    loss = lax.psum(loss[0, 0], ("x", "y", "c"))
    srcs, lands = _split_wait(scatter1[0], scatter1[1], scatter1[2], scatter1[3], grad_x, _chips_plan, "scatter_wait")
    received[1] = [lax.dynamic_update_slice_in_dim(ld, lax.dynamic_index_in_dim(s, chip, axis=0), chip, axis=0)
                   for s, ld in zip(srcs, lands)]
    R = [dict(zip(PACKS, r)) for r in received]

    small_pack = _pack_rows([small[n] for n in SMALL])
    small_all = _exchange([small_pack], [False], "gather_small_grads")[0]
    small_sum = _unpack_rows(_sum_sources(small_all), [small[n].shape for n in SMALL])
    grads, delta, new_m, new_v = {}, {}, {}, {}
    for n, g in zip(SMALL, small_sum):
        if n in CONV_SHARDED:
            c = W[n].shape[2]
            g = lax.dynamic_slice_in_dim(g, me * c, c, axis=2)
        grads[n] = g

    big_parts = {
        'ffn1_w_gate': [('p384', 0)], 'ffn1_w_up': [('p384', 1)], 'ffn2_w_gate': [('p384', 2)],
        'ffn2_w_up': [('p384', 3)], 'ffn1_w_down': [('pd', 0)], 'ffn2_w_down': [('pd', 1)],
        'w_in': [('pr', 0), ('pinl', 0), ('ping', 0)], 'w_out': [('pr', 1)], 'ple_w_gate': [('pr', 2)],
        'ple_w_proj': [('ppp', 0)],
    }
    for n in BIG:
        parts = [[(R[l][k], slot) for k, slot in big_parts[n]] for l in range(DEPTH)]
        grads[n], delta[n], new_m[n], new_v[n] = _adam_big(parts, W[n], M[n], V[n], "adam_" + n)
    shapes = [W[n].shape for n in SMALL]
    d_s, m_s, v_s = _adam_small(*[_pack_rows([src[n] for n in SMALL]) for src in (W, grads, M, V)])
    for n, dd, mm, vv in zip(SMALL, _unpack_rows(d_s, shapes), _unpack_rows(m_s, shapes), _unpack_rows(v_s, shapes)):
        delta[n], new_m[n], new_v[n] = dd, mm, vv

    return (loss, grad_x[None], *[grads[n] for n in WEIGHTS], *[delta[n] for n in WEIGHTS],
            *[new_m[n] for n in WEIGHTS], *[new_v[n] for n in WEIGHTS])
```

```python
import functools
import math

import jax
import jax.numpy as jnp
from jax import lax
from jax.experimental import pallas as pl
from jax.experimental.pallas import tpu as pltpu

f32 = jnp.float32
bf16 = jnp.bfloat16

NDEV = 8
DEPTH = 2
D = 1024
FS = 352
FSP = 384
FB = 2
NF = NDEV // FB
PLE = 256
CH = 64
RET_H, GDN_H = 4, 6
RET_W, LRU_W, GDN_W = 256, 384, 384
GDN_IN = 1664
GDN_IN_REAL = 1548
D_IN = 3340
ALPHA = 4.0 ** 0.25
LN_EPS = 1e-5
ROPE_THETA = 10000.0
TM = 512
RB_RET, RB_LRU, RB_GDN = 512, 512, 256
VMEM_LIMIT = 56 * 1024 * 1024
ADAM_LR, ADAM_B1, ADAM_B2, ADAM_EPS, ADAM_WD, ADAM_STEP = 0.001, 0.9, 0.999, 1e-08, 0.01, 10

WEIGHTS = ['ln_ffn1_g', 'ln_ffn1_b', 'ffn1_w_gate', 'ffn1_w_up', 'ffn1_w_down', 'w_in', 'ret_norm_g', 'lru_conv_w',
           'lru_conv_b', 'lru_w_a', 'lru_b_a', 'lru_w_x', 'lru_b_x', 'lru_lambda', 'gdn_conv_w', 'gdn_a_log',
           'gdn_dt_bias', 'gdn_norm_g', 'w_out', 'ln_mix_g', 'ln_mix_b', 'ffn2_w_gate', 'ffn2_w_up', 'ffn2_w_down',
           'ple_w_gate', 'ple_w_proj', 'ln_ffn2_g', 'ln_ffn2_b']
BIG = ['ffn1_w_gate', 'ffn1_w_up', 'ffn1_w_down', 'w_in', 'w_out', 'ffn2_w_gate', 'ffn2_w_up', 'ffn2_w_down',
       'ple_w_gate', 'ple_w_proj']
SMALL = [n for n in WEIGHTS if n not in BIG]
PACKS = ('p384', 'pd', 'pr', 'pinl', 'ping', 'ppp')
CONV_SHARDED = {'lru_conv_w': LRU_W, 'gdn_conv_w': 3 * GDN_W}


def _cparams(sem=None):
    return pltpu.CompilerParams(dimension_semantics=sem, vmem_limit_bytes=VMEM_LIMIT)


def _sigmoid(x):
    return 1.0 / (1.0 + jnp.exp(-x))


def _silu(x):
    return x * _sigmoid(x)


def _dsilu(x):
    s = _sigmoid(x)
    return s * (1.0 + x * (1.0 - s))


def _softplus(x):
    return jnp.maximum(x, 0.0) + jnp.log(1.0 + jnp.exp(-jnp.abs(x)))


def _gelu(x):
    return 0.5 * x * (1.0 + jnp.tanh(0.7978845608028654 * (x + 0.044715 * x * x * x)))


def _dot(a, b):
    return jnp.dot(a.astype(bf16), b.astype(bf16), preferred_element_type=f32)


def _dot_nt(a, b):
    return lax.dot_general(a.astype(bf16), b.astype(bf16), (((1,), (1,)), ((), ())), preferred_element_type=f32)


def _dot_tn(a, b):
    return lax.dot_general(a.astype(bf16), b.astype(bf16), (((0,), (0,)), ((), ())), preferred_element_type=f32)


def _bmm(eq, a, b):
    return jnp.einsum(eq, a.astype(bf16), b.astype(bf16), preferred_element_type=f32)


def _split3(a):
    a1 = a.astype(bf16)
    r = a - a1.astype(f32)
    a2 = r.astype(bf16)
    return a1, a2, (r - a2.astype(f32)).astype(bf16)


def _bmm3(eq, a, b):
    a1, a2, _ = _split3(a)
    b1, b2, _ = _split3(b)
    e = lambda x, y: jnp.einsum(eq, x, y, preferred_element_type=f32)
    return e(a1, b1) + (e(a1, b2) + e(a2, b1))


def _rowsum(x):
    x1, x2, _ = _split3(x)
    ones = jnp.ones((x.shape[0], CH, CH), bf16)
    e = lambda y: jnp.einsum('bij,bjk->bik', y, ones, preferred_element_type=f32)
    return e(x1) + e(x2)


def _tri_ones(B, upper=False):
    ii = lax.broadcasted_iota(jnp.int32, (B, CH, CH), 1)
    jj = lax.broadcasted_iota(jnp.int32, (B, CH, CH), 2)
    return jnp.where((ii <= jj) if upper else (ii >= jj), 1.0, 0.0).astype(bf16)


def _cumsum_mm(t, x):
    x1, x2, x3 = _split3(x)
    e = lambda y: jnp.einsum('bij,bjk->bik', t, y, preferred_element_type=f32)
    return e(x1) + (e(x2) + e(x3))


def _chunk_cumsum(x, reverse=False):
    n = x.shape[0] // CH
    return _cumsum_mm(_tri_ones(n, upper=reverse), x.reshape(n, CH, 128)).reshape(x.shape)


@jax.custom_vjp
def _neumann_inv(m):
    ii = lax.broadcasted_iota(jnp.int32, m.shape, 1)
    jj = lax.broadcasted_iota(jnp.int32, m.shape, 2)
    inv = jnp.where(ii == jj, 1.0, 0.0).astype(f32) + m
    mp = m
    for _ in range(5):
        mp = _bmm3('bij,bjk->bik', mp, mp)
        inv = inv + _bmm3('bij,bjk->bik', inv, mp)
    return inv


def _neumann_inv_fwd(m):
    inv = _neumann_inv(m)
    return inv, inv


def _neumann_inv_bwd(inv, g):
    return (_bmm3('bij,bkj->bik', _bmm3('bji,bjk->bik', inv, g), inv),)


_neumann_inv.defvjp(_neumann_inv_fwd, _neumann_inv_bwd)


@jax.custom_vjp
def _known_inv(m, inv):
    return inv


def _known_inv_fwd(m, inv):
    return inv, inv


def _known_inv_bwd(inv, g):
    return _neumann_inv_bwd(inv, g)[0], jnp.zeros_like(inv)


_known_inv.defvjp(_known_inv_fwd, _known_inv_bwd)


def _ln_stats(z):
    mu = jnp.mean(z, -1, keepdims=True)
    zc = z - mu
    rstd = lax.rsqrt(jnp.mean(zc * zc, -1, keepdims=True) + LN_EPS)
    return zc * rstd, rstd


def _ln_bwd(z, g, dout):
    xh, rstd = _ln_stats(z)
    dxh = dout * g
    dz = rstd * (dxh - jnp.mean(dxh, -1, keepdims=True) - xh * jnp.mean(dxh * xh, -1, keepdims=True))
    return dz, jnp.sum(dout * xh, 0, keepdims=True), jnp.sum(dout, 0, keepdims=True)


def _full_spec(shape):
    nd = len(shape)
    return pl.BlockSpec(shape, lambda *_: (0,) * nd)


def _ffn_fwd(x, p384, pd, lg, lb, layer, which, ple=None):
    T = x.shape[0]
    sg, su, sd = 4 * layer + 2 * which, 4 * layer + 2 * which + 1, 2 * layer + which
    has_ple = ple is not None

    def body(*refs):
        if has_ple:
            (x_ref, wg_ref, wu_ref, wd_ref, lg_ref, lb_ref, p_ref, wpg_ref, wpp_ref,
             z_ref, o_ref, g_ref, u_ref, acc, xb_s) = refs
        else:
            x_ref, wg_ref, wu_ref, wd_ref, lg_ref, lb_ref, z_ref, o_ref, g_ref, u_ref, acc, xb_s = refs
        f = pl.program_id(1)

        @pl.when(f == 0)
        def _():
            x = x_ref[...]
            xb = x.astype(bf16)
            xb_s[...] = xb
            base = ALPHA * x
            if has_ple:
                gate = _sigmoid(_dot(xb, wpg_ref[...].reshape(D, D)))
                base = base + gate * _dot(p_ref[...], wpp_ref[...])
            acc[...] = base

        xb = xb_s[...]
        g = _dot(xb, wg_ref[...])
        u = _dot(xb, wu_ref[...])
        g_ref[...] = g.astype(bf16)
        u_ref[...] = u.astype(bf16)
        acc[...] += 0.5 * _dot(_silu(g) * u, wd_ref[...].reshape(FB * FSP, D))

        @pl.when(f == NF - 1)
        def _():
            z = acc[...]
            z_ref[...] = z
            o_ref[...] = _ln_stats(z)[0] * lg_ref[...] + lb_ref[...]

    row = pl.BlockSpec((TM, D), lambda i, f: (i, 0))
    in_specs = [row,
                pl.BlockSpec((None, D, FB * FSP), lambda i, f: (sg, 0, f)),
                pl.BlockSpec((None, D, FB * FSP), lambda i, f: (su, 0, f)),
                pl.BlockSpec((FB, None, FSP, D), lambda i, f: (f, sd, 0, 0)),
                _full_spec((1, D)), _full_spec((1, D))]
    args = [x, p384, p384, pd, lg, lb]
    if has_ple:
        p, pr, wpp = ple
        in_specs += [pl.BlockSpec((TM, PLE), lambda i, f: (i, 0)),
                     pl.BlockSpec((NDEV, None, 128, D), lambda i, f: (0, 3 * layer + 2, 0, 0)),
                     _full_spec((PLE, D))]
        args += [p, pr, wpp]
    hid = pl.BlockSpec((TM, FB * FSP), lambda i, f: (i, f))
    hshape = jax.ShapeDtypeStruct((T, NDEV * FSP), bf16)
    return pl.pallas_call(
        body, grid=(T // TM, NF), in_specs=in_specs, out_specs=[row, row, hid, hid],
        out_shape=[jax.ShapeDtypeStruct((T, D), f32)] * 2 + [hshape, hshape],
        scratch_shapes=[pltpu.VMEM((TM, D), f32), pltpu.VMEM((TM, D), bf16)],
        compiler_params=_cparams(("arbitrary", "arbitrary")), name=f"ffn{which + 1}_fwd")(*args)


def _ffn_bwd(z, dout, gs, us, p384, pd, lg, layer, which):
    T = z.shape[0]
    TMB = TM
    sg, su, sd = 4 * layer + 2 * which, 4 * layer + 2 * which + 1, 2 * layer + which

    def body(z_ref, do_ref, g_ref, u_ref, wg_ref, wu_ref, wd_ref, lg_ref,
             dx_ref, dg_ref, du_ref, a_ref, dy_ref, dlg_ref, dlb_ref, acc, dyb):
        i, f = pl.program_id(0), pl.program_id(1)

        @pl.when(jnp.logical_and(i == 0, f == 0))
        def _():
            dlg_ref[...] = jnp.zeros_like(dlg_ref)
            dlb_ref[...] = jnp.zeros_like(dlb_ref)

        @pl.when(f == 0)
        def _():
            dz, dlg, dlb = _ln_bwd(z_ref[...], lg_ref[...], do_ref[...])
            dlg_ref[...] += dlg
            dlb_ref[...] += dlb
            dy = (0.5 * dz).astype(bf16)
            dyb[...] = dy
            dy_ref[...] = dy
            acc[...] = ALPHA * dz

        g = g_ref[...].astype(f32)
        u = u_ref[...].astype(f32)
        da = _dot_nt(dyb[...], wd_ref[...].reshape(FB * FSP, D))
        sgm = _sigmoid(g)
        dg = (da * u * (sgm * (1.0 + g * (1.0 - sgm)))).astype(bf16)
        du = (da * (g * sgm)).astype(bf16)
        dg_ref[...] = dg
        du_ref[...] = du
        a_ref[...] = (g * sgm * u).astype(bf16)
        acc[...] += _dot_nt(dg, wg_ref[...]) + _dot_nt(du, wu_ref[...])

        @pl.when(f == NF - 1)
        def _():
            dx_ref[...] = acc[...]

    row = pl.BlockSpec((TMB, D), lambda i, f: (i, 0))
    hid = pl.BlockSpec((TMB, FB * FSP), lambda i, f: (i, f))
    vec = _full_spec((1, D))
    in_specs = [row, row, hid, hid,
                pl.BlockSpec((None, D, FB * FSP), lambda i, f: (sg, 0, f)),
                pl.BlockSpec((None, D, FB * FSP), lambda i, f: (su, 0, f)),
                pl.BlockSpec((FB, None, FSP, D), lambda i, f: (f, sd, 0, 0)),
                vec]
    args = [z, dout, gs, us, p384, p384, pd, lg]
    out_specs = [row, hid, hid, hid, row, vec, vec]
    hshape = jax.ShapeDtypeStruct((T, NDEV * FSP), bf16)
    out_shape = [jax.ShapeDtypeStruct((T, D), f32), hshape, hshape, hshape, jax.ShapeDtypeStruct((T, D), bf16),
                 jax.ShapeDtypeStruct((1, D), f32), jax.ShapeDtypeStruct((1, D), f32)]
    return pl.pallas_call(
        body, grid=(T // TMB, NF), in_specs=in_specs, out_specs=out_specs, out_shape=out_shape,
        scratch_shapes=[pltpu.VMEM((TMB, D), f32), pltpu.VMEM((TMB, D), bf16)],
        compiler_params=_cparams(("arbitrary", "arbitrary")), name=f"ffn{which + 1}_bwd")(*args)


def _ple_bwd(x, p, dy, dx_ffn, pr, wpp, layer):
    T = x.shape[0]

    def body(x_ref, p_ref, dy_ref, dxf_ref, wpg_ref, wpp_ref, dx_ref, dgp_ref, dpj_ref):
        dz = 2.0 * dy_ref[...].astype(f32)
        wpg = wpg_ref[...].reshape(D, D)
        gate = _sigmoid(_dot(x_ref[...], wpg))
        proj = _dot(p_ref[...], wpp_ref[...])
        dgp = (dz * proj * gate * (1.0 - gate)).astype(bf16)
        dgp_ref[...] = dgp
        dpj_ref[...] = (dz * gate).astype(bf16)
        dx_ref[...] = dxf_ref[...] + _dot_nt(dgp, wpg)

    row = pl.BlockSpec((TM, D), lambda i: (i, 0))
    return pl.pallas_call(
        body, grid=(T // TM,),
        in_specs=[row, pl.BlockSpec((TM, PLE), lambda i: (i, 0)), row, row,
                  pl.BlockSpec((NDEV, None, 128, D), lambda i: (0, 3 * layer + 2, 0, 0)), _full_spec((PLE, D))],
        out_specs=[row, row, row],
        out_shape=[jax.ShapeDtypeStruct((T, D), f32), jax.ShapeDtypeStruct((T, D), bf16),
                   jax.ShapeDtypeStruct((T, D), bf16)],
        compiler_params=_cparams(("arbitrary",)), name="ple_bwd")(x, p, dy, dx_ffn, pr, wpp)


def _matmul_tn(a, b, nb, name, nsub=1):
    T, M = a.shape
    N = b.shape[1]
    tk = min(T, 1024)
    nk = T // tk
    wide = nsub * nb

    def body(a_ref, b_ref, o_ref, acc):
        k = pl.program_id(1)

        @pl.when(k == 0)
        def _():
            acc[...] = jnp.zeros_like(acc)

        acc[...] += _dot_tn(a_ref[...], b_ref[...])

        @pl.when(k == nk - 1)
        def _():
            for j in range(nsub):
                o_ref[j] = acc[:, j * nb:(j + 1) * nb].astype(bf16)

    return pl.pallas_call(
        body, grid=(N // wide, nk),
        in_specs=[pl.BlockSpec((tk, M), lambda n, k: (k, 0)), pl.BlockSpec((tk, wide), lambda n, k: (k, n))],
        out_specs=pl.BlockSpec((nsub, M, nb), lambda n, k: (n, 0, 0)),
        out_shape=jax.ShapeDtypeStruct((N // nb, M, nb), bf16),
        scratch_shapes=[pltpu.VMEM((M, wide), f32)],
        compiler_params=_cparams(("arbitrary", "arbitrary")), name=name)(a, b)


def _proj_in(x, pr, pinl, ping, layer):
    T = x.shape[0]

    def body(x_ref, wr_ref, wl_ref, wg_ref, hr_ref, hl_ref, hg_ref):
        xb = x_ref[...].astype(bf16)
        hr_ref[...] = _dot(xb, wr_ref[...].reshape(D, D))
        hl_ref[...] = _dot(xb, wl_ref[...].reshape(D, 2 * LRU_W))
        hg_ref[...] = _dot(xb, wg_ref[...].reshape(D, GDN_IN))

    return pl.pallas_call(
        body, grid=(T // TM,),
        in_specs=[pl.BlockSpec((TM, D), lambda i: (i, 0)),
                  pl.BlockSpec((NDEV, None, 128, D), lambda i: (0, 3 * layer, 0, 0)),
                  pl.BlockSpec((NDEV, None, 128, 2 * LRU_W), lambda i: (0, layer, 0, 0)),
                  pl.BlockSpec((NDEV, None, 128, GDN_IN), lambda i: (0, layer, 0, 0))],
        out_specs=[pl.BlockSpec((TM, D), lambda i: (i, 0)), pl.BlockSpec((TM, 2 * LRU_W), lambda i: (i, 0)),
                   pl.BlockSpec((TM, GDN_IN), lambda i: (i, 0))],
        out_shape=[jax.ShapeDtypeStruct((T, D), f32), jax.ShapeDtypeStruct((T, 2 * LRU_W), f32),
                   jax.ShapeDtypeStruct((T, GDN_IN), f32)],
        compiler_params=_cparams(("arbitrary",)), name="proj_in")(x, pr, pinl, ping)


def _proj_in_bwd(base, dhr, dhl, dhg, pr, pinl, ping, layer):
    T = base.shape[0]

    def body(b_ref, dr_ref, dl_ref, dg_ref, wr_ref, wl_ref, wg_ref, o_ref):
        o_ref[...] = (b_ref[...] + _dot_nt(dr_ref[...], wr_ref[...].reshape(D, D))
                      + _dot_nt(dl_ref[...], wl_ref[...].reshape(D, 2 * LRU_W))
                      + _dot_nt(dg_ref[...], wg_ref[...].reshape(D, GDN_IN)))

    return pl.pallas_call(
        body, grid=(T // TM,),
        in_specs=[pl.BlockSpec((TM, D), lambda i: (i, 0)), pl.BlockSpec((TM, D), lambda i: (i, 0)),
                  pl.BlockSpec((TM, 2 * LRU_W), lambda i: (i, 0)), pl.BlockSpec((TM, GDN_IN), lambda i: (i, 0)),
                  pl.BlockSpec((NDEV, None, 128, D), lambda i: (0, 3 * layer, 0, 0)),
                  pl.BlockSpec((NDEV, None, 128, 2 * LRU_W), lambda i: (0, layer, 0, 0)),
                  pl.BlockSpec((NDEV, None, 128, GDN_IN), lambda i: (0, layer, 0, 0))],
        out_specs=pl.BlockSpec((TM, D), lambda i: (i, 0)),
        out_shape=jax.ShapeDtypeStruct((T, D), f32),
        compiler_params=_cparams(("arbitrary",)), name="proj_in_bwd")(base, dhr, dhl, dhg, pr, pinl, ping)


def _mix_out(x1, o_r, o_l, o_g, pr, lg, lb, layer):
    T = x1.shape[0]

    def body(x_ref, r_ref, l_ref, g_ref, w_ref, lg_ref, lb_ref, z_ref, o_ref):
        w = w_ref[...].reshape(D, D)
        z = (ALPHA * x_ref[...] + _dot(r_ref[...], w[0:RET_W]) + _dot(l_ref[...], w[RET_W:RET_W + LRU_W])
             + _dot(g_ref[...], w[RET_W + LRU_W:D]))
        z_ref[...] = z
        o_ref[...] = _ln_stats(z)[0] * lg_ref[...] + lb_ref[...]

    row = pl.BlockSpec((TM, D), lambda i: (i, 0))
    return pl.pallas_call(
        body, grid=(T // TM,),
        in_specs=[row, pl.BlockSpec((TM, RET_W), lambda i: (i, 0)), pl.BlockSpec((TM, LRU_W), lambda i: (i, 0)),
                  pl.BlockSpec((TM, GDN_W), lambda i: (i, 0)),
                  pl.BlockSpec((NDEV, None, 128, D), lambda i: (0, 3 * layer + 1, 0, 0)),
                  _full_spec((1, D)), _full_spec((1, D))],
        out_specs=[row, row], out_shape=[jax.ShapeDtypeStruct((T, D), f32)] * 2,
        compiler_params=_cparams(("arbitrary",)), name="mix_out")(x1, o_r, o_l, o_g, pr, lg, lb)


def _mix_out_bwd(z, dout, pr, lg, layer):
    T = z.shape[0]

    def body(z_ref, do_ref, w_ref, lg_ref, dxb_ref, dzb_ref, dr_ref, dl_ref, dg_ref, dlg_ref, dlb_ref):
        @pl.when(pl.program_id(0) == 0)
        def _():
            dlg_ref[...] = jnp.zeros_like(dlg_ref)
            dlb_ref[...] = jnp.zeros_like(dlb_ref)

        dz, dlg, dlb = _ln_bwd(z_ref[...], lg_ref[...], do_ref[...])
        dlg_ref[...] += dlg
        dlb_ref[...] += dlb
        dxb_ref[...] = ALPHA * dz
        dzb = dz.astype(bf16)
        dzb_ref[...] = dzb
        w = w_ref[...].reshape(D, D)
        dr_ref[...] = _dot_nt(dzb, w[0:RET_W])
        dl_ref[...] = _dot_nt(dzb, w[RET_W:RET_W + LRU_W])
        dg_ref[...] = _dot_nt(dzb, w[RET_W + LRU_W:D])

    row = pl.BlockSpec((TM, D), lambda i: (i, 0))
    vec = _full_spec((1, D))
    return pl.pallas_call(
        body, grid=(T // TM,),
        in_specs=[row, row, pl.BlockSpec((NDEV, None, 128, D), lambda i: (0, 3 * layer + 1, 0, 0)), vec],
        out_specs=[row, row, pl.BlockSpec((TM, RET_W), lambda i: (i, 0)), pl.BlockSpec((TM, LRU_W), lambda i: (i, 0)),
                   pl.BlockSpec((TM, GDN_W), lambda i: (i, 0)), vec, vec],
        out_shape=[jax.ShapeDtypeStruct((T, D), f32), jax.ShapeDtypeStruct((T, D), bf16),
                   jax.ShapeDtypeStruct((T, RET_W), f32), jax.ShapeDtypeStruct((T, LRU_W), f32),
                   jax.ShapeDtypeStruct((T, GDN_W), f32), jax.ShapeDtypeStruct((1, D), f32),
                   jax.ShapeDtypeStruct((1, D), f32)],
        compiler_params=_cparams(("arbitrary",)), name="mix_out_bwd")(z, dout, pr, lg)


def _loss_grad(y, target):
    T = y.shape[0]

    def body(y_ref, t_ref, dy_ref, l_ref):
        @pl.when(pl.program_id(0) == 0)
        def _():
            l_ref[...] = jnp.zeros_like(l_ref)

        e = y_ref[...] - t_ref[...]
        dy_ref[...] = e * (1.0 / D)
        l_ref[...] += 0.5 * jnp.sum(jnp.sum(e * e, -1, keepdims=True) * (1.0 / D), 0, keepdims=True)

    row = pl.BlockSpec((TM, D), lambda i: (i, 0))
    return pl.pallas_call(
        body, grid=(T // TM,), in_specs=[row, row], out_specs=[row, _full_spec((1, 1))],
        out_shape=[jax.ShapeDtypeStruct((T, D), f32), jax.ShapeDtypeStruct((1, 1), f32)],
        compiler_params=_cparams(("arbitrary",)), name="loss_grad")(y, target)


def _split_heads(x, H):
    n = x.shape[0] // CH
    parts = [x[:, h * CH:(h + 1) * CH].reshape(n, CH, CH) for h in range(H)]
    return jnp.stack(parts, axis=1).reshape(n * H, CH, CH)


def _merge_heads(ref, x, H, col0=0):
    n = x.shape[0] // H
    x4 = x.reshape(n, H, CH, CH)
    for h in range(H):
        ref[:, col0 + h * CH:col0 + (h + 1) * CH] = x4[:, h].reshape(n * CH, CH)


def _conv_fwd(ext, x, tail, w, R):
    ext[0:8, :] = tail
    ext[8:R + 8, :] = x
    y = w[3:4, :] * x
    for k in range(3):
        y = y + w[k:k + 1, :] * ext[5 + k:5 + k + R, :]
    return y


def _conv_bwd(ext, ext2, dy, dy_next, w, R):
    ext2[0:R, :] = dy
    ext2[R:R + 8, :] = dy_next
    dx = w[3:4, :] * dy
    dws = []
    for k in range(3):
        dx = dx + w[k:k + 1, :] * ext2[3 - k:3 - k + R, :]
        dws.append(jnp.sum(dy * ext[5 + k:5 + k + R, :], 0, keepdims=True))
    dws.append(jnp.sum(dy * ext[8:8 + R, :], 0, keepdims=True))
    return dx, jnp.concatenate(dws, axis=0)


def _prev_tail_spec(R, W):
    return pl.BlockSpec((8, W), lambda i: (jnp.maximum(i * (R // 8) - 1, 0), 0))


def _prev_tail_spec_rev(R, W, nb):
    return pl.BlockSpec((8, W), lambda i: (jnp.maximum((nb - 1 - i) * (R // 8) - 1, 0), 0))


def _rope_tables(positions):
    T = positions.shape[0]

    def body(p_ref, c_ref, s_ref):
        lane = lax.broadcasted_iota(jnp.int32, (TM, RET_W), 1)
        fi = (lane % 32).astype(f32)
        inv = jnp.exp(fi * (-math.log(ROPE_THETA) / 32.0))
        ang = p_ref[...].astype(f32) * inv
        c_ref[...] = jnp.cos(ang)
        s_ref[...] = jnp.where(lane % CH < 32, -jnp.sin(ang), jnp.sin(ang))

    row = pl.BlockSpec((TM, RET_W), lambda i: (i, 0))
    return pl.pallas_call(
        body, grid=(T // TM,), in_specs=[pl.BlockSpec((TM, 1), lambda i: (i, 0))], out_specs=[row, row],
        out_shape=[jax.ShapeDtypeStruct((T, RET_W), f32)] * 2,
        compiler_params=_cparams(("arbitrary",)), name="rope_tables")(positions)


def _partner(x):
    lane = lax.broadcasted_iota(jnp.int32, x.shape, 1)
    return jnp.where(lane % CH < 32, pltpu.roll(x, RET_W - 32, 1), pltpu.roll(x, 32, 1))


def _ret_consts():
    ii = lax.broadcasted_iota(jnp.int32, (CH, CH), 0).astype(f32)
    jj = lax.broadcasted_iota(jnp.int32, (CH, CH), 1).astype(f32)
    intra, cross, tail, cd = [], [], [], []
    for h in range(RET_H):
        lg = math.log1p(-(2.0 ** (-5.0 - h)))
        intra.append(jnp.exp(jnp.abs(ii - jj) * lg))
        cross.append(jnp.exp((ii + 1.0) * lg))
        tail.append(jnp.exp((CH - 1.0 - ii) * lg))
        cd.append(jnp.full((CH, CH), math.exp(CH * lg), f32))
    return jnp.stack(intra), jnp.stack(cross), jnp.stack(tail), jnp.stack(cd)


def _ret_chunk(consts, q, k, v, st):
    intra, cross, tail, cd = consts
    s = _bmm('hid,hjd->hij', q, k) * intra
    o = _bmm('hij,hje->hie', s, v) + _bmm('hid,hde->hie', q * cross, st)
    st2 = st * cd + _bmm('hjd,hje->hde', k * tail, v)
    oc = o - _rowsum(o) * (1.0 / CH)
    on = oc * lax.rsqrt(_rowsum(oc * oc) * (1.0 / CH) + 1e-5)
    return on, st2


def _ret_fwd(hr, cosw, sinw, gam):
    T = hr.shape[0]
    R = RB_RET
    nc = R // CH

    def body(h_ref, c_ref, s_ref, g_ref, o_ref, st_ref, st, wide):
        @pl.when(pl.program_id(0) == 0)
        def _():
            st[...] = jnp.zeros_like(st)

        consts = _ret_consts()
        cw, sw = c_ref[...], s_ref[...]
        q, k = h_ref[:, 0:RET_W], h_ref[:, RET_W:2 * RET_W]
        qh = _split_heads((q * cw + _partner(q) * sw) * 0.125, RET_H)
        kh = _split_heads(k * cw + _partner(k) * sw, RET_H)
        vh = _split_heads(h_ref[:, 2 * RET_W:3 * RET_W], RET_H)
        outs = []
        s_cur = st[...]
        for c in range(nc):
            sl = slice(c * RET_H, (c + 1) * RET_H)
            st_ref[c] = s_cur
            on, s_cur = _ret_chunk(consts, qh[sl], kh[sl], vh[sl], s_cur)
            outs.append(on)
        st[...] = s_cur
        _merge_heads(wide, jnp.concatenate(outs, axis=0), RET_H)
        o_ref[...] = wide[...] * g_ref[...] * _silu(h_ref[:, 3 * RET_W:4 * RET_W])

    blk = pl.BlockSpec((R, RET_W), lambda i: (i, 0))
    return pl.pallas_call(
        body, grid=(T // R,),
        in_specs=[pl.BlockSpec((R, D), lambda i: (i, 0)), blk, blk, _full_spec((1, RET_W))],
        out_specs=[blk, pl.BlockSpec((nc, RET_H, CH, CH), lambda i: (i, 0, 0, 0))],
        out_shape=[jax.ShapeDtypeStruct((T, RET_W), f32), jax.ShapeDtypeStruct((T // CH, RET_H, CH, CH), f32)],
        scratch_shapes=[pltpu.VMEM((RET_H, CH, CH), f32), pltpu.VMEM((R, RET_W), f32)],
        compiler_params=_cparams(("arbitrary",)), name="ret_fwd")(hr, cosw, sinw, gam)


def _ret_bwd(hr, cosw, sinw, gam, states, dout):
    T = hr.shape[0]
    R = RB_RET
    nc = R // CH
    nb = T // R

    def body(h_ref, c_ref, s_ref, g_ref, st_ref, do_ref, dh_ref, dgam_ref, dst, wide):
        @pl.when(pl.program_id(0) == 0)
        def _():
            dst[...] = jnp.zeros_like(dst)
            dgam_ref[...] = jnp.zeros_like(dgam_ref)

        consts = _ret_consts()
        cw, sw = c_ref[...], s_ref[...]
        q, k = h_ref[:, 0:RET_W], h_ref[:, RET_W:2 * RET_W]
        gr = h_ref[:, 3 * RET_W:4 * RET_W]
        qh = _split_heads((q * cw + _partner(q) * sw) * 0.125, RET_H)
        kh = _split_heads(k * cw + _partner(k) * sw, RET_H)
        vh = _split_heads(h_ref[:, 2 * RET_W:3 * RET_W], RET_H)
        do = do_ref[...]
        gam = g_ref[...]
        sg = _silu(gr)
        don = _split_heads(do * gam * sg, RET_H)
        ons, dqs, dks, dvs = [None] * nc, [None] * nc, [None] * nc, [None] * nc
        ds = dst[...]
        for c in reversed(range(nc)):
            sl = slice(c * RET_H, (c + 1) * RET_H)
            (on, _), vjp = jax.vjp(functools.partial(_ret_chunk, consts), qh[sl], kh[sl], vh[sl], st_ref[c])
            dqs[c], dks[c], dvs[c], ds = vjp((don[sl], ds))
            ons[c] = on
        dst[...] = ds
        _merge_heads(wide, jnp.concatenate(ons, axis=0), RET_H)
        onw = wide[...]
        dgam_ref[...] += jnp.sum(do * onw * sg, 0, keepdims=True)
        dh_ref[:, 3 * RET_W:4 * RET_W] = do * onw * gam * _dsilu(gr)
        _merge_heads(wide, jnp.concatenate(dqs, axis=0), RET_H)
        u = wide[...] * 0.125
        dh_ref[:, 0:RET_W] = u * cw + _partner(u * sw)
        _merge_heads(wide, jnp.concatenate(dks, axis=0), RET_H)
        u = wide[...]
        dh_ref[:, RET_W:2 * RET_W] = u * cw + _partner(u * sw)
        _merge_heads(dh_ref, jnp.concatenate(dvs, axis=0), RET_H, col0=2 * RET_W)

    blk = pl.BlockSpec((R, RET_W), lambda i: (nb - 1 - i, 0))
    return pl.pallas_call(
        body, grid=(nb,),
        in_specs=[pl.BlockSpec((R, D), lambda i: (nb - 1 - i, 0)), blk, blk, _full_spec((1, RET_W)),
                  pl.BlockSpec((nc, RET_H, CH, CH), lambda i: (nb - 1 - i, 0, 0, 0)), blk],
        out_specs=[pl.BlockSpec((R, D), lambda i: (nb - 1 - i, 0)), _full_spec((1, RET_W))],
        out_shape=[jax.ShapeDtypeStruct((T, D), f32), jax.ShapeDtypeStruct((1, RET_W), f32)],
        scratch_shapes=[pltpu.VMEM((RET_H, CH, CH), f32), pltpu.VMEM((R, RET_W), f32)],
        compiler_params=_cparams(("arbitrary",)), name="ret_bwd")(hr, cosw, sinw, gam, states, dout)


def _lru_ab(xc, wa, ba, wx, bx, lam):
    r = _sigmoid(_dot(xc, wa) + ba)
    i = _sigmoid(_dot(xc, wx) + bx)
    la = 8.0 * r * (-_softplus(-lam))
    a = jnp.exp(la)
    em = jnp.tanh(la) * (jnp.exp(2.0 * la) + 1.0)
    return a, jnp.sqrt(-em) * (i * xc)


def _lru_out(h, gate):
    return h * _gelu(gate)


def _scan_fwd(a, b):
    R = a.shape[0]
    row = lax.broadcasted_iota(jnp.int32, a.shape, 0)
    d = 1
    while d < R:
        m = row >= d
        b = jnp.where(m, a * pltpu.roll(b, d, 0) + b, b)
        a = jnp.where(m, a * pltpu.roll(a, d, 0), a)
        d *= 2
    return a, b


def _scan_bwd(a, b):
    R = a.shape[0]
    row = lax.broadcasted_iota(jnp.int32, a.shape, 0)
    d = 1
    while d < R:
        m = row < R - d
        b = jnp.where(m, a * pltpu.roll(b, R - d, 0) + b, b)
        a = jnp.where(m, a * pltpu.roll(a, R - d, 0), a)
        d *= 2
    return b


def _lru_fwd(hl, cw, cb, wa, ba, wx, bx, lam):
    T = hl.shape[0]
    R = RB_LRU
    W = LRU_W

    def body(h_ref, t_ref, cw_ref, cb_ref, wa_ref, ba_ref, wx_ref, bx_ref, lam_ref, o_ref, hs_ref, carry, ext):
        first = pl.program_id(0) == 0

        @pl.when(first)
        def _():
            carry[...] = jnp.zeros_like(carry)

        tail = jnp.where(first, 0.0, t_ref[:, 0:W])
        xc = _conv_fwd(ext, h_ref[:, 0:W], tail, cw_ref[...], R) + cb_ref[...]
        a, b = _lru_ab(xc, wa_ref[...], ba_ref[...], wx_ref[...], bx_ref[...], lam_ref[...])
        ap, hloc = _scan_fwd(a, b)
        h = hloc + ap * carry[0:1, :]
        carry[...] = jnp.broadcast_to(h[R - 1:R, :], carry.shape)
        hs_ref[...] = h
        o_ref[...] = _lru_out(h, h_ref[:, W:2 * W])

    vec = _full_spec((1, W))
    blk = pl.BlockSpec((R, W), lambda i: (i, 0))
    return pl.pallas_call(
        body, grid=(T // R,),
        in_specs=[pl.BlockSpec((R, 2 * W), lambda i: (i, 0)), _prev_tail_spec(R, 2 * W), _full_spec((4, W)), vec,
                  _full_spec((W, W)), vec, _full_spec((W, W)), vec, vec],
        out_specs=[blk, blk], out_shape=[jax.ShapeDtypeStruct((T, W), f32)] * 2,
        scratch_shapes=[pltpu.VMEM((8, W), f32), pltpu.VMEM((R + 8, W), f32)],
        compiler_params=_cparams(("arbitrary",)), name="lru_fwd")(hl, hl, cw, cb, wa, ba, wx, bx, lam)


def _lru_bwd(hl, hs, cw, cb, wa, ba, wx, bx, lam, dout):
    T = hl.shape[0]
    R = RB_LRU
    W = LRU_W
    nb = T // R

    def body(h_ref, t_ref, hs_ref, hst_ref, cw_ref, cb_ref, wa_ref, ba_ref, wx_ref, bx_ref, lam_ref, do_ref,
             dh_ref, dcw_ref, dcb_ref, dwa_ref, dba_ref, dwx_ref, dbx_ref, dlam_ref, carry_g, carry_dy, ext, ext2):
        i = pl.program_id(0)
        last_blk = i == 0
        first_blk = i == nb - 1

        @pl.when(last_blk)
        def _():
            carry_g[...] = jnp.zeros_like(carry_g)
            carry_dy[...] = jnp.zeros_like(carry_dy)
            for r in (dcw_ref, dcb_ref, dwa_ref, dba_ref, dwx_ref, dbx_ref, dlam_ref):
                r[...] = jnp.zeros_like(r)

        tail = jnp.where(first_blk, 0.0, t_ref[:, 0:W])
        xc = _conv_fwd(ext, h_ref[:, 0:W], tail, cw_ref[...], R) + cb_ref[...]
        (a, _), vjp_ab = jax.vjp(_lru_ab, xc, wa_ref[...], ba_ref[...], wx_ref[...], bx_ref[...], lam_ref[...])
        hs = hs_ref[...]
        _, vjp_out = jax.vjp(_lru_out, hs, h_ref[:, W:2 * W])
        dh, dgate = vjp_out(do_ref[...])
        row = lax.broadcasted_iota(jnp.int32, (R, W), 0)
        dh = jnp.where(row == R - 1, dh + carry_g[0:1, :], dh)
        a_up = jnp.where(row == R - 1, 0.0, pltpu.roll(a, R - 1, 0))
        g = _scan_bwd(a_up, dh)
        carry_g[...] = jnp.broadcast_to(a[0:1, :] * g[0:1, :], carry_g.shape)
        hprev0 = jnp.where(first_blk, 0.0, hst_ref[7:8, :])
        hprev = jnp.where(row == 0, hprev0, pltpu.roll(hs, 1, 0))
        dxc, dwa, dba, dwx, dbx, dlam = vjp_ab((g * hprev, g))
        dwa_ref[...] += dwa
        dba_ref[...] += dba
        dwx_ref[...] += dwx
        dbx_ref[...] += dbx
        dlam_ref[...] += dlam
        dcb_ref[...] += jnp.sum(dxc, 0, keepdims=True)
        dx, dcw = _conv_bwd(ext, ext2, dxc, carry_dy[...], cw_ref[...], R)
        carry_dy[...] = dxc[0:8, :]
        dcw_ref[...] += dcw
        dh_ref[:, 0:W] = dx
        dh_ref[:, W:2 * W] = dgate

    vec = _full_spec((1, W))
    mat = _full_spec((W, W))
    blk = pl.BlockSpec((R, W), lambda i: (nb - 1 - i, 0))
    blk2 = pl.BlockSpec((R, 2 * W), lambda i: (nb - 1 - i, 0))
    return pl.pallas_call(
        body, grid=(nb,),
        in_specs=[blk2, _prev_tail_spec_rev(R, 2 * W, nb), blk, _prev_tail_spec_rev(R, W, nb), _full_spec((4, W)), vec,
                  mat, vec, mat, vec, vec, blk],
        out_specs=[blk2, _full_spec((4, W)), vec, mat, vec, mat, vec, vec],
        out_shape=[jax.ShapeDtypeStruct((T, 2 * W), f32), jax.ShapeDtypeStruct((4, W), f32),
                   jax.ShapeDtypeStruct((1, W), f32), jax.ShapeDtypeStruct((W, W), f32),
                   jax.ShapeDtypeStruct((1, W), f32), jax.ShapeDtypeStruct((W, W), f32),
                   jax.ShapeDtypeStruct((1, W), f32), jax.ShapeDtypeStruct((1, W), f32)],
        scratch_shapes=[pltpu.VMEM((8, W), f32), pltpu.VMEM((8, W), f32), pltpu.VMEM((R + 8, W), f32),
                        pltpu.VMEM((R + 8, W), f32)],
        compiler_params=_cparams(("arbitrary",)), name="lru_bwd")(hl, hl, hs, hs, cw, cb, wa, ba, wx, bx, lam, dout)


def _gdn_local(inverse, qs, ks, vs, gc, bb):
    B = qs.shape[0]
    ii = lax.broadcasted_iota(jnp.int32, (B, CH, CH), 1)
    jj = lax.broadcasted_iota(jnp.int32, (B, CH, CH), 2)
    q = qs * lax.rsqrt(_rowsum(qs * qs) + 1e-6)
    k = ks * lax.rsqrt(_rowsum(ks * ks) + 1e-6)
    gct = jnp.swapaxes(gc, 1, 2)
    decay = jnp.where(ii >= jj, jnp.exp(jnp.minimum(gc - gct, 0.0)), 0.0)
    kk = _bmm('bid,bjd->bij', k, k)
    inv = inverse(-jnp.where(ii > jj, bb * kk * decay, 0.0))
    egc = jnp.exp(gc)
    u = _bmm3('bij,bje->bie', inv, vs * bb)
    w = _bmm3('bij,bje->bie', inv, k * (bb * egc))
    qk = _bmm('bid,bjd->bij', q, k) * (0.125 * decay)
    glast = gc[:, CH - 1:CH, :]
    return u, w, qk, q * (0.125 * egc), k * jnp.exp(glast - gc), jnp.exp(jnp.broadcast_to(glast, gc.shape))


def _gdn_step(st, u, w, qk, qd, kt, egl, z, gn):
    vnew = u - _bmm('hcd,hde->hce', w, st)
    o = _bmm('hcd,hde->hce', qd, st) + _bmm('hij,hje->hie', qk, vnew)
    st2 = st * egl + _bmm('hcd,hce->hde', kt, vnew)
    out = o * lax.rsqrt(_rowsum(o * o) * (1.0 / CH) + 1e-6) * gn * _silu(z)
    return out, st2


def _gdn_scalars(ab, alog, dtb):
    sp = _softplus(ab + dtb)
    return -jnp.exp(alog) * sp, _sigmoid(ab)


def _bcast_heads(blk, lane0, H):
    R = blk.shape[0]
    n = R // CH
    parts = [jnp.broadcast_to(blk[:, lane0 + h:lane0 + h + 1], (R, CH)).reshape(n, CH, CH) for h in range(H)]
    return jnp.stack(parts, axis=1).reshape(n * H, CH, CH)


def _unbcast_heads(x, lane0, H):
    n = x.shape[0] // H
    R = n * CH
    s = jnp.sum(x, axis=2, keepdims=True).reshape(n, H, CH, 1)
    lane = lax.broadcasted_iota(jnp.int32, (R, 128), 1)
    acc = jnp.zeros((R, 128), f32)
    for h in range(H):
        acc = acc + jnp.where(lane == lane0 + h, jnp.broadcast_to(s[:, h].reshape(R, 1), (R, 128)), 0.0)
    return acc


def _gdn_fwd(hg, cw, alog, dtb, gn):
    T = hg.shape[0]
    R = RB_GDN
    nc = R // CH
    W3 = 3 * GDN_W
    H = GDN_H

    def body(h_ref, t_ref, cw_ref, al_ref, dt_ref, gn_ref, o_ref, st_ref, inv_ref, st, ext):
        first = pl.program_id(0) == 0

        @pl.when(first)
        def _():
            st[...] = jnp.zeros_like(st)

        def inverse(m):
            inv = _neumann_inv(m)
            inv_ref[...] = inv
            return inv

        tail = jnp.where(first, 0.0, t_ref[:, 0:W3])
        y = _silu(_conv_fwd(ext, h_ref[:, 0:W3], tail, cw_ref[...], R))
        qs, ks, vs = (_split_heads(y[:, j * GDN_W:(j + 1) * GDN_W], H) for j in range(3))
        zh = _split_heads(h_ref[:, W3:W3 + GDN_W], H)
        g, beta = _gdn_scalars(h_ref[:, W3 + GDN_W:GDN_IN], al_ref[...], dt_ref[...])
        loc = _gdn_local(inverse, qs, ks, vs, _bcast_heads(_chunk_cumsum(g), 0, H), _bcast_heads(beta, H, H))
        gnv = gn_ref[...]
        outs = []
        s_cur = st[...]
        for c in range(nc):
            sl = slice(c * H, (c + 1) * H)
            st_ref[c] = s_cur
            out, s_cur = _gdn_step(s_cur, *(t[sl] for t in loc), zh[sl], gnv)
            outs.append(out)
        st[...] = s_cur
        _merge_heads(o_ref, jnp.concatenate(outs, axis=0), H)

    return pl.pallas_call(
        body, grid=(T // R,),
        in_specs=[pl.BlockSpec((R, GDN_IN), lambda i: (i, 0)), _prev_tail_spec(R, GDN_IN), _full_spec((4, W3)),
                  _full_spec((1, 128)), _full_spec((1, 128)), _full_spec((1, CH))],
        out_specs=[pl.BlockSpec((R, GDN_W), lambda i: (i, 0)), pl.BlockSpec((nc, H, CH, CH), lambda i: (i, 0, 0, 0)),
                   pl.BlockSpec((nc * H, CH, CH), lambda i: (i, 0, 0))],
        out_shape=[jax.ShapeDtypeStruct((T, GDN_W), f32), jax.ShapeDtypeStruct((T // CH, H, CH, CH), f32),
                   jax.ShapeDtypeStruct((T // CH * H, CH, CH), f32)],
        scratch_shapes=[pltpu.VMEM((H, CH, CH), f32), pltpu.VMEM((R + 8, W3), f32)],
        compiler_params=_cparams(("arbitrary",)), name="gdn_fwd")(hg, hg, cw, alog, dtb, gn)


def _gdn_bwd(hg, cw, alog, dtb, gn, states, invs, dout):
    T = hg.shape[0]
    R = RB_GDN
    nc = R // CH
    nb = T // R
    W3 = 3 * GDN_W
    H = GDN_H

    def body(h_ref, t_ref, cw_ref, al_ref, dt_ref, gn_ref, st_ref, inv_ref, do_ref,
             dh_ref, dcw_ref, dal_ref, ddt_ref, dgn_ref, dst, carry_dy, ext, ext2, wide):
        i = pl.program_id(0)
        first_blk = i == nb - 1

        @pl.when(i == 0)
        def _():
            dst[...] = jnp.zeros_like(dst)
            carry_dy[...] = jnp.zeros_like(carry_dy)
            for r in (dcw_ref, dal_ref, ddt_ref, dgn_ref):
                r[...] = jnp.zeros_like(r)

        tail = jnp.where(first_blk, 0.0, t_ref[:, 0:W3])
        ypre = _conv_fwd(ext, h_ref[:, 0:W3], tail, cw_ref[...], R)
        y = _silu(ypre)
        qs, ks, vs = (_split_heads(y[:, j * GDN_W:(j + 1) * GDN_W], H) for j in range(3))
        zh = _split_heads(h_ref[:, W3:W3 + GDN_W], H)
        ab = h_ref[:, W3 + GDN_W:GDN_IN]
        alog, dtb = al_ref[...], dt_ref[...]
        g, beta = _gdn_scalars(ab, alog, dtb)
        kept = inv_ref[...]
        loc, vjp_loc = jax.vjp(functools.partial(_gdn_local, lambda m: _known_inv(m, kept)), qs, ks, vs,
                               _bcast_heads(_chunk_cumsum(g), 0, H), _bcast_heads(beta, H, H))
        doh = _split_heads(do_ref[...], H)
        gnv = gn_ref[...]
        dloc = [[None] * nc for _ in range(6)]
        dzs = [None] * nc
        ds = dst[...]
        dgn = jnp.zeros((1, CH), f32)
        for c in reversed(range(nc)):
            sl = slice(c * H, (c + 1) * H)
            _, vjp = jax.vjp(_gdn_step, st_ref[c], *(t[sl] for t in loc), zh[sl], gnv)
            grads = vjp((doh[sl], ds))
            ds = grads[0]
            for j in range(6):
                dloc[j][c] = grads[1 + j]
            dzs[c] = grads[7]
            dgn = dgn + grads[8]
        dst[...] = ds
        dgn_ref[...] += dgn
        dqs, dks, dvs, dgb, dbb = vjp_loc(tuple(jnp.concatenate(d, axis=0) for d in dloc))
        lane = lax.broadcasted_iota(jnp.int32, (R, 128), 1)
        dg = _chunk_cumsum(_unbcast_heads(dgb, 0, H), reverse=True)
        dbeta = _unbcast_heads(dbb, H, H)
        da = dg * (-jnp.exp(alog)) * _sigmoid(ab + dtb)
        dh_ref[:, W3 + GDN_W:GDN_IN] = jnp.where(lane < H, da, dbeta * beta * (1.0 - beta))
        ddt_ref[...] += jnp.sum(jnp.where(lane < H, da, 0.0), 0, keepdims=True)
        dal_ref[...] += jnp.sum(jnp.where(lane < H, dg * g, 0.0), 0, keepdims=True)
        _merge_heads(dh_ref, jnp.concatenate(dzs, axis=0), H, col0=W3)
        for j, dpart in enumerate((dqs, dks, dvs)):
            _merge_heads(wide, dpart, H, col0=j * GDN_W)
        dy = wide[...] * _dsilu(ypre)
        dx, dcw = _conv_bwd(ext, ext2, dy, carry_dy[...], cw_ref[...], R)
        carry_dy[...] = dy[0:8, :]
        dcw_ref[...] += dcw
        dh_ref[:, 0:W3] = dx

    blk = pl.BlockSpec((R, GDN_IN), lambda i: (nb - 1 - i, 0))
    return pl.pallas_call(
        body, grid=(nb,),
        in_specs=[blk, _prev_tail_spec_rev(R, GDN_IN, nb), _full_spec((4, W3)), _full_spec((1, 128)),
                  _full_spec((1, 128)), _full_spec((1, CH)),
                  pl.BlockSpec((nc, H, CH, CH), lambda i: (nb - 1 - i, 0, 0, 0)),
                  pl.BlockSpec((nc * H, CH, CH), lambda i: (nb - 1 - i, 0, 0)),
                  pl.BlockSpec((R, GDN_W), lambda i: (nb - 1 - i, 0))],
        out_specs=[blk, _full_spec((4, W3)), _full_spec((1, 128)), _full_spec((1, 128)), _full_spec((1, CH))],
        out_shape=[jax.ShapeDtypeStruct((T, GDN_IN), f32), jax.ShapeDtypeStruct((4, W3), f32),
                   jax.ShapeDtypeStruct((1, 128), f32), jax.ShapeDtypeStruct((1, 128), f32),
                   jax.ShapeDtypeStruct((1, CH), f32)],
        scratch_shapes=[pltpu.VMEM((H, CH, CH), f32), pltpu.VMEM((8, W3), f32), pltpu.VMEM((R + 8, W3), f32),
                        pltpu.VMEM((R + 8, W3), f32), pltpu.VMEM((R, W3), f32)],
        compiler_params=_cparams(("arbitrary",)), name="gdn_bwd")(hg, hg, cw, alog, dtb, gn, states, invs, dout)


def _block_diag(w):
    out = jnp.zeros((LRU_W, LRU_W), w.dtype)
    for g in range(w.shape[0]):
        out = lax.dynamic_update_slice(out, w[g], (g * CH, g * CH))
    return out


def _block_diag_t(w):
    return jnp.stack([w[g * CH:(g + 1) * CH, g * CH:(g + 1) * CH] for g in range(LRU_W // CH)])


def _pad_lanes(v, n=128):
    return jnp.pad(v, (0, n - v.shape[0]))[None, :]


def _local_step(x, p, positions, target, fetch, emit, sm):
    cosw, sinw = _rope_tables(positions)
    saved = []
    h = x
    for l in range(DEPTH):
        v = lambda n: sm[n][l][None, :]
        G, tok = fetch(l, h)
        pd, pr, pinl, ping, wpp = G['pd'], G['pr'], G['pinl'], G['ping'], G['wpp']
        p384 = jnp.transpose(G['p384'], (1, 2, 0, 3)).reshape(-1, D, NDEV * FSP)
        wts = (p384, pd, pr, pinl, ping, wpp)
        z1, x1, g1, u1 = _ffn_fwd(h, p384, pd, v('ln_ffn1_g') + tok, v('ln_ffn1_b'), 0, 0)
        hr, hl, hg = _proj_in(x1, pr, pinl, ping, 0)
        o_r, rst = _ret_fwd(hr, cosw, sinw, v('ret_norm_g'))
        lru_args = (sm['lru_conv_w'][l], v('lru_conv_b'), _block_diag(sm['lru_w_a'][l]), v('lru_b_a'),
                    _block_diag(sm['lru_w_x'][l]), v('lru_b_x'), v('lru_lambda'))
        o_l, hs = _lru_fwd(hl, *lru_args)
        gdn_args = (sm['gdn_conv_w'][l], _pad_lanes(sm['gdn_a_log'][l]), _pad_lanes(sm['gdn_dt_bias'][l]),
                    v('gdn_norm_g'))
        o_g, *gst = _gdn_fwd(hg, *gdn_args)
        z2, x2 = _mix_out(x1, o_r, o_l, o_g, pr, v('ln_mix_g'), v('ln_mix_b'), 0)
        z3, x3, g2, u2 = _ffn_fwd(x2, p384, pd, v('ln_ffn2_g'), v('ln_ffn2_b'), 0, 1, ple=(p[l], pr, wpp))
        saved.append((h, z1, x1, hr, hl, hg, o_r, rst, o_l, hs, lru_args, o_g, gst, gdn_args, z2, x2, z3,
                      g1, u1, g2, u2, wts))
        h = x3
    d, loss = _loss_grad(h, target)

    small = {n: [None] * DEPTH for n in SMALL}
    tok = 0.0
    for l in reversed(range(DEPTH)):
        (x0, z1, x1, hr, hl, hg, o_r, rst, o_l, hs, lru_args, o_g, gst, gdn_args, z2, x2, z3,
         g1, u1, g2, u2, wts) = saved[l]
        p384, pd, pr, pinl, ping, wpp = wts
        v = lambda n: sm[n][l][None, :]
        d2, dg2, du2, a2, dy2, small['ln_ffn2_g'][l], small['ln_ffn2_b'][l] = _ffn_bwd(
            z3, d, g2, u2, p384, pd, v('ln_ffn2_g') + tok, 0, 1)
        d2, dgp, dpj = _ple_bwd(x2, p[l], dy2, d2, pr, wpp, 0)
        dxb, dzb, do_r, do_l, do_g, small['ln_mix_g'][l], small['ln_mix_b'][l] = _mix_out_bwd(
            z2, d2, pr, v('ln_mix_g'), 0)
        dhr, small['ret_norm_g'][l] = _ret_bwd(hr, cosw, sinw, v('ret_norm_g'), rst, do_r)
        (dhl, small['lru_conv_w'][l], small['lru_conv_b'][l], dwa, small['lru_b_a'][l], dwx, small['lru_b_x'][l],
         small['lru_lambda'][l]) = _lru_bwd(hl, hs, *lru_args, do_l)
        small['lru_w_a'][l], small['lru_w_x'][l] = _block_diag_t(dwa), _block_diag_t(dwx)
        dhg, small['gdn_conv_w'][l], dal, ddt, small['gdn_norm_g'][l] = _gdn_bwd(hg, *gdn_args, *gst, do_g)
        small['gdn_a_log'][l], small['gdn_dt_bias'][l] = dal[:, 0:GDN_H], ddt[:, 0:GDN_H]
        d1 = _proj_in_bwd(dxb, dhr, dhl, dhg, pr, pinl, ping, 0)
        d, dg1, du1, a1, dy1, small['ln_ffn1_g'][l], small['ln_ffn1_b'][l] = _ffn_bwd(
            z1, d1, g1, u1, p384, pd, v('ln_ffn1_g'), 0, 0)
        rows = lambda m: m.reshape(NDEV, m.shape[1] // NDEV, m.shape[2])
        dwo = jnp.concatenate([_matmul_tn(o_r, dzb, D, "dw_out_r"), _matmul_tn(o_l, dzb, D, "dw_out_l"),
                               _matmul_tn(o_g, dzb, D, "dw_out_g")], axis=1)
        tok = emit(l, {
            'p384': jnp.stack([_matmul_tn(x0, dg1, FSP, "dw_gate", FB), _matmul_tn(x0, du1, FSP, "dw_up", FB),
                               _matmul_tn(x2, dg2, FSP, "dw_gate", FB), _matmul_tn(x2, du2, FSP, "dw_up", FB)],
                              axis=1),
            'pd': jnp.stack([rows(_matmul_tn(a1, dy1, D, "dw_down")), rows(_matmul_tn(a2, dy2, D, "dw_down"))],
                            axis=1),
            'pr': jnp.stack([rows(_matmul_tn(x1, dhr, D, "dw_in_r")), rows(dwo),
                             rows(_matmul_tn(x2, dgp, D, "dw_ple_gate"))], axis=1),
            'pinl': rows(_matmul_tn(x1, dhl, 2 * LRU_W, "dw_in_l"))[:, None],
            'ping': rows(_matmul_tn(x1, dhg, GDN_IN, "dw_in_g"))[:, None],
            'ppp': jnp.transpose(_matmul_tn(p[l], dpj, D, "dw_ple_proj").reshape(PLE, NDEV, 128), (1, 0, 2))[:, None]})
    small = {n: jnp.stack([g.reshape(sm[n].shape[1:]) for g in gs]) for n, gs in small.items()}
    return loss, d, small


def _pack_big(ws, dtype=bf16):
    padc = lambda a, n: jnp.pad(a, ((0, 0), (0, 0), (0, n - a.shape[2])))
    padr = lambda a, n: jnp.pad(a, ((0, 0), (0, n - a.shape[1]), (0, 0)))
    per_layer = lambda arrs: jnp.stack(arrs, axis=1).reshape((-1,) + arrs[0].shape[1:])
    w_in = ws['w_in']
    out = {
        'p384': per_layer([padc(ws[n], FSP) for n in ('ffn1_w_gate', 'ffn1_w_up', 'ffn2_w_gate', 'ffn2_w_up')]),
        'pd': per_layer([padr(ws[n], FSP) for n in ('ffn1_w_down', 'ffn2_w_down')]),
        'pr': per_layer([w_in[:, :, 0:D], ws['w_out'], ws['ple_w_gate']]),
        'pinl': w_in[:, :, D:D + 2 * LRU_W],
        'ping': padc(w_in[:, :, D + 2 * LRU_W:D_IN], GDN_IN),
        'ppp': ws['ple_w_proj'],
    }
    return {k: a.astype(dtype) for k, a in out.items()}


def _exchange(arrays, scatter, name):
    n = len(arrays)

    def body(*refs):
        ins, outs = refs[:n], refs[n:2 * n]
        send_sems, recv_sems, local_sems = refs[2 * n:]
        x, y, c = lax.axis_index("x"), lax.axis_index("y"), lax.axis_index("c")
        me = 4 * x + 2 * y + c
        copies = []
        for i in range(n):
            src = ins[i].at[me] if scatter[i] else ins[i]
            cp = pltpu.make_async_copy(src, outs[i].at[me], local_sems.at[i])
            cp.start()
            copies.append(cp)
        sends = []
        for j in range(1, NDEV):
            peer = (me + j) % NDEV
            pid = (peer // 4, (peer // 2) % 2, peer % 2)
            for i in range(n):
                src = ins[i].at[peer] if scatter[i] else ins[i]
                cp = pltpu.make_async_remote_copy(
                    src_ref=src, dst_ref=outs[i].at[me], send_sem=send_sems.at[i, j - 1],
                    recv_sem=recv_sems.at[i, j - 1], device_id=pid, device_id_type=pl.DeviceIdType.MESH)
                cp.start()
                sends.append(cp)
        for j in range(1, NDEV):
            source = (me + NDEV - j) % NDEV
            sid = (source // 4, (source // 2) % 2, source % 2)
            for i in range(n):
                src = ins[i].at[me] if scatter[i] else ins[i]
                pltpu.make_async_remote_copy(
                    src_ref=src, dst_ref=outs[i].at[source], send_sem=send_sems.at[i, j - 1],
                    recv_sem=recv_sems.at[i, j - 1], device_id=sid, device_id_type=pl.DeviceIdType.MESH).wait_recv()
        for cp in sends:
            cp.wait_send()
        for cp in copies:
            cp.wait()

    hbm = pl.BlockSpec(memory_space=pltpu.HBM)
    out_shape = [jax.ShapeDtypeStruct(a.shape if s else (NDEV,) + a.shape, a.dtype) for a, s in zip(arrays, scatter)]
    return pl.pallas_call(
        body, in_specs=[hbm] * n, out_specs=[hbm] * n, out_shape=out_shape,
        scratch_shapes=[pltpu.SemaphoreType.DMA((n, NDEV - 1)), pltpu.SemaphoreType.DMA((n, NDEV - 1)),
                        pltpu.SemaphoreType.DMA((n,))],
        compiler_params=pltpu.CompilerParams(has_side_effects=True), name=name)(*arrays)


def _gather_two_level(arrays, name):
    n = len(arrays)

    def body(*refs):
        ins, outs = refs[:n], refs[n:2 * n]
        send_sems, recv_sems, local_sems = refs[2 * n:]
        x, y, c = lax.axis_index("x"), lax.axis_index("y"), lax.axis_index("c")
        me, sibling = (x, y, c), (x, y, 1 - c)
        chips = [(1 - x, y), (x, 1 - y), (1 - x, 1 - y)]
        slot = lambda d: 4 * d[0] + 2 * d[1] + d[2]

        def copy(i, k, block, to, src=None):
            return pltpu.make_async_remote_copy(
                src_ref=outs[i].at[slot(block)] if src is None else src, dst_ref=outs[i].at[slot(block)],
                send_sem=send_sems.at[i, k], recv_sem=recv_sems.at[i, k], device_id=to,
                device_id_type=pl.DeviceIdType.MESH)

        mine, first, passed = [], [], []
        for i in range(n):
            cp = pltpu.make_async_copy(ins[i], outs[i].at[slot(me)], local_sems.at[i])
            cp.start()
            mine.append(cp)
            first.append(copy(i, 0, me, sibling, src=ins[i]))
            first += [copy(i, 1 + j, me, (*chip, c), src=ins[i]) for j, chip in enumerate(chips)]
        for cp in first:
            cp.start()
        for i in range(n):
            for j, chip in enumerate(chips):
                copy(i, 1 + j, (*chip, c), me).wait_recv()
                cp = copy(i, 4 + j, (*chip, c), sibling)
                cp.start()
                passed.append(cp)
        for i in range(n):
            copy(i, 0, sibling, me).wait_recv()
            for j, chip in enumerate(chips):
                copy(i, 4 + j, (*chip, 1 - c), me).wait_recv()
        for cp in first + passed:
            cp.wait_send()
        for cp in mine:
            cp.wait()

    hbm = pl.BlockSpec(memory_space=pltpu.HBM)
    return pl.pallas_call(
        body, in_specs=[hbm] * n, out_specs=[hbm] * n,
        out_shape=[jax.ShapeDtypeStruct((NDEV,) + a.shape, a.dtype) for a in arrays],
        scratch_shapes=[pltpu.SemaphoreType.DMA((n, NDEV - 1)), pltpu.SemaphoreType.DMA((n, NDEV - 1)),
                        pltpu.SemaphoreType.DMA((n,))],
        compiler_params=pltpu.CompilerParams(has_side_effects=True), name=name)(*arrays)


def _scatter_pairs(arrays, name):
    n = len(arrays)

    def body(*refs):
        ins, gots = refs[:n], refs[n:2 * n]
        send_sems, recv_sems = refs[2 * n:]
        x, y, c = lax.axis_index("x"), lax.axis_index("y"), lax.axis_index("c")
        sends = []
        for i in range(n):
            for q in range(4):
                cp = pltpu.make_async_remote_copy(
                    src_ref=ins[i].at[2 * q + 1 - c], dst_ref=gots[i].at[q], send_sem=send_sems.at[i, q],
                    recv_sem=recv_sems.at[i, q], device_id=(x, y, 1 - c), device_id_type=pl.DeviceIdType.MESH)
                cp.start()
                sends.append(cp)
        for cp in sends:
            cp.wait_recv()
        for cp in sends:
            cp.wait_send()

    hbm = pl.BlockSpec(memory_space=pltpu.HBM)
    return pl.pallas_call(
        body, in_specs=[hbm] * n, out_specs=[hbm] * n,
        out_shape=[jax.ShapeDtypeStruct((4,) + a.shape[1:], a.dtype) for a in arrays],
        scratch_shapes=[pltpu.SemaphoreType.DMA((n, 4)), pltpu.SemaphoreType.DMA((n, 4))],
        compiler_params=pltpu.CompilerParams(has_side_effects=True), name=name)(*arrays)


def _pair_sum(own, got, name):
    def body(a_ref, b_ref, o_ref):
        o_ref[...] = (a_ref[...].astype(f32) + b_ref[...].astype(f32)).astype(bf16)

    spec = pl.BlockSpec((None, None) + own.shape[2:], lambda q, s: (q, s, 0, 0))
    return pl.pallas_call(
        body, grid=own.shape[:2], in_specs=[spec, spec], out_specs=spec,
        out_shape=jax.ShapeDtypeStruct(own.shape, bf16),
        compiler_params=_cparams(("arbitrary", "arbitrary")), name=name)(own, got)


def _scatter_chips(arrays, name):
    n = len(arrays)

    def body(*refs):
        ins, outs = refs[:n], refs[n:2 * n]
        send_sems, recv_sems, local_sems = refs[2 * n:]
        x, y, c = lax.axis_index("x"), lax.axis_index("y"), lax.axis_index("c")
        chip = 2 * x + y
        keeps, sends = [], []
        for i in range(n):
            cp = pltpu.make_async_copy(ins[i].at[chip], outs[i].at[chip], local_sems.at[i])
            cp.start()
            keeps.append(cp)
        for j in range(1, 4):
            peer = (chip + j) % 4
            for i in range(n):
                cp = pltpu.make_async_remote_copy(
                    src_ref=ins[i].at[peer], dst_ref=outs[i].at[chip], send_sem=send_sems.at[i, j - 1],
                    recv_sem=recv_sems.at[i, j - 1], device_id=(peer // 2, peer % 2, c),
                    device_id_type=pl.DeviceIdType.MESH)
                cp.start()
                sends.append(cp)
        for j in range(1, 4):
            source = (chip + 4 - j) % 4
            for i in range(n):
                pltpu.make_async_remote_copy(
                    src_ref=ins[i].at[chip], dst_ref=outs[i].at[source], send_sem=send_sems.at[i, j - 1],
                    recv_sem=recv_sems.at[i, j - 1], device_id=(source // 2, source % 2, c),
                    device_id_type=pl.DeviceIdType.MESH).wait_recv()
        for cp in sends:
            cp.wait_send()
        for cp in keeps:
            cp.wait()

    hbm = pl.BlockSpec(memory_space=pltpu.HBM)
    return pl.pallas_call(
        body, in_specs=[hbm] * n, out_specs=[hbm] * n,
        out_shape=[jax.ShapeDtypeStruct(a.shape, a.dtype) for a in arrays],
        scratch_shapes=[pltpu.SemaphoreType.DMA((n, 3)), pltpu.SemaphoreType.DMA((n, 3)),
                        pltpu.SemaphoreType.DMA((n,))],
        compiler_params=pltpu.CompilerParams(has_side_effects=True), name=name)(*arrays)


def _gather_plan(srcs, lands, x, y, c):
    me = 4 * x + 2 * y + c
    sends, arrivals = [], []
    for j in range(1, NDEV):
        peer, source = (me + j) % NDEV, (me + NDEV - j) % NDEV
        for i in range(len(srcs)):
            k = i * (NDEV - 1) + j - 1
            sends.append((srcs[i], lands[i].at[me], (peer // 4, (peer // 2) % 2, peer % 2), k))
            arrivals.append((srcs[i], lands[i].at[source], (source // 4, (source // 2) % 2, source % 2), k))
    return sends, arrivals


def _chips_plan(srcs, lands, x, y, c):
    chip = 2 * x + y
    sends, arrivals = [], []
    for j in range(1, 4):
        peer, source = (chip + j) % 4, (chip + 4 - j) % 4
        for i in range(len(srcs)):
            k = i * 3 + j - 1
            sends.append((srcs[i].at[peer], lands[i].at[chip], (peer // 2, peer % 2, c), k))
            arrivals.append((srcs[i].at[chip], lands[i].at[source], (source // 2, source % 2, c), k))
    return sends, arrivals


def _remote(entry, send_sems, recv_sems):
    src, dst, dev, k = entry
    return pltpu.make_async_remote_copy(src_ref=src, dst_ref=dst, send_sem=send_sems.at[k], recv_sem=recv_sems.at[k],
                                        device_id=dev, device_id_type=pl.DeviceIdType.MESH)


_HBM = pl.BlockSpec(memory_space=pltpu.HBM)
_SEM = pl.BlockSpec(memory_space=pltpu.SEMAPHORE)


def _split_start(arrays, land_shapes, plan, npeer, name):
    n = len(arrays)

    def body(*refs):
        srcs, lands = refs[:n], refs[n:2 * n]
        send_sems, recv_sems, token = refs[2 * n], refs[2 * n + 1], refs[-1]
        sends, _ = plan(srcs, lands, lax.axis_index("x"), lax.axis_index("y"), lax.axis_index("c"))
        for entry in sends:
            _remote(entry, send_sems, recv_sems).start()
        token[...] = jnp.zeros_like(token)

    lands = [lax.empty(s, a.dtype) for s, a in zip(land_shapes, arrays)]
    thru = [pltpu.HBM(a.shape, a.dtype) for a in arrays + lands]
    out = pl.pallas_call(
        body, name=name, in_specs=[_HBM] * (2 * n),
        out_specs=(_SEM, _SEM, *([_HBM] * (2 * n)), pl.BlockSpec(memory_space=pltpu.VMEM)),
        out_shape=(pltpu.SemaphoreType.DMA((n * npeer,)), pltpu.SemaphoreType.DMA((n * npeer,)), *thru,
                   jax.ShapeDtypeStruct((8, 128), f32)),
        input_output_aliases={i: 2 + i for i in range(2 * n)},
        compiler_params=pltpu.CompilerParams(has_side_effects=pltpu.SideEffectType.DATAFLOW_SIDE_EFFECTING),
    )(*[pltpu.with_memory_space_constraint(a, pltpu.HBM) for a in arrays + lands])
    return out[0], out[1], list(out[2:2 + n]), list(out[2 + n:2 + 2 * n]), out[-1]


def _split_wait(send_sems, recv_sems, srcs, lands, after, plan, name):
    n = len(srcs)

    def body(*refs):
        s_refs, l_refs = refs[:n], refs[n:2 * n]
        ssem, rsem = refs[2 * n], refs[2 * n + 1]
        sends, arrivals = plan(s_refs, l_refs, lax.axis_index("x"), lax.axis_index("y"), lax.axis_index("c"))
        for entry in sends:
            _remote(entry, ssem, rsem).wait_send()
        for entry in arrivals:
            _remote(entry, ssem, rsem).wait_recv()

    out = pl.pallas_call(
        body, name=name, in_specs=[_HBM] * (2 * n) + [_SEM, _SEM, pl.BlockSpec(memory_space=pl.ANY)],
        out_specs=[_HBM] * (2 * n), out_shape=[pltpu.HBM(a.shape, a.dtype) for a in srcs + lands],
        input_output_aliases={i: i for i in range(2 * n)},
        compiler_params=pltpu.CompilerParams(has_side_effects=pltpu.SideEffectType.DATAFLOW_SIDE_EFFECTING),
    )(*srcs, *lands, send_sems, recv_sems, after)
    return list(out[:n]), list(out[n:])


def _adam_math(w, g, m, v):
    m2 = ADAM_B1 * m + (1.0 - ADAM_B1) * g
    v2 = ADAM_B2 * v + (1.0 - ADAM_B2) * (g * g)
    m_hat = m2 / (1.0 - ADAM_B1 ** ADAM_STEP)
    v_hat = v2 / (1.0 - ADAM_B2 ** ADAM_STEP)
    return -ADAM_LR * (m_hat / (jnp.sqrt(v_hat) + ADAM_EPS) + ADAM_WD * w), m2, v2


def _adam_big(parts, w, m, v, name):
    L, rows, cols = w.shape
    flat = [(a, slot) for layer_parts in parts for a, slot in layer_parts]
    per = len(parts[0])

    def body(*refs):
        prefs = refs[:len(flat)]
        w_ref, m_ref, v_ref, g_ref, d_ref, m2_ref, v2_ref = refs[len(flat):]
        for li in range(L):
            @pl.when(pl.program_id(0) == li)
            def _():
                c0 = 0
                for pref in prefs[li * per:(li + 1) * per]:
                    acc = pref[0].astype(f32)
                    for s in range(1, pref.shape[0]):
                        acc = acc + pref[s].astype(f32)
                    width = min(acc.shape[1], cols - c0)
                    g_ref[:, c0:c0 + width] = acc[0:rows, 0:width]
                    c0 += width

        d, m2, v2 = _adam_math(w_ref[...], g_ref[...], m_ref[...], v_ref[...])
        d_ref[...] = d
        m2_ref[...] = m2
        v2_ref[...] = v2

    wspec = pl.BlockSpec((None, rows, cols), lambda l: (l, 0, 0))
    in_specs = [pl.BlockSpec((a.shape[0], None) + a.shape[2:], functools.partial(lambda l, slot: (0, slot, 0, 0), slot=slot))
                for a, slot in flat]
    return pl.pallas_call(
        body, grid=(L,), in_specs=in_specs + [wspec] * 3, out_specs=[wspec] * 4,
        out_shape=[jax.ShapeDtypeStruct(w.shape, f32)] * 4,
        compiler_params=_cparams(("arbitrary",)), name=name)(*[a for a, _ in flat], w, m, v)


def _sum_sources(stacked):
    rows = stacked.shape[1]

    def body(s_ref, o_ref):
        acc = s_ref[0]
        for s in range(1, NDEV):
            acc = acc + s_ref[s]
        o_ref[...] = acc

    return pl.pallas_call(body, out_shape=jax.ShapeDtypeStruct((rows, 128), f32), name="sum_small_grads")(stacked)


def _adam_small(w, g, m, v):
    def body(w_ref, g_ref, m_ref, v_ref, d_ref, m2_ref, v2_ref):
        d, m2, v2 = _adam_math(w_ref[...], g_ref[...], m_ref[...], v_ref[...])
        d_ref[...] = d
        m2_ref[...] = m2
        v2_ref[...] = v2

    return pl.pallas_call(body, out_shape=[jax.ShapeDtypeStruct(w.shape, f32)] * 3, name="adam_small")(w, g, m, v)


def _pack_rows(arrs):
    flat = []
    for a in arrs:
        a = a.reshape(-1)
        flat.append(jnp.pad(a, (0, (-a.shape[0]) % 1024)))
    return jnp.concatenate(flat).reshape(-1, 128)


def _unpack_rows(packed, shapes):
    out, off = [], 0
    flat = packed.reshape(-1)
    for s in shapes:
        n = math.prod(s)
        out.append(flat[off:off + n].reshape(s))
        off += n + (-n) % 1024
    return out


def _gather_conv(gathered, shape):
    L, K, c = shape
    return jnp.transpose(gathered, (1, 2, 0, 3)).reshape(L, K, NDEV * c)


def kernel(x, p, positions, ln_ffn1_g, ln_ffn1_b, ffn1_w_gate, ffn1_w_up, ffn1_w_down, w_in, ret_norm_g, lru_conv_w, lru_conv_b, lru_w_a, lru_b_a, lru_w_x, lru_b_x, lru_lambda, gdn_conv_w, gdn_a_log, gdn_dt_bias, gdn_norm_g, w_out, ln_mix_g, ln_mix_b, ffn2_w_gate, ffn2_w_up, ffn2_w_down, ple_w_gate, ple_w_proj, ln_ffn2_g, ln_ffn2_b, loss_target, m_ln_ffn1_g, m_ln_ffn1_b, m_ffn1_w_gate, m_ffn1_w_up, m_ffn1_w_down, m_w_in, m_ret_norm_g, m_lru_conv_w, m_lru_conv_b, m_lru_w_a, m_lru_b_a, m_lru_w_x, m_lru_b_x, m_lru_lambda, m_gdn_conv_w, m_gdn_a_log, m_gdn_dt_bias, m_gdn_norm_g, m_w_out, m_ln_mix_g, m_ln_mix_b, m_ffn2_w_gate, m_ffn2_w_up, m_ffn2_w_down, m_ple_w_gate, m_ple_w_proj, m_ln_ffn2_g, m_ln_ffn2_b, v_ln_ffn1_g, v_ln_ffn1_b, v_ffn1_w_gate, v_ffn1_w_up, v_ffn1_w_down, v_w_in, v_ret_norm_g, v_lru_conv_w, v_lru_conv_b, v_lru_w_a, v_lru_b_a, v_lru_w_x, v_lru_b_x, v_lru_lambda, v_gdn_conv_w, v_gdn_a_log, v_gdn_dt_bias, v_gdn_norm_g, v_w_out, v_ln_mix_g, v_ln_mix_b, v_ffn2_w_gate, v_ffn2_w_up, v_ffn2_w_down, v_ple_w_gate, v_ple_w_proj, v_ln_ffn2_g, v_ln_ffn2_b):
    args = locals()
    W = {n: args[n] for n in WEIGHTS}
    M = {n: args['m_' + n] for n in WEIGHTS}
    V = {n: args['v_' + n] for n in WEIGHTS}
    me = 4 * lax.axis_index("x") + 2 * lax.axis_index("y") + lax.axis_index("c")

    core = lax.axis_index("c")
    chip = 2 * lax.axis_index("x") + lax.axis_index("y")

    packed = _pack_big(W)
    layer_pack = lambda l: [packed[k][l * (packed[k].shape[0] // DEPTH):(l + 1) * (packed[k].shape[0] // DEPTH)]
                            for k in PACKS]
    conv_pack = _pack_rows([W[n] for n in CONV_SHARDED])
    g0 = _gather_two_level(layer_pack(0) + [conv_pack], "gather_weights")
    g0, src1 = lax.optimization_barrier((g0, layer_pack(1)))
    gather1 = _split_start(src1, [(NDEV,) + a.shape for a in src1], _gather_plan, NDEV - 1, "gather_start")

    def as_weights(arrs):
        G = dict(zip(PACKS, arrs))
        G['wpp'] = jnp.transpose(G.pop('ppp'), (1, 2, 0, 3)).reshape(PLE, D)
        return G

    def fetch(l, after):
        if l == 0:
            return as_weights(g0[:-1]), gather1[4][0, 0]
        srcs, lands = _split_wait(gather1[0], gather1[1], gather1[2], gather1[3], after, _gather_plan, "gather_wait")
        return as_weights([lax.dynamic_update_slice_in_dim(ld, s[None], me, axis=0)
                           for s, ld in zip(srcs, lands)]), 0.0

    conv_all = g0[-1]
    sm = {n: W[n] for n in SMALL}
    conv_shards = [_unpack_rows(conv_all[s], [W[n].shape for n in CONV_SHARDED]) for s in range(NDEV)]
    for i, n in enumerate(CONV_SHARDED):
        sm[n] = _gather_conv(jnp.stack([cs[i] for cs in conv_shards]), W[n].shape)

    received = [None] * DEPTH
    scatter1 = []

    def emit(l, grads_l):
        arrs = [grads_l[k] for k in PACKS]
        gots = _scatter_pairs(arrs, "scatter_pairs")
        owns = [lax.dynamic_index_in_dim(a.reshape((4, 2) + a.shape[1:]), core, axis=1, keepdims=False) for a in arrs]
        pair = [_pair_sum(o, g, "pair_sum_" + k) for k, o, g in zip(PACKS, owns, gots)]
        if l == 0:
            received[l] = _scatter_chips(pair, "scatter_chips")
            return 0.0
        scatter1.extend(_split_start(pair, [a.shape for a in pair], _chips_plan, 3, "scatter_start"))
        return scatter1[4][0, 0]

    loss, grad_x, small = _local_step(x[0], p[:, 0], positions.reshape(-1, 1), loss_target[0], fetch, emit, sm)
    loss = lax.psum(loss[0, 0], ("x", "y", "c"))
    srcs, lands = _split_wait(scatter1[0], scatter1[1], scatter1[2], scatter1[3], grad_x, _chips_plan, "scatter_wait")
    received[1] = [lax.dynamic_update_slice_in_dim(ld, lax.dynamic_index_in_dim(s, chip, axis=0), chip, axis=0)
                   for s, ld in zip(srcs, lands)]
    R = [dict(zip(PACKS, r)) for r in received]

    small_pack = _pack_rows([small[n] for n in SMALL])
    small_all = _exchange([small_pack], [False], "gather_small_grads")[0]
    small_sum = _unpack_rows(_sum_sources(small_all), [small[n].shape for n in SMALL])
    grads, delta, new_m, new_v = {}, {}, {}, {}
    for n, g in zip(SMALL, small_sum):
        if n in CONV_SHARDED:
            c = W[n].shape[2]
            g = lax.dynamic_slice_in_dim(g, me * c, c, axis=2)
        grads[n] = g

    big_parts = {
        'ffn1_w_gate': [('p384', 0)], 'ffn1_w_up': [('p384', 1)], 'ffn2_w_gate': [('p384', 2)],
        'ffn2_w_up': [('p384', 3)], 'ffn1_w_down': [('pd', 0)], 'ffn2_w_down': [('pd', 1)],
        'w_in': [('pr', 0), ('pinl', 0), ('ping', 0)], 'w_out': [('pr', 1)], 'ple_w_gate': [('pr', 2)],
        'ple_w_proj': [('ppp', 0)],
    }
    for n in BIG:
        parts = [[(R[l][k], slot) for k, slot in big_parts[n]] for l in range(DEPTH)]
        grads[n], delta[n], new_m[n], new_v[n] = _adam_big(parts, W[n], M[n], V[n], "adam_" + n)
    shapes = [W[n].shape for n in SMALL]
    d_s, m_s, v_s = _adam_small(*[_pack_rows([src[n] for n in SMALL]) for src in (W, grads, M, V)])
    for n, dd, mm, vv in zip(SMALL, _unpack_rows(d_s, shapes), _unpack_rows(m_s, shapes), _unpack_rows(v_s, shapes)):
        delta[n], new_m[n], new_v[n] = dd, mm, vv

    return (loss, grad_x[None], *[grads[n] for n in WEIGHTS], *[delta[n] for n in WEIGHTS],
            *[new_m[n] for n in WEIGHTS], *[new_v[n] for n in WEIGHTS])
```

```python
import functools
import math

import jax
import jax.numpy as jnp
from jax import lax
from jax.experimental import pallas as pl
from jax.experimental.pallas import tpu as pltpu

f32 = jnp.float32
bf16 = jnp.bfloat16

NDEV = 8
DEPTH = 2
D = 1024
FS = 352
FSP = 384
FB = 2
NF = NDEV // FB
PLE = 256
CH = 64
RET_H, GDN_H = 4, 6
RET_W, LRU_W, GDN_W = 256, 384, 384
GDN_IN = 1664
GDN_IN_REAL = 1548
D_IN = 3340
ALPHA = 4.0 ** 0.25
LN_EPS = 1e-5
ROPE_THETA = 10000.0
TM = 512
RB_RET, RB_LRU, RB_GDN = 512, 512, 256
VMEM_LIMIT = 56 * 1024 * 1024
ADAM_LR, ADAM_B1, ADAM_B2, ADAM_EPS, ADAM_WD, ADAM_STEP = 0.001, 0.9, 0.999, 1e-08, 0.01, 10

WEIGHTS = ['ln_ffn1_g', 'ln_ffn1_b', 'ffn1_w_gate', 'ffn1_w_up', 'ffn1_w_down', 'w_in', 'ret_norm_g', 'lru_conv_w',
           'lru_conv_b', 'lru_w_a', 'lru_b_a', 'lru_w_x', 'lru_b_x', 'lru_lambda', 'gdn_conv_w', 'gdn_a_log',
           'gdn_dt_bias', 'gdn_norm_g', 'w_out', 'ln_mix_g', 'ln_mix_b', 'ffn2_w_gate', 'ffn2_w_up', 'ffn2_w_down',
           'ple_w_gate', 'ple_w_proj', 'ln_ffn2_g', 'ln_ffn2_b']
BIG = ['ffn1_w_gate', 'ffn1_w_up', 'ffn1_w_down', 'w_in', 'w_out', 'ffn2_w_gate', 'ffn2_w_up', 'ffn2_w_down',
       'ple_w_gate', 'ple_w_proj']
SMALL = [n for n in WEIGHTS if n not in BIG]
PACKS = ('p384', 'pd', 'pr', 'pinl', 'ping', 'ppp')
CONV_SHARDED = {'lru_conv_w': LRU_W, 'gdn_conv_w': 3 * GDN_W}


def _cparams(sem=None):
    return pltpu.CompilerParams(dimension_semantics=sem, vmem_limit_bytes=VMEM_LIMIT)


def _sigmoid(x):
    return 1.0 / (1.0 + jnp.exp(-x))


def _silu(x):
    return x * _sigmoid(x)


def _dsilu(x):
    s = _sigmoid(x)
    return s * (1.0 + x * (1.0 - s))


def _softplus(x):
    return jnp.maximum(x, 0.0) + jnp.log(1.0 + jnp.exp(-jnp.abs(x)))


def _gelu(x):
    return 0.5 * x * (1.0 + jnp.tanh(0.7978845608028654 * (x + 0.044715 * x * x * x)))


def _dot(a, b):
    return jnp.dot(a.astype(bf16), b.astype(bf16), preferred_element_type=f32)


def _dot_nt(a, b):
    return lax.dot_general(a.astype(bf16), b.astype(bf16), (((1,), (1,)), ((), ())), preferred_element_type=f32)


def _dot_tn(a, b):
    return lax.dot_general(a.astype(bf16), b.astype(bf16), (((0,), (0,)), ((), ())), preferred_element_type=f32)


def _bmm(eq, a, b):
    return jnp.einsum(eq, a.astype(bf16), b.astype(bf16), preferred_element_type=f32)


def _split3(a):
    a1 = a.astype(bf16)
    r = a - a1.astype(f32)
    a2 = r.astype(bf16)
    return a1, a2, (r - a2.astype(f32)).astype(bf16)


def _bmm3(eq, a, b):
    a1, a2, _ = _split3(a)
    b1, b2, _ = _split3(b)
    e = lambda x, y: jnp.einsum(eq, x, y, preferred_element_type=f32)
    return e(a1, b1) + (e(a1, b2) + e(a2, b1))


def _rowsum(x):
    x1, x2, _ = _split3(x)
    ones = jnp.ones((x.shape[0], CH, CH), bf16)
    e = lambda y: jnp.einsum('bij,bjk->bik', y, ones, preferred_element_type=f32)
    return e(x1) + e(x2)


def _tri_ones(B, upper=False):
    ii = lax.broadcasted_iota(jnp.int32, (B, CH, CH), 1)
    jj = lax.broadcasted_iota(jnp.int32, (B, CH, CH), 2)
    return jnp.where((ii <= jj) if upper else (ii >= jj), 1.0, 0.0).astype(bf16)


def _cumsum_mm(t, x):
    x1, x2, x3 = _split3(x)
    e = lambda y: jnp.einsum('bij,bjk->bik', t, y, preferred_element_type=f32)
    return e(x1) + (e(x2) + e(x3))


def _chunk_cumsum(x, reverse=False):
    n = x.shape[0] // CH
    return _cumsum_mm(_tri_ones(n, upper=reverse), x.reshape(n, CH, 128)).reshape(x.shape)


@jax.custom_vjp
def _neumann_inv(m):
    ii = lax.broadcasted_iota(jnp.int32, m.shape, 1)
    jj = lax.broadcasted_iota(jnp.int32, m.shape, 2)
    inv = jnp.where(ii == jj, 1.0, 0.0).astype(f32) + m
    mp = m
    for _ in range(5):
        mp = _bmm3('bij,bjk->bik', mp, mp)
        inv = inv + _bmm3('bij,bjk->bik', inv, mp)
    return inv


def _neumann_inv_fwd(m):
    inv = _neumann_inv(m)
    return inv, inv


def _neumann_inv_bwd(inv, g):
    return (_bmm3('bij,bkj->bik', _bmm3('bji,bjk->bik', inv, g), inv),)


_neumann_inv.defvjp(_neumann_inv_fwd, _neumann_inv_bwd)


@jax.custom_vjp
def _known_inv(m, inv):
    return inv


def _known_inv_fwd(m, inv):
    return inv, inv


def _known_inv_bwd(inv, g):
    return _neumann_inv_bwd(inv, g)[0], jnp.zeros_like(inv)


_known_inv.defvjp(_known_inv_fwd, _known_inv_bwd)


def _ln_stats(z):
    mu = jnp.mean(z, -1, keepdims=True)
    zc = z - mu
    rstd = lax.rsqrt(jnp.mean(zc * zc, -1, keepdims=True) + LN_EPS)
    return zc * rstd, rstd


def _ln_bwd(z, g, dout):
    xh, rstd = _ln_stats(z)
    dxh = dout * g
    dz = rstd * (dxh - jnp.mean(dxh, -1, keepdims=True) - xh * jnp.mean(dxh * xh, -1, keepdims=True))
    return dz, jnp.sum(dout * xh, 0, keepdims=True), jnp.sum(dout, 0, keepdims=True)


def _full_spec(shape):
    nd = len(shape)
    return pl.BlockSpec(shape, lambda *_: (0,) * nd)


def _ffn_fwd(x, p384, pd, lg, lb, slot, which, ple=None):
    T = x.shape[0]
    sg, su, sd = 2 * slot, 2 * slot + 1, slot
    has_ple = ple is not None

    def body(*refs):
        if has_ple:
            (x_ref, wg_ref, wu_ref, wd_ref, lg_ref, lb_ref, p_ref, wpg_ref, wpp_ref,
             z_ref, o_ref, g_ref, u_ref, acc, xb_s) = refs
        else:
            x_ref, wg_ref, wu_ref, wd_ref, lg_ref, lb_ref, z_ref, o_ref, g_ref, u_ref, acc, xb_s = refs
        f = pl.program_id(1)

        @pl.when(f == 0)
        def _():
            x = x_ref[...]
            xb = x.astype(bf16)
            xb_s[...] = xb
            base = ALPHA * x
            if has_ple:
                gate = _sigmoid(_dot(xb, wpg_ref[...].reshape(D, D)))
                base = base + gate * _dot(p_ref[...], wpp_ref[...])
            acc[...] = base

        xb = xb_s[...]
        g = _dot(xb, wg_ref[...])
        u = _dot(xb, wu_ref[...])
        g_ref[...] = g.astype(bf16)
        u_ref[...] = u.astype(bf16)
        acc[...] += 0.5 * _dot(_silu(g) * u, wd_ref[...].reshape(FB * FSP, D))

        @pl.when(f == NF - 1)
        def _():
            z = acc[...]
            z_ref[...] = z
            o_ref[...] = _ln_stats(z)[0] * lg_ref[...] + lb_ref[...]

    row = pl.BlockSpec((TM, D), lambda i, f: (i, 0))
    in_specs = [row,
                pl.BlockSpec((None, D, FB * FSP), lambda i, f: (sg, 0, f)),
                pl.BlockSpec((None, D, FB * FSP), lambda i, f: (su, 0, f)),
                pl.BlockSpec((FB, None, FSP, D), lambda i, f: (f, sd, 0, 0)),
                _full_spec((1, D)), _full_spec((1, D))]
    args = [x, p384, p384, pd, lg, lb]
    if has_ple:
        p, pr, wpp = ple
        in_specs += [pl.BlockSpec((TM, PLE), lambda i, f: (i, 0)),
                     pl.BlockSpec((NDEV, None, 128, D), lambda i, f: (0, 2, 0, 0)),
                     _full_spec((PLE, D))]
        args += [p, pr, wpp]
    hid = pl.BlockSpec((TM, FB * FSP), lambda i, f: (i, f))
    hshape = jax.ShapeDtypeStruct((T, NDEV * FSP), bf16)
    return pl.pallas_call(
        body, grid=(T // TM, NF), in_specs=in_specs, out_specs=[row, row, hid, hid],
        out_shape=[jax.ShapeDtypeStruct((T, D), f32)] * 2 + [hshape, hshape],
        scratch_shapes=[pltpu.VMEM((TM, D), f32), pltpu.VMEM((TM, D), bf16)],
        compiler_params=_cparams(("arbitrary", "arbitrary")), name=f"ffn{which + 1}_fwd")(*args)


def _ffn_bwd(z, dout, gs, us, p384, pd, lg, slot, which):
    T = z.shape[0]
    TMB = TM
    sg, su, sd = 2 * slot, 2 * slot + 1, slot

    def body(z_ref, do_ref, g_ref, u_ref, wg_ref, wu_ref, wd_ref, lg_ref,
             dx_ref, dg_ref, du_ref, a_ref, dy_ref, dlg_ref, dlb_ref, acc, dyb):
        i, f = pl.program_id(0), pl.program_id(1)

        @pl.when(jnp.logical_and(i == 0, f == 0))
        def _():
            dlg_ref[...] = jnp.zeros_like(dlg_ref)
            dlb_ref[...] = jnp.zeros_like(dlb_ref)

        @pl.when(f == 0)
        def _():
            dz, dlg, dlb = _ln_bwd(z_ref[...], lg_ref[...], do_ref[...])
            dlg_ref[...] += dlg
            dlb_ref[...] += dlb
            dy = (0.5 * dz).astype(bf16)
            dyb[...] = dy
            dy_ref[...] = dy
            acc[...] = ALPHA * dz

        g = g_ref[...].astype(f32)
        u = u_ref[...].astype(f32)
        da = _dot_nt(dyb[...], wd_ref[...].reshape(FB * FSP, D))
        sgm = _sigmoid(g)
        dg = (da * u * (sgm * (1.0 + g * (1.0 - sgm)))).astype(bf16)
        du = (da * (g * sgm)).astype(bf16)
        dg_ref[...] = dg
        du_ref[...] = du
        a_ref[...] = (g * sgm * u).astype(bf16)
        acc[...] += _dot_nt(dg, wg_ref[...]) + _dot_nt(du, wu_ref[...])

        @pl.when(f == NF - 1)
        def _():
            dx_ref[...] = acc[...]

    row = pl.BlockSpec((TMB, D), lambda i, f: (i, 0))
    hid = pl.BlockSpec((TMB, FB * FSP), lambda i, f: (i, f))
    vec = _full_spec((1, D))
    in_specs = [row, row, hid, hid,
                pl.BlockSpec((None, D, FB * FSP), lambda i, f: (sg, 0, f)),
                pl.BlockSpec((None, D, FB * FSP), lambda i, f: (su, 0, f)),
                pl.BlockSpec((FB, None, FSP, D), lambda i, f: (f, sd, 0, 0)),
                vec]
    args = [z, dout, gs, us, p384, p384, pd, lg]
    out_specs = [row, hid, hid, hid, row, vec, vec]
    hshape = jax.ShapeDtypeStruct((T, NDEV * FSP), bf16)
    out_shape = [jax.ShapeDtypeStruct((T, D), f32), hshape, hshape, hshape, jax.ShapeDtypeStruct((T, D), bf16),
                 jax.ShapeDtypeStruct((1, D), f32), jax.ShapeDtypeStruct((1, D), f32)]
    return pl.pallas_call(
        body, grid=(T // TMB, NF), in_specs=in_specs, out_specs=out_specs, out_shape=out_shape,
        scratch_shapes=[pltpu.VMEM((TMB, D), f32), pltpu.VMEM((TMB, D), bf16)],
        compiler_params=_cparams(("arbitrary", "arbitrary")), name=f"ffn{which + 1}_bwd")(*args)


def _ple_bwd(x, p, dy, dx_ffn, pr, wpp, layer):
    T = x.shape[0]

    def body(x_ref, p_ref, dy_ref, dxf_ref, wpg_ref, wpp_ref, dx_ref, dgp_ref, dpj_ref):
        dz = 2.0 * dy_ref[...].astype(f32)
        wpg = wpg_ref[...].reshape(D, D)
        gate = _sigmoid(_dot(x_ref[...], wpg))
        proj = _dot(p_ref[...], wpp_ref[...])
        dgp = (dz * proj * gate * (1.0 - gate)).astype(bf16)
        dgp_ref[...] = dgp
        dpj_ref[...] = (dz * gate).astype(bf16)
        dx_ref[...] = dxf_ref[...] + _dot_nt(dgp, wpg)

    row = pl.BlockSpec((TM, D), lambda i: (i, 0))
    return pl.pallas_call(
        body, grid=(T // TM,),
        in_specs=[row, pl.BlockSpec((TM, PLE), lambda i: (i, 0)), row, row,
                  pl.BlockSpec((NDEV, None, 128, D), lambda i: (0, 3 * layer + 2, 0, 0)), _full_spec((PLE, D))],
        out_specs=[row, row, row],
        out_shape=[jax.ShapeDtypeStruct((T, D), f32), jax.ShapeDtypeStruct((T, D), bf16),
                   jax.ShapeDtypeStruct((T, D), bf16)],
        compiler_params=_cparams(("arbitrary",)), name="ple_bwd")(x, p, dy, dx_ffn, pr, wpp)


def _matmul_tn(a, b, nb, name, nsub=1):
    T, M = a.shape
    N = b.shape[1]
    wide = nsub * nb
    tk = min(T, 1024 if wide <= 2048 else 512)
    nk = T // tk

    def body(a_ref, b_ref, o_ref, acc):
        k = pl.program_id(1)

        @pl.when(k == 0)
        def _():
            acc[...] = jnp.zeros_like(acc)

        acc[...] += _dot_tn(a_ref[...], b_ref[...])

        @pl.when(k == nk - 1)
        def _():
            for j in range(nsub):
                o_ref[j] = acc[:, j * nb:(j + 1) * nb].astype(bf16)

    return pl.pallas_call(
        body, grid=(N // wide, nk),
        in_specs=[pl.BlockSpec((tk, M), lambda n, k: (k, 0)), pl.BlockSpec((tk, wide), lambda n, k: (k, n))],
        out_specs=pl.BlockSpec((nsub, M, nb), lambda n, k: (n, 0, 0)),
        out_shape=jax.ShapeDtypeStruct((N // nb, M, nb), bf16),
        scratch_shapes=[pltpu.VMEM((M, wide), f32)],
        compiler_params=_cparams(("arbitrary", "arbitrary")), name=name)(a, b)


def _proj_in(x, pr, pinl, ping, layer):
    T = x.shape[0]

    def body(x_ref, wr_ref, wl_ref, wg_ref, hr_ref, hl_ref, hg_ref):
        xb = x_ref[...].astype(bf16)
        hr_ref[...] = _dot(xb, wr_ref[...].reshape(D, D))
        hl_ref[...] = _dot(xb, wl_ref[...].reshape(D, 2 * LRU_W))
        hg_ref[...] = _dot(xb, wg_ref[...].reshape(D, GDN_IN))

    return pl.pallas_call(
        body, grid=(T // TM,),
        in_specs=[pl.BlockSpec((TM, D), lambda i: (i, 0)),
                  pl.BlockSpec((NDEV, None, 128, D), lambda i: (0, 3 * layer, 0, 0)),
                  pl.BlockSpec((NDEV, None, 128, 2 * LRU_W), lambda i: (0, layer, 0, 0)),
                  pl.BlockSpec((NDEV, None, 128, GDN_IN), lambda i: (0, layer, 0, 0))],
        out_specs=[pl.BlockSpec((TM, D), lambda i: (i, 0)), pl.BlockSpec((TM, 2 * LRU_W), lambda i: (i, 0)),
                   pl.BlockSpec((TM, GDN_IN), lambda i: (i, 0))],
        out_shape=[jax.ShapeDtypeStruct((T, D), f32), jax.ShapeDtypeStruct((T, 2 * LRU_W), f32),
                   jax.ShapeDtypeStruct((T, GDN_IN), f32)],
        compiler_params=_cparams(("arbitrary",)), name="proj_in")(x, pr, pinl, ping)


def _proj_in_bwd(base, dhr, dhl, dhg, pr, pinl, ping, layer):
    T = base.shape[0]

    def body(b_ref, dr_ref, dl_ref, dg_ref, wr_ref, wl_ref, wg_ref, o_ref):
        o_ref[...] = (b_ref[...] + _dot_nt(dr_ref[...], wr_ref[...].reshape(D, D))
                      + _dot_nt(dl_ref[...], wl_ref[...].reshape(D, 2 * LRU_W))
                      + _dot_nt(dg_ref[...], wg_ref[...].reshape(D, GDN_IN)))

    return pl.pallas_call(
        body, grid=(T // TM,),
        in_specs=[pl.BlockSpec((TM, D), lambda i: (i, 0)), pl.BlockSpec((TM, D), lambda i: (i, 0)),
                  pl.BlockSpec((TM, 2 * LRU_W), lambda i: (i, 0)), pl.BlockSpec((TM, GDN_IN), lambda i: (i, 0)),
                  pl.BlockSpec((NDEV, None, 128, D), lambda i: (0, 3 * layer, 0, 0)),
                  pl.BlockSpec((NDEV, None, 128, 2 * LRU_W), lambda i: (0, layer, 0, 0)),
                  pl.BlockSpec((NDEV, None, 128, GDN_IN), lambda i: (0, layer, 0, 0))],
        out_specs=pl.BlockSpec((TM, D), lambda i: (i, 0)),
        out_shape=jax.ShapeDtypeStruct((T, D), f32),
        compiler_params=_cparams(("arbitrary",)), name="proj_in_bwd")(base, dhr, dhl, dhg, pr, pinl, ping)


def _mix_out(x1, o_r, o_l, o_g, pr, lg, lb, layer):
    T = x1.shape[0]

    def body(x_ref, r_ref, l_ref, g_ref, w_ref, lg_ref, lb_ref, z_ref, o_ref):
        w = w_ref[...].reshape(D, D)
        z = (ALPHA * x_ref[...] + _dot(r_ref[...], w[0:RET_W]) + _dot(l_ref[...], w[RET_W:RET_W + LRU_W])
             + _dot(g_ref[...], w[RET_W + LRU_W:D]))
        z_ref[...] = z
        o_ref[...] = _ln_stats(z)[0] * lg_ref[...] + lb_ref[...]

    row = pl.BlockSpec((TM, D), lambda i: (i, 0))
    return pl.pallas_call(
        body, grid=(T // TM,),
        in_specs=[row, pl.BlockSpec((TM, RET_W), lambda i: (i, 0)), pl.BlockSpec((TM, LRU_W), lambda i: (i, 0)),
                  pl.BlockSpec((TM, GDN_W), lambda i: (i, 0)),
                  pl.BlockSpec((NDEV, None, 128, D), lambda i: (0, 3 * layer + 1, 0, 0)),
                  _full_spec((1, D)), _full_spec((1, D))],
        out_specs=[row, row], out_shape=[jax.ShapeDtypeStruct((T, D), f32)] * 2,
        compiler_params=_cparams(("arbitrary",)), name="mix_out")(x1, o_r, o_l, o_g, pr, lg, lb)


def _mix_out_bwd(z, dout, pr, lg, layer):
    T = z.shape[0]

    def body(z_ref, do_ref, w_ref, lg_ref, dxb_ref, dzb_ref, dr_ref, dl_ref, dg_ref, dlg_ref, dlb_ref):
        @pl.when(pl.program_id(0) == 0)
        def _():
            dlg_ref[...] = jnp.zeros_like(dlg_ref)
            dlb_ref[...] = jnp.zeros_like(dlb_ref)

        dz, dlg, dlb = _ln_bwd(z_ref[...], lg_ref[...], do_ref[...])
        dlg_ref[...] += dlg
        dlb_ref[...] += dlb
        dxb_ref[...] = ALPHA * dz
        dzb = dz.astype(bf16)
        dzb_ref[...] = dzb
        w = w_ref[...].reshape(D, D)
        dr_ref[...] = _dot_nt(dzb, w[0:RET_W])
        dl_ref[...] = _dot_nt(dzb, w[RET_W:RET_W + LRU_W])
        dg_ref[...] = _dot_nt(dzb, w[RET_W + LRU_W:D])

    row = pl.BlockSpec((TM, D), lambda i: (i, 0))
    vec = _full_spec((1, D))
    return pl.pallas_call(
        body, grid=(T // TM,),
        in_specs=[row, row, pl.BlockSpec((NDEV, None, 128, D), lambda i: (0, 3 * layer + 1, 0, 0)), vec],
        out_specs=[row, row, pl.BlockSpec((TM, RET_W), lambda i: (i, 0)), pl.BlockSpec((TM, LRU_W), lambda i: (i, 0)),
                   pl.BlockSpec((TM, GDN_W), lambda i: (i, 0)), vec, vec],
        out_shape=[jax.ShapeDtypeStruct((T, D), f32), jax.ShapeDtypeStruct((T, D), bf16),
                   jax.ShapeDtypeStruct((T, RET_W), f32), jax.ShapeDtypeStruct((T, LRU_W), f32),
                   jax.ShapeDtypeStruct((T, GDN_W), f32), jax.ShapeDtypeStruct((1, D), f32),
                   jax.ShapeDtypeStruct((1, D), f32)],
        compiler_params=_cparams(("arbitrary",)), name="mix_out_bwd")(z, dout, pr, lg)


def _loss_grad(y, target):
    T = y.shape[0]

    def body(y_ref, t_ref, dy_ref, l_ref):
        @pl.when(pl.program_id(0) == 0)
        def _():
            l_ref[...] = jnp.zeros_like(l_ref)

        e = y_ref[...] - t_ref[...]
        dy_ref[...] = e * (1.0 / D)
        l_ref[...] += 0.5 * jnp.sum(jnp.sum(e * e, -1, keepdims=True) * (1.0 / D), 0, keepdims=True)

    row = pl.BlockSpec((TM, D), lambda i: (i, 0))
    return pl.pallas_call(
        body, grid=(T // TM,), in_specs=[row, row], out_specs=[row, _full_spec((1, 1))],
        out_shape=[jax.ShapeDtypeStruct((T, D), f32), jax.ShapeDtypeStruct((1, 1), f32)],
        compiler_params=_cparams(("arbitrary",)), name="loss_grad")(y, target)


def _split_heads(x, H):
    n = x.shape[0] // CH
    parts = [x[:, h * CH:(h + 1) * CH].reshape(n, CH, CH) for h in range(H)]
    return jnp.stack(parts, axis=1).reshape(n * H, CH, CH)


def _merge_heads(ref, x, H, col0=0):
    n = x.shape[0] // H
    x4 = x.reshape(n, H, CH, CH)
    for h in range(H):
        ref[:, col0 + h * CH:col0 + (h + 1) * CH] = x4[:, h].reshape(n * CH, CH)


def _conv_fwd(ext, x, tail, w, R):
    ext[0:8, :] = tail
    ext[8:R + 8, :] = x
    y = w[3:4, :] * x
    for k in range(3):
        y = y + w[k:k + 1, :] * ext[5 + k:5 + k + R, :]
    return y


def _conv_bwd(ext, ext2, dy, dy_next, w, R):
    ext2[0:R, :] = dy
    ext2[R:R + 8, :] = dy_next
    dx = w[3:4, :] * dy
    dws = []
    for k in range(3):
        dx = dx + w[k:k + 1, :] * ext2[3 - k:3 - k + R, :]
        dws.append(jnp.sum(dy * ext[5 + k:5 + k + R, :], 0, keepdims=True))
    dws.append(jnp.sum(dy * ext[8:8 + R, :], 0, keepdims=True))
    return dx, jnp.concatenate(dws, axis=0)


def _prev_tail_spec(R, W):
    return pl.BlockSpec((8, W), lambda i: (jnp.maximum(i * (R // 8) - 1, 0), 0))


def _prev_tail_spec_rev(R, W, nb):
    return pl.BlockSpec((8, W), lambda i: (jnp.maximum((nb - 1 - i) * (R // 8) - 1, 0), 0))


def _rope_tables(positions):
    T = positions.shape[0]

    def body(p_ref, c_ref, s_ref):
        lane = lax.broadcasted_iota(jnp.int32, (TM, RET_W), 1)
        fi = (lane % 32).astype(f32)
        inv = jnp.exp(fi * (-math.log(ROPE_THETA) / 32.0))
        ang = p_ref[...].astype(f32) * inv
        c_ref[...] = jnp.cos(ang)
        s_ref[...] = jnp.where(lane % CH < 32, -jnp.sin(ang), jnp.sin(ang))

    row = pl.BlockSpec((TM, RET_W), lambda i: (i, 0))
    return pl.pallas_call(
        body, grid=(T // TM,), in_specs=[pl.BlockSpec((TM, 1), lambda i: (i, 0))], out_specs=[row, row],
        out_shape=[jax.ShapeDtypeStruct((T, RET_W), f32)] * 2,
        compiler_params=_cparams(("arbitrary",)), name="rope_tables")(positions)


def _partner(x):
    lane = lax.broadcasted_iota(jnp.int32, x.shape, 1)
    return jnp.where(lane % CH < 32, pltpu.roll(x, RET_W - 32, 1), pltpu.roll(x, 32, 1))


def _ret_consts():
    ii = lax.broadcasted_iota(jnp.int32, (CH, CH), 0).astype(f32)
    jj = lax.broadcasted_iota(jnp.int32, (CH, CH), 1).astype(f32)
    intra, cross, tail, cd = [], [], [], []
    for h in range(RET_H):
        lg = math.log1p(-(2.0 ** (-5.0 - h)))
        intra.append(jnp.exp(jnp.abs(ii - jj) * lg))
        cross.append(jnp.exp((ii + 1.0) * lg))
        tail.append(jnp.exp((CH - 1.0 - ii) * lg))
        cd.append(jnp.full((CH, CH), math.exp(CH * lg), f32))
    return jnp.stack(intra), jnp.stack(cross), jnp.stack(tail), jnp.stack(cd)


def _ret_chunk(consts, q, k, v, st):
    intra, cross, tail, cd = consts
    s = _bmm('hid,hjd->hij', q, k) * intra
    o = _bmm('hij,hje->hie', s, v) + _bmm('hid,hde->hie', q * cross, st)
    st2 = st * cd + _bmm('hjd,hje->hde', k * tail, v)
    oc = o - _rowsum(o) * (1.0 / CH)
    on = oc * lax.rsqrt(_rowsum(oc * oc) * (1.0 / CH) + 1e-5)
    return on, st2


def _ret_fwd(hr, cosw, sinw, gam):
    T = hr.shape[0]
    R = RB_RET
    nc = R // CH

    def body(h_ref, c_ref, s_ref, g_ref, o_ref, st_ref, st, wide):
        @pl.when(pl.program_id(0) == 0)
        def _():
            st[...] = jnp.zeros_like(st)

        consts = _ret_consts()
        cw, sw = c_ref[...], s_ref[...]
        q, k = h_ref[:, 0:RET_W], h_ref[:, RET_W:2 * RET_W]
        qh = _split_heads((q * cw + _partner(q) * sw) * 0.125, RET_H)
        kh = _split_heads(k * cw + _partner(k) * sw, RET_H)
        vh = _split_heads(h_ref[:, 2 * RET_W:3 * RET_W], RET_H)
        outs = []
        s_cur = st[...]
        for c in range(nc):
            sl = slice(c * RET_H, (c + 1) * RET_H)
            st_ref[c] = s_cur
            on, s_cur = _ret_chunk(consts, qh[sl], kh[sl], vh[sl], s_cur)
            outs.append(on)
        st[...] = s_cur
        _merge_heads(wide, jnp.concatenate(outs, axis=0), RET_H)
        o_ref[...] = wide[...] * g_ref[...] * _silu(h_ref[:, 3 * RET_W:4 * RET_W])

    blk = pl.BlockSpec((R, RET_W), lambda i: (i, 0))
    return pl.pallas_call(
        body, grid=(T // R,),
        in_specs=[pl.BlockSpec((R, D), lambda i: (i, 0)), blk, blk, _full_spec((1, RET_W))],
        out_specs=[blk, pl.BlockSpec((nc, RET_H, CH, CH), lambda i: (i, 0, 0, 0))],
        out_shape=[jax.ShapeDtypeStruct((T, RET_W), f32), jax.ShapeDtypeStruct((T // CH, RET_H, CH, CH), f32)],
        scratch_shapes=[pltpu.VMEM((RET_H, CH, CH), f32), pltpu.VMEM((R, RET_W), f32)],
        compiler_params=_cparams(("arbitrary",)), name="ret_fwd")(hr, cosw, sinw, gam)


def _ret_bwd(hr, cosw, sinw, gam, states, dout):
    T = hr.shape[0]
    R = RB_RET
    nc = R // CH
    nb = T // R

    def body(h_ref, c_ref, s_ref, g_ref, st_ref, do_ref, dh_ref, dgam_ref, dst, wide):
        @pl.when(pl.program_id(0) == 0)
        def _():
            dst[...] = jnp.zeros_like(dst)
            dgam_ref[...] = jnp.zeros_like(dgam_ref)

        consts = _ret_consts()
        cw, sw = c_ref[...], s_ref[...]
        q, k = h_ref[:, 0:RET_W], h_ref[:, RET_W:2 * RET_W]
        gr = h_ref[:, 3 * RET_W:4 * RET_W]
        qh = _split_heads((q * cw + _partner(q) * sw) * 0.125, RET_H)
        kh = _split_heads(k * cw + _partner(k) * sw, RET_H)
        vh = _split_heads(h_ref[:, 2 * RET_W:3 * RET_W], RET_H)
        do = do_ref[...]
        gam = g_ref[...]
        sg = _silu(gr)
        don = _split_heads(do * gam * sg, RET_H)
        ons, dqs, dks, dvs = [None] * nc, [None] * nc, [None] * nc, [None] * nc
        ds = dst[...]
        for c in reversed(range(nc)):
            sl = slice(c * RET_H, (c + 1) * RET_H)
            (on, _), vjp = jax.vjp(functools.partial(_ret_chunk, consts), qh[sl], kh[sl], vh[sl], st_ref[c])
            dqs[c], dks[c], dvs[c], ds = vjp((don[sl], ds))
            ons[c] = on
        dst[...] = ds
        _merge_heads(wide, jnp.concatenate(ons, axis=0), RET_H)
        onw = wide[...]
        dgam_ref[...] += jnp.sum(do * onw * sg, 0, keepdims=True)
        dh_ref[:, 3 * RET_W:4 * RET_W] = do * onw * gam * _dsilu(gr)
        _merge_heads(wide, jnp.concatenate(dqs, axis=0), RET_H)
        u = wide[...] * 0.125
        dh_ref[:, 0:RET_W] = u * cw + _partner(u * sw)
        _merge_heads(wide, jnp.concatenate(dks, axis=0), RET_H)
        u = wide[...]
        dh_ref[:, RET_W:2 * RET_W] = u * cw + _partner(u * sw)
        _merge_heads(dh_ref, jnp.concatenate(dvs, axis=0), RET_H, col0=2 * RET_W)

    blk = pl.BlockSpec((R, RET_W), lambda i: (nb - 1 - i, 0))
    return pl.pallas_call(
        body, grid=(nb,),
        in_specs=[pl.BlockSpec((R, D), lambda i: (nb - 1 - i, 0)), blk, blk, _full_spec((1, RET_W)),
                  pl.BlockSpec((nc, RET_H, CH, CH), lambda i: (nb - 1 - i, 0, 0, 0)), blk],
        out_specs=[pl.BlockSpec((R, D), lambda i: (nb - 1 - i, 0)), _full_spec((1, RET_W))],
        out_shape=[jax.ShapeDtypeStruct((T, D), f32), jax.ShapeDtypeStruct((1, RET_W), f32)],
        scratch_shapes=[pltpu.VMEM((RET_H, CH, CH), f32), pltpu.VMEM((R, RET_W), f32)],
        compiler_params=_cparams(("arbitrary",)), name="ret_bwd")(hr, cosw, sinw, gam, states, dout)


def _lru_ab(xc, wa, ba, wx, bx, lam):
    r = _sigmoid(_dot(xc, wa) + ba)
    i = _sigmoid(_dot(xc, wx) + bx)
    la = 8.0 * r * (-_softplus(-lam))
    a = jnp.exp(la)
    em = jnp.tanh(la) * (jnp.exp(2.0 * la) + 1.0)
    return a, jnp.sqrt(-em) * (i * xc)


def _lru_out(h, gate):
    return h * _gelu(gate)


def _scan_fwd(a, b):
    R = a.shape[0]
    row = lax.broadcasted_iota(jnp.int32, a.shape, 0)
    d = 1
    while d < R:
        m = row >= d
        b = jnp.where(m, a * pltpu.roll(b, d, 0) + b, b)
        a = jnp.where(m, a * pltpu.roll(a, d, 0), a)
        d *= 2
    return a, b


def _scan_bwd(a, b):
    R = a.shape[0]
    row = lax.broadcasted_iota(jnp.int32, a.shape, 0)
    d = 1
    while d < R:
        m = row < R - d
        b = jnp.where(m, a * pltpu.roll(b, R - d, 0) + b, b)
        a = jnp.where(m, a * pltpu.roll(a, R - d, 0), a)
        d *= 2
    return b


def _lru_fwd(hl, cw, cb, wa, ba, wx, bx, lam):
    T = hl.shape[0]
    R = RB_LRU
    W = LRU_W

    def body(h_ref, t_ref, cw_ref, cb_ref, wa_ref, ba_ref, wx_ref, bx_ref, lam_ref, o_ref, hs_ref, carry, ext):
        first = pl.program_id(0) == 0

        @pl.when(first)
        def _():
            carry[...] = jnp.zeros_like(carry)

        tail = jnp.where(first, 0.0, t_ref[:, 0:W])
        xc = _conv_fwd(ext, h_ref[:, 0:W], tail, cw_ref[...], R) + cb_ref[...]
        a, b = _lru_ab(xc, wa_ref[...], ba_ref[...], wx_ref[...], bx_ref[...], lam_ref[...])
        ap, hloc = _scan_fwd(a, b)
        h = hloc + ap * carry[0:1, :]
        carry[...] = jnp.broadcast_to(h[R - 1:R, :], carry.shape)
        hs_ref[...] = h
        o_ref[...] = _lru_out(h, h_ref[:, W:2 * W])

    vec = _full_spec((1, W))
    blk = pl.BlockSpec((R, W), lambda i: (i, 0))
    return pl.pallas_call(
        body, grid=(T // R,),
        in_specs=[pl.BlockSpec((R, 2 * W), lambda i: (i, 0)), _prev_tail_spec(R, 2 * W), _full_spec((4, W)), vec,
                  _full_spec((W, W)), vec, _full_spec((W, W)), vec, vec],
        out_specs=[blk, blk], out_shape=[jax.ShapeDtypeStruct((T, W), f32)] * 2,
        scratch_shapes=[pltpu.VMEM((8, W), f32), pltpu.VMEM((R + 8, W), f32)],
        compiler_params=_cparams(("arbitrary",)), name="lru_fwd")(hl, hl, cw, cb, wa, ba, wx, bx, lam)


def _lru_bwd(hl, hs, cw, cb, wa, ba, wx, bx, lam, dout):
    T = hl.shape[0]
    R = RB_LRU
    W = LRU_W
    nb = T // R

    def body(h_ref, t_ref, hs_ref, hst_ref, cw_ref, cb_ref, wa_ref, ba_ref, wx_ref, bx_ref, lam_ref, do_ref,
             dh_ref, dcw_ref, dcb_ref, dwa_ref, dba_ref, dwx_ref, dbx_ref, dlam_ref, carry_g, carry_dy, ext, ext2):
        i = pl.program_id(0)
        last_blk = i == 0
        first_blk = i == nb - 1

        @pl.when(last_blk)
        def _():
            carry_g[...] = jnp.zeros_like(carry_g)
            carry_dy[...] = jnp.zeros_like(carry_dy)
            for r in (dcw_ref, dcb_ref, dwa_ref, dba_ref, dwx_ref, dbx_ref, dlam_ref):
                r[...] = jnp.zeros_like(r)

        tail = jnp.where(first_blk, 0.0, t_ref[:, 0:W])
        xc = _conv_fwd(ext, h_ref[:, 0:W], tail, cw_ref[...], R) + cb_ref[...]
        (a, _), vjp_ab = jax.vjp(_lru_ab, xc, wa_ref[...], ba_ref[...], wx_ref[...], bx_ref[...], lam_ref[...])
        hs = hs_ref[...]
        _, vjp_out = jax.vjp(_lru_out, hs, h_ref[:, W:2 * W])
        dh, dgate = vjp_out(do_ref[...])
        row = lax.broadcasted_iota(jnp.int32, (R, W), 0)
        dh = jnp.where(row == R - 1, dh + carry_g[0:1, :], dh)
        a_up = jnp.where(row == R - 1, 0.0, pltpu.roll(a, R - 1, 0))
        g = _scan_bwd(a_up, dh)
        carry_g[...] = jnp.broadcast_to(a[0:1, :] * g[0:1, :], carry_g.shape)
        hprev0 = jnp.where(first_blk, 0.0, hst_ref[7:8, :])
        hprev = jnp.where(row == 0, hprev0, pltpu.roll(hs, 1, 0))
        dxc, dwa, dba, dwx, dbx, dlam = vjp_ab((g * hprev, g))
        dwa_ref[...] += dwa
        dba_ref[...] += dba
        dwx_ref[...] += dwx
        dbx_ref[...] += dbx
        dlam_ref[...] += dlam
        dcb_ref[...] += jnp.sum(dxc, 0, keepdims=True)
        dx, dcw = _conv_bwd(ext, ext2, dxc, carry_dy[...], cw_ref[...], R)
        carry_dy[...] = dxc[0:8, :]
        dcw_ref[...] += dcw
        dh_ref[:, 0:W] = dx
        dh_ref[:, W:2 * W] = dgate

    vec = _full_spec((1, W))
    mat = _full_spec((W, W))
    blk = pl.BlockSpec((R, W), lambda i: (nb - 1 - i, 0))
    blk2 = pl.BlockSpec((R, 2 * W), lambda i: (nb - 1 - i, 0))
    return pl.pallas_call(
        body, grid=(nb,),
        in_specs=[blk2, _prev_tail_spec_rev(R, 2 * W, nb), blk, _prev_tail_spec_rev(R, W, nb), _full_spec((4, W)), vec,
                  mat, vec, mat, vec, vec, blk],
        out_specs=[blk2, _full_spec((4, W)), vec, mat, vec, mat, vec, vec],
        out_shape=[jax.ShapeDtypeStruct((T, 2 * W), f32), jax.ShapeDtypeStruct((4, W), f32),
                   jax.ShapeDtypeStruct((1, W), f32), jax.ShapeDtypeStruct((W, W), f32),
                   jax.ShapeDtypeStruct((1, W), f32), jax.ShapeDtypeStruct((W, W), f32),
                   jax.ShapeDtypeStruct((1, W), f32), jax.ShapeDtypeStruct((1, W), f32)],
        scratch_shapes=[pltpu.VMEM((8, W), f32), pltpu.VMEM((8, W), f32), pltpu.VMEM((R + 8, W), f32),
                        pltpu.VMEM((R + 8, W), f32)],
        compiler_params=_cparams(("arbitrary",)), name="lru_bwd")(hl, hl, hs, hs, cw, cb, wa, ba, wx, bx, lam, dout)


def _gdn_local(inverse, qs, ks, vs, gc, bb):
    B = qs.shape[0]
    ii = lax.broadcasted_iota(jnp.int32, (B, CH, CH), 1)
    jj = lax.broadcasted_iota(jnp.int32, (B, CH, CH), 2)
    q = qs * lax.rsqrt(_rowsum(qs * qs) + 1e-6)
    k = ks * lax.rsqrt(_rowsum(ks * ks) + 1e-6)
    gct = jnp.swapaxes(gc, 1, 2)
    decay = jnp.where(ii >= jj, jnp.exp(jnp.minimum(gc - gct, 0.0)), 0.0)
    kk = _bmm('bid,bjd->bij', k, k)
    inv = inverse(-jnp.where(ii > jj, bb * kk * decay, 0.0))
    egc = jnp.exp(gc)
    u = _bmm3('bij,bje->bie', inv, vs * bb)
    w = _bmm3('bij,bje->bie', inv, k * (bb * egc))
    qk = _bmm('bid,bjd->bij', q, k) * (0.125 * decay)
    glast = gc[:, CH - 1:CH, :]
    return u, w, qk, q * (0.125 * egc), k * jnp.exp(glast - gc), jnp.exp(jnp.broadcast_to(glast, gc.shape))


def _gdn_step(st, u, w, qk, qd, kt, egl, z, gn):
    vnew = u - _bmm('hcd,hde->hce', w, st)
    o = _bmm('hcd,hde->hce', qd, st) + _bmm('hij,hje->hie', qk, vnew)
    st2 = st * egl + _bmm('hcd,hce->hde', kt, vnew)
    out = o * lax.rsqrt(_rowsum(o * o) * (1.0 / CH) + 1e-6) * gn * _silu(z)
    return out, st2


def _gdn_scalars(ab, alog, dtb):
    sp = _softplus(ab + dtb)
    return -jnp.exp(alog) * sp, _sigmoid(ab)


def _bcast_heads(blk, lane0, H):
    R = blk.shape[0]
    n = R // CH
    parts = [jnp.broadcast_to(blk[:, lane0 + h:lane0 + h + 1], (R, CH)).reshape(n, CH, CH) for h in range(H)]
    return jnp.stack(parts, axis=1).reshape(n * H, CH, CH)


def _unbcast_heads(x, lane0, H):
    n = x.shape[0] // H
    R = n * CH
    s = jnp.sum(x, axis=2, keepdims=True).reshape(n, H, CH, 1)
    lane = lax.broadcasted_iota(jnp.int32, (R, 128), 1)
    acc = jnp.zeros((R, 128), f32)
    for h in range(H):
        acc = acc + jnp.where(lane == lane0 + h, jnp.broadcast_to(s[:, h].reshape(R, 1), (R, 128)), 0.0)
    return acc


def _gdn_fwd(hg, cw, alog, dtb, gn):
    T = hg.shape[0]
    R = RB_GDN
    nc = R // CH
    W3 = 3 * GDN_W
    H = GDN_H

    def body(h_ref, t_ref, cw_ref, al_ref, dt_ref, gn_ref, o_ref, st_ref, inv_ref, st, ext):
        first = pl.program_id(0) == 0

        @pl.when(first)
        def _():
            st[...] = jnp.zeros_like(st)

        def inverse(m):
            inv = _neumann_inv(m)
            inv_ref[...] = inv
            return inv

        tail = jnp.where(first, 0.0, t_ref[:, 0:W3])
        y = _silu(_conv_fwd(ext, h_ref[:, 0:W3], tail, cw_ref[...], R))
        qs, ks, vs = (_split_heads(y[:, j * GDN_W:(j + 1) * GDN_W], H) for j in range(3))
        zh = _split_heads(h_ref[:, W3:W3 + GDN_W], H)
        g, beta = _gdn_scalars(h_ref[:, W3 + GDN_W:GDN_IN], al_ref[...], dt_ref[...])
        loc = _gdn_local(inverse, qs, ks, vs, _bcast_heads(_chunk_cumsum(g), 0, H), _bcast_heads(beta, H, H))
        gnv = gn_ref[...]
        outs = []
        s_cur = st[...]
        for c in range(nc):
            sl = slice(c * H, (c + 1) * H)
            st_ref[c] = s_cur
            out, s_cur = _gdn_step(s_cur, *(t[sl] for t in loc), zh[sl], gnv)
            outs.append(out)
        st[...] = s_cur
        _merge_heads(o_ref, jnp.concatenate(outs, axis=0), H)

    return pl.pallas_call(
        body, grid=(T // R,),
        in_specs=[pl.BlockSpec((R, GDN_IN), lambda i: (i, 0)), _prev_tail_spec(R, GDN_IN), _full_spec((4, W3)),
                  _full_spec((1, 128)), _full_spec((1, 128)), _full_spec((1, CH))],
        out_specs=[pl.BlockSpec((R, GDN_W), lambda i: (i, 0)), pl.BlockSpec((nc, H, CH, CH), lambda i: (i, 0, 0, 0)),
                   pl.BlockSpec((nc * H, CH, CH), lambda i: (i, 0, 0))],
        out_shape=[jax.ShapeDtypeStruct((T, GDN_W), f32), jax.ShapeDtypeStruct((T // CH, H, CH, CH), f32),
                   jax.ShapeDtypeStruct((T // CH * H, CH, CH), f32)],
        scratch_shapes=[pltpu.VMEM((H, CH, CH), f32), pltpu.VMEM((R + 8, W3), f32)],
        compiler_params=_cparams(("arbitrary",)), name="gdn_fwd")(hg, hg, cw, alog, dtb, gn)


def _gdn_bwd(hg, cw, alog, dtb, gn, states, invs, dout):
    T = hg.shape[0]
    R = RB_GDN
    nc = R // CH
    nb = T // R
    W3 = 3 * GDN_W
    H = GDN_H

    def body(h_ref, t_ref, cw_ref, al_ref, dt_ref, gn_ref, st_ref, inv_ref, do_ref,
             dh_ref, dcw_ref, dal_ref, ddt_ref, dgn_ref, dst, carry_dy, ext, ext2, wide):
        i = pl.program_id(0)
        first_blk = i == nb - 1

        @pl.when(i == 0)
        def _():
            dst[...] = jnp.zeros_like(dst)
            carry_dy[...] = jnp.zeros_like(carry_dy)
            for r in (dcw_ref, dal_ref, ddt_ref, dgn_ref):
                r[...] = jnp.zeros_like(r)

        tail = jnp.where(first_blk, 0.0, t_ref[:, 0:W3])
        ypre = _conv_fwd(ext, h_ref[:, 0:W3], tail, cw_ref[...], R)
        y = _silu(ypre)
        qs, ks, vs = (_split_heads(y[:, j * GDN_W:(j + 1) * GDN_W], H) for j in range(3))
        zh = _split_heads(h_ref[:, W3:W3 + GDN_W], H)
        ab = h_ref[:, W3 + GDN_W:GDN_IN]
        alog, dtb = al_ref[...], dt_ref[...]
        g, beta = _gdn_scalars(ab, alog, dtb)
        kept = inv_ref[...]
        loc, vjp_loc = jax.vjp(functools.partial(_gdn_local, lambda m: _known_inv(m, kept)), qs, ks, vs,
                               _bcast_heads(_chunk_cumsum(g), 0, H), _bcast_heads(beta, H, H))
        doh = _split_heads(do_ref[...], H)
        gnv = gn_ref[...]
        dloc = [[None] * nc for _ in range(6)]
        dzs = [None] * nc
        ds = dst[...]
        dgn = jnp.zeros((1, CH), f32)
        for c in reversed(range(nc)):
            sl = slice(c * H, (c + 1) * H)
            _, vjp = jax.vjp(_gdn_step, st_ref[c], *(t[sl] for t in loc), zh[sl], gnv)
            grads = vjp((doh[sl], ds))
            ds = grads[0]
            for j in range(6):
                dloc[j][c] = grads[1 + j]
            dzs[c] = grads[7]
            dgn = dgn + grads[8]
        dst[...] = ds
        dgn_ref[...] += dgn
        dqs, dks, dvs, dgb, dbb = vjp_loc(tuple(jnp.concatenate(d, axis=0) for d in dloc))
        lane = lax.broadcasted_iota(jnp.int32, (R, 128), 1)
        dg = _chunk_cumsum(_unbcast_heads(dgb, 0, H), reverse=True)
        dbeta = _unbcast_heads(dbb, H, H)
        da = dg * (-jnp.exp(alog)) * _sigmoid(ab + dtb)
        dh_ref[:, W3 + GDN_W:GDN_IN] = jnp.where(lane < H, da, dbeta * beta * (1.0 - beta))
        ddt_ref[...] += jnp.sum(jnp.where(lane < H, da, 0.0), 0, keepdims=True)
        dal_ref[...] += jnp.sum(jnp.where(lane < H, dg * g, 0.0), 0, keepdims=True)
        _merge_heads(dh_ref, jnp.concatenate(dzs, axis=0), H, col0=W3)
        for j, dpart in enumerate((dqs, dks, dvs)):
            _merge_heads(wide, dpart, H, col0=j * GDN_W)
        dy = wide[...] * _dsilu(ypre)
        dx, dcw = _conv_bwd(ext, ext2, dy, carry_dy[...], cw_ref[...], R)
        carry_dy[...] = dy[0:8, :]
        dcw_ref[...] += dcw
        dh_ref[:, 0:W3] = dx

    blk = pl.BlockSpec((R, GDN_IN), lambda i: (nb - 1 - i, 0))
    return pl.pallas_call(
        body, grid=(nb,),
        in_specs=[blk, _prev_tail_spec_rev(R, GDN_IN, nb), _full_spec((4, W3)), _full_spec((1, 128)),
                  _full_spec((1, 128)), _full_spec((1, CH)),
                  pl.BlockSpec((nc, H, CH, CH), lambda i: (nb - 1 - i, 0, 0, 0)),
                  pl.BlockSpec((nc * H, CH, CH), lambda i: (nb - 1 - i, 0, 0)),
                  pl.BlockSpec((R, GDN_W), lambda i: (nb - 1 - i, 0))],
        out_specs=[blk, _full_spec((4, W3)), _full_spec((1, 128)), _full_spec((1, 128)), _full_spec((1, CH))],
        out_shape=[jax.ShapeDtypeStruct((T, GDN_IN), f32), jax.ShapeDtypeStruct((4, W3), f32),
                   jax.ShapeDtypeStruct((1, 128), f32), jax.ShapeDtypeStruct((1, 128), f32),
                   jax.ShapeDtypeStruct((1, CH), f32)],
        scratch_shapes=[pltpu.VMEM((H, CH, CH), f32), pltpu.VMEM((8, W3), f32), pltpu.VMEM((R + 8, W3), f32),
                        pltpu.VMEM((R + 8, W3), f32), pltpu.VMEM((R, W3), f32)],
        compiler_params=_cparams(("arbitrary",)), name="gdn_bwd")(hg, hg, cw, alog, dtb, gn, states, invs, dout)


def _block_diag(w):
    out = jnp.zeros((LRU_W, LRU_W), w.dtype)
    for g in range(w.shape[0]):
        out = lax.dynamic_update_slice(out, w[g], (g * CH, g * CH))
    return out


def _block_diag_t(w):
    return jnp.stack([w[g * CH:(g + 1) * CH, g * CH:(g + 1) * CH] for g in range(LRU_W // CH)])


def _pad_lanes(v, n=128):
    return jnp.pad(v, (0, n - v.shape[0]))[None, :]


def _local_step(x, p, positions, target, fetch, emit, sm):
    cosw, sinw = _rope_tables(positions)
    cols = lambda w: jnp.transpose(w, (1, 2, 0, 3)).reshape(-1, D, NDEV * FSP)
    saved = []
    h = x
    for l in range(DEPTH):
        v = lambda n: sm[n][l][None, :]
        F1, tok = fetch(l, 'f1', h)
        p384a, pda = cols(F1['p384']), F1['pd']
        z1, x1, g1, u1 = _ffn_fwd(h, p384a, pda, v('ln_ffn1_g') + tok, v('ln_ffn1_b'), 0, 0)
        G, _ = fetch(l, 'rest', x1)
        p384, pd, pr, pinl, ping, wpp = cols(G['p384']), G['pd'], G['pr'], G['pinl'], G['ping'], G['wpp']
        wts = (p384a, pda, p384, pd, pr, pinl, ping, wpp)
        hr, hl, hg = _proj_in(x1, pr, pinl, ping, 0)
        o_r, rst = _ret_fwd(hr, cosw, sinw, v('ret_norm_g'))
        lru_args = (sm['lru_conv_w'][l], v('lru_conv_b'), _block_diag(sm['lru_w_a'][l]), v('lru_b_a'),
                    _block_diag(sm['lru_w_x'][l]), v('lru_b_x'), v('lru_lambda'))
        o_l, hs = _lru_fwd(hl, *lru_args)
        gdn_args = (sm['gdn_conv_w'][l], _pad_lanes(sm['gdn_a_log'][l]), _pad_lanes(sm['gdn_dt_bias'][l]),
                    v('gdn_norm_g'))
        o_g, *gst = _gdn_fwd(hg, *gdn_args)
        z2, x2 = _mix_out(x1, o_r, o_l, o_g, pr, v('ln_mix_g'), v('ln_mix_b'), 0)
        z3, x3, g2, u2 = _ffn_fwd(x2, p384, pd, v('ln_ffn2_g'), v('ln_ffn2_b'), 0, 1, ple=(p[l], pr, wpp))
        saved.append((h, z1, x1, hr, hl, hg, o_r, rst, o_l, hs, lru_args, o_g, gst, gdn_args, z2, x2, z3,
                      g1, u1, g2, u2, wts))
        h = x3
    d, loss = _loss_grad(h, target)

    small = {n: [None] * DEPTH for n in SMALL}
    tok = 0.0
    for l in reversed(range(DEPTH)):
        (x0, z1, x1, hr, hl, hg, o_r, rst, o_l, hs, lru_args, o_g, gst, gdn_args, z2, x2, z3,
         g1, u1, g2, u2, wts) = saved[l]
        p384a, pda, p384, pd, pr, pinl, ping, wpp = wts
        v = lambda n: sm[n][l][None, :]
        rows = lambda m: m.reshape(NDEV, m.shape[1] // NDEV, m.shape[2])
        d2, dg2, du2, a2, dy2, small['ln_ffn2_g'][l], small['ln_ffn2_b'][l] = _ffn_bwd(
            z3, d, g2, u2, p384, pd, v('ln_ffn2_g') + tok, 0, 1)
        d2, dgp, dpj = _ple_bwd(x2, p[l], dy2, d2, pr, wpp, 0)
        dxb, dzb, do_r, do_l, do_g, small['ln_mix_g'][l], small['ln_mix_b'][l] = _mix_out_bwd(
            z2, d2, pr, v('ln_mix_g'), 0)
        dhr, small['ret_norm_g'][l] = _ret_bwd(hr, cosw, sinw, v('ret_norm_g'), rst, do_r)
        (dhl, small['lru_conv_w'][l], small['lru_conv_b'][l], dwa, small['lru_b_a'][l], dwx, small['lru_b_x'][l],
         small['lru_lambda'][l]) = _lru_bwd(hl, hs, *lru_args, do_l)
        small['lru_w_a'][l], small['lru_w_x'][l] = _block_diag_t(dwa), _block_diag_t(dwx)
        dhg, small['gdn_conv_w'][l], dal, ddt, small['gdn_norm_g'][l] = _gdn_bwd(hg, *gdn_args, *gst, do_g)
        small['gdn_a_log'][l], small['gdn_dt_bias'][l] = dal[:, 0:GDN_H], ddt[:, 0:GDN_H]
        d1 = _proj_in_bwd(dxb, dhr, dhl, dhg, pr, pinl, ping, 0)
        dwo = jnp.concatenate([_matmul_tn(o_r, dzb, D, "dw_out_r"), _matmul_tn(o_l, dzb, D, "dw_out_l"),
                               _matmul_tn(o_g, dzb, D, "dw_out_g")], axis=1)
        tok = emit(l, 'rest', {
            'p384': jnp.stack([_matmul_tn(x2, dg2, FSP, "dw_gate", NDEV), _matmul_tn(x2, du2, FSP, "dw_up", NDEV)],
                              axis=1),
            'pd': rows(_matmul_tn(a2, dy2, D, "dw_down"))[:, None],
            'pr': jnp.stack([rows(_matmul_tn(x1, dhr, D, "dw_in_r")), rows(dwo),
                             rows(_matmul_tn(x2, dgp, D, "dw_ple_gate"))], axis=1),
            'pinl': rows(_matmul_tn(x1, dhl, 2 * LRU_W, "dw_in_l"))[:, None],
            'ping': rows(_matmul_tn(x1, dhg, GDN_IN, "dw_in_g"))[:, None],
            'ppp': jnp.transpose(_matmul_tn(p[l], dpj, D, "dw_ple_proj").reshape(PLE, NDEV, 128), (1, 0, 2))[:, None]})
        d, dg1, du1, a1, dy1, small['ln_ffn1_g'][l], small['ln_ffn1_b'][l] = _ffn_bwd(
            z1, d1, g1, u1, p384a, pda, v('ln_ffn1_g') + tok, 0, 0)
        tok = emit(l, 'f1', {
            'p384': jnp.stack([_matmul_tn(x0, dg1, FSP, "dw_gate", NDEV), _matmul_tn(x0, du1, FSP, "dw_up", NDEV)],
                              axis=1),
            'pd': rows(_matmul_tn(a1, dy1, D, "dw_down"))[:, None]})
    small = {n: jnp.stack([g.reshape(sm[n].shape[1:]) for g in gs]) for n, gs in small.items()}
    return loss, d, small


def _pack_big(ws, dtype=bf16):
    padc = lambda a, n: jnp.pad(a, ((0, 0), (0, 0), (0, n - a.shape[2])))
    padr = lambda a, n: jnp.pad(a, ((0, 0), (0, n - a.shape[1]), (0, 0)))
    per_layer = lambda arrs: jnp.stack(arrs, axis=1).reshape((-1,) + arrs[0].shape[1:])
    w_in = ws['w_in']
    out = {
        'p384': per_layer([padc(ws[n], FSP) for n in ('ffn1_w_gate', 'ffn1_w_up', 'ffn2_w_gate', 'ffn2_w_up')]),
        'pd': per_layer([padr(ws[n], FSP) for n in ('ffn1_w_down', 'ffn2_w_down')]),
        'pr': per_layer([w_in[:, :, 0:D], ws['w_out'], ws['ple_w_gate']]),
        'pinl': w_in[:, :, D:D + 2 * LRU_W],
        'ping': padc(w_in[:, :, D + 2 * LRU_W:D_IN], GDN_IN),
        'ppp': ws['ple_w_proj'],
    }
    return {k: a.astype(dtype) for k, a in out.items()}


def _exchange(arrays, scatter, name):
    n = len(arrays)

    def body(*refs):
        ins, outs = refs[:n], refs[n:2 * n]
        send_sems, recv_sems, local_sems = refs[2 * n:]
        x, y, c = lax.axis_index("x"), lax.axis_index("y"), lax.axis_index("c")
        me = 4 * x + 2 * y + c
        copies = []
        for i in range(n):
            src = ins[i].at[me] if scatter[i] else ins[i]
            cp = pltpu.make_async_copy(src, outs[i].at[me], local_sems.at[i])
            cp.start()
            copies.append(cp)
        sends = []
        for j in range(1, NDEV):
            peer = (me + j) % NDEV
            pid = (peer // 4, (peer // 2) % 2, peer % 2)
            for i in range(n):
                src = ins[i].at[peer] if scatter[i] else ins[i]
                cp = pltpu.make_async_remote_copy(
                    src_ref=src, dst_ref=outs[i].at[me], send_sem=send_sems.at[i, j - 1],
                    recv_sem=recv_sems.at[i, j - 1], device_id=pid, device_id_type=pl.DeviceIdType.MESH)
                cp.start()
                sends.append(cp)
        for j in range(1, NDEV):
            source = (me + NDEV - j) % NDEV
            sid = (source // 4, (source // 2) % 2, source % 2)
            for i in range(n):
                src = ins[i].at[me] if scatter[i] else ins[i]
                pltpu.make_async_remote_copy(
                    src_ref=src, dst_ref=outs[i].at[source], send_sem=send_sems.at[i, j - 1],
                    recv_sem=recv_sems.at[i, j - 1], device_id=sid, device_id_type=pl.DeviceIdType.MESH).wait_recv()
        for cp in sends:
            cp.wait_send()
        for cp in copies:
            cp.wait()

    hbm = pl.BlockSpec(memory_space=pltpu.HBM)
    out_shape = [jax.ShapeDtypeStruct(a.shape if s else (NDEV,) + a.shape, a.dtype) for a, s in zip(arrays, scatter)]
    return pl.pallas_call(
        body, in_specs=[hbm] * n, out_specs=[hbm] * n, out_shape=out_shape,
        scratch_shapes=[pltpu.SemaphoreType.DMA((n, NDEV - 1)), pltpu.SemaphoreType.DMA((n, NDEV - 1)),
                        pltpu.SemaphoreType.DMA((n,))],
        compiler_params=pltpu.CompilerParams(has_side_effects=True), name=name)(*arrays)


def _gather_two_level(arrays, name):
    n = len(arrays)

    def body(*refs):
        ins, outs = refs[:n], refs[n:2 * n]
        send_sems, recv_sems, local_sems = refs[2 * n:]
        x, y, c = lax.axis_index("x"), lax.axis_index("y"), lax.axis_index("c")
        me, sibling = (x, y, c), (x, y, 1 - c)
        chips = [(1 - x, y), (x, 1 - y), (1 - x, 1 - y)]
        slot = lambda d: 4 * d[0] + 2 * d[1] + d[2]

        def copy(i, k, block, to, src=None):
            return pltpu.make_async_remote_copy(
                src_ref=outs[i].at[slot(block)] if src is None else src, dst_ref=outs[i].at[slot(block)],
                send_sem=send_sems.at[i, k], recv_sem=recv_sems.at[i, k], device_id=to,
                device_id_type=pl.DeviceIdType.MESH)

        mine, first, passed = [], [], []
        for i in range(n):
            cp = pltpu.make_async_copy(ins[i], outs[i].at[slot(me)], local_sems.at[i])
            cp.start()
            mine.append(cp)
            first.append(copy(i, 0, me, sibling, src=ins[i]))
            first += [copy(i, 1 + j, me, (*chip, c), src=ins[i]) for j, chip in enumerate(chips)]
        for cp in first:
            cp.start()
        for i in range(n):
            for j, chip in enumerate(chips):
                copy(i, 1 + j, (*chip, c), me).wait_recv()
                cp = copy(i, 4 + j, (*chip, c), sibling)
                cp.start()
                passed.append(cp)
        for i in range(n):
            copy(i, 0, sibling, me).wait_recv()
            for j, chip in enumerate(chips):
                copy(i, 4 + j, (*chip, 1 - c), me).wait_recv()
        for cp in first + passed:
            cp.wait_send()
        for cp in mine:
            cp.wait()

    hbm = pl.BlockSpec(memory_space=pltpu.HBM)
    return pl.pallas_call(
        body, in_specs=[hbm] * n, out_specs=[hbm] * n,
        out_shape=[jax.ShapeDtypeStruct((NDEV,) + a.shape, a.dtype) for a in arrays],
        scratch_shapes=[pltpu.SemaphoreType.DMA((n, NDEV - 1)), pltpu.SemaphoreType.DMA((n, NDEV - 1)),
                        pltpu.SemaphoreType.DMA((n,))],
        compiler_params=pltpu.CompilerParams(has_side_effects=True), name=name)(*arrays)


def _scatter_pairs(arrays, name):
    n = len(arrays)

    def body(*refs):
        ins, gots = refs[:n], refs[n:2 * n]
        send_sems, recv_sems = refs[2 * n:]
        x, y, c = lax.axis_index("x"), lax.axis_index("y"), lax.axis_index("c")
        sends = []
        for i in range(n):
            for q in range(4):
                cp = pltpu.make_async_remote_copy(
                    src_ref=ins[i].at[2 * q + 1 - c], dst_ref=gots[i].at[q], send_sem=send_sems.at[i, q],
                    recv_sem=recv_sems.at[i, q], device_id=(x, y, 1 - c), device_id_type=pl.DeviceIdType.MESH)
                cp.start()
                sends.append(cp)
        for cp in sends:
            cp.wait_recv()
        for cp in sends:
            cp.wait_send()

    hbm = pl.BlockSpec(memory_space=pltpu.HBM)
    return pl.pallas_call(
        body, in_specs=[hbm] * n, out_specs=[hbm] * n,
        out_shape=[jax.ShapeDtypeStruct((4,) + a.shape[1:], a.dtype) for a in arrays],
        scratch_shapes=[pltpu.SemaphoreType.DMA((n, 4)), pltpu.SemaphoreType.DMA((n, 4))],
        compiler_params=pltpu.CompilerParams(has_side_effects=True), name=name)(*arrays)


def _pair_sum(own, got, name):
    def body(a_ref, b_ref, o_ref):
        o_ref[...] = (a_ref[...].astype(f32) + b_ref[...].astype(f32)).astype(bf16)

    spec = pl.BlockSpec((None, None) + own.shape[2:], lambda q, s: (q, s, 0, 0))
    return pl.pallas_call(
        body, grid=own.shape[:2], in_specs=[spec, spec], out_specs=spec,
        out_shape=jax.ShapeDtypeStruct(own.shape, bf16),
        compiler_params=_cparams(("arbitrary", "arbitrary")), name=name)(own, got)


def _scatter_chips(arrays, name):
    n = len(arrays)

    def body(*refs):
        ins, outs = refs[:n], refs[n:2 * n]
        send_sems, recv_sems, local_sems = refs[2 * n:]
        x, y, c = lax.axis_index("x"), lax.axis_index("y"), lax.axis_index("c")
        chip = 2 * x + y
        keeps, sends = [], []
        for i in range(n):
            cp = pltpu.make_async_copy(ins[i].at[chip], outs[i].at[chip], local_sems.at[i])
            cp.start()
            keeps.append(cp)
        for j in range(1, 4):
            peer = (chip + j) % 4
            for i in range(n):
                cp = pltpu.make_async_remote_copy(
                    src_ref=ins[i].at[peer], dst_ref=outs[i].at[chip], send_sem=send_sems.at[i, j - 1],
                    recv_sem=recv_sems.at[i, j - 1], device_id=(peer // 2, peer % 2, c),
                    device_id_type=pl.DeviceIdType.MESH)
                cp.start()
                sends.append(cp)
        for j in range(1, 4):
            source = (chip + 4 - j) % 4
            for i in range(n):
                pltpu.make_async_remote_copy(
                    src_ref=ins[i].at[chip], dst_ref=outs[i].at[source], send_sem=send_sems.at[i, j - 1],
                    recv_sem=recv_sems.at[i, j - 1], device_id=(source // 2, source % 2, c),
                    device_id_type=pl.DeviceIdType.MESH).wait_recv()
        for cp in sends:
            cp.wait_send()
        for cp in keeps:
            cp.wait()

    hbm = pl.BlockSpec(memory_space=pltpu.HBM)
    return pl.pallas_call(
        body, in_specs=[hbm] * n, out_specs=[hbm] * n,
        out_shape=[jax.ShapeDtypeStruct(a.shape, a.dtype) for a in arrays],
        scratch_shapes=[pltpu.SemaphoreType.DMA((n, 3)), pltpu.SemaphoreType.DMA((n, 3)),
                        pltpu.SemaphoreType.DMA((n,))],
        compiler_params=pltpu.CompilerParams(has_side_effects=True), name=name)(*arrays)


def _gather_plan(srcs, lands, x, y, c):
    me = 4 * x + 2 * y + c
    sends, arrivals = [], []
    for j in range(1, NDEV):
        peer, source = (me + j) % NDEV, (me + NDEV - j) % NDEV
        for i in range(len(srcs)):
            k = i * (NDEV - 1) + j - 1
            sends.append((srcs[i], lands[i].at[me], (peer // 4, (peer // 2) % 2, peer % 2), k))
            arrivals.append((srcs[i], lands[i].at[source], (source // 4, (source // 2) % 2, source % 2), k))
    return sends, arrivals


def _chips_plan(srcs, lands, x, y, c):
    chip = 2 * x + y
    sends, arrivals = [], []
    for j in range(1, 4):
        peer, source = (chip + j) % 4, (chip + 4 - j) % 4
        for i in range(len(srcs)):
            k = i * 3 + j - 1
            sends.append((srcs[i].at[peer], lands[i].at[chip], (peer // 2, peer % 2, c), k))
            arrivals.append((srcs[i].at[chip], lands[i].at[source], (source // 2, source % 2, c), k))
    return sends, arrivals


def _remote(entry, send_sems, recv_sems):
    src, dst, dev, k = entry
    return pltpu.make_async_remote_copy(src_ref=src, dst_ref=dst, send_sem=send_sems.at[k], recv_sem=recv_sems.at[k],
                                        device_id=dev, device_id_type=pl.DeviceIdType.MESH)


_HBM = pl.BlockSpec(memory_space=pltpu.HBM)
_SEM = pl.BlockSpec(memory_space=pltpu.SEMAPHORE)


def _split_start(arrays, land_shapes, plan, npeer, name):
    n = len(arrays)

    def body(*refs):
        srcs, lands = refs[:n], refs[n:2 * n]
        send_sems, recv_sems, token = refs[2 * n], refs[2 * n + 1], refs[-1]
        sends, _ = plan(srcs, lands, lax.axis_index("x"), lax.axis_index("y"), lax.axis_index("c"))
        for entry in sends:
            _remote(entry, send_sems, recv_sems).start()
        token[...] = jnp.zeros_like(token)

    lands = [lax.empty(s, a.dtype) for s, a in zip(land_shapes, arrays)]
    thru = [pltpu.HBM(a.shape, a.dtype) for a in arrays + lands]
    out = pl.pallas_call(
        body, name=name, in_specs=[_HBM] * (2 * n),
        out_specs=(_SEM, _SEM, *([_HBM] * (2 * n)), pl.BlockSpec(memory_space=pltpu.VMEM)),
        out_shape=(pltpu.SemaphoreType.DMA((n * npeer,)), pltpu.SemaphoreType.DMA((n * npeer,)), *thru,
                   jax.ShapeDtypeStruct((8, 128), f32)),
        input_output_aliases={i: 2 + i for i in range(2 * n)},
        compiler_params=pltpu.CompilerParams(has_side_effects=pltpu.SideEffectType.DATAFLOW_SIDE_EFFECTING),
    )(*[pltpu.with_memory_space_constraint(a, pltpu.HBM) for a in arrays + lands])
    return out[0], out[1], list(out[2:2 + n]), list(out[2 + n:2 + 2 * n]), out[-1]


def _split_wait(send_sems, recv_sems, srcs, lands, after, plan, name):
    n = len(srcs)

    def body(*refs):
        s_refs, l_refs = refs[:n], refs[n:2 * n]
        ssem, rsem = refs[2 * n], refs[2 * n + 1]
        sends, arrivals = plan(s_refs, l_refs, lax.axis_index("x"), lax.axis_index("y"), lax.axis_index("c"))
        for entry in sends:
            _remote(entry, ssem, rsem).wait_send()
        for entry in arrivals:
            _remote(entry, ssem, rsem).wait_recv()

    out = pl.pallas_call(
        body, name=name, in_specs=[_HBM] * (2 * n) + [_SEM, _SEM, pl.BlockSpec(memory_space=pl.ANY)],
        out_specs=[_HBM] * (2 * n), out_shape=[pltpu.HBM(a.shape, a.dtype) for a in srcs + lands],
        input_output_aliases={i: i for i in range(2 * n)},
        compiler_params=pltpu.CompilerParams(has_side_effects=pltpu.SideEffectType.DATAFLOW_SIDE_EFFECTING),
    )(*srcs, *lands, send_sems, recv_sems, after)
    return list(out[:n]), list(out[n:])


def _adam_math(w, g, m, v):
    m2 = ADAM_B1 * m + (1.0 - ADAM_B1) * g
    v2 = ADAM_B2 * v + (1.0 - ADAM_B2) * (g * g)
    m_hat = m2 / (1.0 - ADAM_B1 ** ADAM_STEP)
    v_hat = v2 / (1.0 - ADAM_B2 ** ADAM_STEP)
    return -ADAM_LR * (m_hat / (jnp.sqrt(v_hat) + ADAM_EPS) + ADAM_WD * w), m2, v2


def _adam_big(parts, w, m, v, name):
    L, rows, cols = w.shape
    flat = [(a, slot) for layer_parts in parts for a, slot in layer_parts]
    per = len(parts[0])

    def body(*refs):
        prefs = refs[:len(flat)]
        w_ref, m_ref, v_ref, g_ref, d_ref, m2_ref, v2_ref = refs[len(flat):]
        for li in range(L):
            @pl.when(pl.program_id(0) == li)
            def _():
                c0 = 0
                for pref in prefs[li * per:(li + 1) * per]:
                    acc = pref[0].astype(f32)
                    for s in range(1, pref.shape[0]):
                        acc = acc + pref[s].astype(f32)
                    width = min(acc.shape[1], cols - c0)
                    g_ref[:, c0:c0 + width] = acc[0:rows, 0:width]
                    c0 += width

        d, m2, v2 = _adam_math(w_ref[...], g_ref[...], m_ref[...], v_ref[...])
        d_ref[...] = d
        m2_ref[...] = m2
        v2_ref[...] = v2

    wspec = pl.BlockSpec((None, rows, cols), lambda l: (l, 0, 0))
    in_specs = [pl.BlockSpec((a.shape[0], None) + a.shape[2:], functools.partial(lambda l, slot: (0, slot, 0, 0), slot=slot))
                for a, slot in flat]
    return pl.pallas_call(
        body, grid=(L,), in_specs=in_specs + [wspec] * 3, out_specs=[wspec] * 4,
        out_shape=[jax.ShapeDtypeStruct(w.shape, f32)] * 4,
        compiler_params=_cparams(("arbitrary",)), name=name)(*[a for a, _ in flat], w, m, v)


def _sum_sources(stacked):
    rows = stacked.shape[1]

    def body(s_ref, o_ref):
        acc = s_ref[0]
        for s in range(1, NDEV):
            acc = acc + s_ref[s]
        o_ref[...] = acc

    return pl.pallas_call(body, out_shape=jax.ShapeDtypeStruct((rows, 128), f32), name="sum_small_grads")(stacked)


def _adam_small(w, g, m, v):
    def body(w_ref, g_ref, m_ref, v_ref, d_ref, m2_ref, v2_ref):
        d, m2, v2 = _adam_math(w_ref[...], g_ref[...], m_ref[...], v_ref[...])
        d_ref[...] = d
        m2_ref[...] = m2
        v2_ref[...] = v2

    return pl.pallas_call(body, out_shape=[jax.ShapeDtypeStruct(w.shape, f32)] * 3, name="adam_small")(w, g, m, v)


def _pack_rows(arrs):
    flat = []
    for a in arrs:
        a = a.reshape(-1)
        flat.append(jnp.pad(a, (0, (-a.shape[0]) % 1024)))
    return jnp.concatenate(flat).reshape(-1, 128)


def _unpack_rows(packed, shapes):
    out, off = [], 0
    flat = packed.reshape(-1)
    for s in shapes:
        n = math.prod(s)
        out.append(flat[off:off + n].reshape(s))
        off += n + (-n) % 1024
    return out


def _gather_conv(gathered, shape):
    L, K, c = shape
    return jnp.transpose(gathered, (1, 2, 0, 3)).reshape(L, K, NDEV * c)


def kernel(x, p, positions, ln_ffn1_g, ln_ffn1_b, ffn1_w_gate, ffn1_w_up, ffn1_w_down, w_in, ret_norm_g, lru_conv_w, lru_conv_b, lru_w_a, lru_b_a, lru_w_x, lru_b_x, lru_lambda, gdn_conv_w, gdn_a_log, gdn_dt_bias, gdn_norm_g, w_out, ln_mix_g, ln_mix_b, ffn2_w_gate, ffn2_w_up, ffn2_w_down, ple_w_gate, ple_w_proj, ln_ffn2_g, ln_ffn2_b, loss_target, m_ln_ffn1_g, m_ln_ffn1_b, m_ffn1_w_gate, m_ffn1_w_up, m_ffn1_w_down, m_w_in, m_ret_norm_g, m_lru_conv_w, m_lru_conv_b, m_lru_w_a, m_lru_b_a, m_lru_w_x, m_lru_b_x, m_lru_lambda, m_gdn_conv_w, m_gdn_a_log, m_gdn_dt_bias, m_gdn_norm_g, m_w_out, m_ln_mix_g, m_ln_mix_b, m_ffn2_w_gate, m_ffn2_w_up, m_ffn2_w_down, m_ple_w_gate, m_ple_w_proj, m_ln_ffn2_g, m_ln_ffn2_b, v_ln_ffn1_g, v_ln_ffn1_b, v_ffn1_w_gate, v_ffn1_w_up, v_ffn1_w_down, v_w_in, v_ret_norm_g, v_lru_conv_w, v_lru_conv_b, v_lru_w_a, v_lru_b_a, v_lru_w_x, v_lru_b_x, v_lru_lambda, v_gdn_conv_w, v_gdn_a_log, v_gdn_dt_bias, v_gdn_norm_g, v_w_out, v_ln_mix_g, v_ln_mix_b, v_ffn2_w_gate, v_ffn2_w_up, v_ffn2_w_down, v_ple_w_gate, v_ple_w_proj, v_ln_ffn2_g, v_ln_ffn2_b):
    args = locals()
    W = {n: args[n] for n in WEIGHTS}
    M = {n: args['m_' + n] for n in WEIGHTS}
    V = {n: args['v_' + n] for n in WEIGHTS}
    me = 4 * lax.axis_index("x") + 2 * lax.axis_index("y") + lax.axis_index("c")

    core = lax.axis_index("c")
    chip = 2 * lax.axis_index("x") + lax.axis_index("y")

    packed = _pack_big(W)

    def group(l, name):
        per = {k: packed[k].shape[0] // DEPTH for k in PACKS}
        if name == 'f1':
            return [packed['p384'][l * per['p384']:l * per['p384'] + 2], packed['pd'][l * per['pd']:l * per['pd'] + 1]]
        return [packed['p384'][l * per['p384'] + 2:(l + 1) * per['p384']],
                packed['pd'][l * per['pd'] + 1:(l + 1) * per['pd']]] + [
                    packed[k][l * per[k]:(l + 1) * per[k]] for k in PACKS[2:]]

    def as_weights(arrs):
        G = dict(zip(PACKS, arrs))
        if 'ppp' in G:
            G['wpp'] = jnp.transpose(G.pop('ppp'), (1, 2, 0, 3)).reshape(PLE, D)
        return G

    conv_pack = _pack_rows([W[n] for n in CONV_SHARDED])
    g0 = _gather_two_level(group(0, 'f1') + [conv_pack], "gather_weights")
    g0, rest0 = lax.optimization_barrier((g0, group(0, 'rest')))
    start0 = _split_start(rest0, [(NDEV,) + a.shape for a in rest0], _gather_plan, NDEV - 1, "gather_start_0")
    tok0, all1 = lax.optimization_barrier((start0[4], group(1, 'f1') + group(1, 'rest')))
    start1 = _split_start(all1, [(NDEV,) + a.shape for a in all1], _gather_plan, NDEV - 1, "gather_start_1")
    arrived = {}

    def gather_done(started, after, name):
        srcs, lands = _split_wait(started[0], started[1], started[2], started[3], after, _gather_plan, name)
        return [lax.dynamic_update_slice_in_dim(ld, s[None], me, axis=0) for s, ld in zip(srcs, lands)]

    def fetch(l, name, after):
        if l == 0 and name == 'f1':
            return as_weights(g0[:-1]), tok0[0, 0] + start1[4][0, 0]
        if l == 0:
            return as_weights(gather_done(start0, after, "gather_wait_0")), 0.0
        if name == 'f1':
            arrived[1] = gather_done(start1, after, "gather_wait_1")
            return as_weights(arrived[1][:2]), 0.0
        return as_weights(arrived[1][2:]), 0.0

    conv_all = g0[-1]
    sm = {n: W[n] for n in SMALL}
    conv_shards = [_unpack_rows(conv_all[s], [W[n].shape for n in CONV_SHARDED]) for s in range(NDEV)]
    for i, n in enumerate(CONV_SHARDED):
        sm[n] = _gather_conv(jnp.stack([cs[i] for cs in conv_shards]), W[n].shape)

    received, started = {}, {}

    def emit(l, name, grads):
        keys = list(grads)
        arrs = [grads[k] for k in keys]
        gots = _scatter_pairs(arrs, "scatter_pairs")
        owns = [lax.dynamic_index_in_dim(a.reshape((4, 2) + a.shape[1:]), core, axis=1, keepdims=False) for a in arrs]
        pair = [_pair_sum(o, g, "pair_sum_" + k) for k, o, g in zip(keys, owns, gots)]
        if (l, name) == (0, 'f1'):
            received[l, name] = dict(zip(keys, _scatter_chips(pair, "scatter_chips")))
            return 0.0
        started[l, name] = (keys, _split_start(pair, [a.shape for a in pair], _chips_plan, 3,
                                               f"scatter_start_{l}_{name}"))
        return started[l, name][1][4][0, 0]

    loss, grad_x, small = _local_step(x[0], p[:, 0], positions.reshape(-1, 1), loss_target[0], fetch, emit, sm)
    loss = lax.psum(loss[0, 0], ("x", "y", "c"))
    for (l, name), (keys, st) in started.items():
        srcs, lands = _split_wait(st[0], st[1], st[2], st[3], grad_x, _chips_plan, f"scatter_wait_{l}_{name}")
        received[l, name] = dict(zip(keys, [
            lax.dynamic_update_slice_in_dim(ld, lax.dynamic_index_in_dim(s, chip, axis=0), chip, axis=0)
            for s, ld in zip(srcs, lands)]))

    small_pack = _pack_rows([small[n] for n in SMALL])
    small_all = _exchange([small_pack], [False], "gather_small_grads")[0]
    small_sum = _unpack_rows(_sum_sources(small_all), [small[n].shape for n in SMALL])
    grads, delta, new_m, new_v = {}, {}, {}, {}
    for n, g in zip(SMALL, small_sum):
        if n in CONV_SHARDED:
            c = W[n].shape[2]
            g = lax.dynamic_slice_in_dim(g, me * c, c, axis=2)
        grads[n] = g

    big_parts = {
        'ffn1_w_gate': [('f1', 'p384', 0)], 'ffn1_w_up': [('f1', 'p384', 1)], 'ffn1_w_down': [('f1', 'pd', 0)],
        'ffn2_w_gate': [('rest', 'p384', 0)], 'ffn2_w_up': [('rest', 'p384', 1)], 'ffn2_w_down': [('rest', 'pd', 0)],
        'w_in': [('rest', 'pr', 0), ('rest', 'pinl', 0), ('rest', 'ping', 0)], 'w_out': [('rest', 'pr', 1)],
        'ple_w_gate': [('rest', 'pr', 2)], 'ple_w_proj': [('rest', 'ppp', 0)],
    }
    for n in BIG:
        parts = [[(received[l, grp][k], slot) for grp, k, slot in big_parts[n]] for l in range(DEPTH)]
        grads[n], delta[n], new_m[n], new_v[n] = _adam_big(parts, W[n], M[n], V[n], "adam_" + n)
    shapes = [W[n].shape for n in SMALL]
    d_s, m_s, v_s = _adam_small(*[_pack_rows([src[n] for n in SMALL]) for src in (W, grads, M, V)])
    for n, dd, mm, vv in zip(SMALL, _unpack_rows(d_s, shapes), _unpack_rows(m_s, shapes), _unpack_rows(v_s, shapes)):
        delta[n], new_m[n], new_v[n] = dd, mm, vv

    return (loss, grad_x[None], *[grads[n] for n in WEIGHTS], *[delta[n] for n in WEIGHTS],
            *[new_m[n] for n in WEIGHTS], *[new_v[n] for n in WEIGHTS])
```

```python
import functools
import math

import jax
import jax.numpy as jnp
from jax import lax
from jax.experimental import pallas as pl
from jax.experimental.pallas import tpu as pltpu

f32 = jnp.float32
bf16 = jnp.bfloat16

NDEV = 8
DEPTH = 2
D = 1024
FS = 352
FSP = 384
FB = 2
NF = NDEV // FB
PLE = 256
CH = 64
RET_H, GDN_H = 4, 6
RET_W, LRU_W, GDN_W = 256, 384, 384
GDN_IN = 1664
GDN_IN_REAL = 1548
D_IN = 3340
ALPHA = 4.0 ** 0.25
LN_EPS = 1e-5
ROPE_THETA = 10000.0
TM = 512
RB_RET, RB_LRU, RB_GDN = 512, 512, 256
VMEM_LIMIT = 56 * 1024 * 1024
ADAM_LR, ADAM_B1, ADAM_B2, ADAM_EPS, ADAM_WD, ADAM_STEP = 0.001, 0.9, 0.999, 1e-08, 0.01, 10

WEIGHTS = ['ln_ffn1_g', 'ln_ffn1_b', 'ffn1_w_gate', 'ffn1_w_up', 'ffn1_w_down', 'w_in', 'ret_norm_g', 'lru_conv_w',
           'lru_conv_b', 'lru_w_a', 'lru_b_a', 'lru_w_x', 'lru_b_x', 'lru_lambda', 'gdn_conv_w', 'gdn_a_log',
           'gdn_dt_bias', 'gdn_norm_g', 'w_out', 'ln_mix_g', 'ln_mix_b', 'ffn2_w_gate', 'ffn2_w_up', 'ffn2_w_down',
           'ple_w_gate', 'ple_w_proj', 'ln_ffn2_g', 'ln_ffn2_b']
BIG = ['ffn1_w_gate', 'ffn1_w_up', 'ffn1_w_down', 'w_in', 'w_out', 'ffn2_w_gate', 'ffn2_w_up', 'ffn2_w_down',
       'ple_w_gate', 'ple_w_proj']
SMALL = [n for n in WEIGHTS if n not in BIG]
PACKS = ('p384', 'pd', 'pr', 'pinl', 'ping', 'ppp')
CONV_SHARDED = {'lru_conv_w': LRU_W, 'gdn_conv_w': 3 * GDN_W}


def _cparams(sem=None):
    return pltpu.CompilerParams(dimension_semantics=sem, vmem_limit_bytes=VMEM_LIMIT)


def _sigmoid(x):
    return 1.0 / (1.0 + jnp.exp(-x))


def _silu(x):
    return x * _sigmoid(x)


def _dsilu(x):
    s = _sigmoid(x)
    return s * (1.0 + x * (1.0 - s))


def _softplus(x):
    return jnp.maximum(x, 0.0) + jnp.log(1.0 + jnp.exp(-jnp.abs(x)))


def _gelu(x):
    return 0.5 * x * (1.0 + jnp.tanh(0.7978845608028654 * (x + 0.044715 * x * x * x)))


def _dot(a, b):
    return jnp.dot(a.astype(bf16), b.astype(bf16), preferred_element_type=f32)


def _dot_nt(a, b):
    return lax.dot_general(a.astype(bf16), b.astype(bf16), (((1,), (1,)), ((), ())), preferred_element_type=f32)


def _dot_tn(a, b):
    return lax.dot_general(a.astype(bf16), b.astype(bf16), (((0,), (0,)), ((), ())), preferred_element_type=f32)


def _bmm(eq, a, b):
    return jnp.einsum(eq, a.astype(bf16), b.astype(bf16), preferred_element_type=f32)


def _split3(a):
    a1 = a.astype(bf16)
    r = a - a1.astype(f32)
    a2 = r.astype(bf16)
    return a1, a2, (r - a2.astype(f32)).astype(bf16)


def _bmm3(eq, a, b):
    a1, a2, _ = _split3(a)
    b1, b2, _ = _split3(b)
    e = lambda x, y: jnp.einsum(eq, x, y, preferred_element_type=f32)
    return e(a1, b1) + (e(a1, b2) + e(a2, b1))


def _rowsum(x):
    x1, x2, _ = _split3(x)
    ones = jnp.ones((x.shape[0], CH, CH), bf16)
    e = lambda y: jnp.einsum('bij,bjk->bik', y, ones, preferred_element_type=f32)
    return e(x1) + e(x2)


def _tri_ones(B, upper=False):
    ii = lax.broadcasted_iota(jnp.int32, (B, CH, CH), 1)
    jj = lax.broadcasted_iota(jnp.int32, (B, CH, CH), 2)
    return jnp.where((ii <= jj) if upper else (ii >= jj), 1.0, 0.0).astype(bf16)


def _cumsum_mm(t, x):
    x1, x2, x3 = _split3(x)
    e = lambda y: jnp.einsum('bij,bjk->bik', t, y, preferred_element_type=f32)
    return e(x1) + (e(x2) + e(x3))


def _chunk_cumsum(x, reverse=False):
    n = x.shape[0] // CH
    return _cumsum_mm(_tri_ones(n, upper=reverse), x.reshape(n, CH, 128)).reshape(x.shape)


@jax.custom_vjp
def _neumann_inv(m):
    ii = lax.broadcasted_iota(jnp.int32, m.shape, 1)
    jj = lax.broadcasted_iota(jnp.int32, m.shape, 2)
    inv = jnp.where(ii == jj, 1.0, 0.0).astype(f32) + m
    mp = m
    for _ in range(5):
        mp = _bmm3('bij,bjk->bik', mp, mp)
        inv = inv + _bmm3('bij,bjk->bik', inv, mp)
    return inv


def _neumann_inv_fwd(m):
    inv = _neumann_inv(m)
    return inv, inv


def _neumann_inv_bwd(inv, g):
    return (_bmm3('bij,bkj->bik', _bmm3('bji,bjk->bik', inv, g), inv),)


_neumann_inv.defvjp(_neumann_inv_fwd, _neumann_inv_bwd)


@jax.custom_vjp
def _known_inv(m, inv):
    return inv


def _known_inv_fwd(m, inv):
    return inv, inv


def _known_inv_bwd(inv, g):
    return _neumann_inv_bwd(inv, g)[0], jnp.zeros_like(inv)


_known_inv.defvjp(_known_inv_fwd, _known_inv_bwd)


def _ln_stats(z):
    mu = jnp.mean(z, -1, keepdims=True)
    zc = z - mu
    rstd = lax.rsqrt(jnp.mean(zc * zc, -1, keepdims=True) + LN_EPS)
    return zc * rstd, rstd


def _ln_bwd(z, g, dout):
    xh, rstd = _ln_stats(z)
    dxh = dout * g
    dz = rstd * (dxh - jnp.mean(dxh, -1, keepdims=True) - xh * jnp.mean(dxh * xh, -1, keepdims=True))
    return dz, jnp.sum(dout * xh, 0, keepdims=True), jnp.sum(dout, 0, keepdims=True)


def _full_spec(shape):
    nd = len(shape)
    return pl.BlockSpec(shape, lambda *_: (0,) * nd)


def _ffn_fwd(x, p384, pd, lg, lb, slot, which, ple=None):
    T = x.shape[0]
    sg, su, sd = 2 * slot, 2 * slot + 1, slot
    has_ple = ple is not None

    def body(*refs):
        if has_ple:
            (x_ref, wg_ref, wu_ref, wd_ref, lg_ref, lb_ref, p_ref, wpg_ref, wpp_ref,
             z_ref, o_ref, g_ref, u_ref, acc, xb_s) = refs
        else:
            x_ref, wg_ref, wu_ref, wd_ref, lg_ref, lb_ref, z_ref, o_ref, g_ref, u_ref, acc, xb_s = refs
        f = pl.program_id(1)

        @pl.when(f == 0)
        def _():
            x = x_ref[...]
            xb = x.astype(bf16)
            xb_s[...] = xb
            base = ALPHA * x
            if has_ple:
                gate = _sigmoid(_dot(xb, wpg_ref[...].reshape(D, D)))
                base = base + gate * _dot(p_ref[...], wpp_ref[...])
            acc[...] = base

        xb = xb_s[...]
        g = _dot(xb, wg_ref[...])
        u = _dot(xb, wu_ref[...])
        g_ref[...] = g.astype(bf16)
        u_ref[...] = u.astype(bf16)
        acc[...] += 0.5 * _dot(_silu(g) * u, wd_ref[...].reshape(FB * FSP, D))

        @pl.when(f == NF - 1)
        def _():
            z = acc[...]
            z_ref[...] = z
            o_ref[...] = _ln_stats(z)[0] * lg_ref[...] + lb_ref[...]

    row = pl.BlockSpec((TM, D), lambda i, f: (i, 0))
    in_specs = [row,
                pl.BlockSpec((None, D, FB * FSP), lambda i, f: (sg, 0, f)),
                pl.BlockSpec((None, D, FB * FSP), lambda i, f: (su, 0, f)),
                pl.BlockSpec((FB, None, FSP, D), lambda i, f: (f, sd, 0, 0)),
                _full_spec((1, D)), _full_spec((1, D))]
    args = [x, p384, p384, pd, lg, lb]
    if has_ple:
        p, pr, wpp = ple
        in_specs += [pl.BlockSpec((TM, PLE), lambda i, f: (i, 0)),
                     pl.BlockSpec((NDEV, None, 128, D), lambda i, f: (0, 2, 0, 0)),
                     _full_spec((PLE, D))]
        args += [p, pr, wpp]
    hid = pl.BlockSpec((TM, FB * FSP), lambda i, f: (i, f))
    hshape = jax.ShapeDtypeStruct((T, NDEV * FSP), bf16)
    return pl.pallas_call(
        body, grid=(T // TM, NF), in_specs=in_specs, out_specs=[row, row, hid, hid],
        out_shape=[jax.ShapeDtypeStruct((T, D), f32)] * 2 + [hshape, hshape],
        scratch_shapes=[pltpu.VMEM((TM, D), f32), pltpu.VMEM((TM, D), bf16)],
        compiler_params=_cparams(("arbitrary", "arbitrary")), name=f"ffn{which + 1}_fwd")(*args)


def _ffn_bwd(z, dout, gs, us, p384, pd, lg, slot, which):
    T = z.shape[0]
    TMB = TM
    sg, su, sd = 2 * slot, 2 * slot + 1, slot

    def body(z_ref, do_ref, g_ref, u_ref, wg_ref, wu_ref, wd_ref, lg_ref,
             dx_ref, dg_ref, du_ref, a_ref, dy_ref, dlg_ref, dlb_ref, acc, dyb):
        i, f = pl.program_id(0), pl.program_id(1)

        @pl.when(jnp.logical_and(i == 0, f == 0))
        def _():
            dlg_ref[...] = jnp.zeros_like(dlg_ref)
            dlb_ref[...] = jnp.zeros_like(dlb_ref)

        @pl.when(f == 0)
        def _():
            dz, dlg, dlb = _ln_bwd(z_ref[...], lg_ref[...], do_ref[...])
            dlg_ref[...] += dlg
            dlb_ref[...] += dlb
            dy = (0.5 * dz).astype(bf16)
            dyb[...] = dy
            dy_ref[...] = dy
            acc[...] = ALPHA * dz

        g = g_ref[...].astype(f32)
        u = u_ref[...].astype(f32)
        da = _dot_nt(dyb[...], wd_ref[...].reshape(FB * FSP, D))
        sgm = _sigmoid(g)
        dg = (da * u * (sgm * (1.0 + g * (1.0 - sgm)))).astype(bf16)
        du = (da * (g * sgm)).astype(bf16)
        dg_ref[...] = dg
        du_ref[...] = du
        a_ref[...] = (g * sgm * u).astype(bf16)
        acc[...] += _dot_nt(dg, wg_ref[...]) + _dot_nt(du, wu_ref[...])

        @pl.when(f == NF - 1)
        def _():
            dx_ref[...] = acc[...]

    row = pl.BlockSpec((TMB, D), lambda i, f: (i, 0))
    hid = pl.BlockSpec((TMB, FB * FSP), lambda i, f: (i, f))
    vec = _full_spec((1, D))
    in_specs = [row, row, hid, hid,
                pl.BlockSpec((None, D, FB * FSP), lambda i, f: (sg, 0, f)),
                pl.BlockSpec((None, D, FB * FSP), lambda i, f: (su, 0, f)),
                pl.BlockSpec((FB, None, FSP, D), lambda i, f: (f, sd, 0, 0)),
                vec]
    args = [z, dout, gs, us, p384, p384, pd, lg]
    out_specs = [row, hid, hid, hid, row, vec, vec]
    hshape = jax.ShapeDtypeStruct((T, NDEV * FSP), bf16)
    out_shape = [jax.ShapeDtypeStruct((T, D), f32), hshape, hshape, hshape, jax.ShapeDtypeStruct((T, D), bf16),
                 jax.ShapeDtypeStruct((1, D), f32), jax.ShapeDtypeStruct((1, D), f32)]
    return pl.pallas_call(
        body, grid=(T // TMB, NF), in_specs=in_specs, out_specs=out_specs, out_shape=out_shape,
        scratch_shapes=[pltpu.VMEM((TMB, D), f32), pltpu.VMEM((TMB, D), bf16)],
        compiler_params=_cparams(("arbitrary", "arbitrary")), name=f"ffn{which + 1}_bwd")(*args)


def _ple_bwd(x, p, dy, dx_ffn, pr, wpp, layer):
    T = x.shape[0]

    def body(x_ref, p_ref, dy_ref, dxf_ref, wpg_ref, wpp_ref, dx_ref, dgp_ref, dpj_ref):
        dz = 2.0 * dy_ref[...].astype(f32)
        wpg = wpg_ref[...].reshape(D, D)
        gate = _sigmoid(_dot(x_ref[...], wpg))
        proj = _dot(p_ref[...], wpp_ref[...])
        dgp = (dz * proj * gate * (1.0 - gate)).astype(bf16)
        dgp_ref[...] = dgp
        dpj_ref[...] = (dz * gate).astype(bf16)
        dx_ref[...] = dxf_ref[...] + _dot_nt(dgp, wpg)

    row = pl.BlockSpec((TM, D), lambda i: (i, 0))
    return pl.pallas_call(
        body, grid=(T // TM,),
        in_specs=[row, pl.BlockSpec((TM, PLE), lambda i: (i, 0)), row, row,
                  pl.BlockSpec((NDEV, None, 128, D), lambda i: (0, 3 * layer + 2, 0, 0)), _full_spec((PLE, D))],
        out_specs=[row, row, row],
        out_shape=[jax.ShapeDtypeStruct((T, D), f32), jax.ShapeDtypeStruct((T, D), bf16),
                   jax.ShapeDtypeStruct((T, D), bf16)],
        compiler_params=_cparams(("arbitrary",)), name="ple_bwd")(x, p, dy, dx_ffn, pr, wpp)


def _matmul_tn(a, b, nb, name, nsub=1):
    T, M = a.shape
    N = b.shape[1]
    wide = nsub * nb
    tk = min(T, 1024 if wide <= 2048 else 512)
    nk = T // tk

    def body(a_ref, b_ref, o_ref, acc):
        k = pl.program_id(1)

        @pl.when(k == 0)
        def _():
            acc[...] = jnp.zeros_like(acc)

        acc[...] += _dot_tn(a_ref[...], b_ref[...])

        @pl.when(k == nk - 1)
        def _():
            for j in range(nsub):
                o_ref[j] = acc[:, j * nb:(j + 1) * nb].astype(bf16)

    return pl.pallas_call(
        body, grid=(N // wide, nk),
        in_specs=[pl.BlockSpec((tk, M), lambda n, k: (k, 0)), pl.BlockSpec((tk, wide), lambda n, k: (k, n))],
        out_specs=pl.BlockSpec((nsub, M, nb), lambda n, k: (n, 0, 0)),
        out_shape=jax.ShapeDtypeStruct((N // nb, M, nb), bf16),
        scratch_shapes=[pltpu.VMEM((M, wide), f32)],
        compiler_params=_cparams(("arbitrary", "arbitrary")), name=name)(a, b)


def _proj_in(x, pr, pinl, ping, layer):
    T = x.shape[0]

    def body(x_ref, wr_ref, wl_ref, wg_ref, hr_ref, hl_ref, hg_ref):
        xb = x_ref[...].astype(bf16)
        hr_ref[...] = _dot(xb, wr_ref[...].reshape(D, D))
        hl_ref[...] = _dot(xb, wl_ref[...].reshape(D, 2 * LRU_W))
        hg_ref[...] = _dot(xb, wg_ref[...].reshape(D, GDN_IN))

    return pl.pallas_call(
        body, grid=(T // TM,),
        in_specs=[pl.BlockSpec((TM, D), lambda i: (i, 0)),
                  pl.BlockSpec((NDEV, None, 128, D), lambda i: (0, 3 * layer, 0, 0)),
                  pl.BlockSpec((NDEV, None, 128, 2 * LRU_W), lambda i: (0, layer, 0, 0)),
                  pl.BlockSpec((NDEV, None, 128, GDN_IN), lambda i: (0, layer, 0, 0))],
        out_specs=[pl.BlockSpec((TM, D), lambda i: (i, 0)), pl.BlockSpec((TM, 2 * LRU_W), lambda i: (i, 0)),
                   pl.BlockSpec((TM, GDN_IN), lambda i: (i, 0))],
        out_shape=[jax.ShapeDtypeStruct((T, D), f32), jax.ShapeDtypeStruct((T, 2 * LRU_W), f32),
                   jax.ShapeDtypeStruct((T, GDN_IN), f32)],
        compiler_params=_cparams(("arbitrary",)), name="proj_in")(x, pr, pinl, ping)


def _proj_in_bwd(base, dhr, dhl, dhg, pr, pinl, ping, layer):
    T = base.shape[0]

    def body(b_ref, dr_ref, dl_ref, dg_ref, wr_ref, wl_ref, wg_ref, o_ref):
        o_ref[...] = (b_ref[...] + _dot_nt(dr_ref[...], wr_ref[...].reshape(D, D))
                      + _dot_nt(dl_ref[...], wl_ref[...].reshape(D, 2 * LRU_W))
                      + _dot_nt(dg_ref[...], wg_ref[...].reshape(D, GDN_IN)))

    return pl.pallas_call(
        body, grid=(T // TM,),
        in_specs=[pl.BlockSpec((TM, D), lambda i: (i, 0)), pl.BlockSpec((TM, D), lambda i: (i, 0)),
                  pl.BlockSpec((TM, 2 * LRU_W), lambda i: (i, 0)), pl.BlockSpec((TM, GDN_IN), lambda i: (i, 0)),
                  pl.BlockSpec((NDEV, None, 128, D), lambda i: (0, 3 * layer, 0, 0)),
                  pl.BlockSpec((NDEV, None, 128, 2 * LRU_W), lambda i: (0, layer, 0, 0)),
                  pl.BlockSpec((NDEV, None, 128, GDN_IN), lambda i: (0, layer, 0, 0))],
        out_specs=pl.BlockSpec((TM, D), lambda i: (i, 0)),
        out_shape=jax.ShapeDtypeStruct((T, D), f32),
        compiler_params=_cparams(("arbitrary",)), name="proj_in_bwd")(base, dhr, dhl, dhg, pr, pinl, ping)


def _mix_out(x1, o_r, o_l, o_g, pr, lg, lb, layer):
    T = x1.shape[0]

    def body(x_ref, r_ref, l_ref, g_ref, w_ref, lg_ref, lb_ref, z_ref, o_ref):
        w = w_ref[...].reshape(D, D)
        z = (ALPHA * x_ref[...] + _dot(r_ref[...], w[0:RET_W]) + _dot(l_ref[...], w[RET_W:RET_W + LRU_W])
             + _dot(g_ref[...], w[RET_W + LRU_W:D]))
        z_ref[...] = z
        o_ref[...] = _ln_stats(z)[0] * lg_ref[...] + lb_ref[...]

    row = pl.BlockSpec((TM, D), lambda i: (i, 0))
    return pl.pallas_call(
        body, grid=(T // TM,),
        in_specs=[row, pl.BlockSpec((TM, RET_W), lambda i: (i, 0)), pl.BlockSpec((TM, LRU_W), lambda i: (i, 0)),
                  pl.BlockSpec((TM, GDN_W), lambda i: (i, 0)),
                  pl.BlockSpec((NDEV, None, 128, D), lambda i: (0, 3 * layer + 1, 0, 0)),
                  _full_spec((1, D)), _full_spec((1, D))],
        out_specs=[row, row], out_shape=[jax.ShapeDtypeStruct((T, D), f32)] * 2,
        compiler_params=_cparams(("arbitrary",)), name="mix_out")(x1, o_r, o_l, o_g, pr, lg, lb)


def _mix_out_bwd(z, dout, pr, lg, layer):
    T = z.shape[0]

    def body(z_ref, do_ref, w_ref, lg_ref, dxb_ref, dzb_ref, dr_ref, dl_ref, dg_ref, dlg_ref, dlb_ref):
        @pl.when(pl.program_id(0) == 0)
        def _():
            dlg_ref[...] = jnp.zeros_like(dlg_ref)
            dlb_ref[...] = jnp.zeros_like(dlb_ref)

        dz, dlg, dlb = _ln_bwd(z_ref[...], lg_ref[...], do_ref[...])
        dlg_ref[...] += dlg
        dlb_ref[...] += dlb
        dxb_ref[...] = ALPHA * dz
        dzb = dz.astype(bf16)
        dzb_ref[...] = dzb
        w = w_ref[...].reshape(D, D)
        dr_ref[...] = _dot_nt(dzb, w[0:RET_W])
        dl_ref[...] = _dot_nt(dzb, w[RET_W:RET_W + LRU_W])
        dg_ref[...] = _dot_nt(dzb, w[RET_W + LRU_W:D])

    row = pl.BlockSpec((TM, D), lambda i: (i, 0))
    vec = _full_spec((1, D))
    return pl.pallas_call(
        body, grid=(T // TM,),
        in_specs=[row, row, pl.BlockSpec((NDEV, None, 128, D), lambda i: (0, 3 * layer + 1, 0, 0)), vec],
        out_specs=[row, row, pl.BlockSpec((TM, RET_W), lambda i: (i, 0)), pl.BlockSpec((TM, LRU_W), lambda i: (i, 0)),
                   pl.BlockSpec((TM, GDN_W), lambda i: (i, 0)), vec, vec],
        out_shape=[jax.ShapeDtypeStruct((T, D), f32), jax.ShapeDtypeStruct((T, D), bf16),
                   jax.ShapeDtypeStruct((T, RET_W), f32), jax.ShapeDtypeStruct((T, LRU_W), f32),
                   jax.ShapeDtypeStruct((T, GDN_W), f32), jax.ShapeDtypeStruct((1, D), f32),
                   jax.ShapeDtypeStruct((1, D), f32)],
        compiler_params=_cparams(("arbitrary",)), name="mix_out_bwd")(z, dout, pr, lg)


def _loss_grad(y, target):
    T = y.shape[0]

    def body(y_ref, t_ref, dy_ref, l_ref):
        @pl.when(pl.program_id(0) == 0)
        def _():
            l_ref[...] = jnp.zeros_like(l_ref)

        e = y_ref[...] - t_ref[...]
        dy_ref[...] = e * (1.0 / D)
        l_ref[...] += 0.5 * jnp.sum(jnp.sum(e * e, -1, keepdims=True) * (1.0 / D), 0, keepdims=True)

    row = pl.BlockSpec((TM, D), lambda i: (i, 0))
    return pl.pallas_call(
        body, grid=(T // TM,), in_specs=[row, row], out_specs=[row, _full_spec((1, 1))],
        out_shape=[jax.ShapeDtypeStruct((T, D), f32), jax.ShapeDtypeStruct((1, 1), f32)],
        compiler_params=_cparams(("arbitrary",)), name="loss_grad")(y, target)


def _split_heads(x, H):
    n = x.shape[0] // CH
    parts = [x[:, h * CH:(h + 1) * CH].reshape(n, CH, CH) for h in range(H)]
    return jnp.stack(parts, axis=1).reshape(n * H, CH, CH)


def _merge_heads(ref, x, H, col0=0):
    n = x.shape[0] // H
    x4 = x.reshape(n, H, CH, CH)
    for h in range(H):
        ref[:, col0 + h * CH:col0 + (h + 1) * CH] = x4[:, h].reshape(n * CH, CH)


def _rows_down(x, before, s):
    r8 = lax.broadcasted_iota(jnp.int32, before.shape, 0)
    top = jnp.where(r8 < s, pltpu.roll(before, s, 0), pltpu.roll(x[0:8], s, 0))
    return jnp.concatenate([top, pltpu.roll(x, s, 0)[8:]], axis=0)


def _rows_up(x, after, s):
    R = x.shape[0]
    r8 = lax.broadcasted_iota(jnp.int32, after.shape, 0)
    bottom = jnp.where(r8 >= 8 - s, pltpu.roll(after, 8 - s, 0), pltpu.roll(x[R - 8:R], 8 - s, 0))
    return jnp.concatenate([pltpu.roll(x, R - s, 0)[0:R - 8], bottom], axis=0)


def _conv_fwd(ext, x, tail, w, R):
    ext[0:8, :] = tail
    ext[8:R + 8, :] = x
    y = w[3:4, :] * x
    for k in range(3):
        y = y + w[k:k + 1, :] * _rows_down(x, tail, 3 - k)
    return y


def _conv_bwd(ext, dy, dy_next, w, R):
    x, tail = ext[8:8 + R, :], ext[0:8, :]
    dx = w[3:4, :] * dy
    dws = []
    for k in range(3):
        dx = dx + w[k:k + 1, :] * _rows_up(dy, dy_next, 3 - k)
        dws.append(jnp.sum(dy * _rows_down(x, tail, 3 - k), 0, keepdims=True))
    dws.append(jnp.sum(dy * x, 0, keepdims=True))
    return dx, jnp.concatenate(dws, axis=0)


def _prev_tail_spec(R, W):
    return pl.BlockSpec((8, W), lambda i: (jnp.maximum(i * (R // 8) - 1, 0), 0))


def _prev_tail_spec_rev(R, W, nb):
    return pl.BlockSpec((8, W), lambda i: (jnp.maximum((nb - 1 - i) * (R // 8) - 1, 0), 0))


def _rope_tables(positions):
    T = positions.shape[0]

    def body(p_ref, c_ref, s_ref):
        lane = lax.broadcasted_iota(jnp.int32, (TM, RET_W), 1)
        fi = (lane % 32).astype(f32)
        inv = jnp.exp(fi * (-math.log(ROPE_THETA) / 32.0))
        ang = p_ref[...].astype(f32) * inv
        c_ref[...] = jnp.cos(ang)
        s_ref[...] = jnp.where(lane % CH < 32, -jnp.sin(ang), jnp.sin(ang))

    row = pl.BlockSpec((TM, RET_W), lambda i: (i, 0))
    return pl.pallas_call(
        body, grid=(T // TM,), in_specs=[pl.BlockSpec((TM, 1), lambda i: (i, 0))], out_specs=[row, row],
        out_shape=[jax.ShapeDtypeStruct((T, RET_W), f32)] * 2,
        compiler_params=_cparams(("arbitrary",)), name="rope_tables")(positions)


def _partner(x):
    lane = lax.broadcasted_iota(jnp.int32, x.shape, 1)
    return jnp.where(lane % CH < 32, pltpu.roll(x, RET_W - 32, 1), pltpu.roll(x, 32, 1))


def _ret_consts():
    ii = lax.broadcasted_iota(jnp.int32, (CH, CH), 0).astype(f32)
    jj = lax.broadcasted_iota(jnp.int32, (CH, CH), 1).astype(f32)
    intra, cross, tail, cd = [], [], [], []
    for h in range(RET_H):
        lg = math.log1p(-(2.0 ** (-5.0 - h)))
        intra.append(jnp.exp(jnp.abs(ii - jj) * lg))
        cross.append(jnp.exp((ii + 1.0) * lg))
        tail.append(jnp.exp((CH - 1.0 - ii) * lg))
        cd.append(jnp.full((CH, CH), math.exp(CH * lg), f32))
    return jnp.stack(intra), jnp.stack(cross), jnp.stack(tail), jnp.stack(cd)


def _ret_chunk(consts, q, k, v, st):
    intra, cross, tail, cd = consts
    s = _bmm('hid,hjd->hij', q, k) * intra
    o = _bmm('hij,hje->hie', s, v) + _bmm('hid,hde->hie', q * cross, st)
    st2 = st * cd + _bmm('hjd,hje->hde', k * tail, v)
    oc = o - _rowsum(o) * (1.0 / CH)
    on = oc * lax.rsqrt(_rowsum(oc * oc) * (1.0 / CH) + 1e-5)
    return on, st2


def _ret_fwd(hr, cosw, sinw, gam):
    T = hr.shape[0]
    R = RB_RET
    nc = R // CH

    def body(h_ref, c_ref, s_ref, g_ref, o_ref, st_ref, st, wide):
        @pl.when(pl.program_id(0) == 0)
        def _():
            st[...] = jnp.zeros_like(st)

        consts = _ret_consts()
        cw, sw = c_ref[...], s_ref[...]
        q, k = h_ref[:, 0:RET_W], h_ref[:, RET_W:2 * RET_W]
        qh = _split_heads((q * cw + _partner(q) * sw) * 0.125, RET_H)
        kh = _split_heads(k * cw + _partner(k) * sw, RET_H)
        vh = _split_heads(h_ref[:, 2 * RET_W:3 * RET_W], RET_H)
        outs = []
        s_cur = st[...]
        for c in range(nc):
            sl = slice(c * RET_H, (c + 1) * RET_H)
            st_ref[c] = s_cur
            on, s_cur = _ret_chunk(consts, qh[sl], kh[sl], vh[sl], s_cur)
            outs.append(on)
        st[...] = s_cur
        _merge_heads(wide, jnp.concatenate(outs, axis=0), RET_H)
        o_ref[...] = wide[...] * g_ref[...] * _silu(h_ref[:, 3 * RET_W:4 * RET_W])

    blk = pl.BlockSpec((R, RET_W), lambda i: (i, 0))
    return pl.pallas_call(
        body, grid=(T // R,),
        in_specs=[pl.BlockSpec((R, D), lambda i: (i, 0)), blk, blk, _full_spec((1, RET_W))],
        out_specs=[blk, pl.BlockSpec((nc, RET_H, CH, CH), lambda i: (i, 0, 0, 0))],
        out_shape=[jax.ShapeDtypeStruct((T, RET_W), f32), jax.ShapeDtypeStruct((T // CH, RET_H, CH, CH), f32)],
        scratch_shapes=[pltpu.VMEM((RET_H, CH, CH), f32), pltpu.VMEM((R, RET_W), f32)],
        compiler_params=_cparams(("arbitrary",)), name="ret_fwd")(hr, cosw, sinw, gam)


def _ret_bwd(hr, cosw, sinw, gam, states, dout):
    T = hr.shape[0]
    R = RB_RET
    nc = R // CH
    nb = T // R

    def body(h_ref, c_ref, s_ref, g_ref, st_ref, do_ref, dh_ref, dgam_ref, dst, wide):
        @pl.when(pl.program_id(0) == 0)
        def _():
            dst[...] = jnp.zeros_like(dst)
            dgam_ref[...] = jnp.zeros_like(dgam_ref)

        consts = _ret_consts()
        cw, sw = c_ref[...], s_ref[...]
        q, k = h_ref[:, 0:RET_W], h_ref[:, RET_W:2 * RET_W]
        gr = h_ref[:, 3 * RET_W:4 * RET_W]
        qh = _split_heads((q * cw + _partner(q) * sw) * 0.125, RET_H)
        kh = _split_heads(k * cw + _partner(k) * sw, RET_H)
        vh = _split_heads(h_ref[:, 2 * RET_W:3 * RET_W], RET_H)
        do = do_ref[...]
        gam = g_ref[...]
        sg = _silu(gr)
        don = _split_heads(do * gam * sg, RET_H)
        ons, dqs, dks, dvs = [None] * nc, [None] * nc, [None] * nc, [None] * nc
        ds = dst[...]
        for c in reversed(range(nc)):
            sl = slice(c * RET_H, (c + 1) * RET_H)
            (on, _), vjp = jax.vjp(functools.partial(_ret_chunk, consts), qh[sl], kh[sl], vh[sl], st_ref[c])
            dqs[c], dks[c], dvs[c], ds = vjp((don[sl], ds))
            ons[c] = on
        dst[...] = ds
        _merge_heads(wide, jnp.concatenate(ons, axis=0), RET_H)
        onw = wide[...]
        dgam_ref[...] += jnp.sum(do * onw * sg, 0, keepdims=True)
        dh_ref[:, 3 * RET_W:4 * RET_W] = do * onw * gam * _dsilu(gr)
        _merge_heads(wide, jnp.concatenate(dqs, axis=0), RET_H)
        u = wide[...] * 0.125
        dh_ref[:, 0:RET_W] = u * cw + _partner(u * sw)
        _merge_heads(wide, jnp.concatenate(dks, axis=0), RET_H)
        u = wide[...]
        dh_ref[:, RET_W:2 * RET_W] = u * cw + _partner(u * sw)
        _merge_heads(dh_ref, jnp.concatenate(dvs, axis=0), RET_H, col0=2 * RET_W)

    blk = pl.BlockSpec((R, RET_W), lambda i: (nb - 1 - i, 0))
    return pl.pallas_call(
        body, grid=(nb,),
        in_specs=[pl.BlockSpec((R, D), lambda i: (nb - 1 - i, 0)), blk, blk, _full_spec((1, RET_W)),
                  pl.BlockSpec((nc, RET_H, CH, CH), lambda i: (nb - 1 - i, 0, 0, 0)), blk],
        out_specs=[pl.BlockSpec((R, D), lambda i: (nb - 1 - i, 0)), _full_spec((1, RET_W))],
        out_shape=[jax.ShapeDtypeStruct((T, D), f32), jax.ShapeDtypeStruct((1, RET_W), f32)],
        scratch_shapes=[pltpu.VMEM((RET_H, CH, CH), f32), pltpu.VMEM((R, RET_W), f32)],
        compiler_params=_cparams(("arbitrary",)), name="ret_bwd")(hr, cosw, sinw, gam, states, dout)


def _lru_ab(xc, wa, ba, wx, bx, lam):
    r = _sigmoid(_dot(xc, wa) + ba)
    i = _sigmoid(_dot(xc, wx) + bx)
    la = 8.0 * r * (-_softplus(-lam))
    a = jnp.exp(la)
    em = jnp.tanh(la) * (jnp.exp(2.0 * la) + 1.0)
    return a, jnp.sqrt(-em) * (i * xc)


def _lru_out(h, gate):
    return h * _gelu(gate)


def _scan_fwd(a, b):
    R = a.shape[0]
    row = lax.broadcasted_iota(jnp.int32, a.shape, 0)
    d = 1
    while d < R:
        m = row >= d
        b = jnp.where(m, a * pltpu.roll(b, d, 0) + b, b)
        a = jnp.where(m, a * pltpu.roll(a, d, 0), a)
        d *= 2
    return a, b


def _scan_bwd(a, b):
    R = a.shape[0]
    row = lax.broadcasted_iota(jnp.int32, a.shape, 0)
    d = 1
    while d < R:
        m = row < R - d
        b = jnp.where(m, a * pltpu.roll(b, R - d, 0) + b, b)
        a = jnp.where(m, a * pltpu.roll(a, R - d, 0), a)
        d *= 2
    return b


def _lru_fwd(hl, cw, cb, wa, ba, wx, bx, lam):
    T = hl.shape[0]
    R = RB_LRU
    W = LRU_W

    def body(h_ref, t_ref, cw_ref, cb_ref, wa_ref, ba_ref, wx_ref, bx_ref, lam_ref, o_ref, hs_ref, carry, ext):
        first = pl.program_id(0) == 0

        @pl.when(first)
        def _():
            carry[...] = jnp.zeros_like(carry)

        tail = jnp.where(first, 0.0, t_ref[:, 0:W])
        xc = _conv_fwd(ext, h_ref[:, 0:W], tail, cw_ref[...], R) + cb_ref[...]
        a, b = _lru_ab(xc, wa_ref[...], ba_ref[...], wx_ref[...], bx_ref[...], lam_ref[...])
        ap, hloc = _scan_fwd(a, b)
        h = hloc + ap * carry[0:1, :]
        carry[...] = jnp.broadcast_to(h[R - 1:R, :], carry.shape)
        hs_ref[...] = h
        o_ref[...] = _lru_out(h, h_ref[:, W:2 * W])

    vec = _full_spec((1, W))
    blk = pl.BlockSpec((R, W), lambda i: (i, 0))
    return pl.pallas_call(
        body, grid=(T // R,),
        in_specs=[pl.BlockSpec((R, 2 * W), lambda i: (i, 0)), _prev_tail_spec(R, 2 * W), _full_spec((4, W)), vec,
                  _full_spec((W, W)), vec, _full_spec((W, W)), vec, vec],
        out_specs=[blk, blk], out_shape=[jax.ShapeDtypeStruct((T, W), f32)] * 2,
        scratch_shapes=[pltpu.VMEM((8, W), f32), pltpu.VMEM((R + 8, W), f32)],
        compiler_params=_cparams(("arbitrary",)), name="lru_fwd")(hl, hl, cw, cb, wa, ba, wx, bx, lam)


def _lru_bwd(hl, hs, cw, cb, wa, ba, wx, bx, lam, dout):
    T = hl.shape[0]
    R = RB_LRU
    W = LRU_W
    nb = T // R

    def body(h_ref, t_ref, hs_ref, hst_ref, cw_ref, cb_ref, wa_ref, ba_ref, wx_ref, bx_ref, lam_ref, do_ref,
             dh_ref, dcw_ref, dcb_ref, dwa_ref, dba_ref, dwx_ref, dbx_ref, dlam_ref, carry_g, carry_dy, ext):
        i = pl.program_id(0)
        last_blk = i == 0
        first_blk = i == nb - 1

        @pl.when(last_blk)
        def _():
            carry_g[...] = jnp.zeros_like(carry_g)
            carry_dy[...] = jnp.zeros_like(carry_dy)
            for r in (dcw_ref, dcb_ref, dwa_ref, dba_ref, dwx_ref, dbx_ref, dlam_ref):
                r[...] = jnp.zeros_like(r)

        tail = jnp.where(first_blk, 0.0, t_ref[:, 0:W])
        xc = _conv_fwd(ext, h_ref[:, 0:W], tail, cw_ref[...], R) + cb_ref[...]
        (a, _), vjp_ab = jax.vjp(_lru_ab, xc, wa_ref[...], ba_ref[...], wx_ref[...], bx_ref[...], lam_ref[...])
        hs = hs_ref[...]
        _, vjp_out = jax.vjp(_lru_out, hs, h_ref[:, W:2 * W])
        dh, dgate = vjp_out(do_ref[...])
        row = lax.broadcasted_iota(jnp.int32, (R, W), 0)
        dh = jnp.where(row == R - 1, dh + carry_g[0:1, :], dh)
        a_up = jnp.where(row == R - 1, 0.0, pltpu.roll(a, R - 1, 0))
        g = _scan_bwd(a_up, dh)
        carry_g[...] = jnp.broadcast_to(a[0:1, :] * g[0:1, :], carry_g.shape)
        hprev0 = jnp.where(first_blk, 0.0, hst_ref[7:8, :])
        hprev = jnp.where(row == 0, hprev0, pltpu.roll(hs, 1, 0))
        dxc, dwa, dba, dwx, dbx, dlam = vjp_ab((g * hprev, g))
        dwa_ref[...] += dwa
        dba_ref[...] += dba
        dwx_ref[...] += dwx
        dbx_ref[...] += dbx
        dlam_ref[...] += dlam
        dcb_ref[...] += jnp.sum(dxc, 0, keepdims=True)
        dx, dcw = _conv_bwd(ext, dxc, carry_dy[...], cw_ref[...], R)
        carry_dy[...] = dxc[0:8, :]
        dcw_ref[...] += dcw
        dh_ref[:, 0:W] = dx
        dh_ref[:, W:2 * W] = dgate

    vec = _full_spec((1, W))
    mat = _full_spec((W, W))
    blk = pl.BlockSpec((R, W), lambda i: (nb - 1 - i, 0))
    blk2 = pl.BlockSpec((R, 2 * W), lambda i: (nb - 1 - i, 0))
    return pl.pallas_call(
        body, grid=(nb,),
        in_specs=[blk2, _prev_tail_spec_rev(R, 2 * W, nb), blk, _prev_tail_spec_rev(R, W, nb), _full_spec((4, W)), vec,
                  mat, vec, mat, vec, vec, blk],
        out_specs=[blk2, _full_spec((4, W)), vec, mat, vec, mat, vec, vec],
        out_shape=[jax.ShapeDtypeStruct((T, 2 * W), f32), jax.ShapeDtypeStruct((4, W), f32),
                   jax.ShapeDtypeStruct((1, W), f32), jax.ShapeDtypeStruct((W, W), f32),
                   jax.ShapeDtypeStruct((1, W), f32), jax.ShapeDtypeStruct((W, W), f32),
                   jax.ShapeDtypeStruct((1, W), f32), jax.ShapeDtypeStruct((1, W), f32)],
        scratch_shapes=[pltpu.VMEM((8, W), f32), pltpu.VMEM((8, W), f32), pltpu.VMEM((R + 8, W), f32)],
        compiler_params=_cparams(("arbitrary",)), name="lru_bwd")(hl, hl, hs, hs, cw, cb, wa, ba, wx, bx, lam, dout)


def _head_ones():
    i = lax.broadcasted_iota(jnp.int32, (GDN_W, GDN_W), 0)
    j = lax.broadcasted_iota(jnp.int32, (GDN_W, GDN_W), 1)
    return jnp.where(jnp.bitwise_xor(i, j) < CH, 1.0, 0.0).astype(bf16)


def _head_sums(x, ones):
    x1, x2, _ = _split3(x)
    return jnp.dot(x1, ones, preferred_element_type=f32) + jnp.dot(x2, ones, preferred_element_type=f32)


def _l2n(y, ones):
    r = lax.rsqrt(_head_sums(y * y, ones) + 1e-6)
    return y * r, r


def _l2n_bwd(dn, n, r, ones):
    return r * (dn - n * _head_sums(dn * n, ones))


def _gdn_local(inverse, q, k, vs, gc, bb):
    B = q.shape[0]
    ii = lax.broadcasted_iota(jnp.int32, (B, CH, CH), 1)
    jj = lax.broadcasted_iota(jnp.int32, (B, CH, CH), 2)
    gct = jnp.swapaxes(gc, 1, 2)
    decay = jnp.where(ii >= jj, jnp.exp(jnp.minimum(gc - gct, 0.0)), 0.0)
    kk = _bmm('bid,bjd->bij', k, k)
    inv = inverse(-jnp.where(ii > jj, bb * kk * decay, 0.0))
    egc = jnp.exp(gc)
    u = _bmm3('bij,bje->bie', inv, vs * bb)
    w = _bmm3('bij,bje->bie', inv, k * (bb * egc))
    qk = _bmm('bid,bjd->bij', q, k) * (0.125 * decay)
    glast = gc[:, CH - 1:CH, :]
    return u, w, qk, q * (0.125 * egc), k * jnp.exp(glast - gc), jnp.exp(jnp.broadcast_to(glast, gc.shape))


def _gdn_step(st, u, w, qk, qd, kt, egl, z, gn):
    vnew = u - _bmm('hcd,hde->hce', w, st)
    o = _bmm('hcd,hde->hce', qd, st) + _bmm('hij,hje->hie', qk, vnew)
    st2 = st * egl + _bmm('hcd,hce->hde', kt, vnew)
    out = o * lax.rsqrt(_rowsum(o * o) * (1.0 / CH) + 1e-6) * gn * _silu(z)
    return out, st2


def _gdn_scalars(ab, alog, dtb):
    sp = _softplus(ab + dtb)
    return -jnp.exp(alog) * sp, _sigmoid(ab)


def _bcast_heads(blk, lane0, H):
    R = blk.shape[0]
    n = R // CH
    parts = [jnp.broadcast_to(blk[:, lane0 + h:lane0 + h + 1], (R, CH)).reshape(n, CH, CH) for h in range(H)]
    return jnp.stack(parts, axis=1).reshape(n * H, CH, CH)


def _unbcast_heads(x, lane0, H):
    n = x.shape[0] // H
    R = n * CH
    s = jnp.sum(x, axis=2, keepdims=True).reshape(n, H, CH, 1)
    lane = lax.broadcasted_iota(jnp.int32, (R, 128), 1)
    acc = jnp.zeros((R, 128), f32)
    for h in range(H):
        acc = acc + jnp.where(lane == lane0 + h, jnp.broadcast_to(s[:, h].reshape(R, 1), (R, 128)), 0.0)
    return acc


def _gdn_fwd(hg, cw, alog, dtb, gn):
    T = hg.shape[0]
    R = RB_GDN
    nc = R // CH
    W3 = 3 * GDN_W
    H = GDN_H

    def body(h_ref, t_ref, cw_ref, al_ref, dt_ref, gn_ref, o_ref, st_ref, inv_ref, st, ext):
        first = pl.program_id(0) == 0

        @pl.when(first)
        def _():
            st[...] = jnp.zeros_like(st)

        def inverse(m):
            inv = _neumann_inv(m)
            inv_ref[...] = inv
            return inv

        tail = jnp.where(first, 0.0, t_ref[:, 0:W3])
        y = _silu(_conv_fwd(ext, h_ref[:, 0:W3], tail, cw_ref[...], R))
        ones = _head_ones()
        qs = _split_heads(_l2n(y[:, 0:GDN_W], ones)[0], H)
        ks = _split_heads(_l2n(y[:, GDN_W:2 * GDN_W], ones)[0], H)
        vs = _split_heads(y[:, 2 * GDN_W:W3], H)
        zh = _split_heads(h_ref[:, W3:W3 + GDN_W], H)
        g, beta = _gdn_scalars(h_ref[:, W3 + GDN_W:GDN_IN], al_ref[...], dt_ref[...])
        loc = _gdn_local(inverse, qs, ks, vs, _bcast_heads(_chunk_cumsum(g), 0, H), _bcast_heads(beta, H, H))
        gnv = gn_ref[...]
        outs = []
        s_cur = st[...]
        for c in range(nc):
            sl = slice(c * H, (c + 1) * H)
            st_ref[c] = s_cur
            out, s_cur = _gdn_step(s_cur, *(t[sl] for t in loc), zh[sl], gnv)
            outs.append(out)
        st[...] = s_cur
        _merge_heads(o_ref, jnp.concatenate(outs, axis=0), H)

    return pl.pallas_call(
        body, grid=(T // R,),
        in_specs=[pl.BlockSpec((R, GDN_IN), lambda i: (i, 0)), _prev_tail_spec(R, GDN_IN), _full_spec((4, W3)),
                  _full_spec((1, 128)), _full_spec((1, 128)), _full_spec((1, CH))],
        out_specs=[pl.BlockSpec((R, GDN_W), lambda i: (i, 0)), pl.BlockSpec((nc, H, CH, CH), lambda i: (i, 0, 0, 0)),
                   pl.BlockSpec((nc * H, CH, CH), lambda i: (i, 0, 0))],
        out_shape=[jax.ShapeDtypeStruct((T, GDN_W), f32), jax.ShapeDtypeStruct((T // CH, H, CH, CH), f32),
                   jax.ShapeDtypeStruct((T // CH * H, CH, CH), f32)],
        scratch_shapes=[pltpu.VMEM((H, CH, CH), f32), pltpu.VMEM((R + 8, W3), f32)],
        compiler_params=_cparams(("arbitrary",)), name="gdn_fwd")(hg, hg, cw, alog, dtb, gn)


def _gdn_bwd(hg, cw, alog, dtb, gn, states, invs, dout):
    T = hg.shape[0]
    R = RB_GDN
    nc = R // CH
    nb = T // R
    W3 = 3 * GDN_W
    H = GDN_H

    def body(h_ref, t_ref, cw_ref, al_ref, dt_ref, gn_ref, st_ref, inv_ref, do_ref,
             dh_ref, dcw_ref, dal_ref, ddt_ref, dgn_ref, dst, carry_dy, ext, wide):
        i = pl.program_id(0)
        first_blk = i == nb - 1

        @pl.when(i == 0)
        def _():
            dst[...] = jnp.zeros_like(dst)
            carry_dy[...] = jnp.zeros_like(carry_dy)
            for r in (dcw_ref, dal_ref, ddt_ref, dgn_ref):
                r[...] = jnp.zeros_like(r)

        tail = jnp.where(first_blk, 0.0, t_ref[:, 0:W3])
        ypre = _conv_fwd(ext, h_ref[:, 0:W3], tail, cw_ref[...], R)
        y = _silu(ypre)
        ones = _head_ones()
        qn, rq = _l2n(y[:, 0:GDN_W], ones)
        kn, rk = _l2n(y[:, GDN_W:2 * GDN_W], ones)
        qs, ks, vs = _split_heads(qn, H), _split_heads(kn, H), _split_heads(y[:, 2 * GDN_W:W3], H)
        zh = _split_heads(h_ref[:, W3:W3 + GDN_W], H)
        ab = h_ref[:, W3 + GDN_W:GDN_IN]
        alog, dtb = al_ref[...], dt_ref[...]
        g, beta = _gdn_scalars(ab, alog, dtb)
        kept = inv_ref[...]
        loc, vjp_loc = jax.vjp(functools.partial(_gdn_local, lambda m: _known_inv(m, kept)), qs, ks, vs,
                               _bcast_heads(_chunk_cumsum(g), 0, H), _bcast_heads(beta, H, H))
        doh = _split_heads(do_ref[...], H)
        gnv = gn_ref[...]
        dloc = [[None] * nc for _ in range(6)]
        dzs = [None] * nc
        ds = dst[...]
        dgn = jnp.zeros((1, CH), f32)
        for c in reversed(range(nc)):
            sl = slice(c * H, (c + 1) * H)
            _, vjp = jax.vjp(_gdn_step, st_ref[c], *(t[sl] for t in loc), zh[sl], gnv)
            grads = vjp((doh[sl], ds))
            ds = grads[0]
            for j in range(6):
                dloc[j][c] = grads[1 + j]
            dzs[c] = grads[7]
            dgn = dgn + grads[8]
        dst[...] = ds
        dgn_ref[...] += dgn
        dqs, dks, dvs, dgb, dbb = vjp_loc(tuple(jnp.concatenate(d, axis=0) for d in dloc))
        lane = lax.broadcasted_iota(jnp.int32, (R, 128), 1)
        dg = _chunk_cumsum(_unbcast_heads(dgb, 0, H), reverse=True)
        dbeta = _unbcast_heads(dbb, H, H)
        da = dg * (-jnp.exp(alog)) * _sigmoid(ab + dtb)
        dh_ref[:, W3 + GDN_W:GDN_IN] = jnp.where(lane < H, da, dbeta * beta * (1.0 - beta))
        ddt_ref[...] += jnp.sum(jnp.where(lane < H, da, 0.0), 0, keepdims=True)
        dal_ref[...] += jnp.sum(jnp.where(lane < H, dg * g, 0.0), 0, keepdims=True)
        _merge_heads(dh_ref, jnp.concatenate(dzs, axis=0), H, col0=W3)
        for j, dpart in enumerate((dqs, dks, dvs)):
            _merge_heads(wide, dpart, H, col0=j * GDN_W)
        wide[:, 0:GDN_W] = _l2n_bwd(wide[:, 0:GDN_W], qn, rq, ones)
        wide[:, GDN_W:2 * GDN_W] = _l2n_bwd(wide[:, GDN_W:2 * GDN_W], kn, rk, ones)
        dy = wide[...] * _dsilu(ypre)
        dx, dcw = _conv_bwd(ext, dy, carry_dy[...], cw_ref[...], R)
        carry_dy[...] = dy[0:8, :]
        dcw_ref[...] += dcw
        dh_ref[:, 0:W3] = dx

    blk = pl.BlockSpec((R, GDN_IN), lambda i: (nb - 1 - i, 0))
    return pl.pallas_call(
        body, grid=(nb,),
        in_specs=[blk, _prev_tail_spec_rev(R, GDN_IN, nb), _full_spec((4, W3)), _full_spec((1, 128)),
                  _full_spec((1, 128)), _full_spec((1, CH)),
                  pl.BlockSpec((nc, H, CH, CH), lambda i: (nb - 1 - i, 0, 0, 0)),
                  pl.BlockSpec((nc * H, CH, CH), lambda i: (nb - 1 - i, 0, 0)),
                  pl.BlockSpec((R, GDN_W), lambda i: (nb - 1 - i, 0))],
        out_specs=[blk, _full_spec((4, W3)), _full_spec((1, 128)), _full_spec((1, 128)), _full_spec((1, CH))],
        out_shape=[jax.ShapeDtypeStruct((T, GDN_IN), f32), jax.ShapeDtypeStruct((4, W3), f32),
                   jax.ShapeDtypeStruct((1, 128), f32), jax.ShapeDtypeStruct((1, 128), f32),
                   jax.ShapeDtypeStruct((1, CH), f32)],
        scratch_shapes=[pltpu.VMEM((H, CH, CH), f32), pltpu.VMEM((8, W3), f32), pltpu.VMEM((R + 8, W3), f32),
                        pltpu.VMEM((R, W3), f32)],
        compiler_params=_cparams(("arbitrary",)), name="gdn_bwd")(hg, hg, cw, alog, dtb, gn, states, invs, dout)


def _block_diag(w):
    out = jnp.zeros((LRU_W, LRU_W), w.dtype)
    for g in range(w.shape[0]):
        out = lax.dynamic_update_slice(out, w[g], (g * CH, g * CH))
    return out


def _block_diag_t(w):
    return jnp.stack([w[g * CH:(g + 1) * CH, g * CH:(g + 1) * CH] for g in range(LRU_W // CH)])


def _pad_lanes(v, n=128):
    return jnp.pad(v, (0, n - v.shape[0]))[None, :]


def _local_step(x, p, positions, target, fetch, emit, sm):
    cosw, sinw = _rope_tables(positions)
    cols = lambda w: jnp.transpose(w, (1, 2, 0, 3)).reshape(-1, D, NDEV * FSP)
    saved = []
    h = x
    for l in range(DEPTH):
        v = lambda n: sm[n][l][None, :]
        F1, tok = fetch(l, 'f1', h)
        p384a, pda = cols(F1['p384']), F1['pd']
        z1, x1, g1, u1 = _ffn_fwd(h, p384a, pda, v('ln_ffn1_g') + tok, v('ln_ffn1_b'), 0, 0)
        G, _ = fetch(l, 'rest', x1)
        p384, pd, pr, pinl, ping, wpp = cols(G['p384']), G['pd'], G['pr'], G['pinl'], G['ping'], G['wpp']
        wts = (p384a, pda, p384, pd, pr, pinl, ping, wpp)
        hr, hl, hg = _proj_in(x1, pr, pinl, ping, 0)
        o_r, rst = _ret_fwd(hr, cosw, sinw, v('ret_norm_g'))
        lru_args = (sm['lru_conv_w'][l], v('lru_conv_b'), _block_diag(sm['lru_w_a'][l]), v('lru_b_a'),
                    _block_diag(sm['lru_w_x'][l]), v('lru_b_x'), v('lru_lambda'))
        o_l, hs = _lru_fwd(hl, *lru_args)
        gdn_args = (sm['gdn_conv_w'][l], _pad_lanes(sm['gdn_a_log'][l]), _pad_lanes(sm['gdn_dt_bias'][l]),
                    v('gdn_norm_g'))
        o_g, *gst = _gdn_fwd(hg, *gdn_args)
        z2, x2 = _mix_out(x1, o_r, o_l, o_g, pr, v('ln_mix_g'), v('ln_mix_b'), 0)
        z3, x3, g2, u2 = _ffn_fwd(x2, p384, pd, v('ln_ffn2_g'), v('ln_ffn2_b'), 0, 1, ple=(p[l], pr, wpp))
        saved.append((h, z1, x1, hr, hl, hg, o_r, rst, o_l, hs, lru_args, o_g, gst, gdn_args, z2, x2, z3,
                      g1, u1, g2, u2, wts))
        h = x3
    d, loss = _loss_grad(h, target)

    small = {n: [None] * DEPTH for n in SMALL}
    tok = 0.0
    for l in reversed(range(DEPTH)):
        (x0, z1, x1, hr, hl, hg, o_r, rst, o_l, hs, lru_args, o_g, gst, gdn_args, z2, x2, z3,
         g1, u1, g2, u2, wts) = saved[l]
        p384a, pda, p384, pd, pr, pinl, ping, wpp = wts
        v = lambda n: sm[n][l][None, :]
        rows = lambda m: m.reshape(NDEV, m.shape[1] // NDEV, m.shape[2])
        d2, dg2, du2, a2, dy2, small['ln_ffn2_g'][l], small['ln_ffn2_b'][l] = _ffn_bwd(
            z3, d, g2, u2, p384, pd, v('ln_ffn2_g') + tok, 0, 1)
        d2, dgp, dpj = _ple_bwd(x2, p[l], dy2, d2, pr, wpp, 0)
        dxb, dzb, do_r, do_l, do_g, small['ln_mix_g'][l], small['ln_mix_b'][l] = _mix_out_bwd(
            z2, d2, pr, v('ln_mix_g'), 0)
        dhr, small['ret_norm_g'][l] = _ret_bwd(hr, cosw, sinw, v('ret_norm_g'), rst, do_r)
        (dhl, small['lru_conv_w'][l], small['lru_conv_b'][l], dwa, small['lru_b_a'][l], dwx, small['lru_b_x'][l],
         small['lru_lambda'][l]) = _lru_bwd(hl, hs, *lru_args, do_l)
        small['lru_w_a'][l], small['lru_w_x'][l] = _block_diag_t(dwa), _block_diag_t(dwx)
        dhg, small['gdn_conv_w'][l], dal, ddt, small['gdn_norm_g'][l] = _gdn_bwd(hg, *gdn_args, *gst, do_g)
        small['gdn_a_log'][l], small['gdn_dt_bias'][l] = dal[:, 0:GDN_H], ddt[:, 0:GDN_H]
        d1 = _proj_in_bwd(dxb, dhr, dhl, dhg, pr, pinl, ping, 0)
        dwo = jnp.concatenate([_matmul_tn(o_r, dzb, D, "dw_out_r"), _matmul_tn(o_l, dzb, D, "dw_out_l"),
                               _matmul_tn(o_g, dzb, D, "dw_out_g")], axis=1)
        tok = emit(l, 'rest', {
            'p384': jnp.stack([_matmul_tn(x2, dg2, FSP, "dw_gate", NDEV), _matmul_tn(x2, du2, FSP, "dw_up", NDEV)],
                              axis=1),
            'pd': rows(_matmul_tn(a2, dy2, D, "dw_down"))[:, None],
            'pr': jnp.stack([rows(_matmul_tn(x1, dhr, D, "dw_in_r")), rows(dwo),
                             rows(_matmul_tn(x2, dgp, D, "dw_ple_gate"))], axis=1),
            'pinl': rows(_matmul_tn(x1, dhl, 2 * LRU_W, "dw_in_l"))[:, None],
            'ping': rows(_matmul_tn(x1, dhg, GDN_IN, "dw_in_g"))[:, None],
            'ppp': jnp.transpose(_matmul_tn(p[l], dpj, D, "dw_ple_proj").reshape(PLE, NDEV, 128), (1, 0, 2))[:, None]})
        d, dg1, du1, a1, dy1, small['ln_ffn1_g'][l], small['ln_ffn1_b'][l] = _ffn_bwd(
            z1, d1, g1, u1, p384a, pda, v('ln_ffn1_g') + tok, 0, 0)
        tok = emit(l, 'f1', {
            'p384': jnp.stack([_matmul_tn(x0, dg1, FSP, "dw_gate", NDEV), _matmul_tn(x0, du1, FSP, "dw_up", NDEV)],
                              axis=1),
            'pd': rows(_matmul_tn(a1, dy1, D, "dw_down"))[:, None]})
    small = {n: jnp.stack([g.reshape(sm[n].shape[1:]) for g in gs]) for n, gs in small.items()}
    return loss, d, small


def _pack_big(ws, dtype=bf16):
    padc = lambda a, n: jnp.pad(a, ((0, 0), (0, 0), (0, n - a.shape[2])))
    padr = lambda a, n: jnp.pad(a, ((0, 0), (0, n - a.shape[1]), (0, 0)))
    per_layer = lambda arrs: jnp.stack(arrs, axis=1).reshape((-1,) + arrs[0].shape[1:])
    w_in = ws['w_in']
    out = {
        'p384': per_layer([padc(ws[n], FSP) for n in ('ffn1_w_gate', 'ffn1_w_up', 'ffn2_w_gate', 'ffn2_w_up')]),
        'pd': per_layer([padr(ws[n], FSP) for n in ('ffn1_w_down', 'ffn2_w_down')]),
        'pr': per_layer([w_in[:, :, 0:D], ws['w_out'], ws['ple_w_gate']]),
        'pinl': w_in[:, :, D:D + 2 * LRU_W],
        'ping': padc(w_in[:, :, D + 2 * LRU_W:D_IN], GDN_IN),
        'ppp': ws['ple_w_proj'],
    }
    return {k: a.astype(dtype) for k, a in out.items()}


def _exchange(arrays, scatter, name):
    n = len(arrays)

    def body(*refs):
        ins, outs = refs[:n], refs[n:2 * n]
        send_sems, recv_sems, local_sems = refs[2 * n:]
        x, y, c = lax.axis_index("x"), lax.axis_index("y"), lax.axis_index("c")
        me = 4 * x + 2 * y + c
        copies = []
        for i in range(n):
            src = ins[i].at[me] if scatter[i] else ins[i]
            cp = pltpu.make_async_copy(src, outs[i].at[me], local_sems.at[i])
            cp.start()
            copies.append(cp)
        sends = []
        for j in range(1, NDEV):
            peer = (me + j) % NDEV
            pid = (peer // 4, (peer // 2) % 2, peer % 2)
            for i in range(n):
                src = ins[i].at[peer] if scatter[i] else ins[i]
                cp = pltpu.make_async_remote_copy(
                    src_ref=src, dst_ref=outs[i].at[me], send_sem=send_sems.at[i, j - 1],
                    recv_sem=recv_sems.at[i, j - 1], device_id=pid, device_id_type=pl.DeviceIdType.MESH)
                cp.start()
                sends.append(cp)
        for j in range(1, NDEV):
            source = (me + NDEV - j) % NDEV
            sid = (source // 4, (source // 2) % 2, source % 2)
            for i in range(n):
                src = ins[i].at[me] if scatter[i] else ins[i]
                pltpu.make_async_remote_copy(
                    src_ref=src, dst_ref=outs[i].at[source], send_sem=send_sems.at[i, j - 1],
                    recv_sem=recv_sems.at[i, j - 1], device_id=sid, device_id_type=pl.DeviceIdType.MESH).wait_recv()
        for cp in sends:
            cp.wait_send()
        for cp in copies:
            cp.wait()

    hbm = pl.BlockSpec(memory_space=pltpu.HBM)
    out_shape = [jax.ShapeDtypeStruct(a.shape if s else (NDEV,) + a.shape, a.dtype) for a, s in zip(arrays, scatter)]
    return pl.pallas_call(
        body, in_specs=[hbm] * n, out_specs=[hbm] * n, out_shape=out_shape,
        scratch_shapes=[pltpu.SemaphoreType.DMA((n, NDEV - 1)), pltpu.SemaphoreType.DMA((n, NDEV - 1)),
                        pltpu.SemaphoreType.DMA((n,))],
        compiler_params=pltpu.CompilerParams(has_side_effects=True), name=name)(*arrays)


def _gather_two_level(arrays, name):
    n = len(arrays)

    def body(*refs):
        ins, outs = refs[:n], refs[n:2 * n]
        send_sems, recv_sems, local_sems = refs[2 * n:]
        x, y, c = lax.axis_index("x"), lax.axis_index("y"), lax.axis_index("c")
        me, sibling = (x, y, c), (x, y, 1 - c)
        chips = [(1 - x, y), (x, 1 - y), (1 - x, 1 - y)]
        slot = lambda d: 4 * d[0] + 2 * d[1] + d[2]

        def copy(i, k, block, to, src=None):
            return pltpu.make_async_remote_copy(
                src_ref=outs[i].at[slot(block)] if src is None else src, dst_ref=outs[i].at[slot(block)],
                send_sem=send_sems.at[i, k], recv_sem=recv_sems.at[i, k], device_id=to,
                device_id_type=pl.DeviceIdType.MESH)

        mine, first, passed = [], [], []
        for i in range(n):
            cp = pltpu.make_async_copy(ins[i], outs[i].at[slot(me)], local_sems.at[i])
            cp.start()
            mine.append(cp)
            first.append(copy(i, 0, me, sibling, src=ins[i]))
            first += [copy(i, 1 + j, me, (*chip, c), src=ins[i]) for j, chip in enumerate(chips)]
        for cp in first:
            cp.start()
        for i in range(n):
            for j, chip in enumerate(chips):
                copy(i, 1 + j, (*chip, c), me).wait_recv()
                cp = copy(i, 4 + j, (*chip, c), sibling)
                cp.start()
                passed.append(cp)
        for i in range(n):
            copy(i, 0, sibling, me).wait_recv()
            for j, chip in enumerate(chips):
                copy(i, 4 + j, (*chip, 1 - c), me).wait_recv()
        for cp in first + passed:
            cp.wait_send()
        for cp in mine:
            cp.wait()

    hbm = pl.BlockSpec(memory_space=pltpu.HBM)
    return pl.pallas_call(
        body, in_specs=[hbm] * n, out_specs=[hbm] * n,
        out_shape=[jax.ShapeDtypeStruct((NDEV,) + a.shape, a.dtype) for a in arrays],
        scratch_shapes=[pltpu.SemaphoreType.DMA((n, NDEV - 1)), pltpu.SemaphoreType.DMA((n, NDEV - 1)),
                        pltpu.SemaphoreType.DMA((n,))],
        compiler_params=pltpu.CompilerParams(has_side_effects=True), name=name)(*arrays)


def _scatter_pairs(arrays, name):
    n = len(arrays)

    def body(*refs):
        ins, gots = refs[:n], refs[n:2 * n]
        send_sems, recv_sems = refs[2 * n:]
        x, y, c = lax.axis_index("x"), lax.axis_index("y"), lax.axis_index("c")
        sends = []
        for i in range(n):
            for q in range(4):
                cp = pltpu.make_async_remote_copy(
                    src_ref=ins[i].at[2 * q + 1 - c], dst_ref=gots[i].at[q], send_sem=send_sems.at[i, q],
                    recv_sem=recv_sems.at[i, q], device_id=(x, y, 1 - c), device_id_type=pl.DeviceIdType.MESH)
                cp.start()
                sends.append(cp)
        for cp in sends:
            cp.wait_recv()
        for cp in sends:
            cp.wait_send()

    hbm = pl.BlockSpec(memory_space=pltpu.HBM)
    return pl.pallas_call(
        body, in_specs=[hbm] * n, out_specs=[hbm] * n,
        out_shape=[jax.ShapeDtypeStruct((4,) + a.shape[1:], a.dtype) for a in arrays],
        scratch_shapes=[pltpu.SemaphoreType.DMA((n, 4)), pltpu.SemaphoreType.DMA((n, 4))],
        compiler_params=pltpu.CompilerParams(has_side_effects=True), name=name)(*arrays)


def _pair_sum(own, got, name):
    def body(a_ref, b_ref, o_ref):
        o_ref[...] = (a_ref[...].astype(f32) + b_ref[...].astype(f32)).astype(bf16)

    spec = pl.BlockSpec((None, None) + own.shape[2:], lambda q, s: (q, s, 0, 0))
    return pl.pallas_call(
        body, grid=own.shape[:2], in_specs=[spec, spec], out_specs=spec,
        out_shape=jax.ShapeDtypeStruct(own.shape, bf16),
        compiler_params=_cparams(("arbitrary", "arbitrary")), name=name)(own, got)


def _gather_plan(srcs, lands, x, y, c):
    me = 4 * x + 2 * y + c
    sends, arrivals = [], []
    for j in range(1, NDEV):
        peer, source = (me + j) % NDEV, (me + NDEV - j) % NDEV
        for i in range(len(srcs)):
            k = i * (NDEV - 1) + j - 1
            sends.append((srcs[i], lands[i].at[me], (peer // 4, (peer // 2) % 2, peer % 2), k))
            arrivals.append((srcs[i], lands[i].at[source], (source // 4, (source // 2) % 2, source % 2), k))
    return sends, arrivals


def _chips_plan(srcs, lands, x, y, c):
    chip = 2 * x + y
    sends, arrivals = [], []
    for j in range(1, 4):
        peer, source = (chip + j) % 4, (chip + 4 - j) % 4
        for i in range(len(srcs)):
            k = i * 3 + j - 1
            sends.append((srcs[i].at[peer], lands[i].at[chip], (peer // 2, peer % 2, c), k))
            arrivals.append((srcs[i].at[chip], lands[i].at[source], (source // 2, source % 2, c), k))
    return sends, arrivals


def _remote(entry, send_sems, recv_sems):
    src, dst, dev, k = entry
    return pltpu.make_async_remote_copy(src_ref=src, dst_ref=dst, send_sem=send_sems.at[k], recv_sem=recv_sems.at[k],
                                        device_id=dev, device_id_type=pl.DeviceIdType.MESH)


_HBM = pl.BlockSpec(memory_space=pltpu.HBM)
_SEM = pl.BlockSpec(memory_space=pltpu.SEMAPHORE)


def _split_start(arrays, land_shapes, plan, npeer, name):
    n = len(arrays)

    def body(*refs):
        srcs, lands = refs[:n], refs[n:2 * n]
        send_sems, recv_sems, token = refs[2 * n], refs[2 * n + 1], refs[-1]
        sends, _ = plan(srcs, lands, lax.axis_index("x"), lax.axis_index("y"), lax.axis_index("c"))
        for entry in sends:
            _remote(entry, send_sems, recv_sems).start()
        token[...] = jnp.zeros_like(token)

    lands = [lax.empty(s, a.dtype) for s, a in zip(land_shapes, arrays)]
    thru = [pltpu.HBM(a.shape, a.dtype) for a in arrays + lands]
    out = pl.pallas_call(
        body, name=name, in_specs=[_HBM] * (2 * n),
        out_specs=(_SEM, _SEM, *([_HBM] * (2 * n)), pl.BlockSpec(memory_space=pltpu.VMEM)),
        out_shape=(pltpu.SemaphoreType.DMA((n * npeer,)), pltpu.SemaphoreType.DMA((n * npeer,)), *thru,
                   jax.ShapeDtypeStruct((8, 128), f32)),
        input_output_aliases={i: 2 + i for i in range(2 * n)},
        compiler_params=pltpu.CompilerParams(has_side_effects=pltpu.SideEffectType.DATAFLOW_SIDE_EFFECTING),
    )(*[pltpu.with_memory_space_constraint(a, pltpu.HBM) for a in arrays + lands])
    return out[0], out[1], list(out[2:2 + n]), list(out[2 + n:2 + 2 * n]), out[-1]


def _split_wait(send_sems, recv_sems, srcs, lands, after, plan, name):
    n = len(srcs)

    def body(*refs):
        s_refs, l_refs = refs[:n], refs[n:2 * n]
        ssem, rsem = refs[2 * n], refs[2 * n + 1]
        sends, arrivals = plan(s_refs, l_refs, lax.axis_index("x"), lax.axis_index("y"), lax.axis_index("c"))
        for entry in sends:
            _remote(entry, ssem, rsem).wait_send()
        for entry in arrivals:
            _remote(entry, ssem, rsem).wait_recv()

    out = pl.pallas_call(
        body, name=name, in_specs=[_HBM] * (2 * n) + [_SEM, _SEM, pl.BlockSpec(memory_space=pl.ANY)],
        out_specs=[_HBM] * (2 * n), out_shape=[pltpu.HBM(a.shape, a.dtype) for a in srcs + lands],
        input_output_aliases={i: i for i in range(2 * n)},
        compiler_params=pltpu.CompilerParams(has_side_effects=pltpu.SideEffectType.DATAFLOW_SIDE_EFFECTING),
    )(*srcs, *lands, send_sems, recv_sems, after)
    return list(out[:n]), list(out[n:])


def _adam_math(w, g, m, v):
    m2 = ADAM_B1 * m + (1.0 - ADAM_B1) * g
    v2 = ADAM_B2 * v + (1.0 - ADAM_B2) * (g * g)
    m_hat = m2 / (1.0 - ADAM_B1 ** ADAM_STEP)
    v_hat = v2 / (1.0 - ADAM_B2 ** ADAM_STEP)
    return -ADAM_LR * (m_hat / (jnp.sqrt(v_hat) + ADAM_EPS) + ADAM_WD * w), m2, v2


def _adam_big(parts, w, m, v, name):
    L, rows, cols = w.shape
    flat = [(a, slot) for layer_parts in parts for a, slot in layer_parts]
    per = len(parts[0])

    def body(*refs):
        prefs = refs[:len(flat)]
        w_ref, m_ref, v_ref, g_ref, d_ref, m2_ref, v2_ref = refs[len(flat):]
        for li in range(L):
            @pl.when(pl.program_id(0) == li)
            def _():
                c0 = 0
                for pref in prefs[li * per:(li + 1) * per]:
                    acc = pref[0].astype(f32)
                    for s in range(1, pref.shape[0]):
                        acc = acc + pref[s].astype(f32)
                    width = min(acc.shape[1], cols - c0)
                    g_ref[:, c0:c0 + width] = acc[0:rows, 0:width]
                    c0 += width

        d, m2, v2 = _adam_math(w_ref[...], g_ref[...], m_ref[...], v_ref[...])
        d_ref[...] = d
        m2_ref[...] = m2
        v2_ref[...] = v2

    wspec = pl.BlockSpec((None, rows, cols), lambda l: (l, 0, 0))
    in_specs = [pl.BlockSpec((a.shape[0], None) + a.shape[2:], functools.partial(lambda l, slot: (0, slot, 0, 0), slot=slot))
                for a, slot in flat]
    return pl.pallas_call(
        body, grid=(L,), in_specs=in_specs + [wspec] * 3, out_specs=[wspec] * 4,
        out_shape=[jax.ShapeDtypeStruct(w.shape, f32)] * 4,
        compiler_params=_cparams(("arbitrary",)), name=name)(*[a for a, _ in flat], w, m, v)


def _sum_sources(stacked):
    rows = stacked.shape[1]

    def body(s_ref, o_ref):
        acc = s_ref[0]
        for s in range(1, NDEV):
            acc = acc + s_ref[s]
        o_ref[...] = acc

    return pl.pallas_call(body, out_shape=jax.ShapeDtypeStruct((rows, 128), f32), name="sum_small_grads")(stacked)


def _adam_small(w, g, m, v):
    def body(w_ref, g_ref, m_ref, v_ref, d_ref, m2_ref, v2_ref):
        d, m2, v2 = _adam_math(w_ref[...], g_ref[...], m_ref[...], v_ref[...])
        d_ref[...] = d
        m2_ref[...] = m2
        v2_ref[...] = v2

    return pl.pallas_call(body, out_shape=[jax.ShapeDtypeStruct(w.shape, f32)] * 3, name="adam_small")(w, g, m, v)


def _pack_rows(arrs):
    flat = []
    for a in arrs:
        a = a.reshape(-1)
        flat.append(jnp.pad(a, (0, (-a.shape[0]) % 1024)))
    return jnp.concatenate(flat).reshape(-1, 128)


def _unpack_rows(packed, shapes):
    out, off = [], 0
    flat = packed.reshape(-1)
    for s in shapes:
        n = math.prod(s)
        out.append(flat[off:off + n].reshape(s))
        off += n + (-n) % 1024
    return out


def _gather_conv(gathered, shape):
    L, K, c = shape
    return jnp.transpose(gathered, (1, 2, 0, 3)).reshape(L, K, NDEV * c)


def kernel(x, p, positions, ln_ffn1_g, ln_ffn1_b, ffn1_w_gate, ffn1_w_up, ffn1_w_down, w_in, ret_norm_g, lru_conv_w, lru_conv_b, lru_w_a, lru_b_a, lru_w_x, lru_b_x, lru_lambda, gdn_conv_w, gdn_a_log, gdn_dt_bias, gdn_norm_g, w_out, ln_mix_g, ln_mix_b, ffn2_w_gate, ffn2_w_up, ffn2_w_down, ple_w_gate, ple_w_proj, ln_ffn2_g, ln_ffn2_b, loss_target, m_ln_ffn1_g, m_ln_ffn1_b, m_ffn1_w_gate, m_ffn1_w_up, m_ffn1_w_down, m_w_in, m_ret_norm_g, m_lru_conv_w, m_lru_conv_b, m_lru_w_a, m_lru_b_a, m_lru_w_x, m_lru_b_x, m_lru_lambda, m_gdn_conv_w, m_gdn_a_log, m_gdn_dt_bias, m_gdn_norm_g, m_w_out, m_ln_mix_g, m_ln_mix_b, m_ffn2_w_gate, m_ffn2_w_up, m_ffn2_w_down, m_ple_w_gate, m_ple_w_proj, m_ln_ffn2_g, m_ln_ffn2_b, v_ln_ffn1_g, v_ln_ffn1_b, v_ffn1_w_gate, v_ffn1_w_up, v_ffn1_w_down, v_w_in, v_ret_norm_g, v_lru_conv_w, v_lru_conv_b, v_lru_w_a, v_lru_b_a, v_lru_w_x, v_lru_b_x, v_lru_lambda, v_gdn_conv_w, v_gdn_a_log, v_gdn_dt_bias, v_gdn_norm_g, v_w_out, v_ln_mix_g, v_ln_mix_b, v_ffn2_w_gate, v_ffn2_w_up, v_ffn2_w_down, v_ple_w_gate, v_ple_w_proj, v_ln_ffn2_g, v_ln_ffn2_b):
    args = locals()
    W = {n: args[n] for n in WEIGHTS}
    M = {n: args['m_' + n] for n in WEIGHTS}
    V = {n: args['v_' + n] for n in WEIGHTS}
    me = 4 * lax.axis_index("x") + 2 * lax.axis_index("y") + lax.axis_index("c")

    core = lax.axis_index("c")
    chip = 2 * lax.axis_index("x") + lax.axis_index("y")

    packed = _pack_big(W)

    def group(l, name):
        per = {k: packed[k].shape[0] // DEPTH for k in PACKS}
        if name == 'f1':
            return [packed['p384'][l * per['p384']:l * per['p384'] + 2], packed['pd'][l * per['pd']:l * per['pd'] + 1]]
        return [packed['p384'][l * per['p384'] + 2:(l + 1) * per['p384']],
                packed['pd'][l * per['pd'] + 1:(l + 1) * per['pd']]] + [
                    packed[k][l * per[k]:(l + 1) * per[k]] for k in PACKS[2:]]

    def as_weights(arrs):
        G = dict(zip(PACKS, arrs))
        if 'ppp' in G:
            G['wpp'] = jnp.transpose(G.pop('ppp'), (1, 2, 0, 3)).reshape(PLE, D)
        return G

    conv_pack = _pack_rows([W[n] for n in CONV_SHARDED])
    g0 = _gather_two_level(group(0, 'f1') + [conv_pack], "gather_weights")
    g0, rest0 = lax.optimization_barrier((g0, group(0, 'rest')))
    start0 = _split_start(rest0, [(NDEV,) + a.shape for a in rest0], _gather_plan, NDEV - 1, "gather_start_0")
    tok0, all1 = lax.optimization_barrier((start0[4], group(1, 'f1') + group(1, 'rest')))
    start1 = _split_start(all1, [(NDEV,) + a.shape for a in all1], _gather_plan, NDEV - 1, "gather_start_1")
    arrived = {}

    def gather_done(started, after, name):
        srcs, lands = _split_wait(started[0], started[1], started[2], started[3], after, _gather_plan, name)
        return [lax.dynamic_update_slice_in_dim(ld, s[None], me, axis=0) for s, ld in zip(srcs, lands)]

    def fetch(l, name, after):
        if l == 0 and name == 'f1':
            return as_weights(g0[:-1]), tok0[0, 0] + start1[4][0, 0]
        if l == 0:
            return as_weights(gather_done(start0, after, "gather_wait_0")), 0.0
        if name == 'f1':
            arrived[1] = gather_done(start1, after, "gather_wait_1")
            return as_weights(arrived[1][:2]), 0.0
        return as_weights(arrived[1][2:]), 0.0

    conv_all = g0[-1]
    sm = {n: W[n] for n in SMALL}
    conv_shards = [_unpack_rows(conv_all[s], [W[n].shape for n in CONV_SHARDED]) for s in range(NDEV)]
    for i, n in enumerate(CONV_SHARDED):
        sm[n] = _gather_conv(jnp.stack([cs[i] for cs in conv_shards]), W[n].shape)

    received, started = {}, {}

    def emit(l, name, grads):
        keys = list(grads)
        arrs = [grads[k] for k in keys]
        gots = _scatter_pairs(arrs, "scatter_pairs")
        owns = [lax.dynamic_index_in_dim(a.reshape((4, 2) + a.shape[1:]), core, axis=1, keepdims=False) for a in arrs]
        pair = [_pair_sum(o, g, "pair_sum_" + k) for k, o, g in zip(keys, owns, gots)]
        started[l, name] = (keys, _split_start(pair, [a.shape for a in pair], _chips_plan, 3,
                                               f"scatter_start_{l}_{name}"))
        return started[l, name][1][4][0, 0]

    def scatter_done(l, name, after):
        keys, st = started[l, name]
        srcs, lands = _split_wait(st[0], st[1], st[2], st[3], after, _chips_plan, f"scatter_wait_{l}_{name}")
        received[l, name] = dict(zip(keys, [
            lax.dynamic_update_slice_in_dim(ld, lax.dynamic_index_in_dim(s, chip, axis=0), chip, axis=0)
            for s, ld in zip(srcs, lands)]))

    loss, grad_x, small = _local_step(x[0], p[:, 0], positions.reshape(-1, 1), loss_target[0], fetch, emit, sm)
    loss = lax.psum(loss[0, 0], ("x", "y", "c"))
    last = (0, 'f1')
    for l, name in started:
        if (l, name) != last:
            scatter_done(l, name, grad_x)

    small_pack = _pack_rows([small[n] for n in SMALL])
    small_all = _exchange([small_pack], [False], "gather_small_grads")[0]
    small_sum = _unpack_rows(_sum_sources(small_all), [small[n].shape for n in SMALL])
    grads, delta, new_m, new_v = {}, {}, {}, {}
    for n, g in zip(SMALL, small_sum):
        if n in CONV_SHARDED:
            c = W[n].shape[2]
            g = lax.dynamic_slice_in_dim(g, me * c, c, axis=2)
        grads[n] = g

    big_parts = {
        'ffn1_w_gate': [('f1', 'p384', 0)], 'ffn1_w_up': [('f1', 'p384', 1)], 'ffn1_w_down': [('f1', 'pd', 0)],
        'ffn2_w_gate': [('rest', 'p384', 0)], 'ffn2_w_up': [('rest', 'p384', 1)], 'ffn2_w_down': [('rest', 'pd', 0)],
        'w_in': [('rest', 'pr', 0), ('rest', 'pinl', 0), ('rest', 'ping', 0)], 'w_out': [('rest', 'pr', 1)],
        'ple_w_gate': [('rest', 'pr', 2)], 'ple_w_proj': [('rest', 'ppp', 0)],
    }
    def adam(n):
        parts = [[(received[l, grp][k], slot) for grp, k, slot in big_parts[n]] for l in range(DEPTH)]
        grads[n], delta[n], new_m[n], new_v[n] = _adam_big(parts, W[n], M[n], V[n], "adam_" + n)

    shapes = [W[n].shape for n in SMALL]
    d_s, m_s, v_s = _adam_small(*[_pack_rows([src[n] for n in SMALL]) for src in (W, grads, M, V)])
    for n, dd, mm, vv in zip(SMALL, _unpack_rows(d_s, shapes), _unpack_rows(m_s, shapes), _unpack_rows(v_s, shapes)):
        delta[n], new_m[n], new_v[n] = dd, mm, vv
    waits_last = [n for n in BIG if big_parts[n][0][0] == last[1]]
    for n in BIG:
        if n not in waits_last:
            adam(n)
    scatter_done(*last, delta['w_out'])
    for n in waits_last:
        adam(n)

    return (loss, grad_x[None], *[grads[n] for n in WEIGHTS], *[delta[n] for n in WEIGHTS],
            *[new_m[n] for n in WEIGHTS], *[new_v[n] for n in WEIGHTS])
```

```python
import functools
import math

import jax
import jax.numpy as jnp
from jax import lax
from jax.experimental import pallas as pl
from jax.experimental.pallas import tpu as pltpu

f32 = jnp.float32
bf16 = jnp.bfloat16

NDEV = 8
DEPTH = 2
D = 1024
FS = 352
FSP = 384
FB = 2
NF = NDEV // FB
PLE = 256
CH = 64
RET_H, GDN_H = 4, 6
RET_W, LRU_W, GDN_W = 256, 384, 384
GDN_IN = 1664
GDN_IN_REAL = 1548
D_IN = 3340
ALPHA = 4.0 ** 0.25
LN_EPS = 1e-5
ROPE_THETA = 10000.0
TM = 512
RB_RET, RB_LRU, RB_GDN = 512, 512, 256
VMEM_LIMIT = 56 * 1024 * 1024
ADAM_LR, ADAM_B1, ADAM_B2, ADAM_EPS, ADAM_WD, ADAM_STEP = 0.001, 0.9, 0.999, 1e-08, 0.01, 10

WEIGHTS = ['ln_ffn1_g', 'ln_ffn1_b', 'ffn1_w_gate', 'ffn1_w_up', 'ffn1_w_down', 'w_in', 'ret_norm_g', 'lru_conv_w',
           'lru_conv_b', 'lru_w_a', 'lru_b_a', 'lru_w_x', 'lru_b_x', 'lru_lambda', 'gdn_conv_w', 'gdn_a_log',
           'gdn_dt_bias', 'gdn_norm_g', 'w_out', 'ln_mix_g', 'ln_mix_b', 'ffn2_w_gate', 'ffn2_w_up', 'ffn2_w_down',
           'ple_w_gate', 'ple_w_proj', 'ln_ffn2_g', 'ln_ffn2_b']
BIG = ['ffn1_w_gate', 'ffn1_w_up', 'ffn1_w_down', 'w_in', 'w_out', 'ffn2_w_gate', 'ffn2_w_up', 'ffn2_w_down',
       'ple_w_gate', 'ple_w_proj']
SMALL = [n for n in WEIGHTS if n not in BIG]
PACKS = ('p384', 'pd', 'pr', 'pinl', 'ping', 'ppp')
CONV_SHARDED = {'lru_conv_w': LRU_W, 'gdn_conv_w': 3 * GDN_W}


def _cparams(sem=None):
    return pltpu.CompilerParams(dimension_semantics=sem, vmem_limit_bytes=VMEM_LIMIT)


def _sigmoid(x):
    return 1.0 / (1.0 + jnp.exp(-x))


def _silu(x):
    return x * _sigmoid(x)


def _dsilu(x):
    s = _sigmoid(x)
    return s * (1.0 + x * (1.0 - s))


def _softplus(x):
    return jnp.maximum(x, 0.0) + jnp.log(1.0 + jnp.exp(-jnp.abs(x)))


def _gelu(x):
    return 0.5 * x * (1.0 + jnp.tanh(0.7978845608028654 * (x + 0.044715 * x * x * x)))


def _dot(a, b):
    return jnp.dot(a.astype(bf16), b.astype(bf16), preferred_element_type=f32)


def _dot_nt(a, b):
    return lax.dot_general(a.astype(bf16), b.astype(bf16), (((1,), (1,)), ((), ())), preferred_element_type=f32)


def _dot_tn(a, b):
    return lax.dot_general(a.astype(bf16), b.astype(bf16), (((0,), (0,)), ((), ())), preferred_element_type=f32)


def _bmm(eq, a, b):
    return jnp.einsum(eq, a.astype(bf16), b.astype(bf16), preferred_element_type=f32)


def _split3(a):
    a1 = a.astype(bf16)
    r = a - a1.astype(f32)
    a2 = r.astype(bf16)
    return a1, a2, (r - a2.astype(f32)).astype(bf16)


def _bmm3(eq, a, b):
    a1, a2, _ = _split3(a)
    b1, b2, _ = _split3(b)
    e = lambda x, y: jnp.einsum(eq, x, y, preferred_element_type=f32)
    return e(a1, b1) + (e(a1, b2) + e(a2, b1))


def _rowsum(x):
    x1, x2, _ = _split3(x)
    ones = jnp.ones((x.shape[0], CH, CH), bf16)
    e = lambda y: jnp.einsum('bij,bjk->bik', y, ones, preferred_element_type=f32)
    return e(x1) + e(x2)


def _tri_ones(B, upper=False):
    ii = lax.broadcasted_iota(jnp.int32, (B, CH, CH), 1)
    jj = lax.broadcasted_iota(jnp.int32, (B, CH, CH), 2)
    return jnp.where((ii <= jj) if upper else (ii >= jj), 1.0, 0.0).astype(bf16)


def _cumsum_mm(t, x):
    x1, x2, x3 = _split3(x)
    e = lambda y: jnp.einsum('bij,bjk->bik', t, y, preferred_element_type=f32)
    return e(x1) + (e(x2) + e(x3))


def _chunk_cumsum(x, reverse=False):
    n = x.shape[0] // CH
    return _cumsum_mm(_tri_ones(n, upper=reverse), x.reshape(n, CH, 128)).reshape(x.shape)


@jax.custom_vjp
def _neumann_inv(m):
    ii = lax.broadcasted_iota(jnp.int32, m.shape, 1)
    jj = lax.broadcasted_iota(jnp.int32, m.shape, 2)
    inv = jnp.where(ii == jj, 1.0, 0.0).astype(f32) + m
    mp = m
    for _ in range(5):
        mp = _bmm3('bij,bjk->bik', mp, mp)
        inv = inv + _bmm3('bij,bjk->bik', inv, mp)
    return inv


def _neumann_inv_fwd(m):
    inv = _neumann_inv(m)
    return inv, inv


def _neumann_inv_bwd(inv, g):
    return (_bmm3('bij,bkj->bik', _bmm3('bji,bjk->bik', inv, g), inv),)


_neumann_inv.defvjp(_neumann_inv_fwd, _neumann_inv_bwd)


@jax.custom_vjp
def _known_inv(m, inv):
    return inv


def _known_inv_fwd(m, inv):
    return inv, inv


def _known_inv_bwd(inv, g):
    return _neumann_inv_bwd(inv, g)[0], jnp.zeros_like(inv)


_known_inv.defvjp(_known_inv_fwd, _known_inv_bwd)


def _ln_stats(z):
    mu = jnp.mean(z, -1, keepdims=True)
    zc = z - mu
    rstd = lax.rsqrt(jnp.mean(zc * zc, -1, keepdims=True) + LN_EPS)
    return zc * rstd, rstd


def _ln_bwd(z, g, dout):
    xh, rstd = _ln_stats(z)
    dxh = dout * g
    dz = rstd * (dxh - jnp.mean(dxh, -1, keepdims=True) - xh * jnp.mean(dxh * xh, -1, keepdims=True))
    return dz, jnp.sum(dout * xh, 0, keepdims=True), jnp.sum(dout, 0, keepdims=True)


def _full_spec(shape):
    nd = len(shape)
    return pl.BlockSpec(shape, lambda *_: (0,) * nd)


def _ffn_fwd(x, p384, pd, lg, lb, slot, which, ple=None):
    T = x.shape[0]
    sg, su, sd = 2 * slot, 2 * slot + 1, slot
    has_ple = ple is not None

    def body(*refs):
        if has_ple:
            (x_ref, wg_ref, wu_ref, wd_ref, lg_ref, lb_ref, p_ref, wpg_ref, wpp_ref,
             z_ref, o_ref, g_ref, u_ref, acc, xb_s) = refs
        else:
            x_ref, wg_ref, wu_ref, wd_ref, lg_ref, lb_ref, z_ref, o_ref, g_ref, u_ref, acc, xb_s = refs
        f = pl.program_id(1)

        @pl.when(f == 0)
        def _():
            x = x_ref[...]
            xb = x.astype(bf16)
            xb_s[...] = xb
            base = ALPHA * x
            if has_ple:
                gate = _sigmoid(_dot(xb, wpg_ref[...].reshape(D, D)))
                base = base + gate * _dot(p_ref[...], wpp_ref[...])
            acc[...] = base

        xb = xb_s[...]
        g = _dot(xb, wg_ref[...])
        u = _dot(xb, wu_ref[...])
        g_ref[...] = g.astype(bf16)
        u_ref[...] = u.astype(bf16)
        acc[...] += 0.5 * _dot(_silu(g) * u, wd_ref[...].reshape(FB * FSP, D))

        @pl.when(f == NF - 1)
        def _():
            z = acc[...]
            z_ref[...] = z
            o_ref[...] = _ln_stats(z)[0] * lg_ref[...] + lb_ref[...]

    row = pl.BlockSpec((TM, D), lambda i, f: (i, 0))
    in_specs = [row,
                pl.BlockSpec((None, D, FB * FSP), lambda i, f: (sg, 0, f)),
                pl.BlockSpec((None, D, FB * FSP), lambda i, f: (su, 0, f)),
                pl.BlockSpec((FB, None, FSP, D), lambda i, f: (f, sd, 0, 0)),
                _full_spec((1, D)), _full_spec((1, D))]
    args = [x, p384, p384, pd, lg, lb]
    if has_ple:
        p, pr, wpp = ple
        in_specs += [pl.BlockSpec((TM, PLE), lambda i, f: (i, 0)),
                     pl.BlockSpec((NDEV, None, 128, D), lambda i, f: (0, 2, 0, 0)),
                     _full_spec((PLE, D))]
        args += [p, pr, wpp]
    hid = pl.BlockSpec((TM, FB * FSP), lambda i, f: (i, f))
    hshape = jax.ShapeDtypeStruct((T, NDEV * FSP), bf16)
    return pl.pallas_call(
        body, grid=(T // TM, NF), in_specs=in_specs, out_specs=[row, row, hid, hid],
        out_shape=[jax.ShapeDtypeStruct((T, D), f32)] * 2 + [hshape, hshape],
        scratch_shapes=[pltpu.VMEM((TM, D), f32), pltpu.VMEM((TM, D), bf16)],
        compiler_params=_cparams(("arbitrary", "arbitrary")), name=f"ffn{which + 1}_fwd")(*args)


def _ffn_bwd(z, dout, gs, us, p384, pd, lg, slot, which):
    T = z.shape[0]
    TMB = TM
    sg, su, sd = 2 * slot, 2 * slot + 1, slot

    def body(z_ref, do_ref, g_ref, u_ref, wg_ref, wu_ref, wd_ref, lg_ref,
             dx_ref, dg_ref, du_ref, a_ref, dy_ref, dlg_ref, dlb_ref, acc, dyb):
        i, f = pl.program_id(0), pl.program_id(1)

        @pl.when(jnp.logical_and(i == 0, f == 0))
        def _():
            dlg_ref[...] = jnp.zeros_like(dlg_ref)
            dlb_ref[...] = jnp.zeros_like(dlb_ref)

        @pl.when(f == 0)
        def _():
            dz, dlg, dlb = _ln_bwd(z_ref[...], lg_ref[...], do_ref[...])
            dlg_ref[...] += dlg
            dlb_ref[...] += dlb
            dy = (0.5 * dz).astype(bf16)
            dyb[...] = dy
            dy_ref[...] = dy
            acc[...] = ALPHA * dz

        g = g_ref[...].astype(f32)
        u = u_ref[...].astype(f32)
        da = _dot_nt(dyb[...], wd_ref[...].reshape(FB * FSP, D))
        sgm = _sigmoid(g)
        dg = (da * u * (sgm * (1.0 + g * (1.0 - sgm)))).astype(bf16)
        du = (da * (g * sgm)).astype(bf16)
        dg_ref[...] = dg
        du_ref[...] = du
        a_ref[...] = (g * sgm * u).astype(bf16)
        acc[...] += _dot_nt(dg, wg_ref[...]) + _dot_nt(du, wu_ref[...])

        @pl.when(f == NF - 1)
        def _():
            dx_ref[...] = acc[...]

    row = pl.BlockSpec((TMB, D), lambda i, f: (i, 0))
    hid = pl.BlockSpec((TMB, FB * FSP), lambda i, f: (i, f))
    vec = _full_spec((1, D))
    in_specs = [row, row, hid, hid,
                pl.BlockSpec((None, D, FB * FSP), lambda i, f: (sg, 0, f)),
                pl.BlockSpec((None, D, FB * FSP), lambda i, f: (su, 0, f)),
                pl.BlockSpec((FB, None, FSP, D), lambda i, f: (f, sd, 0, 0)),
                vec]
    args = [z, dout, gs, us, p384, p384, pd, lg]
    out_specs = [row, hid, hid, hid, row, vec, vec]
    hshape = jax.ShapeDtypeStruct((T, NDEV * FSP), bf16)
    out_shape = [jax.ShapeDtypeStruct((T, D), f32), hshape, hshape, hshape, jax.ShapeDtypeStruct((T, D), bf16),
                 jax.ShapeDtypeStruct((1, D), f32), jax.ShapeDtypeStruct((1, D), f32)]
    return pl.pallas_call(
        body, grid=(T // TMB, NF), in_specs=in_specs, out_specs=out_specs, out_shape=out_shape,
        scratch_shapes=[pltpu.VMEM((TMB, D), f32), pltpu.VMEM((TMB, D), bf16)],
        compiler_params=_cparams(("arbitrary", "arbitrary")), name=f"ffn{which + 1}_bwd")(*args)


def _ple_bwd(x, p, dy, dx_ffn, pr, wpp, layer):
    T = x.shape[0]

    def body(x_ref, p_ref, dy_ref, dxf_ref, wpg_ref, wpp_ref, dx_ref, dgp_ref, dpj_ref):
        dz = 2.0 * dy_ref[...].astype(f32)
        wpg = wpg_ref[...].reshape(D, D)
        gate = _sigmoid(_dot(x_ref[...], wpg))
        proj = _dot(p_ref[...], wpp_ref[...])
        dgp = (dz * proj * gate * (1.0 - gate)).astype(bf16)
        dgp_ref[...] = dgp
        dpj_ref[...] = (dz * gate).astype(bf16)
        dx_ref[...] = dxf_ref[...] + _dot_nt(dgp, wpg)

    row = pl.BlockSpec((TM, D), lambda i: (i, 0))
    return pl.pallas_call(
        body, grid=(T // TM,),
        in_specs=[row, pl.BlockSpec((TM, PLE), lambda i: (i, 0)), row, row,
                  pl.BlockSpec((NDEV, None, 128, D), lambda i: (0, 3 * layer + 2, 0, 0)), _full_spec((PLE, D))],
        out_specs=[row, row, row],
        out_shape=[jax.ShapeDtypeStruct((T, D), f32), jax.ShapeDtypeStruct((T, D), bf16),
                   jax.ShapeDtypeStruct((T, D), bf16)],
        compiler_params=_cparams(("arbitrary",)), name="ple_bwd")(x, p, dy, dx_ffn, pr, wpp)


def _matmul_tn(a, b, nb, name, nsub=1):
    T, M = a.shape
    N = b.shape[1]
    wide = nsub * nb
    tk = min(T, 1024 if wide <= 2048 else 512)
    nk = T // tk

    def body(a_ref, b_ref, o_ref, acc):
        k = pl.program_id(1)

        @pl.when(k == 0)
        def _():
            acc[...] = jnp.zeros_like(acc)

        acc[...] += _dot_tn(a_ref[...], b_ref[...])

        @pl.when(k == nk - 1)
        def _():
            for j in range(nsub):
                o_ref[j] = acc[:, j * nb:(j + 1) * nb].astype(bf16)

    return pl.pallas_call(
        body, grid=(N // wide, nk),
        in_specs=[pl.BlockSpec((tk, M), lambda n, k: (k, 0)), pl.BlockSpec((tk, wide), lambda n, k: (k, n))],
        out_specs=pl.BlockSpec((nsub, M, nb), lambda n, k: (n, 0, 0)),
        out_shape=jax.ShapeDtypeStruct((N // nb, M, nb), bf16),
        scratch_shapes=[pltpu.VMEM((M, wide), f32)],
        compiler_params=_cparams(("arbitrary", "arbitrary")), name=name)(a, b)


def _proj_in(x, pr, pinl, ping, layer):
    T = x.shape[0]

    def body(x_ref, wr_ref, wl_ref, wg_ref, hr_ref, hl_ref, hg_ref):
        xb = x_ref[...].astype(bf16)
        hr_ref[...] = _dot(xb, wr_ref[...].reshape(D, D))
        hl_ref[...] = _dot(xb, wl_ref[...].reshape(D, 2 * LRU_W))
        hg_ref[...] = _dot(xb, wg_ref[...].reshape(D, GDN_IN))

    return pl.pallas_call(
        body, grid=(T // TM,),
        in_specs=[pl.BlockSpec((TM, D), lambda i: (i, 0)),
                  pl.BlockSpec((NDEV, None, 128, D), lambda i: (0, 3 * layer, 0, 0)),
                  pl.BlockSpec((NDEV, None, 128, 2 * LRU_W), lambda i: (0, layer, 0, 0)),
                  pl.BlockSpec((NDEV, None, 128, GDN_IN), lambda i: (0, layer, 0, 0))],
        out_specs=[pl.BlockSpec((TM, D), lambda i: (i, 0)), pl.BlockSpec((TM, 2 * LRU_W), lambda i: (i, 0)),
                   pl.BlockSpec((TM, GDN_IN), lambda i: (i, 0))],
        out_shape=[jax.ShapeDtypeStruct((T, D), f32), jax.ShapeDtypeStruct((T, 2 * LRU_W), f32),
                   jax.ShapeDtypeStruct((T, GDN_IN), f32)],
        compiler_params=_cparams(("arbitrary",)), name="proj_in")(x, pr, pinl, ping)


def _proj_in_bwd(base, dhr, dhl, dhg, pr, pinl, ping, layer):
    T = base.shape[0]

    def body(b_ref, dr_ref, dl_ref, dg_ref, wr_ref, wl_ref, wg_ref, o_ref):
        o_ref[...] = (b_ref[...] + _dot_nt(dr_ref[...], wr_ref[...].reshape(D, D))
                      + _dot_nt(dl_ref[...], wl_ref[...].reshape(D, 2 * LRU_W))
                      + _dot_nt(dg_ref[...], wg_ref[...].reshape(D, GDN_IN)))

    return pl.pallas_call(
        body, grid=(T // TM,),
        in_specs=[pl.BlockSpec((TM, D), lambda i: (i, 0)), pl.BlockSpec((TM, D), lambda i: (i, 0)),
                  pl.BlockSpec((TM, 2 * LRU_W), lambda i: (i, 0)), pl.BlockSpec((TM, GDN_IN), lambda i: (i, 0)),
                  pl.BlockSpec((NDEV, None, 128, D), lambda i: (0, 3 * layer, 0, 0)),
                  pl.BlockSpec((NDEV, None, 128, 2 * LRU_W), lambda i: (0, layer, 0, 0)),
                  pl.BlockSpec((NDEV, None, 128, GDN_IN), lambda i: (0, layer, 0, 0))],
        out_specs=pl.BlockSpec((TM, D), lambda i: (i, 0)),
        out_shape=jax.ShapeDtypeStruct((T, D), f32),
        compiler_params=_cparams(("arbitrary",)), name="proj_in_bwd")(base, dhr, dhl, dhg, pr, pinl, ping)


def _mix_out(x1, o_r, o_l, o_g, pr, lg, lb, layer):
    T = x1.shape[0]

    def body(x_ref, r_ref, l_ref, g_ref, w_ref, lg_ref, lb_ref, z_ref, o_ref):
        w = w_ref[...].reshape(D, D)
        z = (ALPHA * x_ref[...] + _dot(r_ref[...], w[0:RET_W]) + _dot(l_ref[...], w[RET_W:RET_W + LRU_W])
             + _dot(g_ref[...], w[RET_W + LRU_W:D]))
        z_ref[...] = z
        o_ref[...] = _ln_stats(z)[0] * lg_ref[...] + lb_ref[...]

    row = pl.BlockSpec((TM, D), lambda i: (i, 0))
    return pl.pallas_call(
        body, grid=(T // TM,),
        in_specs=[row, pl.BlockSpec((TM, RET_W), lambda i: (i, 0)), pl.BlockSpec((TM, LRU_W), lambda i: (i, 0)),
                  pl.BlockSpec((TM, GDN_W), lambda i: (i, 0)),
                  pl.BlockSpec((NDEV, None, 128, D), lambda i: (0, 3 * layer + 1, 0, 0)),
                  _full_spec((1, D)), _full_spec((1, D))],
        out_specs=[row, row], out_shape=[jax.ShapeDtypeStruct((T, D), f32)] * 2,
        compiler_params=_cparams(("arbitrary",)), name="mix_out")(x1, o_r, o_l, o_g, pr, lg, lb)


def _mix_out_bwd(z, dout, pr, lg, layer):
    T = z.shape[0]

    def body(z_ref, do_ref, w_ref, lg_ref, dxb_ref, dzb_ref, dr_ref, dl_ref, dg_ref, dlg_ref, dlb_ref):
        @pl.when(pl.program_id(0) == 0)
        def _():
            dlg_ref[...] = jnp.zeros_like(dlg_ref)
            dlb_ref[...] = jnp.zeros_like(dlb_ref)

        dz, dlg, dlb = _ln_bwd(z_ref[...], lg_ref[...], do_ref[...])
        dlg_ref[...] += dlg
        dlb_ref[...] += dlb
        dxb_ref[...] = ALPHA * dz
        dzb = dz.astype(bf16)
        dzb_ref[...] = dzb
        w = w_ref[...].reshape(D, D)
        dr_ref[...] = _dot_nt(dzb, w[0:RET_W])
        dl_ref[...] = _dot_nt(dzb, w[RET_W:RET_W + LRU_W])
        dg_ref[...] = _dot_nt(dzb, w[RET_W + LRU_W:D])

    row = pl.BlockSpec((TM, D), lambda i: (i, 0))
    vec = _full_spec((1, D))
    return pl.pallas_call(
        body, grid=(T // TM,),
        in_specs=[row, row, pl.BlockSpec((NDEV, None, 128, D), lambda i: (0, 3 * layer + 1, 0, 0)), vec],
        out_specs=[row, row, pl.BlockSpec((TM, RET_W), lambda i: (i, 0)), pl.BlockSpec((TM, LRU_W), lambda i: (i, 0)),
                   pl.BlockSpec((TM, GDN_W), lambda i: (i, 0)), vec, vec],
        out_shape=[jax.ShapeDtypeStruct((T, D), f32), jax.ShapeDtypeStruct((T, D), bf16),
                   jax.ShapeDtypeStruct((T, RET_W), f32), jax.ShapeDtypeStruct((T, LRU_W), f32),
                   jax.ShapeDtypeStruct((T, GDN_W), f32), jax.ShapeDtypeStruct((1, D), f32),
                   jax.ShapeDtypeStruct((1, D), f32)],
        compiler_params=_cparams(("arbitrary",)), name="mix_out_bwd")(z, dout, pr, lg)


def _loss_grad(y, target):
    T = y.shape[0]

    def body(y_ref, t_ref, dy_ref, l_ref):
        @pl.when(pl.program_id(0) == 0)
        def _():
            l_ref[...] = jnp.zeros_like(l_ref)

        e = y_ref[...] - t_ref[...]
        dy_ref[...] = e * (1.0 / D)
        l_ref[...] += 0.5 * jnp.sum(jnp.sum(e * e, -1, keepdims=True) * (1.0 / D), 0, keepdims=True)

    row = pl.BlockSpec((TM, D), lambda i: (i, 0))
    return pl.pallas_call(
        body, grid=(T // TM,), in_specs=[row, row], out_specs=[row, _full_spec((1, 1))],
        out_shape=[jax.ShapeDtypeStruct((T, D), f32), jax.ShapeDtypeStruct((1, 1), f32)],
        compiler_params=_cparams(("arbitrary",)), name="loss_grad")(y, target)


def _split_heads(x, H):
    n = x.shape[0] // CH
    parts = [x[:, h * CH:(h + 1) * CH].reshape(n, CH, CH) for h in range(H)]
    return jnp.stack(parts, axis=1).reshape(n * H, CH, CH)


def _merge_heads(ref, x, H, col0=0):
    n = x.shape[0] // H
    x4 = x.reshape(n, H, CH, CH)
    for h in range(H):
        ref[:, col0 + h * CH:col0 + (h + 1) * CH] = x4[:, h].reshape(n * CH, CH)


def _rows_down(x, before, s):
    r8 = lax.broadcasted_iota(jnp.int32, before.shape, 0)
    top = jnp.where(r8 < s, pltpu.roll(before, s, 0), pltpu.roll(x[0:8], s, 0))
    return jnp.concatenate([top, pltpu.roll(x, s, 0)[8:]], axis=0)


def _rows_up(x, after, s):
    R = x.shape[0]
    r8 = lax.broadcasted_iota(jnp.int32, after.shape, 0)
    bottom = jnp.where(r8 >= 8 - s, pltpu.roll(after, 8 - s, 0), pltpu.roll(x[R - 8:R], 8 - s, 0))
    return jnp.concatenate([pltpu.roll(x, R - s, 0)[0:R - 8], bottom], axis=0)


def _conv_fwd(ext, x, tail, w, R):
    ext[0:8, :] = tail
    ext[8:R + 8, :] = x
    y = w[3:4, :] * x
    for k in range(3):
        y = y + w[k:k + 1, :] * _rows_down(x, tail, 3 - k)
    return y


def _conv_bwd(ext, dy, dy_next, w, R):
    x, tail = ext[8:8 + R, :], ext[0:8, :]
    dx = w[3:4, :] * dy
    dws = []
    for k in range(3):
        dx = dx + w[k:k + 1, :] * _rows_up(dy, dy_next, 3 - k)
        dws.append(jnp.sum(dy * _rows_down(x, tail, 3 - k), 0, keepdims=True))
    dws.append(jnp.sum(dy * x, 0, keepdims=True))
    return dx, jnp.concatenate(dws, axis=0)


def _prev_tail_spec(R, W):
    return pl.BlockSpec((8, W), lambda i: (jnp.maximum(i * (R // 8) - 1, 0), 0))


def _prev_tail_spec_rev(R, W, nb):
    return pl.BlockSpec((8, W), lambda i: (jnp.maximum((nb - 1 - i) * (R // 8) - 1, 0), 0))


def _rope_tables(positions):
    T = positions.shape[0]

    def body(p_ref, c_ref, s_ref):
        lane = lax.broadcasted_iota(jnp.int32, (TM, RET_W), 1)
        fi = (lane % 32).astype(f32)
        inv = jnp.exp(fi * (-math.log(ROPE_THETA) / 32.0))
        ang = p_ref[...].astype(f32) * inv
        c_ref[...] = jnp.cos(ang)
        s_ref[...] = jnp.where(lane % CH < 32, -jnp.sin(ang), jnp.sin(ang))

    row = pl.BlockSpec((TM, RET_W), lambda i: (i, 0))
    return pl.pallas_call(
        body, grid=(T // TM,), in_specs=[pl.BlockSpec((TM, 1), lambda i: (i, 0))], out_specs=[row, row],
        out_shape=[jax.ShapeDtypeStruct((T, RET_W), f32)] * 2,
        compiler_params=_cparams(("arbitrary",)), name="rope_tables")(positions)


def _partner(x):
    lane = lax.broadcasted_iota(jnp.int32, x.shape, 1)
    return jnp.where(lane % CH < 32, pltpu.roll(x, RET_W - 32, 1), pltpu.roll(x, 32, 1))


def _ret_consts():
    ii = lax.broadcasted_iota(jnp.int32, (CH, CH), 0).astype(f32)
    jj = lax.broadcasted_iota(jnp.int32, (CH, CH), 1).astype(f32)
    intra, cross, tail, cd = [], [], [], []
    for h in range(RET_H):
        lg = math.log1p(-(2.0 ** (-5.0 - h)))
        intra.append(jnp.exp(jnp.abs(ii - jj) * lg))
        cross.append(jnp.exp((ii + 1.0) * lg))
        tail.append(jnp.exp((CH - 1.0 - ii) * lg))
        cd.append(jnp.full((CH, CH), math.exp(CH * lg), f32))
    return jnp.stack(intra), jnp.stack(cross), jnp.stack(tail), jnp.stack(cd)


def _ret_chunk(consts, q, k, v, st):
    intra, cross, tail, cd = consts
    s = _bmm('hid,hjd->hij', q, k) * intra
    o = _bmm('hij,hje->hie', s, v) + _bmm('hid,hde->hie', q * cross, st)
    st2 = st * cd + _bmm('hjd,hje->hde', k * tail, v)
    oc = o - _rowsum(o) * (1.0 / CH)
    on = oc * lax.rsqrt(_rowsum(oc * oc) * (1.0 / CH) + 1e-5)
    return on, st2


def _ret_fwd(hr, cosw, sinw, gam):
    T = hr.shape[0]
    R = RB_RET
    nc = R // CH

    def body(h_ref, c_ref, s_ref, g_ref, o_ref, st_ref, st, wide):
        @pl.when(pl.program_id(0) == 0)
        def _():
            st[...] = jnp.zeros_like(st)

        consts = _ret_consts()
        cw, sw = c_ref[...], s_ref[...]
        q, k = h_ref[:, 0:RET_W], h_ref[:, RET_W:2 * RET_W]
        qh = _split_heads((q * cw + _partner(q) * sw) * 0.125, RET_H)
        kh = _split_heads(k * cw + _partner(k) * sw, RET_H)
        vh = _split_heads(h_ref[:, 2 * RET_W:3 * RET_W], RET_H)
        outs = []
        s_cur = st[...]
        for c in range(nc):
            sl = slice(c * RET_H, (c + 1) * RET_H)
            st_ref[c] = s_cur
            on, s_cur = _ret_chunk(consts, qh[sl], kh[sl], vh[sl], s_cur)
            outs.append(on)
        st[...] = s_cur
        _merge_heads(wide, jnp.concatenate(outs, axis=0), RET_H)
        o_ref[...] = wide[...] * g_ref[...] * _silu(h_ref[:, 3 * RET_W:4 * RET_W])

    blk = pl.BlockSpec((R, RET_W), lambda i: (i, 0))
    return pl.pallas_call(
        body, grid=(T // R,),
        in_specs=[pl.BlockSpec((R, D), lambda i: (i, 0)), blk, blk, _full_spec((1, RET_W))],
        out_specs=[blk, pl.BlockSpec((nc, RET_H, CH, CH), lambda i: (i, 0, 0, 0))],
        out_shape=[jax.ShapeDtypeStruct((T, RET_W), f32), jax.ShapeDtypeStruct((T // CH, RET_H, CH, CH), f32)],
        scratch_shapes=[pltpu.VMEM((RET_H, CH, CH), f32), pltpu.VMEM((R, RET_W), f32)],
        compiler_params=_cparams(("arbitrary",)), name="ret_fwd")(hr, cosw, sinw, gam)


def _ret_bwd(hr, cosw, sinw, gam, states, dout):
    T = hr.shape[0]
    R = RB_RET
    nc = R // CH
    nb = T // R

    def body(h_ref, c_ref, s_ref, g_ref, st_ref, do_ref, dh_ref, dgam_ref, dst, wide):
        @pl.when(pl.program_id(0) == 0)
        def _():
            dst[...] = jnp.zeros_like(dst)
            dgam_ref[...] = jnp.zeros_like(dgam_ref)

        consts = _ret_consts()
        cw, sw = c_ref[...], s_ref[...]
        q, k = h_ref[:, 0:RET_W], h_ref[:, RET_W:2 * RET_W]
        gr = h_ref[:, 3 * RET_W:4 * RET_W]
        qh = _split_heads((q * cw + _partner(q) * sw) * 0.125, RET_H)
        kh = _split_heads(k * cw + _partner(k) * sw, RET_H)
        vh = _split_heads(h_ref[:, 2 * RET_W:3 * RET_W], RET_H)
        do = do_ref[...]
        gam = g_ref[...]
        sg = _silu(gr)
        don = _split_heads(do * gam * sg, RET_H)
        ons, dqs, dks, dvs = [None] * nc, [None] * nc, [None] * nc, [None] * nc
        ds = dst[...]
        for c in reversed(range(nc)):
            sl = slice(c * RET_H, (c + 1) * RET_H)
            (on, _), vjp = jax.vjp(functools.partial(_ret_chunk, consts), qh[sl], kh[sl], vh[sl], st_ref[c])
            dqs[c], dks[c], dvs[c], ds = vjp((don[sl], ds))
            ons[c] = on
        dst[...] = ds
        _merge_heads(wide, jnp.concatenate(ons, axis=0), RET_H)
        onw = wide[...]
        dgam_ref[...] += jnp.sum(do * onw * sg, 0, keepdims=True)
        dh_ref[:, 3 * RET_W:4 * RET_W] = do * onw * gam * _dsilu(gr)
        _merge_heads(wide, jnp.concatenate(dqs, axis=0), RET_H)
        u = wide[...] * 0.125
        dh_ref[:, 0:RET_W] = u * cw + _partner(u * sw)
        _merge_heads(wide, jnp.concatenate(dks, axis=0), RET_H)
        u = wide[...]
        dh_ref[:, RET_W:2 * RET_W] = u * cw + _partner(u * sw)
        _merge_heads(dh_ref, jnp.concatenate(dvs, axis=0), RET_H, col0=2 * RET_W)

    blk = pl.BlockSpec((R, RET_W), lambda i: (nb - 1 - i, 0))
    return pl.pallas_call(
        body, grid=(nb,),
        in_specs=[pl.BlockSpec((R, D), lambda i: (nb - 1 - i, 0)), blk, blk, _full_spec((1, RET_W)),
                  pl.BlockSpec((nc, RET_H, CH, CH), lambda i: (nb - 1 - i, 0, 0, 0)), blk],
        out_specs=[pl.BlockSpec((R, D), lambda i: (nb - 1 - i, 0)), _full_spec((1, RET_W))],
        out_shape=[jax.ShapeDtypeStruct((T, D), f32), jax.ShapeDtypeStruct((1, RET_W), f32)],
        scratch_shapes=[pltpu.VMEM((RET_H, CH, CH), f32), pltpu.VMEM((R, RET_W), f32)],
        compiler_params=_cparams(("arbitrary",)), name="ret_bwd")(hr, cosw, sinw, gam, states, dout)


def _lru_ab(xc, wa, ba, wx, bx, lam):
    r = _sigmoid(_dot(xc, wa) + ba)
    i = _sigmoid(_dot(xc, wx) + bx)
    la = 8.0 * r * (-_softplus(-lam))
    a = jnp.exp(la)
    em = jnp.tanh(la) * (jnp.exp(2.0 * la) + 1.0)
    return a, jnp.sqrt(-em) * (i * xc)


def _lru_out(h, gate):
    return h * _gelu(gate)


def _scan_fwd(a, b):
    R = a.shape[0]
    row = lax.broadcasted_iota(jnp.int32, a.shape, 0)
    d = 1
    while d < R:
        m = row >= d
        b = jnp.where(m, a * pltpu.roll(b, d, 0) + b, b)
        a = jnp.where(m, a * pltpu.roll(a, d, 0), a)
        d *= 2
    return a, b


def _scan_bwd(a, b):
    R = a.shape[0]
    row = lax.broadcasted_iota(jnp.int32, a.shape, 0)
    d = 1
    while d < R:
        m = row < R - d
        b = jnp.where(m, a * pltpu.roll(b, R - d, 0) + b, b)
        a = jnp.where(m, a * pltpu.roll(a, R - d, 0), a)
        d *= 2
    return b


def _lru_fwd(hl, cw, cb, wa, ba, wx, bx, lam):
    T = hl.shape[0]
    R = RB_LRU
    W = LRU_W

    def body(h_ref, t_ref, cw_ref, cb_ref, wa_ref, ba_ref, wx_ref, bx_ref, lam_ref, o_ref, hs_ref, carry, ext):
        first = pl.program_id(0) == 0

        @pl.when(first)
        def _():
            carry[...] = jnp.zeros_like(carry)

        tail = jnp.where(first, 0.0, t_ref[:, 0:W])
        xc = _conv_fwd(ext, h_ref[:, 0:W], tail, cw_ref[...], R) + cb_ref[...]
        a, b = _lru_ab(xc, wa_ref[...], ba_ref[...], wx_ref[...], bx_ref[...], lam_ref[...])
        ap, hloc = _scan_fwd(a, b)
        h = hloc + ap * carry[0:1, :]
        carry[...] = jnp.broadcast_to(h[R - 1:R, :], carry.shape)
        hs_ref[...] = h
        o_ref[...] = _lru_out(h, h_ref[:, W:2 * W])

    vec = _full_spec((1, W))
    blk = pl.BlockSpec((R, W), lambda i: (i, 0))
    return pl.pallas_call(
        body, grid=(T // R,),
        in_specs=[pl.BlockSpec((R, 2 * W), lambda i: (i, 0)), _prev_tail_spec(R, 2 * W), _full_spec((4, W)), vec,
                  _full_spec((W, W)), vec, _full_spec((W, W)), vec, vec],
        out_specs=[blk, blk], out_shape=[jax.ShapeDtypeStruct((T, W), f32)] * 2,
        scratch_shapes=[pltpu.VMEM((8, W), f32), pltpu.VMEM((R + 8, W), f32)],
        compiler_params=_cparams(("arbitrary",)), name="lru_fwd")(hl, hl, cw, cb, wa, ba, wx, bx, lam)


def _lru_bwd(hl, hs, cw, cb, wa, ba, wx, bx, lam, dout):
    T = hl.shape[0]
    R = RB_LRU
    W = LRU_W
    nb = T // R

    def body(h_ref, t_ref, hs_ref, hst_ref, cw_ref, cb_ref, wa_ref, ba_ref, wx_ref, bx_ref, lam_ref, do_ref,
             dh_ref, dcw_ref, dcb_ref, dwa_ref, dba_ref, dwx_ref, dbx_ref, dlam_ref, carry_g, carry_dy, ext):
        i = pl.program_id(0)
        last_blk = i == 0
        first_blk = i == nb - 1

        @pl.when(last_blk)
        def _():
            carry_g[...] = jnp.zeros_like(carry_g)
            carry_dy[...] = jnp.zeros_like(carry_dy)
            for r in (dcw_ref, dcb_ref, dwa_ref, dba_ref, dwx_ref, dbx_ref, dlam_ref):
                r[...] = jnp.zeros_like(r)

        tail = jnp.where(first_blk, 0.0, t_ref[:, 0:W])
        xc = _conv_fwd(ext, h_ref[:, 0:W], tail, cw_ref[...], R) + cb_ref[...]
        (a, _), vjp_ab = jax.vjp(_lru_ab, xc, wa_ref[...], ba_ref[...], wx_ref[...], bx_ref[...], lam_ref[...])
        hs = hs_ref[...]
        _, vjp_out = jax.vjp(_lru_out, hs, h_ref[:, W:2 * W])
        dh, dgate = vjp_out(do_ref[...])
        row = lax.broadcasted_iota(jnp.int32, (R, W), 0)
        dh = jnp.where(row == R - 1, dh + carry_g[0:1, :], dh)
        a_up = jnp.where(row == R - 1, 0.0, pltpu.roll(a, R - 1, 0))
        g = _scan_bwd(a_up, dh)
        carry_g[...] = jnp.broadcast_to(a[0:1, :] * g[0:1, :], carry_g.shape)
        hprev0 = jnp.where(first_blk, 0.0, hst_ref[7:8, :])
        hprev = jnp.where(row == 0, hprev0, pltpu.roll(hs, 1, 0))
        dxc, dwa, dba, dwx, dbx, dlam = vjp_ab((g * hprev, g))
        dwa_ref[...] += dwa
        dba_ref[...] += dba
        dwx_ref[...] += dwx
        dbx_ref[...] += dbx
        dlam_ref[...] += dlam
        dcb_ref[...] += jnp.sum(dxc, 0, keepdims=True)
        dx, dcw = _conv_bwd(ext, dxc, carry_dy[...], cw_ref[...], R)
        carry_dy[...] = dxc[0:8, :]
        dcw_ref[...] += dcw
        dh_ref[:, 0:W] = dx
        dh_ref[:, W:2 * W] = dgate

    vec = _full_spec((1, W))
    mat = _full_spec((W, W))
    blk = pl.BlockSpec((R, W), lambda i: (nb - 1 - i, 0))
    blk2 = pl.BlockSpec((R, 2 * W), lambda i: (nb - 1 - i, 0))
    return pl.pallas_call(
        body, grid=(nb,),
        in_specs=[blk2, _prev_tail_spec_rev(R, 2 * W, nb), blk, _prev_tail_spec_rev(R, W, nb), _full_spec((4, W)), vec,
                  mat, vec, mat, vec, vec, blk],
        out_specs=[blk2, _full_spec((4, W)), vec, mat, vec, mat, vec, vec],
        out_shape=[jax.ShapeDtypeStruct((T, 2 * W), f32), jax.ShapeDtypeStruct((4, W), f32),
                   jax.ShapeDtypeStruct((1, W), f32), jax.ShapeDtypeStruct((W, W), f32),
                   jax.ShapeDtypeStruct((1, W), f32), jax.ShapeDtypeStruct((W, W), f32),
                   jax.ShapeDtypeStruct((1, W), f32), jax.ShapeDtypeStruct((1, W), f32)],
        scratch_shapes=[pltpu.VMEM((8, W), f32), pltpu.VMEM((8, W), f32), pltpu.VMEM((R + 8, W), f32)],
        compiler_params=_cparams(("arbitrary",)), name="lru_bwd")(hl, hl, hs, hs, cw, cb, wa, ba, wx, bx, lam, dout)


def _head_ones():
    i = lax.broadcasted_iota(jnp.int32, (GDN_W, GDN_W), 0)
    j = lax.broadcasted_iota(jnp.int32, (GDN_W, GDN_W), 1)
    return jnp.where(jnp.bitwise_xor(i, j) < CH, 1.0, 0.0).astype(bf16)


def _head_sums(x, ones):
    x1, x2, _ = _split3(x)
    return jnp.dot(x1, ones, preferred_element_type=f32) + jnp.dot(x2, ones, preferred_element_type=f32)


def _l2n(y, ones):
    r = lax.rsqrt(_head_sums(y * y, ones) + 1e-6)
    return y * r, r


def _l2n_bwd(dn, n, r, ones):
    return r * (dn - n * _head_sums(dn * n, ones))


def _gdn_local(inverse, q, k, vs, gc, bb):
    B = q.shape[0]
    ii = lax.broadcasted_iota(jnp.int32, (B, CH, CH), 1)
    jj = lax.broadcasted_iota(jnp.int32, (B, CH, CH), 2)
    gct = jnp.swapaxes(gc, 1, 2)
    decay = jnp.where(ii >= jj, jnp.exp(jnp.minimum(gc - gct, 0.0)), 0.0)
    kk = _bmm('bid,bjd->bij', k, k)
    inv = inverse(-jnp.where(ii > jj, bb * kk * decay, 0.0))
    egc = jnp.exp(gc)
    u = _bmm3('bij,bje->bie', inv, vs * bb)
    w = _bmm3('bij,bje->bie', inv, k * (bb * egc))
    qk = _bmm('bid,bjd->bij', q, k) * (0.125 * decay)
    glast = gc[:, CH - 1:CH, :]
    return u, w, qk, q * (0.125 * egc), k * jnp.exp(glast - gc), jnp.exp(jnp.broadcast_to(glast, gc.shape))


def _gdn_step(st, u, w, qk, qd, kt, egl, z, gn):
    vnew = u - _bmm('hcd,hde->hce', w, st)
    o = _bmm('hcd,hde->hce', qd, st) + _bmm('hij,hje->hie', qk, vnew)
    st2 = st * egl + _bmm('hcd,hce->hde', kt, vnew)
    out = o * lax.rsqrt(_rowsum(o * o) * (1.0 / CH) + 1e-6) * gn * _silu(z)
    return out, st2


def _gdn_scalars(ab, alog, dtb):
    sp = _softplus(ab + dtb)
    return -jnp.exp(alog) * sp, _sigmoid(ab)


def _bcast_heads(blk, lane0, H):
    R = blk.shape[0]
    n = R // CH
    parts = [jnp.broadcast_to(blk[:, lane0 + h:lane0 + h + 1], (R, CH)).reshape(n, CH, CH) for h in range(H)]
    return jnp.stack(parts, axis=1).reshape(n * H, CH, CH)


def _unbcast_heads(x, lane0, H):
    n = x.shape[0] // H
    R = n * CH
    s = jnp.sum(x, axis=2, keepdims=True).reshape(n, H, CH, 1)
    lane = lax.broadcasted_iota(jnp.int32, (R, 128), 1)
    acc = jnp.zeros((R, 128), f32)
    for h in range(H):
        acc = acc + jnp.where(lane == lane0 + h, jnp.broadcast_to(s[:, h].reshape(R, 1), (R, 128)), 0.0)
    return acc


def _gdn_fwd(hg, cw, alog, dtb, gn):
    T = hg.shape[0]
    R = RB_GDN
    nc = R // CH
    W3 = 3 * GDN_W
    H = GDN_H

    def body(h_ref, t_ref, cw_ref, al_ref, dt_ref, gn_ref, o_ref, st_ref, inv_ref, st, ext):
        first = pl.program_id(0) == 0

        @pl.when(first)
        def _():
            st[...] = jnp.zeros_like(st)

        def inverse(m):
            inv = _neumann_inv(m)
            inv_ref[...] = inv
            return inv

        tail = jnp.where(first, 0.0, t_ref[:, 0:W3])
        y = _silu(_conv_fwd(ext, h_ref[:, 0:W3], tail, cw_ref[...], R))
        ones = _head_ones()
        qs = _split_heads(_l2n(y[:, 0:GDN_W], ones)[0], H)
        ks = _split_heads(_l2n(y[:, GDN_W:2 * GDN_W], ones)[0], H)
        vs = _split_heads(y[:, 2 * GDN_W:W3], H)
        zh = _split_heads(h_ref[:, W3:W3 + GDN_W], H)
        g, beta = _gdn_scalars(h_ref[:, W3 + GDN_W:GDN_IN], al_ref[...], dt_ref[...])
        loc = _gdn_local(inverse, qs, ks, vs, _bcast_heads(_chunk_cumsum(g), 0, H), _bcast_heads(beta, H, H))
        gnv = gn_ref[...]
        outs = []
        s_cur = st[...]
        for c in range(nc):
            sl = slice(c * H, (c + 1) * H)
            st_ref[c] = s_cur
            out, s_cur = _gdn_step(s_cur, *(t[sl] for t in loc), zh[sl], gnv)
            outs.append(out)
        st[...] = s_cur
        _merge_heads(o_ref, jnp.concatenate(outs, axis=0), H)

    return pl.pallas_call(
        body, grid=(T // R,),
        in_specs=[pl.BlockSpec((R, GDN_IN), lambda i: (i, 0)), _prev_tail_spec(R, GDN_IN), _full_spec((4, W3)),
                  _full_spec((1, 128)), _full_spec((1, 128)), _full_spec((1, CH))],
        out_specs=[pl.BlockSpec((R, GDN_W), lambda i: (i, 0)), pl.BlockSpec((nc, H, CH, CH), lambda i: (i, 0, 0, 0)),
                   pl.BlockSpec((nc * H, CH, CH), lambda i: (i, 0, 0))],
        out_shape=[jax.ShapeDtypeStruct((T, GDN_W), f32), jax.ShapeDtypeStruct((T // CH, H, CH, CH), f32),
                   jax.ShapeDtypeStruct((T // CH * H, CH, CH), f32)],
        scratch_shapes=[pltpu.VMEM((H, CH, CH), f32), pltpu.VMEM((R + 8, W3), f32)],
        compiler_params=_cparams(("arbitrary",)), name="gdn_fwd")(hg, hg, cw, alog, dtb, gn)


def _gdn_bwd(hg, cw, alog, dtb, gn, states, invs, dout):
    T = hg.shape[0]
    R = RB_GDN
    nc = R // CH
    nb = T // R
    W3 = 3 * GDN_W
    H = GDN_H

    def body(h_ref, t_ref, cw_ref, al_ref, dt_ref, gn_ref, st_ref, inv_ref, do_ref,
             dh_ref, dcw_ref, dal_ref, ddt_ref, dgn_ref, dst, carry_dy, ext, wide):
        i = pl.program_id(0)
        first_blk = i == nb - 1

        @pl.when(i == 0)
        def _():
            dst[...] = jnp.zeros_like(dst)
            carry_dy[...] = jnp.zeros_like(carry_dy)
            for r in (dcw_ref, dal_ref, ddt_ref, dgn_ref):
                r[...] = jnp.zeros_like(r)

        tail = jnp.where(first_blk, 0.0, t_ref[:, 0:W3])
        ypre = _conv_fwd(ext, h_ref[:, 0:W3], tail, cw_ref[...], R)
        y = _silu(ypre)
        ones = _head_ones()
        qn, rq = _l2n(y[:, 0:GDN_W], ones)
        kn, rk = _l2n(y[:, GDN_W:2 * GDN_W], ones)
        qs, ks, vs = _split_heads(qn, H), _split_heads(kn, H), _split_heads(y[:, 2 * GDN_W:W3], H)
        zh = _split_heads(h_ref[:, W3:W3 + GDN_W], H)
        ab = h_ref[:, W3 + GDN_W:GDN_IN]
        alog, dtb = al_ref[...], dt_ref[...]
        g, beta = _gdn_scalars(ab, alog, dtb)
        kept = inv_ref[...]
        loc, vjp_loc = jax.vjp(functools.partial(_gdn_local, lambda m: _known_inv(m, kept)), qs, ks, vs,
                               _bcast_heads(_chunk_cumsum(g), 0, H), _bcast_heads(beta, H, H))
        doh = _split_heads(do_ref[...], H)
        gnv = gn_ref[...]
        dloc = [[None] * nc for _ in range(6)]
        dzs = [None] * nc
        ds = dst[...]
        dgn = jnp.zeros((1, CH), f32)
        for c in reversed(range(nc)):
            sl = slice(c * H, (c + 1) * H)
            _, vjp = jax.vjp(_gdn_step, st_ref[c], *(t[sl] for t in loc), zh[sl], gnv)
            grads = vjp((doh[sl], ds))
            ds = grads[0]
            for j in range(6):
                dloc[j][c] = grads[1 + j]
            dzs[c] = grads[7]
            dgn = dgn + grads[8]
        dst[...] = ds
        dgn_ref[...] += dgn
        dqs, dks, dvs, dgb, dbb = vjp_loc(tuple(jnp.concatenate(d, axis=0) for d in dloc))
        lane = lax.broadcasted_iota(jnp.int32, (R, 128), 1)
        dg = _chunk_cumsum(_unbcast_heads(dgb, 0, H), reverse=True)
        dbeta = _unbcast_heads(dbb, H, H)
        da = dg * (-jnp.exp(alog)) * _sigmoid(ab + dtb)
        dh_ref[:, W3 + GDN_W:GDN_IN] = jnp.where(lane < H, da, dbeta * beta * (1.0 - beta))
        ddt_ref[...] += jnp.sum(jnp.where(lane < H, da, 0.0), 0, keepdims=True)
        dal_ref[...] += jnp.sum(jnp.where(lane < H, dg * g, 0.0), 0, keepdims=True)
        _merge_heads(dh_ref, jnp.concatenate(dzs, axis=0), H, col0=W3)
        for j, dpart in enumerate((dqs, dks, dvs)):
            _merge_heads(wide, dpart, H, col0=j * GDN_W)
        wide[:, 0:GDN_W] = _l2n_bwd(wide[:, 0:GDN_W], qn, rq, ones)
        wide[:, GDN_W:2 * GDN_W] = _l2n_bwd(wide[:, GDN_W:2 * GDN_W], kn, rk, ones)
        dy = wide[...] * _dsilu(ypre)
        dx, dcw = _conv_bwd(ext, dy, carry_dy[...], cw_ref[...], R)
        carry_dy[...] = dy[0:8, :]
        dcw_ref[...] += dcw
        dh_ref[:, 0:W3] = dx

    blk = pl.BlockSpec((R, GDN_IN), lambda i: (nb - 1 - i, 0))
    return pl.pallas_call(
        body, grid=(nb,),
        in_specs=[blk, _prev_tail_spec_rev(R, GDN_IN, nb), _full_spec((4, W3)), _full_spec((1, 128)),
                  _full_spec((1, 128)), _full_spec((1, CH)),
                  pl.BlockSpec((nc, H, CH, CH), lambda i: (nb - 1 - i, 0, 0, 0)),
                  pl.BlockSpec((nc * H, CH, CH), lambda i: (nb - 1 - i, 0, 0)),
                  pl.BlockSpec((R, GDN_W), lambda i: (nb - 1 - i, 0))],
        out_specs=[blk, _full_spec((4, W3)), _full_spec((1, 128)), _full_spec((1, 128)), _full_spec((1, CH))],
        out_shape=[jax.ShapeDtypeStruct((T, GDN_IN), f32), jax.ShapeDtypeStruct((4, W3), f32),
                   jax.ShapeDtypeStruct((1, 128), f32), jax.ShapeDtypeStruct((1, 128), f32),
                   jax.ShapeDtypeStruct((1, CH), f32)],
        scratch_shapes=[pltpu.VMEM((H, CH, CH), f32), pltpu.VMEM((8, W3), f32), pltpu.VMEM((R + 8, W3), f32),
                        pltpu.VMEM((R, W3), f32)],
        compiler_params=_cparams(("arbitrary",)), name="gdn_bwd")(hg, hg, cw, alog, dtb, gn, states, invs, dout)


def _block_diag(w):
    out = jnp.zeros((LRU_W, LRU_W), w.dtype)
    for g in range(w.shape[0]):
        out = lax.dynamic_update_slice(out, w[g], (g * CH, g * CH))
    return out


def _block_diag_t(w):
    return jnp.stack([w[g * CH:(g + 1) * CH, g * CH:(g + 1) * CH] for g in range(LRU_W // CH)])


def _pad_lanes(v, n=128):
    return jnp.pad(v, (0, n - v.shape[0]))[None, :]


def _local_step(x, p, positions, target, fetch, emit, sm):
    cosw, sinw = _rope_tables(positions)
    cols = lambda w: jnp.transpose(w, (1, 2, 0, 3)).reshape(-1, D, NDEV * FSP)
    saved = []
    h = x
    for l in range(DEPTH):
        v = lambda n: sm[n][l][None, :]
        F1, tok = fetch(l, 'f1', h)
        p384a, pda = cols(F1['p384']), F1['pd']
        z1, x1, g1, u1 = _ffn_fwd(h, p384a, pda, v('ln_ffn1_g') + tok, v('ln_ffn1_b'), 0, 0)
        G, _ = fetch(l, 'rest', x1)
        p384, pd, pr, pinl, ping, wpp = cols(G['p384']), G['pd'], G['pr'], G['pinl'], G['ping'], G['wpp']
        wts = (p384a, pda, p384, pd, pr, pinl, ping, wpp)
        hr, hl, hg = _proj_in(x1, pr, pinl, ping, 0)
        o_r, rst = _ret_fwd(hr, cosw, sinw, v('ret_norm_g'))
        lru_args = (sm['lru_conv_w'][l], v('lru_conv_b'), _block_diag(sm['lru_w_a'][l]), v('lru_b_a'),
                    _block_diag(sm['lru_w_x'][l]), v('lru_b_x'), v('lru_lambda'))
        o_l, hs = _lru_fwd(hl, *lru_args)
        gdn_args = (sm['gdn_conv_w'][l], _pad_lanes(sm['gdn_a_log'][l]), _pad_lanes(sm['gdn_dt_bias'][l]),
                    v('gdn_norm_g'))
        o_g, *gst = _gdn_fwd(hg, *gdn_args)
        z2, x2 = _mix_out(x1, o_r, o_l, o_g, pr, v('ln_mix_g'), v('ln_mix_b'), 0)
        z3, x3, g2, u2 = _ffn_fwd(x2, p384, pd, v('ln_ffn2_g'), v('ln_ffn2_b'), 0, 1, ple=(p[l], pr, wpp))
        saved.append((h, z1, x1, hr, hl, hg, o_r, rst, o_l, hs, lru_args, o_g, gst, gdn_args, z2, x2, z3,
                      g1, u1, g2, u2, wts))
        h = x3
    d, loss = _loss_grad(h, target)

    small = {n: [None] * DEPTH for n in SMALL}
    tok = 0.0
    for l in reversed(range(DEPTH)):
        (x0, z1, x1, hr, hl, hg, o_r, rst, o_l, hs, lru_args, o_g, gst, gdn_args, z2, x2, z3,
         g1, u1, g2, u2, wts) = saved[l]
        p384a, pda, p384, pd, pr, pinl, ping, wpp = wts
        v = lambda n: sm[n][l][None, :]
        rows = lambda m: m.reshape(NDEV, m.shape[1] // NDEV, m.shape[2])
        d2, dg2, du2, a2, dy2, small['ln_ffn2_g'][l], small['ln_ffn2_b'][l] = _ffn_bwd(
            z3, d, g2, u2, p384, pd, v('ln_ffn2_g') + tok, 0, 1)
        d2, dgp, dpj = _ple_bwd(x2, p[l], dy2, d2, pr, wpp, 0)
        dxb, dzb, do_r, do_l, do_g, small['ln_mix_g'][l], small['ln_mix_b'][l] = _mix_out_bwd(
            z2, d2, pr, v('ln_mix_g'), 0)
        dhr, small['ret_norm_g'][l] = _ret_bwd(hr, cosw, sinw, v('ret_norm_g'), rst, do_r)
        (dhl, small['lru_conv_w'][l], small['lru_conv_b'][l], dwa, small['lru_b_a'][l], dwx, small['lru_b_x'][l],
         small['lru_lambda'][l]) = _lru_bwd(hl, hs, *lru_args, do_l)
        small['lru_w_a'][l], small['lru_w_x'][l] = _block_diag_t(dwa), _block_diag_t(dwx)
        dhg, small['gdn_conv_w'][l], dal, ddt, small['gdn_norm_g'][l] = _gdn_bwd(hg, *gdn_args, *gst, do_g)
        small['gdn_a_log'][l], small['gdn_dt_bias'][l] = dal[:, 0:GDN_H], ddt[:, 0:GDN_H]
        d1 = _proj_in_bwd(dxb, dhr, dhl, dhg, pr, pinl, ping, 0)
        dwo = jnp.concatenate([_matmul_tn(o_r, dzb, D, "dw_out_r"), _matmul_tn(o_l, dzb, D, "dw_out_l"),
                               _matmul_tn(o_g, dzb, D, "dw_out_g")], axis=1)
        tok = emit(l, 'rest', {
            'p384': jnp.stack([_matmul_tn(x2, dg2, FSP, "dw_gate", NDEV), _matmul_tn(x2, du2, FSP, "dw_up", NDEV)],
                              axis=1),
            'pd': rows(_matmul_tn(a2, dy2, D, "dw_down"))[:, None],
            'pr': jnp.stack([rows(_matmul_tn(x1, dhr, D, "dw_in_r")), rows(dwo),
                             rows(_matmul_tn(x2, dgp, D, "dw_ple_gate"))], axis=1),
            'pinl': rows(_matmul_tn(x1, dhl, 2 * LRU_W, "dw_in_l"))[:, None],
            'ping': rows(_matmul_tn(x1, dhg, GDN_IN, "dw_in_g"))[:, None],
            'ppp': jnp.transpose(_matmul_tn(p[l], dpj, D, "dw_ple_proj").reshape(PLE, NDEV, 128), (1, 0, 2))[:, None]})
        d, dg1, du1, a1, dy1, small['ln_ffn1_g'][l], small['ln_ffn1_b'][l] = _ffn_bwd(
            z1, d1, g1, u1, p384a, pda, v('ln_ffn1_g') + tok, 0, 0)
        tok = emit(l, 'f1', {
            'p384': jnp.stack([_matmul_tn(x0, dg1, FSP, "dw_gate", NDEV), _matmul_tn(x0, du1, FSP, "dw_up", NDEV)],
                              axis=1),
            'pd': rows(_matmul_tn(a1, dy1, D, "dw_down"))[:, None]})
    small = {n: jnp.stack([g.reshape(sm[n].shape[1:]) for g in gs]) for n, gs in small.items()}
    return loss, d, small


def _pack_big(ws, dtype=bf16):
    padc = lambda a, n: jnp.pad(a, ((0, 0), (0, 0), (0, n - a.shape[2])))
    padr = lambda a, n: jnp.pad(a, ((0, 0), (0, n - a.shape[1]), (0, 0)))
    per_layer = lambda arrs: jnp.stack(arrs, axis=1).reshape((-1,) + arrs[0].shape[1:])
    w_in = ws['w_in']
    out = {
        'p384': per_layer([padc(ws[n], FSP) for n in ('ffn1_w_gate', 'ffn1_w_up', 'ffn2_w_gate', 'ffn2_w_up')]),
        'pd': per_layer([padr(ws[n], FSP) for n in ('ffn1_w_down', 'ffn2_w_down')]),
        'pr': per_layer([w_in[:, :, 0:D], ws['w_out'], ws['ple_w_gate']]),
        'pinl': w_in[:, :, D:D + 2 * LRU_W],
        'ping': padc(w_in[:, :, D + 2 * LRU_W:D_IN], GDN_IN),
        'ppp': ws['ple_w_proj'],
    }
    return {k: a.astype(dtype) for k, a in out.items()}


def _exchange(arrays, scatter, name):
    n = len(arrays)

    def body(*refs):
        ins, outs = refs[:n], refs[n:2 * n]
        send_sems, recv_sems, local_sems = refs[2 * n:]
        x, y, c = lax.axis_index("x"), lax.axis_index("y"), lax.axis_index("c")
        me = 4 * x + 2 * y + c
        copies = []
        for i in range(n):
            src = ins[i].at[me] if scatter[i] else ins[i]
            cp = pltpu.make_async_copy(src, outs[i].at[me], local_sems.at[i])
            cp.start()
            copies.append(cp)
        sends = []
        for j in range(1, NDEV):
            peer = (me + j) % NDEV
            pid = (peer // 4, (peer // 2) % 2, peer % 2)
            for i in range(n):
                src = ins[i].at[peer] if scatter[i] else ins[i]
                cp = pltpu.make_async_remote_copy(
                    src_ref=src, dst_ref=outs[i].at[me], send_sem=send_sems.at[i, j - 1],
                    recv_sem=recv_sems.at[i, j - 1], device_id=pid, device_id_type=pl.DeviceIdType.MESH)
                cp.start()
                sends.append(cp)
        for j in range(1, NDEV):
            source = (me + NDEV - j) % NDEV
            sid = (source // 4, (source // 2) % 2, source % 2)
            for i in range(n):
                src = ins[i].at[me] if scatter[i] else ins[i]
                pltpu.make_async_remote_copy(
                    src_ref=src, dst_ref=outs[i].at[source], send_sem=send_sems.at[i, j - 1],
                    recv_sem=recv_sems.at[i, j - 1], device_id=sid, device_id_type=pl.DeviceIdType.MESH).wait_recv()
        for cp in sends:
            cp.wait_send()
        for cp in copies:
            cp.wait()

    hbm = pl.BlockSpec(memory_space=pltpu.HBM)
    out_shape = [jax.ShapeDtypeStruct(a.shape if s else (NDEV,) + a.shape, a.dtype) for a, s in zip(arrays, scatter)]
    return pl.pallas_call(
        body, in_specs=[hbm] * n, out_specs=[hbm] * n, out_shape=out_shape,
        scratch_shapes=[pltpu.SemaphoreType.DMA((n, NDEV - 1)), pltpu.SemaphoreType.DMA((n, NDEV - 1)),
                        pltpu.SemaphoreType.DMA((n,))],
        compiler_params=pltpu.CompilerParams(has_side_effects=True), name=name)(*arrays)


def _gather_two_level(arrays, name):
    n = len(arrays)

    def body(*refs):
        ins, outs = refs[:n], refs[n:2 * n]
        send_sems, recv_sems, local_sems = refs[2 * n:]
        x, y, c = lax.axis_index("x"), lax.axis_index("y"), lax.axis_index("c")
        me, sibling = (x, y, c), (x, y, 1 - c)
        chips = [(1 - x, y), (x, 1 - y), (1 - x, 1 - y)]
        slot = lambda d: 4 * d[0] + 2 * d[1] + d[2]

        def copy(i, k, block, to, src=None):
            return pltpu.make_async_remote_copy(
                src_ref=outs[i].at[slot(block)] if src is None else src, dst_ref=outs[i].at[slot(block)],
                send_sem=send_sems.at[i, k], recv_sem=recv_sems.at[i, k], device_id=to,
                device_id_type=pl.DeviceIdType.MESH)

        mine, first, passed = [], [], []
        for i in range(n):
            cp = pltpu.make_async_copy(ins[i], outs[i].at[slot(me)], local_sems.at[i])
            cp.start()
            mine.append(cp)
            first.append(copy(i, 0, me, sibling, src=ins[i]))
            first += [copy(i, 1 + j, me, (*chip, c), src=ins[i]) for j, chip in enumerate(chips)]
        for cp in first:
            cp.start()
        for i in range(n):
            for j, chip in enumerate(chips):
                copy(i, 1 + j, (*chip, c), me).wait_recv()
                cp = copy(i, 4 + j, (*chip, c), sibling)
                cp.start()
                passed.append(cp)
        for i in range(n):
            copy(i, 0, sibling, me).wait_recv()
            for j, chip in enumerate(chips):
                copy(i, 4 + j, (*chip, 1 - c), me).wait_recv()
        for cp in first + passed:
            cp.wait_send()
        for cp in mine:
            cp.wait()

    hbm = pl.BlockSpec(memory_space=pltpu.HBM)
    return pl.pallas_call(
        body, in_specs=[hbm] * n, out_specs=[hbm] * n,
        out_shape=[jax.ShapeDtypeStruct((NDEV,) + a.shape, a.dtype) for a in arrays],
        scratch_shapes=[pltpu.SemaphoreType.DMA((n, NDEV - 1)), pltpu.SemaphoreType.DMA((n, NDEV - 1)),
                        pltpu.SemaphoreType.DMA((n,))],
        compiler_params=pltpu.CompilerParams(has_side_effects=True), name=name)(*arrays)


def _scatter_pairs(arrays, name):
    n = len(arrays)

    def body(*refs):
        ins, gots = refs[:n], refs[n:2 * n]
        send_sems, recv_sems = refs[2 * n:]
        x, y, c = lax.axis_index("x"), lax.axis_index("y"), lax.axis_index("c")
        sends = []
        for i in range(n):
            for q in range(4):
                cp = pltpu.make_async_remote_copy(
                    src_ref=ins[i].at[2 * q + 1 - c], dst_ref=gots[i].at[q], send_sem=send_sems.at[i, q],
                    recv_sem=recv_sems.at[i, q], device_id=(x, y, 1 - c), device_id_type=pl.DeviceIdType.MESH)
                cp.start()
                sends.append(cp)
        for cp in sends:
            cp.wait_recv()
        for cp in sends:
            cp.wait_send()

    hbm = pl.BlockSpec(memory_space=pltpu.HBM)
    return pl.pallas_call(
        body, in_specs=[hbm] * n, out_specs=[hbm] * n,
        out_shape=[jax.ShapeDtypeStruct((4,) + a.shape[1:], a.dtype) for a in arrays],
        scratch_shapes=[pltpu.SemaphoreType.DMA((n, 4)), pltpu.SemaphoreType.DMA((n, 4))],
        compiler_params=pltpu.CompilerParams(has_side_effects=True), name=name)(*arrays)


def _pair_sum(own, got, name):
    def body(a_ref, b_ref, o_ref):
        o_ref[...] = (a_ref[...].astype(f32) + b_ref[...].astype(f32)).astype(bf16)

    spec = pl.BlockSpec((None, None) + own.shape[2:], lambda q, s: (q, s, 0, 0))
    return pl.pallas_call(
        body, grid=own.shape[:2], in_specs=[spec, spec], out_specs=spec,
        out_shape=jax.ShapeDtypeStruct(own.shape, bf16),
        compiler_params=_cparams(("arbitrary", "arbitrary")), name=name)(own, got)


def _gather_plan(srcs, lands, x, y, c):
    me = 4 * x + 2 * y + c
    sends, arrivals = [], []
    for j in range(1, NDEV):
        peer, source = (me + j) % NDEV, (me + NDEV - j) % NDEV
        for i in range(len(srcs)):
            k = i * (NDEV - 1) + j - 1
            sends.append((srcs[i], lands[i].at[me], (peer // 4, (peer // 2) % 2, peer % 2), k))
            arrivals.append((srcs[i], lands[i].at[source], (source // 4, (source // 2) % 2, source % 2), k))
    return sends, arrivals


def _chips_plan(srcs, lands, x, y, c):
    chip = 2 * x + y
    sends, arrivals = [], []
    for j in range(1, 4):
        peer, source = (chip + j) % 4, (chip + 4 - j) % 4
        for i in range(len(srcs)):
            k = i * 3 + j - 1
            sends.append((srcs[i].at[peer], lands[i].at[chip], (peer // 2, peer % 2, c), k))
            arrivals.append((srcs[i].at[chip], lands[i].at[source], (source // 2, source % 2, c), k))
    return sends, arrivals


def _remote(entry, send_sems, recv_sems):
    src, dst, dev, k = entry
    return pltpu.make_async_remote_copy(src_ref=src, dst_ref=dst, send_sem=send_sems.at[k], recv_sem=recv_sems.at[k],
                                        device_id=dev, device_id_type=pl.DeviceIdType.MESH)


_HBM = pl.BlockSpec(memory_space=pltpu.HBM)
_SEM = pl.BlockSpec(memory_space=pltpu.SEMAPHORE)


def _split_start(arrays, land_shapes, plan, npeer, name):
    n = len(arrays)

    def body(*refs):
        srcs, lands = refs[:n], refs[n:2 * n]
        send_sems, recv_sems, token = refs[2 * n], refs[2 * n + 1], refs[-1]
        sends, _ = plan(srcs, lands, lax.axis_index("x"), lax.axis_index("y"), lax.axis_index("c"))
        for entry in sends:
            _remote(entry, send_sems, recv_sems).start()
        token[...] = jnp.zeros_like(token)

    lands = [lax.empty(s, a.dtype) for s, a in zip(land_shapes, arrays)]
    thru = [pltpu.HBM(a.shape, a.dtype) for a in arrays + lands]
    out = pl.pallas_call(
        body, name=name, in_specs=[_HBM] * (2 * n),
        out_specs=(_SEM, _SEM, *([_HBM] * (2 * n)), pl.BlockSpec(memory_space=pltpu.VMEM)),
        out_shape=(pltpu.SemaphoreType.DMA((n * npeer,)), pltpu.SemaphoreType.DMA((n * npeer,)), *thru,
                   jax.ShapeDtypeStruct((8, 128), f32)),
        input_output_aliases={i: 2 + i for i in range(2 * n)},
        compiler_params=pltpu.CompilerParams(has_side_effects=pltpu.SideEffectType.DATAFLOW_SIDE_EFFECTING),
    )(*[pltpu.with_memory_space_constraint(a, pltpu.HBM) for a in arrays + lands])
    return out[0], out[1], list(out[2:2 + n]), list(out[2 + n:2 + 2 * n]), out[-1]


def _split_wait(send_sems, recv_sems, srcs, lands, after, plan, name):
    n = len(srcs)

    def body(*refs):
        s_refs, l_refs = refs[:n], refs[n:2 * n]
        ssem, rsem = refs[2 * n], refs[2 * n + 1]
        sends, arrivals = plan(s_refs, l_refs, lax.axis_index("x"), lax.axis_index("y"), lax.axis_index("c"))
        for entry in sends:
            _remote(entry, ssem, rsem).wait_send()
        for entry in arrivals:
            _remote(entry, ssem, rsem).wait_recv()

    out = pl.pallas_call(
        body, name=name, in_specs=[_HBM] * (2 * n) + [_SEM, _SEM, pl.BlockSpec(memory_space=pl.ANY)],
        out_specs=[_HBM] * (2 * n), out_shape=[pltpu.HBM(a.shape, a.dtype) for a in srcs + lands],
        input_output_aliases={i: i for i in range(2 * n)},
        compiler_params=pltpu.CompilerParams(has_side_effects=pltpu.SideEffectType.DATAFLOW_SIDE_EFFECTING),
    )(*srcs, *lands, send_sems, recv_sems, after)
    return list(out[:n]), list(out[n:])


def _adam_math(w, g, m, v):
    m2 = ADAM_B1 * m + (1.0 - ADAM_B1) * g
    v2 = ADAM_B2 * v + (1.0 - ADAM_B2) * (g * g)
    m_hat = m2 / (1.0 - ADAM_B1 ** ADAM_STEP)
    v_hat = v2 / (1.0 - ADAM_B2 ** ADAM_STEP)
    return -ADAM_LR * (m_hat / (jnp.sqrt(v_hat) + ADAM_EPS) + ADAM_WD * w), m2, v2


def _adam_big(parts, w, m, v, name):
    L, rows, cols = w.shape
    flat = [(a, slot) for layer_parts in parts for a, slot in layer_parts]
    per = len(parts[0])

    def body(*refs):
        prefs = refs[:len(flat)]
        w_ref, m_ref, v_ref, g_ref, d_ref, m2_ref, v2_ref = refs[len(flat):]
        for li in range(L):
            @pl.when(pl.program_id(0) == li)
            def _():
                c0 = 0
                for pref in prefs[li * per:(li + 1) * per]:
                    acc = pref[0].astype(f32)
                    for s in range(1, pref.shape[0]):
                        acc = acc + pref[s].astype(f32)
                    width = min(acc.shape[1], cols - c0)
                    g_ref[:, c0:c0 + width] = acc[0:rows, 0:width]
                    c0 += width

        d, m2, v2 = _adam_math(w_ref[...], g_ref[...], m_ref[...], v_ref[...])
        d_ref[...] = d
        m2_ref[...] = m2
        v2_ref[...] = v2

    wspec = pl.BlockSpec((None, rows, cols), lambda l: (l, 0, 0))
    in_specs = [pl.BlockSpec((a.shape[0], None) + a.shape[2:], functools.partial(lambda l, slot: (0, slot, 0, 0), slot=slot))
                for a, slot in flat]
    return pl.pallas_call(
        body, grid=(L,), in_specs=in_specs + [wspec] * 3, out_specs=[wspec] * 4,
        out_shape=[jax.ShapeDtypeStruct(w.shape, f32)] * 4,
        compiler_params=_cparams(("arbitrary",)), name=name)(*[a for a, _ in flat], w, m, v)


def _sum_sources(stacked):
    rows = stacked.shape[1]

    def body(s_ref, o_ref):
        acc = s_ref[0]
        for s in range(1, NDEV):
            acc = acc + s_ref[s]
        o_ref[...] = acc

    return pl.pallas_call(body, out_shape=jax.ShapeDtypeStruct((rows, 128), f32), name="sum_small_grads")(stacked)


def _adam_small(w, g, m, v):
    def body(w_ref, g_ref, m_ref, v_ref, d_ref, m2_ref, v2_ref):
        d, m2, v2 = _adam_math(w_ref[...], g_ref[...], m_ref[...], v_ref[...])
        d_ref[...] = d
        m2_ref[...] = m2
        v2_ref[...] = v2

    return pl.pallas_call(body, out_shape=[jax.ShapeDtypeStruct(w.shape, f32)] * 3, name="adam_small")(w, g, m, v)


def _pack_rows(arrs):
    flat = []
    for a in arrs:
        a = a.reshape(-1)
        flat.append(jnp.pad(a, (0, (-a.shape[0]) % 1024)))
    return jnp.concatenate(flat).reshape(-1, 128)


def _unpack_rows(packed, shapes):
    out, off = [], 0
    flat = packed.reshape(-1)
    for s in shapes:
        n = math.prod(s)
        out.append(flat[off:off + n].reshape(s))
        off += n + (-n) % 1024
    return out


def _gather_conv(gathered, shape):
    L, K, c = shape
    return jnp.transpose(gathered, (1, 2, 0, 3)).reshape(L, K, NDEV * c)


def kernel(x, p, positions, ln_ffn1_g, ln_ffn1_b, ffn1_w_gate, ffn1_w_up, ffn1_w_down, w_in, ret_norm_g, lru_conv_w, lru_conv_b, lru_w_a, lru_b_a, lru_w_x, lru_b_x, lru_lambda, gdn_conv_w, gdn_a_log, gdn_dt_bias, gdn_norm_g, w_out, ln_mix_g, ln_mix_b, ffn2_w_gate, ffn2_w_up, ffn2_w_down, ple_w_gate, ple_w_proj, ln_ffn2_g, ln_ffn2_b, loss_target, m_ln_ffn1_g, m_ln_ffn1_b, m_ffn1_w_gate, m_ffn1_w_up, m_ffn1_w_down, m_w_in, m_ret_norm_g, m_lru_conv_w, m_lru_conv_b, m_lru_w_a, m_lru_b_a, m_lru_w_x, m_lru_b_x, m_lru_lambda, m_gdn_conv_w, m_gdn_a_log, m_gdn_dt_bias, m_gdn_norm_g, m_w_out, m_ln_mix_g, m_ln_mix_b, m_ffn2_w_gate, m_ffn2_w_up, m_ffn2_w_down, m_ple_w_gate, m_ple_w_proj, m_ln_ffn2_g, m_ln_ffn2_b, v_ln_ffn1_g, v_ln_ffn1_b, v_ffn1_w_gate, v_ffn1_w_up, v_ffn1_w_down, v_w_in, v_ret_norm_g, v_lru_conv_w, v_lru_conv_b, v_lru_w_a, v_lru_b_a, v_lru_w_x, v_lru_b_x, v_lru_lambda, v_gdn_conv_w, v_gdn_a_log, v_gdn_dt_bias, v_gdn_norm_g, v_w_out, v_ln_mix_g, v_ln_mix_b, v_ffn2_w_gate, v_ffn2_w_up, v_ffn2_w_down, v_ple_w_gate, v_ple_w_proj, v_ln_ffn2_g, v_ln_ffn2_b):
    args = locals()
    W = {n: args[n] for n in WEIGHTS}
    M = {n: args['m_' + n] for n in WEIGHTS}
    V = {n: args['v_' + n] for n in WEIGHTS}
    me = 4 * lax.axis_index("x") + 2 * lax.axis_index("y") + lax.axis_index("c")

    core = lax.axis_index("c")
    chip = 2 * lax.axis_index("x") + lax.axis_index("y")

    packed = _pack_big(W)

    def group(l, name):
        per = {k: packed[k].shape[0] // DEPTH for k in PACKS}
        if name == 'f1':
            return [packed['p384'][l * per['p384']:l * per['p384'] + 2], packed['pd'][l * per['pd']:l * per['pd'] + 1]]
        return [packed['p384'][l * per['p384'] + 2:(l + 1) * per['p384']],
                packed['pd'][l * per['pd'] + 1:(l + 1) * per['pd']]] + [
                    packed[k][l * per[k]:(l + 1) * per[k]] for k in PACKS[2:]]

    def as_weights(arrs):
        G = dict(zip(PACKS, arrs))
        if 'ppp' in G:
            G['wpp'] = jnp.transpose(G.pop('ppp'), (1, 2, 0, 3)).reshape(PLE, D)
        return G

    conv_pack = _pack_rows([W[n] for n in CONV_SHARDED])
    g0 = _gather_two_level(group(0, 'f1') + [conv_pack], "gather_weights")
    g0, rest0 = lax.optimization_barrier((g0, group(0, 'rest')))
    start0 = _split_start(rest0, [(NDEV,) + a.shape for a in rest0], _gather_plan, NDEV - 1, "gather_start_0")
    tok0, all1 = lax.optimization_barrier((start0[4], group(1, 'f1') + group(1, 'rest')))
    start1 = _split_start(all1, [(NDEV,) + a.shape for a in all1], _gather_plan, NDEV - 1, "gather_start_1")
    arrived = {}

    def gather_done(started, after, name):
        srcs, lands = _split_wait(started[0], started[1], started[2], started[3], after, _gather_plan, name)
        return [lax.dynamic_update_slice_in_dim(ld, s[None], me, axis=0) for s, ld in zip(srcs, lands)]

    def fetch(l, name, after):
        if l == 0 and name == 'f1':
            return as_weights(g0[:-1]), tok0[0, 0] + start1[4][0, 0]
        if l == 0:
            return as_weights(gather_done(start0, after, "gather_wait_0")), 0.0
        if name == 'f1':
            arrived[1] = gather_done(start1, after, "gather_wait_1")
            return as_weights(arrived[1][:2]), 0.0
        return as_weights(arrived[1][2:]), 0.0

    conv_all = g0[-1]
    sm = {n: W[n] for n in SMALL}
    conv_shards = [_unpack_rows(conv_all[s], [W[n].shape for n in CONV_SHARDED]) for s in range(NDEV)]
    for i, n in enumerate(CONV_SHARDED):
        sm[n] = _gather_conv(jnp.stack([cs[i] for cs in conv_shards]), W[n].shape)

    received, started = {}, {}

    def emit(l, name, grads):
        keys = list(grads)
        arrs = [grads[k] for k in keys]
        gots = _scatter_pairs(arrs, "scatter_pairs")
        owns = [lax.dynamic_index_in_dim(a.reshape((4, 2) + a.shape[1:]), core, axis=1, keepdims=False) for a in arrs]
        pair = [_pair_sum(o, g, "pair_sum_" + k) for k, o, g in zip(keys, owns, gots)]
        started[l, name] = (keys, _split_start(pair, [a.shape for a in pair], _chips_plan, 3,
                                               f"scatter_start_{l}_{name}"))
        return started[l, name][1][4][0, 0]

    def scatter_done(l, name, after):
        keys, st = started[l, name]
        srcs, lands = _split_wait(st[0], st[1], st[2], st[3], after, _chips_plan, f"scatter_wait_{l}_{name}")
        received[l, name] = dict(zip(keys, [
            lax.dynamic_update_slice_in_dim(ld, lax.dynamic_index_in_dim(s, chip, axis=0), chip, axis=0)
            for s, ld in zip(srcs, lands)]))

    loss, grad_x, small = _local_step(x[0], p[:, 0], positions.reshape(-1, 1), loss_target[0], fetch, emit, sm)
    loss = lax.psum(loss[0, 0], ("x", "y", "c"))
    last = (0, 'f1')
    for l, name in started:
        if (l, name) != last:
            scatter_done(l, name, grad_x)

    small_pack = _pack_rows([small[n] for n in SMALL])
    small_all = _exchange([small_pack], [False], "gather_small_grads")[0]
    small_sum = _unpack_rows(_sum_sources(small_all), [small[n].shape for n in SMALL])
    grads, delta, new_m, new_v = {}, {}, {}, {}
    for n, g in zip(SMALL, small_sum):
        if n in CONV_SHARDED:
            c = W[n].shape[2]
            g = lax.dynamic_slice_in_dim(g, me * c, c, axis=2)
        grads[n] = g

    big_parts = {
        'ffn1_w_gate': [('f1', 'p384', 0)], 'ffn1_w_up': [('f1', 'p384', 1)], 'ffn1_w_down': [('f1', 'pd', 0)],
        'ffn2_w_gate': [('rest', 'p384', 0)], 'ffn2_w_up': [('rest', 'p384', 1)], 'ffn2_w_down': [('rest', 'pd', 0)],
        'w_in': [('rest', 'pr', 0), ('rest', 'pinl', 0), ('rest', 'ping', 0)], 'w_out': [('rest', 'pr', 1)],
        'ple_w_gate': [('rest', 'pr', 2)], 'ple_w_proj': [('rest', 'ppp', 0)],
    }
    def adam(n):
        parts = [[(received[l, grp][k], slot) for grp, k, slot in big_parts[n]] for l in range(DEPTH)]
        grads[n], delta[n], new_m[n], new_v[n] = _adam_big(parts, W[n], M[n], V[n], "adam_" + n)

    shapes = [W[n].shape for n in SMALL]
    d_s, m_s, v_s = _adam_small(*[_pack_rows([src[n] for n in SMALL]) for src in (W, grads, M, V)])
    for n, dd, mm, vv in zip(SMALL, _unpack_rows(d_s, shapes), _unpack_rows(m_s, shapes), _unpack_rows(v_s, shapes)):
        delta[n], new_m[n], new_v[n] = dd, mm, vv
    waits_last = [n for n in BIG if big_parts[n][0][0] == last[1]]
    for n in BIG:
        if n not in waits_last:
            adam(n)
    done = jnp.stack([d_s[0, 0]] + [delta[n][0, 0, 0] for n in BIG if n not in waits_last])
    scatter_done(*last, done)
    for n in waits_last:
        adam(n)

    return (loss, grad_x[None], *[grads[n] for n in WEIGHTS], *[delta[n] for n in WEIGHTS],
            *[new_m[n] for n in WEIGHTS], *[new_v[n] for n in WEIGHTS])
```

```python
import functools
import math

import jax
import jax.numpy as jnp
from jax import lax
from jax.experimental import pallas as pl
from jax.experimental.pallas import tpu as pltpu

f32 = jnp.float32
bf16 = jnp.bfloat16

NDEV = 8
DEPTH = 2
D = 1024
FS = 352
FSP = 384
FB = 2
NF = NDEV // FB
PLE = 256
CH = 64
RET_H, GDN_H = 4, 6
RET_W, LRU_W, GDN_W = 256, 384, 384
GDN_IN = 1664
GDN_IN_REAL = 1548
D_IN = 3340
ALPHA = 4.0 ** 0.25
LN_EPS = 1e-5
ROPE_THETA = 10000.0
TM = 512
RB_RET, RB_LRU, RB_GDN = 512, 512, 256
VMEM_LIMIT = 56 * 1024 * 1024
ADAM_LR, ADAM_B1, ADAM_B2, ADAM_EPS, ADAM_WD, ADAM_STEP = 0.001, 0.9, 0.999, 1e-08, 0.01, 10

WEIGHTS = ['ln_ffn1_g', 'ln_ffn1_b', 'ffn1_w_gate', 'ffn1_w_up', 'ffn1_w_down', 'w_in', 'ret_norm_g', 'lru_conv_w',
           'lru_conv_b', 'lru_w_a', 'lru_b_a', 'lru_w_x', 'lru_b_x', 'lru_lambda', 'gdn_conv_w', 'gdn_a_log',
           'gdn_dt_bias', 'gdn_norm_g', 'w_out', 'ln_mix_g', 'ln_mix_b', 'ffn2_w_gate', 'ffn2_w_up', 'ffn2_w_down',
           'ple_w_gate', 'ple_w_proj', 'ln_ffn2_g', 'ln_ffn2_b']
BIG = ['ffn1_w_gate', 'ffn1_w_up', 'ffn1_w_down', 'w_in', 'w_out', 'ffn2_w_gate', 'ffn2_w_up', 'ffn2_w_down',
       'ple_w_gate', 'ple_w_proj']
SMALL = [n for n in WEIGHTS if n not in BIG]
PACKS = ('p384', 'pd', 'pr', 'pinl', 'ping', 'ppp')
CONV_SHARDED = {'lru_conv_w': LRU_W, 'gdn_conv_w': 3 * GDN_W}


def _cparams(sem=None):
    return pltpu.CompilerParams(dimension_semantics=sem, vmem_limit_bytes=VMEM_LIMIT)


def _sigmoid(x):
    return 1.0 / (1.0 + jnp.exp(-x))


def _silu(x):
    return x * _sigmoid(x)


def _dsilu(x):
    s = _sigmoid(x)
    return s * (1.0 + x * (1.0 - s))


def _softplus(x):
    return jnp.maximum(x, 0.0) + jnp.log(1.0 + jnp.exp(-jnp.abs(x)))


def _gelu(x):
    return 0.5 * x * (1.0 + jnp.tanh(0.7978845608028654 * (x + 0.044715 * x * x * x)))


def _dot(a, b):
    return jnp.dot(a.astype(bf16), b.astype(bf16), preferred_element_type=f32)


def _dot_nt(a, b):
    return lax.dot_general(a.astype(bf16), b.astype(bf16), (((1,), (1,)), ((), ())), preferred_element_type=f32)


def _dot_tn(a, b):
    return lax.dot_general(a.astype(bf16), b.astype(bf16), (((0,), (0,)), ((), ())), preferred_element_type=f32)


def _bmm(eq, a, b):
    return jnp.einsum(eq, a.astype(bf16), b.astype(bf16), preferred_element_type=f32)


def _split3(a):
    a1 = a.astype(bf16)
    r = a - a1.astype(f32)
    a2 = r.astype(bf16)
    return a1, a2, (r - a2.astype(f32)).astype(bf16)


def _bmm3(eq, a, b):
    a1, a2, _ = _split3(a)
    b1, b2, _ = _split3(b)
    e = lambda x, y: jnp.einsum(eq, x, y, preferred_element_type=f32)
    return e(a1, b1) + (e(a1, b2) + e(a2, b1))


def _rowsum(x):
    x1, x2, _ = _split3(x)
    ones = jnp.ones((x.shape[0], CH, CH), bf16)
    e = lambda y: jnp.einsum('bij,bjk->bik', y, ones, preferred_element_type=f32)
    return e(x1) + e(x2)


def _tri_ones(B, upper=False):
    ii = lax.broadcasted_iota(jnp.int32, (B, CH, CH), 1)
    jj = lax.broadcasted_iota(jnp.int32, (B, CH, CH), 2)
    return jnp.where((ii <= jj) if upper else (ii >= jj), 1.0, 0.0).astype(bf16)


def _cumsum_mm(t, x):
    x1, x2, x3 = _split3(x)
    e = lambda y: jnp.einsum('bij,bjk->bik', t, y, preferred_element_type=f32)
    return e(x1) + (e(x2) + e(x3))


def _chunk_cumsum(x, reverse=False):
    n = x.shape[0] // CH
    return _cumsum_mm(_tri_ones(n, upper=reverse), x.reshape(n, CH, 128)).reshape(x.shape)


@jax.custom_vjp
def _neumann_inv(m):
    ii = lax.broadcasted_iota(jnp.int32, m.shape, 1)
    jj = lax.broadcasted_iota(jnp.int32, m.shape, 2)
    inv = jnp.where(ii == jj, 1.0, 0.0).astype(f32) + m
    mp = m
    for _ in range(5):
        mp = _bmm3('bij,bjk->bik', mp, mp)
        inv = inv + _bmm3('bij,bjk->bik', inv, mp)
    return inv


def _neumann_inv_fwd(m):
    inv = _neumann_inv(m)
    return inv, inv


def _neumann_inv_bwd(inv, g):
    return (_bmm3('bij,bkj->bik', _bmm3('bji,bjk->bik', inv, g), inv),)


_neumann_inv.defvjp(_neumann_inv_fwd, _neumann_inv_bwd)


@jax.custom_vjp
def _known_inv(m, inv):
    return inv


def _known_inv_fwd(m, inv):
    return inv, inv


def _known_inv_bwd(inv, g):
    return _neumann_inv_bwd(inv, g)[0], jnp.zeros_like(inv)


_known_inv.defvjp(_known_inv_fwd, _known_inv_bwd)


def _ln_stats(z):
    mu = jnp.mean(z, -1, keepdims=True)
    zc = z - mu
    rstd = lax.rsqrt(jnp.mean(zc * zc, -1, keepdims=True) + LN_EPS)
    return zc * rstd, rstd


def _ln_bwd(z, g, dout):
    xh, rstd = _ln_stats(z)
    dxh = dout * g
    dz = rstd * (dxh - jnp.mean(dxh, -1, keepdims=True) - xh * jnp.mean(dxh * xh, -1, keepdims=True))
    return dz, jnp.sum(dout * xh, 0, keepdims=True), jnp.sum(dout, 0, keepdims=True)


def _full_spec(shape):
    nd = len(shape)
    return pl.BlockSpec(shape, lambda *_: (0,) * nd)


def _ffn_fwd(x, p384, pd, lg, lb, slot, which, ple=None):
    T = x.shape[0]
    sg, su, sd = 2 * slot, 2 * slot + 1, slot
    has_ple = ple is not None

    def body(*refs):
        if has_ple:
            (x_ref, wg_ref, wu_ref, wd_ref, lg_ref, lb_ref, p_ref, wpg_ref, wpp_ref,
             z_ref, o_ref, g_ref, u_ref, acc, xb_s) = refs
        else:
            x_ref, wg_ref, wu_ref, wd_ref, lg_ref, lb_ref, z_ref, o_ref, g_ref, u_ref, acc, xb_s = refs
        f = pl.program_id(1)

        @pl.when(f == 0)
        def _():
            x = x_ref[...]
            xb = x.astype(bf16)
            xb_s[...] = xb
            base = ALPHA * x
            if has_ple:
                gate = _sigmoid(_dot(xb, wpg_ref[...].reshape(D, D)))
                base = base + gate * _dot(p_ref[...], wpp_ref[...])
            acc[...] = base

        xb = xb_s[...]
        g = _dot(xb, wg_ref[...])
        u = _dot(xb, wu_ref[...])
        g_ref[...] = g.astype(bf16)
        u_ref[...] = u.astype(bf16)
        acc[...] += 0.5 * _dot(_silu(g) * u, wd_ref[...].reshape(FB * FSP, D))

        @pl.when(f == NF - 1)
        def _():
            z = acc[...]
            z_ref[...] = z
            o_ref[...] = _ln_stats(z)[0] * lg_ref[...] + lb_ref[...]

    row = pl.BlockSpec((TM, D), lambda i, f: (i, 0))
    in_specs = [row,
                pl.BlockSpec((None, D, FB * FSP), lambda i, f: (sg, 0, f)),
                pl.BlockSpec((None, D, FB * FSP), lambda i, f: (su, 0, f)),
                pl.BlockSpec((FB, None, FSP, D), lambda i, f: (f, sd, 0, 0)),
                _full_spec((1, D)), _full_spec((1, D))]
    args = [x, p384, p384, pd, lg, lb]
    if has_ple:
        p, pr, wpp = ple
        in_specs += [pl.BlockSpec((TM, PLE), lambda i, f: (i, 0)),
                     pl.BlockSpec((NDEV, None, 128, D), lambda i, f: (0, 2, 0, 0)),
                     _full_spec((PLE, D))]
        args += [p, pr, wpp]
    hid = pl.BlockSpec((TM, FB * FSP), lambda i, f: (i, f))
    hshape = jax.ShapeDtypeStruct((T, NDEV * FSP), bf16)
    return pl.pallas_call(
        body, grid=(T // TM, NF), in_specs=in_specs, out_specs=[row, row, hid, hid],
        out_shape=[jax.ShapeDtypeStruct((T, D), f32)] * 2 + [hshape, hshape],
        scratch_shapes=[pltpu.VMEM((TM, D), f32), pltpu.VMEM((TM, D), bf16)],
        compiler_params=_cparams(("arbitrary", "arbitrary")), name=f"ffn{which + 1}_fwd")(*args)


def _ffn_bwd(z, dout, gs, us, p384, pd, lg, slot, which):
    T = z.shape[0]
    TMB = TM
    sg, su, sd = 2 * slot, 2 * slot + 1, slot

    def body(z_ref, do_ref, g_ref, u_ref, wg_ref, wu_ref, wd_ref, lg_ref,
             dx_ref, dg_ref, du_ref, a_ref, dy_ref, dlg_ref, dlb_ref, acc, dyb):
        i, f = pl.program_id(0), pl.program_id(1)

        @pl.when(jnp.logical_and(i == 0, f == 0))
        def _():
            dlg_ref[...] = jnp.zeros_like(dlg_ref)
            dlb_ref[...] = jnp.zeros_like(dlb_ref)

        @pl.when(f == 0)
        def _():
            dz, dlg, dlb = _ln_bwd(z_ref[...], lg_ref[...], do_ref[...])
            dlg_ref[...] += dlg
            dlb_ref[...] += dlb
            dy = (0.5 * dz).astype(bf16)
            dyb[...] = dy
            dy_ref[...] = dy
            acc[...] = ALPHA * dz

        g = g_ref[...].astype(f32)
        u = u_ref[...].astype(f32)
        da = _dot_nt(dyb[...], wd_ref[...].reshape(FB * FSP, D))
        sgm = _sigmoid(g)
        dg = (da * u * (sgm * (1.0 + g * (1.0 - sgm)))).astype(bf16)
        du = (da * (g * sgm)).astype(bf16)
        dg_ref[...] = dg
        du_ref[...] = du
        a_ref[...] = (g * sgm * u).astype(bf16)
        acc[...] += _dot_nt(dg, wg_ref[...]) + _dot_nt(du, wu_ref[...])

        @pl.when(f == NF - 1)
        def _():
            dx_ref[...] = acc[...]

    row = pl.BlockSpec((TMB, D), lambda i, f: (i, 0))
    hid = pl.BlockSpec((TMB, FB * FSP), lambda i, f: (i, f))
    vec = _full_spec((1, D))
    in_specs = [row, row, hid, hid,
                pl.BlockSpec((None, D, FB * FSP), lambda i, f: (sg, 0, f)),
                pl.BlockSpec((None, D, FB * FSP), lambda i, f: (su, 0, f)),
                pl.BlockSpec((FB, None, FSP, D), lambda i, f: (f, sd, 0, 0)),
                vec]
    args = [z, dout, gs, us, p384, p384, pd, lg]
    out_specs = [row, hid, hid, hid, row, vec, vec]
    hshape = jax.ShapeDtypeStruct((T, NDEV * FSP), bf16)
    out_shape = [jax.ShapeDtypeStruct((T, D), f32), hshape, hshape, hshape, jax.ShapeDtypeStruct((T, D), bf16),
                 jax.ShapeDtypeStruct((1, D), f32), jax.ShapeDtypeStruct((1, D), f32)]
    return pl.pallas_call(
        body, grid=(T // TMB, NF), in_specs=in_specs, out_specs=out_specs, out_shape=out_shape,
        scratch_shapes=[pltpu.VMEM((TMB, D), f32), pltpu.VMEM((TMB, D), bf16)],
        compiler_params=_cparams(("arbitrary", "arbitrary")), name=f"ffn{which + 1}_bwd")(*args)


def _ple_bwd(x, p, dy, dx_ffn, pr, wpp, layer):
    T = x.shape[0]

    def body(x_ref, p_ref, dy_ref, dxf_ref, wpg_ref, wpp_ref, dx_ref, dgp_ref, dpj_ref):
        dz = 2.0 * dy_ref[...].astype(f32)
        wpg = wpg_ref[...].reshape(D, D)
        gate = _sigmoid(_dot(x_ref[...], wpg))
        proj = _dot(p_ref[...], wpp_ref[...])
        dgp = (dz * proj * gate * (1.0 - gate)).astype(bf16)
        dgp_ref[...] = dgp
        dpj_ref[...] = (dz * gate).astype(bf16)
        dx_ref[...] = dxf_ref[...] + _dot_nt(dgp, wpg)

    row = pl.BlockSpec((TM, D), lambda i: (i, 0))
    return pl.pallas_call(
        body, grid=(T // TM,),
        in_specs=[row, pl.BlockSpec((TM, PLE), lambda i: (i, 0)), row, row,
                  pl.BlockSpec((NDEV, None, 128, D), lambda i: (0, 3 * layer + 2, 0, 0)), _full_spec((PLE, D))],
        out_specs=[row, row, row],
        out_shape=[jax.ShapeDtypeStruct((T, D), f32), jax.ShapeDtypeStruct((T, D), bf16),
                   jax.ShapeDtypeStruct((T, D), bf16)],
        compiler_params=_cparams(("arbitrary",)), name="ple_bwd")(x, p, dy, dx_ffn, pr, wpp)


def _matmul_tn(a, b, nb, name, nsub=1):
    T, M = a.shape
    N = b.shape[1]
    wide = nsub * nb
    tk = min(T, 1024 if wide <= 2048 else 512)
    nk = T // tk

    def body(a_ref, b_ref, o_ref, acc):
        k = pl.program_id(1)

        @pl.when(k == 0)
        def _():
            acc[...] = jnp.zeros_like(acc)

        acc[...] += _dot_tn(a_ref[...], b_ref[...])

        @pl.when(k == nk - 1)
        def _():
            for j in range(nsub):
                o_ref[j] = acc[:, j * nb:(j + 1) * nb].astype(bf16)

    return pl.pallas_call(
        body, grid=(N // wide, nk),
        in_specs=[pl.BlockSpec((tk, M), lambda n, k: (k, 0)), pl.BlockSpec((tk, wide), lambda n, k: (k, n))],
        out_specs=pl.BlockSpec((nsub, M, nb), lambda n, k: (n, 0, 0)),
        out_shape=jax.ShapeDtypeStruct((N // nb, M, nb), bf16),
        scratch_shapes=[pltpu.VMEM((M, wide), f32)],
        compiler_params=_cparams(("arbitrary", "arbitrary")), name=name)(a, b)


def _proj_in(x, pr, pinl, ping, layer):
    T = x.shape[0]

    def body(x_ref, wr_ref, wl_ref, wg_ref, hr_ref, hl_ref, hg_ref):
        xb = x_ref[...].astype(bf16)
        hr_ref[...] = _dot(xb, wr_ref[...].reshape(D, D))
        hl_ref[...] = _dot(xb, wl_ref[...].reshape(D, 2 * LRU_W))
        hg_ref[...] = _dot(xb, wg_ref[...].reshape(D, GDN_IN))

    return pl.pallas_call(
        body, grid=(T // TM,),
        in_specs=[pl.BlockSpec((TM, D), lambda i: (i, 0)),
                  pl.BlockSpec((NDEV, None, 128, D), lambda i: (0, 3 * layer, 0, 0)),
                  pl.BlockSpec((NDEV, None, 128, 2 * LRU_W), lambda i: (0, layer, 0, 0)),
                  pl.BlockSpec((NDEV, None, 128, GDN_IN), lambda i: (0, layer, 0, 0))],
        out_specs=[pl.BlockSpec((TM, D), lambda i: (i, 0)), pl.BlockSpec((TM, 2 * LRU_W), lambda i: (i, 0)),
                   pl.BlockSpec((TM, GDN_IN), lambda i: (i, 0))],
        out_shape=[jax.ShapeDtypeStruct((T, D), f32), jax.ShapeDtypeStruct((T, 2 * LRU_W), f32),
                   jax.ShapeDtypeStruct((T, GDN_IN), f32)],
        compiler_params=_cparams(("arbitrary",)), name="proj_in")(x, pr, pinl, ping)


def _proj_in_bwd(base, dhr, dhl, dhg, pr, pinl, ping, layer):
    T = base.shape[0]

    def body(b_ref, dr_ref, dl_ref, dg_ref, wr_ref, wl_ref, wg_ref, o_ref):
        o_ref[...] = (b_ref[...] + _dot_nt(dr_ref[...], wr_ref[...].reshape(D, D))
                      + _dot_nt(dl_ref[...], wl_ref[...].reshape(D, 2 * LRU_W))
                      + _dot_nt(dg_ref[...], wg_ref[...].reshape(D, GDN_IN)))

    return pl.pallas_call(
        body, grid=(T // TM,),
        in_specs=[pl.BlockSpec((TM, D), lambda i: (i, 0)), pl.BlockSpec((TM, D), lambda i: (i, 0)),
                  pl.BlockSpec((TM, 2 * LRU_W), lambda i: (i, 0)), pl.BlockSpec((TM, GDN_IN), lambda i: (i, 0)),
                  pl.BlockSpec((NDEV, None, 128, D), lambda i: (0, 3 * layer, 0, 0)),
                  pl.BlockSpec((NDEV, None, 128, 2 * LRU_W), lambda i: (0, layer, 0, 0)),
                  pl.BlockSpec((NDEV, None, 128, GDN_IN), lambda i: (0, layer, 0, 0))],
        out_specs=pl.BlockSpec((TM, D), lambda i: (i, 0)),
        out_shape=jax.ShapeDtypeStruct((T, D), f32),
        compiler_params=_cparams(("arbitrary",)), name="proj_in_bwd")(base, dhr, dhl, dhg, pr, pinl, ping)


def _mix_out(x1, o_r, o_l, o_g, pr, lg, lb, layer):
    T = x1.shape[0]

    def body(x_ref, r_ref, l_ref, g_ref, w_ref, lg_ref, lb_ref, z_ref, o_ref):
        w = w_ref[...].reshape(D, D)
        z = (ALPHA * x_ref[...] + _dot(r_ref[...], w[0:RET_W]) + _dot(l_ref[...], w[RET_W:RET_W + LRU_W])
             + _dot(g_ref[...], w[RET_W + LRU_W:D]))
        z_ref[...] = z
        o_ref[...] = _ln_stats(z)[0] * lg_ref[...] + lb_ref[...]

    row = pl.BlockSpec((TM, D), lambda i: (i, 0))
    return pl.pallas_call(
        body, grid=(T // TM,),
        in_specs=[row, pl.BlockSpec((TM, RET_W), lambda i: (i, 0)), pl.BlockSpec((TM, LRU_W), lambda i: (i, 0)),
                  pl.BlockSpec((TM, GDN_W), lambda i: (i, 0)),
                  pl.BlockSpec((NDEV, None, 128, D), lambda i: (0, 3 * layer + 1, 0, 0)),
                  _full_spec((1, D)), _full_spec((1, D))],
        out_specs=[row, row], out_shape=[jax.ShapeDtypeStruct((T, D), f32)] * 2,
        compiler_params=_cparams(("arbitrary",)), name="mix_out")(x1, o_r, o_l, o_g, pr, lg, lb)


def _mix_out_bwd(z, dout, pr, lg, layer):
    T = z.shape[0]

    def body(z_ref, do_ref, w_ref, lg_ref, dxb_ref, dzb_ref, dr_ref, dl_ref, dg_ref, dlg_ref, dlb_ref):
        @pl.when(pl.program_id(0) == 0)
        def _():
            dlg_ref[...] = jnp.zeros_like(dlg_ref)
            dlb_ref[...] = jnp.zeros_like(dlb_ref)

        dz, dlg, dlb = _ln_bwd(z_ref[...], lg_ref[...], do_ref[...])
        dlg_ref[...] += dlg
        dlb_ref[...] += dlb
        dxb_ref[...] = ALPHA * dz
        dzb = dz.astype(bf16)
        dzb_ref[...] = dzb
        w = w_ref[...].reshape(D, D)
        dr_ref[...] = _dot_nt(dzb, w[0:RET_W])
        dl_ref[...] = _dot_nt(dzb, w[RET_W:RET_W + LRU_W])
        dg_ref[...] = _dot_nt(dzb, w[RET_W + LRU_W:D])

    row = pl.BlockSpec((TM, D), lambda i: (i, 0))
    vec = _full_spec((1, D))
    return pl.pallas_call(
        body, grid=(T // TM,),
        in_specs=[row, row, pl.BlockSpec((NDEV, None, 128, D), lambda i: (0, 3 * layer + 1, 0, 0)), vec],
        out_specs=[row, row, pl.BlockSpec((TM, RET_W), lambda i: (i, 0)), pl.BlockSpec((TM, LRU_W), lambda i: (i, 0)),
                   pl.BlockSpec((TM, GDN_W), lambda i: (i, 0)), vec, vec],
        out_shape=[jax.ShapeDtypeStruct((T, D), f32), jax.ShapeDtypeStruct((T, D), bf16),
                   jax.ShapeDtypeStruct((T, RET_W), f32), jax.ShapeDtypeStruct((T, LRU_W), f32),
                   jax.ShapeDtypeStruct((T, GDN_W), f32), jax.ShapeDtypeStruct((1, D), f32),
                   jax.ShapeDtypeStruct((1, D), f32)],
        compiler_params=_cparams(("arbitrary",)), name="mix_out_bwd")(z, dout, pr, lg)


def _loss_grad(y, target):
    T = y.shape[0]

    def body(y_ref, t_ref, dy_ref, l_ref):
        @pl.when(pl.program_id(0) == 0)
        def _():
            l_ref[...] = jnp.zeros_like(l_ref)

        e = y_ref[...] - t_ref[...]
        dy_ref[...] = e * (1.0 / D)
        l_ref[...] += 0.5 * jnp.sum(jnp.sum(e * e, -1, keepdims=True) * (1.0 / D), 0, keepdims=True)

    row = pl.BlockSpec((TM, D), lambda i: (i, 0))
    return pl.pallas_call(
        body, grid=(T // TM,), in_specs=[row, row], out_specs=[row, _full_spec((1, 1))],
        out_shape=[jax.ShapeDtypeStruct((T, D), f32), jax.ShapeDtypeStruct((1, 1), f32)],
        compiler_params=_cparams(("arbitrary",)), name="loss_grad")(y, target)


def _split_heads(x, H):
    n = x.shape[0] // CH
    parts = [x[:, h * CH:(h + 1) * CH].reshape(n, CH, CH) for h in range(H)]
    return jnp.stack(parts, axis=1).reshape(n * H, CH, CH)


def _merge_heads(ref, x, H, col0=0):
    n = x.shape[0] // H
    x4 = x.reshape(n, H, CH, CH)
    for h in range(H):
        ref[:, col0 + h * CH:col0 + (h + 1) * CH] = x4[:, h].reshape(n * CH, CH)


def _rows_down(x, before, s):
    r8 = lax.broadcasted_iota(jnp.int32, before.shape, 0)
    top = jnp.where(r8 < s, pltpu.roll(before, s, 0), pltpu.roll(x[0:8], s, 0))
    return jnp.concatenate([top, pltpu.roll(x, s, 0)[8:]], axis=0)


def _rows_up(x, after, s):
    R = x.shape[0]
    r8 = lax.broadcasted_iota(jnp.int32, after.shape, 0)
    bottom = jnp.where(r8 >= 8 - s, pltpu.roll(after, 8 - s, 0), pltpu.roll(x[R - 8:R], 8 - s, 0))
    return jnp.concatenate([pltpu.roll(x, R - s, 0)[0:R - 8], bottom], axis=0)


def _conv_fwd(ext, x, tail, w, R):
    ext[0:8, :] = tail
    ext[8:R + 8, :] = x
    y = w[3:4, :] * x
    for k in range(3):
        y = y + w[k:k + 1, :] * _rows_down(x, tail, 3 - k)
    return y


def _conv_bwd(ext, dy, dy_next, w, R):
    x, tail = ext[8:8 + R, :], ext[0:8, :]
    dx = w[3:4, :] * dy
    dws = []
    for k in range(3):
        dx = dx + w[k:k + 1, :] * _rows_up(dy, dy_next, 3 - k)
        dws.append(jnp.sum(dy * _rows_down(x, tail, 3 - k), 0, keepdims=True))
    dws.append(jnp.sum(dy * x, 0, keepdims=True))
    return dx, jnp.concatenate(dws, axis=0)


def _prev_tail_spec(R, W):
    return pl.BlockSpec((8, W), lambda i: (jnp.maximum(i * (R // 8) - 1, 0), 0))


def _prev_tail_spec_rev(R, W, nb):
    return pl.BlockSpec((8, W), lambda i: (jnp.maximum((nb - 1 - i) * (R // 8) - 1, 0), 0))


def _rope_tables(positions):
    T = positions.shape[0]

    def body(p_ref, c_ref, s_ref):
        lane = lax.broadcasted_iota(jnp.int32, (TM, RET_W), 1)
        fi = (lane % 32).astype(f32)
        inv = jnp.exp(fi * (-math.log(ROPE_THETA) / 32.0))
        ang = p_ref[...].astype(f32) * inv
        c_ref[...] = jnp.cos(ang)
        s_ref[...] = jnp.where(lane % CH < 32, -jnp.sin(ang), jnp.sin(ang))

    row = pl.BlockSpec((TM, RET_W), lambda i: (i, 0))
    return pl.pallas_call(
        body, grid=(T // TM,), in_specs=[pl.BlockSpec((TM, 1), lambda i: (i, 0))], out_specs=[row, row],
        out_shape=[jax.ShapeDtypeStruct((T, RET_W), f32)] * 2,
        compiler_params=_cparams(("arbitrary",)), name="rope_tables")(positions)


def _partner(x):
    lane = lax.broadcasted_iota(jnp.int32, x.shape, 1)
    return jnp.where(lane % CH < 32, pltpu.roll(x, RET_W - 32, 1), pltpu.roll(x, 32, 1))


def _ret_consts():
    ii = lax.broadcasted_iota(jnp.int32, (CH, CH), 0).astype(f32)
    jj = lax.broadcasted_iota(jnp.int32, (CH, CH), 1).astype(f32)
    intra, cross, tail, cd = [], [], [], []
    for h in range(RET_H):
        lg = math.log1p(-(2.0 ** (-5.0 - h)))
        intra.append(jnp.exp(jnp.abs(ii - jj) * lg))
        cross.append(jnp.exp((ii + 1.0) * lg))
        tail.append(jnp.exp((CH - 1.0 - ii) * lg))
        cd.append(jnp.full((CH, CH), math.exp(CH * lg), f32))
    return jnp.stack(intra), jnp.stack(cross), jnp.stack(tail), jnp.stack(cd)


def _ret_chunk(consts, q, k, v, st):
    intra, cross, tail, cd = consts
    s = _bmm('hid,hjd->hij', q, k) * intra
    o = _bmm('hij,hje->hie', s, v) + _bmm('hid,hde->hie', q * cross, st)
    st2 = st * cd + _bmm('hjd,hje->hde', k * tail, v)
    oc = o - _rowsum(o) * (1.0 / CH)
    on = oc * lax.rsqrt(_rowsum(oc * oc) * (1.0 / CH) + 1e-5)
    return on, st2


def _ret_fwd(hr, cosw, sinw, gam):
    T = hr.shape[0]
    R = RB_RET
    nc = R // CH

    def body(h_ref, c_ref, s_ref, g_ref, o_ref, st_ref, st, wide):
        @pl.when(pl.program_id(0) == 0)
        def _():
            st[...] = jnp.zeros_like(st)

        consts = _ret_consts()
        cw, sw = c_ref[...], s_ref[...]
        q, k = h_ref[:, 0:RET_W], h_ref[:, RET_W:2 * RET_W]
        qh = _split_heads((q * cw + _partner(q) * sw) * 0.125, RET_H)
        kh = _split_heads(k * cw + _partner(k) * sw, RET_H)
        vh = _split_heads(h_ref[:, 2 * RET_W:3 * RET_W], RET_H)
        outs = []
        s_cur = st[...]
        for c in range(nc):
            sl = slice(c * RET_H, (c + 1) * RET_H)
            st_ref[c] = s_cur
            on, s_cur = _ret_chunk(consts, qh[sl], kh[sl], vh[sl], s_cur)
            outs.append(on)
        st[...] = s_cur
        _merge_heads(wide, jnp.concatenate(outs, axis=0), RET_H)
        o_ref[...] = wide[...] * g_ref[...] * _silu(h_ref[:, 3 * RET_W:4 * RET_W])

    blk = pl.BlockSpec((R, RET_W), lambda i: (i, 0))
    return pl.pallas_call(
        body, grid=(T // R,),
        in_specs=[pl.BlockSpec((R, D), lambda i: (i, 0)), blk, blk, _full_spec((1, RET_W))],
        out_specs=[blk, pl.BlockSpec((nc, RET_H, CH, CH), lambda i: (i, 0, 0, 0))],
        out_shape=[jax.ShapeDtypeStruct((T, RET_W), f32), jax.ShapeDtypeStruct((T // CH, RET_H, CH, CH), f32)],
        scratch_shapes=[pltpu.VMEM((RET_H, CH, CH), f32), pltpu.VMEM((R, RET_W), f32)],
        compiler_params=_cparams(("arbitrary",)), name="ret_fwd")(hr, cosw, sinw, gam)


def _ret_bwd(hr, cosw, sinw, gam, states, dout):
    T = hr.shape[0]
    R = RB_RET
    nc = R // CH
    nb = T // R

    def body(h_ref, c_ref, s_ref, g_ref, st_ref, do_ref, dh_ref, dgam_ref, dst, wide):
        @pl.when(pl.program_id(0) == 0)
        def _():
            dst[...] = jnp.zeros_like(dst)
            dgam_ref[...] = jnp.zeros_like(dgam_ref)

        consts = _ret_consts()
        cw, sw = c_ref[...], s_ref[...]
        q, k = h_ref[:, 0:RET_W], h_ref[:, RET_W:2 * RET_W]
        gr = h_ref[:, 3 * RET_W:4 * RET_W]
        qh = _split_heads((q * cw + _partner(q) * sw) * 0.125, RET_H)
        kh = _split_heads(k * cw + _partner(k) * sw, RET_H)
        vh = _split_heads(h_ref[:, 2 * RET_W:3 * RET_W], RET_H)
        do = do_ref[...]
        gam = g_ref[...]
        sg = _silu(gr)
        don = _split_heads(do * gam * sg, RET_H)
        ons, dqs, dks, dvs = [None] * nc, [None] * nc, [None] * nc, [None] * nc
        ds = dst[...]
        for c in reversed(range(nc)):
            sl = slice(c * RET_H, (c + 1) * RET_H)
            (on, _), vjp = jax.vjp(functools.partial(_ret_chunk, consts), qh[sl], kh[sl], vh[sl], st_ref[c])
            dqs[c], dks[c], dvs[c], ds = vjp((don[sl], ds))
            ons[c] = on
        dst[...] = ds
        _merge_heads(wide, jnp.concatenate(ons, axis=0), RET_H)
        onw = wide[...]
        dgam_ref[...] += jnp.sum(do * onw * sg, 0, keepdims=True)
        dh_ref[:, 3 * RET_W:4 * RET_W] = do * onw * gam * _dsilu(gr)
        _merge_heads(wide, jnp.concatenate(dqs, axis=0), RET_H)
        u = wide[...] * 0.125
        dh_ref[:, 0:RET_W] = u * cw + _partner(u * sw)
        _merge_heads(wide, jnp.concatenate(dks, axis=0), RET_H)
        u = wide[...]
        dh_ref[:, RET_W:2 * RET_W] = u * cw + _partner(u * sw)
        _merge_heads(dh_ref, jnp.concatenate(dvs, axis=0), RET_H, col0=2 * RET_W)

    blk = pl.BlockSpec((R, RET_W), lambda i: (nb - 1 - i, 0))
    return pl.pallas_call(
        body, grid=(nb,),
        in_specs=[pl.BlockSpec((R, D), lambda i: (nb - 1 - i, 0)), blk, blk, _full_spec((1, RET_W)),
                  pl.BlockSpec((nc, RET_H, CH, CH), lambda i: (nb - 1 - i, 0, 0, 0)), blk],
        out_specs=[pl.BlockSpec((R, D), lambda i: (nb - 1 - i, 0)), _full_spec((1, RET_W))],
        out_shape=[jax.ShapeDtypeStruct((T, D), f32), jax.ShapeDtypeStruct((1, RET_W), f32)],
        scratch_shapes=[pltpu.VMEM((RET_H, CH, CH), f32), pltpu.VMEM((R, RET_W), f32)],
        compiler_params=_cparams(("arbitrary",)), name="ret_bwd")(hr, cosw, sinw, gam, states, dout)


def _lru_ab(xc, wa, ba, wx, bx, lam):
    r = _sigmoid(_dot(xc, wa) + ba)
    i = _sigmoid(_dot(xc, wx) + bx)
    la = 8.0 * r * (-_softplus(-lam))
    a = jnp.exp(la)
    em = jnp.tanh(la) * (jnp.exp(2.0 * la) + 1.0)
    return a, jnp.sqrt(-em) * (i * xc)


def _lru_out(h, gate):
    return h * _gelu(gate)


def _scan_fwd(a, b):
    R = a.shape[0]
    row = lax.broadcasted_iota(jnp.int32, a.shape, 0)
    d = 1
    while d < R:
        m = row >= d
        b = jnp.where(m, a * pltpu.roll(b, d, 0) + b, b)
        a = jnp.where(m, a * pltpu.roll(a, d, 0), a)
        d *= 2
    return a, b


def _scan_bwd(a, b):
    R = a.shape[0]
    row = lax.broadcasted_iota(jnp.int32, a.shape, 0)
    d = 1
    while d < R:
        m = row < R - d
        b = jnp.where(m, a * pltpu.roll(b, R - d, 0) + b, b)
        a = jnp.where(m, a * pltpu.roll(a, R - d, 0), a)
        d *= 2
    return b


def _lru_fwd(hl, cw, cb, wa, ba, wx, bx, lam):
    T = hl.shape[0]
    R = RB_LRU
    W = LRU_W

    def body(h_ref, t_ref, cw_ref, cb_ref, wa_ref, ba_ref, wx_ref, bx_ref, lam_ref, o_ref, hs_ref, carry, ext):
        first = pl.program_id(0) == 0

        @pl.when(first)
        def _():
            carry[...] = jnp.zeros_like(carry)

        tail = jnp.where(first, 0.0, t_ref[:, 0:W])
        xc = _conv_fwd(ext, h_ref[:, 0:W], tail, cw_ref[...], R) + cb_ref[...]
        a, b = _lru_ab(xc, wa_ref[...], ba_ref[...], wx_ref[...], bx_ref[...], lam_ref[...])
        ap, hloc = _scan_fwd(a, b)
        h = hloc + ap * carry[0:1, :]
        carry[...] = jnp.broadcast_to(h[R - 1:R, :], carry.shape)
        hs_ref[...] = h
        o_ref[...] = _lru_out(h, h_ref[:, W:2 * W])

    vec = _full_spec((1, W))
    blk = pl.BlockSpec((R, W), lambda i: (i, 0))
    return pl.pallas_call(
        body, grid=(T // R,),
        in_specs=[pl.BlockSpec((R, 2 * W), lambda i: (i, 0)), _prev_tail_spec(R, 2 * W), _full_spec((4, W)), vec,
                  _full_spec((W, W)), vec, _full_spec((W, W)), vec, vec],
        out_specs=[blk, blk], out_shape=[jax.ShapeDtypeStruct((T, W), f32)] * 2,
        scratch_shapes=[pltpu.VMEM((8, W), f32), pltpu.VMEM((R + 8, W), f32)],
        compiler_params=_cparams(("arbitrary",)), name="lru_fwd")(hl, hl, cw, cb, wa, ba, wx, bx, lam)


def _lru_bwd(hl, hs, cw, cb, wa, ba, wx, bx, lam, dout):
    T = hl.shape[0]
    R = RB_LRU
    W = LRU_W
    nb = T // R

    def body(h_ref, t_ref, hs_ref, hst_ref, cw_ref, cb_ref, wa_ref, ba_ref, wx_ref, bx_ref, lam_ref, do_ref,
             dh_ref, dcw_ref, dcb_ref, dwa_ref, dba_ref, dwx_ref, dbx_ref, dlam_ref, carry_g, carry_dy, ext):
        i = pl.program_id(0)
        last_blk = i == 0
        first_blk = i == nb - 1

        @pl.when(last_blk)
        def _():
            carry_g[...] = jnp.zeros_like(carry_g)
            carry_dy[...] = jnp.zeros_like(carry_dy)
            for r in (dcw_ref, dcb_ref, dwa_ref, dba_ref, dwx_ref, dbx_ref, dlam_ref):
                r[...] = jnp.zeros_like(r)

        tail = jnp.where(first_blk, 0.0, t_ref[:, 0:W])
        xc = _conv_fwd(ext, h_ref[:, 0:W], tail, cw_ref[...], R) + cb_ref[...]
        (a, _), vjp_ab = jax.vjp(_lru_ab, xc, wa_ref[...], ba_ref[...], wx_ref[...], bx_ref[...], lam_ref[...])
        hs = hs_ref[...]
        _, vjp_out = jax.vjp(_lru_out, hs, h_ref[:, W:2 * W])
        dh, dgate = vjp_out(do_ref[...])
        row = lax.broadcasted_iota(jnp.int32, (R, W), 0)
        dh = jnp.where(row == R - 1, dh + carry_g[0:1, :], dh)
        a_up = jnp.where(row == R - 1, 0.0, pltpu.roll(a, R - 1, 0))
        g = _scan_bwd(a_up, dh)
        carry_g[...] = jnp.broadcast_to(a[0:1, :] * g[0:1, :], carry_g.shape)
        hprev0 = jnp.where(first_blk, 0.0, hst_ref[7:8, :])
        hprev = jnp.where(row == 0, hprev0, pltpu.roll(hs, 1, 0))
        dxc, dwa, dba, dwx, dbx, dlam = vjp_ab((g * hprev, g))
        dwa_ref[...] += dwa
        dba_ref[...] += dba
        dwx_ref[...] += dwx
        dbx_ref[...] += dbx
        dlam_ref[...] += dlam
        dcb_ref[...] += jnp.sum(dxc, 0, keepdims=True)
        dx, dcw = _conv_bwd(ext, dxc, carry_dy[...], cw_ref[...], R)
        carry_dy[...] = dxc[0:8, :]
        dcw_ref[...] += dcw
        dh_ref[:, 0:W] = dx
        dh_ref[:, W:2 * W] = dgate

    vec = _full_spec((1, W))
    mat = _full_spec((W, W))
    blk = pl.BlockSpec((R, W), lambda i: (nb - 1 - i, 0))
    blk2 = pl.BlockSpec((R, 2 * W), lambda i: (nb - 1 - i, 0))
    return pl.pallas_call(
        body, grid=(nb,),
        in_specs=[blk2, _prev_tail_spec_rev(R, 2 * W, nb), blk, _prev_tail_spec_rev(R, W, nb), _full_spec((4, W)), vec,
                  mat, vec, mat, vec, vec, blk],
        out_specs=[blk2, _full_spec((4, W)), vec, mat, vec, mat, vec, vec],
        out_shape=[jax.ShapeDtypeStruct((T, 2 * W), f32), jax.ShapeDtypeStruct((4, W), f32),
                   jax.ShapeDtypeStruct((1, W), f32), jax.ShapeDtypeStruct((W, W), f32),
                   jax.ShapeDtypeStruct((1, W), f32), jax.ShapeDtypeStruct((W, W), f32),
                   jax.ShapeDtypeStruct((1, W), f32), jax.ShapeDtypeStruct((1, W), f32)],
        scratch_shapes=[pltpu.VMEM((8, W), f32), pltpu.VMEM((8, W), f32), pltpu.VMEM((R + 8, W), f32)],
        compiler_params=_cparams(("arbitrary",)), name="lru_bwd")(hl, hl, hs, hs, cw, cb, wa, ba, wx, bx, lam, dout)


def _head_ones():
    i = lax.broadcasted_iota(jnp.int32, (GDN_W, GDN_W), 0)
    j = lax.broadcasted_iota(jnp.int32, (GDN_W, GDN_W), 1)
    return jnp.where(jnp.bitwise_xor(i, j) < CH, 1.0, 0.0).astype(bf16)


def _head_sums(x, ones):
    x1, x2, _ = _split3(x)
    return jnp.dot(x1, ones, preferred_element_type=f32) + jnp.dot(x2, ones, preferred_element_type=f32)


def _l2n(y, ones):
    r = lax.rsqrt(_head_sums(y * y, ones) + 1e-6)
    return y * r, r


def _l2n_bwd(dn, n, r, ones):
    return r * (dn - n * _head_sums(dn * n, ones))


def _gdn_local(inverse, q, k, vs, gc, bb):
    B = q.shape[0]
    ii = lax.broadcasted_iota(jnp.int32, (B, CH, CH), 1)
    jj = lax.broadcasted_iota(jnp.int32, (B, CH, CH), 2)
    gct = jnp.swapaxes(gc, 1, 2)
    decay = jnp.where(ii >= jj, jnp.exp(jnp.minimum(gc - gct, 0.0)), 0.0)
    kk = _bmm('bid,bjd->bij', k, k)
    inv = inverse(-jnp.where(ii > jj, bb * kk * decay, 0.0))
    egc = jnp.exp(gc)
    u = _bmm3('bij,bje->bie', inv, vs * bb)
    w = _bmm3('bij,bje->bie', inv, k * (bb * egc))
    qk = _bmm('bid,bjd->bij', q, k) * (0.125 * decay)
    glast = gc[:, CH - 1:CH, :]
    return u, w, qk, q * (0.125 * egc), k * jnp.exp(glast - gc), jnp.exp(jnp.broadcast_to(glast, gc.shape))


def _gdn_step(st, u, w, qk, qd, kt, egl, z, gn):
    vnew = u - _bmm('hcd,hde->hce', w, st)
    o = _bmm('hcd,hde->hce', qd, st) + _bmm('hij,hje->hie', qk, vnew)
    st2 = st * egl + _bmm('hcd,hce->hde', kt, vnew)
    out = o * lax.rsqrt(_rowsum(o * o) * (1.0 / CH) + 1e-6) * gn * _silu(z)
    return out, st2


def _gdn_scalars(ab, alog, dtb):
    sp = _softplus(ab + dtb)
    return -jnp.exp(alog) * sp, _sigmoid(ab)


def _bcast_heads(blk, lane0, H):
    R = blk.shape[0]
    n = R // CH
    parts = [jnp.broadcast_to(blk[:, lane0 + h:lane0 + h + 1], (R, CH)).reshape(n, CH, CH) for h in range(H)]
    return jnp.stack(parts, axis=1).reshape(n * H, CH, CH)


def _unbcast_heads(x, lane0, H):
    n = x.shape[0] // H
    R = n * CH
    s = jnp.sum(x, axis=2, keepdims=True).reshape(n, H, CH, 1)
    lane = lax.broadcasted_iota(jnp.int32, (R, 128), 1)
    acc = jnp.zeros((R, 128), f32)
    for h in range(H):
        acc = acc + jnp.where(lane == lane0 + h, jnp.broadcast_to(s[:, h].reshape(R, 1), (R, 128)), 0.0)
    return acc


def _gdn_fwd(hg, cw, alog, dtb, gn):
    T = hg.shape[0]
    R = RB_GDN
    nc = R // CH
    W3 = 3 * GDN_W
    H = GDN_H

    def body(h_ref, t_ref, cw_ref, al_ref, dt_ref, gn_ref, o_ref, st_ref, inv_ref, st, ext):
        first = pl.program_id(0) == 0

        @pl.when(first)
        def _():
            st[...] = jnp.zeros_like(st)

        def inverse(m):
            inv = _neumann_inv(m)
            inv_ref[...] = inv
            return inv

        tail = jnp.where(first, 0.0, t_ref[:, 0:W3])
        y = _silu(_conv_fwd(ext, h_ref[:, 0:W3], tail, cw_ref[...], R))
        ones = _head_ones()
        qs = _split_heads(_l2n(y[:, 0:GDN_W], ones)[0], H)
        ks = _split_heads(_l2n(y[:, GDN_W:2 * GDN_W], ones)[0], H)
        vs = _split_heads(y[:, 2 * GDN_W:W3], H)
        zh = _split_heads(h_ref[:, W3:W3 + GDN_W], H)
        g, beta = _gdn_scalars(h_ref[:, W3 + GDN_W:GDN_IN], al_ref[...], dt_ref[...])
        loc = _gdn_local(inverse, qs, ks, vs, _bcast_heads(_chunk_cumsum(g), 0, H), _bcast_heads(beta, H, H))
        gnv = gn_ref[...]
        outs = []
        s_cur = st[...]
        for c in range(nc):
            sl = slice(c * H, (c + 1) * H)
            st_ref[c] = s_cur
            out, s_cur = _gdn_step(s_cur, *(t[sl] for t in loc), zh[sl], gnv)
            outs.append(out)
        st[...] = s_cur
        _merge_heads(o_ref, jnp.concatenate(outs, axis=0), H)

    return pl.pallas_call(
        body, grid=(T // R,),
        in_specs=[pl.BlockSpec((R, GDN_IN), lambda i: (i, 0)), _prev_tail_spec(R, GDN_IN), _full_spec((4, W3)),
                  _full_spec((1, 128)), _full_spec((1, 128)), _full_spec((1, CH))],
        out_specs=[pl.BlockSpec((R, GDN_W), lambda i: (i, 0)), pl.BlockSpec((nc, H, CH, CH), lambda i: (i, 0, 0, 0)),
                   pl.BlockSpec((nc * H, CH, CH), lambda i: (i, 0, 0))],
        out_shape=[jax.ShapeDtypeStruct((T, GDN_W), f32), jax.ShapeDtypeStruct((T // CH, H, CH, CH), f32),
                   jax.ShapeDtypeStruct((T // CH * H, CH, CH), f32)],
        scratch_shapes=[pltpu.VMEM((H, CH, CH), f32), pltpu.VMEM((R + 8, W3), f32)],
        compiler_params=_cparams(("arbitrary",)), name="gdn_fwd")(hg, hg, cw, alog, dtb, gn)


def _gdn_bwd(hg, cw, alog, dtb, gn, states, invs, dout):
    T = hg.shape[0]
    R = RB_GDN
    nc = R // CH
    nb = T // R
    W3 = 3 * GDN_W
    H = GDN_H

    def body(h_ref, t_ref, cw_ref, al_ref, dt_ref, gn_ref, st_ref, inv_ref, do_ref,
             dh_ref, dcw_ref, dal_ref, ddt_ref, dgn_ref, dst, carry_dy, ext, wide):
        i = pl.program_id(0)
        first_blk = i == nb - 1

        @pl.when(i == 0)
        def _():
            dst[...] = jnp.zeros_like(dst)
            carry_dy[...] = jnp.zeros_like(carry_dy)
            for r in (dcw_ref, dal_ref, ddt_ref, dgn_ref):
                r[...] = jnp.zeros_like(r)

        tail = jnp.where(first_blk, 0.0, t_ref[:, 0:W3])
        ypre = _conv_fwd(ext, h_ref[:, 0:W3], tail, cw_ref[...], R)
        y = _silu(ypre)
        ones = _head_ones()
        qn, rq = _l2n(y[:, 0:GDN_W], ones)
        kn, rk = _l2n(y[:, GDN_W:2 * GDN_W], ones)
        qs, ks, vs = _split_heads(qn, H), _split_heads(kn, H), _split_heads(y[:, 2 * GDN_W:W3], H)
        zh = _split_heads(h_ref[:, W3:W3 + GDN_W], H)
        ab = h_ref[:, W3 + GDN_W:GDN_IN]
        alog, dtb = al_ref[...], dt_ref[...]
        g, beta = _gdn_scalars(ab, alog, dtb)
        kept = inv_ref[...]
        loc, vjp_loc = jax.vjp(functools.partial(_gdn_local, lambda m: _known_inv(m, kept)), qs, ks, vs,
                               _bcast_heads(_chunk_cumsum(g), 0, H), _bcast_heads(beta, H, H))
        doh = _split_heads(do_ref[...], H)
        gnv = gn_ref[...]
        dloc = [[None] * nc for _ in range(6)]
        dzs = [None] * nc
        ds = dst[...]
        dgn = jnp.zeros((1, CH), f32)
        for c in reversed(range(nc)):
            sl = slice(c * H, (c + 1) * H)
            _, vjp = jax.vjp(_gdn_step, st_ref[c], *(t[sl] for t in loc), zh[sl], gnv)
            grads = vjp((doh[sl], ds))
            ds = grads[0]
            for j in range(6):
                dloc[j][c] = grads[1 + j]
            dzs[c] = grads[7]
            dgn = dgn + grads[8]
        dst[...] = ds
        dgn_ref[...] += dgn
        dqs, dks, dvs, dgb, dbb = vjp_loc(tuple(jnp.concatenate(d, axis=0) for d in dloc))
        lane = lax.broadcasted_iota(jnp.int32, (R, 128), 1)
        dg = _chunk_cumsum(_unbcast_heads(dgb, 0, H), reverse=True)
        dbeta = _unbcast_heads(dbb, H, H)
        da = dg * (-jnp.exp(alog)) * _sigmoid(ab + dtb)
        dh_ref[:, W3 + GDN_W:GDN_IN] = jnp.where(lane < H, da, dbeta * beta * (1.0 - beta))
        ddt_ref[...] += jnp.sum(jnp.where(lane < H, da, 0.0), 0, keepdims=True)
        dal_ref[...] += jnp.sum(jnp.where(lane < H, dg * g, 0.0), 0, keepdims=True)
        _merge_heads(dh_ref, jnp.concatenate(dzs, axis=0), H, col0=W3)
        for j, dpart in enumerate((dqs, dks, dvs)):
            _merge_heads(wide, dpart, H, col0=j * GDN_W)
        wide[:, 0:GDN_W] = _l2n_bwd(wide[:, 0:GDN_W], qn, rq, ones)
        wide[:, GDN_W:2 * GDN_W] = _l2n_bwd(wide[:, GDN_W:2 * GDN_W], kn, rk, ones)
        dy = wide[...] * _dsilu(ypre)
        dx, dcw = _conv_bwd(ext, dy, carry_dy[...], cw_ref[...], R)
        carry_dy[...] = dy[0:8, :]
        dcw_ref[...] += dcw
        dh_ref[:, 0:W3] = dx

    blk = pl.BlockSpec((R, GDN_IN), lambda i: (nb - 1 - i, 0))
    return pl.pallas_call(
        body, grid=(nb,),
        in_specs=[blk, _prev_tail_spec_rev(R, GDN_IN, nb), _full_spec((4, W3)), _full_spec((1, 128)),
                  _full_spec((1, 128)), _full_spec((1, CH)),
                  pl.BlockSpec((nc, H, CH, CH), lambda i: (nb - 1 - i, 0, 0, 0)),
                  pl.BlockSpec((nc * H, CH, CH), lambda i: (nb - 1 - i, 0, 0)),
                  pl.BlockSpec((R, GDN_W), lambda i: (nb - 1 - i, 0))],
        out_specs=[blk, _full_spec((4, W3)), _full_spec((1, 128)), _full_spec((1, 128)), _full_spec((1, CH))],
        out_shape=[jax.ShapeDtypeStruct((T, GDN_IN), f32), jax.ShapeDtypeStruct((4, W3), f32),
                   jax.ShapeDtypeStruct((1, 128), f32), jax.ShapeDtypeStruct((1, 128), f32),
                   jax.ShapeDtypeStruct((1, CH), f32)],
        scratch_shapes=[pltpu.VMEM((H, CH, CH), f32), pltpu.VMEM((8, W3), f32), pltpu.VMEM((R + 8, W3), f32),
                        pltpu.VMEM((R, W3), f32)],
        compiler_params=_cparams(("arbitrary",)), name="gdn_bwd")(hg, hg, cw, alog, dtb, gn, states, invs, dout)


def _block_diag(w):
    out = jnp.zeros((LRU_W, LRU_W), w.dtype)
    for g in range(w.shape[0]):
        out = lax.dynamic_update_slice(out, w[g], (g * CH, g * CH))
    return out


def _block_diag_t(w):
    return jnp.stack([w[g * CH:(g + 1) * CH, g * CH:(g + 1) * CH] for g in range(LRU_W // CH)])


def _pad_lanes(v, n=128):
    return jnp.pad(v, (0, n - v.shape[0]))[None, :]


def _local_step(x, p, positions, target, fetch, emit, sm):
    cosw, sinw = _rope_tables(positions)
    cols = lambda w: jnp.transpose(w, (1, 2, 0, 3)).reshape(-1, D, NDEV * FSP)
    saved = []
    h = x
    for l in range(DEPTH):
        v = lambda n: sm[n][l][None, :]
        F1, tok = fetch(l, 'f1', h)
        p384a, pda = cols(F1['p384']), F1['pd']
        z1, x1, g1, u1 = _ffn_fwd(h, p384a, pda, v('ln_ffn1_g') + tok, v('ln_ffn1_b'), 0, 0)
        G, _ = fetch(l, 'rest', x1)
        p384, pd, pr, pinl, ping, wpp = cols(G['p384']), G['pd'], G['pr'], G['pinl'], G['ping'], G['wpp']
        wts = (p384a, pda, p384, pd, pr, pinl, ping, wpp)
        hr, hl, hg = _proj_in(x1, pr, pinl, ping, 0)
        o_r, rst = _ret_fwd(hr, cosw, sinw, v('ret_norm_g'))
        lru_args = (sm['lru_conv_w'][l], v('lru_conv_b'), _block_diag(sm['lru_w_a'][l]), v('lru_b_a'),
                    _block_diag(sm['lru_w_x'][l]), v('lru_b_x'), v('lru_lambda'))
        o_l, hs = _lru_fwd(hl, *lru_args)
        gdn_args = (sm['gdn_conv_w'][l], _pad_lanes(sm['gdn_a_log'][l]), _pad_lanes(sm['gdn_dt_bias'][l]),
                    v('gdn_norm_g'))
        o_g, *gst = _gdn_fwd(hg, *gdn_args)
        z2, x2 = _mix_out(x1, o_r, o_l, o_g, pr, v('ln_mix_g'), v('ln_mix_b'), 0)
        z3, x3, g2, u2 = _ffn_fwd(x2, p384, pd, v('ln_ffn2_g'), v('ln_ffn2_b'), 0, 1, ple=(p[l], pr, wpp))
        saved.append((h, z1, x1, hr, hl, hg, o_r, rst, o_l, hs, lru_args, o_g, gst, gdn_args, z2, x2, z3,
                      g1, u1, g2, u2, wts))
        h = x3
    d, loss = _loss_grad(h, target)

    small = {n: [None] * DEPTH for n in SMALL}
    tok = 0.0
    for l in reversed(range(DEPTH)):
        (x0, z1, x1, hr, hl, hg, o_r, rst, o_l, hs, lru_args, o_g, gst, gdn_args, z2, x2, z3,
         g1, u1, g2, u2, wts) = saved[l]
        p384a, pda, p384, pd, pr, pinl, ping, wpp = wts
        v = lambda n: sm[n][l][None, :]
        rows = lambda m: m.reshape(NDEV, m.shape[1] // NDEV, m.shape[2])
        d2, dg2, du2, a2, dy2, small['ln_ffn2_g'][l], small['ln_ffn2_b'][l] = _ffn_bwd(
            z3, d, g2, u2, p384, pd, v('ln_ffn2_g') + tok, 0, 1)
        d2, dgp, dpj = _ple_bwd(x2, p[l], dy2, d2, pr, wpp, 0)
        dxb, dzb, do_r, do_l, do_g, small['ln_mix_g'][l], small['ln_mix_b'][l] = _mix_out_bwd(
            z2, d2, pr, v('ln_mix_g'), 0)
        dhr, small['ret_norm_g'][l] = _ret_bwd(hr, cosw, sinw, v('ret_norm_g'), rst, do_r)
        (dhl, small['lru_conv_w'][l], small['lru_conv_b'][l], dwa, small['lru_b_a'][l], dwx, small['lru_b_x'][l],
         small['lru_lambda'][l]) = _lru_bwd(hl, hs, *lru_args, do_l)
        small['lru_w_a'][l], small['lru_w_x'][l] = _block_diag_t(dwa), _block_diag_t(dwx)
        dhg, small['gdn_conv_w'][l], dal, ddt, small['gdn_norm_g'][l] = _gdn_bwd(hg, *gdn_args, *gst, do_g)
        small['gdn_a_log'][l], small['gdn_dt_bias'][l] = dal[:, 0:GDN_H], ddt[:, 0:GDN_H]
        d1 = _proj_in_bwd(dxb, dhr, dhl, dhg, pr, pinl, ping, 0)
        dwo = jnp.concatenate([_matmul_tn(o_r, dzb, D, "dw_out_r"), _matmul_tn(o_l, dzb, D, "dw_out_l"),
                               _matmul_tn(o_g, dzb, D, "dw_out_g")], axis=1)
        tok = emit(l, 'rest', {
            'p384': jnp.stack([_matmul_tn(x2, dg2, FSP, "dw_gate", NDEV), _matmul_tn(x2, du2, FSP, "dw_up", NDEV)],
                              axis=1),
            'pd': rows(_matmul_tn(a2, dy2, D, "dw_down"))[:, None],
            'pr': jnp.stack([rows(_matmul_tn(x1, dhr, D, "dw_in_r")), rows(dwo),
                             rows(_matmul_tn(x2, dgp, D, "dw_ple_gate"))], axis=1),
            'pinl': rows(_matmul_tn(x1, dhl, 2 * LRU_W, "dw_in_l"))[:, None],
            'ping': rows(_matmul_tn(x1, dhg, GDN_IN, "dw_in_g"))[:, None],
            'ppp': jnp.transpose(_matmul_tn(p[l], dpj, D, "dw_ple_proj").reshape(PLE, NDEV, 128), (1, 0, 2))[:, None]})
        d, dg1, du1, a1, dy1, small['ln_ffn1_g'][l], small['ln_ffn1_b'][l] = _ffn_bwd(
            z1, d1, g1, u1, p384a, pda, v('ln_ffn1_g') + tok, 0, 0)
        tok = emit(l, 'f1', {
            'p384': jnp.stack([_matmul_tn(x0, dg1, FSP, "dw_gate", NDEV), _matmul_tn(x0, du1, FSP, "dw_up", NDEV)],
                              axis=1),
            'pd': rows(_matmul_tn(a1, dy1, D, "dw_down"))[:, None]})
    small = {n: jnp.stack([g.reshape(sm[n].shape[1:]) for g in gs]) for n, gs in small.items()}
    return loss, d, small


def _pack_big(ws, dtype=bf16):
    padc = lambda a, n: jnp.pad(a, ((0, 0), (0, 0), (0, n - a.shape[2])))
    padr = lambda a, n: jnp.pad(a, ((0, 0), (0, n - a.shape[1]), (0, 0)))
    per_layer = lambda arrs: jnp.stack(arrs, axis=1).reshape((-1,) + arrs[0].shape[1:])
    w_in = ws['w_in']
    out = {
        'p384': per_layer([padc(ws[n], FSP) for n in ('ffn1_w_gate', 'ffn1_w_up', 'ffn2_w_gate', 'ffn2_w_up')]),
        'pd': per_layer([padr(ws[n], FSP) for n in ('ffn1_w_down', 'ffn2_w_down')]),
        'pr': per_layer([w_in[:, :, 0:D], ws['w_out'], ws['ple_w_gate']]),
        'pinl': w_in[:, :, D:D + 2 * LRU_W],
        'ping': padc(w_in[:, :, D + 2 * LRU_W:D_IN], GDN_IN),
        'ppp': ws['ple_w_proj'],
    }
    return {k: a.astype(dtype) for k, a in out.items()}


def _exchange(arrays, scatter, name):
    n = len(arrays)

    def body(*refs):
        ins, outs = refs[:n], refs[n:2 * n]
        send_sems, recv_sems, local_sems = refs[2 * n:]
        x, y, c = lax.axis_index("x"), lax.axis_index("y"), lax.axis_index("c")
        me = 4 * x + 2 * y + c
        copies = []
        for i in range(n):
            src = ins[i].at[me] if scatter[i] else ins[i]
            cp = pltpu.make_async_copy(src, outs[i].at[me], local_sems.at[i])
            cp.start()
            copies.append(cp)
        sends = []
        for j in range(1, NDEV):
            peer = (me + j) % NDEV
            pid = (peer // 4, (peer // 2) % 2, peer % 2)
            for i in range(n):
                src = ins[i].at[peer] if scatter[i] else ins[i]
                cp = pltpu.make_async_remote_copy(
                    src_ref=src, dst_ref=outs[i].at[me], send_sem=send_sems.at[i, j - 1],
                    recv_sem=recv_sems.at[i, j - 1], device_id=pid, device_id_type=pl.DeviceIdType.MESH)
                cp.start()
                sends.append(cp)
        for j in range(1, NDEV):
            source = (me + NDEV - j) % NDEV
            sid = (source // 4, (source // 2) % 2, source % 2)
            for i in range(n):
                src = ins[i].at[me] if scatter[i] else ins[i]
                pltpu.make_async_remote_copy(
                    src_ref=src, dst_ref=outs[i].at[source], send_sem=send_sems.at[i, j - 1],
                    recv_sem=recv_sems.at[i, j - 1], device_id=sid, device_id_type=pl.DeviceIdType.MESH).wait_recv()
        for cp in sends:
            cp.wait_send()
        for cp in copies:
            cp.wait()

    hbm = pl.BlockSpec(memory_space=pltpu.HBM)
    out_shape = [jax.ShapeDtypeStruct(a.shape if s else (NDEV,) + a.shape, a.dtype) for a, s in zip(arrays, scatter)]
    return pl.pallas_call(
        body, in_specs=[hbm] * n, out_specs=[hbm] * n, out_shape=out_shape,
        scratch_shapes=[pltpu.SemaphoreType.DMA((n, NDEV - 1)), pltpu.SemaphoreType.DMA((n, NDEV - 1)),
                        pltpu.SemaphoreType.DMA((n,))],
        compiler_params=pltpu.CompilerParams(has_side_effects=True), name=name)(*arrays)


def _gather_two_level(arrays, name):
    n = len(arrays)

    def body(*refs):
        ins, outs = refs[:n], refs[n:2 * n]
        send_sems, recv_sems, local_sems = refs[2 * n:]
        x, y, c = lax.axis_index("x"), lax.axis_index("y"), lax.axis_index("c")
        me, sibling = (x, y, c), (x, y, 1 - c)
        chips = [(1 - x, y), (x, 1 - y), (1 - x, 1 - y)]
        slot = lambda d: 4 * d[0] + 2 * d[1] + d[2]

        def copy(i, k, block, to, src=None):
            return pltpu.make_async_remote_copy(
                src_ref=outs[i].at[slot(block)] if src is None else src, dst_ref=outs[i].at[slot(block)],
                send_sem=send_sems.at[i, k], recv_sem=recv_sems.at[i, k], device_id=to,
                device_id_type=pl.DeviceIdType.MESH)

        mine, first, passed = [], [], []
        for i in range(n):
            cp = pltpu.make_async_copy(ins[i], outs[i].at[slot(me)], local_sems.at[i])
            cp.start()
            mine.append(cp)
            first.append(copy(i, 0, me, sibling, src=ins[i]))
            first += [copy(i, 1 + j, me, (*chip, c), src=ins[i]) for j, chip in enumerate(chips)]
        for cp in first:
            cp.start()
        for i in range(n):
            for j, chip in enumerate(chips):
                copy(i, 1 + j, (*chip, c), me).wait_recv()
                cp = copy(i, 4 + j, (*chip, c), sibling)
                cp.start()
                passed.append(cp)
        for i in range(n):
            copy(i, 0, sibling, me).wait_recv()
            for j, chip in enumerate(chips):
                copy(i, 4 + j, (*chip, 1 - c), me).wait_recv()
        for cp in first + passed:
            cp.wait_send()
        for cp in mine:
            cp.wait()

    hbm = pl.BlockSpec(memory_space=pltpu.HBM)
    return pl.pallas_call(
        body, in_specs=[hbm] * n, out_specs=[hbm] * n,
        out_shape=[jax.ShapeDtypeStruct((NDEV,) + a.shape, a.dtype) for a in arrays],
        scratch_shapes=[pltpu.SemaphoreType.DMA((n, NDEV - 1)), pltpu.SemaphoreType.DMA((n, NDEV - 1)),
                        pltpu.SemaphoreType.DMA((n,))],
        compiler_params=pltpu.CompilerParams(has_side_effects=True), name=name)(*arrays)


def _scatter_pairs(arrays, name):
    n = len(arrays)

    def body(*refs):
        ins, gots = refs[:n], refs[n:2 * n]
        send_sems, recv_sems = refs[2 * n:]
        x, y, c = lax.axis_index("x"), lax.axis_index("y"), lax.axis_index("c")
        sends = []
        for i in range(n):
            for q in range(4):
                cp = pltpu.make_async_remote_copy(
                    src_ref=ins[i].at[2 * q + 1 - c], dst_ref=gots[i].at[q], send_sem=send_sems.at[i, q],
                    recv_sem=recv_sems.at[i, q], device_id=(x, y, 1 - c), device_id_type=pl.DeviceIdType.MESH)
                cp.start()
                sends.append(cp)
        for cp in sends:
            cp.wait_recv()
        for cp in sends:
            cp.wait_send()

    hbm = pl.BlockSpec(memory_space=pltpu.HBM)
    return pl.pallas_call(
        body, in_specs=[hbm] * n, out_specs=[hbm] * n,
        out_shape=[jax.ShapeDtypeStruct((4,) + a.shape[1:], a.dtype) for a in arrays],
        scratch_shapes=[pltpu.SemaphoreType.DMA((n, 4)), pltpu.SemaphoreType.DMA((n, 4))],
        compiler_params=pltpu.CompilerParams(has_side_effects=True), name=name)(*arrays)


def _pair_sum(own, got, name):
    def body(a_ref, b_ref, o_ref):
        o_ref[...] = (a_ref[...].astype(f32) + b_ref[...].astype(f32)).astype(bf16)

    spec = pl.BlockSpec((None, None) + own.shape[2:], lambda q, s: (q, s, 0, 0))
    return pl.pallas_call(
        body, grid=own.shape[:2], in_specs=[spec, spec], out_specs=spec,
        out_shape=jax.ShapeDtypeStruct(own.shape, bf16),
        compiler_params=_cparams(("arbitrary", "arbitrary")), name=name)(own, got)


def _gather_plan(srcs, lands, x, y, c):
    me = 4 * x + 2 * y + c
    sends, arrivals = [], []
    for j in range(1, NDEV):
        peer, source = (me + j) % NDEV, (me + NDEV - j) % NDEV
        for i in range(len(srcs)):
            k = i * (NDEV - 1) + j - 1
            sends.append((srcs[i], lands[i].at[me], (peer // 4, (peer // 2) % 2, peer % 2), k))
            arrivals.append((srcs[i], lands[i].at[source], (source // 4, (source // 2) % 2, source % 2), k))
    return sends, arrivals


def _chips_plan(srcs, lands, x, y, c):
    chip = 2 * x + y
    sends, arrivals = [], []
    for j in range(1, 4):
        peer, source = (chip + j) % 4, (chip + 4 - j) % 4
        for i in range(len(srcs)):
            k = i * 3 + j - 1
            sends.append((srcs[i].at[peer], lands[i].at[chip], (peer // 2, peer % 2, c), k))
            arrivals.append((srcs[i].at[chip], lands[i].at[source], (source // 2, source % 2, c), k))
    return sends, arrivals


def _remote(entry, send_sems, recv_sems):
    src, dst, dev, k = entry
    return pltpu.make_async_remote_copy(src_ref=src, dst_ref=dst, send_sem=send_sems.at[k], recv_sem=recv_sems.at[k],
                                        device_id=dev, device_id_type=pl.DeviceIdType.MESH)


_HBM = pl.BlockSpec(memory_space=pltpu.HBM)
_SEM = pl.BlockSpec(memory_space=pltpu.SEMAPHORE)


def _split_start(arrays, land_shapes, plan, npeer, name):
    n = len(arrays)

    def body(*refs):
        srcs, lands = refs[:n], refs[n:2 * n]
        send_sems, recv_sems, token = refs[2 * n], refs[2 * n + 1], refs[-1]
        sends, _ = plan(srcs, lands, lax.axis_index("x"), lax.axis_index("y"), lax.axis_index("c"))
        for entry in sends:
            _remote(entry, send_sems, recv_sems).start()
        token[...] = jnp.zeros_like(token)

    lands = [lax.empty(s, a.dtype) for s, a in zip(land_shapes, arrays)]
    thru = [pltpu.HBM(a.shape, a.dtype) for a in arrays + lands]
    out = pl.pallas_call(
        body, name=name, in_specs=[_HBM] * (2 * n),
        out_specs=(_SEM, _SEM, *([_HBM] * (2 * n)), pl.BlockSpec(memory_space=pltpu.VMEM)),
        out_shape=(pltpu.SemaphoreType.DMA((n * npeer,)), pltpu.SemaphoreType.DMA((n * npeer,)), *thru,
                   jax.ShapeDtypeStruct((8, 128), f32)),
        input_output_aliases={i: 2 + i for i in range(2 * n)},
        compiler_params=pltpu.CompilerParams(has_side_effects=pltpu.SideEffectType.DATAFLOW_SIDE_EFFECTING),
    )(*[pltpu.with_memory_space_constraint(a, pltpu.HBM) for a in arrays + lands])
    return out[0], out[1], list(out[2:2 + n]), list(out[2 + n:2 + 2 * n]), out[-1]


def _split_wait(send_sems, recv_sems, srcs, lands, after, plan, name):
    n = len(srcs)

    def body(*refs):
        s_refs, l_refs = refs[:n], refs[n:2 * n]
        ssem, rsem = refs[2 * n], refs[2 * n + 1]
        sends, arrivals = plan(s_refs, l_refs, lax.axis_index("x"), lax.axis_index("y"), lax.axis_index("c"))
        for entry in sends:
            _remote(entry, ssem, rsem).wait_send()
        for entry in arrivals:
            _remote(entry, ssem, rsem).wait_recv()

    out = pl.pallas_call(
        body, name=name, in_specs=[_HBM] * (2 * n) + [_SEM, _SEM, pl.BlockSpec(memory_space=pl.ANY)],
        out_specs=[_HBM] * (2 * n), out_shape=[pltpu.HBM(a.shape, a.dtype) for a in srcs + lands],
        input_output_aliases={i: i for i in range(2 * n)},
        compiler_params=pltpu.CompilerParams(has_side_effects=pltpu.SideEffectType.DATAFLOW_SIDE_EFFECTING),
    )(*srcs, *lands, send_sems, recv_sems, after)
    return list(out[:n]), list(out[n:])


def _adam_math(w, g, m, v):
    m2 = ADAM_B1 * m + (1.0 - ADAM_B1) * g
    v2 = ADAM_B2 * v + (1.0 - ADAM_B2) * (g * g)
    m_hat = m2 / (1.0 - ADAM_B1 ** ADAM_STEP)
    v_hat = v2 / (1.0 - ADAM_B2 ** ADAM_STEP)
    return -ADAM_LR * (m_hat / (jnp.sqrt(v_hat) + ADAM_EPS) + ADAM_WD * w), m2, v2


def _adam_big(parts, w, m, v, anchor, name):
    L, rows, cols = w.shape
    flat = [(a, slot) for layer_parts in parts for a, slot in layer_parts]
    per = len(parts[0])

    def body(*refs):
        prefs = refs[:len(flat)]
        w_ref, m_ref, v_ref, _, g_ref, d_ref, m2_ref, v2_ref = refs[len(flat):]
        for li in range(L):
            @pl.when(pl.program_id(0) == li)
            def _():
                c0 = 0
                for pref in prefs[li * per:(li + 1) * per]:
                    acc = pref[0].astype(f32)
                    for s in range(1, pref.shape[0]):
                        acc = acc + pref[s].astype(f32)
                    width = min(acc.shape[1], cols - c0)
                    g_ref[:, c0:c0 + width] = acc[0:rows, 0:width]
                    c0 += width

        d, m2, v2 = _adam_math(w_ref[...], g_ref[...], m_ref[...], v_ref[...])
        d_ref[...] = d
        m2_ref[...] = m2
        v2_ref[...] = v2

    wspec = pl.BlockSpec((None, rows, cols), lambda l: (l, 0, 0))
    in_specs = [pl.BlockSpec((a.shape[0], None) + a.shape[2:], functools.partial(lambda l, slot: (0, slot, 0, 0), slot=slot))
                for a, slot in flat]
    return pl.pallas_call(
        body, grid=(L,), in_specs=in_specs + [wspec] * 3 + [_full_spec((8, 128))], out_specs=[wspec] * 4,
        out_shape=[jax.ShapeDtypeStruct(w.shape, f32)] * 4,
        compiler_params=_cparams(("arbitrary",)), name=name)(*[a for a, _ in flat], w, m, v, anchor)


def _sum_sources(stacked):
    rows = stacked.shape[1]

    def body(s_ref, o_ref):
        acc = s_ref[0]
        for s in range(1, NDEV):
            acc = acc + s_ref[s]
        o_ref[...] = acc

    return pl.pallas_call(body, out_shape=jax.ShapeDtypeStruct((rows, 128), f32), name="sum_small_grads")(stacked)


def _adam_small(w, g, m, v):
    def body(w_ref, g_ref, m_ref, v_ref, d_ref, m2_ref, v2_ref):
        d, m2, v2 = _adam_math(w_ref[...], g_ref[...], m_ref[...], v_ref[...])
        d_ref[...] = d
        m2_ref[...] = m2
        v2_ref[...] = v2

    return pl.pallas_call(body, out_shape=[jax.ShapeDtypeStruct(w.shape, f32)] * 3, name="adam_small")(w, g, m, v)


def _pack_rows(arrs):
    flat = []
    for a in arrs:
        a = a.reshape(-1)
        flat.append(jnp.pad(a, (0, (-a.shape[0]) % 1024)))
    return jnp.concatenate(flat).reshape(-1, 128)


def _unpack_rows(packed, shapes):
    out, off = [], 0
    flat = packed.reshape(-1)
    for s in shapes:
        n = math.prod(s)
        out.append(flat[off:off + n].reshape(s))
        off += n + (-n) % 1024
    return out


def _gather_conv(gathered, shape):
    L, K, c = shape
    return jnp.transpose(gathered, (1, 2, 0, 3)).reshape(L, K, NDEV * c)


def kernel(x, p, positions, ln_ffn1_g, ln_ffn1_b, ffn1_w_gate, ffn1_w_up, ffn1_w_down, w_in, ret_norm_g, lru_conv_w, lru_conv_b, lru_w_a, lru_b_a, lru_w_x, lru_b_x, lru_lambda, gdn_conv_w, gdn_a_log, gdn_dt_bias, gdn_norm_g, w_out, ln_mix_g, ln_mix_b, ffn2_w_gate, ffn2_w_up, ffn2_w_down, ple_w_gate, ple_w_proj, ln_ffn2_g, ln_ffn2_b, loss_target, m_ln_ffn1_g, m_ln_ffn1_b, m_ffn1_w_gate, m_ffn1_w_up, m_ffn1_w_down, m_w_in, m_ret_norm_g, m_lru_conv_w, m_lru_conv_b, m_lru_w_a, m_lru_b_a, m_lru_w_x, m_lru_b_x, m_lru_lambda, m_gdn_conv_w, m_gdn_a_log, m_gdn_dt_bias, m_gdn_norm_g, m_w_out, m_ln_mix_g, m_ln_mix_b, m_ffn2_w_gate, m_ffn2_w_up, m_ffn2_w_down, m_ple_w_gate, m_ple_w_proj, m_ln_ffn2_g, m_ln_ffn2_b, v_ln_ffn1_g, v_ln_ffn1_b, v_ffn1_w_gate, v_ffn1_w_up, v_ffn1_w_down, v_w_in, v_ret_norm_g, v_lru_conv_w, v_lru_conv_b, v_lru_w_a, v_lru_b_a, v_lru_w_x, v_lru_b_x, v_lru_lambda, v_gdn_conv_w, v_gdn_a_log, v_gdn_dt_bias, v_gdn_norm_g, v_w_out, v_ln_mix_g, v_ln_mix_b, v_ffn2_w_gate, v_ffn2_w_up, v_ffn2_w_down, v_ple_w_gate, v_ple_w_proj, v_ln_ffn2_g, v_ln_ffn2_b):
    args = locals()
    W = {n: args[n] for n in WEIGHTS}
    M = {n: args['m_' + n] for n in WEIGHTS}
    V = {n: args['v_' + n] for n in WEIGHTS}
    me = 4 * lax.axis_index("x") + 2 * lax.axis_index("y") + lax.axis_index("c")

    core = lax.axis_index("c")
    chip = 2 * lax.axis_index("x") + lax.axis_index("y")

    packed = _pack_big(W)

    def group(l, name):
        per = {k: packed[k].shape[0] // DEPTH for k in PACKS}
        if name == 'f1':
            return [packed['p384'][l * per['p384']:l * per['p384'] + 2], packed['pd'][l * per['pd']:l * per['pd'] + 1]]
        return [packed['p384'][l * per['p384'] + 2:(l + 1) * per['p384']],
                packed['pd'][l * per['pd'] + 1:(l + 1) * per['pd']]] + [
                    packed[k][l * per[k]:(l + 1) * per[k]] for k in PACKS[2:]]

    def as_weights(arrs):
        G = dict(zip(PACKS, arrs))
        if 'ppp' in G:
            G['wpp'] = jnp.transpose(G.pop('ppp'), (1, 2, 0, 3)).reshape(PLE, D)
        return G

    conv_pack = _pack_rows([W[n] for n in CONV_SHARDED])
    g0 = _gather_two_level(group(0, 'f1') + [conv_pack], "gather_weights")
    g0, rest0 = lax.optimization_barrier((g0, group(0, 'rest')))
    start0 = _split_start(rest0, [(NDEV,) + a.shape for a in rest0], _gather_plan, NDEV - 1, "gather_start_0")
    tok0, all1 = lax.optimization_barrier((start0[4], group(1, 'f1') + group(1, 'rest')))
    start1 = _split_start(all1, [(NDEV,) + a.shape for a in all1], _gather_plan, NDEV - 1, "gather_start_1")
    arrived = {}

    def gather_done(started, after, name):
        srcs, lands = _split_wait(started[0], started[1], started[2], started[3], after, _gather_plan, name)
        return [lax.dynamic_update_slice_in_dim(ld, s[None], me, axis=0) for s, ld in zip(srcs, lands)]

    def fetch(l, name, after):
        if l == 0 and name == 'f1':
            return as_weights(g0[:-1]), tok0[0, 0] + start1[4][0, 0]
        if l == 0:
            return as_weights(gather_done(start0, after, "gather_wait_0")), 0.0
        if name == 'f1':
            arrived[1] = gather_done(start1, after, "gather_wait_1")
            return as_weights(arrived[1][:2]), 0.0
        return as_weights(arrived[1][2:]), 0.0

    conv_all = g0[-1]
    sm = {n: W[n] for n in SMALL}
    conv_shards = [_unpack_rows(conv_all[s], [W[n].shape for n in CONV_SHARDED]) for s in range(NDEV)]
    for i, n in enumerate(CONV_SHARDED):
        sm[n] = _gather_conv(jnp.stack([cs[i] for cs in conv_shards]), W[n].shape)

    received, started = {}, {}

    def emit(l, name, grads):
        keys = list(grads)
        arrs = [grads[k] for k in keys]
        gots = _scatter_pairs(arrs, "scatter_pairs")
        owns = [lax.dynamic_index_in_dim(a.reshape((4, 2) + a.shape[1:]), core, axis=1, keepdims=False) for a in arrs]
        pair = [_pair_sum(o, g, "pair_sum_" + k) for k, o, g in zip(keys, owns, gots)]
        started[l, name] = (keys, _split_start(pair, [a.shape for a in pair], _chips_plan, 3,
                                               f"scatter_start_{l}_{name}"))
        return started[l, name][1][4][0, 0]

    def scatter_done(l, name, after):
        keys, st = started[l, name]
        srcs, lands = _split_wait(st[0], st[1], st[2], st[3], after, _chips_plan, f"scatter_wait_{l}_{name}")
        received[l, name] = dict(zip(keys, [
            lax.dynamic_update_slice_in_dim(ld, lax.dynamic_index_in_dim(s, chip, axis=0), chip, axis=0)
            for s, ld in zip(srcs, lands)]))

    loss, grad_x, small = _local_step(x[0], p[:, 0], positions.reshape(-1, 1), loss_target[0], fetch, emit, sm)
    loss = lax.psum(loss[0, 0], ("x", "y", "c"))
    last = (0, 'f1')
    for l, name in started:
        if (l, name) != last:
            scatter_done(l, name, grad_x)

    anchor = started[last][1][4]
    small_pack = _pack_rows([small[n] for n in SMALL]) + anchor[0, 0]
    small_all = _exchange([small_pack], [False], "gather_small_grads")[0]
    small_sum = _unpack_rows(_sum_sources(small_all), [small[n].shape for n in SMALL])
    grads, delta, new_m, new_v = {}, {}, {}, {}
    for n, g in zip(SMALL, small_sum):
        if n in CONV_SHARDED:
            c = W[n].shape[2]
            g = lax.dynamic_slice_in_dim(g, me * c, c, axis=2)
        grads[n] = g

    big_parts = {
        'ffn1_w_gate': [('f1', 'p384', 0)], 'ffn1_w_up': [('f1', 'p384', 1)], 'ffn1_w_down': [('f1', 'pd', 0)],
        'ffn2_w_gate': [('rest', 'p384', 0)], 'ffn2_w_up': [('rest', 'p384', 1)], 'ffn2_w_down': [('rest', 'pd', 0)],
        'w_in': [('rest', 'pr', 0), ('rest', 'pinl', 0), ('rest', 'ping', 0)], 'w_out': [('rest', 'pr', 1)],
        'ple_w_gate': [('rest', 'pr', 2)], 'ple_w_proj': [('rest', 'ppp', 0)],
    }
    def adam(n):
        parts = [[(received[l, grp][k], slot) for grp, k, slot in big_parts[n]] for l in range(DEPTH)]
        grads[n], delta[n], new_m[n], new_v[n] = _adam_big(parts, W[n], M[n], V[n], anchor, "adam_" + n)

    shapes = [W[n].shape for n in SMALL]
    d_s, m_s, v_s = _adam_small(*[_pack_rows([src[n] for n in SMALL]) for src in (W, grads, M, V)])
    for n, dd, mm, vv in zip(SMALL, _unpack_rows(d_s, shapes), _unpack_rows(m_s, shapes), _unpack_rows(v_s, shapes)):
        delta[n], new_m[n], new_v[n] = dd, mm, vv
    waits_last = [n for n in BIG if big_parts[n][0][0] == last[1]]
    for n in BIG:
        if n not in waits_last:
            adam(n)
    done = jnp.stack([d_s[0, 0]] + [delta[n][0, 0, 0] for n in BIG if n not in waits_last])
    scatter_done(*last, done)
    for n in waits_last:
        adam(n)

    return (loss, grad_x[None], *[grads[n] for n in WEIGHTS], *[delta[n] for n in WEIGHTS],
            *[new_m[n] for n in WEIGHTS], *[new_v[n] for n in WEIGHTS])
```

```python
import functools
import math

import jax
import jax.numpy as jnp
from jax import lax
from jax.experimental import pallas as pl
from jax.experimental.pallas import tpu as pltpu

f32 = jnp.float32
bf16 = jnp.bfloat16

NDEV = 8
DEPTH = 2
D = 1024
FS = 352
FSP = 384
FB = 2
NF = NDEV // FB
PLE = 256
CH = 64
RET_H, GDN_H = 4, 6
RET_W, LRU_W, GDN_W = 256, 384, 384
GDN_IN = 1664
GDN_IN_REAL = 1548
D_IN = 3340
ALPHA = 4.0 ** 0.25
LN_EPS = 1e-5
ROPE_THETA = 10000.0
TM = 512
RB_RET, RB_LRU, RB_GDN = 512, 512, 256
VMEM_LIMIT = 56 * 1024 * 1024
ADAM_LR, ADAM_B1, ADAM_B2, ADAM_EPS, ADAM_WD, ADAM_STEP = 0.001, 0.9, 0.999, 1e-08, 0.01, 10

WEIGHTS = ['ln_ffn1_g', 'ln_ffn1_b', 'ffn1_w_gate', 'ffn1_w_up', 'ffn1_w_down', 'w_in', 'ret_norm_g', 'lru_conv_w',
           'lru_conv_b', 'lru_w_a', 'lru_b_a', 'lru_w_x', 'lru_b_x', 'lru_lambda', 'gdn_conv_w', 'gdn_a_log',
           'gdn_dt_bias', 'gdn_norm_g', 'w_out', 'ln_mix_g', 'ln_mix_b', 'ffn2_w_gate', 'ffn2_w_up', 'ffn2_w_down',
           'ple_w_gate', 'ple_w_proj', 'ln_ffn2_g', 'ln_ffn2_b']
BIG = ['ffn1_w_gate', 'ffn1_w_up', 'ffn1_w_down', 'w_in', 'w_out', 'ffn2_w_gate', 'ffn2_w_up', 'ffn2_w_down',
       'ple_w_gate', 'ple_w_proj']
SMALL = [n for n in WEIGHTS if n not in BIG]
PACKS = ('p384', 'pd', 'pr', 'pinl', 'ping', 'ppp')
CONV_SHARDED = {'lru_conv_w': LRU_W, 'gdn_conv_w': 3 * GDN_W}


def _cparams(sem=None):
    return pltpu.CompilerParams(dimension_semantics=sem, vmem_limit_bytes=VMEM_LIMIT)


def _sigmoid(x):
    return 1.0 / (1.0 + jnp.exp(-x))


def _silu(x):
    return x * _sigmoid(x)


def _dsilu(x):
    s = _sigmoid(x)
    return s * (1.0 + x * (1.0 - s))


def _softplus(x):
    return jnp.maximum(x, 0.0) + jnp.log(1.0 + jnp.exp(-jnp.abs(x)))


def _gelu(x):
    return 0.5 * x * (1.0 + jnp.tanh(0.7978845608028654 * (x + 0.044715 * x * x * x)))


def _dot(a, b):
    return jnp.dot(a.astype(bf16), b.astype(bf16), preferred_element_type=f32)


def _dot_nt(a, b):
    return lax.dot_general(a.astype(bf16), b.astype(bf16), (((1,), (1,)), ((), ())), preferred_element_type=f32)


def _dot_tn(a, b):
    return lax.dot_general(a.astype(bf16), b.astype(bf16), (((0,), (0,)), ((), ())), preferred_element_type=f32)


def _bmm(eq, a, b):
    return jnp.einsum(eq, a.astype(bf16), b.astype(bf16), preferred_element_type=f32)


def _split3(a):
    a1 = a.astype(bf16)
    r = a - a1.astype(f32)
    a2 = r.astype(bf16)
    return a1, a2, (r - a2.astype(f32)).astype(bf16)


def _bmm3(eq, a, b):
    a1, a2, _ = _split3(a)
    b1, b2, _ = _split3(b)
    e = lambda x, y: jnp.einsum(eq, x, y, preferred_element_type=f32)
    return e(a1, b1) + (e(a1, b2) + e(a2, b1))


def _rowsum(x):
    x1, x2, _ = _split3(x)
    ones = jnp.ones((x.shape[0], CH, CH), bf16)
    e = lambda y: jnp.einsum('bij,bjk->bik', y, ones, preferred_element_type=f32)
    return e(x1) + e(x2)


def _tri_ones(B, upper=False):
    ii = lax.broadcasted_iota(jnp.int32, (B, CH, CH), 1)
    jj = lax.broadcasted_iota(jnp.int32, (B, CH, CH), 2)
    return jnp.where((ii <= jj) if upper else (ii >= jj), 1.0, 0.0).astype(bf16)


def _cumsum_mm(t, x):
    x1, x2, x3 = _split3(x)
    e = lambda y: jnp.einsum('bij,bjk->bik', t, y, preferred_element_type=f32)
    return e(x1) + (e(x2) + e(x3))


def _chunk_cumsum(x, reverse=False):
    n = x.shape[0] // CH
    return _cumsum_mm(_tri_ones(n, upper=reverse), x.reshape(n, CH, 128)).reshape(x.shape)


@jax.custom_vjp
def _neumann_inv(m):
    ii = lax.broadcasted_iota(jnp.int32, m.shape, 1)
    jj = lax.broadcasted_iota(jnp.int32, m.shape, 2)
    inv = jnp.where(ii == jj, 1.0, 0.0).astype(f32) + m
    mp = m
    for _ in range(5):
        mp = _bmm3('bij,bjk->bik', mp, mp)
        inv = inv + _bmm3('bij,bjk->bik', inv, mp)
    return inv


def _neumann_inv_fwd(m):
    inv = _neumann_inv(m)
    return inv, inv


def _neumann_inv_bwd(inv, g):
    return (_bmm3('bij,bkj->bik', _bmm3('bji,bjk->bik', inv, g), inv),)


_neumann_inv.defvjp(_neumann_inv_fwd, _neumann_inv_bwd)


@jax.custom_vjp
def _known_inv(m, inv):
    return inv


def _known_inv_fwd(m, inv):
    return inv, inv


def _known_inv_bwd(inv, g):
    return _neumann_inv_bwd(inv, g)[0], jnp.zeros_like(inv)


_known_inv.defvjp(_known_inv_fwd, _known_inv_bwd)


def _ln_stats(z):
    mu = jnp.mean(z, -1, keepdims=True)
    zc = z - mu
    rstd = lax.rsqrt(jnp.mean(zc * zc, -1, keepdims=True) + LN_EPS)
    return zc * rstd, rstd


def _ln_bwd(z, g, dout):
    xh, rstd = _ln_stats(z)
    dxh = dout * g
    dz = rstd * (dxh - jnp.mean(dxh, -1, keepdims=True) - xh * jnp.mean(dxh * xh, -1, keepdims=True))
    return dz, jnp.sum(dout * xh, 0, keepdims=True), jnp.sum(dout, 0, keepdims=True)


def _full_spec(shape):
    nd = len(shape)
    return pl.BlockSpec(shape, lambda *_: (0,) * nd)


def _ffn_fwd(x, p384, pd, lg, lb, slot, which, ple=None):
    T = x.shape[0]
    sg, su, sd = 2 * slot, 2 * slot + 1, slot
    has_ple = ple is not None

    def body(*refs):
        if has_ple:
            (x_ref, wg_ref, wu_ref, wd_ref, lg_ref, lb_ref, p_ref, wpg_ref, wpp_ref,
             z_ref, o_ref, g_ref, u_ref, acc, xb_s) = refs
        else:
            x_ref, wg_ref, wu_ref, wd_ref, lg_ref, lb_ref, z_ref, o_ref, g_ref, u_ref, acc, xb_s = refs
        f = pl.program_id(1)

        @pl.when(f == 0)
        def _():
            x = x_ref[...]
            xb = x.astype(bf16)
            xb_s[...] = xb
            base = ALPHA * x
            if has_ple:
                gate = _sigmoid(_dot(xb, wpg_ref[...].reshape(D, D)))
                base = base + gate * _dot(p_ref[...], wpp_ref[...])
            acc[...] = base

        xb = xb_s[...]
        g = _dot(xb, wg_ref[...])
        u = _dot(xb, wu_ref[...])
        g_ref[...] = g.astype(bf16)
        u_ref[...] = u.astype(bf16)
        acc[...] += 0.5 * _dot(_silu(g) * u, wd_ref[...].reshape(FB * FSP, D))

        @pl.when(f == NF - 1)
        def _():
            z = acc[...]
            z_ref[...] = z
            o_ref[...] = _ln_stats(z)[0] * lg_ref[...] + lb_ref[...]

    row = pl.BlockSpec((TM, D), lambda i, f: (i, 0))
    in_specs = [row,
                pl.BlockSpec((None, D, FB * FSP), lambda i, f: (sg, 0, f)),
                pl.BlockSpec((None, D, FB * FSP), lambda i, f: (su, 0, f)),
                pl.BlockSpec((FB, None, FSP, D), lambda i, f: (f, sd, 0, 0)),
                _full_spec((1, D)), _full_spec((1, D))]
    args = [x, p384, p384, pd, lg, lb]
    if has_ple:
        p, pr, wpp = ple
        in_specs += [pl.BlockSpec((TM, PLE), lambda i, f: (i, 0)),
                     pl.BlockSpec((NDEV, None, 128, D), lambda i, f: (0, 2, 0, 0)),
                     _full_spec((PLE, D))]
        args += [p, pr, wpp]
    hid = pl.BlockSpec((TM, FB * FSP), lambda i, f: (i, f))
    hshape = jax.ShapeDtypeStruct((T, NDEV * FSP), bf16)
    return pl.pallas_call(
        body, grid=(T // TM, NF), in_specs=in_specs, out_specs=[row, row, hid, hid],
        out_shape=[jax.ShapeDtypeStruct((T, D), f32)] * 2 + [hshape, hshape],
        scratch_shapes=[pltpu.VMEM((TM, D), f32), pltpu.VMEM((TM, D), bf16)],
        compiler_params=_cparams(("arbitrary", "arbitrary")), name=f"ffn{which + 1}_fwd")(*args)


def _ffn_bwd(z, dout, gs, us, p384, pd, lg, slot, which):
    T = z.shape[0]
    TMB = TM
    sg, su, sd = 2 * slot, 2 * slot + 1, slot

    def body(z_ref, do_ref, g_ref, u_ref, wg_ref, wu_ref, wd_ref, lg_ref,
             dx_ref, dg_ref, du_ref, a_ref, dy_ref, dlg_ref, dlb_ref, acc, dyb):
        i, f = pl.program_id(0), pl.program_id(1)

        @pl.when(jnp.logical_and(i == 0, f == 0))
        def _():
            dlg_ref[...] = jnp.zeros_like(dlg_ref)
            dlb_ref[...] = jnp.zeros_like(dlb_ref)

        @pl.when(f == 0)
        def _():
            dz, dlg, dlb = _ln_bwd(z_ref[...], lg_ref[...], do_ref[...])
            dlg_ref[...] += dlg
            dlb_ref[...] += dlb
            dy = (0.5 * dz).astype(bf16)
            dyb[...] = dy
            dy_ref[...] = dy
            acc[...] = ALPHA * dz

        g = g_ref[...].astype(f32)
        u = u_ref[...].astype(f32)
        da = _dot_nt(dyb[...], wd_ref[...].reshape(FB * FSP, D))
        sgm = _sigmoid(g)
        dg = (da * u * (sgm * (1.0 + g * (1.0 - sgm)))).astype(bf16)
        du = (da * (g * sgm)).astype(bf16)
        dg_ref[...] = dg
        du_ref[...] = du
        a_ref[...] = (g * sgm * u).astype(bf16)
        acc[...] += _dot_nt(dg, wg_ref[...]) + _dot_nt(du, wu_ref[...])

        @pl.when(f == NF - 1)
        def _():
            dx_ref[...] = acc[...]

    row = pl.BlockSpec((TMB, D), lambda i, f: (i, 0))
    hid = pl.BlockSpec((TMB, FB * FSP), lambda i, f: (i, f))
    vec = _full_spec((1, D))
    in_specs = [row, row, hid, hid,
                pl.BlockSpec((None, D, FB * FSP), lambda i, f: (sg, 0, f)),
                pl.BlockSpec((None, D, FB * FSP), lambda i, f: (su, 0, f)),
                pl.BlockSpec((FB, None, FSP, D), lambda i, f: (f, sd, 0, 0)),
                vec]
    args = [z, dout, gs, us, p384, p384, pd, lg]
    out_specs = [row, hid, hid, hid, row, vec, vec]
    hshape = jax.ShapeDtypeStruct((T, NDEV * FSP), bf16)
    out_shape = [jax.ShapeDtypeStruct((T, D), f32), hshape, hshape, hshape, jax.ShapeDtypeStruct((T, D), bf16),
                 jax.ShapeDtypeStruct((1, D), f32), jax.ShapeDtypeStruct((1, D), f32)]
    return pl.pallas_call(
        body, grid=(T // TMB, NF), in_specs=in_specs, out_specs=out_specs, out_shape=out_shape,
        scratch_shapes=[pltpu.VMEM((TMB, D), f32), pltpu.VMEM((TMB, D), bf16)],
        compiler_params=_cparams(("arbitrary", "arbitrary")), name=f"ffn{which + 1}_bwd")(*args)


def _ple_bwd(x, p, dy, dx_ffn, pr, wpp, layer):
    T = x.shape[0]

    def body(x_ref, p_ref, dy_ref, dxf_ref, wpg_ref, wpp_ref, dx_ref, dgp_ref, dpj_ref):
        dz = 2.0 * dy_ref[...].astype(f32)
        wpg = wpg_ref[...].reshape(D, D)
        gate = _sigmoid(_dot(x_ref[...], wpg))
        proj = _dot(p_ref[...], wpp_ref[...])
        dgp = (dz * proj * gate * (1.0 - gate)).astype(bf16)
        dgp_ref[...] = dgp
        dpj_ref[...] = (dz * gate).astype(bf16)
        dx_ref[...] = dxf_ref[...] + _dot_nt(dgp, wpg)

    row = pl.BlockSpec((TM, D), lambda i: (i, 0))
    return pl.pallas_call(
        body, grid=(T // TM,),
        in_specs=[row, pl.BlockSpec((TM, PLE), lambda i: (i, 0)), row, row,
                  pl.BlockSpec((NDEV, None, 128, D), lambda i: (0, 3 * layer + 2, 0, 0)), _full_spec((PLE, D))],
        out_specs=[row, row, row],
        out_shape=[jax.ShapeDtypeStruct((T, D), f32), jax.ShapeDtypeStruct((T, D), bf16),
                   jax.ShapeDtypeStruct((T, D), bf16)],
        compiler_params=_cparams(("arbitrary",)), name="ple_bwd")(x, p, dy, dx_ffn, pr, wpp)


def _matmul_tn(a, b, nb, name, nsub=1):
    T, M = a.shape
    N = b.shape[1]
    wide = nsub * nb
    tk = min(T, 1024 if wide <= 2048 else 512)
    nk = T // tk

    def body(a_ref, b_ref, o_ref, acc):
        k = pl.program_id(1)

        @pl.when(k == 0)
        def _():
            acc[...] = jnp.zeros_like(acc)

        acc[...] += _dot_tn(a_ref[...], b_ref[...])

        @pl.when(k == nk - 1)
        def _():
            for j in range(nsub):
                o_ref[j] = acc[:, j * nb:(j + 1) * nb].astype(bf16)

    return pl.pallas_call(
        body, grid=(N // wide, nk),
        in_specs=[pl.BlockSpec((tk, M), lambda n, k: (k, 0)), pl.BlockSpec((tk, wide), lambda n, k: (k, n))],
        out_specs=pl.BlockSpec((nsub, M, nb), lambda n, k: (n, 0, 0)),
        out_shape=jax.ShapeDtypeStruct((N // nb, M, nb), bf16),
        scratch_shapes=[pltpu.VMEM((M, wide), f32)],
        compiler_params=_cparams(("arbitrary", "arbitrary")), name=name)(a, b)


def _proj_in(x, pr, pinl, ping, layer):
    T = x.shape[0]

    def body(x_ref, wr_ref, wl_ref, wg_ref, hr_ref, hl_ref, hg_ref):
        xb = x_ref[...].astype(bf16)
        hr_ref[...] = _dot(xb, wr_ref[...].reshape(D, D))
        hl_ref[...] = _dot(xb, wl_ref[...].reshape(D, 2 * LRU_W))
        hg_ref[...] = _dot(xb, wg_ref[...].reshape(D, GDN_IN))

    return pl.pallas_call(
        body, grid=(T // TM,),
        in_specs=[pl.BlockSpec((TM, D), lambda i: (i, 0)),
                  pl.BlockSpec((NDEV, None, 128, D), lambda i: (0, 3 * layer, 0, 0)),
                  pl.BlockSpec((NDEV, None, 128, 2 * LRU_W), lambda i: (0, layer, 0, 0)),
                  pl.BlockSpec((NDEV, None, 128, GDN_IN), lambda i: (0, layer, 0, 0))],
        out_specs=[pl.BlockSpec((TM, D), lambda i: (i, 0)), pl.BlockSpec((TM, 2 * LRU_W), lambda i: (i, 0)),
                   pl.BlockSpec((TM, GDN_IN), lambda i: (i, 0))],
        out_shape=[jax.ShapeDtypeStruct((T, D), f32), jax.ShapeDtypeStruct((T, 2 * LRU_W), f32),
                   jax.ShapeDtypeStruct((T, GDN_IN), f32)],
        compiler_params=_cparams(("arbitrary",)), name="proj_in")(x, pr, pinl, ping)


def _proj_in_bwd(base, dhr, dhl, dhg, pr, pinl, ping, layer):
    T = base.shape[0]

    def body(b_ref, dr_ref, dl_ref, dg_ref, wr_ref, wl_ref, wg_ref, o_ref):
        o_ref[...] = (b_ref[...] + _dot_nt(dr_ref[...], wr_ref[...].reshape(D, D))
                      + _dot_nt(dl_ref[...], wl_ref[...].reshape(D, 2 * LRU_W))
                      + _dot_nt(dg_ref[...], wg_ref[...].reshape(D, GDN_IN)))

    return pl.pallas_call(
        body, grid=(T // TM,),
        in_specs=[pl.BlockSpec((TM, D), lambda i: (i, 0)), pl.BlockSpec((TM, D), lambda i: (i, 0)),
                  pl.BlockSpec((TM, 2 * LRU_W), lambda i: (i, 0)), pl.BlockSpec((TM, GDN_IN), lambda i: (i, 0)),
                  pl.BlockSpec((NDEV, None, 128, D), lambda i: (0, 3 * layer, 0, 0)),
                  pl.BlockSpec((NDEV, None, 128, 2 * LRU_W), lambda i: (0, layer, 0, 0)),
                  pl.BlockSpec((NDEV, None, 128, GDN_IN), lambda i: (0, layer, 0, 0))],
        out_specs=pl.BlockSpec((TM, D), lambda i: (i, 0)),
        out_shape=jax.ShapeDtypeStruct((T, D), f32),
        compiler_params=_cparams(("arbitrary",)), name="proj_in_bwd")(base, dhr, dhl, dhg, pr, pinl, ping)


def _mix_out(x1, o_r, o_l, o_g, pr, lg, lb, layer):
    T = x1.shape[0]

    def body(x_ref, r_ref, l_ref, g_ref, w_ref, lg_ref, lb_ref, z_ref, o_ref):
        w = w_ref[...].reshape(D, D)
        z = (ALPHA * x_ref[...] + _dot(r_ref[...], w[0:RET_W]) + _dot(l_ref[...], w[RET_W:RET_W + LRU_W])
             + _dot(g_ref[...], w[RET_W + LRU_W:D]))
        z_ref[...] = z
        o_ref[...] = _ln_stats(z)[0] * lg_ref[...] + lb_ref[...]

    row = pl.BlockSpec((TM, D), lambda i: (i, 0))
    return pl.pallas_call(
        body, grid=(T // TM,),
        in_specs=[row, pl.BlockSpec((TM, RET_W), lambda i: (i, 0)), pl.BlockSpec((TM, LRU_W), lambda i: (i, 0)),
                  pl.BlockSpec((TM, GDN_W), lambda i: (i, 0)),
                  pl.BlockSpec((NDEV, None, 128, D), lambda i: (0, 3 * layer + 1, 0, 0)),
                  _full_spec((1, D)), _full_spec((1, D))],
        out_specs=[row, row], out_shape=[jax.ShapeDtypeStruct((T, D), f32)] * 2,
        compiler_params=_cparams(("arbitrary",)), name="mix_out")(x1, o_r, o_l, o_g, pr, lg, lb)


def _mix_out_bwd(z, dout, pr, lg, layer):
    T = z.shape[0]

    def body(z_ref, do_ref, w_ref, lg_ref, dxb_ref, dzb_ref, dr_ref, dl_ref, dg_ref, dlg_ref, dlb_ref):
        @pl.when(pl.program_id(0) == 0)
        def _():
            dlg_ref[...] = jnp.zeros_like(dlg_ref)
            dlb_ref[...] = jnp.zeros_like(dlb_ref)

        dz, dlg, dlb = _ln_bwd(z_ref[...], lg_ref[...], do_ref[...])
        dlg_ref[...] += dlg
        dlb_ref[...] += dlb
        dxb_ref[...] = ALPHA * dz
        dzb = dz.astype(bf16)
        dzb_ref[...] = dzb
        w = w_ref[...].reshape(D, D)
        dr_ref[...] = _dot_nt(dzb, w[0:RET_W])
        dl_ref[...] = _dot_nt(dzb, w[RET_W:RET_W + LRU_W])
        dg_ref[...] = _dot_nt(dzb, w[RET_W + LRU_W:D])

    row = pl.BlockSpec((TM, D), lambda i: (i, 0))
    vec = _full_spec((1, D))
    return pl.pallas_call(
        body, grid=(T // TM,),
        in_specs=[row, row, pl.BlockSpec((NDEV, None, 128, D), lambda i: (0, 3 * layer + 1, 0, 0)), vec],
        out_specs=[row, row, pl.BlockSpec((TM, RET_W), lambda i: (i, 0)), pl.BlockSpec((TM, LRU_W), lambda i: (i, 0)),
                   pl.BlockSpec((TM, GDN_W), lambda i: (i, 0)), vec, vec],
        out_shape=[jax.ShapeDtypeStruct((T, D), f32), jax.ShapeDtypeStruct((T, D), bf16),
                   jax.ShapeDtypeStruct((T, RET_W), f32), jax.ShapeDtypeStruct((T, LRU_W), f32),
                   jax.ShapeDtypeStruct((T, GDN_W), f32), jax.ShapeDtypeStruct((1, D), f32),
                   jax.ShapeDtypeStruct((1, D), f32)],
        compiler_params=_cparams(("arbitrary",)), name="mix_out_bwd")(z, dout, pr, lg)


def _loss_grad(y, target):
    T = y.shape[0]

    def body(y_ref, t_ref, dy_ref, l_ref):
        @pl.when(pl.program_id(0) == 0)
        def _():
            l_ref[...] = jnp.zeros_like(l_ref)

        e = y_ref[...] - t_ref[...]
        dy_ref[...] = e * (1.0 / D)
        l_ref[...] += 0.5 * jnp.sum(jnp.sum(e * e, -1, keepdims=True) * (1.0 / D), 0, keepdims=True)

    row = pl.BlockSpec((TM, D), lambda i: (i, 0))
    return pl.pallas_call(
        body, grid=(T // TM,), in_specs=[row, row], out_specs=[row, _full_spec((1, 1))],
        out_shape=[jax.ShapeDtypeStruct((T, D), f32), jax.ShapeDtypeStruct((1, 1), f32)],
        compiler_params=_cparams(("arbitrary",)), name="loss_grad")(y, target)


def _split_heads(x, H):
    n = x.shape[0] // CH
    parts = [x[:, h * CH:(h + 1) * CH].reshape(n, CH, CH) for h in range(H)]
    return jnp.stack(parts, axis=1).reshape(n * H, CH, CH)


def _merge_heads(ref, x, H, col0=0):
    n = x.shape[0] // H
    x4 = x.reshape(n, H, CH, CH)
    for h in range(H):
        ref[:, col0 + h * CH:col0 + (h + 1) * CH] = x4[:, h].reshape(n * CH, CH)


def _rows_down(x, before, s):
    r8 = lax.broadcasted_iota(jnp.int32, before.shape, 0)
    top = jnp.where(r8 < s, pltpu.roll(before, s, 0), pltpu.roll(x[0:8], s, 0))
    return jnp.concatenate([top, pltpu.roll(x, s, 0)[8:]], axis=0)


def _rows_up(x, after, s):
    R = x.shape[0]
    r8 = lax.broadcasted_iota(jnp.int32, after.shape, 0)
    bottom = jnp.where(r8 >= 8 - s, pltpu.roll(after, 8 - s, 0), pltpu.roll(x[R - 8:R], 8 - s, 0))
    return jnp.concatenate([pltpu.roll(x, R - s, 0)[0:R - 8], bottom], axis=0)


def _conv_fwd(ext, x, tail, w, R):
    ext[0:8, :] = tail
    ext[8:R + 8, :] = x
    y = w[3:4, :] * x
    for k in range(3):
        y = y + w[k:k + 1, :] * _rows_down(x, tail, 3 - k)
    return y


def _conv_bwd(ext, dy, dy_next, w, R):
    x, tail = ext[8:8 + R, :], ext[0:8, :]
    dx = w[3:4, :] * dy
    dws = []
    for k in range(3):
        dx = dx + w[k:k + 1, :] * _rows_up(dy, dy_next, 3 - k)
        dws.append(jnp.sum(dy * _rows_down(x, tail, 3 - k), 0, keepdims=True))
    dws.append(jnp.sum(dy * x, 0, keepdims=True))
    return dx, jnp.concatenate(dws, axis=0)


def _prev_tail_spec(R, W):
    return pl.BlockSpec((8, W), lambda i: (jnp.maximum(i * (R // 8) - 1, 0), 0))


def _prev_tail_spec_rev(R, W, nb):
    return pl.BlockSpec((8, W), lambda i: (jnp.maximum((nb - 1 - i) * (R // 8) - 1, 0), 0))


def _rope_tables(positions):
    T = positions.shape[0]

    def body(p_ref, c_ref, s_ref):
        lane = lax.broadcasted_iota(jnp.int32, (TM, RET_W), 1)
        fi = (lane % 32).astype(f32)
        inv = jnp.exp(fi * (-math.log(ROPE_THETA) / 32.0))
        ang = p_ref[...].astype(f32) * inv
        c_ref[...] = jnp.cos(ang)
        s_ref[...] = jnp.where(lane % CH < 32, -jnp.sin(ang), jnp.sin(ang))

    row = pl.BlockSpec((TM, RET_W), lambda i: (i, 0))
    return pl.pallas_call(
        body, grid=(T // TM,), in_specs=[pl.BlockSpec((TM, 1), lambda i: (i, 0))], out_specs=[row, row],
        out_shape=[jax.ShapeDtypeStruct((T, RET_W), f32)] * 2,
        compiler_params=_cparams(("arbitrary",)), name="rope_tables")(positions)


def _partner(x):
    lane = lax.broadcasted_iota(jnp.int32, x.shape, 1)
    return jnp.where(lane % CH < 32, pltpu.roll(x, RET_W - 32, 1), pltpu.roll(x, 32, 1))


def _ret_consts():
    ii = lax.broadcasted_iota(jnp.int32, (CH, CH), 0).astype(f32)
    jj = lax.broadcasted_iota(jnp.int32, (CH, CH), 1).astype(f32)
    intra, cross, tail, cd = [], [], [], []
    for h in range(RET_H):
        lg = math.log1p(-(2.0 ** (-5.0 - h)))
        intra.append(jnp.exp(jnp.abs(ii - jj) * lg))
        cross.append(jnp.exp((ii + 1.0) * lg))
        tail.append(jnp.exp((CH - 1.0 - ii) * lg))
        cd.append(jnp.full((CH, CH), math.exp(CH * lg), f32))
    return jnp.stack(intra), jnp.stack(cross), jnp.stack(tail), jnp.stack(cd)


def _ret_chunk(consts, q, k, v, st):
    intra, cross, tail, cd = consts
    s = _bmm('hid,hjd->hij', q, k) * intra
    o = _bmm('hij,hje->hie', s, v) + _bmm('hid,hde->hie', q * cross, st)
    st2 = st * cd + _bmm('hjd,hje->hde', k * tail, v)
    oc = o - _rowsum(o) * (1.0 / CH)
    on = oc * lax.rsqrt(_rowsum(oc * oc) * (1.0 / CH) + 1e-5)
    return on, st2


def _ret_fwd(hr, cosw, sinw, gam):
    T = hr.shape[0]
    R = RB_RET
    nc = R // CH

    def body(h_ref, c_ref, s_ref, g_ref, o_ref, st_ref, st, wide):
        @pl.when(pl.program_id(0) == 0)
        def _():
            st[...] = jnp.zeros_like(st)

        consts = _ret_consts()
        cw, sw = c_ref[...], s_ref[...]
        q, k = h_ref[:, 0:RET_W], h_ref[:, RET_W:2 * RET_W]
        qh = _split_heads((q * cw + _partner(q) * sw) * 0.125, RET_H)
        kh = _split_heads(k * cw + _partner(k) * sw, RET_H)
        vh = _split_heads(h_ref[:, 2 * RET_W:3 * RET_W], RET_H)
        outs = []
        s_cur = st[...]
        for c in range(nc):
            sl = slice(c * RET_H, (c + 1) * RET_H)
            st_ref[c] = s_cur
            on, s_cur = _ret_chunk(consts, qh[sl], kh[sl], vh[sl], s_cur)
            outs.append(on)
        st[...] = s_cur
        _merge_heads(wide, jnp.concatenate(outs, axis=0), RET_H)
        o_ref[...] = wide[...] * g_ref[...] * _silu(h_ref[:, 3 * RET_W:4 * RET_W])

    blk = pl.BlockSpec((R, RET_W), lambda i: (i, 0))
    return pl.pallas_call(
        body, grid=(T // R,),
        in_specs=[pl.BlockSpec((R, D), lambda i: (i, 0)), blk, blk, _full_spec((1, RET_W))],
        out_specs=[blk, pl.BlockSpec((nc, RET_H, CH, CH), lambda i: (i, 0, 0, 0))],
        out_shape=[jax.ShapeDtypeStruct((T, RET_W), f32), jax.ShapeDtypeStruct((T // CH, RET_H, CH, CH), f32)],
        scratch_shapes=[pltpu.VMEM((RET_H, CH, CH), f32), pltpu.VMEM((R, RET_W), f32)],
        compiler_params=_cparams(("arbitrary",)), name="ret_fwd")(hr, cosw, sinw, gam)


def _ret_bwd(hr, cosw, sinw, gam, states, dout):
    T = hr.shape[0]
    R = RB_RET
    nc = R // CH
    nb = T // R

    def body(h_ref, c_ref, s_ref, g_ref, st_ref, do_ref, dh_ref, dgam_ref, dst, wide):
        @pl.when(pl.program_id(0) == 0)
        def _():
            dst[...] = jnp.zeros_like(dst)
            dgam_ref[...] = jnp.zeros_like(dgam_ref)

        consts = _ret_consts()
        cw, sw = c_ref[...], s_ref[...]
        q, k = h_ref[:, 0:RET_W], h_ref[:, RET_W:2 * RET_W]
        gr = h_ref[:, 3 * RET_W:4 * RET_W]
        qh = _split_heads((q * cw + _partner(q) * sw) * 0.125, RET_H)
        kh = _split_heads(k * cw + _partner(k) * sw, RET_H)
        vh = _split_heads(h_ref[:, 2 * RET_W:3 * RET_W], RET_H)
        do = do_ref[...]
        gam = g_ref[...]
        sg = _silu(gr)
        don = _split_heads(do * gam * sg, RET_H)
        ons, dqs, dks, dvs = [None] * nc, [None] * nc, [None] * nc, [None] * nc
        ds = dst[...]
        for c in reversed(range(nc)):
            sl = slice(c * RET_H, (c + 1) * RET_H)
            (on, _), vjp = jax.vjp(functools.partial(_ret_chunk, consts), qh[sl], kh[sl], vh[sl], st_ref[c])
            dqs[c], dks[c], dvs[c], ds = vjp((don[sl], ds))
            ons[c] = on
        dst[...] = ds
        _merge_heads(wide, jnp.concatenate(ons, axis=0), RET_H)
        onw = wide[...]
        dgam_ref[...] += jnp.sum(do * onw * sg, 0, keepdims=True)
        dh_ref[:, 3 * RET_W:4 * RET_W] = do * onw * gam * _dsilu(gr)
        _merge_heads(wide, jnp.concatenate(dqs, axis=0), RET_H)
        u = wide[...] * 0.125
        dh_ref[:, 0:RET_W] = u * cw + _partner(u * sw)
        _merge_heads(wide, jnp.concatenate(dks, axis=0), RET_H)
        u = wide[...]
        dh_ref[:, RET_W:2 * RET_W] = u * cw + _partner(u * sw)
        _merge_heads(dh_ref, jnp.concatenate(dvs, axis=0), RET_H, col0=2 * RET_W)

    blk = pl.BlockSpec((R, RET_W), lambda i: (nb - 1 - i, 0))
    return pl.pallas_call(
        body, grid=(nb,),
        in_specs=[pl.BlockSpec((R, D), lambda i: (nb - 1 - i, 0)), blk, blk, _full_spec((1, RET_W)),
                  pl.BlockSpec((nc, RET_H, CH, CH), lambda i: (nb - 1 - i, 0, 0, 0)), blk],
        out_specs=[pl.BlockSpec((R, D), lambda i: (nb - 1 - i, 0)), _full_spec((1, RET_W))],
        out_shape=[jax.ShapeDtypeStruct((T, D), f32), jax.ShapeDtypeStruct((1, RET_W), f32)],
        scratch_shapes=[pltpu.VMEM((RET_H, CH, CH), f32), pltpu.VMEM((R, RET_W), f32)],
        compiler_params=_cparams(("arbitrary",)), name="ret_bwd")(hr, cosw, sinw, gam, states, dout)


def _lru_ab(xc, wa, ba, wx, bx, lam):
    r = _sigmoid(_dot(xc, wa) + ba)
    i = _sigmoid(_dot(xc, wx) + bx)
    la = 8.0 * r * (-_softplus(-lam))
    a = jnp.exp(la)
    em = jnp.tanh(la) * (jnp.exp(2.0 * la) + 1.0)
    return a, jnp.sqrt(-em) * (i * xc)


def _lru_out(h, gate):
    return h * _gelu(gate)


def _scan_fwd(a, b):
    R = a.shape[0]
    row = lax.broadcasted_iota(jnp.int32, a.shape, 0)
    d = 1
    while d < R:
        m = row >= d
        b = jnp.where(m, a * pltpu.roll(b, d, 0) + b, b)
        a = jnp.where(m, a * pltpu.roll(a, d, 0), a)
        d *= 2
    return a, b


def _scan_bwd(a, b):
    R = a.shape[0]
    row = lax.broadcasted_iota(jnp.int32, a.shape, 0)
    d = 1
    while d < R:
        m = row < R - d
        b = jnp.where(m, a * pltpu.roll(b, R - d, 0) + b, b)
        a = jnp.where(m, a * pltpu.roll(a, R - d, 0), a)
        d *= 2
    return b


def _lru_fwd(hl, cw, cb, wa, ba, wx, bx, lam):
    T = hl.shape[0]
    R = RB_LRU
    W = LRU_W

    def body(h_ref, t_ref, cw_ref, cb_ref, wa_ref, ba_ref, wx_ref, bx_ref, lam_ref, o_ref, hs_ref, carry, ext):
        first = pl.program_id(0) == 0

        @pl.when(first)
        def _():
            carry[...] = jnp.zeros_like(carry)

        tail = jnp.where(first, 0.0, t_ref[:, 0:W])
        xc = _conv_fwd(ext, h_ref[:, 0:W], tail, cw_ref[...], R) + cb_ref[...]
        a, b = _lru_ab(xc, wa_ref[...], ba_ref[...], wx_ref[...], bx_ref[...], lam_ref[...])
        ap, hloc = _scan_fwd(a, b)
        h = hloc + ap * carry[0:1, :]
        carry[...] = jnp.broadcast_to(h[R - 1:R, :], carry.shape)
        hs_ref[...] = h
        o_ref[...] = _lru_out(h, h_ref[:, W:2 * W])

    vec = _full_spec((1, W))
    blk = pl.BlockSpec((R, W), lambda i: (i, 0))
    return pl.pallas_call(
        body, grid=(T // R,),
        in_specs=[pl.BlockSpec((R, 2 * W), lambda i: (i, 0)), _prev_tail_spec(R, 2 * W), _full_spec((4, W)), vec,
                  _full_spec((W, W)), vec, _full_spec((W, W)), vec, vec],
        out_specs=[blk, blk], out_shape=[jax.ShapeDtypeStruct((T, W), f32)] * 2,
        scratch_shapes=[pltpu.VMEM((8, W), f32), pltpu.VMEM((R + 8, W), f32)],
        compiler_params=_cparams(("arbitrary",)), name="lru_fwd")(hl, hl, cw, cb, wa, ba, wx, bx, lam)


def _lru_bwd(hl, hs, cw, cb, wa, ba, wx, bx, lam, dout):
    T = hl.shape[0]
    R = RB_LRU
    W = LRU_W
    nb = T // R

    def body(h_ref, t_ref, hs_ref, hst_ref, cw_ref, cb_ref, wa_ref, ba_ref, wx_ref, bx_ref, lam_ref, do_ref,
             dh_ref, dcw_ref, dcb_ref, dwa_ref, dba_ref, dwx_ref, dbx_ref, dlam_ref, carry_g, carry_dy, ext):
        i = pl.program_id(0)
        last_blk = i == 0
        first_blk = i == nb - 1

        @pl.when(last_blk)
        def _():
            carry_g[...] = jnp.zeros_like(carry_g)
            carry_dy[...] = jnp.zeros_like(carry_dy)
            for r in (dcw_ref, dcb_ref, dwa_ref, dba_ref, dwx_ref, dbx_ref, dlam_ref):
                r[...] = jnp.zeros_like(r)

        tail = jnp.where(first_blk, 0.0, t_ref[:, 0:W])
        xc = _conv_fwd(ext, h_ref[:, 0:W], tail, cw_ref[...], R) + cb_ref[...]
        (a, _), vjp_ab = jax.vjp(_lru_ab, xc, wa_ref[...], ba_ref[...], wx_ref[...], bx_ref[...], lam_ref[...])
        hs = hs_ref[...]
        _, vjp_out = jax.vjp(_lru_out, hs, h_ref[:, W:2 * W])
        dh, dgate = vjp_out(do_ref[...])
        row = lax.broadcasted_iota(jnp.int32, (R, W), 0)
        dh = jnp.where(row == R - 1, dh + carry_g[0:1, :], dh)
        a_up = jnp.where(row == R - 1, 0.0, pltpu.roll(a, R - 1, 0))
        g = _scan_bwd(a_up, dh)
        carry_g[...] = jnp.broadcast_to(a[0:1, :] * g[0:1, :], carry_g.shape)
        hprev0 = jnp.where(first_blk, 0.0, hst_ref[7:8, :])
        hprev = jnp.where(row == 0, hprev0, pltpu.roll(hs, 1, 0))
        dxc, dwa, dba, dwx, dbx, dlam = vjp_ab((g * hprev, g))
        dwa_ref[...] += dwa
        dba_ref[...] += dba
        dwx_ref[...] += dwx
        dbx_ref[...] += dbx
        dlam_ref[...] += dlam
        dcb_ref[...] += jnp.sum(dxc, 0, keepdims=True)
        dx, dcw = _conv_bwd(ext, dxc, carry_dy[...], cw_ref[...], R)
        carry_dy[...] = dxc[0:8, :]
        dcw_ref[...] += dcw
        dh_ref[:, 0:W] = dx
        dh_ref[:, W:2 * W] = dgate

    vec = _full_spec((1, W))
    mat = _full_spec((W, W))
    blk = pl.BlockSpec((R, W), lambda i: (nb - 1 - i, 0))
    blk2 = pl.BlockSpec((R, 2 * W), lambda i: (nb - 1 - i, 0))
    return pl.pallas_call(
        body, grid=(nb,),
        in_specs=[blk2, _prev_tail_spec_rev(R, 2 * W, nb), blk, _prev_tail_spec_rev(R, W, nb), _full_spec((4, W)), vec,
                  mat, vec, mat, vec, vec, blk],
        out_specs=[blk2, _full_spec((4, W)), vec, mat, vec, mat, vec, vec],
        out_shape=[jax.ShapeDtypeStruct((T, 2 * W), f32), jax.ShapeDtypeStruct((4, W), f32),
                   jax.ShapeDtypeStruct((1, W), f32), jax.ShapeDtypeStruct((W, W), f32),
                   jax.ShapeDtypeStruct((1, W), f32), jax.ShapeDtypeStruct((W, W), f32),
                   jax.ShapeDtypeStruct((1, W), f32), jax.ShapeDtypeStruct((1, W), f32)],
        scratch_shapes=[pltpu.VMEM((8, W), f32), pltpu.VMEM((8, W), f32), pltpu.VMEM((R + 8, W), f32)],
        compiler_params=_cparams(("arbitrary",)), name="lru_bwd")(hl, hl, hs, hs, cw, cb, wa, ba, wx, bx, lam, dout)


def _head_ones():
    i = lax.broadcasted_iota(jnp.int32, (GDN_W, GDN_W), 0)
    j = lax.broadcasted_iota(jnp.int32, (GDN_W, GDN_W), 1)
    return jnp.where(jnp.bitwise_xor(i, j) < CH, 1.0, 0.0).astype(bf16)


def _head_sums(x, ones):
    x1, x2, _ = _split3(x)
    return jnp.dot(x1, ones, preferred_element_type=f32) + jnp.dot(x2, ones, preferred_element_type=f32)


def _l2n(y, ones):
    r = lax.rsqrt(_head_sums(y * y, ones) + 1e-6)
    return y * r, r


def _l2n_bwd(dn, n, r, ones):
    return r * (dn - n * _head_sums(dn * n, ones))


def _gdn_local(inverse, q, k, vs, gc, bb):
    B = q.shape[0]
    ii = lax.broadcasted_iota(jnp.int32, (B, CH, CH), 1)
    jj = lax.broadcasted_iota(jnp.int32, (B, CH, CH), 2)
    gct = jnp.swapaxes(gc, 1, 2)
    decay = jnp.where(ii >= jj, jnp.exp(jnp.minimum(gc - gct, 0.0)), 0.0)
    kk = _bmm('bid,bjd->bij', k, k)
    inv = inverse(-jnp.where(ii > jj, bb * kk * decay, 0.0))
    egc = jnp.exp(gc)
    u = _bmm3('bij,bje->bie', inv, vs * bb)
    w = _bmm3('bij,bje->bie', inv, k * (bb * egc))
    qk = _bmm('bid,bjd->bij', q, k) * (0.125 * decay)
    glast = gc[:, CH - 1:CH, :]
    return u, w, qk, q * (0.125 * egc), k * jnp.exp(glast - gc), jnp.exp(jnp.broadcast_to(glast, gc.shape))


def _gdn_step(st, u, w, qk, qd, kt, egl, z, gn):
    vnew = u - _bmm('hcd,hde->hce', w, st)
    o = _bmm('hcd,hde->hce', qd, st) + _bmm('hij,hje->hie', qk, vnew)
    st2 = st * egl + _bmm('hcd,hce->hde', kt, vnew)
    out = o * lax.rsqrt(_rowsum(o * o) * (1.0 / CH) + 1e-6) * gn * _silu(z)
    return out, st2


def _gdn_scalars(ab, alog, dtb):
    sp = _softplus(ab + dtb)
    return -jnp.exp(alog) * sp, _sigmoid(ab)


def _bcast_heads(blk, lane0, H):
    R = blk.shape[0]
    n = R // CH
    parts = [jnp.broadcast_to(blk[:, lane0 + h:lane0 + h + 1], (R, CH)).reshape(n, CH, CH) for h in range(H)]
    return jnp.stack(parts, axis=1).reshape(n * H, CH, CH)


def _unbcast_heads(x, lane0, H):
    n = x.shape[0] // H
    R = n * CH
    s = jnp.sum(x, axis=2, keepdims=True).reshape(n, H, CH, 1)
    lane = lax.broadcasted_iota(jnp.int32, (R, 128), 1)
    acc = jnp.zeros((R, 128), f32)
    for h in range(H):
        acc = acc + jnp.where(lane == lane0 + h, jnp.broadcast_to(s[:, h].reshape(R, 1), (R, 128)), 0.0)
    return acc


def _gdn_fwd(hg, cw, alog, dtb, gn):
    T = hg.shape[0]
    R = RB_GDN
    nc = R // CH
    W3 = 3 * GDN_W
    H = GDN_H

    def body(h_ref, t_ref, cw_ref, al_ref, dt_ref, gn_ref, o_ref, st_ref, inv_ref, st, ext):
        first = pl.program_id(0) == 0

        @pl.when(first)
        def _():
            st[...] = jnp.zeros_like(st)

        def inverse(m):
            inv = _neumann_inv(m)
            inv_ref[...] = inv
            return inv

        tail = jnp.where(first, 0.0, t_ref[:, 0:W3])
        y = _silu(_conv_fwd(ext, h_ref[:, 0:W3], tail, cw_ref[...], R))
        ones = _head_ones()
        qs = _split_heads(_l2n(y[:, 0:GDN_W], ones)[0], H)
        ks = _split_heads(_l2n(y[:, GDN_W:2 * GDN_W], ones)[0], H)
        vs = _split_heads(y[:, 2 * GDN_W:W3], H)
        zh = _split_heads(h_ref[:, W3:W3 + GDN_W], H)
        g, beta = _gdn_scalars(h_ref[:, W3 + GDN_W:GDN_IN], al_ref[...], dt_ref[...])
        loc = _gdn_local(inverse, qs, ks, vs, _bcast_heads(_chunk_cumsum(g), 0, H), _bcast_heads(beta, H, H))
        gnv = gn_ref[...]
        outs = []
        s_cur = st[...]
        for c in range(nc):
            sl = slice(c * H, (c + 1) * H)
            st_ref[c] = s_cur
            out, s_cur = _gdn_step(s_cur, *(t[sl] for t in loc), zh[sl], gnv)
            outs.append(out)
        st[...] = s_cur
        _merge_heads(o_ref, jnp.concatenate(outs, axis=0), H)

    return pl.pallas_call(
        body, grid=(T // R,),
        in_specs=[pl.BlockSpec((R, GDN_IN), lambda i: (i, 0)), _prev_tail_spec(R, GDN_IN), _full_spec((4, W3)),
                  _full_spec((1, 128)), _full_spec((1, 128)), _full_spec((1, CH))],
        out_specs=[pl.BlockSpec((R, GDN_W), lambda i: (i, 0)), pl.BlockSpec((nc, H, CH, CH), lambda i: (i, 0, 0, 0)),
                   pl.BlockSpec((nc * H, CH, CH), lambda i: (i, 0, 0))],
        out_shape=[jax.ShapeDtypeStruct((T, GDN_W), f32), jax.ShapeDtypeStruct((T // CH, H, CH, CH), f32),
                   jax.ShapeDtypeStruct((T // CH * H, CH, CH), f32)],
        scratch_shapes=[pltpu.VMEM((H, CH, CH), f32), pltpu.VMEM((R + 8, W3), f32)],
        compiler_params=_cparams(("arbitrary",)), name="gdn_fwd")(hg, hg, cw, alog, dtb, gn)


def _gdn_bwd(hg, cw, alog, dtb, gn, states, invs, dout):
    T = hg.shape[0]
    R = RB_GDN
    nc = R // CH
    nb = T // R
    W3 = 3 * GDN_W
    H = GDN_H

    def body(h_ref, t_ref, cw_ref, al_ref, dt_ref, gn_ref, st_ref, inv_ref, do_ref,
             dh_ref, dcw_ref, dal_ref, ddt_ref, dgn_ref, dst, carry_dy, ext, wide):
        i = pl.program_id(0)
        first_blk = i == nb - 1

        @pl.when(i == 0)
        def _():
            dst[...] = jnp.zeros_like(dst)
            carry_dy[...] = jnp.zeros_like(carry_dy)
            for r in (dcw_ref, dal_ref, ddt_ref, dgn_ref):
                r[...] = jnp.zeros_like(r)

        tail = jnp.where(first_blk, 0.0, t_ref[:, 0:W3])
        ypre = _conv_fwd(ext, h_ref[:, 0:W3], tail, cw_ref[...], R)
        y = _silu(ypre)
        ones = _head_ones()
        qn, rq = _l2n(y[:, 0:GDN_W], ones)
        kn, rk = _l2n(y[:, GDN_W:2 * GDN_W], ones)
        qs, ks, vs = _split_heads(qn, H), _split_heads(kn, H), _split_heads(y[:, 2 * GDN_W:W3], H)
        zh = _split_heads(h_ref[:, W3:W3 + GDN_W], H)
        ab = h_ref[:, W3 + GDN_W:GDN_IN]
        alog, dtb = al_ref[...], dt_ref[...]
        g, beta = _gdn_scalars(ab, alog, dtb)
        kept = inv_ref[...]
        loc, vjp_loc = jax.vjp(functools.partial(_gdn_local, lambda m: _known_inv(m, kept)), qs, ks, vs,
                               _bcast_heads(_chunk_cumsum(g), 0, H), _bcast_heads(beta, H, H))
        doh = _split_heads(do_ref[...], H)
        gnv = gn_ref[...]
        dloc = [[None] * nc for _ in range(6)]
        dzs = [None] * nc
        ds = dst[...]
        dgn = jnp.zeros((1, CH), f32)
        for c in reversed(range(nc)):
            sl = slice(c * H, (c + 1) * H)
            _, vjp = jax.vjp(_gdn_step, st_ref[c], *(t[sl] for t in loc), zh[sl], gnv)
            grads = vjp((doh[sl], ds))
            ds = grads[0]
            for j in range(6):
                dloc[j][c] = grads[1 + j]
            dzs[c] = grads[7]
            dgn = dgn + grads[8]
        dst[...] = ds
        dgn_ref[...] += dgn
        dqs, dks, dvs, dgb, dbb = vjp_loc(tuple(jnp.concatenate(d, axis=0) for d in dloc))
        lane = lax.broadcasted_iota(jnp.int32, (R, 128), 1)
        dg = _chunk_cumsum(_unbcast_heads(dgb, 0, H), reverse=True)
        dbeta = _unbcast_heads(dbb, H, H)
        da = dg * (-jnp.exp(alog)) * _sigmoid(ab + dtb)
        dh_ref[:, W3 + GDN_W:GDN_IN] = jnp.where(lane < H, da, dbeta * beta * (1.0 - beta))
        ddt_ref[...] += jnp.sum(jnp.where(lane < H, da, 0.0), 0, keepdims=True)
        dal_ref[...] += jnp.sum(jnp.where(lane < H, dg * g, 0.0), 0, keepdims=True)
        _merge_heads(dh_ref, jnp.concatenate(dzs, axis=0), H, col0=W3)
        for j, dpart in enumerate((dqs, dks, dvs)):
            _merge_heads(wide, dpart, H, col0=j * GDN_W)
        wide[:, 0:GDN_W] = _l2n_bwd(wide[:, 0:GDN_W], qn, rq, ones)
        wide[:, GDN_W:2 * GDN_W] = _l2n_bwd(wide[:, GDN_W:2 * GDN_W], kn, rk, ones)
        dy = wide[...] * _dsilu(ypre)
        dx, dcw = _conv_bwd(ext, dy, carry_dy[...], cw_ref[...], R)
        carry_dy[...] = dy[0:8, :]
        dcw_ref[...] += dcw
        dh_ref[:, 0:W3] = dx

    blk = pl.BlockSpec((R, GDN_IN), lambda i: (nb - 1 - i, 0))
    return pl.pallas_call(
        body, grid=(nb,),
        in_specs=[blk, _prev_tail_spec_rev(R, GDN_IN, nb), _full_spec((4, W3)), _full_spec((1, 128)),
                  _full_spec((1, 128)), _full_spec((1, CH)),
                  pl.BlockSpec((nc, H, CH, CH), lambda i: (nb - 1 - i, 0, 0, 0)),
                  pl.BlockSpec((nc * H, CH, CH), lambda i: (nb - 1 - i, 0, 0)),
                  pl.BlockSpec((R, GDN_W), lambda i: (nb - 1 - i, 0))],
        out_specs=[blk, _full_spec((4, W3)), _full_spec((1, 128)), _full_spec((1, 128)), _full_spec((1, CH))],
        out_shape=[jax.ShapeDtypeStruct((T, GDN_IN), f32), jax.ShapeDtypeStruct((4, W3), f32),
                   jax.ShapeDtypeStruct((1, 128), f32), jax.ShapeDtypeStruct((1, 128), f32),
                   jax.ShapeDtypeStruct((1, CH), f32)],
        scratch_shapes=[pltpu.VMEM((H, CH, CH), f32), pltpu.VMEM((8, W3), f32), pltpu.VMEM((R + 8, W3), f32),
                        pltpu.VMEM((R, W3), f32)],
        compiler_params=_cparams(("arbitrary",)), name="gdn_bwd")(hg, hg, cw, alog, dtb, gn, states, invs, dout)


def _block_diag(w):
    out = jnp.zeros((LRU_W, LRU_W), w.dtype)
    for g in range(w.shape[0]):
        out = lax.dynamic_update_slice(out, w[g], (g * CH, g * CH))
    return out


def _block_diag_t(w):
    return jnp.stack([w[g * CH:(g + 1) * CH, g * CH:(g + 1) * CH] for g in range(LRU_W // CH)])


def _pad_lanes(v, n=128):
    return jnp.pad(v, (0, n - v.shape[0]))[None, :]


def _local_step(x, p, positions, target, fetch, emit, sm):
    cosw, sinw = _rope_tables(positions)
    cols = lambda w: jnp.transpose(w, (1, 2, 0, 3)).reshape(-1, D, NDEV * FSP)
    saved = []
    h = x
    for l in range(DEPTH):
        v = lambda n: sm[n][l][None, :]
        F1, tok = fetch(l, 'f1', h)
        p384a, pda = cols(F1['p384']), F1['pd']
        z1, x1, g1, u1 = _ffn_fwd(h, p384a, pda, v('ln_ffn1_g') + tok, v('ln_ffn1_b'), 0, 0)
        G, _ = fetch(l, 'rest', x1)
        p384, pd, pr, pinl, ping, wpp = cols(G['p384']), G['pd'], G['pr'], G['pinl'], G['ping'], G['wpp']
        wts = (p384a, pda, p384, pd, pr, pinl, ping, wpp)
        hr, hl, hg = _proj_in(x1, pr, pinl, ping, 0)
        o_r, rst = _ret_fwd(hr, cosw, sinw, v('ret_norm_g'))
        lru_args = (sm['lru_conv_w'][l], v('lru_conv_b'), _block_diag(sm['lru_w_a'][l]), v('lru_b_a'),
                    _block_diag(sm['lru_w_x'][l]), v('lru_b_x'), v('lru_lambda'))
        o_l, hs = _lru_fwd(hl, *lru_args)
        gdn_args = (sm['gdn_conv_w'][l], _pad_lanes(sm['gdn_a_log'][l]), _pad_lanes(sm['gdn_dt_bias'][l]),
                    v('gdn_norm_g'))
        o_g, *gst = _gdn_fwd(hg, *gdn_args)
        z2, x2 = _mix_out(x1, o_r, o_l, o_g, pr, v('ln_mix_g'), v('ln_mix_b'), 0)
        z3, x3, g2, u2 = _ffn_fwd(x2, p384, pd, v('ln_ffn2_g'), v('ln_ffn2_b'), 0, 1, ple=(p[l], pr, wpp))
        saved.append((h, z1, x1, hr, hl, hg, o_r, rst, o_l, hs, lru_args, o_g, gst, gdn_args, z2, x2, z3,
                      g1, u1, g2, u2, wts))
        h = x3
    d, loss = _loss_grad(h, target)

    small = {n: [None] * DEPTH for n in SMALL}
    tok = 0.0
    for l in reversed(range(DEPTH)):
        (x0, z1, x1, hr, hl, hg, o_r, rst, o_l, hs, lru_args, o_g, gst, gdn_args, z2, x2, z3,
         g1, u1, g2, u2, wts) = saved[l]
        p384a, pda, p384, pd, pr, pinl, ping, wpp = wts
        v = lambda n: sm[n][l][None, :]
        rows = lambda m: m.reshape(NDEV, m.shape[1] // NDEV, m.shape[2])
        d2, dg2, du2, a2, dy2, small['ln_ffn2_g'][l], small['ln_ffn2_b'][l] = _ffn_bwd(
            z3, d, g2, u2, p384, pd, v('ln_ffn2_g') + tok, 0, 1)
        d2, dgp, dpj = _ple_bwd(x2, p[l], dy2, d2, pr, wpp, 0)
        dxb, dzb, do_r, do_l, do_g, small['ln_mix_g'][l], small['ln_mix_b'][l] = _mix_out_bwd(
            z2, d2, pr, v('ln_mix_g'), 0)
        dhr, small['ret_norm_g'][l] = _ret_bwd(hr, cosw, sinw, v('ret_norm_g'), rst, do_r)
        (dhl, small['lru_conv_w'][l], small['lru_conv_b'][l], dwa, small['lru_b_a'][l], dwx, small['lru_b_x'][l],
         small['lru_lambda'][l]) = _lru_bwd(hl, hs, *lru_args, do_l)
        small['lru_w_a'][l], small['lru_w_x'][l] = _block_diag_t(dwa), _block_diag_t(dwx)
        dhg, small['gdn_conv_w'][l], dal, ddt, small['gdn_norm_g'][l] = _gdn_bwd(hg, *gdn_args, *gst, do_g)
        small['gdn_a_log'][l], small['gdn_dt_bias'][l] = dal[:, 0:GDN_H], ddt[:, 0:GDN_H]
        d1 = _proj_in_bwd(dxb, dhr, dhl, dhg, pr, pinl, ping, 0)
        dwo = jnp.concatenate([_matmul_tn(o_r, dzb, D, "dw_out_r"), _matmul_tn(o_l, dzb, D, "dw_out_l"),
                               _matmul_tn(o_g, dzb, D, "dw_out_g")], axis=1)
        tok = emit(l, 'rest', {
            'p384': jnp.stack([_matmul_tn(x2, dg2, FSP, "dw_gate", NDEV), _matmul_tn(x2, du2, FSP, "dw_up", NDEV)],
                              axis=1),
            'pd': rows(_matmul_tn(a2, dy2, D, "dw_down"))[:, None],
            'pr': jnp.stack([rows(_matmul_tn(x1, dhr, D, "dw_in_r")), rows(dwo),
                             rows(_matmul_tn(x2, dgp, D, "dw_ple_gate"))], axis=1),
            'pinl': rows(_matmul_tn(x1, dhl, 2 * LRU_W, "dw_in_l"))[:, None],
            'ping': rows(_matmul_tn(x1, dhg, GDN_IN, "dw_in_g"))[:, None],
            'ppp': jnp.transpose(_matmul_tn(p[l], dpj, D, "dw_ple_proj").reshape(PLE, NDEV, 128), (1, 0, 2))[:, None]})
        d, dg1, du1, a1, dy1, small['ln_ffn1_g'][l], small['ln_ffn1_b'][l] = _ffn_bwd(
            z1, d1, g1, u1, p384a, pda, v('ln_ffn1_g') + tok, 0, 0)
        if l == 0:
            emit(l, 'small', {n: jnp.stack([g.reshape(sm[n].shape[1:]) for g in gs]) for n, gs in small.items()})
        tok = emit(l, 'f1', {
            'p384': jnp.stack([_matmul_tn(x0, dg1, FSP, "dw_gate", NDEV), _matmul_tn(x0, du1, FSP, "dw_up", NDEV)],
                              axis=1),
            'pd': rows(_matmul_tn(a1, dy1, D, "dw_down"))[:, None]})
    return loss, d


def _pack_big(ws, dtype=bf16):
    padc = lambda a, n: jnp.pad(a, ((0, 0), (0, 0), (0, n - a.shape[2])))
    padr = lambda a, n: jnp.pad(a, ((0, 0), (0, n - a.shape[1]), (0, 0)))
    per_layer = lambda arrs: jnp.stack(arrs, axis=1).reshape((-1,) + arrs[0].shape[1:])
    w_in = ws['w_in']
    out = {
        'p384': per_layer([padc(ws[n], FSP) for n in ('ffn1_w_gate', 'ffn1_w_up', 'ffn2_w_gate', 'ffn2_w_up')]),
        'pd': per_layer([padr(ws[n], FSP) for n in ('ffn1_w_down', 'ffn2_w_down')]),
        'pr': per_layer([w_in[:, :, 0:D], ws['w_out'], ws['ple_w_gate']]),
        'pinl': w_in[:, :, D:D + 2 * LRU_W],
        'ping': padc(w_in[:, :, D + 2 * LRU_W:D_IN], GDN_IN),
        'ppp': ws['ple_w_proj'],
    }
    return {k: a.astype(dtype) for k, a in out.items()}


def _gather_two_level(arrays, name):
    n = len(arrays)

    def body(*refs):
        ins, outs = refs[:n], refs[n:2 * n]
        send_sems, recv_sems, local_sems = refs[2 * n:]
        x, y, c = lax.axis_index("x"), lax.axis_index("y"), lax.axis_index("c")
        me, sibling = (x, y, c), (x, y, 1 - c)
        chips = [(1 - x, y), (x, 1 - y), (1 - x, 1 - y)]
        slot = lambda d: 4 * d[0] + 2 * d[1] + d[2]

        def copy(i, k, block, to, src=None):
            return pltpu.make_async_remote_copy(
                src_ref=outs[i].at[slot(block)] if src is None else src, dst_ref=outs[i].at[slot(block)],
                send_sem=send_sems.at[i, k], recv_sem=recv_sems.at[i, k], device_id=to,
                device_id_type=pl.DeviceIdType.MESH)

        mine, first, passed = [], [], []
        for i in range(n):
            cp = pltpu.make_async_copy(ins[i], outs[i].at[slot(me)], local_sems.at[i])
            cp.start()
            mine.append(cp)
            first.append(copy(i, 0, me, sibling, src=ins[i]))
            first += [copy(i, 1 + j, me, (*chip, c), src=ins[i]) for j, chip in enumerate(chips)]
        for cp in first:
            cp.start()
        for i in range(n):
            for j, chip in enumerate(chips):
                copy(i, 1 + j, (*chip, c), me).wait_recv()
                cp = copy(i, 4 + j, (*chip, c), sibling)
                cp.start()
                passed.append(cp)
        for i in range(n):
            copy(i, 0, sibling, me).wait_recv()
            for j, chip in enumerate(chips):
                copy(i, 4 + j, (*chip, 1 - c), me).wait_recv()
        for cp in first + passed:
            cp.wait_send()
        for cp in mine:
            cp.wait()

    hbm = pl.BlockSpec(memory_space=pltpu.HBM)
    return pl.pallas_call(
        body, in_specs=[hbm] * n, out_specs=[hbm] * n,
        out_shape=[jax.ShapeDtypeStruct((NDEV,) + a.shape, a.dtype) for a in arrays],
        scratch_shapes=[pltpu.SemaphoreType.DMA((n, NDEV - 1)), pltpu.SemaphoreType.DMA((n, NDEV - 1)),
                        pltpu.SemaphoreType.DMA((n,))],
        compiler_params=pltpu.CompilerParams(has_side_effects=True), name=name)(*arrays)


def _scatter_pairs(arrays, name):
    n = len(arrays)

    def body(*refs):
        ins, gots = refs[:n], refs[n:2 * n]
        send_sems, recv_sems = refs[2 * n:]
        x, y, c = lax.axis_index("x"), lax.axis_index("y"), lax.axis_index("c")
        sends = []
        for i in range(n):
            for q in range(4):
                cp = pltpu.make_async_remote_copy(
                    src_ref=ins[i].at[2 * q + 1 - c], dst_ref=gots[i].at[q], send_sem=send_sems.at[i, q],
                    recv_sem=recv_sems.at[i, q], device_id=(x, y, 1 - c), device_id_type=pl.DeviceIdType.MESH)
                cp.start()
                sends.append(cp)
        for cp in sends:
            cp.wait_recv()
        for cp in sends:
            cp.wait_send()

    hbm = pl.BlockSpec(memory_space=pltpu.HBM)
    return pl.pallas_call(
        body, in_specs=[hbm] * n, out_specs=[hbm] * n,
        out_shape=[jax.ShapeDtypeStruct((4,) + a.shape[1:], a.dtype) for a in arrays],
        scratch_shapes=[pltpu.SemaphoreType.DMA((n, 4)), pltpu.SemaphoreType.DMA((n, 4))],
        compiler_params=pltpu.CompilerParams(has_side_effects=True), name=name)(*arrays)


def _pair_sum(own, got, name):
    def body(a_ref, b_ref, o_ref):
        o_ref[...] = (a_ref[...].astype(f32) + b_ref[...].astype(f32)).astype(bf16)

    spec = pl.BlockSpec((None, None) + own.shape[2:], lambda q, s: (q, s, 0, 0))
    return pl.pallas_call(
        body, grid=own.shape[:2], in_specs=[spec, spec], out_specs=spec,
        out_shape=jax.ShapeDtypeStruct(own.shape, bf16),
        compiler_params=_cparams(("arbitrary", "arbitrary")), name=name)(own, got)


def _gather_plan(srcs, lands, x, y, c):
    me = 4 * x + 2 * y + c
    sends, arrivals = [], []
    for j in range(1, NDEV):
        peer, source = (me + j) % NDEV, (me + NDEV - j) % NDEV
        for i in range(len(srcs)):
            k = i * (NDEV - 1) + j - 1
            sends.append((srcs[i], lands[i].at[me], (peer // 4, (peer // 2) % 2, peer % 2), k))
            arrivals.append((srcs[i], lands[i].at[source], (source // 4, (source // 2) % 2, source % 2), k))
    return sends, arrivals


def _chips_plan(srcs, lands, x, y, c):
    chip = 2 * x + y
    sends, arrivals = [], []
    for j in range(1, 4):
        peer, source = (chip + j) % 4, (chip + 4 - j) % 4
        for i in range(len(srcs)):
            k = i * 3 + j - 1
            sends.append((srcs[i].at[peer], lands[i].at[chip], (peer // 2, peer % 2, c), k))
            arrivals.append((srcs[i].at[chip], lands[i].at[source], (source // 2, source % 2, c), k))
    return sends, arrivals


def _remote(entry, send_sems, recv_sems):
    src, dst, dev, k = entry
    return pltpu.make_async_remote_copy(src_ref=src, dst_ref=dst, send_sem=send_sems.at[k], recv_sem=recv_sems.at[k],
                                        device_id=dev, device_id_type=pl.DeviceIdType.MESH)


_HBM = pl.BlockSpec(memory_space=pltpu.HBM)
_SEM = pl.BlockSpec(memory_space=pltpu.SEMAPHORE)


def _split_start(arrays, land_shapes, plan, npeer, name):
    n = len(arrays)

    def body(*refs):
        srcs, lands = refs[:n], refs[n:2 * n]
        send_sems, recv_sems, token = refs[2 * n], refs[2 * n + 1], refs[-1]
        sends, _ = plan(srcs, lands, lax.axis_index("x"), lax.axis_index("y"), lax.axis_index("c"))
        for entry in sends:
            _remote(entry, send_sems, recv_sems).start()
        token[...] = jnp.zeros_like(token)

    lands = [lax.empty(s, a.dtype) for s, a in zip(land_shapes, arrays)]
    thru = [pltpu.HBM(a.shape, a.dtype) for a in arrays + lands]
    out = pl.pallas_call(
        body, name=name, in_specs=[_HBM] * (2 * n),
        out_specs=(_SEM, _SEM, *([_HBM] * (2 * n)), pl.BlockSpec(memory_space=pltpu.VMEM)),
        out_shape=(pltpu.SemaphoreType.DMA((n * npeer,)), pltpu.SemaphoreType.DMA((n * npeer,)), *thru,
                   jax.ShapeDtypeStruct((8, 128), f32)),
        input_output_aliases={i: 2 + i for i in range(2 * n)},
        compiler_params=pltpu.CompilerParams(has_side_effects=pltpu.SideEffectType.DATAFLOW_SIDE_EFFECTING),
    )(*[pltpu.with_memory_space_constraint(a, pltpu.HBM) for a in arrays + lands])
    return out[0], out[1], list(out[2:2 + n]), list(out[2 + n:2 + 2 * n]), out[-1]


def _split_wait(send_sems, recv_sems, srcs, lands, after, plan, name):
    n = len(srcs)

    def body(*refs):
        s_refs, l_refs = refs[:n], refs[n:2 * n]
        ssem, rsem = refs[2 * n], refs[2 * n + 1]
        sends, arrivals = plan(s_refs, l_refs, lax.axis_index("x"), lax.axis_index("y"), lax.axis_index("c"))
        for entry in sends:
            _remote(entry, ssem, rsem).wait_send()
        for entry in arrivals:
            _remote(entry, ssem, rsem).wait_recv()

    out = pl.pallas_call(
        body, name=name, in_specs=[_HBM] * (2 * n) + [_SEM, _SEM, pl.BlockSpec(memory_space=pl.ANY)],
        out_specs=[_HBM] * (2 * n), out_shape=[pltpu.HBM(a.shape, a.dtype) for a in srcs + lands],
        input_output_aliases={i: i for i in range(2 * n)},
        compiler_params=pltpu.CompilerParams(has_side_effects=pltpu.SideEffectType.DATAFLOW_SIDE_EFFECTING),
    )(*srcs, *lands, send_sems, recv_sems, after)
    return list(out[:n]), list(out[n:])


def _adam_math(w, g, m, v):
    m2 = ADAM_B1 * m + (1.0 - ADAM_B1) * g
    v2 = ADAM_B2 * v + (1.0 - ADAM_B2) * (g * g)
    m_hat = m2 / (1.0 - ADAM_B1 ** ADAM_STEP)
    v_hat = v2 / (1.0 - ADAM_B2 ** ADAM_STEP)
    return -ADAM_LR * (m_hat / (jnp.sqrt(v_hat) + ADAM_EPS) + ADAM_WD * w), m2, v2


def _adam_big(parts, w, m, v, anchor, name):
    L, rows, cols = w.shape
    flat = [(a, slot) for layer_parts in parts for a, slot in layer_parts]
    per = len(parts[0])

    def body(*refs):
        prefs = refs[:len(flat)]
        w_ref, m_ref, v_ref, _, g_ref, d_ref, m2_ref, v2_ref = refs[len(flat):]
        for li in range(L):
            @pl.when(pl.program_id(0) == li)
            def _():
                c0 = 0
                for pref in prefs[li * per:(li + 1) * per]:
                    acc = pref[0].astype(f32)
                    for s in range(1, pref.shape[0]):
                        acc = acc + pref[s].astype(f32)
                    width = min(acc.shape[1], cols - c0)
                    g_ref[:, c0:c0 + width] = acc[0:rows, 0:width]
                    c0 += width

        d, m2, v2 = _adam_math(w_ref[...], g_ref[...], m_ref[...], v_ref[...])
        d_ref[...] = d
        m2_ref[...] = m2
        v2_ref[...] = v2

    wspec = pl.BlockSpec((None, rows, cols), lambda l: (l, 0, 0))
    in_specs = [pl.BlockSpec((a.shape[0], None) + a.shape[2:], functools.partial(lambda l, slot: (0, slot, 0, 0), slot=slot))
                for a, slot in flat]
    return pl.pallas_call(
        body, grid=(L,), in_specs=in_specs + [wspec] * 3 + [_full_spec((8, 128))], out_specs=[wspec] * 4,
        out_shape=[jax.ShapeDtypeStruct(w.shape, f32)] * 4,
        compiler_params=_cparams(("arbitrary",)), name=name)(*[a for a, _ in flat], w, m, v, anchor)


def _sum_sources(stacked):
    rows = stacked.shape[1]

    def body(s_ref, o_ref):
        acc = s_ref[0]
        for s in range(1, NDEV):
            acc = acc + s_ref[s]
        o_ref[...] = acc

    return pl.pallas_call(body, out_shape=jax.ShapeDtypeStruct((rows, 128), f32), name="sum_small_grads")(stacked)


def _adam_small(w, g, m, v):
    def body(w_ref, g_ref, m_ref, v_ref, d_ref, m2_ref, v2_ref):
        d, m2, v2 = _adam_math(w_ref[...], g_ref[...], m_ref[...], v_ref[...])
        d_ref[...] = d
        m2_ref[...] = m2
        v2_ref[...] = v2

    return pl.pallas_call(body, out_shape=[jax.ShapeDtypeStruct(w.shape, f32)] * 3, name="adam_small")(w, g, m, v)


def _pack_rows(arrs):
    flat = []
    for a in arrs:
        a = a.reshape(-1)
        flat.append(jnp.pad(a, (0, (-a.shape[0]) % 1024)))
    return jnp.concatenate(flat).reshape(-1, 128)


def _unpack_rows(packed, shapes):
    out, off = [], 0
    flat = packed.reshape(-1)
    for s in shapes:
        n = math.prod(s)
        out.append(flat[off:off + n].reshape(s))
        off += n + (-n) % 1024
    return out


def _gather_conv(gathered, shape):
    L, K, c = shape
    return jnp.transpose(gathered, (1, 2, 0, 3)).reshape(L, K, NDEV * c)


def kernel(x, p, positions, ln_ffn1_g, ln_ffn1_b, ffn1_w_gate, ffn1_w_up, ffn1_w_down, w_in, ret_norm_g, lru_conv_w, lru_conv_b, lru_w_a, lru_b_a, lru_w_x, lru_b_x, lru_lambda, gdn_conv_w, gdn_a_log, gdn_dt_bias, gdn_norm_g, w_out, ln_mix_g, ln_mix_b, ffn2_w_gate, ffn2_w_up, ffn2_w_down, ple_w_gate, ple_w_proj, ln_ffn2_g, ln_ffn2_b, loss_target, m_ln_ffn1_g, m_ln_ffn1_b, m_ffn1_w_gate, m_ffn1_w_up, m_ffn1_w_down, m_w_in, m_ret_norm_g, m_lru_conv_w, m_lru_conv_b, m_lru_w_a, m_lru_b_a, m_lru_w_x, m_lru_b_x, m_lru_lambda, m_gdn_conv_w, m_gdn_a_log, m_gdn_dt_bias, m_gdn_norm_g, m_w_out, m_ln_mix_g, m_ln_mix_b, m_ffn2_w_gate, m_ffn2_w_up, m_ffn2_w_down, m_ple_w_gate, m_ple_w_proj, m_ln_ffn2_g, m_ln_ffn2_b, v_ln_ffn1_g, v_ln_ffn1_b, v_ffn1_w_gate, v_ffn1_w_up, v_ffn1_w_down, v_w_in, v_ret_norm_g, v_lru_conv_w, v_lru_conv_b, v_lru_w_a, v_lru_b_a, v_lru_w_x, v_lru_b_x, v_lru_lambda, v_gdn_conv_w, v_gdn_a_log, v_gdn_dt_bias, v_gdn_norm_g, v_w_out, v_ln_mix_g, v_ln_mix_b, v_ffn2_w_gate, v_ffn2_w_up, v_ffn2_w_down, v_ple_w_gate, v_ple_w_proj, v_ln_ffn2_g, v_ln_ffn2_b):
    args = locals()
    W = {n: args[n] for n in WEIGHTS}
    M = {n: args['m_' + n] for n in WEIGHTS}
    V = {n: args['v_' + n] for n in WEIGHTS}
    me = 4 * lax.axis_index("x") + 2 * lax.axis_index("y") + lax.axis_index("c")

    core = lax.axis_index("c")
    chip = 2 * lax.axis_index("x") + lax.axis_index("y")

    packed = _pack_big(W)

    def group(l, name):
        per = {k: packed[k].shape[0] // DEPTH for k in PACKS}
        if name == 'f1':
            return [packed['p384'][l * per['p384']:l * per['p384'] + 2], packed['pd'][l * per['pd']:l * per['pd'] + 1]]
        return [packed['p384'][l * per['p384'] + 2:(l + 1) * per['p384']],
                packed['pd'][l * per['pd'] + 1:(l + 1) * per['pd']]] + [
                    packed[k][l * per[k]:(l + 1) * per[k]] for k in PACKS[2:]]

    def as_weights(arrs):
        G = dict(zip(PACKS, arrs))
        if 'ppp' in G:
            G['wpp'] = jnp.transpose(G.pop('ppp'), (1, 2, 0, 3)).reshape(PLE, D)
        return G

    conv_pack = _pack_rows([W[n] for n in CONV_SHARDED])
    g0 = _gather_two_level(group(0, 'f1') + [conv_pack], "gather_weights")
    g0, rest0 = lax.optimization_barrier((g0, group(0, 'rest')))
    start0 = _split_start(rest0, [(NDEV,) + a.shape for a in rest0], _gather_plan, NDEV - 1, "gather_start_0")
    tok0, all1 = lax.optimization_barrier((start0[4], group(1, 'f1') + group(1, 'rest')))
    start1 = _split_start(all1, [(NDEV,) + a.shape for a in all1], _gather_plan, NDEV - 1, "gather_start_1")
    arrived = {}

    def gather_done(started, after, name):
        srcs, lands = _split_wait(started[0], started[1], started[2], started[3], after, _gather_plan, name)
        return [lax.dynamic_update_slice_in_dim(ld, s[None], me, axis=0) for s, ld in zip(srcs, lands)]

    def fetch(l, name, after):
        if l == 0 and name == 'f1':
            return as_weights(g0[:-1]), tok0[0, 0] + start1[4][0, 0]
        if l == 0:
            return as_weights(gather_done(start0, after, "gather_wait_0")), 0.0
        if name == 'f1':
            arrived[1] = gather_done(start1, after, "gather_wait_1")
            return as_weights(arrived[1][:2]), 0.0
        return as_weights(arrived[1][2:]), 0.0

    conv_all = g0[-1]
    sm = {n: W[n] for n in SMALL}
    conv_shards = [_unpack_rows(conv_all[s], [W[n].shape for n in CONV_SHARDED]) for s in range(NDEV)]
    for i, n in enumerate(CONV_SHARDED):
        sm[n] = _gather_conv(jnp.stack([cs[i] for cs in conv_shards]), W[n].shape)

    received, started = {}, {}

    small_shapes = [sm[n].shape for n in SMALL]
    small_started = []

    def emit(l, name, grads):
        if name == 'small':
            pack = _pack_rows([grads[n] for n in SMALL])
            small_started.extend(_split_start([pack], [(NDEV,) + pack.shape], _gather_plan, NDEV - 1, "small_start"))
            return 0.0
        keys = list(grads)
        arrs = [grads[k] for k in keys]
        gots = _scatter_pairs(arrs, "scatter_pairs")
        owns = [lax.dynamic_index_in_dim(a.reshape((4, 2) + a.shape[1:]), core, axis=1, keepdims=False) for a in arrs]
        pair = [_pair_sum(o, g, "pair_sum_" + k) for k, o, g in zip(keys, owns, gots)]
        started[l, name] = (keys, _split_start(pair, [a.shape for a in pair], _chips_plan, 3,
                                               f"scatter_start_{l}_{name}"))
        return started[l, name][1][4][0, 0]

    def scatter_done(l, name, after):
        keys, st = started[l, name]
        srcs, lands = _split_wait(st[0], st[1], st[2], st[3], after, _chips_plan, f"scatter_wait_{l}_{name}")
        received[l, name] = dict(zip(keys, [
            lax.dynamic_update_slice_in_dim(ld, lax.dynamic_index_in_dim(s, chip, axis=0), chip, axis=0)
            for s, ld in zip(srcs, lands)]))

    loss, grad_x = _local_step(x[0], p[:, 0], positions.reshape(-1, 1), loss_target[0], fetch, emit, sm)
    loss = lax.psum(loss[0, 0], ("x", "y", "c"))
    last = (0, 'f1')
    for l, name in started:
        if (l, name) != last:
            scatter_done(l, name, grad_x)

    anchor = started[last][1][4]
    srcs, lands = _split_wait(small_started[0], small_started[1], small_started[2], small_started[3], anchor,
                              _gather_plan, "small_wait")
    small_all = lax.dynamic_update_slice_in_dim(lands[0], srcs[0][None], me, axis=0)
    small_sum = _unpack_rows(_sum_sources(small_all), small_shapes)
    grads, delta, new_m, new_v = {}, {}, {}, {}
    for n, g in zip(SMALL, small_sum):
        if n in CONV_SHARDED:
            c = W[n].shape[2]
            g = lax.dynamic_slice_in_dim(g, me * c, c, axis=2)
        grads[n] = g

    big_parts = {
        'ffn1_w_gate': [('f1', 'p384', 0)], 'ffn1_w_up': [('f1', 'p384', 1)], 'ffn1_w_down': [('f1', 'pd', 0)],
        'ffn2_w_gate': [('rest', 'p384', 0)], 'ffn2_w_up': [('rest', 'p384', 1)], 'ffn2_w_down': [('rest', 'pd', 0)],
        'w_in': [('rest', 'pr', 0), ('rest', 'pinl', 0), ('rest', 'ping', 0)], 'w_out': [('rest', 'pr', 1)],
        'ple_w_gate': [('rest', 'pr', 2)], 'ple_w_proj': [('rest', 'ppp', 0)],
    }
    def adam(n):
        parts = [[(received[l, grp][k], slot) for grp, k, slot in big_parts[n]] for l in range(DEPTH)]
        grads[n], delta[n], new_m[n], new_v[n] = _adam_big(parts, W[n], M[n], V[n], anchor, "adam_" + n)

    shapes = [W[n].shape for n in SMALL]
    d_s, m_s, v_s = _adam_small(*[_pack_rows([src[n] for n in SMALL]) for src in (W, grads, M, V)])
    for n, dd, mm, vv in zip(SMALL, _unpack_rows(d_s, shapes), _unpack_rows(m_s, shapes), _unpack_rows(v_s, shapes)):
        delta[n], new_m[n], new_v[n] = dd, mm, vv
    waits_last = [n for n in BIG if big_parts[n][0][0] == last[1]]
    for n in BIG:
        if n not in waits_last:
            adam(n)
    done = jnp.stack([d_s[0, 0]] + [delta[n][0, 0, 0] for n in BIG if n not in waits_last])
    scatter_done(*last, done)
    for n in waits_last:
        adam(n)

    return (loss, grad_x[None], *[grads[n] for n in WEIGHTS], *[delta[n] for n in WEIGHTS],
            *[new_m[n] for n in WEIGHTS], *[new_v[n] for n in WEIGHTS])
```

```python
import functools
import math

import jax
import jax.numpy as jnp
from jax import lax
from jax.experimental import pallas as pl
from jax.experimental.pallas import tpu as pltpu

f32 = jnp.float32
bf16 = jnp.bfloat16

NDEV = 8
DEPTH = 2
D = 1024
FS = 352
FSP = 384
FB = 2
NF = NDEV // FB
PLE = 256
CH = 64
RET_H, GDN_H = 4, 6
RET_W, LRU_W, GDN_W = 256, 384, 384
GDN_IN = 1664
GDN_IN_REAL = 1548
D_IN = 3340
ALPHA = 4.0 ** 0.25
LN_EPS = 1e-5
ROPE_THETA = 10000.0
TM = 512
RB_RET, RB_LRU, RB_GDN = 512, 512, 256
VMEM_LIMIT = 56 * 1024 * 1024
ADAM_LR, ADAM_B1, ADAM_B2, ADAM_EPS, ADAM_WD, ADAM_STEP = 0.001, 0.9, 0.999, 1e-08, 0.01, 10

WEIGHTS = ['ln_ffn1_g', 'ln_ffn1_b', 'ffn1_w_gate', 'ffn1_w_up', 'ffn1_w_down', 'w_in', 'ret_norm_g', 'lru_conv_w',
           'lru_conv_b', 'lru_w_a', 'lru_b_a', 'lru_w_x', 'lru_b_x', 'lru_lambda', 'gdn_conv_w', 'gdn_a_log',
           'gdn_dt_bias', 'gdn_norm_g', 'w_out', 'ln_mix_g', 'ln_mix_b', 'ffn2_w_gate', 'ffn2_w_up', 'ffn2_w_down',
           'ple_w_gate', 'ple_w_proj', 'ln_ffn2_g', 'ln_ffn2_b']
BIG = ['ffn1_w_gate', 'ffn1_w_up', 'ffn1_w_down', 'w_in', 'w_out', 'ffn2_w_gate', 'ffn2_w_up', 'ffn2_w_down',
       'ple_w_gate', 'ple_w_proj']
SMALL = [n for n in WEIGHTS if n not in BIG]
PACKS = ('p384', 'pd', 'pr', 'pinl', 'ping', 'ppp')
CONV_SHARDED = {'lru_conv_w': LRU_W, 'gdn_conv_w': 3 * GDN_W}


def _cparams(sem=None):
    return pltpu.CompilerParams(dimension_semantics=sem, vmem_limit_bytes=VMEM_LIMIT)


def _sigmoid(x):
    return 1.0 / (1.0 + jnp.exp(-x))


def _silu(x):
    return x * _sigmoid(x)


def _dsilu(x):
    s = _sigmoid(x)
    return s * (1.0 + x * (1.0 - s))


def _softplus(x):
    return jnp.maximum(x, 0.0) + jnp.log(1.0 + jnp.exp(-jnp.abs(x)))


def _gelu(x):
    return 0.5 * x * (1.0 + jnp.tanh(0.7978845608028654 * (x + 0.044715 * x * x * x)))


def _dot(a, b):
    return jnp.dot(a.astype(bf16), b.astype(bf16), preferred_element_type=f32)


def _dot_nt(a, b):
    return lax.dot_general(a.astype(bf16), b.astype(bf16), (((1,), (1,)), ((), ())), preferred_element_type=f32)


def _dot_tn(a, b):
    return lax.dot_general(a.astype(bf16), b.astype(bf16), (((0,), (0,)), ((), ())), preferred_element_type=f32)


def _bmm(eq, a, b):
    return jnp.einsum(eq, a.astype(bf16), b.astype(bf16), preferred_element_type=f32)


def _split3(a):
    a1 = a.astype(bf16)
    r = a - a1.astype(f32)
    a2 = r.astype(bf16)
    return a1, a2, (r - a2.astype(f32)).astype(bf16)


def _bmm3(eq, a, b):
    a1, a2, _ = _split3(a)
    b1, b2, _ = _split3(b)
    e = lambda x, y: jnp.einsum(eq, x, y, preferred_element_type=f32)
    return e(a1, b1) + (e(a1, b2) + e(a2, b1))


def _rowsum(x):
    x1, x2, _ = _split3(x)
    ones = jnp.ones((x.shape[0], CH, CH), bf16)
    e = lambda y: jnp.einsum('bij,bjk->bik', y, ones, preferred_element_type=f32)
    return e(x1) + e(x2)


def _tri_ones(B, upper=False):
    ii = lax.broadcasted_iota(jnp.int32, (B, CH, CH), 1)
    jj = lax.broadcasted_iota(jnp.int32, (B, CH, CH), 2)
    return jnp.where((ii <= jj) if upper else (ii >= jj), 1.0, 0.0).astype(bf16)


def _cumsum_mm(t, x):
    x1, x2, x3 = _split3(x)
    e = lambda y: jnp.einsum('bij,bjk->bik', t, y, preferred_element_type=f32)
    return e(x1) + (e(x2) + e(x3))


def _chunk_cumsum(x, reverse=False):
    n = x.shape[0] // CH
    return _cumsum_mm(_tri_ones(n, upper=reverse), x.reshape(n, CH, 128)).reshape(x.shape)


@jax.custom_vjp
def _neumann_inv(m):
    ii = lax.broadcasted_iota(jnp.int32, m.shape, 1)
    jj = lax.broadcasted_iota(jnp.int32, m.shape, 2)
    inv = jnp.where(ii == jj, 1.0, 0.0).astype(f32) + m
    mp = m
    for _ in range(5):
        mp = _bmm3('bij,bjk->bik', mp, mp)
        inv = inv + _bmm3('bij,bjk->bik', inv, mp)
    return inv


def _neumann_inv_fwd(m):
    inv = _neumann_inv(m)
    return inv, inv


def _neumann_inv_bwd(inv, g):
    return (_bmm3('bij,bkj->bik', _bmm3('bji,bjk->bik', inv, g), inv),)


_neumann_inv.defvjp(_neumann_inv_fwd, _neumann_inv_bwd)


@jax.custom_vjp
def _known_inv(m, inv):
    return inv


def _known_inv_fwd(m, inv):
    return inv, inv


def _known_inv_bwd(inv, g):
    return _neumann_inv_bwd(inv, g)[0], jnp.zeros_like(inv)


_known_inv.defvjp(_known_inv_fwd, _known_inv_bwd)


def _ln_stats(z):
    mu = jnp.mean(z, -1, keepdims=True)
    zc = z - mu
    rstd = lax.rsqrt(jnp.mean(zc * zc, -1, keepdims=True) + LN_EPS)
    return zc * rstd, rstd


def _ln_bwd(z, g, dout):
    xh, rstd = _ln_stats(z)
    dxh = dout * g
    dz = rstd * (dxh - jnp.mean(dxh, -1, keepdims=True) - xh * jnp.mean(dxh * xh, -1, keepdims=True))
    return dz, jnp.sum(dout * xh, 0, keepdims=True), jnp.sum(dout, 0, keepdims=True)


def _full_spec(shape):
    nd = len(shape)
    return pl.BlockSpec(shape, lambda *_: (0,) * nd)


def _ffn_fwd(x, p384, pd, lg, lb, slot, which, ple=None):
    T = x.shape[0]
    sg, su, sd = 2 * slot, 2 * slot + 1, slot
    has_ple = ple is not None

    def body(*refs):
        if has_ple:
            (x_ref, wg_ref, wu_ref, wd_ref, lg_ref, lb_ref, p_ref, wpg_ref, wpp_ref,
             z_ref, o_ref, g_ref, u_ref, a_ref, acc, xb_s) = refs
        else:
            x_ref, wg_ref, wu_ref, wd_ref, lg_ref, lb_ref, z_ref, o_ref, g_ref, u_ref, a_ref, acc, xb_s = refs
        f = pl.program_id(1)

        @pl.when(f == 0)
        def _():
            x = x_ref[...]
            xb = x.astype(bf16)
            xb_s[...] = xb
            base = ALPHA * x
            if has_ple:
                gate = _sigmoid(_dot(xb, wpg_ref[...].reshape(D, D)))
                base = base + gate * _dot(p_ref[...], wpp_ref[...])
            acc[...] = base

        xb = xb_s[...]
        g = _dot(xb, wg_ref[...])
        u = _dot(xb, wu_ref[...])
        g_ref[...] = g.astype(bf16)
        u_ref[...] = u.astype(bf16)
        a = (_silu(g) * u).astype(bf16)
        a_ref[...] = a
        acc[...] += 0.5 * _dot(a, wd_ref[...].reshape(FB * FSP, D))

        @pl.when(f == NF - 1)
        def _():
            z = acc[...]
            z_ref[...] = z
            o_ref[...] = _ln_stats(z)[0] * lg_ref[...] + lb_ref[...]

    row = pl.BlockSpec((TM, D), lambda i, f: (i, 0))
    in_specs = [row,
                pl.BlockSpec((None, D, FB * FSP), lambda i, f: (sg, 0, f)),
                pl.BlockSpec((None, D, FB * FSP), lambda i, f: (su, 0, f)),
                pl.BlockSpec((FB, None, FSP, D), lambda i, f: (f, sd, 0, 0)),
                _full_spec((1, D)), _full_spec((1, D))]
    args = [x, p384, p384, pd, lg, lb]
    if has_ple:
        p, pr, wpp = ple
        in_specs += [pl.BlockSpec((TM, PLE), lambda i, f: (i, 0)),
                     pl.BlockSpec((NDEV, None, 128, D), lambda i, f: (0, 2, 0, 0)),
                     _full_spec((PLE, D))]
        args += [p, pr, wpp]
    hid = pl.BlockSpec((TM, FB * FSP), lambda i, f: (i, f))
    hshape = jax.ShapeDtypeStruct((T, NDEV * FSP), bf16)
    return pl.pallas_call(
        body, grid=(T // TM, NF), in_specs=in_specs, out_specs=[row, row, hid, hid, hid],
        out_shape=[jax.ShapeDtypeStruct((T, D), f32)] * 2 + [hshape, hshape, hshape],
        scratch_shapes=[pltpu.VMEM((TM, D), f32), pltpu.VMEM((TM, D), bf16)],
        compiler_params=_cparams(("arbitrary", "arbitrary")), name=f"ffn{which + 1}_fwd")(*args)


def _ffn_bwd(z, dout, gs, us, p384, pd, lg, slot, which):
    T = z.shape[0]
    TMB = TM
    sg, su, sd = 2 * slot, 2 * slot + 1, slot

    def body(z_ref, do_ref, g_ref, u_ref, wg_ref, wu_ref, wd_ref, lg_ref,
             dx_ref, dg_ref, du_ref, dy_ref, dlg_ref, dlb_ref, acc, dyb):
        i, f = pl.program_id(0), pl.program_id(1)

        @pl.when(jnp.logical_and(i == 0, f == 0))
        def _():
            dlg_ref[...] = jnp.zeros_like(dlg_ref)
            dlb_ref[...] = jnp.zeros_like(dlb_ref)

        @pl.when(f == 0)
        def _():
            dz, dlg, dlb = _ln_bwd(z_ref[...], lg_ref[...], do_ref[...])
            dlg_ref[...] += dlg
            dlb_ref[...] += dlb
            dy = (0.5 * dz).astype(bf16)
            dyb[...] = dy
            dy_ref[...] = dy
            acc[...] = ALPHA * dz

        g = g_ref[...].astype(f32)
        u = u_ref[...].astype(f32)
        da = _dot_nt(dyb[...], wd_ref[...].reshape(FB * FSP, D))
        sgm = _sigmoid(g)
        dg = (da * u * (sgm * (1.0 + g * (1.0 - sgm)))).astype(bf16)
        du = (da * (g * sgm)).astype(bf16)
        dg_ref[...] = dg
        du_ref[...] = du
        acc[...] += _dot_nt(dg, wg_ref[...]) + _dot_nt(du, wu_ref[...])

        @pl.when(f == NF - 1)
        def _():
            dx_ref[...] = acc[...]

    row = pl.BlockSpec((TMB, D), lambda i, f: (i, 0))
    hid = pl.BlockSpec((TMB, FB * FSP), lambda i, f: (i, f))
    vec = _full_spec((1, D))
    in_specs = [row, row, hid, hid,
                pl.BlockSpec((None, D, FB * FSP), lambda i, f: (sg, 0, f)),
                pl.BlockSpec((None, D, FB * FSP), lambda i, f: (su, 0, f)),
                pl.BlockSpec((FB, None, FSP, D), lambda i, f: (f, sd, 0, 0)),
                vec]
    args = [z, dout, gs, us, p384, p384, pd, lg]
    out_specs = [row, hid, hid, row, vec, vec]
    hshape = jax.ShapeDtypeStruct((T, NDEV * FSP), bf16)
    out_shape = [jax.ShapeDtypeStruct((T, D), f32), hshape, hshape, jax.ShapeDtypeStruct((T, D), bf16),
                 jax.ShapeDtypeStruct((1, D), f32), jax.ShapeDtypeStruct((1, D), f32)]
    return pl.pallas_call(
        body, grid=(T // TMB, NF), in_specs=in_specs, out_specs=out_specs, out_shape=out_shape,
        scratch_shapes=[pltpu.VMEM((TMB, D), f32), pltpu.VMEM((TMB, D), bf16)],
        compiler_params=_cparams(("arbitrary", "arbitrary")), name=f"ffn{which + 1}_bwd")(*args)


def _ple_bwd(x, p, dy, dx_ffn, pr, wpp, layer):
    T = x.shape[0]

    def body(x_ref, p_ref, dy_ref, dxf_ref, wpg_ref, wpp_ref, dx_ref, dgp_ref, dpj_ref):
        dz = 2.0 * dy_ref[...].astype(f32)
        wpg = wpg_ref[...].reshape(D, D)
        gate = _sigmoid(_dot(x_ref[...], wpg))
        proj = _dot(p_ref[...], wpp_ref[...])
        dgp = (dz * proj * gate * (1.0 - gate)).astype(bf16)
        dgp_ref[...] = dgp
        dpj_ref[...] = (dz * gate).astype(bf16)
        dx_ref[...] = dxf_ref[...] + _dot_nt(dgp, wpg)

    row = pl.BlockSpec((TM, D), lambda i: (i, 0))
    return pl.pallas_call(
        body, grid=(T // TM,),
        in_specs=[row, pl.BlockSpec((TM, PLE), lambda i: (i, 0)), row, row,
                  pl.BlockSpec((NDEV, None, 128, D), lambda i: (0, 3 * layer + 2, 0, 0)), _full_spec((PLE, D))],
        out_specs=[row, row, row],
        out_shape=[jax.ShapeDtypeStruct((T, D), f32), jax.ShapeDtypeStruct((T, D), bf16),
                   jax.ShapeDtypeStruct((T, D), bf16)],
        compiler_params=_cparams(("arbitrary",)), name="ple_bwd")(x, p, dy, dx_ffn, pr, wpp)


def _matmul_tn(a, b, nb, name, nsub=1):
    T, M = a.shape
    N = b.shape[1]
    wide = nsub * nb
    tk = min(T, 1024 if wide <= 2048 else 512)
    nk = T // tk

    def body(a_ref, b_ref, o_ref, acc):
        k = pl.program_id(1)

        @pl.when(k == 0)
        def _():
            acc[...] = jnp.zeros_like(acc)

        acc[...] += _dot_tn(a_ref[...], b_ref[...])

        @pl.when(k == nk - 1)
        def _():
            for j in range(nsub):
                o_ref[j] = acc[:, j * nb:(j + 1) * nb].astype(bf16)

    return pl.pallas_call(
        body, grid=(N // wide, nk),
        in_specs=[pl.BlockSpec((tk, M), lambda n, k: (k, 0)), pl.BlockSpec((tk, wide), lambda n, k: (k, n))],
        out_specs=pl.BlockSpec((nsub, M, nb), lambda n, k: (n, 0, 0)),
        out_shape=jax.ShapeDtypeStruct((N // nb, M, nb), bf16),
        scratch_shapes=[pltpu.VMEM((M, wide), f32)],
        compiler_params=_cparams(("arbitrary", "arbitrary")), name=name)(a, b)


def _proj_in(x, pr, pinl, ping, layer):
    T = x.shape[0]

    def body(x_ref, wr_ref, wl_ref, wg_ref, hr_ref, hl_ref, hg_ref):
        xb = x_ref[...].astype(bf16)
        hr_ref[...] = _dot(xb, wr_ref[...].reshape(D, D))
        hl_ref[...] = _dot(xb, wl_ref[...].reshape(D, 2 * LRU_W))
        hg_ref[...] = _dot(xb, wg_ref[...].reshape(D, GDN_IN))

    return pl.pallas_call(
        body, grid=(T // TM,),
        in_specs=[pl.BlockSpec((TM, D), lambda i: (i, 0)),
                  pl.BlockSpec((NDEV, None, 128, D), lambda i: (0, 3 * layer, 0, 0)),
                  pl.BlockSpec((NDEV, None, 128, 2 * LRU_W), lambda i: (0, layer, 0, 0)),
                  pl.BlockSpec((NDEV, None, 128, GDN_IN), lambda i: (0, layer, 0, 0))],
        out_specs=[pl.BlockSpec((TM, D), lambda i: (i, 0)), pl.BlockSpec((TM, 2 * LRU_W), lambda i: (i, 0)),
                   pl.BlockSpec((TM, GDN_IN), lambda i: (i, 0))],
        out_shape=[jax.ShapeDtypeStruct((T, D), f32), jax.ShapeDtypeStruct((T, 2 * LRU_W), f32),
                   jax.ShapeDtypeStruct((T, GDN_IN), f32)],
        compiler_params=_cparams(("arbitrary",)), name="proj_in")(x, pr, pinl, ping)


def _proj_in_bwd(base, dhr, dhl, dhg, pr, pinl, ping, layer):
    T = base.shape[0]

    def body(b_ref, dr_ref, dl_ref, dg_ref, wr_ref, wl_ref, wg_ref, o_ref):
        o_ref[...] = (b_ref[...] + _dot_nt(dr_ref[...], wr_ref[...].reshape(D, D))
                      + _dot_nt(dl_ref[...], wl_ref[...].reshape(D, 2 * LRU_W))
                      + _dot_nt(dg_ref[...], wg_ref[...].reshape(D, GDN_IN)))

    return pl.pallas_call(
        body, grid=(T // TM,),
        in_specs=[pl.BlockSpec((TM, D), lambda i: (i, 0)), pl.BlockSpec((TM, D), lambda i: (i, 0)),
                  pl.BlockSpec((TM, 2 * LRU_W), lambda i: (i, 0)), pl.BlockSpec((TM, GDN_IN), lambda i: (i, 0)),
                  pl.BlockSpec((NDEV, None, 128, D), lambda i: (0, 3 * layer, 0, 0)),
                  pl.BlockSpec((NDEV, None, 128, 2 * LRU_W), lambda i: (0, layer, 0, 0)),
                  pl.BlockSpec((NDEV, None, 128, GDN_IN), lambda i: (0, layer, 0, 0))],
        out_specs=pl.BlockSpec((TM, D), lambda i: (i, 0)),
        out_shape=jax.ShapeDtypeStruct((T, D), f32),
        compiler_params=_cparams(("arbitrary",)), name="proj_in_bwd")(base, dhr, dhl, dhg, pr, pinl, ping)


def _mix_out(x1, o_r, o_l, o_g, pr, lg, lb, layer):
    T = x1.shape[0]

    def body(x_ref, r_ref, l_ref, g_ref, w_ref, lg_ref, lb_ref, z_ref, o_ref):
        w = w_ref[...].reshape(D, D)
        z = (ALPHA * x_ref[...] + _dot(r_ref[...], w[0:RET_W]) + _dot(l_ref[...], w[RET_W:RET_W + LRU_W])
             + _dot(g_ref[...], w[RET_W + LRU_W:D]))
        z_ref[...] = z
        o_ref[...] = _ln_stats(z)[0] * lg_ref[...] + lb_ref[...]

    row = pl.BlockSpec((TM, D), lambda i: (i, 0))
    return pl.pallas_call(
        body, grid=(T // TM,),
        in_specs=[row, pl.BlockSpec((TM, RET_W), lambda i: (i, 0)), pl.BlockSpec((TM, LRU_W), lambda i: (i, 0)),
                  pl.BlockSpec((TM, GDN_W), lambda i: (i, 0)),
                  pl.BlockSpec((NDEV, None, 128, D), lambda i: (0, 3 * layer + 1, 0, 0)),
                  _full_spec((1, D)), _full_spec((1, D))],
        out_specs=[row, row], out_shape=[jax.ShapeDtypeStruct((T, D), f32)] * 2,
        compiler_params=_cparams(("arbitrary",)), name="mix_out")(x1, o_r, o_l, o_g, pr, lg, lb)


def _mix_out_bwd(z, dout, pr, lg, layer):
    T = z.shape[0]

    def body(z_ref, do_ref, w_ref, lg_ref, dxb_ref, dzb_ref, dr_ref, dl_ref, dg_ref, dlg_ref, dlb_ref):
        @pl.when(pl.program_id(0) == 0)
        def _():
            dlg_ref[...] = jnp.zeros_like(dlg_ref)
            dlb_ref[...] = jnp.zeros_like(dlb_ref)

        dz, dlg, dlb = _ln_bwd(z_ref[...], lg_ref[...], do_ref[...])
        dlg_ref[...] += dlg
        dlb_ref[...] += dlb
        dxb_ref[...] = ALPHA * dz
        dzb = dz.astype(bf16)
        dzb_ref[...] = dzb
        w = w_ref[...].reshape(D, D)
        dr_ref[...] = _dot_nt(dzb, w[0:RET_W])
        dl_ref[...] = _dot_nt(dzb, w[RET_W:RET_W + LRU_W])
        dg_ref[...] = _dot_nt(dzb, w[RET_W + LRU_W:D])

    row = pl.BlockSpec((TM, D), lambda i: (i, 0))
    vec = _full_spec((1, D))
    return pl.pallas_call(
        body, grid=(T // TM,),
        in_specs=[row, row, pl.BlockSpec((NDEV, None, 128, D), lambda i: (0, 3 * layer + 1, 0, 0)), vec],
        out_specs=[row, row, pl.BlockSpec((TM, RET_W), lambda i: (i, 0)), pl.BlockSpec((TM, LRU_W), lambda i: (i, 0)),
                   pl.BlockSpec((TM, GDN_W), lambda i: (i, 0)), vec, vec],
        out_shape=[jax.ShapeDtypeStruct((T, D), f32), jax.ShapeDtypeStruct((T, D), bf16),
                   jax.ShapeDtypeStruct((T, RET_W), f32), jax.ShapeDtypeStruct((T, LRU_W), f32),
                   jax.ShapeDtypeStruct((T, GDN_W), f32), jax.ShapeDtypeStruct((1, D), f32),
                   jax.ShapeDtypeStruct((1, D), f32)],
        compiler_params=_cparams(("arbitrary",)), name="mix_out_bwd")(z, dout, pr, lg)


def _loss_grad(y, target):
    T = y.shape[0]

    def body(y_ref, t_ref, dy_ref, l_ref):
        @pl.when(pl.program_id(0) == 0)
        def _():
            l_ref[...] = jnp.zeros_like(l_ref)

        e = y_ref[...] - t_ref[...]
        dy_ref[...] = e * (1.0 / D)
        l_ref[...] += 0.5 * jnp.sum(jnp.sum(e * e, -1, keepdims=True) * (1.0 / D), 0, keepdims=True)

    row = pl.BlockSpec((TM, D), lambda i: (i, 0))
    return pl.pallas_call(
        body, grid=(T // TM,), in_specs=[row, row], out_specs=[row, _full_spec((1, 1))],
        out_shape=[jax.ShapeDtypeStruct((T, D), f32), jax.ShapeDtypeStruct((1, 1), f32)],
        compiler_params=_cparams(("arbitrary",)), name="loss_grad")(y, target)


def _split_heads(x, H):
    n = x.shape[0] // CH
    parts = [x[:, h * CH:(h + 1) * CH].reshape(n, CH, CH) for h in range(H)]
    return jnp.stack(parts, axis=1).reshape(n * H, CH, CH)


def _merge_heads(ref, x, H, col0=0):
    n = x.shape[0] // H
    x4 = x.reshape(n, H, CH, CH)
    for h in range(H):
        ref[:, col0 + h * CH:col0 + (h + 1) * CH] = x4[:, h].reshape(n * CH, CH)


def _rows_down(x, before, s):
    r8 = lax.broadcasted_iota(jnp.int32, before.shape, 0)
    top = jnp.where(r8 < s, pltpu.roll(before, s, 0), pltpu.roll(x[0:8], s, 0))
    return jnp.concatenate([top, pltpu.roll(x, s, 0)[8:]], axis=0)


def _rows_up(x, after, s):
    R = x.shape[0]
    r8 = lax.broadcasted_iota(jnp.int32, after.shape, 0)
    bottom = jnp.where(r8 >= 8 - s, pltpu.roll(after, 8 - s, 0), pltpu.roll(x[R - 8:R], 8 - s, 0))
    return jnp.concatenate([pltpu.roll(x, R - s, 0)[0:R - 8], bottom], axis=0)


def _conv_fwd(ext, x, tail, w, R):
    ext[0:8, :] = tail
    ext[8:R + 8, :] = x
    y = w[3:4, :] * x
    for k in range(3):
        y = y + w[k:k + 1, :] * _rows_down(x, tail, 3 - k)
    return y


def _conv_bwd(ext, dy, dy_next, w, R):
    x, tail = ext[8:8 + R, :], ext[0:8, :]
    dx = w[3:4, :] * dy
    dws = []
    for k in range(3):
        dx = dx + w[k:k + 1, :] * _rows_up(dy, dy_next, 3 - k)
        dws.append(jnp.sum(dy * _rows_down(x, tail, 3 - k), 0, keepdims=True))
    dws.append(jnp.sum(dy * x, 0, keepdims=True))
    return dx, jnp.concatenate(dws, axis=0)


def _prev_tail_spec(R, W):
    return pl.BlockSpec((8, W), lambda i: (jnp.maximum(i * (R // 8) - 1, 0), 0))


def _prev_tail_spec_rev(R, W, nb):
    return pl.BlockSpec((8, W), lambda i: (jnp.maximum((nb - 1 - i) * (R // 8) - 1, 0), 0))


def _rope_tables(positions):
    T = positions.shape[0]

    def body(p_ref, c_ref, s_ref):
        lane = lax.broadcasted_iota(jnp.int32, (TM, RET_W), 1)
        fi = (lane % 32).astype(f32)
        inv = jnp.exp(fi * (-math.log(ROPE_THETA) / 32.0))
        ang = p_ref[...].astype(f32) * inv
        c_ref[...] = jnp.cos(ang)
        s_ref[...] = jnp.where(lane % CH < 32, -jnp.sin(ang), jnp.sin(ang))

    row = pl.BlockSpec((TM, RET_W), lambda i: (i, 0))
    return pl.pallas_call(
        body, grid=(T // TM,), in_specs=[pl.BlockSpec((TM, 1), lambda i: (i, 0))], out_specs=[row, row],
        out_shape=[jax.ShapeDtypeStruct((T, RET_W), f32)] * 2,
        compiler_params=_cparams(("arbitrary",)), name="rope_tables")(positions)


def _partner(x):
    lane = lax.broadcasted_iota(jnp.int32, x.shape, 1)
    return jnp.where(lane % CH < 32, pltpu.roll(x, RET_W - 32, 1), pltpu.roll(x, 32, 1))


def _ret_consts():
    ii = lax.broadcasted_iota(jnp.int32, (CH, CH), 0).astype(f32)
    jj = lax.broadcasted_iota(jnp.int32, (CH, CH), 1).astype(f32)
    intra, cross, tail, cd = [], [], [], []
    for h in range(RET_H):
        lg = math.log1p(-(2.0 ** (-5.0 - h)))
        intra.append(jnp.exp(jnp.abs(ii - jj) * lg))
        cross.append(jnp.exp((ii + 1.0) * lg))
        tail.append(jnp.exp((CH - 1.0 - ii) * lg))
        cd.append(jnp.full((CH, CH), math.exp(CH * lg), f32))
    return jnp.stack(intra), jnp.stack(cross), jnp.stack(tail), jnp.stack(cd)


def _ret_chunk(consts, q, k, v, st):
    intra, cross, tail, cd = consts
    s = _bmm('hid,hjd->hij', q, k) * intra
    o = _bmm('hij,hje->hie', s, v) + _bmm('hid,hde->hie', q * cross, st)
    st2 = st * cd + _bmm('hjd,hje->hde', k * tail, v)
    oc = o - jnp.mean(o, -1, keepdims=True)
    on = oc * lax.rsqrt(jnp.mean(oc * oc, -1, keepdims=True) + 1e-5)
    return on, st2


def _ret_fwd(hr, cosw, sinw, gam):
    T = hr.shape[0]
    R = RB_RET
    nc = R // CH

    def body(h_ref, c_ref, s_ref, g_ref, o_ref, st_ref, st, wide):
        @pl.when(pl.program_id(0) == 0)
        def _():
            st[...] = jnp.zeros_like(st)

        consts = _ret_consts()
        cw, sw = c_ref[...], s_ref[...]
        q, k = h_ref[:, 0:RET_W], h_ref[:, RET_W:2 * RET_W]
        qh = _split_heads((q * cw + _partner(q) * sw) * 0.125, RET_H)
        kh = _split_heads(k * cw + _partner(k) * sw, RET_H)
        vh = _split_heads(h_ref[:, 2 * RET_W:3 * RET_W], RET_H)
        outs = []
        s_cur = st[...]
        for c in range(nc):
            sl = slice(c * RET_H, (c + 1) * RET_H)
            st_ref[c] = s_cur
            on, s_cur = _ret_chunk(consts, qh[sl], kh[sl], vh[sl], s_cur)
            outs.append(on)
        st[...] = s_cur
        _merge_heads(wide, jnp.concatenate(outs, axis=0), RET_H)
        o_ref[...] = wide[...] * g_ref[...] * _silu(h_ref[:, 3 * RET_W:4 * RET_W])

    blk = pl.BlockSpec((R, RET_W), lambda i: (i, 0))
    return pl.pallas_call(
        body, grid=(T // R,),
        in_specs=[pl.BlockSpec((R, D), lambda i: (i, 0)), blk, blk, _full_spec((1, RET_W))],
        out_specs=[blk, pl.BlockSpec((nc, RET_H, CH, CH), lambda i: (i, 0, 0, 0))],
        out_shape=[jax.ShapeDtypeStruct((T, RET_W), f32), jax.ShapeDtypeStruct((T // CH, RET_H, CH, CH), f32)],
        scratch_shapes=[pltpu.VMEM((RET_H, CH, CH), f32), pltpu.VMEM((R, RET_W), f32)],
        compiler_params=_cparams(("arbitrary",)), name="ret_fwd")(hr, cosw, sinw, gam)


def _ret_bwd(hr, cosw, sinw, gam, states, dout):
    T = hr.shape[0]
    R = RB_RET
    nc = R // CH
    nb = T // R

    def body(h_ref, c_ref, s_ref, g_ref, st_ref, do_ref, dh_ref, dgam_ref, dst, wide):
        @pl.when(pl.program_id(0) == 0)
        def _():
            dst[...] = jnp.zeros_like(dst)
            dgam_ref[...] = jnp.zeros_like(dgam_ref)

        consts = _ret_consts()
        cw, sw = c_ref[...], s_ref[...]
        q, k = h_ref[:, 0:RET_W], h_ref[:, RET_W:2 * RET_W]
        gr = h_ref[:, 3 * RET_W:4 * RET_W]
        qh = _split_heads((q * cw + _partner(q) * sw) * 0.125, RET_H)
        kh = _split_heads(k * cw + _partner(k) * sw, RET_H)
        vh = _split_heads(h_ref[:, 2 * RET_W:3 * RET_W], RET_H)
        do = do_ref[...]
        gam = g_ref[...]
        sg = _silu(gr)
        don = _split_heads(do * gam * sg, RET_H)
        ons, dqs, dks, dvs = [None] * nc, [None] * nc, [None] * nc, [None] * nc
        ds = dst[...]
        for c in reversed(range(nc)):
            sl = slice(c * RET_H, (c + 1) * RET_H)
            (on, _), vjp = jax.vjp(functools.partial(_ret_chunk, consts), qh[sl], kh[sl], vh[sl], st_ref[c])
            dqs[c], dks[c], dvs[c], ds = vjp((don[sl], ds))
            ons[c] = on
        dst[...] = ds
        _merge_heads(wide, jnp.concatenate(ons, axis=0), RET_H)
        onw = wide[...]
        dgam_ref[...] += jnp.sum(do * onw * sg, 0, keepdims=True)
        dh_ref[:, 3 * RET_W:4 * RET_W] = do * onw * gam * _dsilu(gr)
        _merge_heads(wide, jnp.concatenate(dqs, axis=0), RET_H)
        u = wide[...] * 0.125
        dh_ref[:, 0:RET_W] = u * cw + _partner(u * sw)
        _merge_heads(wide, jnp.concatenate(dks, axis=0), RET_H)
        u = wide[...]
        dh_ref[:, RET_W:2 * RET_W] = u * cw + _partner(u * sw)
        _merge_heads(dh_ref, jnp.concatenate(dvs, axis=0), RET_H, col0=2 * RET_W)

    blk = pl.BlockSpec((R, RET_W), lambda i: (nb - 1 - i, 0))
    return pl.pallas_call(
        body, grid=(nb,),
        in_specs=[pl.BlockSpec((R, D), lambda i: (nb - 1 - i, 0)), blk, blk, _full_spec((1, RET_W)),
                  pl.BlockSpec((nc, RET_H, CH, CH), lambda i: (nb - 1 - i, 0, 0, 0)), blk],
        out_specs=[pl.BlockSpec((R, D), lambda i: (nb - 1 - i, 0)), _full_spec((1, RET_W))],
        out_shape=[jax.ShapeDtypeStruct((T, D), f32), jax.ShapeDtypeStruct((1, RET_W), f32)],
        scratch_shapes=[pltpu.VMEM((RET_H, CH, CH), f32), pltpu.VMEM((R, RET_W), f32)],
        compiler_params=_cparams(("arbitrary",)), name="ret_bwd")(hr, cosw, sinw, gam, states, dout)


def _lru_ab(xc, wa, ba, wx, bx, lam):
    r = _sigmoid(_dot(xc, wa) + ba)
    i = _sigmoid(_dot(xc, wx) + bx)
    la = 8.0 * r * (-_softplus(-lam))
    a = jnp.exp(la)
    em = jnp.tanh(la) * (jnp.exp(2.0 * la) + 1.0)
    return a, jnp.sqrt(-em) * (i * xc)


def _lru_out(h, gate):
    return h * _gelu(gate)


def _scan_fwd(a, b):
    R = a.shape[0]
    row = lax.broadcasted_iota(jnp.int32, a.shape, 0)
    d = 1
    while d < R:
        m = row >= d
        b = jnp.where(m, a * pltpu.roll(b, d, 0) + b, b)
        a = jnp.where(m, a * pltpu.roll(a, d, 0), a)
        d *= 2
    return a, b


def _scan_bwd(a, b):
    R = a.shape[0]
    row = lax.broadcasted_iota(jnp.int32, a.shape, 0)
    d = 1
    while d < R:
        m = row < R - d
        b = jnp.where(m, a * pltpu.roll(b, R - d, 0) + b, b)
        a = jnp.where(m, a * pltpu.roll(a, R - d, 0), a)
        d *= 2
    return b


def _lru_fwd(hl, cw, cb, wa, ba, wx, bx, lam):
    T = hl.shape[0]
    R = RB_LRU
    W = LRU_W

    def body(h_ref, t_ref, cw_ref, cb_ref, wa_ref, ba_ref, wx_ref, bx_ref, lam_ref, o_ref, hs_ref, carry, ext):
        first = pl.program_id(0) == 0

        @pl.when(first)
        def _():
            carry[...] = jnp.zeros_like(carry)

        tail = jnp.where(first, 0.0, t_ref[:, 0:W])
        xc = _conv_fwd(ext, h_ref[:, 0:W], tail, cw_ref[...], R) + cb_ref[...]
        a, b = _lru_ab(xc, wa_ref[...], ba_ref[...], wx_ref[...], bx_ref[...], lam_ref[...])
        ap, hloc = _scan_fwd(a, b)
        h = hloc + ap * carry[0:1, :]
        carry[...] = jnp.broadcast_to(h[R - 1:R, :], carry.shape)
        hs_ref[...] = h
        o_ref[...] = _lru_out(h, h_ref[:, W:2 * W])

    vec = _full_spec((1, W))
    blk = pl.BlockSpec((R, W), lambda i: (i, 0))
    return pl.pallas_call(
        body, grid=(T // R,),
        in_specs=[pl.BlockSpec((R, 2 * W), lambda i: (i, 0)), _prev_tail_spec(R, 2 * W), _full_spec((4, W)), vec,
                  _full_spec((W, W)), vec, _full_spec((W, W)), vec, vec],
        out_specs=[blk, blk], out_shape=[jax.ShapeDtypeStruct((T, W), f32)] * 2,
        scratch_shapes=[pltpu.VMEM((8, W), f32), pltpu.VMEM((R + 8, W), f32)],
        compiler_params=_cparams(("arbitrary",)), name="lru_fwd")(hl, hl, cw, cb, wa, ba, wx, bx, lam)


def _lru_bwd(hl, hs, cw, cb, wa, ba, wx, bx, lam, dout):
    T = hl.shape[0]
    R = RB_LRU
    W = LRU_W
    nb = T // R

    def body(h_ref, t_ref, hs_ref, hst_ref, cw_ref, cb_ref, wa_ref, ba_ref, wx_ref, bx_ref, lam_ref, do_ref,
             dh_ref, dcw_ref, dcb_ref, dwa_ref, dba_ref, dwx_ref, dbx_ref, dlam_ref, carry_g, carry_dy, ext):
        i = pl.program_id(0)
        last_blk = i == 0
        first_blk = i == nb - 1

        @pl.when(last_blk)
        def _():
            carry_g[...] = jnp.zeros_like(carry_g)
            carry_dy[...] = jnp.zeros_like(carry_dy)
            for r in (dcw_ref, dcb_ref, dwa_ref, dba_ref, dwx_ref, dbx_ref, dlam_ref):
                r[...] = jnp.zeros_like(r)

        tail = jnp.where(first_blk, 0.0, t_ref[:, 0:W])
        xc = _conv_fwd(ext, h_ref[:, 0:W], tail, cw_ref[...], R) + cb_ref[...]
        (a, _), vjp_ab = jax.vjp(_lru_ab, xc, wa_ref[...], ba_ref[...], wx_ref[...], bx_ref[...], lam_ref[...])
        hs = hs_ref[...]
        _, vjp_out = jax.vjp(_lru_out, hs, h_ref[:, W:2 * W])
        dh, dgate = vjp_out(do_ref[...])
        row = lax.broadcasted_iota(jnp.int32, (R, W), 0)
        dh = jnp.where(row == R - 1, dh + carry_g[0:1, :], dh)
        a_up = jnp.where(row == R - 1, 0.0, pltpu.roll(a, R - 1, 0))
        g = _scan_bwd(a_up, dh)
        carry_g[...] = jnp.broadcast_to(a[0:1, :] * g[0:1, :], carry_g.shape)
        hprev0 = jnp.where(first_blk, 0.0, hst_ref[7:8, :])
        hprev = jnp.where(row == 0, hprev0, pltpu.roll(hs, 1, 0))
        dxc, dwa, dba, dwx, dbx, dlam = vjp_ab((g * hprev, g))
        dwa_ref[...] += dwa
        dba_ref[...] += dba
        dwx_ref[...] += dwx
        dbx_ref[...] += dbx
        dlam_ref[...] += dlam
        dcb_ref[...] += jnp.sum(dxc, 0, keepdims=True)
        dx, dcw = _conv_bwd(ext, dxc, carry_dy[...], cw_ref[...], R)
        carry_dy[...] = dxc[0:8, :]
        dcw_ref[...] += dcw
        dh_ref[:, 0:W] = dx
        dh_ref[:, W:2 * W] = dgate

    vec = _full_spec((1, W))
    mat = _full_spec((W, W))
    blk = pl.BlockSpec((R, W), lambda i: (nb - 1 - i, 0))
    blk2 = pl.BlockSpec((R, 2 * W), lambda i: (nb - 1 - i, 0))
    return pl.pallas_call(
        body, grid=(nb,),
        in_specs=[blk2, _prev_tail_spec_rev(R, 2 * W, nb), blk, _prev_tail_spec_rev(R, W, nb), _full_spec((4, W)), vec,
                  mat, vec, mat, vec, vec, blk],
        out_specs=[blk2, _full_spec((4, W)), vec, mat, vec, mat, vec, vec],
        out_shape=[jax.ShapeDtypeStruct((T, 2 * W), f32), jax.ShapeDtypeStruct((4, W), f32),
                   jax.ShapeDtypeStruct((1, W), f32), jax.ShapeDtypeStruct((W, W), f32),
                   jax.ShapeDtypeStruct((1, W), f32), jax.ShapeDtypeStruct((W, W), f32),
                   jax.ShapeDtypeStruct((1, W), f32), jax.ShapeDtypeStruct((1, W), f32)],
        scratch_shapes=[pltpu.VMEM((8, W), f32), pltpu.VMEM((8, W), f32), pltpu.VMEM((R + 8, W), f32)],
        compiler_params=_cparams(("arbitrary",)), name="lru_bwd")(hl, hl, hs, hs, cw, cb, wa, ba, wx, bx, lam, dout)


def _head_ones():
    i = lax.broadcasted_iota(jnp.int32, (GDN_W, GDN_W), 0)
    j = lax.broadcasted_iota(jnp.int32, (GDN_W, GDN_W), 1)
    return jnp.where(jnp.bitwise_xor(i, j) < CH, 1.0, 0.0).astype(bf16)


def _head_sums(x, ones):
    x1, x2, _ = _split3(x)
    return jnp.dot(x1, ones, preferred_element_type=f32) + jnp.dot(x2, ones, preferred_element_type=f32)


def _l2n(y, ones):
    r = lax.rsqrt(_head_sums(y * y, ones) + 1e-6)
    return y * r, r


def _l2n_bwd(dn, n, r, ones):
    return r * (dn - n * _head_sums(dn * n, ones))


def _gdn_local(inverse, q, k, vs, gc, bb):
    B = q.shape[0]
    ii = lax.broadcasted_iota(jnp.int32, (B, CH, CH), 1)
    jj = lax.broadcasted_iota(jnp.int32, (B, CH, CH), 2)
    gct = jnp.swapaxes(gc, 1, 2)
    decay = jnp.where(ii >= jj, jnp.exp(jnp.minimum(gc - gct, 0.0)), 0.0)
    kk = _bmm('bid,bjd->bij', k, k)
    inv = inverse(-jnp.where(ii > jj, bb * kk * decay, 0.0))
    egc = jnp.exp(gc)
    u = _bmm3('bij,bje->bie', inv, vs * bb)
    w = _bmm3('bij,bje->bie', inv, k * (bb * egc))
    qk = _bmm('bid,bjd->bij', q, k) * (0.125 * decay)
    glast = gc[:, CH - 1:CH, :]
    return u, w, qk, q * (0.125 * egc), k * jnp.exp(glast - gc), jnp.exp(jnp.broadcast_to(glast, gc.shape))


def _gdn_step(st, u, w, qk, qd, kt, egl, z, gn):
    vnew = u - _bmm('hcd,hde->hce', w, st)
    o = _bmm('hcd,hde->hce', qd, st) + _bmm('hij,hje->hie', qk, vnew)
    st2 = st * egl + _bmm('hcd,hce->hde', kt, vnew)
    out = o * lax.rsqrt(_rowsum(o * o) * (1.0 / CH) + 1e-6) * gn * _silu(z)
    return out, st2


def _gdn_scalars(ab, alog, dtb):
    sp = _softplus(ab + dtb)
    return -jnp.exp(alog) * sp, _sigmoid(ab)


def _bcast_heads(blk, lane0, H):
    R = blk.shape[0]
    n = R // CH
    parts = [jnp.broadcast_to(blk[:, lane0 + h:lane0 + h + 1], (R, CH)).reshape(n, CH, CH) for h in range(H)]
    return jnp.stack(parts, axis=1).reshape(n * H, CH, CH)


def _unbcast_heads(x, lane0, H):
    n = x.shape[0] // H
    R = n * CH
    s = jnp.sum(x, axis=2, keepdims=True).reshape(n, H, CH, 1)
    lane = lax.broadcasted_iota(jnp.int32, (R, 128), 1)
    acc = jnp.zeros((R, 128), f32)
    for h in range(H):
        acc = acc + jnp.where(lane == lane0 + h, jnp.broadcast_to(s[:, h].reshape(R, 1), (R, 128)), 0.0)
    return acc


def _gdn_fwd(hg, cw, alog, dtb, gn):
    T = hg.shape[0]
    R = RB_GDN
    nc = R // CH
    W3 = 3 * GDN_W
    H = GDN_H

    def body(h_ref, t_ref, cw_ref, al_ref, dt_ref, gn_ref, o_ref, st_ref, inv_ref, st, ext):
        first = pl.program_id(0) == 0

        @pl.when(first)
        def _():
            st[...] = jnp.zeros_like(st)

        def inverse(m):
            inv = _neumann_inv(m)
            inv_ref[...] = inv
            return inv

        tail = jnp.where(first, 0.0, t_ref[:, 0:W3])
        y = _silu(_conv_fwd(ext, h_ref[:, 0:W3], tail, cw_ref[...], R))
        ones = _head_ones()
        qs = _split_heads(_l2n(y[:, 0:GDN_W], ones)[0], H)
        ks = _split_heads(_l2n(y[:, GDN_W:2 * GDN_W], ones)[0], H)
        vs = _split_heads(y[:, 2 * GDN_W:W3], H)
        zh = _split_heads(h_ref[:, W3:W3 + GDN_W], H)
        g, beta = _gdn_scalars(h_ref[:, W3 + GDN_W:GDN_IN], al_ref[...], dt_ref[...])
        loc = _gdn_local(inverse, qs, ks, vs, _bcast_heads(_chunk_cumsum(g), 0, H), _bcast_heads(beta, H, H))
        gnv = gn_ref[...]
        outs = []
        s_cur = st[...]
        for c in range(nc):
            sl = slice(c * H, (c + 1) * H)
            st_ref[c] = s_cur
            out, s_cur = _gdn_step(s_cur, *(t[sl] for t in loc), zh[sl], gnv)
            outs.append(out)
        st[...] = s_cur
        _merge_heads(o_ref, jnp.concatenate(outs, axis=0), H)

    return pl.pallas_call(
        body, grid=(T // R,),
        in_specs=[pl.BlockSpec((R, GDN_IN), lambda i: (i, 0)), _prev_tail_spec(R, GDN_IN), _full_spec((4, W3)),
                  _full_spec((1, 128)), _full_spec((1, 128)), _full_spec((1, CH))],
        out_specs=[pl.BlockSpec((R, GDN_W), lambda i: (i, 0)), pl.BlockSpec((nc, H, CH, CH), lambda i: (i, 0, 0, 0)),
                   pl.BlockSpec((nc * H, CH, CH), lambda i: (i, 0, 0))],
        out_shape=[jax.ShapeDtypeStruct((T, GDN_W), f32), jax.ShapeDtypeStruct((T // CH, H, CH, CH), f32),
                   jax.ShapeDtypeStruct((T // CH * H, CH, CH), f32)],
        scratch_shapes=[pltpu.VMEM((H, CH, CH), f32), pltpu.VMEM((R + 8, W3), f32)],
        compiler_params=_cparams(("arbitrary",)), name="gdn_fwd")(hg, hg, cw, alog, dtb, gn)


def _gdn_bwd(hg, cw, alog, dtb, gn, states, invs, dout):
    T = hg.shape[0]
    R = RB_GDN
    nc = R // CH
    nb = T // R
    W3 = 3 * GDN_W
    H = GDN_H

    def body(h_ref, t_ref, cw_ref, al_ref, dt_ref, gn_ref, st_ref, inv_ref, do_ref,
             dh_ref, dcw_ref, dal_ref, ddt_ref, dgn_ref, dst, carry_dy, ext, wide):
        i = pl.program_id(0)
        first_blk = i == nb - 1

        @pl.when(i == 0)
        def _():
            dst[...] = jnp.zeros_like(dst)
            carry_dy[...] = jnp.zeros_like(carry_dy)
            for r in (dcw_ref, dal_ref, ddt_ref, dgn_ref):
                r[...] = jnp.zeros_like(r)

        tail = jnp.where(first_blk, 0.0, t_ref[:, 0:W3])
        ypre = _conv_fwd(ext, h_ref[:, 0:W3], tail, cw_ref[...], R)
        y = _silu(ypre)
        ones = _head_ones()
        qn, rq = _l2n(y[:, 0:GDN_W], ones)
        kn, rk = _l2n(y[:, GDN_W:2 * GDN_W], ones)
        qs, ks, vs = _split_heads(qn, H), _split_heads(kn, H), _split_heads(y[:, 2 * GDN_W:W3], H)
        zh = _split_heads(h_ref[:, W3:W3 + GDN_W], H)
        ab = h_ref[:, W3 + GDN_W:GDN_IN]
        alog, dtb = al_ref[...], dt_ref[...]
        g, beta = _gdn_scalars(ab, alog, dtb)
        kept = inv_ref[...]
        loc, vjp_loc = jax.vjp(functools.partial(_gdn_local, lambda m: _known_inv(m, kept)), qs, ks, vs,
                               _bcast_heads(_chunk_cumsum(g), 0, H), _bcast_heads(beta, H, H))
        doh = _split_heads(do_ref[...], H)
        gnv = gn_ref[...]
        dloc = [[None] * nc for _ in range(6)]
        dzs = [None] * nc
        ds = dst[...]
        dgn = jnp.zeros((1, CH), f32)
        for c in reversed(range(nc)):
            sl = slice(c * H, (c + 1) * H)
            _, vjp = jax.vjp(_gdn_step, st_ref[c], *(t[sl] for t in loc), zh[sl], gnv)
            grads = vjp((doh[sl], ds))
            ds = grads[0]
            for j in range(6):
                dloc[j][c] = grads[1 + j]
            dzs[c] = grads[7]
            dgn = dgn + grads[8]
        dst[...] = ds
        dgn_ref[...] += dgn
        dqs, dks, dvs, dgb, dbb = vjp_loc(tuple(jnp.concatenate(d, axis=0) for d in dloc))
        lane = lax.broadcasted_iota(jnp.int32, (R, 128), 1)
        dg = _chunk_cumsum(_unbcast_heads(dgb, 0, H), reverse=True)
        dbeta = _unbcast_heads(dbb, H, H)
        da = dg * (-jnp.exp(alog)) * _sigmoid(ab + dtb)
        dh_ref[:, W3 + GDN_W:GDN_IN] = jnp.where(lane < H, da, dbeta * beta * (1.0 - beta))
        ddt_ref[...] += jnp.sum(jnp.where(lane < H, da, 0.0), 0, keepdims=True)
        dal_ref[...] += jnp.sum(jnp.where(lane < H, dg * g, 0.0), 0, keepdims=True)
        _merge_heads(dh_ref, jnp.concatenate(dzs, axis=0), H, col0=W3)
        for j, dpart in enumerate((dqs, dks, dvs)):
            _merge_heads(wide, dpart, H, col0=j * GDN_W)
        wide[:, 0:GDN_W] = _l2n_bwd(wide[:, 0:GDN_W], qn, rq, ones)
        wide[:, GDN_W:2 * GDN_W] = _l2n_bwd(wide[:, GDN_W:2 * GDN_W], kn, rk, ones)
        dy = wide[...] * _dsilu(ypre)
        dx, dcw = _conv_bwd(ext, dy, carry_dy[...], cw_ref[...], R)
        carry_dy[...] = dy[0:8, :]
        dcw_ref[...] += dcw
        dh_ref[:, 0:W3] = dx

    blk = pl.BlockSpec((R, GDN_IN), lambda i: (nb - 1 - i, 0))
    return pl.pallas_call(
        body, grid=(nb,),
        in_specs=[blk, _prev_tail_spec_rev(R, GDN_IN, nb), _full_spec((4, W3)), _full_spec((1, 128)),
                  _full_spec((1, 128)), _full_spec((1, CH)),
                  pl.BlockSpec((nc, H, CH, CH), lambda i: (nb - 1 - i, 0, 0, 0)),
                  pl.BlockSpec((nc * H, CH, CH), lambda i: (nb - 1 - i, 0, 0)),
                  pl.BlockSpec((R, GDN_W), lambda i: (nb - 1 - i, 0))],
        out_specs=[blk, _full_spec((4, W3)), _full_spec((1, 128)), _full_spec((1, 128)), _full_spec((1, CH))],
        out_shape=[jax.ShapeDtypeStruct((T, GDN_IN), f32), jax.ShapeDtypeStruct((4, W3), f32),
                   jax.ShapeDtypeStruct((1, 128), f32), jax.ShapeDtypeStruct((1, 128), f32),
                   jax.ShapeDtypeStruct((1, CH), f32)],
        scratch_shapes=[pltpu.VMEM((H, CH, CH), f32), pltpu.VMEM((8, W3), f32), pltpu.VMEM((R + 8, W3), f32),
                        pltpu.VMEM((R, W3), f32)],
        compiler_params=_cparams(("arbitrary",)), name="gdn_bwd")(hg, hg, cw, alog, dtb, gn, states, invs, dout)


def _block_diag(w):
    out = jnp.zeros((LRU_W, LRU_W), w.dtype)
    for g in range(w.shape[0]):
        out = lax.dynamic_update_slice(out, w[g], (g * CH, g * CH))
    return out


def _block_diag_t(w):
    return jnp.stack([w[g * CH:(g + 1) * CH, g * CH:(g + 1) * CH] for g in range(LRU_W // CH)])


def _pad_lanes(v, n=128):
    return jnp.pad(v, (0, n - v.shape[0]))[None, :]


def _local_step(x, p, positions, target, fetch, emit, sm):
    cosw, sinw = _rope_tables(positions)
    cols = lambda w: jnp.transpose(w, (1, 2, 0, 3)).reshape(-1, D, NDEV * FSP)
    saved = []
    h = x
    for l in range(DEPTH):
        v = lambda n: sm[n][l][None, :]
        F1, tok = fetch(l, 'f1', h)
        p384a, pda = cols(F1['p384']), F1['pd']
        z1, x1, g1, u1, a1 = _ffn_fwd(h, p384a, pda, v('ln_ffn1_g') + tok, v('ln_ffn1_b'), 0, 0)
        G, _ = fetch(l, 'rest', x1)
        p384, pd, pr, pinl, ping, wpp = cols(G['p384']), G['pd'], G['pr'], G['pinl'], G['ping'], G['wpp']
        wts = (p384a, pda, p384, pd, pr, pinl, ping, wpp)
        hr, hl, hg = _proj_in(x1, pr, pinl, ping, 0)
        o_r, rst = _ret_fwd(hr, cosw, sinw, v('ret_norm_g'))
        lru_args = (sm['lru_conv_w'][l], v('lru_conv_b'), _block_diag(sm['lru_w_a'][l]), v('lru_b_a'),
                    _block_diag(sm['lru_w_x'][l]), v('lru_b_x'), v('lru_lambda'))
        o_l, hs = _lru_fwd(hl, *lru_args)
        gdn_args = (sm['gdn_conv_w'][l], _pad_lanes(sm['gdn_a_log'][l]), _pad_lanes(sm['gdn_dt_bias'][l]),
                    v('gdn_norm_g'))
        o_g, *gst = _gdn_fwd(hg, *gdn_args)
        z2, x2 = _mix_out(x1, o_r, o_l, o_g, pr, v('ln_mix_g'), v('ln_mix_b'), 0)
        z3, x3, g2, u2, a2 = _ffn_fwd(x2, p384, pd, v('ln_ffn2_g'), v('ln_ffn2_b'), 0, 1, ple=(p[l], pr, wpp))
        saved.append((h, z1, x1, hr, hl, hg, o_r, rst, o_l, hs, lru_args, o_g, gst, gdn_args, z2, x2, z3,
                      g1, u1, a1, g2, u2, a2, wts))
        h = x3
    d, loss = _loss_grad(h, target)

    small = {n: [None] * DEPTH for n in SMALL}
    tok = 0.0
    for l in reversed(range(DEPTH)):
        (x0, z1, x1, hr, hl, hg, o_r, rst, o_l, hs, lru_args, o_g, gst, gdn_args, z2, x2, z3,
         g1, u1, a1, g2, u2, a2, wts) = saved[l]
        p384a, pda, p384, pd, pr, pinl, ping, wpp = wts
        v = lambda n: sm[n][l][None, :]
        rows = lambda m: m.reshape(NDEV, m.shape[1] // NDEV, m.shape[2])
        d2, dg2, du2, dy2, small['ln_ffn2_g'][l], small['ln_ffn2_b'][l] = _ffn_bwd(
            z3, d, g2, u2, p384, pd, v('ln_ffn2_g') + tok, 0, 1)
        d2, dgp, dpj = _ple_bwd(x2, p[l], dy2, d2, pr, wpp, 0)
        dxb, dzb, do_r, do_l, do_g, small['ln_mix_g'][l], small['ln_mix_b'][l] = _mix_out_bwd(
            z2, d2, pr, v('ln_mix_g'), 0)
        dhr, small['ret_norm_g'][l] = _ret_bwd(hr, cosw, sinw, v('ret_norm_g'), rst, do_r)
        (dhl, small['lru_conv_w'][l], small['lru_conv_b'][l], dwa, small['lru_b_a'][l], dwx, small['lru_b_x'][l],
         small['lru_lambda'][l]) = _lru_bwd(hl, hs, *lru_args, do_l)
        small['lru_w_a'][l], small['lru_w_x'][l] = _block_diag_t(dwa), _block_diag_t(dwx)
        dhg, small['gdn_conv_w'][l], dal, ddt, small['gdn_norm_g'][l] = _gdn_bwd(hg, *gdn_args, *gst, do_g)
        small['gdn_a_log'][l], small['gdn_dt_bias'][l] = dal[:, 0:GDN_H], ddt[:, 0:GDN_H]
        d1 = _proj_in_bwd(dxb, dhr, dhl, dhg, pr, pinl, ping, 0)
        dwo = jnp.concatenate([_matmul_tn(o_r, dzb, D, "dw_out_r"), _matmul_tn(o_l, dzb, D, "dw_out_l"),
                               _matmul_tn(o_g, dzb, D, "dw_out_g")], axis=1)
        tok = emit(l, 'rest', {
            'p384': jnp.stack([_matmul_tn(x2, dg2, FSP, "dw_gate", NDEV), _matmul_tn(x2, du2, FSP, "dw_up", NDEV)],
                              axis=1),
            'pd': rows(_matmul_tn(a2, dy2, D, "dw_down"))[:, None],
            'pr': jnp.stack([rows(_matmul_tn(x1, dhr, D, "dw_in_r")), rows(dwo),
                             rows(_matmul_tn(x2, dgp, D, "dw_ple_gate"))], axis=1),
            'pinl': rows(_matmul_tn(x1, dhl, 2 * LRU_W, "dw_in_l"))[:, None],
            'ping': rows(_matmul_tn(x1, dhg, GDN_IN, "dw_in_g"))[:, None],
            'ppp': jnp.transpose(_matmul_tn(p[l], dpj, D, "dw_ple_proj").reshape(PLE, NDEV, 128), (1, 0, 2))[:, None]})
        d, dg1, du1, dy1, small['ln_ffn1_g'][l], small['ln_ffn1_b'][l] = _ffn_bwd(
            z1, d1, g1, u1, p384a, pda, v('ln_ffn1_g') + tok, 0, 0)
        if l == 0:
            emit(l, 'small', {n: jnp.stack([g.reshape(sm[n].shape[1:]) for g in gs]) for n, gs in small.items()})
        tok = emit(l, 'f1', {
            'p384': jnp.stack([_matmul_tn(x0, dg1, FSP, "dw_gate", NDEV), _matmul_tn(x0, du1, FSP, "dw_up", NDEV)],
                              axis=1),
            'pd': rows(_matmul_tn(a1, dy1, D, "dw_down"))[:, None]})
    return loss, d


def _pack_big(ws, dtype=bf16):
    padc = lambda a, n: jnp.pad(a, ((0, 0), (0, 0), (0, n - a.shape[2])))
    padr = lambda a, n: jnp.pad(a, ((0, 0), (0, n - a.shape[1]), (0, 0)))
    per_layer = lambda arrs: jnp.stack(arrs, axis=1).reshape((-1,) + arrs[0].shape[1:])
    w_in = ws['w_in']
    out = {
        'p384': per_layer([padc(ws[n], FSP) for n in ('ffn1_w_gate', 'ffn1_w_up', 'ffn2_w_gate', 'ffn2_w_up')]),
        'pd': per_layer([padr(ws[n], FSP) for n in ('ffn1_w_down', 'ffn2_w_down')]),
        'pr': per_layer([w_in[:, :, 0:D], ws['w_out'], ws['ple_w_gate']]),
        'pinl': w_in[:, :, D:D + 2 * LRU_W],
        'ping': padc(w_in[:, :, D + 2 * LRU_W:D_IN], GDN_IN),
        'ppp': ws['ple_w_proj'],
    }
    return {k: a.astype(dtype) for k, a in out.items()}


def _gather_two_level(arrays, name):
    n = len(arrays)

    def body(*refs):
        ins, outs = refs[:n], refs[n:2 * n]
        send_sems, recv_sems, local_sems = refs[2 * n:]
        x, y, c = lax.axis_index("x"), lax.axis_index("y"), lax.axis_index("c")
        me, sibling = (x, y, c), (x, y, 1 - c)
        chips = [(1 - x, y), (x, 1 - y), (1 - x, 1 - y)]
        slot = lambda d: 4 * d[0] + 2 * d[1] + d[2]

        def copy(i, k, block, to, src=None):
            return pltpu.make_async_remote_copy(
                src_ref=outs[i].at[slot(block)] if src is None else src, dst_ref=outs[i].at[slot(block)],
                send_sem=send_sems.at[i, k], recv_sem=recv_sems.at[i, k], device_id=to,
                device_id_type=pl.DeviceIdType.MESH)

        mine, first, passed = [], [], []
        for i in range(n):
            cp = pltpu.make_async_copy(ins[i], outs[i].at[slot(me)], local_sems.at[i])
            cp.start()
            mine.append(cp)
            first.append(copy(i, 0, me, sibling, src=ins[i]))
            first += [copy(i, 1 + j, me, (*chip, c), src=ins[i]) for j, chip in enumerate(chips)]
        for cp in first:
            cp.start()
        for i in range(n):
            for j, chip in enumerate(chips):
                copy(i, 1 + j, (*chip, c), me).wait_recv()
                cp = copy(i, 4 + j, (*chip, c), sibling)
                cp.start()
                passed.append(cp)
        for i in range(n):
            copy(i, 0, sibling, me).wait_recv()
            for j, chip in enumerate(chips):
                copy(i, 4 + j, (*chip, 1 - c), me).wait_recv()
        for cp in first + passed:
            cp.wait_send()
        for cp in mine:
            cp.wait()

    hbm = pl.BlockSpec(memory_space=pltpu.HBM)
    return pl.pallas_call(
        body, in_specs=[hbm] * n, out_specs=[hbm] * n,
        out_shape=[jax.ShapeDtypeStruct((NDEV,) + a.shape, a.dtype) for a in arrays],
        scratch_shapes=[pltpu.SemaphoreType.DMA((n, NDEV - 1)), pltpu.SemaphoreType.DMA((n, NDEV - 1)),
                        pltpu.SemaphoreType.DMA((n,))],
        compiler_params=pltpu.CompilerParams(has_side_effects=True), name=name)(*arrays)


def _scatter_pairs(arrays, name):
    n = len(arrays)

    def body(*refs):
        ins, gots = refs[:n], refs[n:2 * n]
        send_sems, recv_sems = refs[2 * n:]
        x, y, c = lax.axis_index("x"), lax.axis_index("y"), lax.axis_index("c")
        sends = []
        for i in range(n):
            for q in range(4):
                cp = pltpu.make_async_remote_copy(
                    src_ref=ins[i].at[2 * q + 1 - c], dst_ref=gots[i].at[q], send_sem=send_sems.at[i, q],
                    recv_sem=recv_sems.at[i, q], device_id=(x, y, 1 - c), device_id_type=pl.DeviceIdType.MESH)
                cp.start()
                sends.append(cp)
        for cp in sends:
            cp.wait_recv()
        for cp in sends:
            cp.wait_send()

    hbm = pl.BlockSpec(memory_space=pltpu.HBM)
    return pl.pallas_call(
        body, in_specs=[hbm] * n, out_specs=[hbm] * n,
        out_shape=[jax.ShapeDtypeStruct((4,) + a.shape[1:], a.dtype) for a in arrays],
        scratch_shapes=[pltpu.SemaphoreType.DMA((n, 4)), pltpu.SemaphoreType.DMA((n, 4))],
        compiler_params=pltpu.CompilerParams(has_side_effects=True), name=name)(*arrays)


def _pair_sum(own, got, name):
    def body(a_ref, b_ref, o_ref):
        o_ref[...] = (a_ref[...].astype(f32) + b_ref[...].astype(f32)).astype(bf16)

    spec = pl.BlockSpec((None, None) + own.shape[2:], lambda q, s: (q, s, 0, 0))
    return pl.pallas_call(
        body, grid=own.shape[:2], in_specs=[spec, spec], out_specs=spec,
        out_shape=jax.ShapeDtypeStruct(own.shape, bf16),
        compiler_params=_cparams(("arbitrary", "arbitrary")), name=name)(own, got)


def _gather_plan(srcs, lands, x, y, c):
    me = 4 * x + 2 * y + c
    sends, arrivals = [], []
    for j in range(1, NDEV):
        peer, source = (me + j) % NDEV, (me + NDEV - j) % NDEV
        for i in range(len(srcs)):
            k = i * (NDEV - 1) + j - 1
            sends.append((srcs[i], lands[i].at[me], (peer // 4, (peer // 2) % 2, peer % 2), k))
            arrivals.append((srcs[i], lands[i].at[source], (source // 4, (source // 2) % 2, source % 2), k))
    return sends, arrivals


def _chips_plan(srcs, lands, x, y, c):
    chip = 2 * x + y
    sends, arrivals = [], []
    for j in range(1, 4):
        peer, source = (chip + j) % 4, (chip + 4 - j) % 4
        for i in range(len(srcs)):
            k = i * 3 + j - 1
            sends.append((srcs[i].at[peer], lands[i].at[chip], (peer // 2, peer % 2, c), k))
            arrivals.append((srcs[i].at[chip], lands[i].at[source], (source // 2, source % 2, c), k))
    return sends, arrivals


def _remote(entry, send_sems, recv_sems):
    src, dst, dev, k = entry
    return pltpu.make_async_remote_copy(src_ref=src, dst_ref=dst, send_sem=send_sems.at[k], recv_sem=recv_sems.at[k],
                                        device_id=dev, device_id_type=pl.DeviceIdType.MESH)


_HBM = pl.BlockSpec(memory_space=pltpu.HBM)
_SEM = pl.BlockSpec(memory_space=pltpu.SEMAPHORE)


def _split_start(arrays, land_shapes, plan, npeer, name):
    n = len(arrays)

    def body(*refs):
        srcs, lands = refs[:n], refs[n:2 * n]
        send_sems, recv_sems, token = refs[2 * n], refs[2 * n + 1], refs[-1]
        sends, _ = plan(srcs, lands, lax.axis_index("x"), lax.axis_index("y"), lax.axis_index("c"))
        for entry in sends:
            _remote(entry, send_sems, recv_sems).start()
        token[...] = jnp.zeros_like(token)

    lands = [lax.empty(s, a.dtype) for s, a in zip(land_shapes, arrays)]
    thru = [pltpu.HBM(a.shape, a.dtype) for a in arrays + lands]
    out = pl.pallas_call(
        body, name=name, in_specs=[_HBM] * (2 * n),
        out_specs=(_SEM, _SEM, *([_HBM] * (2 * n)), pl.BlockSpec(memory_space=pltpu.VMEM)),
        out_shape=(pltpu.SemaphoreType.DMA((n * npeer,)), pltpu.SemaphoreType.DMA((n * npeer,)), *thru,
                   jax.ShapeDtypeStruct((8, 128), f32)),
        input_output_aliases={i: 2 + i for i in range(2 * n)},
        compiler_params=pltpu.CompilerParams(has_side_effects=pltpu.SideEffectType.DATAFLOW_SIDE_EFFECTING),
    )(*[pltpu.with_memory_space_constraint(a, pltpu.HBM) for a in arrays + lands])
    return out[0], out[1], list(out[2:2 + n]), list(out[2 + n:2 + 2 * n]), out[-1]


def _split_wait(send_sems, recv_sems, srcs, lands, after, plan, name):
    n = len(srcs)

    def body(*refs):
        s_refs, l_refs = refs[:n], refs[n:2 * n]
        ssem, rsem = refs[2 * n], refs[2 * n + 1]
        sends, arrivals = plan(s_refs, l_refs, lax.axis_index("x"), lax.axis_index("y"), lax.axis_index("c"))
        for entry in sends:
            _remote(entry, ssem, rsem).wait_send()
        for entry in arrivals:
            _remote(entry, ssem, rsem).wait_recv()

    out = pl.pallas_call(
        body, name=name, in_specs=[_HBM] * (2 * n) + [_SEM, _SEM, pl.BlockSpec(memory_space=pl.ANY)],
        out_specs=[_HBM] * (2 * n), out_shape=[pltpu.HBM(a.shape, a.dtype) for a in srcs + lands],
        input_output_aliases={i: i for i in range(2 * n)},
        compiler_params=pltpu.CompilerParams(has_side_effects=pltpu.SideEffectType.DATAFLOW_SIDE_EFFECTING),
    )(*srcs, *lands, send_sems, recv_sems, after)
    return list(out[:n]), list(out[n:])


def _adam_math(w, g, m, v):
    m2 = ADAM_B1 * m + (1.0 - ADAM_B1) * g
    v2 = ADAM_B2 * v + (1.0 - ADAM_B2) * (g * g)
    m_hat = m2 / (1.0 - ADAM_B1 ** ADAM_STEP)
    v_hat = v2 / (1.0 - ADAM_B2 ** ADAM_STEP)
    return -ADAM_LR * (m_hat / (jnp.sqrt(v_hat) + ADAM_EPS) + ADAM_WD * w), m2, v2


def _adam_big(parts, w, m, v, anchor, name):
    L, rows, cols = w.shape
    flat = [(a, slot) for layer_parts in parts for a, slot in layer_parts]
    per = len(parts[0])

    def body(*refs):
        prefs = refs[:len(flat)]
        w_ref, m_ref, v_ref, _, g_ref, d_ref, m2_ref, v2_ref = refs[len(flat):]
        for li in range(L):
            @pl.when(pl.program_id(0) == li)
            def _():
                c0 = 0
                for pref in prefs[li * per:(li + 1) * per]:
                    acc = pref[0].astype(f32)
                    for s in range(1, pref.shape[0]):
                        acc = acc + pref[s].astype(f32)
                    width = min(acc.shape[1], cols - c0)
                    g_ref[:, c0:c0 + width] = acc[0:rows, 0:width]
                    c0 += width

        d, m2, v2 = _adam_math(w_ref[...], g_ref[...], m_ref[...], v_ref[...])
        d_ref[...] = d
        m2_ref[...] = m2
        v2_ref[...] = v2

    wspec = pl.BlockSpec((None, rows, cols), lambda l: (l, 0, 0))
    in_specs = [pl.BlockSpec((a.shape[0], None) + a.shape[2:], functools.partial(lambda l, slot: (0, slot, 0, 0), slot=slot))
                for a, slot in flat]
    return pl.pallas_call(
        body, grid=(L,), in_specs=in_specs + [wspec] * 3 + [_full_spec((8, 128))], out_specs=[wspec] * 4,
        out_shape=[jax.ShapeDtypeStruct(w.shape, f32)] * 4,
        compiler_params=_cparams(("arbitrary",)), name=name)(*[a for a, _ in flat], w, m, v, anchor)


def _sum_sources(stacked):
    rows = stacked.shape[1]

    def body(s_ref, o_ref):
        acc = s_ref[0]
        for s in range(1, NDEV):
            acc = acc + s_ref[s]
        o_ref[...] = acc

    return pl.pallas_call(body, out_shape=jax.ShapeDtypeStruct((rows, 128), f32), name="sum_small_grads")(stacked)


def _adam_small(w, g, m, v):
    def body(w_ref, g_ref, m_ref, v_ref, d_ref, m2_ref, v2_ref):
        d, m2, v2 = _adam_math(w_ref[...], g_ref[...], m_ref[...], v_ref[...])
        d_ref[...] = d
        m2_ref[...] = m2
        v2_ref[...] = v2

    return pl.pallas_call(body, out_shape=[jax.ShapeDtypeStruct(w.shape, f32)] * 3, name="adam_small")(w, g, m, v)


def _pack_rows(arrs):
    flat = []
    for a in arrs:
        a = a.reshape(-1)
        flat.append(jnp.pad(a, (0, (-a.shape[0]) % 1024)))
    return jnp.concatenate(flat).reshape(-1, 128)


def _unpack_rows(packed, shapes):
    out, off = [], 0
    flat = packed.reshape(-1)
    for s in shapes:
        n = math.prod(s)
        out.append(flat[off:off + n].reshape(s))
        off += n + (-n) % 1024
    return out


def _gather_conv(gathered, shape):
    L, K, c = shape
    return jnp.transpose(gathered, (1, 2, 0, 3)).reshape(L, K, NDEV * c)


def kernel(x, p, positions, ln_ffn1_g, ln_ffn1_b, ffn1_w_gate, ffn1_w_up, ffn1_w_down, w_in, ret_norm_g, lru_conv_w, lru_conv_b, lru_w_a, lru_b_a, lru_w_x, lru_b_x, lru_lambda, gdn_conv_w, gdn_a_log, gdn_dt_bias, gdn_norm_g, w_out, ln_mix_g, ln_mix_b, ffn2_w_gate, ffn2_w_up, ffn2_w_down, ple_w_gate, ple_w_proj, ln_ffn2_g, ln_ffn2_b, loss_target, m_ln_ffn1_g, m_ln_ffn1_b, m_ffn1_w_gate, m_ffn1_w_up, m_ffn1_w_down, m_w_in, m_ret_norm_g, m_lru_conv_w, m_lru_conv_b, m_lru_w_a, m_lru_b_a, m_lru_w_x, m_lru_b_x, m_lru_lambda, m_gdn_conv_w, m_gdn_a_log, m_gdn_dt_bias, m_gdn_norm_g, m_w_out, m_ln_mix_g, m_ln_mix_b, m_ffn2_w_gate, m_ffn2_w_up, m_ffn2_w_down, m_ple_w_gate, m_ple_w_proj, m_ln_ffn2_g, m_ln_ffn2_b, v_ln_ffn1_g, v_ln_ffn1_b, v_ffn1_w_gate, v_ffn1_w_up, v_ffn1_w_down, v_w_in, v_ret_norm_g, v_lru_conv_w, v_lru_conv_b, v_lru_w_a, v_lru_b_a, v_lru_w_x, v_lru_b_x, v_lru_lambda, v_gdn_conv_w, v_gdn_a_log, v_gdn_dt_bias, v_gdn_norm_g, v_w_out, v_ln_mix_g, v_ln_mix_b, v_ffn2_w_gate, v_ffn2_w_up, v_ffn2_w_down, v_ple_w_gate, v_ple_w_proj, v_ln_ffn2_g, v_ln_ffn2_b):
    args = locals()
    W = {n: args[n] for n in WEIGHTS}
    M = {n: args['m_' + n] for n in WEIGHTS}
    V = {n: args['v_' + n] for n in WEIGHTS}
    me = 4 * lax.axis_index("x") + 2 * lax.axis_index("y") + lax.axis_index("c")

    core = lax.axis_index("c")
    chip = 2 * lax.axis_index("x") + lax.axis_index("y")

    packed = _pack_big(W)

    def group(l, name):
        per = {k: packed[k].shape[0] // DEPTH for k in PACKS}
        if name == 'f1':
            return [packed['p384'][l * per['p384']:l * per['p384'] + 2], packed['pd'][l * per['pd']:l * per['pd'] + 1]]
        return [packed['p384'][l * per['p384'] + 2:(l + 1) * per['p384']],
                packed['pd'][l * per['pd'] + 1:(l + 1) * per['pd']]] + [
                    packed[k][l * per[k]:(l + 1) * per[k]] for k in PACKS[2:]]

    def as_weights(arrs):
        G = dict(zip(PACKS, arrs))
        if 'ppp' in G:
            G['wpp'] = jnp.transpose(G.pop('ppp'), (1, 2, 0, 3)).reshape(PLE, D)
        return G

    conv_pack = _pack_rows([W[n] for n in CONV_SHARDED])
    g0 = _gather_two_level(group(0, 'f1') + [conv_pack], "gather_weights")
    g0, rest0 = lax.optimization_barrier((g0, group(0, 'rest')))
    start0 = _split_start(rest0, [(NDEV,) + a.shape for a in rest0], _gather_plan, NDEV - 1, "gather_start_0")
    tok0, all1 = lax.optimization_barrier((start0[4], group(1, 'f1') + group(1, 'rest')))
    start1 = _split_start(all1, [(NDEV,) + a.shape for a in all1], _gather_plan, NDEV - 1, "gather_start_1")
    arrived = {}

    def gather_done(started, after, name):
        srcs, lands = _split_wait(started[0], started[1], started[2], started[3], after, _gather_plan, name)
        return [lax.dynamic_update_slice_in_dim(ld, s[None], me, axis=0) for s, ld in zip(srcs, lands)]

    def fetch(l, name, after):
        if l == 0 and name == 'f1':
            return as_weights(g0[:-1]), tok0[0, 0] + start1[4][0, 0]
        if l == 0:
            return as_weights(gather_done(start0, after, "gather_wait_0")), 0.0
        if name == 'f1':
            arrived[1] = gather_done(start1, after, "gather_wait_1")
            return as_weights(arrived[1][:2]), 0.0
        return as_weights(arrived[1][2:]), 0.0

    conv_all = g0[-1]
    sm = {n: W[n] for n in SMALL}
    conv_shards = [_unpack_rows(conv_all[s], [W[n].shape for n in CONV_SHARDED]) for s in range(NDEV)]
    for i, n in enumerate(CONV_SHARDED):
        sm[n] = _gather_conv(jnp.stack([cs[i] for cs in conv_shards]), W[n].shape)

    received, started = {}, {}

    small_shapes = [sm[n].shape for n in SMALL]
    small_started = []

    def emit(l, name, grads):
        if name == 'small':
            pack = _pack_rows([grads[n] for n in SMALL])
            small_started.extend(_split_start([pack], [(NDEV,) + pack.shape], _gather_plan, NDEV - 1, "small_start"))
            return 0.0
        keys = list(grads)
        arrs = [grads[k] for k in keys]
        gots = _scatter_pairs(arrs, "scatter_pairs")
        owns = [lax.dynamic_index_in_dim(a.reshape((4, 2) + a.shape[1:]), core, axis=1, keepdims=False) for a in arrs]
        pair = [_pair_sum(o, g, "pair_sum_" + k) for k, o, g in zip(keys, owns, gots)]
        started[l, name] = (keys, _split_start(pair, [a.shape for a in pair], _chips_plan, 3,
                                               f"scatter_start_{l}_{name}"))
        return started[l, name][1][4][0, 0]

    def scatter_done(l, name, after):
        keys, st = started[l, name]
        srcs, lands = _split_wait(st[0], st[1], st[2], st[3], after, _chips_plan, f"scatter_wait_{l}_{name}")
        received[l, name] = dict(zip(keys, [
            lax.dynamic_update_slice_in_dim(ld, lax.dynamic_index_in_dim(s, chip, axis=0), chip, axis=0)
            for s, ld in zip(srcs, lands)]))

    loss, grad_x = _local_step(x[0], p[:, 0], positions.reshape(-1, 1), loss_target[0], fetch, emit, sm)
    loss = lax.psum(loss[0, 0], ("x", "y", "c"))
    last = (0, 'f1')
    for l, name in started:
        if (l, name) != last:
            scatter_done(l, name, grad_x)

    anchor = started[last][1][4]
    srcs, lands = _split_wait(small_started[0], small_started[1], small_started[2], small_started[3], anchor,
                              _gather_plan, "small_wait")
    small_all = lax.dynamic_update_slice_in_dim(lands[0], srcs[0][None], me, axis=0)
    small_sum = _unpack_rows(_sum_sources(small_all), small_shapes)
    grads, delta, new_m, new_v = {}, {}, {}, {}
    for n, g in zip(SMALL, small_sum):
        if n in CONV_SHARDED:
            c = W[n].shape[2]
            g = lax.dynamic_slice_in_dim(g, me * c, c, axis=2)
        grads[n] = g

    big_parts = {
        'ffn1_w_gate': [('f1', 'p384', 0)], 'ffn1_w_up': [('f1', 'p384', 1)], 'ffn1_w_down': [('f1', 'pd', 0)],
        'ffn2_w_gate': [('rest', 'p384', 0)], 'ffn2_w_up': [('rest', 'p384', 1)], 'ffn2_w_down': [('rest', 'pd', 0)],
        'w_in': [('rest', 'pr', 0), ('rest', 'pinl', 0), ('rest', 'ping', 0)], 'w_out': [('rest', 'pr', 1)],
        'ple_w_gate': [('rest', 'pr', 2)], 'ple_w_proj': [('rest', 'ppp', 0)],
    }
    def adam(n):
        parts = [[(received[l, grp][k], slot) for grp, k, slot in big_parts[n]] for l in range(DEPTH)]
        grads[n], delta[n], new_m[n], new_v[n] = _adam_big(parts, W[n], M[n], V[n], anchor, "adam_" + n)

    shapes = [W[n].shape for n in SMALL]
    d_s, m_s, v_s = _adam_small(*[_pack_rows([src[n] for n in SMALL]) for src in (W, grads, M, V)])
    for n, dd, mm, vv in zip(SMALL, _unpack_rows(d_s, shapes), _unpack_rows(m_s, shapes), _unpack_rows(v_s, shapes)):
        delta[n], new_m[n], new_v[n] = dd, mm, vv
    waits_last = [n for n in BIG if big_parts[n][0][0] == last[1]]
    for n in BIG:
        if n not in waits_last:
            adam(n)
    done = jnp.stack([d_s[0, 0]] + [delta[n][0, 0, 0] for n in BIG if n not in waits_last])
    scatter_done(*last, done)
    for n in waits_last:
        adam(n)

    return (loss, grad_x[None], *[grads[n] for n in WEIGHTS], *[delta[n] for n in WEIGHTS],
            *[new_m[n] for n in WEIGHTS], *[new_v[n] for n in WEIGHTS])
```

```python
import functools
import math

import jax
import jax.numpy as jnp
from jax import lax
from jax.experimental import pallas as pl
from jax.experimental.pallas import tpu as pltpu

f32 = jnp.float32
bf16 = jnp.bfloat16

NDEV = 8
DEPTH = 2
D = 1024
FSP = 384
FB = 2
NF = NDEV // FB
PLE = 256
CH = 64
RET_H, GDN_H = 4, 6
RET_W, LRU_W, GDN_W = 256, 384, 384
GDN_IN = 1664
D_IN = 3340
ALPHA = 4.0 ** 0.25
LN_EPS = 1e-5
ROPE_THETA = 10000.0
TM = 512
RB_RET, RB_LRU, RB_GDN = 512, 512, 256
VMEM_LIMIT = 56 * 1024 * 1024
ADAM_LR, ADAM_B1, ADAM_B2, ADAM_EPS, ADAM_WD, ADAM_STEP = 0.001, 0.9, 0.999, 1e-08, 0.01, 10

WEIGHTS = ['ln_ffn1_g', 'ln_ffn1_b', 'ffn1_w_gate', 'ffn1_w_up', 'ffn1_w_down', 'w_in', 'ret_norm_g', 'lru_conv_w',
           'lru_conv_b', 'lru_w_a', 'lru_b_a', 'lru_w_x', 'lru_b_x', 'lru_lambda', 'gdn_conv_w', 'gdn_a_log',
           'gdn_dt_bias', 'gdn_norm_g', 'w_out', 'ln_mix_g', 'ln_mix_b', 'ffn2_w_gate', 'ffn2_w_up', 'ffn2_w_down',
           'ple_w_gate', 'ple_w_proj', 'ln_ffn2_g', 'ln_ffn2_b']
BIG = ['ffn1_w_gate', 'ffn1_w_up', 'ffn1_w_down', 'w_in', 'w_out', 'ffn2_w_gate', 'ffn2_w_up', 'ffn2_w_down',
       'ple_w_gate', 'ple_w_proj']
SMALL = [n for n in WEIGHTS if n not in BIG]
PACKS = ('p384', 'pd', 'pr', 'pinl', 'ping', 'ppp')
CONV_SHARDED = {'lru_conv_w': LRU_W, 'gdn_conv_w': 3 * GDN_W}


def _cparams(sem=None):
    return pltpu.CompilerParams(dimension_semantics=sem, vmem_limit_bytes=VMEM_LIMIT)


def _sigmoid(x):
    return 1.0 / (1.0 + jnp.exp(-x))


def _silu(x):
    return x * _sigmoid(x)


def _dsilu(x):
    s = _sigmoid(x)
    return s * (1.0 + x * (1.0 - s))


def _softplus(x):
    return jnp.maximum(x, 0.0) + jnp.log(1.0 + jnp.exp(-jnp.abs(x)))


def _gelu(x):
    return 0.5 * x * (1.0 + jnp.tanh(0.7978845608028654 * (x + 0.044715 * x * x * x)))


def _dot(a, b):
    return jnp.dot(a.astype(bf16), b.astype(bf16), preferred_element_type=f32)


def _dot_nt(a, b):
    return lax.dot_general(a.astype(bf16), b.astype(bf16), (((1,), (1,)), ((), ())), preferred_element_type=f32)


def _dot_tn(a, b):
    return lax.dot_general(a.astype(bf16), b.astype(bf16), (((0,), (0,)), ((), ())), preferred_element_type=f32)


def _bmm(eq, a, b):
    return jnp.einsum(eq, a.astype(bf16), b.astype(bf16), preferred_element_type=f32)


def _split3(a):
    a1 = a.astype(bf16)
    r = a - a1.astype(f32)
    a2 = r.astype(bf16)
    return a1, a2, (r - a2.astype(f32)).astype(bf16)


def _bmm3(eq, a, b):
    a1, a2, _ = _split3(a)
    b1, b2, _ = _split3(b)
    e = lambda x, y: jnp.einsum(eq, x, y, preferred_element_type=f32)
    return e(a1, b1) + (e(a1, b2) + e(a2, b1))


def _rowsum(x):
    x1, x2, _ = _split3(x)
    ones = jnp.ones((x.shape[0], CH, CH), bf16)
    e = lambda y: jnp.einsum('bij,bjk->bik', y, ones, preferred_element_type=f32)
    return e(x1) + e(x2)


def _tri_ones(B, upper=False):
    ii = lax.broadcasted_iota(jnp.int32, (B, CH, CH), 1)
    jj = lax.broadcasted_iota(jnp.int32, (B, CH, CH), 2)
    return jnp.where((ii <= jj) if upper else (ii >= jj), 1.0, 0.0).astype(bf16)


def _cumsum_mm(t, x):
    x1, x2, x3 = _split3(x)
    e = lambda y: jnp.einsum('bij,bjk->bik', t, y, preferred_element_type=f32)
    return e(x1) + (e(x2) + e(x3))


def _chunk_cumsum(x, reverse=False):
    n = x.shape[0] // CH
    return _cumsum_mm(_tri_ones(n, upper=reverse), x.reshape(n, CH, 128)).reshape(x.shape)


@jax.custom_vjp
def _neumann_inv(m):
    ii = lax.broadcasted_iota(jnp.int32, m.shape, 1)
    jj = lax.broadcasted_iota(jnp.int32, m.shape, 2)
    inv = jnp.where(ii == jj, 1.0, 0.0).astype(f32) + m
    mp = m
    for _ in range(5):
        mp = _bmm3('bij,bjk->bik', mp, mp)
        inv = inv + _bmm3('bij,bjk->bik', inv, mp)
    return inv


def _neumann_inv_fwd(m):
    inv = _neumann_inv(m)
    return inv, inv


def _neumann_inv_bwd(inv, g):
    return (_bmm3('bij,bkj->bik', _bmm3('bji,bjk->bik', inv, g), inv),)


_neumann_inv.defvjp(_neumann_inv_fwd, _neumann_inv_bwd)


@jax.custom_vjp
def _known_inv(m, inv):
    return inv


def _known_inv_fwd(m, inv):
    return inv, inv


def _known_inv_bwd(inv, g):
    return _neumann_inv_bwd(inv, g)[0], jnp.zeros_like(inv)


_known_inv.defvjp(_known_inv_fwd, _known_inv_bwd)


def _ln_stats(z):
    mu = jnp.mean(z, -1, keepdims=True)
    zc = z - mu
    rstd = lax.rsqrt(jnp.mean(zc * zc, -1, keepdims=True) + LN_EPS)
    return zc * rstd, rstd


def _ln_bwd(z, g, dout):
    xh, rstd = _ln_stats(z)
    dxh = dout * g
    dz = rstd * (dxh - jnp.mean(dxh, -1, keepdims=True) - xh * jnp.mean(dxh * xh, -1, keepdims=True))
    return dz, jnp.sum(dout * xh, 0, keepdims=True), jnp.sum(dout, 0, keepdims=True)


def _full_spec(shape):
    nd = len(shape)
    return pl.BlockSpec(shape, lambda *_: (0,) * nd)


def _ffn_fwd(x, p384, pd, lg, lb, slot, which, ple=None):
    T = x.shape[0]
    sg, su, sd = 2 * slot, 2 * slot + 1, slot
    has_ple = ple is not None

    def body(*refs):
        if has_ple:
            (x_ref, wg_ref, wu_ref, wd_ref, lg_ref, lb_ref, p_ref, wpg_ref, wpp_ref,
             z_ref, o_ref, g_ref, u_ref, a_ref, acc, xb_s) = refs
        else:
            x_ref, wg_ref, wu_ref, wd_ref, lg_ref, lb_ref, z_ref, o_ref, g_ref, u_ref, a_ref, acc, xb_s = refs
        f = pl.program_id(1)

        @pl.when(f == 0)
        def _():
            x = x_ref[...]
            xb = x.astype(bf16)
            xb_s[...] = xb
            base = ALPHA * x
            if has_ple:
                gate = _sigmoid(_dot(xb, wpg_ref[...].reshape(D, D)))
                base = base + gate * _dot(p_ref[...], wpp_ref[...])
            acc[...] = base

        xb = xb_s[...]
        g = _dot(xb, wg_ref[...])
        u = _dot(xb, wu_ref[...])
        g_ref[...] = g.astype(bf16)
        u_ref[...] = u.astype(bf16)
        a = (_silu(g) * u).astype(bf16)
        a_ref[...] = a
        acc[...] += 0.5 * _dot(a, wd_ref[...].reshape(FB * FSP, D))

        @pl.when(f == NF - 1)
        def _():
            z = acc[...]
            z_ref[...] = z
            o_ref[...] = _ln_stats(z)[0] * lg_ref[...] + lb_ref[...]

    row = pl.BlockSpec((TM, D), lambda i, f: (i, 0))
    in_specs = [row,
                pl.BlockSpec((None, D, FB * FSP), lambda i, f: (sg, 0, f)),
                pl.BlockSpec((None, D, FB * FSP), lambda i, f: (su, 0, f)),
                pl.BlockSpec((FB, None, FSP, D), lambda i, f: (f, sd, 0, 0)),
                _full_spec((1, D)), _full_spec((1, D))]
    args = [x, p384, p384, pd, lg, lb]
    if has_ple:
        p, pr, wpp = ple
        in_specs += [pl.BlockSpec((TM, PLE), lambda i, f: (i, 0)),
                     pl.BlockSpec((NDEV, None, 128, D), lambda i, f: (0, 2, 0, 0)),
                     _full_spec((PLE, D))]
        args += [p, pr, wpp]
    hid = pl.BlockSpec((TM, FB * FSP), lambda i, f: (i, f))
    hshape = jax.ShapeDtypeStruct((T, NDEV * FSP), bf16)
    return pl.pallas_call(
        body, grid=(T // TM, NF), in_specs=in_specs, out_specs=[row, row, hid, hid, hid],
        out_shape=[jax.ShapeDtypeStruct((T, D), f32)] * 2 + [hshape, hshape, hshape],
        scratch_shapes=[pltpu.VMEM((TM, D), f32), pltpu.VMEM((TM, D), bf16)],
        compiler_params=_cparams(("arbitrary", "arbitrary")), name=f"ffn{which + 1}_fwd")(*args)


def _ffn_bwd(z, dout, gs, us, p384, pd, lg, slot, which):
    T = z.shape[0]
    TMB = TM
    sg, su, sd = 2 * slot, 2 * slot + 1, slot

    def body(z_ref, do_ref, g_ref, u_ref, wg_ref, wu_ref, wd_ref, lg_ref,
             dx_ref, dg_ref, du_ref, dy_ref, dlg_ref, dlb_ref, acc, dyb):
        i, f = pl.program_id(0), pl.program_id(1)

        @pl.when(jnp.logical_and(i == 0, f == 0))
        def _():
            dlg_ref[...] = jnp.zeros_like(dlg_ref)
            dlb_ref[...] = jnp.zeros_like(dlb_ref)

        @pl.when(f == 0)
        def _():
            dz, dlg, dlb = _ln_bwd(z_ref[...], lg_ref[...], do_ref[...])
            dlg_ref[...] += dlg
            dlb_ref[...] += dlb
            dy = (0.5 * dz).astype(bf16)
            dyb[...] = dy
            dy_ref[...] = dy
            acc[...] = ALPHA * dz

        g = g_ref[...].astype(f32)
        u = u_ref[...].astype(f32)
        da = _dot_nt(dyb[...], wd_ref[...].reshape(FB * FSP, D))
        sgm = _sigmoid(g)
        dg = (da * u * (sgm * (1.0 + g * (1.0 - sgm)))).astype(bf16)
        du = (da * (g * sgm)).astype(bf16)
        dg_ref[...] = dg
        du_ref[...] = du
        acc[...] += _dot_nt(dg, wg_ref[...]) + _dot_nt(du, wu_ref[...])

        @pl.when(f == NF - 1)
        def _():
            dx_ref[...] = acc[...]

    row = pl.BlockSpec((TMB, D), lambda i, f: (i, 0))
    hid = pl.BlockSpec((TMB, FB * FSP), lambda i, f: (i, f))
    vec = _full_spec((1, D))
    in_specs = [row, row, hid, hid,
                pl.BlockSpec((None, D, FB * FSP), lambda i, f: (sg, 0, f)),
                pl.BlockSpec((None, D, FB * FSP), lambda i, f: (su, 0, f)),
                pl.BlockSpec((FB, None, FSP, D), lambda i, f: (f, sd, 0, 0)),
                vec]
    args = [z, dout, gs, us, p384, p384, pd, lg]
    out_specs = [row, hid, hid, row, vec, vec]
    hshape = jax.ShapeDtypeStruct((T, NDEV * FSP), bf16)
    out_shape = [jax.ShapeDtypeStruct((T, D), f32), hshape, hshape, jax.ShapeDtypeStruct((T, D), bf16),
                 jax.ShapeDtypeStruct((1, D), f32), jax.ShapeDtypeStruct((1, D), f32)]
    return pl.pallas_call(
        body, grid=(T // TMB, NF), in_specs=in_specs, out_specs=out_specs, out_shape=out_shape,
        scratch_shapes=[pltpu.VMEM((TMB, D), f32), pltpu.VMEM((TMB, D), bf16)],
        compiler_params=_cparams(("arbitrary", "arbitrary")), name=f"ffn{which + 1}_bwd")(*args)


def _ple_bwd(x, p, dy, dx_ffn, pr, wpp, layer):
    T = x.shape[0]

    def body(x_ref, p_ref, dy_ref, dxf_ref, wpg_ref, wpp_ref, dx_ref, dgp_ref, dpj_ref):
        dz = 2.0 * dy_ref[...].astype(f32)
        wpg = wpg_ref[...].reshape(D, D)
        gate = _sigmoid(_dot(x_ref[...], wpg))
        proj = _dot(p_ref[...], wpp_ref[...])
        dgp = (dz * proj * gate * (1.0 - gate)).astype(bf16)
        dgp_ref[...] = dgp
        dpj_ref[...] = (dz * gate).astype(bf16)
        dx_ref[...] = dxf_ref[...] + _dot_nt(dgp, wpg)

    row = pl.BlockSpec((TM, D), lambda i: (i, 0))
    return pl.pallas_call(
        body, grid=(T // TM,),
        in_specs=[row, pl.BlockSpec((TM, PLE), lambda i: (i, 0)), row, row,
                  pl.BlockSpec((NDEV, None, 128, D), lambda i: (0, 3 * layer + 2, 0, 0)), _full_spec((PLE, D))],
        out_specs=[row, row, row],
        out_shape=[jax.ShapeDtypeStruct((T, D), f32), jax.ShapeDtypeStruct((T, D), bf16),
                   jax.ShapeDtypeStruct((T, D), bf16)],
        compiler_params=_cparams(("arbitrary",)), name="ple_bwd")(x, p, dy, dx_ffn, pr, wpp)


def _matmul_tn(a, b, nb, name, nsub=1):
    T, M = a.shape
    N = b.shape[1]
    wide = nsub * nb
    tk = min(T, 1024 if wide <= 2048 else 512)
    nk = T // tk

    def body(a_ref, b_ref, o_ref, acc):
        k = pl.program_id(1)

        @pl.when(k == 0)
        def _():
            acc[...] = jnp.zeros_like(acc)

        acc[...] += _dot_tn(a_ref[...], b_ref[...])

        @pl.when(k == nk - 1)
        def _():
            for j in range(nsub):
                o_ref[j] = acc[:, j * nb:(j + 1) * nb].astype(bf16)

    return pl.pallas_call(
        body, grid=(N // wide, nk),
        in_specs=[pl.BlockSpec((tk, M), lambda n, k: (k, 0)), pl.BlockSpec((tk, wide), lambda n, k: (k, n))],
        out_specs=pl.BlockSpec((nsub, M, nb), lambda n, k: (n, 0, 0)),
        out_shape=jax.ShapeDtypeStruct((N // nb, M, nb), bf16),
        scratch_shapes=[pltpu.VMEM((M, wide), f32)],
        compiler_params=_cparams(("arbitrary", "arbitrary")), name=name)(a, b)


def _proj_in(x, pr, pinl, ping, layer):
    T = x.shape[0]

    def body(x_ref, wr_ref, wl_ref, wg_ref, hr_ref, hl_ref, hg_ref):
        xb = x_ref[...].astype(bf16)
        hr_ref[...] = _dot(xb, wr_ref[...].reshape(D, D))
        hl_ref[...] = _dot(xb, wl_ref[...].reshape(D, 2 * LRU_W))
        hg_ref[...] = _dot(xb, wg_ref[...].reshape(D, GDN_IN))

    return pl.pallas_call(
        body, grid=(T // TM,),
        in_specs=[pl.BlockSpec((TM, D), lambda i: (i, 0)),
                  pl.BlockSpec((NDEV, None, 128, D), lambda i: (0, 3 * layer, 0, 0)),
                  pl.BlockSpec((NDEV, None, 128, 2 * LRU_W), lambda i: (0, layer, 0, 0)),
                  pl.BlockSpec((NDEV, None, 128, GDN_IN), lambda i: (0, layer, 0, 0))],
        out_specs=[pl.BlockSpec((TM, D), lambda i: (i, 0)), pl.BlockSpec((TM, 2 * LRU_W), lambda i: (i, 0)),
                   pl.BlockSpec((TM, GDN_IN), lambda i: (i, 0))],
        out_shape=[jax.ShapeDtypeStruct((T, D), f32), jax.ShapeDtypeStruct((T, 2 * LRU_W), f32),
                   jax.ShapeDtypeStruct((T, GDN_IN), f32)],
        compiler_params=_cparams(("arbitrary",)), name="proj_in")(x, pr, pinl, ping)


def _proj_in_bwd(base, dhr, dhl, dhg, pr, pinl, ping, layer):
    T = base.shape[0]

    def body(b_ref, dr_ref, dl_ref, dg_ref, wr_ref, wl_ref, wg_ref, o_ref):
        o_ref[...] = (b_ref[...] + _dot_nt(dr_ref[...], wr_ref[...].reshape(D, D))
                      + _dot_nt(dl_ref[...], wl_ref[...].reshape(D, 2 * LRU_W))
                      + _dot_nt(dg_ref[...], wg_ref[...].reshape(D, GDN_IN)))

    return pl.pallas_call(
        body, grid=(T // TM,),
        in_specs=[pl.BlockSpec((TM, D), lambda i: (i, 0)), pl.BlockSpec((TM, D), lambda i: (i, 0)),
                  pl.BlockSpec((TM, 2 * LRU_W), lambda i: (i, 0)), pl.BlockSpec((TM, GDN_IN), lambda i: (i, 0)),
                  pl.BlockSpec((NDEV, None, 128, D), lambda i: (0, 3 * layer, 0, 0)),
                  pl.BlockSpec((NDEV, None, 128, 2 * LRU_W), lambda i: (0, layer, 0, 0)),
                  pl.BlockSpec((NDEV, None, 128, GDN_IN), lambda i: (0, layer, 0, 0))],
        out_specs=pl.BlockSpec((TM, D), lambda i: (i, 0)),
        out_shape=jax.ShapeDtypeStruct((T, D), f32),
        compiler_params=_cparams(("arbitrary",)), name="proj_in_bwd")(base, dhr, dhl, dhg, pr, pinl, ping)


def _mix_out(x1, o_r, o_l, o_g, pr, lg, lb, layer):
    T = x1.shape[0]

    def body(x_ref, r_ref, l_ref, g_ref, w_ref, lg_ref, lb_ref, z_ref, o_ref):
        w = w_ref[...].reshape(D, D)
        z = (ALPHA * x_ref[...] + _dot(r_ref[...], w[0:RET_W]) + _dot(l_ref[...], w[RET_W:RET_W + LRU_W])
             + _dot(g_ref[...], w[RET_W + LRU_W:D]))
        z_ref[...] = z
        o_ref[...] = _ln_stats(z)[0] * lg_ref[...] + lb_ref[...]

    row = pl.BlockSpec((TM, D), lambda i: (i, 0))
    return pl.pallas_call(
        body, grid=(T // TM,),
        in_specs=[row, pl.BlockSpec((TM, RET_W), lambda i: (i, 0)), pl.BlockSpec((TM, LRU_W), lambda i: (i, 0)),
                  pl.BlockSpec((TM, GDN_W), lambda i: (i, 0)),
                  pl.BlockSpec((NDEV, None, 128, D), lambda i: (0, 3 * layer + 1, 0, 0)),
                  _full_spec((1, D)), _full_spec((1, D))],
        out_specs=[row, row], out_shape=[jax.ShapeDtypeStruct((T, D), f32)] * 2,
        compiler_params=_cparams(("arbitrary",)), name="mix_out")(x1, o_r, o_l, o_g, pr, lg, lb)


def _mix_out_bwd(z, dout, pr, lg, layer):
    T = z.shape[0]

    def body(z_ref, do_ref, w_ref, lg_ref, dxb_ref, dzb_ref, dr_ref, dl_ref, dg_ref, dlg_ref, dlb_ref):
        @pl.when(pl.program_id(0) == 0)
        def _():
            dlg_ref[...] = jnp.zeros_like(dlg_ref)
            dlb_ref[...] = jnp.zeros_like(dlb_ref)

        dz, dlg, dlb = _ln_bwd(z_ref[...], lg_ref[...], do_ref[...])
        dlg_ref[...] += dlg
        dlb_ref[...] += dlb
        dxb_ref[...] = ALPHA * dz
        dzb = dz.astype(bf16)
        dzb_ref[...] = dzb
        w = w_ref[...].reshape(D, D)
        dr_ref[...] = _dot_nt(dzb, w[0:RET_W])
        dl_ref[...] = _dot_nt(dzb, w[RET_W:RET_W + LRU_W])
        dg_ref[...] = _dot_nt(dzb, w[RET_W + LRU_W:D])

    row = pl.BlockSpec((TM, D), lambda i: (i, 0))
    vec = _full_spec((1, D))
    return pl.pallas_call(
        body, grid=(T // TM,),
        in_specs=[row, row, pl.BlockSpec((NDEV, None, 128, D), lambda i: (0, 3 * layer + 1, 0, 0)), vec],
        out_specs=[row, row, pl.BlockSpec((TM, RET_W), lambda i: (i, 0)), pl.BlockSpec((TM, LRU_W), lambda i: (i, 0)),
                   pl.BlockSpec((TM, GDN_W), lambda i: (i, 0)), vec, vec],
        out_shape=[jax.ShapeDtypeStruct((T, D), f32), jax.ShapeDtypeStruct((T, D), bf16),
                   jax.ShapeDtypeStruct((T, RET_W), f32), jax.ShapeDtypeStruct((T, LRU_W), f32),
                   jax.ShapeDtypeStruct((T, GDN_W), f32), jax.ShapeDtypeStruct((1, D), f32),
                   jax.ShapeDtypeStruct((1, D), f32)],
        compiler_params=_cparams(("arbitrary",)), name="mix_out_bwd")(z, dout, pr, lg)


def _loss_grad(y, target):
    T = y.shape[0]

    def body(y_ref, t_ref, dy_ref, l_ref):
        @pl.when(pl.program_id(0) == 0)
        def _():
            l_ref[...] = jnp.zeros_like(l_ref)

        e = y_ref[...] - t_ref[...]
        dy_ref[...] = e * (1.0 / D)
        l_ref[...] += 0.5 * jnp.sum(jnp.sum(e * e, -1, keepdims=True) * (1.0 / D), 0, keepdims=True)

    row = pl.BlockSpec((TM, D), lambda i: (i, 0))
    return pl.pallas_call(
        body, grid=(T // TM,), in_specs=[row, row], out_specs=[row, _full_spec((1, 1))],
        out_shape=[jax.ShapeDtypeStruct((T, D), f32), jax.ShapeDtypeStruct((1, 1), f32)],
        compiler_params=_cparams(("arbitrary",)), name="loss_grad")(y, target)


def _split_heads(x, H):
    n = x.shape[0] // CH
    parts = [x[:, h * CH:(h + 1) * CH].reshape(n, CH, CH) for h in range(H)]
    return jnp.stack(parts, axis=1).reshape(n * H, CH, CH)


def _merge_heads(ref, x, H, col0=0):
    n = x.shape[0] // H
    x4 = x.reshape(n, H, CH, CH)
    for h in range(H):
        ref[:, col0 + h * CH:col0 + (h + 1) * CH] = x4[:, h].reshape(n * CH, CH)


def _rows_down(x, before, s):
    r8 = lax.broadcasted_iota(jnp.int32, before.shape, 0)
    top = jnp.where(r8 < s, pltpu.roll(before, s, 0), pltpu.roll(x[0:8], s, 0))
    return jnp.concatenate([top, pltpu.roll(x, s, 0)[8:]], axis=0)


def _rows_up(x, after, s):
    R = x.shape[0]
    r8 = lax.broadcasted_iota(jnp.int32, after.shape, 0)
    bottom = jnp.where(r8 >= 8 - s, pltpu.roll(after, 8 - s, 0), pltpu.roll(x[R - 8:R], 8 - s, 0))
    return jnp.concatenate([pltpu.roll(x, R - s, 0)[0:R - 8], bottom], axis=0)


def _conv_fwd(ext, x, tail, w, R):
    ext[0:8, :] = tail
    ext[8:R + 8, :] = x
    y = w[3:4, :] * x
    for k in range(3):
        y = y + w[k:k + 1, :] * _rows_down(x, tail, 3 - k)
    return y


def _conv_bwd(ext, dy, dy_next, w, R):
    x, tail = ext[8:8 + R, :], ext[0:8, :]
    dx = w[3:4, :] * dy
    dws = []
    for k in range(3):
        dx = dx + w[k:k + 1, :] * _rows_up(dy, dy_next, 3 - k)
        dws.append(jnp.sum(dy * _rows_down(x, tail, 3 - k), 0, keepdims=True))
    dws.append(jnp.sum(dy * x, 0, keepdims=True))
    return dx, jnp.concatenate(dws, axis=0)


def _prev_tail_spec(R, W):
    return pl.BlockSpec((8, W), lambda i: (jnp.maximum(i * (R // 8) - 1, 0), 0))


def _prev_tail_spec_rev(R, W, nb):
    return pl.BlockSpec((8, W), lambda i: (jnp.maximum((nb - 1 - i) * (R // 8) - 1, 0), 0))


def _rope_tables(positions):
    T = positions.shape[0]

    def body(p_ref, c_ref, s_ref):
        lane = lax.broadcasted_iota(jnp.int32, (TM, RET_W), 1)
        fi = (lane % 32).astype(f32)
        inv = jnp.exp(fi * (-math.log(ROPE_THETA) / 32.0))
        ang = p_ref[...].astype(f32) * inv
        c_ref[...] = jnp.cos(ang)
        s_ref[...] = jnp.where(lane % CH < 32, -jnp.sin(ang), jnp.sin(ang))

    row = pl.BlockSpec((TM, RET_W), lambda i: (i, 0))
    return pl.pallas_call(
        body, grid=(T // TM,), in_specs=[pl.BlockSpec((TM, 1), lambda i: (i, 0))], out_specs=[row, row],
        out_shape=[jax.ShapeDtypeStruct((T, RET_W), f32)] * 2,
        compiler_params=_cparams(("arbitrary",)), name="rope_tables")(positions)


def _partner(x):
    lane = lax.broadcasted_iota(jnp.int32, x.shape, 1)
    return jnp.where(lane % CH < 32, pltpu.roll(x, RET_W - 32, 1), pltpu.roll(x, 32, 1))


def _ret_consts():
    ii = lax.broadcasted_iota(jnp.int32, (CH, CH), 0).astype(f32)
    jj = lax.broadcasted_iota(jnp.int32, (CH, CH), 1).astype(f32)
    intra, cross, tail, cd = [], [], [], []
    for h in range(RET_H):
        lg = math.log1p(-(2.0 ** (-5.0 - h)))
        intra.append(jnp.exp(jnp.abs(ii - jj) * lg))
        cross.append(jnp.exp((ii + 1.0) * lg))
        tail.append(jnp.exp((CH - 1.0 - ii) * lg))
        cd.append(jnp.full((CH, CH), math.exp(CH * lg), f32))
    return jnp.stack(intra), jnp.stack(cross), jnp.stack(tail), jnp.stack(cd)


def _ret_chunk(consts, q, k, v, st):
    intra, cross, tail, cd = consts
    s = _bmm('hid,hjd->hij', q, k) * intra
    o = _bmm('hij,hje->hie', s, v) + _bmm('hid,hde->hie', q * cross, st)
    st2 = st * cd + _bmm('hjd,hje->hde', k * tail, v)
    oc = o - jnp.mean(o, -1, keepdims=True)
    on = oc * lax.rsqrt(jnp.mean(oc * oc, -1, keepdims=True) + 1e-5)
    return on, st2


def _ret_fwd(hr, cosw, sinw, gam):
    T = hr.shape[0]
    R = RB_RET
    nc = R // CH

    def body(h_ref, c_ref, s_ref, g_ref, o_ref, st_ref, st, wide):
        @pl.when(pl.program_id(0) == 0)
        def _():
            st[...] = jnp.zeros_like(st)

        consts = _ret_consts()
        cw, sw = c_ref[...], s_ref[...]
        q, k = h_ref[:, 0:RET_W], h_ref[:, RET_W:2 * RET_W]
        qh = _split_heads((q * cw + _partner(q) * sw) * 0.125, RET_H)
        kh = _split_heads(k * cw + _partner(k) * sw, RET_H)
        vh = _split_heads(h_ref[:, 2 * RET_W:3 * RET_W], RET_H)
        outs = []
        s_cur = st[...]
        for c in range(nc):
            sl = slice(c * RET_H, (c + 1) * RET_H)
            st_ref[c] = s_cur
            on, s_cur = _ret_chunk(consts, qh[sl], kh[sl], vh[sl], s_cur)
            outs.append(on)
        st[...] = s_cur
        _merge_heads(wide, jnp.concatenate(outs, axis=0), RET_H)
        o_ref[...] = wide[...] * g_ref[...] * _silu(h_ref[:, 3 * RET_W:4 * RET_W])

    blk = pl.BlockSpec((R, RET_W), lambda i: (i, 0))
    return pl.pallas_call(
        body, grid=(T // R,),
        in_specs=[pl.BlockSpec((R, D), lambda i: (i, 0)), blk, blk, _full_spec((1, RET_W))],
        out_specs=[blk, pl.BlockSpec((nc, RET_H, CH, CH), lambda i: (i, 0, 0, 0))],
        out_shape=[jax.ShapeDtypeStruct((T, RET_W), f32), jax.ShapeDtypeStruct((T // CH, RET_H, CH, CH), f32)],
        scratch_shapes=[pltpu.VMEM((RET_H, CH, CH), f32), pltpu.VMEM((R, RET_W), f32)],
        compiler_params=_cparams(("arbitrary",)), name="ret_fwd")(hr, cosw, sinw, gam)


def _ret_bwd(hr, cosw, sinw, gam, states, dout):
    T = hr.shape[0]
    R = RB_RET
    nc = R // CH
    nb = T // R

    def body(h_ref, c_ref, s_ref, g_ref, st_ref, do_ref, dh_ref, dgam_ref, dst, wide):
        @pl.when(pl.program_id(0) == 0)
        def _():
            dst[...] = jnp.zeros_like(dst)
            dgam_ref[...] = jnp.zeros_like(dgam_ref)

        consts = _ret_consts()
        cw, sw = c_ref[...], s_ref[...]
        q, k = h_ref[:, 0:RET_W], h_ref[:, RET_W:2 * RET_W]
        gr = h_ref[:, 3 * RET_W:4 * RET_W]
        qh = _split_heads((q * cw + _partner(q) * sw) * 0.125, RET_H)
        kh = _split_heads(k * cw + _partner(k) * sw, RET_H)
        vh = _split_heads(h_ref[:, 2 * RET_W:3 * RET_W], RET_H)
        do = do_ref[...]
        gam = g_ref[...]
        sg = _silu(gr)
        don = _split_heads(do * gam * sg, RET_H)
        ons, dqs, dks, dvs = [None] * nc, [None] * nc, [None] * nc, [None] * nc
        ds = dst[...]
        for c in reversed(range(nc)):
            sl = slice(c * RET_H, (c + 1) * RET_H)
            (on, _), vjp = jax.vjp(functools.partial(_ret_chunk, consts), qh[sl], kh[sl], vh[sl], st_ref[c])
            dqs[c], dks[c], dvs[c], ds = vjp((don[sl], ds))
            ons[c] = on
        dst[...] = ds
        _merge_heads(wide, jnp.concatenate(ons, axis=0), RET_H)
        onw = wide[...]
        dgam_ref[...] += jnp.sum(do * onw * sg, 0, keepdims=True)
        dh_ref[:, 3 * RET_W:4 * RET_W] = (do * onw * gam * _dsilu(gr)).astype(bf16)
        _merge_heads(wide, jnp.concatenate(dqs, axis=0), RET_H)
        u = wide[...] * 0.125
        dh_ref[:, 0:RET_W] = (u * cw + _partner(u * sw)).astype(bf16)
        _merge_heads(wide, jnp.concatenate(dks, axis=0), RET_H)
        u = wide[...]
        dh_ref[:, RET_W:2 * RET_W] = (u * cw + _partner(u * sw)).astype(bf16)
        _merge_heads(wide, jnp.concatenate(dvs, axis=0), RET_H)
        dh_ref[:, 2 * RET_W:3 * RET_W] = wide[...].astype(bf16)

    blk = pl.BlockSpec((R, RET_W), lambda i: (nb - 1 - i, 0))
    return pl.pallas_call(
        body, grid=(nb,),
        in_specs=[pl.BlockSpec((R, D), lambda i: (nb - 1 - i, 0)), blk, blk, _full_spec((1, RET_W)),
                  pl.BlockSpec((nc, RET_H, CH, CH), lambda i: (nb - 1 - i, 0, 0, 0)), blk],
        out_specs=[pl.BlockSpec((R, D), lambda i: (nb - 1 - i, 0)), _full_spec((1, RET_W))],
        out_shape=[jax.ShapeDtypeStruct((T, D), bf16), jax.ShapeDtypeStruct((1, RET_W), f32)],
        scratch_shapes=[pltpu.VMEM((RET_H, CH, CH), f32), pltpu.VMEM((R, RET_W), f32)],
        compiler_params=_cparams(("arbitrary",)), name="ret_bwd")(hr, cosw, sinw, gam, states, dout)


def _lru_ab(xc, wa, ba, wx, bx, lam):
    r = _sigmoid(_dot(xc, wa) + ba)
    i = _sigmoid(_dot(xc, wx) + bx)
    la = 8.0 * r * (-_softplus(-lam))
    a = jnp.exp(la)
    em = jnp.tanh(la) * (jnp.exp(2.0 * la) + 1.0)
    return a, jnp.sqrt(-em) * (i * xc)


def _lru_out(h, gate):
    return h * _gelu(gate)


def _scan_fwd(a, b):
    R = a.shape[0]
    row = lax.broadcasted_iota(jnp.int32, a.shape, 0)
    d = 1
    while d < R:
        m = row >= d
        b = jnp.where(m, a * pltpu.roll(b, d, 0) + b, b)
        a = jnp.where(m, a * pltpu.roll(a, d, 0), a)
        d *= 2
    return a, b


def _scan_bwd(a, b):
    R = a.shape[0]
    row = lax.broadcasted_iota(jnp.int32, a.shape, 0)
    d = 1
    while d < R:
        m = row < R - d
        b = jnp.where(m, a * pltpu.roll(b, R - d, 0) + b, b)
        a = jnp.where(m, a * pltpu.roll(a, R - d, 0), a)
        d *= 2
    return b


def _lru_fwd(hl, cw, cb, wa, ba, wx, bx, lam):
    T = hl.shape[0]
    R = RB_LRU
    W = LRU_W

    def body(h_ref, t_ref, cw_ref, cb_ref, wa_ref, ba_ref, wx_ref, bx_ref, lam_ref, o_ref, hs_ref, carry, ext):
        first = pl.program_id(0) == 0

        @pl.when(first)
        def _():
            carry[...] = jnp.zeros_like(carry)

        tail = jnp.where(first, 0.0, t_ref[:, 0:W])
        xc = _conv_fwd(ext, h_ref[:, 0:W], tail, cw_ref[...], R) + cb_ref[...]
        a, b = _lru_ab(xc, wa_ref[...], ba_ref[...], wx_ref[...], bx_ref[...], lam_ref[...])
        ap, hloc = _scan_fwd(a, b)
        h = hloc + ap * carry[0:1, :]
        carry[...] = jnp.broadcast_to(h[R - 1:R, :], carry.shape)
        hs_ref[...] = h
        o_ref[...] = _lru_out(h, h_ref[:, W:2 * W])

    vec = _full_spec((1, W))
    blk = pl.BlockSpec((R, W), lambda i: (i, 0))
    return pl.pallas_call(
        body, grid=(T // R,),
        in_specs=[pl.BlockSpec((R, 2 * W), lambda i: (i, 0)), _prev_tail_spec(R, 2 * W), _full_spec((4, W)), vec,
                  _full_spec((W, W)), vec, _full_spec((W, W)), vec, vec],
        out_specs=[blk, blk], out_shape=[jax.ShapeDtypeStruct((T, W), f32)] * 2,
        scratch_shapes=[pltpu.VMEM((8, W), f32), pltpu.VMEM((R + 8, W), f32)],
        compiler_params=_cparams(("arbitrary",)), name="lru_fwd")(hl, hl, cw, cb, wa, ba, wx, bx, lam)


def _lru_bwd(hl, hs, cw, cb, wa, ba, wx, bx, lam, dout):
    T = hl.shape[0]
    R = RB_LRU
    W = LRU_W
    nb = T // R

    def body(h_ref, t_ref, hs_ref, hst_ref, cw_ref, cb_ref, wa_ref, ba_ref, wx_ref, bx_ref, lam_ref, do_ref,
             dh_ref, dcw_ref, dcb_ref, dwa_ref, dba_ref, dwx_ref, dbx_ref, dlam_ref, carry_g, carry_dy, ext):
        i = pl.program_id(0)
        last_blk = i == 0
        first_blk = i == nb - 1

        @pl.when(last_blk)
        def _():
            carry_g[...] = jnp.zeros_like(carry_g)
            carry_dy[...] = jnp.zeros_like(carry_dy)
            for r in (dcw_ref, dcb_ref, dwa_ref, dba_ref, dwx_ref, dbx_ref, dlam_ref):
                r[...] = jnp.zeros_like(r)

        tail = jnp.where(first_blk, 0.0, t_ref[:, 0:W])
        xc = _conv_fwd(ext, h_ref[:, 0:W], tail, cw_ref[...], R) + cb_ref[...]
        (a, _), vjp_ab = jax.vjp(_lru_ab, xc, wa_ref[...], ba_ref[...], wx_ref[...], bx_ref[...], lam_ref[...])
        hs = hs_ref[...]
        _, vjp_out = jax.vjp(_lru_out, hs, h_ref[:, W:2 * W])
        dh, dgate = vjp_out(do_ref[...])
        row = lax.broadcasted_iota(jnp.int32, (R, W), 0)
        dh = jnp.where(row == R - 1, dh + carry_g[0:1, :], dh)
        a_up = jnp.where(row == R - 1, 0.0, pltpu.roll(a, R - 1, 0))
        g = _scan_bwd(a_up, dh)
        carry_g[...] = jnp.broadcast_to(a[0:1, :] * g[0:1, :], carry_g.shape)
        hprev0 = jnp.where(first_blk, 0.0, hst_ref[7:8, :])
        hprev = jnp.where(row == 0, hprev0, pltpu.roll(hs, 1, 0))
        dxc, dwa, dba, dwx, dbx, dlam = vjp_ab((g * hprev, g))
        dwa_ref[...] += dwa
        dba_ref[...] += dba
        dwx_ref[...] += dwx
        dbx_ref[...] += dbx
        dlam_ref[...] += dlam
        dcb_ref[...] += jnp.sum(dxc, 0, keepdims=True)
        dx, dcw = _conv_bwd(ext, dxc, carry_dy[...], cw_ref[...], R)
        carry_dy[...] = dxc[0:8, :]
        dcw_ref[...] += dcw
        dh_ref[:, 0:W] = dx.astype(bf16)
        dh_ref[:, W:2 * W] = dgate.astype(bf16)

    vec = _full_spec((1, W))
    mat = _full_spec((W, W))
    blk = pl.BlockSpec((R, W), lambda i: (nb - 1 - i, 0))
    blk2 = pl.BlockSpec((R, 2 * W), lambda i: (nb - 1 - i, 0))
    return pl.pallas_call(
        body, grid=(nb,),
        in_specs=[blk2, _prev_tail_spec_rev(R, 2 * W, nb), blk, _prev_tail_spec_rev(R, W, nb), _full_spec((4, W)), vec,
                  mat, vec, mat, vec, vec, blk],
        out_specs=[blk2, _full_spec((4, W)), vec, mat, vec, mat, vec, vec],
        out_shape=[jax.ShapeDtypeStruct((T, 2 * W), bf16), jax.ShapeDtypeStruct((4, W), f32),
                   jax.ShapeDtypeStruct((1, W), f32), jax.ShapeDtypeStruct((W, W), f32),
                   jax.ShapeDtypeStruct((1, W), f32), jax.ShapeDtypeStruct((W, W), f32),
                   jax.ShapeDtypeStruct((1, W), f32), jax.ShapeDtypeStruct((1, W), f32)],
        scratch_shapes=[pltpu.VMEM((8, W), f32), pltpu.VMEM((8, W), f32), pltpu.VMEM((R + 8, W), f32)],
        compiler_params=_cparams(("arbitrary",)), name="lru_bwd")(hl, hl, hs, hs, cw, cb, wa, ba, wx, bx, lam, dout)


def _head_ones():
    i = lax.broadcasted_iota(jnp.int32, (GDN_W, GDN_W), 0)
    j = lax.broadcasted_iota(jnp.int32, (GDN_W, GDN_W), 1)
    return jnp.where(jnp.bitwise_xor(i, j) < CH, 1.0, 0.0).astype(bf16)


def _head_sums(x, ones):
    x1, x2, _ = _split3(x)
    return jnp.dot(x1, ones, preferred_element_type=f32) + jnp.dot(x2, ones, preferred_element_type=f32)


def _l2n(y, ones):
    r = lax.rsqrt(_head_sums(y * y, ones) + 1e-6)
    return y * r, r


def _l2n_bwd(dn, n, r, ones):
    return r * (dn - n * _head_sums(dn * n, ones))


def _gdn_local(inverse, q, k, vs, gc, bb):
    B = q.shape[0]
    ii = lax.broadcasted_iota(jnp.int32, (B, CH, CH), 1)
    jj = lax.broadcasted_iota(jnp.int32, (B, CH, CH), 2)
    gct = jnp.swapaxes(gc, 1, 2)
    decay = jnp.where(ii >= jj, jnp.exp(jnp.minimum(gc - gct, 0.0)), 0.0)
    kk = _bmm('bid,bjd->bij', k, k)
    inv = inverse(-jnp.where(ii > jj, bb * kk * decay, 0.0))
    egc = jnp.exp(gc)
    u = _bmm3('bij,bje->bie', inv, vs * bb)
    w = _bmm3('bij,bje->bie', inv, k * (bb * egc))
    qk = _bmm('bid,bjd->bij', q, k) * (0.125 * decay)
    glast = gc[:, CH - 1:CH, :]
    return u, w, qk, q * (0.125 * egc), k * jnp.exp(glast - gc), jnp.exp(jnp.broadcast_to(glast, gc.shape))


def _gdn_step(st, u, w, qk, qd, kt, egl, z, gn):
    vnew = u - _bmm('hcd,hde->hce', w, st)
    o = _bmm('hcd,hde->hce', qd, st) + _bmm('hij,hje->hie', qk, vnew)
    st2 = st * egl + _bmm('hcd,hce->hde', kt, vnew)
    out = o * lax.rsqrt(_rowsum(o * o) * (1.0 / CH) + 1e-6) * gn * _silu(z)
    return out, st2


def _gdn_scalars(ab, alog, dtb):
    sp = _softplus(ab + dtb)
    return -jnp.exp(alog) * sp, _sigmoid(ab)


def _bcast_heads(blk, lane0, H):
    R = blk.shape[0]
    n = R // CH
    parts = [jnp.broadcast_to(blk[:, lane0 + h:lane0 + h + 1], (R, CH)).reshape(n, CH, CH) for h in range(H)]
    return jnp.stack(parts, axis=1).reshape(n * H, CH, CH)


def _unbcast_heads(x, lane0, H):
    n = x.shape[0] // H
    R = n * CH
    s = jnp.sum(x, axis=2, keepdims=True).reshape(n, H, CH, 1)
    lane = lax.broadcasted_iota(jnp.int32, (R, 128), 1)
    acc = jnp.zeros((R, 128), f32)
    for h in range(H):
        acc = acc + jnp.where(lane == lane0 + h, jnp.broadcast_to(s[:, h].reshape(R, 1), (R, 128)), 0.0)
    return acc


def _gdn_fwd(hg, cw, alog, dtb, gn):
    T = hg.shape[0]
    R = RB_GDN
    nc = R // CH
    W3 = 3 * GDN_W
    H = GDN_H

    def body(h_ref, t_ref, cw_ref, al_ref, dt_ref, gn_ref, o_ref, st_ref, inv_ref, st, ext):
        first = pl.program_id(0) == 0

        @pl.when(first)
        def _():
            st[...] = jnp.zeros_like(st)

        def inverse(m):
            inv = _neumann_inv(m)
            inv_ref[...] = inv
            return inv

        tail = jnp.where(first, 0.0, t_ref[:, 0:W3])
        y = _silu(_conv_fwd(ext, h_ref[:, 0:W3], tail, cw_ref[...], R))
        ones = _head_ones()
        qs = _split_heads(_l2n(y[:, 0:GDN_W], ones)[0], H)
        ks = _split_heads(_l2n(y[:, GDN_W:2 * GDN_W], ones)[0], H)
        vs = _split_heads(y[:, 2 * GDN_W:W3], H)
        zh = _split_heads(h_ref[:, W3:W3 + GDN_W], H)
        g, beta = _gdn_scalars(h_ref[:, W3 + GDN_W:GDN_IN], al_ref[...], dt_ref[...])
        loc = _gdn_local(inverse, qs, ks, vs, _bcast_heads(_chunk_cumsum(g), 0, H), _bcast_heads(beta, H, H))
        gnv = gn_ref[...]
        outs = []
        s_cur = st[...]
        for c in range(nc):
            sl = slice(c * H, (c + 1) * H)
            st_ref[c] = s_cur
            out, s_cur = _gdn_step(s_cur, *(t[sl] for t in loc), zh[sl], gnv)
            outs.append(out)
        st[...] = s_cur
        _merge_heads(o_ref, jnp.concatenate(outs, axis=0), H)

    return pl.pallas_call(
        body, grid=(T // R,),
        in_specs=[pl.BlockSpec((R, GDN_IN), lambda i: (i, 0)), _prev_tail_spec(R, GDN_IN), _full_spec((4, W3)),
                  _full_spec((1, 128)), _full_spec((1, 128)), _full_spec((1, CH))],
        out_specs=[pl.BlockSpec((R, GDN_W), lambda i: (i, 0)), pl.BlockSpec((nc, H, CH, CH), lambda i: (i, 0, 0, 0)),
                   pl.BlockSpec((nc * H, CH, CH), lambda i: (i, 0, 0))],
        out_shape=[jax.ShapeDtypeStruct((T, GDN_W), f32), jax.ShapeDtypeStruct((T // CH, H, CH, CH), f32),
                   jax.ShapeDtypeStruct((T // CH * H, CH, CH), f32)],
        scratch_shapes=[pltpu.VMEM((H, CH, CH), f32), pltpu.VMEM((R + 8, W3), f32)],
        compiler_params=_cparams(("arbitrary",)), name="gdn_fwd")(hg, hg, cw, alog, dtb, gn)


def _gdn_bwd(hg, cw, alog, dtb, gn, states, invs, dout):
    T = hg.shape[0]
    R = RB_GDN
    nc = R // CH
    nb = T // R
    W3 = 3 * GDN_W
    H = GDN_H

    def body(h_ref, t_ref, cw_ref, al_ref, dt_ref, gn_ref, st_ref, inv_ref, do_ref,
             dh_ref, dcw_ref, dal_ref, ddt_ref, dgn_ref, dst, carry_dy, ext, wide):
        i = pl.program_id(0)
        first_blk = i == nb - 1

        @pl.when(i == 0)
        def _():
            dst[...] = jnp.zeros_like(dst)
            carry_dy[...] = jnp.zeros_like(carry_dy)
            for r in (dcw_ref, dal_ref, ddt_ref, dgn_ref):
                r[...] = jnp.zeros_like(r)

        tail = jnp.where(first_blk, 0.0, t_ref[:, 0:W3])
        ypre = _conv_fwd(ext, h_ref[:, 0:W3], tail, cw_ref[...], R)
        y = _silu(ypre)
        ones = _head_ones()
        qn, rq = _l2n(y[:, 0:GDN_W], ones)
        kn, rk = _l2n(y[:, GDN_W:2 * GDN_W], ones)
        qs, ks, vs = _split_heads(qn, H), _split_heads(kn, H), _split_heads(y[:, 2 * GDN_W:W3], H)
        zh = _split_heads(h_ref[:, W3:W3 + GDN_W], H)
        ab = h_ref[:, W3 + GDN_W:GDN_IN]
        alog, dtb = al_ref[...], dt_ref[...]
        g, beta = _gdn_scalars(ab, alog, dtb)
        kept = inv_ref[...]
        loc, vjp_loc = jax.vjp(functools.partial(_gdn_local, lambda m: _known_inv(m, kept)), qs, ks, vs,
                               _bcast_heads(_chunk_cumsum(g), 0, H), _bcast_heads(beta, H, H))
        doh = _split_heads(do_ref[...], H)
        gnv = gn_ref[...]
        dloc = [[None] * nc for _ in range(6)]
        dzs = [None] * nc
        ds = dst[...]
        dgn = jnp.zeros((1, CH), f32)
        for c in reversed(range(nc)):
            sl = slice(c * H, (c + 1) * H)
            _, vjp = jax.vjp(_gdn_step, st_ref[c], *(t[sl] for t in loc), zh[sl], gnv)
            grads = vjp((doh[sl], ds))
            ds = grads[0]
            for j in range(6):
                dloc[j][c] = grads[1 + j]
            dzs[c] = grads[7]
            dgn = dgn + grads[8]
        dst[...] = ds
        dgn_ref[...] += dgn
        dqs, dks, dvs, dgb, dbb = vjp_loc(tuple(jnp.concatenate(d, axis=0) for d in dloc))
        lane = lax.broadcasted_iota(jnp.int32, (R, 128), 1)
        dg = _chunk_cumsum(_unbcast_heads(dgb, 0, H), reverse=True)
        dbeta = _unbcast_heads(dbb, H, H)
        da = dg * (-jnp.exp(alog)) * _sigmoid(ab + dtb)
        dh_ref[:, W3 + GDN_W:GDN_IN] = jnp.where(lane < H, da, dbeta * beta * (1.0 - beta)).astype(bf16)
        ddt_ref[...] += jnp.sum(jnp.where(lane < H, da, 0.0), 0, keepdims=True)
        dal_ref[...] += jnp.sum(jnp.where(lane < H, dg * g, 0.0), 0, keepdims=True)
        for j, dpart in enumerate((dqs, dks, dvs)):
            _merge_heads(wide, dpart, H, col0=j * GDN_W)
        wide[:, 0:GDN_W] = _l2n_bwd(wide[:, 0:GDN_W], qn, rq, ones)
        wide[:, GDN_W:2 * GDN_W] = _l2n_bwd(wide[:, GDN_W:2 * GDN_W], kn, rk, ones)
        dy = wide[...] * _dsilu(ypre)
        dx, dcw = _conv_bwd(ext, dy, carry_dy[...], cw_ref[...], R)
        carry_dy[...] = dy[0:8, :]
        dcw_ref[...] += dcw
        dh_ref[:, 0:W3] = dx.astype(bf16)
        _merge_heads(wide, jnp.concatenate(dzs, axis=0), H)
        dh_ref[:, W3:W3 + GDN_W] = wide[:, 0:GDN_W].astype(bf16)

    blk = pl.BlockSpec((R, GDN_IN), lambda i: (nb - 1 - i, 0))
    return pl.pallas_call(
        body, grid=(nb,),
        in_specs=[blk, _prev_tail_spec_rev(R, GDN_IN, nb), _full_spec((4, W3)), _full_spec((1, 128)),
                  _full_spec((1, 128)), _full_spec((1, CH)),
                  pl.BlockSpec((nc, H, CH, CH), lambda i: (nb - 1 - i, 0, 0, 0)),
                  pl.BlockSpec((nc * H, CH, CH), lambda i: (nb - 1 - i, 0, 0)),
                  pl.BlockSpec((R, GDN_W), lambda i: (nb - 1 - i, 0))],
        out_specs=[blk, _full_spec((4, W3)), _full_spec((1, 128)), _full_spec((1, 128)), _full_spec((1, CH))],
        out_shape=[jax.ShapeDtypeStruct((T, GDN_IN), bf16), jax.ShapeDtypeStruct((4, W3), f32),
                   jax.ShapeDtypeStruct((1, 128), f32), jax.ShapeDtypeStruct((1, 128), f32),
                   jax.ShapeDtypeStruct((1, CH), f32)],
        scratch_shapes=[pltpu.VMEM((H, CH, CH), f32), pltpu.VMEM((8, W3), f32), pltpu.VMEM((R + 8, W3), f32),
                        pltpu.VMEM((R, W3), f32)],
        compiler_params=_cparams(("arbitrary",)), name="gdn_bwd")(hg, hg, cw, alog, dtb, gn, states, invs, dout)


def _block_diag(w):
    out = jnp.zeros((LRU_W, LRU_W), w.dtype)
    for g in range(w.shape[0]):
        out = lax.dynamic_update_slice(out, w[g], (g * CH, g * CH))
    return out


def _block_diag_t(w):
    return jnp.stack([w[g * CH:(g + 1) * CH, g * CH:(g + 1) * CH] for g in range(LRU_W // CH)])


def _pad_lanes(v, n=128):
    return jnp.pad(v, (0, n - v.shape[0]))[None, :]


def _local_step(x, p, positions, target, fetch, emit, sm):
    cosw, sinw = _rope_tables(positions)
    cols = lambda w: jnp.transpose(w, (1, 2, 0, 3)).reshape(-1, D, NDEV * FSP)
    saved = []
    h = x
    for l in range(DEPTH):
        v = lambda n: sm[n][l][None, :]
        F1, tok = fetch(l, 'f1', h)
        p384a, pda = cols(F1['p384']), F1['pd']
        z1, x1, g1, u1, a1 = _ffn_fwd(h, p384a, pda, v('ln_ffn1_g') + tok, v('ln_ffn1_b'), 0, 0)
        G, _ = fetch(l, 'rest', x1)
        p384, pd, pr, pinl, ping, wpp = cols(G['p384']), G['pd'], G['pr'], G['pinl'], G['ping'], G['wpp']
        wts = (p384a, pda, p384, pd, pr, pinl, ping, wpp)
        hr, hl, hg = _proj_in(x1, pr, pinl, ping, 0)
        o_r, rst = _ret_fwd(hr, cosw, sinw, v('ret_norm_g'))
        lru_args = (sm['lru_conv_w'][l], v('lru_conv_b'), _block_diag(sm['lru_w_a'][l]), v('lru_b_a'),
                    _block_diag(sm['lru_w_x'][l]), v('lru_b_x'), v('lru_lambda'))
        o_l, hs = _lru_fwd(hl, *lru_args)
        gdn_args = (sm['gdn_conv_w'][l], _pad_lanes(sm['gdn_a_log'][l]), _pad_lanes(sm['gdn_dt_bias'][l]),
                    v('gdn_norm_g'))
        o_g, *gst = _gdn_fwd(hg, *gdn_args)
        z2, x2 = _mix_out(x1, o_r, o_l, o_g, pr, v('ln_mix_g'), v('ln_mix_b'), 0)
        z3, x3, g2, u2, a2 = _ffn_fwd(x2, p384, pd, v('ln_ffn2_g'), v('ln_ffn2_b'), 0, 1, ple=(p[l], pr, wpp))
        saved.append((h, z1, x1, hr, hl, hg, o_r, rst, o_l, hs, lru_args, o_g, gst, gdn_args, z2, x2, z3,
                      g1, u1, a1, g2, u2, a2, wts))
        h = x3
    d, loss = _loss_grad(h, target)

    small = {n: [None] * DEPTH for n in SMALL}
    tok = 0.0
    for l in reversed(range(DEPTH)):
        (x0, z1, x1, hr, hl, hg, o_r, rst, o_l, hs, lru_args, o_g, gst, gdn_args, z2, x2, z3,
         g1, u1, a1, g2, u2, a2, wts) = saved[l]
        p384a, pda, p384, pd, pr, pinl, ping, wpp = wts
        v = lambda n: sm[n][l][None, :]
        rows = lambda m: m.reshape(NDEV, m.shape[1] // NDEV, m.shape[2])
        d2, dg2, du2, dy2, small['ln_ffn2_g'][l], small['ln_ffn2_b'][l] = _ffn_bwd(
            z3, d, g2, u2, p384, pd, v('ln_ffn2_g') + tok, 0, 1)
        d2, dgp, dpj = _ple_bwd(x2, p[l], dy2, d2, pr, wpp, 0)
        dxb, dzb, do_r, do_l, do_g, small['ln_mix_g'][l], small['ln_mix_b'][l] = _mix_out_bwd(
            z2, d2, pr, v('ln_mix_g'), 0)
        dhr, small['ret_norm_g'][l] = _ret_bwd(hr, cosw, sinw, v('ret_norm_g'), rst, do_r)
        (dhl, small['lru_conv_w'][l], small['lru_conv_b'][l], dwa, small['lru_b_a'][l], dwx, small['lru_b_x'][l],
         small['lru_lambda'][l]) = _lru_bwd(hl, hs, *lru_args, do_l)
        small['lru_w_a'][l], small['lru_w_x'][l] = _block_diag_t(dwa), _block_diag_t(dwx)
        dhg, small['gdn_conv_w'][l], dal, ddt, small['gdn_norm_g'][l] = _gdn_bwd(hg, *gdn_args, *gst, do_g)
        small['gdn_a_log'][l], small['gdn_dt_bias'][l] = dal[:, 0:GDN_H], ddt[:, 0:GDN_H]
        d1 = _proj_in_bwd(dxb, dhr, dhl, dhg, pr, pinl, ping, 0)
        dwo = jnp.concatenate([_matmul_tn(o_r, dzb, D, "dw_out_r"), _matmul_tn(o_l, dzb, D, "dw_out_l"),
                               _matmul_tn(o_g, dzb, D, "dw_out_g")], axis=1)
        tok = emit(l, 'rest', {
            'p384': jnp.stack([_matmul_tn(x2, dg2, FSP, "dw_gate", NDEV), _matmul_tn(x2, du2, FSP, "dw_up", NDEV)],
                              axis=1),
            'pd': rows(_matmul_tn(a2, dy2, D, "dw_down"))[:, None],
            'pr': jnp.stack([rows(_matmul_tn(x1, dhr, D, "dw_in_r")), rows(dwo),
                             rows(_matmul_tn(x2, dgp, D, "dw_ple_gate"))], axis=1),
            'pinl': rows(_matmul_tn(x1, dhl, 2 * LRU_W, "dw_in_l"))[:, None],
            'ping': rows(_matmul_tn(x1, dhg, GDN_IN, "dw_in_g"))[:, None],
            'ppp': jnp.transpose(_matmul_tn(p[l], dpj, D, "dw_ple_proj").reshape(PLE, NDEV, 128), (1, 0, 2))[:, None]})
        d, dg1, du1, dy1, small['ln_ffn1_g'][l], small['ln_ffn1_b'][l] = _ffn_bwd(
            z1, d1, g1, u1, p384a, pda, v('ln_ffn1_g') + tok, 0, 0)
        if l == 0:
            emit(l, 'small', {n: jnp.stack([g.reshape(sm[n].shape[1:]) for g in gs]) for n, gs in small.items()})
        tok = emit(l, 'f1', {
            'p384': jnp.stack([_matmul_tn(x0, dg1, FSP, "dw_gate", NDEV), _matmul_tn(x0, du1, FSP, "dw_up", NDEV)],
                              axis=1),
            'pd': rows(_matmul_tn(a1, dy1, D, "dw_down"))[:, None]})
    return loss, d


def _pack_big(ws, dtype=bf16):
    padc = lambda a, n: jnp.pad(a, ((0, 0), (0, 0), (0, n - a.shape[2])))
    padr = lambda a, n: jnp.pad(a, ((0, 0), (0, n - a.shape[1]), (0, 0)))
    per_layer = lambda arrs: jnp.stack(arrs, axis=1).reshape((-1,) + arrs[0].shape[1:])
    w_in = ws['w_in']
    out = {
        'p384': per_layer([padc(ws[n], FSP) for n in ('ffn1_w_gate', 'ffn1_w_up', 'ffn2_w_gate', 'ffn2_w_up')]),
        'pd': per_layer([padr(ws[n], FSP) for n in ('ffn1_w_down', 'ffn2_w_down')]),
        'pr': per_layer([w_in[:, :, 0:D], ws['w_out'], ws['ple_w_gate']]),
        'pinl': w_in[:, :, D:D + 2 * LRU_W],
        'ping': padc(w_in[:, :, D + 2 * LRU_W:D_IN], GDN_IN),
        'ppp': ws['ple_w_proj'],
    }
    return {k: a.astype(dtype) for k, a in out.items()}


def _gather_two_level(arrays, name):
    n = len(arrays)

    def body(*refs):
        ins, outs = refs[:n], refs[n:2 * n]
        send_sems, recv_sems, local_sems = refs[2 * n:]
        x, y, c = lax.axis_index("x"), lax.axis_index("y"), lax.axis_index("c")
        me, sibling = (x, y, c), (x, y, 1 - c)
        chips = [(1 - x, y), (x, 1 - y), (1 - x, 1 - y)]
        slot = lambda d: 4 * d[0] + 2 * d[1] + d[2]

        def copy(i, k, block, to, src=None):
            return pltpu.make_async_remote_copy(
                src_ref=outs[i].at[slot(block)] if src is None else src, dst_ref=outs[i].at[slot(block)],
                send_sem=send_sems.at[i, k], recv_sem=recv_sems.at[i, k], device_id=to,
                device_id_type=pl.DeviceIdType.MESH)

        mine, first, passed = [], [], []
        for i in range(n):
            cp = pltpu.make_async_copy(ins[i], outs[i].at[slot(me)], local_sems.at[i])
            cp.start()
            mine.append(cp)
            first.append(copy(i, 0, me, sibling, src=ins[i]))
            first += [copy(i, 1 + j, me, (*chip, c), src=ins[i]) for j, chip in enumerate(chips)]
        for cp in first:
            cp.start()
        for i in range(n):
            for j, chip in enumerate(chips):
                copy(i, 1 + j, (*chip, c), me).wait_recv()
                cp = copy(i, 4 + j, (*chip, c), sibling)
                cp.start()
                passed.append(cp)
        for i in range(n):
            copy(i, 0, sibling, me).wait_recv()
            for j, chip in enumerate(chips):
                copy(i, 4 + j, (*chip, 1 - c), me).wait_recv()
        for cp in first + passed:
            cp.wait_send()
        for cp in mine:
            cp.wait()

    hbm = pl.BlockSpec(memory_space=pltpu.HBM)
    return pl.pallas_call(
        body, in_specs=[hbm] * n, out_specs=[hbm] * n,
        out_shape=[jax.ShapeDtypeStruct((NDEV,) + a.shape, a.dtype) for a in arrays],
        scratch_shapes=[pltpu.SemaphoreType.DMA((n, NDEV - 1)), pltpu.SemaphoreType.DMA((n, NDEV - 1)),
                        pltpu.SemaphoreType.DMA((n,))],
        compiler_params=pltpu.CompilerParams(has_side_effects=True), name=name)(*arrays)


def _scatter_pairs(arrays, name):
    n = len(arrays)

    def body(*refs):
        ins, gots = refs[:n], refs[n:2 * n]
        send_sems, recv_sems = refs[2 * n:]
        x, y, c = lax.axis_index("x"), lax.axis_index("y"), lax.axis_index("c")
        sends = []
        for i in range(n):
            for q in range(4):
                cp = pltpu.make_async_remote_copy(
                    src_ref=ins[i].at[2 * q + 1 - c], dst_ref=gots[i].at[q], send_sem=send_sems.at[i, q],
                    recv_sem=recv_sems.at[i, q], device_id=(x, y, 1 - c), device_id_type=pl.DeviceIdType.MESH)
                cp.start()
                sends.append(cp)
        for cp in sends:
            cp.wait_recv()
        for cp in sends:
            cp.wait_send()

    hbm = pl.BlockSpec(memory_space=pltpu.HBM)
    return pl.pallas_call(
        body, in_specs=[hbm] * n, out_specs=[hbm] * n,
        out_shape=[jax.ShapeDtypeStruct((4,) + a.shape[1:], a.dtype) for a in arrays],
        scratch_shapes=[pltpu.SemaphoreType.DMA((n, 4)), pltpu.SemaphoreType.DMA((n, 4))],
        compiler_params=pltpu.CompilerParams(has_side_effects=True), name=name)(*arrays)


def _pair_sum(own, got, name):
    def body(a_ref, b_ref, o_ref):
        o_ref[...] = (a_ref[...].astype(f32) + b_ref[...].astype(f32)).astype(bf16)

    spec = pl.BlockSpec((None, None) + own.shape[2:], lambda q, s: (q, s, 0, 0))
    return pl.pallas_call(
        body, grid=own.shape[:2], in_specs=[spec, spec], out_specs=spec,
        out_shape=jax.ShapeDtypeStruct(own.shape, bf16),
        compiler_params=_cparams(("arbitrary", "arbitrary")), name=name)(own, got)


def _gather_plan(srcs, lands, x, y, c):
    me = 4 * x + 2 * y + c
    sends, arrivals = [], []
    for j in range(1, NDEV):
        peer, source = (me + j) % NDEV, (me + NDEV - j) % NDEV
        for i in range(len(srcs)):
            k = i * (NDEV - 1) + j - 1
            sends.append((srcs[i], lands[i].at[me], (peer // 4, (peer // 2) % 2, peer % 2), k))
            arrivals.append((srcs[i], lands[i].at[source], (source // 4, (source // 2) % 2, source % 2), k))
    return sends, arrivals


def _chips_plan(srcs, lands, x, y, c):
    chip = 2 * x + y
    sends, arrivals = [], []
    for j in range(1, 4):
        peer, source = (chip + j) % 4, (chip + 4 - j) % 4
        for i in range(len(srcs)):
            k = i * 3 + j - 1
            sends.append((srcs[i].at[peer], lands[i].at[chip], (peer // 2, peer % 2, c), k))
            arrivals.append((srcs[i].at[chip], lands[i].at[source], (source // 2, source % 2, c), k))
    return sends, arrivals


def _remote(entry, send_sems, recv_sems):
    src, dst, dev, k = entry
    return pltpu.make_async_remote_copy(src_ref=src, dst_ref=dst, send_sem=send_sems.at[k], recv_sem=recv_sems.at[k],
                                        device_id=dev, device_id_type=pl.DeviceIdType.MESH)


_HBM = pl.BlockSpec(memory_space=pltpu.HBM)
_SEM = pl.BlockSpec(memory_space=pltpu.SEMAPHORE)


def _split_start(arrays, land_shapes, plan, npeer, name):
    n = len(arrays)

    def body(*refs):
        srcs, lands = refs[:n], refs[n:2 * n]
        send_sems, recv_sems, token = refs[2 * n], refs[2 * n + 1], refs[-1]
        sends, _ = plan(srcs, lands, lax.axis_index("x"), lax.axis_index("y"), lax.axis_index("c"))
        for entry in sends:
            _remote(entry, send_sems, recv_sems).start()
        token[...] = jnp.zeros_like(token)

    lands = [lax.empty(s, a.dtype) for s, a in zip(land_shapes, arrays)]
    thru = [pltpu.HBM(a.shape, a.dtype) for a in arrays + lands]
    out = pl.pallas_call(
        body, name=name, in_specs=[_HBM] * (2 * n),
        out_specs=(_SEM, _SEM, *([_HBM] * (2 * n)), pl.BlockSpec(memory_space=pltpu.VMEM)),
        out_shape=(pltpu.SemaphoreType.DMA((n * npeer,)), pltpu.SemaphoreType.DMA((n * npeer,)), *thru,
                   jax.ShapeDtypeStruct((8, 128), f32)),
        input_output_aliases={i: 2 + i for i in range(2 * n)},
        compiler_params=pltpu.CompilerParams(has_side_effects=pltpu.SideEffectType.DATAFLOW_SIDE_EFFECTING),
    )(*[pltpu.with_memory_space_constraint(a, pltpu.HBM) for a in arrays + lands])
    return out[0], out[1], list(out[2:2 + n]), list(out[2 + n:2 + 2 * n]), out[-1]


def _split_wait(send_sems, recv_sems, srcs, lands, after, plan, name):
    n = len(srcs)

    def body(*refs):
        s_refs, l_refs = refs[:n], refs[n:2 * n]
        ssem, rsem = refs[2 * n], refs[2 * n + 1]
        sends, arrivals = plan(s_refs, l_refs, lax.axis_index("x"), lax.axis_index("y"), lax.axis_index("c"))
        for entry in sends:
            _remote(entry, ssem, rsem).wait_send()
        for entry in arrivals:
            _remote(entry, ssem, rsem).wait_recv()

    out = pl.pallas_call(
        body, name=name, in_specs=[_HBM] * (2 * n) + [_SEM, _SEM, pl.BlockSpec(memory_space=pl.ANY)],
        out_specs=[_HBM] * (2 * n), out_shape=[pltpu.HBM(a.shape, a.dtype) for a in srcs + lands],
        input_output_aliases={i: i for i in range(2 * n)},
        compiler_params=pltpu.CompilerParams(has_side_effects=pltpu.SideEffectType.DATAFLOW_SIDE_EFFECTING),
    )(*srcs, *lands, send_sems, recv_sems, after)
    return list(out[:n]), list(out[n:])


def _adam_math(w, g, m, v):
    m2 = ADAM_B1 * m + (1.0 - ADAM_B1) * g
    v2 = ADAM_B2 * v + (1.0 - ADAM_B2) * (g * g)
    m_hat = m2 / (1.0 - ADAM_B1 ** ADAM_STEP)
    v_hat = v2 / (1.0 - ADAM_B2 ** ADAM_STEP)
    return -ADAM_LR * (m_hat / (jnp.sqrt(v_hat) + ADAM_EPS) + ADAM_WD * w), m2, v2


def _adam_big(parts, w, m, v, anchor, name):
    L, rows, cols = w.shape
    flat = [(a, slot) for layer_parts in parts for a, slot in layer_parts]
    per = len(parts[0])

    def body(*refs):
        prefs = refs[:len(flat)]
        w_ref, m_ref, v_ref, _, g_ref, d_ref, m2_ref, v2_ref = refs[len(flat):]
        for li in range(L):
            @pl.when(pl.program_id(0) == li)
            def _():
                c0 = 0
                for pref in prefs[li * per:(li + 1) * per]:
                    acc = pref[0].astype(f32)
                    for s in range(1, pref.shape[0]):
                        acc = acc + pref[s].astype(f32)
                    width = min(acc.shape[1], cols - c0)
                    g_ref[:, c0:c0 + width] = acc[0:rows, 0:width]
                    c0 += width

        d, m2, v2 = _adam_math(w_ref[...], g_ref[...], m_ref[...], v_ref[...])
        d_ref[...] = d
        m2_ref[...] = m2
        v2_ref[...] = v2

    wspec = pl.BlockSpec((None, rows, cols), lambda l: (l, 0, 0))
    in_specs = [pl.BlockSpec((a.shape[0], None) + a.shape[2:], functools.partial(lambda l, slot: (0, slot, 0, 0), slot=slot))
                for a, slot in flat]
    return pl.pallas_call(
        body, grid=(L,), in_specs=in_specs + [wspec] * 3 + [_full_spec((8, 128))], out_specs=[wspec] * 4,
        out_shape=[jax.ShapeDtypeStruct(w.shape, f32)] * 4,
        compiler_params=_cparams(("arbitrary",)), name=name)(*[a for a, _ in flat], w, m, v, anchor)


def _sum_sources(stacked):
    rows = stacked.shape[1]

    def body(s_ref, o_ref):
        acc = s_ref[0]
        for s in range(1, NDEV):
            acc = acc + s_ref[s]
        o_ref[...] = acc

    return pl.pallas_call(body, out_shape=jax.ShapeDtypeStruct((rows, 128), f32), name="sum_small_grads")(stacked)


def _adam_small(w, g, m, v):
    def body(w_ref, g_ref, m_ref, v_ref, d_ref, m2_ref, v2_ref):
        d, m2, v2 = _adam_math(w_ref[...], g_ref[...], m_ref[...], v_ref[...])
        d_ref[...] = d
        m2_ref[...] = m2
        v2_ref[...] = v2

    return pl.pallas_call(body, out_shape=[jax.ShapeDtypeStruct(w.shape, f32)] * 3, name="adam_small")(w, g, m, v)


def _pack_rows(arrs):
    flat = []
    for a in arrs:
        a = a.reshape(-1)
        flat.append(jnp.pad(a, (0, (-a.shape[0]) % 1024)))
    return jnp.concatenate(flat).reshape(-1, 128)


def _unpack_rows(packed, shapes):
    out, off = [], 0
    flat = packed.reshape(-1)
    for s in shapes:
        n = math.prod(s)
        out.append(flat[off:off + n].reshape(s))
        off += n + (-n) % 1024
    return out


def _gather_conv(gathered, shape):
    L, K, c = shape
    return jnp.transpose(gathered, (1, 2, 0, 3)).reshape(L, K, NDEV * c)


def kernel(x, p, positions, ln_ffn1_g, ln_ffn1_b, ffn1_w_gate, ffn1_w_up, ffn1_w_down, w_in, ret_norm_g, lru_conv_w, lru_conv_b, lru_w_a, lru_b_a, lru_w_x, lru_b_x, lru_lambda, gdn_conv_w, gdn_a_log, gdn_dt_bias, gdn_norm_g, w_out, ln_mix_g, ln_mix_b, ffn2_w_gate, ffn2_w_up, ffn2_w_down, ple_w_gate, ple_w_proj, ln_ffn2_g, ln_ffn2_b, loss_target, m_ln_ffn1_g, m_ln_ffn1_b, m_ffn1_w_gate, m_ffn1_w_up, m_ffn1_w_down, m_w_in, m_ret_norm_g, m_lru_conv_w, m_lru_conv_b, m_lru_w_a, m_lru_b_a, m_lru_w_x, m_lru_b_x, m_lru_lambda, m_gdn_conv_w, m_gdn_a_log, m_gdn_dt_bias, m_gdn_norm_g, m_w_out, m_ln_mix_g, m_ln_mix_b, m_ffn2_w_gate, m_ffn2_w_up, m_ffn2_w_down, m_ple_w_gate, m_ple_w_proj, m_ln_ffn2_g, m_ln_ffn2_b, v_ln_ffn1_g, v_ln_ffn1_b, v_ffn1_w_gate, v_ffn1_w_up, v_ffn1_w_down, v_w_in, v_ret_norm_g, v_lru_conv_w, v_lru_conv_b, v_lru_w_a, v_lru_b_a, v_lru_w_x, v_lru_b_x, v_lru_lambda, v_gdn_conv_w, v_gdn_a_log, v_gdn_dt_bias, v_gdn_norm_g, v_w_out, v_ln_mix_g, v_ln_mix_b, v_ffn2_w_gate, v_ffn2_w_up, v_ffn2_w_down, v_ple_w_gate, v_ple_w_proj, v_ln_ffn2_g, v_ln_ffn2_b):
    args = locals()
    W = {n: args[n] for n in WEIGHTS}
    M = {n: args['m_' + n] for n in WEIGHTS}
    V = {n: args['v_' + n] for n in WEIGHTS}
    me = 4 * lax.axis_index("x") + 2 * lax.axis_index("y") + lax.axis_index("c")

    core = lax.axis_index("c")
    chip = 2 * lax.axis_index("x") + lax.axis_index("y")

    packed = _pack_big(W)

    def group(l, name):
        per = {k: packed[k].shape[0] // DEPTH for k in PACKS}
        if name == 'f1':
            return [packed['p384'][l * per['p384']:l * per['p384'] + 2], packed['pd'][l * per['pd']:l * per['pd'] + 1]]
        return [packed['p384'][l * per['p384'] + 2:(l + 1) * per['p384']],
                packed['pd'][l * per['pd'] + 1:(l + 1) * per['pd']]] + [
                    packed[k][l * per[k]:(l + 1) * per[k]] for k in PACKS[2:]]

    def as_weights(arrs):
        G = dict(zip(PACKS, arrs))
        if 'ppp' in G:
            G['wpp'] = jnp.transpose(G.pop('ppp'), (1, 2, 0, 3)).reshape(PLE, D)
        return G

    conv_pack = _pack_rows([W[n] for n in CONV_SHARDED])
    g0 = _gather_two_level(group(0, 'f1') + [conv_pack], "gather_weights")
    g0, rest0 = lax.optimization_barrier((g0, group(0, 'rest')))
    start0 = _split_start(rest0, [(NDEV,) + a.shape for a in rest0], _gather_plan, NDEV - 1, "gather_start_0")
    tok0, all1 = lax.optimization_barrier((start0[4], group(1, 'f1') + group(1, 'rest')))
    start1 = _split_start(all1, [(NDEV,) + a.shape for a in all1], _gather_plan, NDEV - 1, "gather_start_1")
    arrived = {}

    def gather_done(started, after, name):
        srcs, lands = _split_wait(started[0], started[1], started[2], started[3], after, _gather_plan, name)
        return [lax.dynamic_update_slice_in_dim(ld, s[None], me, axis=0) for s, ld in zip(srcs, lands)]

    def fetch(l, name, after):
        if l == 0 and name == 'f1':
            return as_weights(g0[:-1]), tok0[0, 0] + start1[4][0, 0]
        if l == 0:
            return as_weights(gather_done(start0, after, "gather_wait_0")), 0.0
        if name == 'f1':
            arrived[1] = gather_done(start1, after, "gather_wait_1")
            return as_weights(arrived[1][:2]), 0.0
        return as_weights(arrived[1][2:]), 0.0

    conv_all = g0[-1]
    sm = {n: W[n] for n in SMALL}
    conv_shards = [_unpack_rows(conv_all[s], [W[n].shape for n in CONV_SHARDED]) for s in range(NDEV)]
    for i, n in enumerate(CONV_SHARDED):
        sm[n] = _gather_conv(jnp.stack([cs[i] for cs in conv_shards]), W[n].shape)

    received, started = {}, {}

    small_shapes = [sm[n].shape for n in SMALL]
    small_started = []

    def emit(l, name, grads):
        if name == 'small':
            pack = _pack_rows([grads[n] for n in SMALL])
            small_started.extend(_split_start([pack], [(NDEV,) + pack.shape], _gather_plan, NDEV - 1, "small_start"))
            return 0.0
        keys = list(grads)
        arrs = [grads[k] for k in keys]
        gots = _scatter_pairs(arrs, "scatter_pairs")
        owns = [lax.dynamic_index_in_dim(a.reshape((4, 2) + a.shape[1:]), core, axis=1, keepdims=False) for a in arrs]
        pair = [_pair_sum(o, g, "pair_sum_" + k) for k, o, g in zip(keys, owns, gots)]
        started[l, name] = (keys, _split_start(pair, [a.shape for a in pair], _chips_plan, 3,
                                               f"scatter_start_{l}_{name}"))
        return started[l, name][1][4][0, 0]

    def scatter_done(l, name, after):
        keys, st = started[l, name]
        srcs, lands = _split_wait(st[0], st[1], st[2], st[3], after, _chips_plan, f"scatter_wait_{l}_{name}")
        received[l, name] = dict(zip(keys, [
            lax.dynamic_update_slice_in_dim(ld, lax.dynamic_index_in_dim(s, chip, axis=0), chip, axis=0)
            for s, ld in zip(srcs, lands)]))

    loss, grad_x = _local_step(x[0], p[:, 0], positions.reshape(-1, 1), loss_target[0], fetch, emit, sm)
    loss = lax.psum(loss[0, 0], ("x", "y", "c"))
    last = (0, 'f1')
    for l, name in started:
        if (l, name) != last:
            scatter_done(l, name, grad_x)

    anchor = started[last][1][4]
    srcs, lands = _split_wait(small_started[0], small_started[1], small_started[2], small_started[3], anchor,
                              _gather_plan, "small_wait")
    small_all = lax.dynamic_update_slice_in_dim(lands[0], srcs[0][None], me, axis=0)
    small_sum = _unpack_rows(_sum_sources(small_all), small_shapes)
    grads, delta, new_m, new_v = {}, {}, {}, {}
    for n, g in zip(SMALL, small_sum):
        if n in CONV_SHARDED:
            c = W[n].shape[2]
            g = lax.dynamic_slice_in_dim(g, me * c, c, axis=2)
        grads[n] = g

    big_parts = {
        'ffn1_w_gate': [('f1', 'p384', 0)], 'ffn1_w_up': [('f1', 'p384', 1)], 'ffn1_w_down': [('f1', 'pd', 0)],
        'ffn2_w_gate': [('rest', 'p384', 0)], 'ffn2_w_up': [('rest', 'p384', 1)], 'ffn2_w_down': [('rest', 'pd', 0)],
        'w_in': [('rest', 'pr', 0), ('rest', 'pinl', 0), ('rest', 'ping', 0)], 'w_out': [('rest', 'pr', 1)],
        'ple_w_gate': [('rest', 'pr', 2)], 'ple_w_proj': [('rest', 'ppp', 0)],
    }
    def adam(n):
        parts = [[(received[l, grp][k], slot) for grp, k, slot in big_parts[n]] for l in range(DEPTH)]
        grads[n], delta[n], new_m[n], new_v[n] = _adam_big(parts, W[n], M[n], V[n], anchor, "adam_" + n)

    shapes = [W[n].shape for n in SMALL]
    d_s, m_s, v_s = _adam_small(*[_pack_rows([src[n] for n in SMALL]) for src in (W, grads, M, V)])
    for n, dd, mm, vv in zip(SMALL, _unpack_rows(d_s, shapes), _unpack_rows(m_s, shapes), _unpack_rows(v_s, shapes)):
        delta[n], new_m[n], new_v[n] = dd, mm, vv
    waits_last = [n for n in BIG if big_parts[n][0][0] == last[1]]
    for n in BIG:
        if n not in waits_last:
            adam(n)
    done = jnp.stack([d_s[0, 0]] + [delta[n][0, 0, 0] for n in BIG if n not in waits_last])
    scatter_done(*last, done)
    for n in waits_last:
        adam(n)

    return (loss, grad_x[None], *[grads[n] for n in WEIGHTS], *[delta[n] for n in WEIGHTS],
            *[new_m[n] for n in WEIGHTS], *[new_v[n] for n in WEIGHTS])
```

```python
import functools
import math

import jax
import jax.numpy as jnp
from jax import lax
from jax.experimental import pallas as pl
from jax.experimental.pallas import tpu as pltpu

f32 = jnp.float32
bf16 = jnp.bfloat16

NDEV = 8
DEPTH = 2
D = 1024
FSP = 384
FB = 2
NF = NDEV // FB
PLE = 256
CH = 64
RET_H, GDN_H = 4, 6
RET_W, LRU_W, GDN_W = 256, 384, 384
GDN_IN = 1664
D_IN = 3340
ALPHA = 4.0 ** 0.25
LN_EPS = 1e-5
ROPE_THETA = 10000.0
TM = 512
RB_RET, RB_LRU, RB_GDN = 512, 512, 256
VMEM_LIMIT = 56 * 1024 * 1024
ADAM_LR, ADAM_B1, ADAM_B2, ADAM_EPS, ADAM_WD, ADAM_STEP = 0.001, 0.9, 0.999, 1e-08, 0.01, 10

WEIGHTS = ['ln_ffn1_g', 'ln_ffn1_b', 'ffn1_w_gate', 'ffn1_w_up', 'ffn1_w_down', 'w_in', 'ret_norm_g', 'lru_conv_w',
           'lru_conv_b', 'lru_w_a', 'lru_b_a', 'lru_w_x', 'lru_b_x', 'lru_lambda', 'gdn_conv_w', 'gdn_a_log',
           'gdn_dt_bias', 'gdn_norm_g', 'w_out', 'ln_mix_g', 'ln_mix_b', 'ffn2_w_gate', 'ffn2_w_up', 'ffn2_w_down',
           'ple_w_gate', 'ple_w_proj', 'ln_ffn2_g', 'ln_ffn2_b']
BIG = ['ffn1_w_gate', 'ffn1_w_up', 'ffn1_w_down', 'w_in', 'w_out', 'ffn2_w_gate', 'ffn2_w_up', 'ffn2_w_down',
       'ple_w_gate', 'ple_w_proj']
SMALL = [n for n in WEIGHTS if n not in BIG]
PACKS = ('p384', 'pd', 'pr', 'pinl', 'ping', 'ppp')
CONV_SHARDED = {'lru_conv_w': LRU_W, 'gdn_conv_w': 3 * GDN_W}


def _cparams(sem=None):
    return pltpu.CompilerParams(dimension_semantics=sem, vmem_limit_bytes=VMEM_LIMIT)


def _sigmoid(x):
    return 1.0 / (1.0 + jnp.exp(-x))


def _silu(x):
    return x * _sigmoid(x)


def _dsilu(x):
    s = _sigmoid(x)
    return s * (1.0 + x * (1.0 - s))


def _softplus(x):
    return jnp.maximum(x, 0.0) + jnp.log(1.0 + jnp.exp(-jnp.abs(x)))


def _gelu(x):
    return 0.5 * x * (1.0 + jnp.tanh(0.7978845608028654 * (x + 0.044715 * x * x * x)))


def _dot(a, b):
    return jnp.dot(a.astype(bf16), b.astype(bf16), preferred_element_type=f32)


def _dot_nt(a, b):
    return lax.dot_general(a.astype(bf16), b.astype(bf16), (((1,), (1,)), ((), ())), preferred_element_type=f32)


def _dot_tn(a, b):
    return lax.dot_general(a.astype(bf16), b.astype(bf16), (((0,), (0,)), ((), ())), preferred_element_type=f32)


def _bmm(eq, a, b):
    return jnp.einsum(eq, a.astype(bf16), b.astype(bf16), preferred_element_type=f32)


def _split3(a):
    a1 = a.astype(bf16)
    r = a - a1.astype(f32)
    a2 = r.astype(bf16)
    return a1, a2, (r - a2.astype(f32)).astype(bf16)


def _bmm3(eq, a, b):
    a1, a2, _ = _split3(a)
    b1, b2, _ = _split3(b)
    e = lambda x, y: jnp.einsum(eq, x, y, preferred_element_type=f32)
    return e(a1, b1) + (e(a1, b2) + e(a2, b1))


def _rowsum(x):
    x1, x2, _ = _split3(x)
    ones = jnp.ones((x.shape[0], CH, CH), bf16)
    e = lambda y: jnp.einsum('bij,bjk->bik', y, ones, preferred_element_type=f32)
    return e(x1) + e(x2)


def _tri_ones(B, upper=False):
    ii = lax.broadcasted_iota(jnp.int32, (B, CH, CH), 1)
    jj = lax.broadcasted_iota(jnp.int32, (B, CH, CH), 2)
    return jnp.where((ii <= jj) if upper else (ii >= jj), 1.0, 0.0).astype(bf16)


def _cumsum_mm(t, x):
    x1, x2, x3 = _split3(x)
    e = lambda y: jnp.einsum('bij,bjk->bik', t, y, preferred_element_type=f32)
    return e(x1) + (e(x2) + e(x3))


def _chunk_cumsum(x, reverse=False):
    n = x.shape[0] // CH
    return _cumsum_mm(_tri_ones(n, upper=reverse), x.reshape(n, CH, 128)).reshape(x.shape)


@jax.custom_vjp
def _neumann_inv(m):
    ii = lax.broadcasted_iota(jnp.int32, m.shape, 1)
    jj = lax.broadcasted_iota(jnp.int32, m.shape, 2)
    inv = jnp.where(ii == jj, 1.0, 0.0).astype(f32) + m
    mp = m
    for _ in range(5):
        mp = _bmm3('bij,bjk->bik', mp, mp)
        inv = inv + _bmm3('bij,bjk->bik', inv, mp)
    return inv


def _neumann_inv_fwd(m):
    inv = _neumann_inv(m)
    return inv, inv


def _neumann_inv_bwd(inv, g):
    return (_bmm3('bij,bkj->bik', _bmm3('bji,bjk->bik', inv, g), inv),)


_neumann_inv.defvjp(_neumann_inv_fwd, _neumann_inv_bwd)


@jax.custom_vjp
def _known_inv(m, inv):
    return inv


def _known_inv_fwd(m, inv):
    return inv, inv


def _known_inv_bwd(inv, g):
    return _neumann_inv_bwd(inv, g)[0], jnp.zeros_like(inv)


_known_inv.defvjp(_known_inv_fwd, _known_inv_bwd)


def _ln_stats(z):
    mu = jnp.mean(z, -1, keepdims=True)
    zc = z - mu
    rstd = lax.rsqrt(jnp.mean(zc * zc, -1, keepdims=True) + LN_EPS)
    return zc * rstd, rstd


def _ln_bwd(z, g, dout):
    xh, rstd = _ln_stats(z)
    dxh = dout * g
    dz = rstd * (dxh - jnp.mean(dxh, -1, keepdims=True) - xh * jnp.mean(dxh * xh, -1, keepdims=True))
    return dz, jnp.sum(dout * xh, 0, keepdims=True), jnp.sum(dout, 0, keepdims=True)


def _full_spec(shape):
    nd = len(shape)
    return pl.BlockSpec(shape, lambda *_: (0,) * nd)


def _ffn_fwd(x, p384, pd, lg, lb, slot, which, ple=None):
    T = x.shape[0]
    sg, su, sd = 2 * slot, 2 * slot + 1, slot
    has_ple = ple is not None

    def body(*refs):
        if has_ple:
            (x_ref, wg_ref, wu_ref, wd_ref, lg_ref, lb_ref, p_ref, wpg_ref, wpp_ref,
             z_ref, o_ref, g_ref, u_ref, a_ref, acc, xb_s) = refs
        else:
            x_ref, wg_ref, wu_ref, wd_ref, lg_ref, lb_ref, z_ref, o_ref, g_ref, u_ref, a_ref, acc, xb_s = refs
        f = pl.program_id(1)

        @pl.when(f == 0)
        def _():
            x = x_ref[...]
            xb = x.astype(bf16)
            xb_s[...] = xb
            base = ALPHA * x
            if has_ple:
                gate = _sigmoid(_dot(xb, wpg_ref[...].reshape(D, D)))
                base = base + gate * _dot(p_ref[...], wpp_ref[...])
            acc[...] = base

        xb = xb_s[...]
        g = _dot(xb, wg_ref[...])
        u = _dot(xb, wu_ref[...])
        g_ref[...] = g.astype(bf16)
        u_ref[...] = u.astype(bf16)
        a = (_silu(g) * u).astype(bf16)
        a_ref[...] = a
        acc[...] += 0.5 * _dot(a, wd_ref[...].reshape(FB * FSP, D))

        @pl.when(f == NF - 1)
        def _():
            z = acc[...]
            z_ref[...] = z
            o_ref[...] = _ln_stats(z)[0] * lg_ref[...] + lb_ref[...]

    row = pl.BlockSpec((TM, D), lambda i, f: (i, 0))
    in_specs = [row,
                pl.BlockSpec((None, D, FB * FSP), lambda i, f: (sg, 0, f)),
                pl.BlockSpec((None, D, FB * FSP), lambda i, f: (su, 0, f)),
                pl.BlockSpec((FB, None, FSP, D), lambda i, f: (f, sd, 0, 0)),
                _full_spec((1, D)), _full_spec((1, D))]
    args = [x, p384, p384, pd, lg, lb]
    if has_ple:
        p, pr, wpp = ple
        in_specs += [pl.BlockSpec((TM, PLE), lambda i, f: (i, 0)),
                     pl.BlockSpec((NDEV, None, 128, D), lambda i, f: (0, 2, 0, 0)),
                     _full_spec((PLE, D))]
        args += [p, pr, wpp]
    hid = pl.BlockSpec((TM, FB * FSP), lambda i, f: (i, f))
    hshape = jax.ShapeDtypeStruct((T, NDEV * FSP), bf16)
    return pl.pallas_call(
        body, grid=(T // TM, NF), in_specs=in_specs, out_specs=[row, row, hid, hid, hid],
        out_shape=[jax.ShapeDtypeStruct((T, D), f32)] * 2 + [hshape, hshape, hshape],
        scratch_shapes=[pltpu.VMEM((TM, D), f32), pltpu.VMEM((TM, D), bf16)],
        compiler_params=_cparams(("arbitrary", "arbitrary")), name=f"ffn{which + 1}_fwd")(*args)


def _ffn_bwd(z, dout, gs, us, p384, pd, lg, slot, which):
    T = z.shape[0]
    TMB = TM
    sg, su, sd = 2 * slot, 2 * slot + 1, slot

    def body(z_ref, do_ref, g_ref, u_ref, wg_ref, wu_ref, wd_ref, lg_ref,
             dx_ref, dg_ref, du_ref, dy_ref, dlg_ref, dlb_ref, acc, dyb):
        i, f = pl.program_id(0), pl.program_id(1)

        @pl.when(jnp.logical_and(i == 0, f == 0))
        def _():
            dlg_ref[...] = jnp.zeros_like(dlg_ref)
            dlb_ref[...] = jnp.zeros_like(dlb_ref)

        @pl.when(f == 0)
        def _():
            dz, dlg, dlb = _ln_bwd(z_ref[...], lg_ref[...], do_ref[...])
            dlg_ref[...] += dlg
            dlb_ref[...] += dlb
            dy = (0.5 * dz).astype(bf16)
            dyb[...] = dy
            dy_ref[...] = dy
            acc[...] = ALPHA * dz

        g = g_ref[...].astype(f32)
        u = u_ref[...].astype(f32)
        da = _dot_nt(dyb[...], wd_ref[...].reshape(FB * FSP, D))
        sgm = _sigmoid(g)
        dg = (da * u * (sgm * (1.0 + g * (1.0 - sgm)))).astype(bf16)
        du = (da * (g * sgm)).astype(bf16)
        dg_ref[...] = dg
        du_ref[...] = du
        acc[...] += _dot_nt(dg, wg_ref[...]) + _dot_nt(du, wu_ref[...])

        @pl.when(f == NF - 1)
        def _():
            dx_ref[...] = acc[...]

    row = pl.BlockSpec((TMB, D), lambda i, f: (i, 0))
    hid = pl.BlockSpec((TMB, FB * FSP), lambda i, f: (i, f))
    vec = _full_spec((1, D))
    in_specs = [row, row, hid, hid,
                pl.BlockSpec((None, D, FB * FSP), lambda i, f: (sg, 0, f)),
                pl.BlockSpec((None, D, FB * FSP), lambda i, f: (su, 0, f)),
                pl.BlockSpec((FB, None, FSP, D), lambda i, f: (f, sd, 0, 0)),
                vec]
    args = [z, dout, gs, us, p384, p384, pd, lg]
    out_specs = [row, hid, hid, row, vec, vec]
    hshape = jax.ShapeDtypeStruct((T, NDEV * FSP), bf16)
    out_shape = [jax.ShapeDtypeStruct((T, D), f32), hshape, hshape, jax.ShapeDtypeStruct((T, D), bf16),
                 jax.ShapeDtypeStruct((1, D), f32), jax.ShapeDtypeStruct((1, D), f32)]
    return pl.pallas_call(
        body, grid=(T // TMB, NF), in_specs=in_specs, out_specs=out_specs, out_shape=out_shape,
        scratch_shapes=[pltpu.VMEM((TMB, D), f32), pltpu.VMEM((TMB, D), bf16)],
        compiler_params=_cparams(("arbitrary", "arbitrary")), name=f"ffn{which + 1}_bwd")(*args)


def _ple_bwd(x, p, dy, dx_ffn, pr, wpp, layer):
    T = x.shape[0]

    def body(x_ref, p_ref, dy_ref, dxf_ref, wpg_ref, wpp_ref, dx_ref, dgp_ref, dpj_ref):
        dz = 2.0 * dy_ref[...].astype(f32)
        wpg = wpg_ref[...].reshape(D, D)
        gate = _sigmoid(_dot(x_ref[...], wpg))
        proj = _dot(p_ref[...], wpp_ref[...])
        dgp = (dz * proj * gate * (1.0 - gate)).astype(bf16)
        dgp_ref[...] = dgp
        dpj_ref[...] = (dz * gate).astype(bf16)
        dx_ref[...] = dxf_ref[...] + _dot_nt(dgp, wpg)

    row = pl.BlockSpec((TM, D), lambda i: (i, 0))
    return pl.pallas_call(
        body, grid=(T // TM,),
        in_specs=[row, pl.BlockSpec((TM, PLE), lambda i: (i, 0)), row, row,
                  pl.BlockSpec((NDEV, None, 128, D), lambda i: (0, 3 * layer + 2, 0, 0)), _full_spec((PLE, D))],
        out_specs=[row, row, row],
        out_shape=[jax.ShapeDtypeStruct((T, D), f32), jax.ShapeDtypeStruct((T, D), bf16),
                   jax.ShapeDtypeStruct((T, D), bf16)],
        compiler_params=_cparams(("arbitrary",)), name="ple_bwd")(x, p, dy, dx_ffn, pr, wpp)


def _matmul_tn(a, b, nb, name, nsub=1):
    T, M = a.shape
    N = b.shape[1]
    wide = nsub * nb
    tk = min(T, 1024 if wide <= 2048 else 512)
    nk = T // tk

    def body(a_ref, b_ref, o_ref, acc):
        k = pl.program_id(1)

        @pl.when(k == 0)
        def _():
            acc[...] = jnp.zeros_like(acc)

        acc[...] += _dot_tn(a_ref[...], b_ref[...])

        @pl.when(k == nk - 1)
        def _():
            for j in range(nsub):
                o_ref[j] = acc[:, j * nb:(j + 1) * nb].astype(bf16)

    return pl.pallas_call(
        body, grid=(N // wide, nk),
        in_specs=[pl.BlockSpec((tk, M), lambda n, k: (k, 0)), pl.BlockSpec((tk, wide), lambda n, k: (k, n))],
        out_specs=pl.BlockSpec((nsub, M, nb), lambda n, k: (n, 0, 0)),
        out_shape=jax.ShapeDtypeStruct((N // nb, M, nb), bf16),
        scratch_shapes=[pltpu.VMEM((M, wide), f32)],
        compiler_params=_cparams(("arbitrary", "arbitrary")), name=name)(a, b)


def _proj_in(x, pr, pinl, ping, layer):
    T = x.shape[0]

    def body(x_ref, wr_ref, wl_ref, wg_ref, hr_ref, hl_ref, hg_ref):
        xb = x_ref[...].astype(bf16)
        hr_ref[...] = _dot(xb, wr_ref[...].reshape(D, D))
        hl_ref[...] = _dot(xb, wl_ref[...].reshape(D, 2 * LRU_W))
        hg_ref[...] = _dot(xb, wg_ref[...].reshape(D, GDN_IN))

    return pl.pallas_call(
        body, grid=(T // TM,),
        in_specs=[pl.BlockSpec((TM, D), lambda i: (i, 0)),
                  pl.BlockSpec((NDEV, None, 128, D), lambda i: (0, 3 * layer, 0, 0)),
                  pl.BlockSpec((NDEV, None, 128, 2 * LRU_W), lambda i: (0, layer, 0, 0)),
                  pl.BlockSpec((NDEV, None, 128, GDN_IN), lambda i: (0, layer, 0, 0))],
        out_specs=[pl.BlockSpec((TM, D), lambda i: (i, 0)), pl.BlockSpec((TM, 2 * LRU_W), lambda i: (i, 0)),
                   pl.BlockSpec((TM, GDN_IN), lambda i: (i, 0))],
        out_shape=[jax.ShapeDtypeStruct((T, D), f32), jax.ShapeDtypeStruct((T, 2 * LRU_W), f32),
                   jax.ShapeDtypeStruct((T, GDN_IN), f32)],
        compiler_params=_cparams(("arbitrary",)), name="proj_in")(x, pr, pinl, ping)


def _proj_in_bwd(base, dhr, dhl, dhg, pr, pinl, ping, layer):
    T = base.shape[0]

    def body(b_ref, dr_ref, dl_ref, dg_ref, wr_ref, wl_ref, wg_ref, o_ref):
        o_ref[...] = (b_ref[...] + _dot_nt(dr_ref[...], wr_ref[...].reshape(D, D))
                      + _dot_nt(dl_ref[...], wl_ref[...].reshape(D, 2 * LRU_W))
                      + _dot_nt(dg_ref[...], wg_ref[...].reshape(D, GDN_IN)))

    return pl.pallas_call(
        body, grid=(T // TM,),
        in_specs=[pl.BlockSpec((TM, D), lambda i: (i, 0)), pl.BlockSpec((TM, D), lambda i: (i, 0)),
                  pl.BlockSpec((TM, 2 * LRU_W), lambda i: (i, 0)), pl.BlockSpec((TM, GDN_IN), lambda i: (i, 0)),
                  pl.BlockSpec((NDEV, None, 128, D), lambda i: (0, 3 * layer, 0, 0)),
                  pl.BlockSpec((NDEV, None, 128, 2 * LRU_W), lambda i: (0, layer, 0, 0)),
                  pl.BlockSpec((NDEV, None, 128, GDN_IN), lambda i: (0, layer, 0, 0))],
        out_specs=pl.BlockSpec((TM, D), lambda i: (i, 0)),
        out_shape=jax.ShapeDtypeStruct((T, D), f32),
        compiler_params=_cparams(("arbitrary",)), name="proj_in_bwd")(base, dhr, dhl, dhg, pr, pinl, ping)


def _mix_out(x1, o_r, o_l, o_g, pr, lg, lb, layer):
    T = x1.shape[0]

    def body(x_ref, r_ref, l_ref, g_ref, w_ref, lg_ref, lb_ref, z_ref, o_ref):
        w = w_ref[...].reshape(D, D)
        z = (ALPHA * x_ref[...] + _dot(r_ref[...], w[0:RET_W]) + _dot(l_ref[...], w[RET_W:RET_W + LRU_W])
             + _dot(g_ref[...], w[RET_W + LRU_W:D]))
        z_ref[...] = z
        o_ref[...] = _ln_stats(z)[0] * lg_ref[...] + lb_ref[...]

    row = pl.BlockSpec((TM, D), lambda i: (i, 0))
    return pl.pallas_call(
        body, grid=(T // TM,),
        in_specs=[row, pl.BlockSpec((TM, RET_W), lambda i: (i, 0)), pl.BlockSpec((TM, LRU_W), lambda i: (i, 0)),
                  pl.BlockSpec((TM, GDN_W), lambda i: (i, 0)),
                  pl.BlockSpec((NDEV, None, 128, D), lambda i: (0, 3 * layer + 1, 0, 0)),
                  _full_spec((1, D)), _full_spec((1, D))],
        out_specs=[row, row], out_shape=[jax.ShapeDtypeStruct((T, D), f32)] * 2,
        compiler_params=_cparams(("arbitrary",)), name="mix_out")(x1, o_r, o_l, o_g, pr, lg, lb)


def _mix_out_bwd(z, dout, pr, lg, layer):
    T = z.shape[0]

    def body(z_ref, do_ref, w_ref, lg_ref, dxb_ref, dzb_ref, dr_ref, dl_ref, dg_ref, dlg_ref, dlb_ref):
        @pl.when(pl.program_id(0) == 0)
        def _():
            dlg_ref[...] = jnp.zeros_like(dlg_ref)
            dlb_ref[...] = jnp.zeros_like(dlb_ref)

        dz, dlg, dlb = _ln_bwd(z_ref[...], lg_ref[...], do_ref[...])
        dlg_ref[...] += dlg
        dlb_ref[...] += dlb
        dxb_ref[...] = ALPHA * dz
        dzb = dz.astype(bf16)
        dzb_ref[...] = dzb
        w = w_ref[...].reshape(D, D)
        dr_ref[...] = _dot_nt(dzb, w[0:RET_W])
        dl_ref[...] = _dot_nt(dzb, w[RET_W:RET_W + LRU_W])
        dg_ref[...] = _dot_nt(dzb, w[RET_W + LRU_W:D])

    row = pl.BlockSpec((TM, D), lambda i: (i, 0))
    vec = _full_spec((1, D))
    return pl.pallas_call(
        body, grid=(T // TM,),
        in_specs=[row, row, pl.BlockSpec((NDEV, None, 128, D), lambda i: (0, 3 * layer + 1, 0, 0)), vec],
        out_specs=[row, row, pl.BlockSpec((TM, RET_W), lambda i: (i, 0)), pl.BlockSpec((TM, LRU_W), lambda i: (i, 0)),
                   pl.BlockSpec((TM, GDN_W), lambda i: (i, 0)), vec, vec],
        out_shape=[jax.ShapeDtypeStruct((T, D), f32), jax.ShapeDtypeStruct((T, D), bf16),
                   jax.ShapeDtypeStruct((T, RET_W), f32), jax.ShapeDtypeStruct((T, LRU_W), f32),
                   jax.ShapeDtypeStruct((T, GDN_W), f32), jax.ShapeDtypeStruct((1, D), f32),
                   jax.ShapeDtypeStruct((1, D), f32)],
        compiler_params=_cparams(("arbitrary",)), name="mix_out_bwd")(z, dout, pr, lg)


def _loss_grad(y, target):
    T = y.shape[0]

    def body(y_ref, t_ref, dy_ref, l_ref):
        @pl.when(pl.program_id(0) == 0)
        def _():
            l_ref[...] = jnp.zeros_like(l_ref)

        e = y_ref[...] - t_ref[...]
        dy_ref[...] = e * (1.0 / D)
        l_ref[...] += 0.5 * jnp.sum(jnp.sum(e * e, -1, keepdims=True) * (1.0 / D), 0, keepdims=True)

    row = pl.BlockSpec((TM, D), lambda i: (i, 0))
    return pl.pallas_call(
        body, grid=(T // TM,), in_specs=[row, row], out_specs=[row, _full_spec((1, 1))],
        out_shape=[jax.ShapeDtypeStruct((T, D), f32), jax.ShapeDtypeStruct((1, 1), f32)],
        compiler_params=_cparams(("arbitrary",)), name="loss_grad")(y, target)


def _split_heads(x, H):
    n = x.shape[0] // CH
    parts = [x[:, h * CH:(h + 1) * CH].reshape(n, CH, CH) for h in range(H)]
    return jnp.stack(parts, axis=1).reshape(n * H, CH, CH)


def _merge_heads(ref, x, H, col0=0):
    n = x.shape[0] // H
    x4 = x.reshape(n, H, CH, CH)
    for h in range(H):
        ref[:, col0 + h * CH:col0 + (h + 1) * CH] = x4[:, h].reshape(n * CH, CH)


def _rows_down(x, before, s):
    r8 = lax.broadcasted_iota(jnp.int32, before.shape, 0)
    top = jnp.where(r8 < s, pltpu.roll(before, s, 0), pltpu.roll(x[0:8], s, 0))
    return jnp.concatenate([top, pltpu.roll(x, s, 0)[8:]], axis=0)


def _rows_up(x, after, s):
    R = x.shape[0]
    r8 = lax.broadcasted_iota(jnp.int32, after.shape, 0)
    bottom = jnp.where(r8 >= 8 - s, pltpu.roll(after, 8 - s, 0), pltpu.roll(x[R - 8:R], 8 - s, 0))
    return jnp.concatenate([pltpu.roll(x, R - s, 0)[0:R - 8], bottom], axis=0)


def _conv_fwd(ext, x, tail, w, R):
    ext[0:8, :] = tail
    ext[8:R + 8, :] = x
    y = w[3:4, :] * x
    for k in range(3):
        y = y + w[k:k + 1, :] * _rows_down(x, tail, 3 - k)
    return y


def _conv_bwd(ext, dy, dy_next, w, R):
    x, tail = ext[8:8 + R, :], ext[0:8, :]
    dx = w[3:4, :] * dy
    dws = []
    for k in range(3):
        dx = dx + w[k:k + 1, :] * _rows_up(dy, dy_next, 3 - k)
        dws.append(jnp.sum(dy * _rows_down(x, tail, 3 - k), 0, keepdims=True))
    dws.append(jnp.sum(dy * x, 0, keepdims=True))
    return dx, jnp.concatenate(dws, axis=0)


def _prev_tail_spec(R, W):
    return pl.BlockSpec((8, W), lambda i: (jnp.maximum(i * (R // 8) - 1, 0), 0))


def _prev_tail_spec_rev(R, W, nb):
    return pl.BlockSpec((8, W), lambda i: (jnp.maximum((nb - 1 - i) * (R // 8) - 1, 0), 0))


def _rope_tables(positions):
    T = positions.shape[0]

    def body(p_ref, c_ref, s_ref):
        lane = lax.broadcasted_iota(jnp.int32, (TM, RET_W), 1)
        fi = (lane % 32).astype(f32)
        inv = jnp.exp(fi * (-math.log(ROPE_THETA) / 32.0))
        ang = p_ref[...].astype(f32) * inv
        c_ref[...] = jnp.cos(ang)
        s_ref[...] = jnp.where(lane % CH < 32, -jnp.sin(ang), jnp.sin(ang))

    row = pl.BlockSpec((TM, RET_W), lambda i: (i, 0))
    return pl.pallas_call(
        body, grid=(T // TM,), in_specs=[pl.BlockSpec((TM, 1), lambda i: (i, 0))], out_specs=[row, row],
        out_shape=[jax.ShapeDtypeStruct((T, RET_W), f32)] * 2,
        compiler_params=_cparams(("arbitrary",)), name="rope_tables")(positions)


def _partner(x):
    lane = lax.broadcasted_iota(jnp.int32, x.shape, 1)
    return jnp.where(lane % CH < 32, pltpu.roll(x, RET_W - 32, 1), pltpu.roll(x, 32, 1))


def _ret_consts():
    ii = lax.broadcasted_iota(jnp.int32, (CH, CH), 0).astype(f32)
    jj = lax.broadcasted_iota(jnp.int32, (CH, CH), 1).astype(f32)
    intra, cross, tail, cd = [], [], [], []
    for h in range(RET_H):
        lg = math.log1p(-(2.0 ** (-5.0 - h)))
        intra.append(jnp.exp(jnp.abs(ii - jj) * lg))
        cross.append(jnp.exp((ii + 1.0) * lg))
        tail.append(jnp.exp((CH - 1.0 - ii) * lg))
        cd.append(jnp.full((CH, CH), math.exp(CH * lg), f32))
    return jnp.stack(intra), jnp.stack(cross), jnp.stack(tail), jnp.stack(cd)


def _ret_chunk(consts, q, k, v, st):
    intra, cross, tail, cd = consts
    s = _bmm('hid,hjd->hij', q, k) * intra
    o = _bmm('hij,hje->hie', s, v) + _bmm('hid,hde->hie', q * cross, st)
    st2 = st * cd + _bmm('hjd,hje->hde', k * tail, v)
    oc = o - jnp.mean(o, -1, keepdims=True)
    on = oc * lax.rsqrt(jnp.mean(oc * oc, -1, keepdims=True) + 1e-5)
    return on, st2


def _ret_fwd(hr, cosw, sinw, gam):
    T = hr.shape[0]
    R = RB_RET
    nc = R // CH

    def body(h_ref, c_ref, s_ref, g_ref, o_ref, st_ref, st, wide):
        @pl.when(pl.program_id(0) == 0)
        def _():
            st[...] = jnp.zeros_like(st)

        consts = _ret_consts()
        cw, sw = c_ref[...], s_ref[...]
        q, k = h_ref[:, 0:RET_W], h_ref[:, RET_W:2 * RET_W]
        qh = _split_heads((q * cw + _partner(q) * sw) * 0.125, RET_H)
        kh = _split_heads(k * cw + _partner(k) * sw, RET_H)
        vh = _split_heads(h_ref[:, 2 * RET_W:3 * RET_W], RET_H)
        outs = []
        s_cur = st[...]
        for c in range(nc):
            sl = slice(c * RET_H, (c + 1) * RET_H)
            st_ref[c] = s_cur
            on, s_cur = _ret_chunk(consts, qh[sl], kh[sl], vh[sl], s_cur)
            outs.append(on)
        st[...] = s_cur
        _merge_heads(wide, jnp.concatenate(outs, axis=0), RET_H)
        o_ref[...] = wide[...] * g_ref[...] * _silu(h_ref[:, 3 * RET_W:4 * RET_W])

    blk = pl.BlockSpec((R, RET_W), lambda i: (i, 0))
    return pl.pallas_call(
        body, grid=(T // R,),
        in_specs=[pl.BlockSpec((R, D), lambda i: (i, 0)), blk, blk, _full_spec((1, RET_W))],
        out_specs=[blk, pl.BlockSpec((nc, RET_H, CH, CH), lambda i: (i, 0, 0, 0))],
        out_shape=[jax.ShapeDtypeStruct((T, RET_W), f32), jax.ShapeDtypeStruct((T // CH, RET_H, CH, CH), f32)],
        scratch_shapes=[pltpu.VMEM((RET_H, CH, CH), f32), pltpu.VMEM((R, RET_W), f32)],
        compiler_params=_cparams(("arbitrary",)), name="ret_fwd")(hr, cosw, sinw, gam)


def _ret_bwd(hr, cosw, sinw, gam, states, dout):
    T = hr.shape[0]
    R = RB_RET
    nc = R // CH
    nb = T // R

    def body(h_ref, c_ref, s_ref, g_ref, st_ref, do_ref, dh_ref, dgam_ref, dst, wide):
        @pl.when(pl.program_id(0) == 0)
        def _():
            dst[...] = jnp.zeros_like(dst)
            dgam_ref[...] = jnp.zeros_like(dgam_ref)

        consts = _ret_consts()
        cw, sw = c_ref[...], s_ref[...]
        q, k = h_ref[:, 0:RET_W], h_ref[:, RET_W:2 * RET_W]
        gr = h_ref[:, 3 * RET_W:4 * RET_W]
        qh = _split_heads((q * cw + _partner(q) * sw) * 0.125, RET_H)
        kh = _split_heads(k * cw + _partner(k) * sw, RET_H)
        vh = _split_heads(h_ref[:, 2 * RET_W:3 * RET_W], RET_H)
        do = do_ref[...]
        gam = g_ref[...]
        sg = _silu(gr)
        don = _split_heads(do * gam * sg, RET_H)
        ons, dqs, dks, dvs = [None] * nc, [None] * nc, [None] * nc, [None] * nc
        ds = dst[...]
        for c in reversed(range(nc)):
            sl = slice(c * RET_H, (c + 1) * RET_H)
            (on, _), vjp = jax.vjp(functools.partial(_ret_chunk, consts), qh[sl], kh[sl], vh[sl], st_ref[c])
            dqs[c], dks[c], dvs[c], ds = vjp((don[sl], ds))
            ons[c] = on
        dst[...] = ds
        _merge_heads(wide, jnp.concatenate(ons, axis=0), RET_H)
        onw = wide[...]
        dgam_ref[...] += jnp.sum(do * onw * sg, 0, keepdims=True)
        dh_ref[:, 3 * RET_W:4 * RET_W] = (do * onw * gam * _dsilu(gr)).astype(bf16)
        _merge_heads(wide, jnp.concatenate(dqs, axis=0), RET_H)
        u = wide[...] * 0.125
        dh_ref[:, 0:RET_W] = (u * cw + _partner(u * sw)).astype(bf16)
        _merge_heads(wide, jnp.concatenate(dks, axis=0), RET_H)
        u = wide[...]
        dh_ref[:, RET_W:2 * RET_W] = (u * cw + _partner(u * sw)).astype(bf16)
        _merge_heads(wide, jnp.concatenate(dvs, axis=0), RET_H)
        dh_ref[:, 2 * RET_W:3 * RET_W] = wide[...].astype(bf16)

    blk = pl.BlockSpec((R, RET_W), lambda i: (nb - 1 - i, 0))
    return pl.pallas_call(
        body, grid=(nb,),
        in_specs=[pl.BlockSpec((R, D), lambda i: (nb - 1 - i, 0)), blk, blk, _full_spec((1, RET_W)),
                  pl.BlockSpec((nc, RET_H, CH, CH), lambda i: (nb - 1 - i, 0, 0, 0)), blk],
        out_specs=[pl.BlockSpec((R, D), lambda i: (nb - 1 - i, 0)), _full_spec((1, RET_W))],
        out_shape=[jax.ShapeDtypeStruct((T, D), bf16), jax.ShapeDtypeStruct((1, RET_W), f32)],
        scratch_shapes=[pltpu.VMEM((RET_H, CH, CH), f32), pltpu.VMEM((R, RET_W), f32)],
        compiler_params=_cparams(("arbitrary",)), name="ret_bwd")(hr, cosw, sinw, gam, states, dout)


def _lru_ab(xc, wa, ba, wx, bx, lam):
    r = _sigmoid(_dot(xc, wa) + ba)
    i = _sigmoid(_dot(xc, wx) + bx)
    la = 8.0 * r * (-_softplus(-lam))
    a = jnp.exp(la)
    em = jnp.tanh(la) * (jnp.exp(2.0 * la) + 1.0)
    return a, jnp.sqrt(-em) * (i * xc)


def _lru_out(h, gate):
    return h * _gelu(gate)


def _scan_fwd(a, b):
    R = a.shape[0]
    row = lax.broadcasted_iota(jnp.int32, a.shape, 0)
    d = 1
    while d < R:
        m = row >= d
        b = jnp.where(m, a * pltpu.roll(b, d, 0) + b, b)
        a = jnp.where(m, a * pltpu.roll(a, d, 0), a)
        d *= 2
    return a, b


def _scan_bwd(a, b):
    R = a.shape[0]
    row = lax.broadcasted_iota(jnp.int32, a.shape, 0)
    d = 1
    while d < R:
        m = row < R - d
        b = jnp.where(m, a * pltpu.roll(b, R - d, 0) + b, b)
        a = jnp.where(m, a * pltpu.roll(a, R - d, 0), a)
        d *= 2
    return b


def _lru_fwd(hl, cw, cb, wa, ba, wx, bx, lam):
    T = hl.shape[0]
    R = RB_LRU
    W = LRU_W

    def body(h_ref, t_ref, cw_ref, cb_ref, wa_ref, ba_ref, wx_ref, bx_ref, lam_ref, o_ref, hs_ref, carry, ext):
        first = pl.program_id(0) == 0

        @pl.when(first)
        def _():
            carry[...] = jnp.zeros_like(carry)

        tail = jnp.where(first, 0.0, t_ref[:, 0:W])
        xc = _conv_fwd(ext, h_ref[:, 0:W], tail, cw_ref[...], R) + cb_ref[...]
        a, b = _lru_ab(xc, wa_ref[...], ba_ref[...], wx_ref[...], bx_ref[...], lam_ref[...])
        ap, hloc = _scan_fwd(a, b)
        h = hloc + ap * carry[0:1, :]
        carry[...] = jnp.broadcast_to(h[R - 1:R, :], carry.shape)
        hs_ref[...] = h
        o_ref[...] = _lru_out(h, h_ref[:, W:2 * W])

    vec = _full_spec((1, W))
    blk = pl.BlockSpec((R, W), lambda i: (i, 0))
    return pl.pallas_call(
        body, grid=(T // R,),
        in_specs=[pl.BlockSpec((R, 2 * W), lambda i: (i, 0)), _prev_tail_spec(R, 2 * W), _full_spec((4, W)), vec,
                  _full_spec((W, W)), vec, _full_spec((W, W)), vec, vec],
        out_specs=[blk, blk], out_shape=[jax.ShapeDtypeStruct((T, W), f32)] * 2,
        scratch_shapes=[pltpu.VMEM((8, W), f32), pltpu.VMEM((R + 8, W), f32)],
        compiler_params=_cparams(("arbitrary",)), name="lru_fwd")(hl, hl, cw, cb, wa, ba, wx, bx, lam)


def _lru_bwd(hl, hs, cw, cb, wa, ba, wx, bx, lam, dout):
    T = hl.shape[0]
    R = RB_LRU
    W = LRU_W
    nb = T // R

    def body(h_ref, t_ref, hs_ref, hst_ref, cw_ref, cb_ref, wa_ref, ba_ref, wx_ref, bx_ref, lam_ref, do_ref,
             dh_ref, dcw_ref, dcb_ref, dwa_ref, dba_ref, dwx_ref, dbx_ref, dlam_ref, carry_g, carry_dy, ext):
        i = pl.program_id(0)
        last_blk = i == 0
        first_blk = i == nb - 1

        @pl.when(last_blk)
        def _():
            carry_g[...] = jnp.zeros_like(carry_g)
            carry_dy[...] = jnp.zeros_like(carry_dy)
            for r in (dcw_ref, dcb_ref, dwa_ref, dba_ref, dwx_ref, dbx_ref, dlam_ref):
                r[...] = jnp.zeros_like(r)

        tail = jnp.where(first_blk, 0.0, t_ref[:, 0:W])
        xc = _conv_fwd(ext, h_ref[:, 0:W], tail, cw_ref[...], R) + cb_ref[...]
        (a, _), vjp_ab = jax.vjp(_lru_ab, xc, wa_ref[...], ba_ref[...], wx_ref[...], bx_ref[...], lam_ref[...])
        hs = hs_ref[...]
        _, vjp_out = jax.vjp(_lru_out, hs, h_ref[:, W:2 * W])
        dh, dgate = vjp_out(do_ref[...])
        row = lax.broadcasted_iota(jnp.int32, (R, W), 0)
        dh = jnp.where(row == R - 1, dh + carry_g[0:1, :], dh)
        a_up = jnp.where(row == R - 1, 0.0, pltpu.roll(a, R - 1, 0))
        g = _scan_bwd(a_up, dh)
        carry_g[...] = jnp.broadcast_to(a[0:1, :] * g[0:1, :], carry_g.shape)
        hprev0 = jnp.where(first_blk, 0.0, hst_ref[7:8, :])
        hprev = jnp.where(row == 0, hprev0, pltpu.roll(hs, 1, 0))
        dxc, dwa, dba, dwx, dbx, dlam = vjp_ab((g * hprev, g))
        dwa_ref[...] += dwa
        dba_ref[...] += dba
        dwx_ref[...] += dwx
        dbx_ref[...] += dbx
        dlam_ref[...] += dlam
        dcb_ref[...] += jnp.sum(dxc, 0, keepdims=True)
        dx, dcw = _conv_bwd(ext, dxc, carry_dy[...], cw_ref[...], R)
        carry_dy[...] = dxc[0:8, :]
        dcw_ref[...] += dcw
        dh_ref[:, 0:W] = dx.astype(bf16)
        dh_ref[:, W:2 * W] = dgate.astype(bf16)

    vec = _full_spec((1, W))
    mat = _full_spec((W, W))
    blk = pl.BlockSpec((R, W), lambda i: (nb - 1 - i, 0))
    blk2 = pl.BlockSpec((R, 2 * W), lambda i: (nb - 1 - i, 0))
    return pl.pallas_call(
        body, grid=(nb,),
        in_specs=[blk2, _prev_tail_spec_rev(R, 2 * W, nb), blk, _prev_tail_spec_rev(R, W, nb), _full_spec((4, W)), vec,
                  mat, vec, mat, vec, vec, blk],
        out_specs=[blk2, _full_spec((4, W)), vec, mat, vec, mat, vec, vec],
        out_shape=[jax.ShapeDtypeStruct((T, 2 * W), bf16), jax.ShapeDtypeStruct((4, W), f32),
                   jax.ShapeDtypeStruct((1, W), f32), jax.ShapeDtypeStruct((W, W), f32),
                   jax.ShapeDtypeStruct((1, W), f32), jax.ShapeDtypeStruct((W, W), f32),
                   jax.ShapeDtypeStruct((1, W), f32), jax.ShapeDtypeStruct((1, W), f32)],
        scratch_shapes=[pltpu.VMEM((8, W), f32), pltpu.VMEM((8, W), f32), pltpu.VMEM((R + 8, W), f32)],
        compiler_params=_cparams(("arbitrary",)), name="lru_bwd")(hl, hl, hs, hs, cw, cb, wa, ba, wx, bx, lam, dout)


def _head_ones():
    i = lax.broadcasted_iota(jnp.int32, (GDN_W, GDN_W), 0)
    j = lax.broadcasted_iota(jnp.int32, (GDN_W, GDN_W), 1)
    return jnp.where(jnp.bitwise_xor(i, j) < CH, 1.0, 0.0).astype(bf16)


def _head_sums(x, ones):
    x1, x2, _ = _split3(x)
    return jnp.dot(x1, ones, preferred_element_type=f32) + jnp.dot(x2, ones, preferred_element_type=f32)


def _l2n(y, ones):
    r = lax.rsqrt(_head_sums(y * y, ones) + 1e-6)
    return y * r, r


def _l2n_bwd(dn, n, r, ones):
    return r * (dn - n * _head_sums(dn * n, ones))


def _gdn_local(inverse, q, k, vs, gc, bb):
    B = q.shape[0]
    ii = lax.broadcasted_iota(jnp.int32, (B, CH, CH), 1)
    jj = lax.broadcasted_iota(jnp.int32, (B, CH, CH), 2)
    gct = jnp.swapaxes(gc, 1, 2)
    decay = jnp.where(ii >= jj, jnp.exp(jnp.minimum(gc - gct, 0.0)), 0.0)
    kk = _bmm('bid,bjd->bij', k, k)
    inv = inverse(-jnp.where(ii > jj, bb * kk * decay, 0.0))
    egc = jnp.exp(gc)
    u = _bmm('bij,bje->bie', inv, vs * bb)
    w = _bmm('bij,bje->bie', inv, k * (bb * egc))
    qk = _bmm('bid,bjd->bij', q, k) * (0.125 * decay)
    glast = gc[:, CH - 1:CH, :]
    return u, w, qk, q * (0.125 * egc), k * jnp.exp(glast - gc), jnp.exp(jnp.broadcast_to(glast, gc.shape))


def _gdn_step(st, u, w, qk, qd, kt, egl, z, gn):
    vnew = u - _bmm('hcd,hde->hce', w, st)
    o = _bmm('hcd,hde->hce', qd, st) + _bmm('hij,hje->hie', qk, vnew)
    st2 = st * egl + _bmm('hcd,hce->hde', kt, vnew)
    out = o * lax.rsqrt(_rowsum(o * o) * (1.0 / CH) + 1e-6) * gn * _silu(z)
    return out, st2


def _gdn_scalars(ab, alog, dtb):
    sp = _softplus(ab + dtb)
    return -jnp.exp(alog) * sp, _sigmoid(ab)


def _bcast_heads(blk, lane0, H):
    R = blk.shape[0]
    n = R // CH
    parts = [jnp.broadcast_to(blk[:, lane0 + h:lane0 + h + 1], (R, CH)).reshape(n, CH, CH) for h in range(H)]
    return jnp.stack(parts, axis=1).reshape(n * H, CH, CH)


def _unbcast_heads(x, lane0, H):
    n = x.shape[0] // H
    R = n * CH
    s = jnp.sum(x, axis=2, keepdims=True).reshape(n, H, CH, 1)
    lane = lax.broadcasted_iota(jnp.int32, (R, 128), 1)
    acc = jnp.zeros((R, 128), f32)
    for h in range(H):
        acc = acc + jnp.where(lane == lane0 + h, jnp.broadcast_to(s[:, h].reshape(R, 1), (R, 128)), 0.0)
    return acc


def _gdn_fwd(hg, cw, alog, dtb, gn):
    T = hg.shape[0]
    R = RB_GDN
    nc = R // CH
    W3 = 3 * GDN_W
    H = GDN_H

    def body(h_ref, t_ref, cw_ref, al_ref, dt_ref, gn_ref, o_ref, st_ref, inv_ref, st, ext):
        first = pl.program_id(0) == 0

        @pl.when(first)
        def _():
            st[...] = jnp.zeros_like(st)

        def inverse(m):
            inv = _neumann_inv(m)
            inv_ref[...] = inv
            return inv

        tail = jnp.where(first, 0.0, t_ref[:, 0:W3])
        y = _silu(_conv_fwd(ext, h_ref[:, 0:W3], tail, cw_ref[...], R))
        ones = _head_ones()
        qs = _split_heads(_l2n(y[:, 0:GDN_W], ones)[0], H)
        ks = _split_heads(_l2n(y[:, GDN_W:2 * GDN_W], ones)[0], H)
        vs = _split_heads(y[:, 2 * GDN_W:W3], H)
        zh = _split_heads(h_ref[:, W3:W3 + GDN_W], H)
        g, beta = _gdn_scalars(h_ref[:, W3 + GDN_W:GDN_IN], al_ref[...], dt_ref[...])
        loc = _gdn_local(inverse, qs, ks, vs, _bcast_heads(_chunk_cumsum(g), 0, H), _bcast_heads(beta, H, H))
        gnv = gn_ref[...]
        outs = []
        s_cur = st[...]
        for c in range(nc):
            sl = slice(c * H, (c + 1) * H)
            st_ref[c] = s_cur
            out, s_cur = _gdn_step(s_cur, *(t[sl] for t in loc), zh[sl], gnv)
            outs.append(out)
        st[...] = s_cur
        _merge_heads(o_ref, jnp.concatenate(outs, axis=0), H)

    return pl.pallas_call(
        body, grid=(T // R,),
        in_specs=[pl.BlockSpec((R, GDN_IN), lambda i: (i, 0)), _prev_tail_spec(R, GDN_IN), _full_spec((4, W3)),
                  _full_spec((1, 128)), _full_spec((1, 128)), _full_spec((1, CH))],
        out_specs=[pl.BlockSpec((R, GDN_W), lambda i: (i, 0)), pl.BlockSpec((nc, H, CH, CH), lambda i: (i, 0, 0, 0)),
                   pl.BlockSpec((nc * H, CH, CH), lambda i: (i, 0, 0))],
        out_shape=[jax.ShapeDtypeStruct((T, GDN_W), f32), jax.ShapeDtypeStruct((T // CH, H, CH, CH), f32),
                   jax.ShapeDtypeStruct((T // CH * H, CH, CH), f32)],
        scratch_shapes=[pltpu.VMEM((H, CH, CH), f32), pltpu.VMEM((R + 8, W3), f32)],
        compiler_params=_cparams(("arbitrary",)), name="gdn_fwd")(hg, hg, cw, alog, dtb, gn)


def _gdn_bwd(hg, cw, alog, dtb, gn, states, invs, dout):
    T = hg.shape[0]
    R = RB_GDN
    nc = R // CH
    nb = T // R
    W3 = 3 * GDN_W
    H = GDN_H

    def body(h_ref, t_ref, cw_ref, al_ref, dt_ref, gn_ref, st_ref, inv_ref, do_ref,
             dh_ref, dcw_ref, dal_ref, ddt_ref, dgn_ref, dst, carry_dy, ext, wide):
        i = pl.program_id(0)
        first_blk = i == nb - 1

        @pl.when(i == 0)
        def _():
            dst[...] = jnp.zeros_like(dst)
            carry_dy[...] = jnp.zeros_like(carry_dy)
            for r in (dcw_ref, dal_ref, ddt_ref, dgn_ref):
                r[...] = jnp.zeros_like(r)

        tail = jnp.where(first_blk, 0.0, t_ref[:, 0:W3])
        ypre = _conv_fwd(ext, h_ref[:, 0:W3], tail, cw_ref[...], R)
        y = _silu(ypre)
        ones = _head_ones()
        qn, rq = _l2n(y[:, 0:GDN_W], ones)
        kn, rk = _l2n(y[:, GDN_W:2 * GDN_W], ones)
        qs, ks, vs = _split_heads(qn, H), _split_heads(kn, H), _split_heads(y[:, 2 * GDN_W:W3], H)
        zh = _split_heads(h_ref[:, W3:W3 + GDN_W], H)
        ab = h_ref[:, W3 + GDN_W:GDN_IN]
        alog, dtb = al_ref[...], dt_ref[...]
        g, beta = _gdn_scalars(ab, alog, dtb)
        kept = inv_ref[...]
        loc, vjp_loc = jax.vjp(functools.partial(_gdn_local, lambda m: _known_inv(m, kept)), qs, ks, vs,
                               _bcast_heads(_chunk_cumsum(g), 0, H), _bcast_heads(beta, H, H))
        doh = _split_heads(do_ref[...], H)
        gnv = gn_ref[...]
        dloc = [[None] * nc for _ in range(6)]
        dzs = [None] * nc
        ds = dst[...]
        dgn = jnp.zeros((1, CH), f32)
        for c in reversed(range(nc)):
            sl = slice(c * H, (c + 1) * H)
            _, vjp = jax.vjp(_gdn_step, st_ref[c], *(t[sl] for t in loc), zh[sl], gnv)
            grads = vjp((doh[sl], ds))
            ds = grads[0]
            for j in range(6):
                dloc[j][c] = grads[1 + j]
            dzs[c] = grads[7]
            dgn = dgn + grads[8]
        dst[...] = ds
        dgn_ref[...] += dgn
        dqs, dks, dvs, dgb, dbb = vjp_loc(tuple(jnp.concatenate(d, axis=0) for d in dloc))
        lane = lax.broadcasted_iota(jnp.int32, (R, 128), 1)
        dg = _chunk_cumsum(_unbcast_heads(dgb, 0, H), reverse=True)
        dbeta = _unbcast_heads(dbb, H, H)
        da = dg * (-jnp.exp(alog)) * _sigmoid(ab + dtb)
        dh_ref[:, W3 + GDN_W:GDN_IN] = jnp.where(lane < H, da, dbeta * beta * (1.0 - beta)).astype(bf16)
        ddt_ref[...] += jnp.sum(jnp.where(lane < H, da, 0.0), 0, keepdims=True)
        dal_ref[...] += jnp.sum(jnp.where(lane < H, dg * g, 0.0), 0, keepdims=True)
        for j, dpart in enumerate((dqs, dks, dvs)):
            _merge_heads(wide, dpart, H, col0=j * GDN_W)
        wide[:, 0:GDN_W] = _l2n_bwd(wide[:, 0:GDN_W], qn, rq, ones)
        wide[:, GDN_W:2 * GDN_W] = _l2n_bwd(wide[:, GDN_W:2 * GDN_W], kn, rk, ones)
        dy = wide[...] * _dsilu(ypre)
        dx, dcw = _conv_bwd(ext, dy, carry_dy[...], cw_ref[...], R)
        carry_dy[...] = dy[0:8, :]
        dcw_ref[...] += dcw
        dh_ref[:, 0:W3] = dx.astype(bf16)
        _merge_heads(wide, jnp.concatenate(dzs, axis=0), H)
        dh_ref[:, W3:W3 + GDN_W] = wide[:, 0:GDN_W].astype(bf16)

    blk = pl.BlockSpec((R, GDN_IN), lambda i: (nb - 1 - i, 0))
    return pl.pallas_call(
        body, grid=(nb,),
        in_specs=[blk, _prev_tail_spec_rev(R, GDN_IN, nb), _full_spec((4, W3)), _full_spec((1, 128)),
                  _full_spec((1, 128)), _full_spec((1, CH)),
                  pl.BlockSpec((nc, H, CH, CH), lambda i: (nb - 1 - i, 0, 0, 0)),
                  pl.BlockSpec((nc * H, CH, CH), lambda i: (nb - 1 - i, 0, 0)),
                  pl.BlockSpec((R, GDN_W), lambda i: (nb - 1 - i, 0))],
        out_specs=[blk, _full_spec((4, W3)), _full_spec((1, 128)), _full_spec((1, 128)), _full_spec((1, CH))],
        out_shape=[jax.ShapeDtypeStruct((T, GDN_IN), bf16), jax.ShapeDtypeStruct((4, W3), f32),
                   jax.ShapeDtypeStruct((1, 128), f32), jax.ShapeDtypeStruct((1, 128), f32),
                   jax.ShapeDtypeStruct((1, CH), f32)],
        scratch_shapes=[pltpu.VMEM((H, CH, CH), f32), pltpu.VMEM((8, W3), f32), pltpu.VMEM((R + 8, W3), f32),
                        pltpu.VMEM((R, W3), f32)],
        compiler_params=_cparams(("arbitrary",)), name="gdn_bwd")(hg, hg, cw, alog, dtb, gn, states, invs, dout)


def _block_diag(w):
    out = jnp.zeros((LRU_W, LRU_W), w.dtype)
    for g in range(w.shape[0]):
        out = lax.dynamic_update_slice(out, w[g], (g * CH, g * CH))
    return out


def _block_diag_t(w):
    return jnp.stack([w[g * CH:(g + 1) * CH, g * CH:(g + 1) * CH] for g in range(LRU_W // CH)])


def _pad_lanes(v, n=128):
    return jnp.pad(v, (0, n - v.shape[0]))[None, :]


def _local_step(x, p, positions, target, fetch, emit, sm):
    cosw, sinw = _rope_tables(positions)
    cols = lambda w: jnp.transpose(w, (1, 2, 0, 3)).reshape(-1, D, NDEV * FSP)
    saved = []
    h = x
    for l in range(DEPTH):
        v = lambda n: sm[n][l][None, :]
        F1, tok = fetch(l, 'f1', h)
        p384a, pda = cols(F1['p384']), F1['pd']
        z1, x1, g1, u1, a1 = _ffn_fwd(h, p384a, pda, v('ln_ffn1_g') + tok, v('ln_ffn1_b'), 0, 0)
        G, _ = fetch(l, 'rest', x1)
        p384, pd, pr, pinl, ping, wpp = cols(G['p384']), G['pd'], G['pr'], G['pinl'], G['ping'], G['wpp']
        wts = (p384a, pda, p384, pd, pr, pinl, ping, wpp)
        hr, hl, hg = _proj_in(x1, pr, pinl, ping, 0)
        o_r, rst = _ret_fwd(hr, cosw, sinw, v('ret_norm_g'))
        lru_args = (sm['lru_conv_w'][l], v('lru_conv_b'), _block_diag(sm['lru_w_a'][l]), v('lru_b_a'),
                    _block_diag(sm['lru_w_x'][l]), v('lru_b_x'), v('lru_lambda'))
        o_l, hs = _lru_fwd(hl, *lru_args)
        gdn_args = (sm['gdn_conv_w'][l], _pad_lanes(sm['gdn_a_log'][l]), _pad_lanes(sm['gdn_dt_bias'][l]),
                    v('gdn_norm_g'))
        o_g, *gst = _gdn_fwd(hg, *gdn_args)
        z2, x2 = _mix_out(x1, o_r, o_l, o_g, pr, v('ln_mix_g'), v('ln_mix_b'), 0)
        z3, x3, g2, u2, a2 = _ffn_fwd(x2, p384, pd, v('ln_ffn2_g'), v('ln_ffn2_b'), 0, 1, ple=(p[l], pr, wpp))
        saved.append((h, z1, x1, hr, hl, hg, o_r, rst, o_l, hs, lru_args, o_g, gst, gdn_args, z2, x2, z3,
                      g1, u1, a1, g2, u2, a2, wts))
        h = x3
    d, loss = _loss_grad(h, target)

    small = {n: [None] * DEPTH for n in SMALL}
    tok = 0.0
    for l in reversed(range(DEPTH)):
        (x0, z1, x1, hr, hl, hg, o_r, rst, o_l, hs, lru_args, o_g, gst, gdn_args, z2, x2, z3,
         g1, u1, a1, g2, u2, a2, wts) = saved[l]
        p384a, pda, p384, pd, pr, pinl, ping, wpp = wts
        v = lambda n: sm[n][l][None, :]
        rows = lambda m: m.reshape(NDEV, m.shape[1] // NDEV, m.shape[2])
        d2, dg2, du2, dy2, small['ln_ffn2_g'][l], small['ln_ffn2_b'][l] = _ffn_bwd(
            z3, d, g2, u2, p384, pd, v('ln_ffn2_g') + tok, 0, 1)
        d2, dgp, dpj = _ple_bwd(x2, p[l], dy2, d2, pr, wpp, 0)
        dxb, dzb, do_r, do_l, do_g, small['ln_mix_g'][l], small['ln_mix_b'][l] = _mix_out_bwd(
            z2, d2, pr, v('ln_mix_g'), 0)
        dhr, small['ret_norm_g'][l] = _ret_bwd(hr, cosw, sinw, v('ret_norm_g'), rst, do_r)
        (dhl, small['lru_conv_w'][l], small['lru_conv_b'][l], dwa, small['lru_b_a'][l], dwx, small['lru_b_x'][l],
         small['lru_lambda'][l]) = _lru_bwd(hl, hs, *lru_args, do_l)
        small['lru_w_a'][l], small['lru_w_x'][l] = _block_diag_t(dwa), _block_diag_t(dwx)
        dhg, small['gdn_conv_w'][l], dal, ddt, small['gdn_norm_g'][l] = _gdn_bwd(hg, *gdn_args, *gst, do_g)
        small['gdn_a_log'][l], small['gdn_dt_bias'][l] = dal[:, 0:GDN_H], ddt[:, 0:GDN_H]
        d1 = _proj_in_bwd(dxb, dhr, dhl, dhg, pr, pinl, ping, 0)
        dwo = jnp.concatenate([_matmul_tn(o_r, dzb, D, "dw_out_r"), _matmul_tn(o_l, dzb, D, "dw_out_l"),
                               _matmul_tn(o_g, dzb, D, "dw_out_g")], axis=1)
        tok = emit(l, 'rest', {
            'p384': jnp.stack([_matmul_tn(x2, dg2, FSP, "dw_gate", NDEV), _matmul_tn(x2, du2, FSP, "dw_up", NDEV)],
                              axis=1),
            'pd': rows(_matmul_tn(a2, dy2, D, "dw_down"))[:, None],
            'pr': jnp.stack([rows(_matmul_tn(x1, dhr, D, "dw_in_r")), rows(dwo),
                             rows(_matmul_tn(x2, dgp, D, "dw_ple_gate"))], axis=1),
            'pinl': rows(_matmul_tn(x1, dhl, 2 * LRU_W, "dw_in_l"))[:, None],
            'ping': rows(_matmul_tn(x1, dhg, GDN_IN, "dw_in_g"))[:, None],
            'ppp': jnp.transpose(_matmul_tn(p[l], dpj, D, "dw_ple_proj").reshape(PLE, NDEV, 128), (1, 0, 2))[:, None]})
        d, dg1, du1, dy1, small['ln_ffn1_g'][l], small['ln_ffn1_b'][l] = _ffn_bwd(
            z1, d1, g1, u1, p384a, pda, v('ln_ffn1_g') + tok, 0, 0)
        if l == 0:
            emit(l, 'small', {n: jnp.stack([g.reshape(sm[n].shape[1:]) for g in gs]) for n, gs in small.items()})
        tok = emit(l, 'f1', {
            'p384': jnp.stack([_matmul_tn(x0, dg1, FSP, "dw_gate", NDEV), _matmul_tn(x0, du1, FSP, "dw_up", NDEV)],
                              axis=1),
            'pd': rows(_matmul_tn(a1, dy1, D, "dw_down"))[:, None]})
    return loss, d


def _pack_big(ws, dtype=bf16):
    padc = lambda a, n: jnp.pad(a, ((0, 0), (0, 0), (0, n - a.shape[2])))
    padr = lambda a, n: jnp.pad(a, ((0, 0), (0, n - a.shape[1]), (0, 0)))
    per_layer = lambda arrs: jnp.stack(arrs, axis=1).reshape((-1,) + arrs[0].shape[1:])
    w_in = ws['w_in']
    out = {
        'p384': per_layer([padc(ws[n], FSP) for n in ('ffn1_w_gate', 'ffn1_w_up', 'ffn2_w_gate', 'ffn2_w_up')]),
        'pd': per_layer([padr(ws[n], FSP) for n in ('ffn1_w_down', 'ffn2_w_down')]),
        'pr': per_layer([w_in[:, :, 0:D], ws['w_out'], ws['ple_w_gate']]),
        'pinl': w_in[:, :, D:D + 2 * LRU_W],
        'ping': padc(w_in[:, :, D + 2 * LRU_W:D_IN], GDN_IN),
        'ppp': ws['ple_w_proj'],
    }
    return {k: a.astype(dtype) for k, a in out.items()}


def _gather_two_level(arrays, name):
    n = len(arrays)

    def body(*refs):
        ins, outs = refs[:n], refs[n:2 * n]
        send_sems, recv_sems, local_sems = refs[2 * n:]
        x, y, c = lax.axis_index("x"), lax.axis_index("y"), lax.axis_index("c")
        me, sibling = (x, y, c), (x, y, 1 - c)
        chips = [(1 - x, y), (x, 1 - y), (1 - x, 1 - y)]
        slot = lambda d: 4 * d[0] + 2 * d[1] + d[2]

        def copy(i, k, block, to, src=None):
            return pltpu.make_async_remote_copy(
                src_ref=outs[i].at[slot(block)] if src is None else src, dst_ref=outs[i].at[slot(block)],
                send_sem=send_sems.at[i, k], recv_sem=recv_sems.at[i, k], device_id=to,
                device_id_type=pl.DeviceIdType.MESH)

        mine, first, passed = [], [], []
        for i in range(n):
            cp = pltpu.make_async_copy(ins[i], outs[i].at[slot(me)], local_sems.at[i])
            cp.start()
            mine.append(cp)
            first.append(copy(i, 0, me, sibling, src=ins[i]))
            first += [copy(i, 1 + j, me, (*chip, c), src=ins[i]) for j, chip in enumerate(chips)]
        for cp in first:
            cp.start()
        for i in range(n):
            for j, chip in enumerate(chips):
                copy(i, 1 + j, (*chip, c), me).wait_recv()
                cp = copy(i, 4 + j, (*chip, c), sibling)
                cp.start()
                passed.append(cp)
        for i in range(n):
            copy(i, 0, sibling, me).wait_recv()
            for j, chip in enumerate(chips):
                copy(i, 4 + j, (*chip, 1 - c), me).wait_recv()
        for cp in first + passed:
            cp.wait_send()
        for cp in mine:
            cp.wait()

    hbm = pl.BlockSpec(memory_space=pltpu.HBM)
    return pl.pallas_call(
        body, in_specs=[hbm] * n, out_specs=[hbm] * n,
        out_shape=[jax.ShapeDtypeStruct((NDEV,) + a.shape, a.dtype) for a in arrays],
        scratch_shapes=[pltpu.SemaphoreType.DMA((n, NDEV - 1)), pltpu.SemaphoreType.DMA((n, NDEV - 1)),
                        pltpu.SemaphoreType.DMA((n,))],
        compiler_params=pltpu.CompilerParams(has_side_effects=True), name=name)(*arrays)


def _scatter_pairs(arrays, name):
    n = len(arrays)

    def body(*refs):
        ins, gots = refs[:n], refs[n:2 * n]
        send_sems, recv_sems = refs[2 * n:]
        x, y, c = lax.axis_index("x"), lax.axis_index("y"), lax.axis_index("c")
        sends = []
        for i in range(n):
            for q in range(4):
                cp = pltpu.make_async_remote_copy(
                    src_ref=ins[i].at[2 * q + 1 - c], dst_ref=gots[i].at[q], send_sem=send_sems.at[i, q],
                    recv_sem=recv_sems.at[i, q], device_id=(x, y, 1 - c), device_id_type=pl.DeviceIdType.MESH)
                cp.start()
                sends.append(cp)
        for cp in sends:
            cp.wait_recv()
        for cp in sends:
            cp.wait_send()

    hbm = pl.BlockSpec(memory_space=pltpu.HBM)
    return pl.pallas_call(
        body, in_specs=[hbm] * n, out_specs=[hbm] * n,
        out_shape=[jax.ShapeDtypeStruct((4,) + a.shape[1:], a.dtype) for a in arrays],
        scratch_shapes=[pltpu.SemaphoreType.DMA((n, 4)), pltpu.SemaphoreType.DMA((n, 4))],
        compiler_params=pltpu.CompilerParams(has_side_effects=True), name=name)(*arrays)


def _pair_sum(own, got, name):
    def body(a_ref, b_ref, o_ref):
        o_ref[...] = (a_ref[...].astype(f32) + b_ref[...].astype(f32)).astype(bf16)

    spec = pl.BlockSpec((None, None) + own.shape[2:], lambda q, s: (q, s, 0, 0))
    return pl.pallas_call(
        body, grid=own.shape[:2], in_specs=[spec, spec], out_specs=spec,
        out_shape=jax.ShapeDtypeStruct(own.shape, bf16),
        compiler_params=_cparams(("arbitrary", "arbitrary")), name=name)(own, got)


def _gather_plan(srcs, lands, x, y, c):
    me = 4 * x + 2 * y + c
    sends, arrivals = [], []
    for j in range(1, NDEV):
        peer, source = (me + j) % NDEV, (me + NDEV - j) % NDEV
        for i in range(len(srcs)):
            k = i * (NDEV - 1) + j - 1
            sends.append((srcs[i], lands[i].at[me], (peer // 4, (peer // 2) % 2, peer % 2), k))
            arrivals.append((srcs[i], lands[i].at[source], (source // 4, (source // 2) % 2, source % 2), k))
    return sends, arrivals


def _chips_plan(srcs, lands, x, y, c):
    chip = 2 * x + y
    sends, arrivals = [], []
    for j in range(1, 4):
        peer, source = (chip + j) % 4, (chip + 4 - j) % 4
        for i in range(len(srcs)):
            k = i * 3 + j - 1
            sends.append((srcs[i].at[peer], lands[i].at[chip], (peer // 2, peer % 2, c), k))
            arrivals.append((srcs[i].at[chip], lands[i].at[source], (source // 2, source % 2, c), k))
    return sends, arrivals


def _remote(entry, send_sems, recv_sems):
    src, dst, dev, k = entry
    return pltpu.make_async_remote_copy(src_ref=src, dst_ref=dst, send_sem=send_sems.at[k], recv_sem=recv_sems.at[k],
                                        device_id=dev, device_id_type=pl.DeviceIdType.MESH)


_HBM = pl.BlockSpec(memory_space=pltpu.HBM)
_SEM = pl.BlockSpec(memory_space=pltpu.SEMAPHORE)


def _split_start(arrays, land_shapes, plan, npeer, name):
    n = len(arrays)

    def body(*refs):
        srcs, lands = refs[:n], refs[n:2 * n]
        send_sems, recv_sems, token = refs[2 * n], refs[2 * n + 1], refs[-1]
        sends, _ = plan(srcs, lands, lax.axis_index("x"), lax.axis_index("y"), lax.axis_index("c"))
        for entry in sends:
            _remote(entry, send_sems, recv_sems).start()
        token[...] = jnp.zeros_like(token)

    lands = [lax.empty(s, a.dtype) for s, a in zip(land_shapes, arrays)]
    thru = [pltpu.HBM(a.shape, a.dtype) for a in arrays + lands]
    out = pl.pallas_call(
        body, name=name, in_specs=[_HBM] * (2 * n),
        out_specs=(_SEM, _SEM, *([_HBM] * (2 * n)), pl.BlockSpec(memory_space=pltpu.VMEM)),
        out_shape=(pltpu.SemaphoreType.DMA((n * npeer,)), pltpu.SemaphoreType.DMA((n * npeer,)), *thru,
                   jax.ShapeDtypeStruct((8, 128), f32)),
        input_output_aliases={i: 2 + i for i in range(2 * n)},
        compiler_params=pltpu.CompilerParams(has_side_effects=pltpu.SideEffectType.DATAFLOW_SIDE_EFFECTING),
    )(*[pltpu.with_memory_space_constraint(a, pltpu.HBM) for a in arrays + lands])
    return out[0], out[1], list(out[2:2 + n]), list(out[2 + n:2 + 2 * n]), out[-1]


def _split_wait(send_sems, recv_sems, srcs, lands, after, plan, name):
    n = len(srcs)

    def body(*refs):
        s_refs, l_refs = refs[:n], refs[n:2 * n]
        ssem, rsem = refs[2 * n], refs[2 * n + 1]
        sends, arrivals = plan(s_refs, l_refs, lax.axis_index("x"), lax.axis_index("y"), lax.axis_index("c"))
        for entry in sends:
            _remote(entry, ssem, rsem).wait_send()
        for entry in arrivals:
            _remote(entry, ssem, rsem).wait_recv()

    out = pl.pallas_call(
        body, name=name, in_specs=[_HBM] * (2 * n) + [_SEM, _SEM, pl.BlockSpec(memory_space=pl.ANY)],
        out_specs=[_HBM] * (2 * n), out_shape=[pltpu.HBM(a.shape, a.dtype) for a in srcs + lands],
        input_output_aliases={i: i for i in range(2 * n)},
        compiler_params=pltpu.CompilerParams(has_side_effects=pltpu.SideEffectType.DATAFLOW_SIDE_EFFECTING),
    )(*srcs, *lands, send_sems, recv_sems, after)
    return list(out[:n]), list(out[n:])


def _adam_math(w, g, m, v):
    m2 = ADAM_B1 * m + (1.0 - ADAM_B1) * g
    v2 = ADAM_B2 * v + (1.0 - ADAM_B2) * (g * g)
    m_hat = m2 / (1.0 - ADAM_B1 ** ADAM_STEP)
    v_hat = v2 / (1.0 - ADAM_B2 ** ADAM_STEP)
    return -ADAM_LR * (m_hat / (jnp.sqrt(v_hat) + ADAM_EPS) + ADAM_WD * w), m2, v2


def _adam_big(parts, w, m, v, anchor, name):
    L, rows, cols = w.shape
    flat = [(a, slot) for layer_parts in parts for a, slot in layer_parts]
    per = len(parts[0])

    def body(*refs):
        prefs = refs[:len(flat)]
        w_ref, m_ref, v_ref, _, g_ref, d_ref, m2_ref, v2_ref = refs[len(flat):]
        for li in range(L):
            @pl.when(pl.program_id(0) == li)
            def _():
                c0 = 0
                for pref in prefs[li * per:(li + 1) * per]:
                    acc = pref[0].astype(f32)
                    for s in range(1, pref.shape[0]):
                        acc = acc + pref[s].astype(f32)
                    width = min(acc.shape[1], cols - c0)
                    g_ref[:, c0:c0 + width] = acc[0:rows, 0:width]
                    c0 += width

        d, m2, v2 = _adam_math(w_ref[...], g_ref[...], m_ref[...], v_ref[...])
        d_ref[...] = d
        m2_ref[...] = m2
        v2_ref[...] = v2

    wspec = pl.BlockSpec((None, rows, cols), lambda l: (l, 0, 0))
    in_specs = [pl.BlockSpec((a.shape[0], None) + a.shape[2:], functools.partial(lambda l, slot: (0, slot, 0, 0), slot=slot))
                for a, slot in flat]
    return pl.pallas_call(
        body, grid=(L,), in_specs=in_specs + [wspec] * 3 + [_full_spec((8, 128))], out_specs=[wspec] * 4,
        out_shape=[jax.ShapeDtypeStruct(w.shape, f32)] * 4,
        compiler_params=_cparams(("arbitrary",)), name=name)(*[a for a, _ in flat], w, m, v, anchor)


def _sum_sources(stacked):
    rows = stacked.shape[1]

    def body(s_ref, o_ref):
        acc = s_ref[0]
        for s in range(1, NDEV):
            acc = acc + s_ref[s]
        o_ref[...] = acc

    return pl.pallas_call(body, out_shape=jax.ShapeDtypeStruct((rows, 128), f32), name="sum_small_grads")(stacked)


def _adam_small(w, g, m, v):
    def body(w_ref, g_ref, m_ref, v_ref, d_ref, m2_ref, v2_ref):
        d, m2, v2 = _adam_math(w_ref[...], g_ref[...], m_ref[...], v_ref[...])
        d_ref[...] = d
        m2_ref[...] = m2
        v2_ref[...] = v2

    return pl.pallas_call(body, out_shape=[jax.ShapeDtypeStruct(w.shape, f32)] * 3, name="adam_small")(w, g, m, v)


def _pack_rows(arrs):
    flat = []
    for a in arrs:
        a = a.reshape(-1)
        flat.append(jnp.pad(a, (0, (-a.shape[0]) % 1024)))
    return jnp.concatenate(flat).reshape(-1, 128)


def _unpack_rows(packed, shapes):
    out, off = [], 0
    flat = packed.reshape(-1)
    for s in shapes:
        n = math.prod(s)
        out.append(flat[off:off + n].reshape(s))
        off += n + (-n) % 1024
    return out


def _gather_conv(gathered, shape):
    L, K, c = shape
    return jnp.transpose(gathered, (1, 2, 0, 3)).reshape(L, K, NDEV * c)


def kernel(x, p, positions, ln_ffn1_g, ln_ffn1_b, ffn1_w_gate, ffn1_w_up, ffn1_w_down, w_in, ret_norm_g, lru_conv_w, lru_conv_b, lru_w_a, lru_b_a, lru_w_x, lru_b_x, lru_lambda, gdn_conv_w, gdn_a_log, gdn_dt_bias, gdn_norm_g, w_out, ln_mix_g, ln_mix_b, ffn2_w_gate, ffn2_w_up, ffn2_w_down, ple_w_gate, ple_w_proj, ln_ffn2_g, ln_ffn2_b, loss_target, m_ln_ffn1_g, m_ln_ffn1_b, m_ffn1_w_gate, m_ffn1_w_up, m_ffn1_w_down, m_w_in, m_ret_norm_g, m_lru_conv_w, m_lru_conv_b, m_lru_w_a, m_lru_b_a, m_lru_w_x, m_lru_b_x, m_lru_lambda, m_gdn_conv_w, m_gdn_a_log, m_gdn_dt_bias, m_gdn_norm_g, m_w_out, m_ln_mix_g, m_ln_mix_b, m_ffn2_w_gate, m_ffn2_w_up, m_ffn2_w_down, m_ple_w_gate, m_ple_w_proj, m_ln_ffn2_g, m_ln_ffn2_b, v_ln_ffn1_g, v_ln_ffn1_b, v_ffn1_w_gate, v_ffn1_w_up, v_ffn1_w_down, v_w_in, v_ret_norm_g, v_lru_conv_w, v_lru_conv_b, v_lru_w_a, v_lru_b_a, v_lru_w_x, v_lru_b_x, v_lru_lambda, v_gdn_conv_w, v_gdn_a_log, v_gdn_dt_bias, v_gdn_norm_g, v_w_out, v_ln_mix_g, v_ln_mix_b, v_ffn2_w_gate, v_ffn2_w_up, v_ffn2_w_down, v_ple_w_gate, v_ple_w_proj, v_ln_ffn2_g, v_ln_ffn2_b):
    args = locals()
    W = {n: args[n] for n in WEIGHTS}
    M = {n: args['m_' + n] for n in WEIGHTS}
    V = {n: args['v_' + n] for n in WEIGHTS}
    me = 4 * lax.axis_index("x") + 2 * lax.axis_index("y") + lax.axis_index("c")

    core = lax.axis_index("c")
    chip = 2 * lax.axis_index("x") + lax.axis_index("y")

    packed = _pack_big(W)

    def group(l, name):
        per = {k: packed[k].shape[0] // DEPTH for k in PACKS}
        if name == 'f1':
            return [packed['p384'][l * per['p384']:l * per['p384'] + 2], packed['pd'][l * per['pd']:l * per['pd'] + 1]]
        return [packed['p384'][l * per['p384'] + 2:(l + 1) * per['p384']],
                packed['pd'][l * per['pd'] + 1:(l + 1) * per['pd']]] + [
                    packed[k][l * per[k]:(l + 1) * per[k]] for k in PACKS[2:]]

    def as_weights(arrs):
        G = dict(zip(PACKS, arrs))
        if 'ppp' in G:
            G['wpp'] = jnp.transpose(G.pop('ppp'), (1, 2, 0, 3)).reshape(PLE, D)
        return G

    conv_pack = _pack_rows([W[n] for n in CONV_SHARDED])
    g0 = _gather_two_level(group(0, 'f1') + [conv_pack], "gather_weights")
    g0, rest0 = lax.optimization_barrier((g0, group(0, 'rest')))
    start0 = _split_start(rest0, [(NDEV,) + a.shape for a in rest0], _gather_plan, NDEV - 1, "gather_start_0")
    tok0, all1 = lax.optimization_barrier((start0[4], group(1, 'f1') + group(1, 'rest')))
    start1 = _split_start(all1, [(NDEV,) + a.shape for a in all1], _gather_plan, NDEV - 1, "gather_start_1")
    arrived = {}

    def gather_done(started, after, name):
        srcs, lands = _split_wait(started[0], started[1], started[2], started[3], after, _gather_plan, name)
        return [lax.dynamic_update_slice_in_dim(ld, s[None], me, axis=0) for s, ld in zip(srcs, lands)]

    def fetch(l, name, after):
        if l == 0 and name == 'f1':
            return as_weights(g0[:-1]), tok0[0, 0] + start1[4][0, 0]
        if l == 0:
            return as_weights(gather_done(start0, after, "gather_wait_0")), 0.0
        if name == 'f1':
            arrived[1] = gather_done(start1, after, "gather_wait_1")
            return as_weights(arrived[1][:2]), 0.0
        return as_weights(arrived[1][2:]), 0.0

    conv_all = g0[-1]
    sm = {n: W[n] for n in SMALL}
    conv_shards = [_unpack_rows(conv_all[s], [W[n].shape for n in CONV_SHARDED]) for s in range(NDEV)]
    for i, n in enumerate(CONV_SHARDED):
        sm[n] = _gather_conv(jnp.stack([cs[i] for cs in conv_shards]), W[n].shape)

    received, started = {}, {}

    small_shapes = [sm[n].shape for n in SMALL]
    small_started = []

    def emit(l, name, grads):
        if name == 'small':
            pack = _pack_rows([grads[n] for n in SMALL])
            small_started.extend(_split_start([pack], [(NDEV,) + pack.shape], _gather_plan, NDEV - 1, "small_start"))
            return 0.0
        keys = list(grads)
        arrs = [grads[k] for k in keys]
        gots = _scatter_pairs(arrs, "scatter_pairs")
        owns = [lax.dynamic_index_in_dim(a.reshape((4, 2) + a.shape[1:]), core, axis=1, keepdims=False) for a in arrs]
        pair = [_pair_sum(o, g, "pair_sum_" + k) for k, o, g in zip(keys, owns, gots)]
        started[l, name] = (keys, _split_start(pair, [a.shape for a in pair], _chips_plan, 3,
                                               f"scatter_start_{l}_{name}"))
        return started[l, name][1][4][0, 0]

    def scatter_done(l, name, after):
        keys, st = started[l, name]
        srcs, lands = _split_wait(st[0], st[1], st[2], st[3], after, _chips_plan, f"scatter_wait_{l}_{name}")
        received[l, name] = dict(zip(keys, [
            lax.dynamic_update_slice_in_dim(ld, lax.dynamic_index_in_dim(s, chip, axis=0), chip, axis=0)
            for s, ld in zip(srcs, lands)]))

    loss, grad_x = _local_step(x[0], p[:, 0], positions.reshape(-1, 1), loss_target[0], fetch, emit, sm)
    loss = lax.psum(loss[0, 0], ("x", "y", "c"))
    last = (0, 'f1')
    for l, name in started:
        if (l, name) != last:
            scatter_done(l, name, grad_x)

    anchor = started[last][1][4]
    srcs, lands = _split_wait(small_started[0], small_started[1], small_started[2], small_started[3], anchor,
                              _gather_plan, "small_wait")
    small_all = lax.dynamic_update_slice_in_dim(lands[0], srcs[0][None], me, axis=0)
    small_sum = _unpack_rows(_sum_sources(small_all), small_shapes)
    grads, delta, new_m, new_v = {}, {}, {}, {}
    for n, g in zip(SMALL, small_sum):
        if n in CONV_SHARDED:
            c = W[n].shape[2]
            g = lax.dynamic_slice_in_dim(g, me * c, c, axis=2)
        grads[n] = g

    big_parts = {
        'ffn1_w_gate': [('f1', 'p384', 0)], 'ffn1_w_up': [('f1', 'p384', 1)], 'ffn1_w_down': [('f1', 'pd', 0)],
        'ffn2_w_gate': [('rest', 'p384', 0)], 'ffn2_w_up': [('rest', 'p384', 1)], 'ffn2_w_down': [('rest', 'pd', 0)],
        'w_in': [('rest', 'pr', 0), ('rest', 'pinl', 0), ('rest', 'ping', 0)], 'w_out': [('rest', 'pr', 1)],
        'ple_w_gate': [('rest', 'pr', 2)], 'ple_w_proj': [('rest', 'ppp', 0)],
    }
    def adam(n):
        parts = [[(received[l, grp][k], slot) for grp, k, slot in big_parts[n]] for l in range(DEPTH)]
        grads[n], delta[n], new_m[n], new_v[n] = _adam_big(parts, W[n], M[n], V[n], anchor, "adam_" + n)

    shapes = [W[n].shape for n in SMALL]
    d_s, m_s, v_s = _adam_small(*[_pack_rows([src[n] for n in SMALL]) for src in (W, grads, M, V)])
    for n, dd, mm, vv in zip(SMALL, _unpack_rows(d_s, shapes), _unpack_rows(m_s, shapes), _unpack_rows(v_s, shapes)):
        delta[n], new_m[n], new_v[n] = dd, mm, vv
    waits_last = [n for n in BIG if big_parts[n][0][0] == last[1]]
    for n in BIG:
        if n not in waits_last:
            adam(n)
    done = jnp.stack([d_s[0, 0]] + [delta[n][0, 0, 0] for n in BIG if n not in waits_last])
    scatter_done(*last, done)
    for n in waits_last:
        adam(n)

    return (loss, grad_x[None], *[grads[n] for n in WEIGHTS], *[delta[n] for n in WEIGHTS],
            *[new_m[n] for n in WEIGHTS], *[new_v[n] for n in WEIGHTS])
```

```python
import functools
import math

import jax
import jax.numpy as jnp
from jax import lax
from jax.experimental import pallas as pl
from jax.experimental.pallas import tpu as pltpu

f32 = jnp.float32
bf16 = jnp.bfloat16

NDEV = 8
DEPTH = 2
D = 1024
FSP = 384
FB = 2
NF = NDEV // FB
PLE = 256
CH = 64
RET_H, GDN_H = 4, 6
RET_W, LRU_W, GDN_W = 256, 384, 384
GDN_IN = 1664
D_IN = 3340
ALPHA = 4.0 ** 0.25
LN_EPS = 1e-5
ROPE_THETA = 10000.0
TM = 512
RB_RET, RB_LRU, RB_GDN = 512, 512, 256
VMEM_LIMIT = 56 * 1024 * 1024
ADAM_LR, ADAM_B1, ADAM_B2, ADAM_EPS, ADAM_WD, ADAM_STEP = 0.001, 0.9, 0.999, 1e-08, 0.01, 10

WEIGHTS = ['ln_ffn1_g', 'ln_ffn1_b', 'ffn1_w_gate', 'ffn1_w_up', 'ffn1_w_down', 'w_in', 'ret_norm_g', 'lru_conv_w',
           'lru_conv_b', 'lru_w_a', 'lru_b_a', 'lru_w_x', 'lru_b_x', 'lru_lambda', 'gdn_conv_w', 'gdn_a_log',
           'gdn_dt_bias', 'gdn_norm_g', 'w_out', 'ln_mix_g', 'ln_mix_b', 'ffn2_w_gate', 'ffn2_w_up', 'ffn2_w_down',
           'ple_w_gate', 'ple_w_proj', 'ln_ffn2_g', 'ln_ffn2_b']
BIG = ['ffn1_w_gate', 'ffn1_w_up', 'ffn1_w_down', 'w_in', 'w_out', 'ffn2_w_gate', 'ffn2_w_up', 'ffn2_w_down',
       'ple_w_gate', 'ple_w_proj']
SMALL = [n for n in WEIGHTS if n not in BIG]
PACKS = ('p384', 'pd', 'pr', 'pinl', 'ping', 'ppp')
CONV_SHARDED = {'lru_conv_w': LRU_W, 'gdn_conv_w': 3 * GDN_W}


def _cparams(sem=None):
    return pltpu.CompilerParams(dimension_semantics=sem, vmem_limit_bytes=VMEM_LIMIT)


def _sigmoid(x):
    return 1.0 / (1.0 + jnp.exp(-x))


def _silu(x):
    return x * _sigmoid(x)


def _dsilu(x):
    s = _sigmoid(x)
    return s * (1.0 + x * (1.0 - s))


def _softplus(x):
    return jnp.maximum(x, 0.0) + jnp.log(1.0 + jnp.exp(-jnp.abs(x)))


def _gelu(x):
    return 0.5 * x * (1.0 + jnp.tanh(0.7978845608028654 * (x + 0.044715 * x * x * x)))


def _dot(a, b):
    return jnp.dot(a.astype(bf16), b.astype(bf16), preferred_element_type=f32)


def _dot_nt(a, b):
    return lax.dot_general(a.astype(bf16), b.astype(bf16), (((1,), (1,)), ((), ())), preferred_element_type=f32)


def _dot_tn(a, b):
    return lax.dot_general(a.astype(bf16), b.astype(bf16), (((0,), (0,)), ((), ())), preferred_element_type=f32)


def _bmm(eq, a, b):
    return jnp.einsum(eq, a.astype(bf16), b.astype(bf16), preferred_element_type=f32)


def _split3(a):
    a1 = a.astype(bf16)
    r = a - a1.astype(f32)
    a2 = r.astype(bf16)
    return a1, a2, (r - a2.astype(f32)).astype(bf16)


def _bmm3(eq, a, b):
    a1, a2, _ = _split3(a)
    b1, b2, _ = _split3(b)
    e = lambda x, y: jnp.einsum(eq, x, y, preferred_element_type=f32)
    return e(a1, b1) + (e(a1, b2) + e(a2, b1))


def _rowsum(x):
    ones = jnp.ones((x.shape[0], CH, CH), bf16)
    return jnp.einsum('bij,bjk->bik', x.astype(bf16), ones, preferred_element_type=f32)


def _tri_ones(B, upper=False):
    ii = lax.broadcasted_iota(jnp.int32, (B, CH, CH), 1)
    jj = lax.broadcasted_iota(jnp.int32, (B, CH, CH), 2)
    return jnp.where((ii <= jj) if upper else (ii >= jj), 1.0, 0.0).astype(bf16)


def _cumsum_mm(t, x):
    x1, x2, x3 = _split3(x)
    e = lambda y: jnp.einsum('bij,bjk->bik', t, y, preferred_element_type=f32)
    return e(x1) + (e(x2) + e(x3))


def _chunk_cumsum(x, reverse=False):
    n = x.shape[0] // CH
    return _cumsum_mm(_tri_ones(n, upper=reverse), x.reshape(n, CH, 128)).reshape(x.shape)


@jax.custom_vjp
def _neumann_inv(m):
    ii = lax.broadcasted_iota(jnp.int32, m.shape, 1)
    jj = lax.broadcasted_iota(jnp.int32, m.shape, 2)
    inv = jnp.where(ii == jj, 1.0, 0.0).astype(f32) + m
    mp = m
    for _ in range(5):
        mp = _bmm3('bij,bjk->bik', mp, mp)
        inv = inv + _bmm3('bij,bjk->bik', inv, mp)
    return inv


def _neumann_inv_fwd(m):
    inv = _neumann_inv(m)
    return inv, inv


def _neumann_inv_bwd(inv, g):
    return (_bmm('bij,bkj->bik', _bmm('bji,bjk->bik', inv, g), inv),)


_neumann_inv.defvjp(_neumann_inv_fwd, _neumann_inv_bwd)


@jax.custom_vjp
def _known_inv(m, inv):
    return inv


def _known_inv_fwd(m, inv):
    return inv, inv


def _known_inv_bwd(inv, g):
    return _neumann_inv_bwd(inv, g)[0], jnp.zeros_like(inv)


_known_inv.defvjp(_known_inv_fwd, _known_inv_bwd)


def _ln_stats(z):
    mu = jnp.mean(z, -1, keepdims=True)
    zc = z - mu
    rstd = lax.rsqrt(jnp.mean(zc * zc, -1, keepdims=True) + LN_EPS)
    return zc * rstd, rstd


def _ln_bwd(z, g, dout):
    xh, rstd = _ln_stats(z)
    dxh = dout * g
    dz = rstd * (dxh - jnp.mean(dxh, -1, keepdims=True) - xh * jnp.mean(dxh * xh, -1, keepdims=True))
    return dz, jnp.sum(dout * xh, 0, keepdims=True), jnp.sum(dout, 0, keepdims=True)


def _full_spec(shape):
    nd = len(shape)
    return pl.BlockSpec(shape, lambda *_: (0,) * nd)


def _ffn_fwd(x, p384, pd, lg, lb, slot, which, ple=None):
    T = x.shape[0]
    sg, su, sd = 2 * slot, 2 * slot + 1, slot
    has_ple = ple is not None

    def body(*refs):
        if has_ple:
            (x_ref, wg_ref, wu_ref, wd_ref, lg_ref, lb_ref, p_ref, wpg_ref, wpp_ref,
             z_ref, o_ref, g_ref, u_ref, a_ref, acc, xb_s) = refs
        else:
            x_ref, wg_ref, wu_ref, wd_ref, lg_ref, lb_ref, z_ref, o_ref, g_ref, u_ref, a_ref, acc, xb_s = refs
        f = pl.program_id(1)

        @pl.when(f == 0)
        def _():
            x = x_ref[...]
            xb = x.astype(bf16)
            xb_s[...] = xb
            base = ALPHA * x
            if has_ple:
                gate = _sigmoid(_dot(xb, wpg_ref[...].reshape(D, D)))
                base = base + gate * _dot(p_ref[...], wpp_ref[...])
            acc[...] = base

        xb = xb_s[...]
        g = _dot(xb, wg_ref[...])
        u = _dot(xb, wu_ref[...])
        g_ref[...] = g.astype(bf16)
        u_ref[...] = u.astype(bf16)
        a = (_silu(g) * u).astype(bf16)
        a_ref[...] = a
        acc[...] += 0.5 * _dot(a, wd_ref[...].reshape(FB * FSP, D))

        @pl.when(f == NF - 1)
        def _():
            z = acc[...]
            z_ref[...] = z
            o_ref[...] = _ln_stats(z)[0] * lg_ref[...] + lb_ref[...]

    row = pl.BlockSpec((TM, D), lambda i, f: (i, 0))
    in_specs = [row,
                pl.BlockSpec((None, D, FB * FSP), lambda i, f: (sg, 0, f)),
                pl.BlockSpec((None, D, FB * FSP), lambda i, f: (su, 0, f)),
                pl.BlockSpec((FB, None, FSP, D), lambda i, f: (f, sd, 0, 0)),
                _full_spec((1, D)), _full_spec((1, D))]
    args = [x, p384, p384, pd, lg, lb]
    if has_ple:
        p, pr, wpp = ple
        in_specs += [pl.BlockSpec((TM, PLE), lambda i, f: (i, 0)),
                     pl.BlockSpec((NDEV, None, 128, D), lambda i, f: (0, 2, 0, 0)),
                     _full_spec((PLE, D))]
        args += [p, pr, wpp]
    hid = pl.BlockSpec((TM, FB * FSP), lambda i, f: (i, f))
    hshape = jax.ShapeDtypeStruct((T, NDEV * FSP), bf16)
    return pl.pallas_call(
        body, grid=(T // TM, NF), in_specs=in_specs, out_specs=[row, row, hid, hid, hid],
        out_shape=[jax.ShapeDtypeStruct((T, D), f32)] * 2 + [hshape, hshape, hshape],
        scratch_shapes=[pltpu.VMEM((TM, D), f32), pltpu.VMEM((TM, D), bf16)],
        compiler_params=_cparams(("arbitrary", "arbitrary")), name=f"ffn{which + 1}_fwd")(*args)


def _ffn_bwd(z, dout, gs, us, p384, pd, lg, slot, which):
    T = z.shape[0]
    TMB = TM
    sg, su, sd = 2 * slot, 2 * slot + 1, slot

    def body(z_ref, do_ref, g_ref, u_ref, wg_ref, wu_ref, wd_ref, lg_ref,
             dx_ref, dg_ref, du_ref, dy_ref, dlg_ref, dlb_ref, acc, dyb):
        i, f = pl.program_id(0), pl.program_id(1)

        @pl.when(jnp.logical_and(i == 0, f == 0))
        def _():
            dlg_ref[...] = jnp.zeros_like(dlg_ref)
            dlb_ref[...] = jnp.zeros_like(dlb_ref)

        @pl.when(f == 0)
        def _():
            dz, dlg, dlb = _ln_bwd(z_ref[...], lg_ref[...], do_ref[...])
            dlg_ref[...] += dlg
            dlb_ref[...] += dlb
            dy = (0.5 * dz).astype(bf16)
            dyb[...] = dy
            dy_ref[...] = dy
            acc[...] = ALPHA * dz

        g = g_ref[...].astype(f32)
        u = u_ref[...].astype(f32)
        da = _dot_nt(dyb[...], wd_ref[...].reshape(FB * FSP, D))
        sgm = _sigmoid(g)
        dg = (da * u * (sgm * (1.0 + g * (1.0 - sgm)))).astype(bf16)
        du = (da * (g * sgm)).astype(bf16)
        dg_ref[...] = dg
        du_ref[...] = du
        acc[...] += _dot_nt(dg, wg_ref[...]) + _dot_nt(du, wu_ref[...])

        @pl.when(f == NF - 1)
        def _():
            dx_ref[...] = acc[...]

    row = pl.BlockSpec((TMB, D), lambda i, f: (i, 0))
    hid = pl.BlockSpec((TMB, FB * FSP), lambda i, f: (i, f))
    vec = _full_spec((1, D))
    in_specs = [row, row, hid, hid,
                pl.BlockSpec((None, D, FB * FSP), lambda i, f: (sg, 0, f)),
                pl.BlockSpec((None, D, FB * FSP), lambda i, f: (su, 0, f)),
                pl.BlockSpec((FB, None, FSP, D), lambda i, f: (f, sd, 0, 0)),
                vec]
    args = [z, dout, gs, us, p384, p384, pd, lg]
    out_specs = [row, hid, hid, row, vec, vec]
    hshape = jax.ShapeDtypeStruct((T, NDEV * FSP), bf16)
    out_shape = [jax.ShapeDtypeStruct((T, D), f32), hshape, hshape, jax.ShapeDtypeStruct((T, D), bf16),
                 jax.ShapeDtypeStruct((1, D), f32), jax.ShapeDtypeStruct((1, D), f32)]
    return pl.pallas_call(
        body, grid=(T // TMB, NF), in_specs=in_specs, out_specs=out_specs, out_shape=out_shape,
        scratch_shapes=[pltpu.VMEM((TMB, D), f32), pltpu.VMEM((TMB, D), bf16)],
        compiler_params=_cparams(("arbitrary", "arbitrary")), name=f"ffn{which + 1}_bwd")(*args)


def _ple_bwd(x, p, dy, dx_ffn, pr, wpp, layer):
    T = x.shape[0]

    def body(x_ref, p_ref, dy_ref, dxf_ref, wpg_ref, wpp_ref, dx_ref, dgp_ref, dpj_ref):
        dz = 2.0 * dy_ref[...].astype(f32)
        wpg = wpg_ref[...].reshape(D, D)
        gate = _sigmoid(_dot(x_ref[...], wpg))
        proj = _dot(p_ref[...], wpp_ref[...])
        dgp = (dz * proj * gate * (1.0 - gate)).astype(bf16)
        dgp_ref[...] = dgp
        dpj_ref[...] = (dz * gate).astype(bf16)
        dx_ref[...] = dxf_ref[...] + _dot_nt(dgp, wpg)

    row = pl.BlockSpec((TM, D), lambda i: (i, 0))
    return pl.pallas_call(
        body, grid=(T // TM,),
        in_specs=[row, pl.BlockSpec((TM, PLE), lambda i: (i, 0)), row, row,
                  pl.BlockSpec((NDEV, None, 128, D), lambda i: (0, 3 * layer + 2, 0, 0)), _full_spec((PLE, D))],
        out_specs=[row, row, row],
        out_shape=[jax.ShapeDtypeStruct((T, D), f32), jax.ShapeDtypeStruct((T, D), bf16),
                   jax.ShapeDtypeStruct((T, D), bf16)],
        compiler_params=_cparams(("arbitrary",)), name="ple_bwd")(x, p, dy, dx_ffn, pr, wpp)


def _matmul_tn(a, b, nb, name, nsub=1):
    T, M = a.shape
    N = b.shape[1]
    wide = nsub * nb
    tk = min(T, 1024 if wide <= 2048 else 512)
    nk = T // tk

    def body(a_ref, b_ref, o_ref, acc):
        k = pl.program_id(1)

        @pl.when(k == 0)
        def _():
            acc[...] = jnp.zeros_like(acc)

        acc[...] += _dot_tn(a_ref[...], b_ref[...])

        @pl.when(k == nk - 1)
        def _():
            for j in range(nsub):
                o_ref[j] = acc[:, j * nb:(j + 1) * nb].astype(bf16)

    return pl.pallas_call(
        body, grid=(N // wide, nk),
        in_specs=[pl.BlockSpec((tk, M), lambda n, k: (k, 0)), pl.BlockSpec((tk, wide), lambda n, k: (k, n))],
        out_specs=pl.BlockSpec((nsub, M, nb), lambda n, k: (n, 0, 0)),
        out_shape=jax.ShapeDtypeStruct((N // nb, M, nb), bf16),
        scratch_shapes=[pltpu.VMEM((M, wide), f32)],
        compiler_params=_cparams(("arbitrary", "arbitrary")), name=name)(a, b)


def _proj_in(x, pr, pinl, ping, layer):
    T = x.shape[0]

    def body(x_ref, wr_ref, wl_ref, wg_ref, hr_ref, hl_ref, hg_ref):
        xb = x_ref[...].astype(bf16)
        hr_ref[...] = _dot(xb, wr_ref[...].reshape(D, D))
        hl_ref[...] = _dot(xb, wl_ref[...].reshape(D, 2 * LRU_W))
        hg_ref[...] = _dot(xb, wg_ref[...].reshape(D, GDN_IN))

    return pl.pallas_call(
        body, grid=(T // TM,),
        in_specs=[pl.BlockSpec((TM, D), lambda i: (i, 0)),
                  pl.BlockSpec((NDEV, None, 128, D), lambda i: (0, 3 * layer, 0, 0)),
                  pl.BlockSpec((NDEV, None, 128, 2 * LRU_W), lambda i: (0, layer, 0, 0)),
                  pl.BlockSpec((NDEV, None, 128, GDN_IN), lambda i: (0, layer, 0, 0))],
        out_specs=[pl.BlockSpec((TM, D), lambda i: (i, 0)), pl.BlockSpec((TM, 2 * LRU_W), lambda i: (i, 0)),
                   pl.BlockSpec((TM, GDN_IN), lambda i: (i, 0))],
        out_shape=[jax.ShapeDtypeStruct((T, D), f32), jax.ShapeDtypeStruct((T, 2 * LRU_W), f32),
                   jax.ShapeDtypeStruct((T, GDN_IN), f32)],
        compiler_params=_cparams(("arbitrary",)), name="proj_in")(x, pr, pinl, ping)


def _proj_in_bwd(base, dhr, dhl, dhg, pr, pinl, ping, layer):
    T = base.shape[0]

    def body(b_ref, dr_ref, dl_ref, dg_ref, wr_ref, wl_ref, wg_ref, o_ref):
        o_ref[...] = (b_ref[...] + _dot_nt(dr_ref[...], wr_ref[...].reshape(D, D))
                      + _dot_nt(dl_ref[...], wl_ref[...].reshape(D, 2 * LRU_W))
                      + _dot_nt(dg_ref[...], wg_ref[...].reshape(D, GDN_IN)))

    return pl.pallas_call(
        body, grid=(T // TM,),
        in_specs=[pl.BlockSpec((TM, D), lambda i: (i, 0)), pl.BlockSpec((TM, D), lambda i: (i, 0)),
                  pl.BlockSpec((TM, 2 * LRU_W), lambda i: (i, 0)), pl.BlockSpec((TM, GDN_IN), lambda i: (i, 0)),
                  pl.BlockSpec((NDEV, None, 128, D), lambda i: (0, 3 * layer, 0, 0)),
                  pl.BlockSpec((NDEV, None, 128, 2 * LRU_W), lambda i: (0, layer, 0, 0)),
                  pl.BlockSpec((NDEV, None, 128, GDN_IN), lambda i: (0, layer, 0, 0))],
        out_specs=pl.BlockSpec((TM, D), lambda i: (i, 0)),
        out_shape=jax.ShapeDtypeStruct((T, D), f32),
        compiler_params=_cparams(("arbitrary",)), name="proj_in_bwd")(base, dhr, dhl, dhg, pr, pinl, ping)


def _mix_out(x1, o_r, o_l, o_g, pr, lg, lb, layer):
    T = x1.shape[0]

    def body(x_ref, r_ref, l_ref, g_ref, w_ref, lg_ref, lb_ref, z_ref, o_ref):
        w = w_ref[...].reshape(D, D)
        z = (ALPHA * x_ref[...] + _dot(r_ref[...], w[0:RET_W]) + _dot(l_ref[...], w[RET_W:RET_W + LRU_W])
             + _dot(g_ref[...], w[RET_W + LRU_W:D]))
        z_ref[...] = z
        o_ref[...] = _ln_stats(z)[0] * lg_ref[...] + lb_ref[...]

    row = pl.BlockSpec((TM, D), lambda i: (i, 0))
    return pl.pallas_call(
        body, grid=(T // TM,),
        in_specs=[row, pl.BlockSpec((TM, RET_W), lambda i: (i, 0)), pl.BlockSpec((TM, LRU_W), lambda i: (i, 0)),
                  pl.BlockSpec((TM, GDN_W), lambda i: (i, 0)),
                  pl.BlockSpec((NDEV, None, 128, D), lambda i: (0, 3 * layer + 1, 0, 0)),
                  _full_spec((1, D)), _full_spec((1, D))],
        out_specs=[row, row], out_shape=[jax.ShapeDtypeStruct((T, D), f32)] * 2,
        compiler_params=_cparams(("arbitrary",)), name="mix_out")(x1, o_r, o_l, o_g, pr, lg, lb)


def _mix_out_bwd(z, dout, pr, lg, layer):
    T = z.shape[0]

    def body(z_ref, do_ref, w_ref, lg_ref, dxb_ref, dzb_ref, dr_ref, dl_ref, dg_ref, dlg_ref, dlb_ref):
        @pl.when(pl.program_id(0) == 0)
        def _():
            dlg_ref[...] = jnp.zeros_like(dlg_ref)
            dlb_ref[...] = jnp.zeros_like(dlb_ref)

        dz, dlg, dlb = _ln_bwd(z_ref[...], lg_ref[...], do_ref[...])
        dlg_ref[...] += dlg
        dlb_ref[...] += dlb
        dxb_ref[...] = ALPHA * dz
        dzb = dz.astype(bf16)
        dzb_ref[...] = dzb
        w = w_ref[...].reshape(D, D)
        dr_ref[...] = _dot_nt(dzb, w[0:RET_W])
        dl_ref[...] = _dot_nt(dzb, w[RET_W:RET_W + LRU_W])
        dg_ref[...] = _dot_nt(dzb, w[RET_W + LRU_W:D])

    row = pl.BlockSpec((TM, D), lambda i: (i, 0))
    vec = _full_spec((1, D))
    return pl.pallas_call(
        body, grid=(T // TM,),
        in_specs=[row, row, pl.BlockSpec((NDEV, None, 128, D), lambda i: (0, 3 * layer + 1, 0, 0)), vec],
        out_specs=[row, row, pl.BlockSpec((TM, RET_W), lambda i: (i, 0)), pl.BlockSpec((TM, LRU_W), lambda i: (i, 0)),
                   pl.BlockSpec((TM, GDN_W), lambda i: (i, 0)), vec, vec],
        out_shape=[jax.ShapeDtypeStruct((T, D), f32), jax.ShapeDtypeStruct((T, D), bf16),
                   jax.ShapeDtypeStruct((T, RET_W), f32), jax.ShapeDtypeStruct((T, LRU_W), f32),
                   jax.ShapeDtypeStruct((T, GDN_W), f32), jax.ShapeDtypeStruct((1, D), f32),
                   jax.ShapeDtypeStruct((1, D), f32)],
        compiler_params=_cparams(("arbitrary",)), name="mix_out_bwd")(z, dout, pr, lg)


def _loss_grad(y, target):
    T = y.shape[0]

    def body(y_ref, t_ref, dy_ref, l_ref):
        @pl.when(pl.program_id(0) == 0)
        def _():
            l_ref[...] = jnp.zeros_like(l_ref)

        e = y_ref[...] - t_ref[...]
        dy_ref[...] = e * (1.0 / D)
        l_ref[...] += 0.5 * jnp.sum(jnp.sum(e * e, -1, keepdims=True) * (1.0 / D), 0, keepdims=True)

    row = pl.BlockSpec((TM, D), lambda i: (i, 0))
    return pl.pallas_call(
        body, grid=(T // TM,), in_specs=[row, row], out_specs=[row, _full_spec((1, 1))],
        out_shape=[jax.ShapeDtypeStruct((T, D), f32), jax.ShapeDtypeStruct((1, 1), f32)],
        compiler_params=_cparams(("arbitrary",)), name="loss_grad")(y, target)


def _split_heads(x, H):
    n = x.shape[0] // CH
    parts = [x[:, h * CH:(h + 1) * CH].reshape(n, CH, CH) for h in range(H)]
    return jnp.stack(parts, axis=1).reshape(n * H, CH, CH)


def _merge_heads(ref, x, H, col0=0):
    n = x.shape[0] // H
    x4 = x.reshape(n, H, CH, CH)
    for h in range(H):
        ref[:, col0 + h * CH:col0 + (h + 1) * CH] = x4[:, h].reshape(n * CH, CH)


def _rows_down(x, before, s):
    r8 = lax.broadcasted_iota(jnp.int32, before.shape, 0)
    top = jnp.where(r8 < s, pltpu.roll(before, s, 0), pltpu.roll(x[0:8], s, 0))
    return jnp.concatenate([top, pltpu.roll(x, s, 0)[8:]], axis=0)


def _rows_up(x, after, s):
    R = x.shape[0]
    r8 = lax.broadcasted_iota(jnp.int32, after.shape, 0)
    bottom = jnp.where(r8 >= 8 - s, pltpu.roll(after, 8 - s, 0), pltpu.roll(x[R - 8:R], 8 - s, 0))
    return jnp.concatenate([pltpu.roll(x, R - s, 0)[0:R - 8], bottom], axis=0)


def _conv_fwd(ext, x, tail, w, R):
    ext[0:8, :] = tail
    ext[8:R + 8, :] = x
    y = w[3:4, :] * x
    for k in range(3):
        y = y + w[k:k + 1, :] * _rows_down(x, tail, 3 - k)
    return y


def _conv_bwd(ext, dy, dy_next, w, R):
    x, tail = ext[8:8 + R, :], ext[0:8, :]
    dx = w[3:4, :] * dy
    dws = []
    for k in range(3):
        dx = dx + w[k:k + 1, :] * _rows_up(dy, dy_next, 3 - k)
        dws.append(jnp.sum(dy * _rows_down(x, tail, 3 - k), 0, keepdims=True))
    dws.append(jnp.sum(dy * x, 0, keepdims=True))
    return dx, jnp.concatenate(dws, axis=0)


def _prev_tail_spec(R, W):
    return pl.BlockSpec((8, W), lambda i: (jnp.maximum(i * (R // 8) - 1, 0), 0))


def _prev_tail_spec_rev(R, W, nb):
    return pl.BlockSpec((8, W), lambda i: (jnp.maximum((nb - 1 - i) * (R // 8) - 1, 0), 0))


def _rope_tables(positions):
    T = positions.shape[0]

    def body(p_ref, c_ref, s_ref):
        lane = lax.broadcasted_iota(jnp.int32, (TM, RET_W), 1)
        fi = (lane % 32).astype(f32)
        inv = jnp.exp(fi * (-math.log(ROPE_THETA) / 32.0))
        ang = p_ref[...].astype(f32) * inv
        c_ref[...] = jnp.cos(ang)
        s_ref[...] = jnp.where(lane % CH < 32, -jnp.sin(ang), jnp.sin(ang))

    row = pl.BlockSpec((TM, RET_W), lambda i: (i, 0))
    return pl.pallas_call(
        body, grid=(T // TM,), in_specs=[pl.BlockSpec((TM, 1), lambda i: (i, 0))], out_specs=[row, row],
        out_shape=[jax.ShapeDtypeStruct((T, RET_W), f32)] * 2,
        compiler_params=_cparams(("arbitrary",)), name="rope_tables")(positions)


def _partner(x):
    lane = lax.broadcasted_iota(jnp.int32, x.shape, 1)
    return jnp.where(lane % CH < 32, pltpu.roll(x, RET_W - 32, 1), pltpu.roll(x, 32, 1))


def _ret_consts():
    ii = lax.broadcasted_iota(jnp.int32, (CH, CH), 0).astype(f32)
    jj = lax.broadcasted_iota(jnp.int32, (CH, CH), 1).astype(f32)
    intra, cross, tail, cd = [], [], [], []
    for h in range(RET_H):
        lg = math.log1p(-(2.0 ** (-5.0 - h)))
        intra.append(jnp.exp(jnp.abs(ii - jj) * lg))
        cross.append(jnp.exp((ii + 1.0) * lg))
        tail.append(jnp.exp((CH - 1.0 - ii) * lg))
        cd.append(jnp.full((CH, CH), math.exp(CH * lg), f32))
    return jnp.stack(intra), jnp.stack(cross), jnp.stack(tail), jnp.stack(cd)


def _ret_chunk(consts, q, k, v, st):
    intra, cross, tail, cd = consts
    s = _bmm('hid,hjd->hij', q, k) * intra
    o = _bmm('hij,hje->hie', s, v) + _bmm('hid,hde->hie', q * cross, st)
    st2 = st * cd + _bmm('hjd,hje->hde', k * tail, v)
    oc = o - jnp.mean(o, -1, keepdims=True)
    on = oc * lax.rsqrt(jnp.mean(oc * oc, -1, keepdims=True) + 1e-5)
    return on, st2


def _ret_fwd(hr, cosw, sinw, gam):
    T = hr.shape[0]
    R = RB_RET
    nc = R // CH

    def body(h_ref, c_ref, s_ref, g_ref, o_ref, st_ref, st, wide):
        @pl.when(pl.program_id(0) == 0)
        def _():
            st[...] = jnp.zeros_like(st)

        consts = _ret_consts()
        cw, sw = c_ref[...], s_ref[...]
        q, k = h_ref[:, 0:RET_W], h_ref[:, RET_W:2 * RET_W]
        qh = _split_heads((q * cw + _partner(q) * sw) * 0.125, RET_H)
        kh = _split_heads(k * cw + _partner(k) * sw, RET_H)
        vh = _split_heads(h_ref[:, 2 * RET_W:3 * RET_W], RET_H)
        outs = []
        s_cur = st[...]
        for c in range(nc):
            sl = slice(c * RET_H, (c + 1) * RET_H)
            st_ref[c] = s_cur
            on, s_cur = _ret_chunk(consts, qh[sl], kh[sl], vh[sl], s_cur)
            outs.append(on)
        st[...] = s_cur
        _merge_heads(wide, jnp.concatenate(outs, axis=0), RET_H)
        o_ref[...] = wide[...] * g_ref[...] * _silu(h_ref[:, 3 * RET_W:4 * RET_W])

    blk = pl.BlockSpec((R, RET_W), lambda i: (i, 0))
    return pl.pallas_call(
        body, grid=(T // R,),
        in_specs=[pl.BlockSpec((R, D), lambda i: (i, 0)), blk, blk, _full_spec((1, RET_W))],
        out_specs=[blk, pl.BlockSpec((nc, RET_H, CH, CH), lambda i: (i, 0, 0, 0))],
        out_shape=[jax.ShapeDtypeStruct((T, RET_W), f32), jax.ShapeDtypeStruct((T // CH, RET_H, CH, CH), f32)],
        scratch_shapes=[pltpu.VMEM((RET_H, CH, CH), f32), pltpu.VMEM((R, RET_W), f32)],
        compiler_params=_cparams(("arbitrary",)), name="ret_fwd")(hr, cosw, sinw, gam)


def _ret_bwd(hr, cosw, sinw, gam, states, dout):
    T = hr.shape[0]
    R = RB_RET
    nc = R // CH
    nb = T // R

    def body(h_ref, c_ref, s_ref, g_ref, st_ref, do_ref, dh_ref, dgam_ref, dst, wide):
        @pl.when(pl.program_id(0) == 0)
        def _():
            dst[...] = jnp.zeros_like(dst)
            dgam_ref[...] = jnp.zeros_like(dgam_ref)

        consts = _ret_consts()
        cw, sw = c_ref[...], s_ref[...]
        q, k = h_ref[:, 0:RET_W], h_ref[:, RET_W:2 * RET_W]
        gr = h_ref[:, 3 * RET_W:4 * RET_W]
        qh = _split_heads((q * cw + _partner(q) * sw) * 0.125, RET_H)
        kh = _split_heads(k * cw + _partner(k) * sw, RET_H)
        vh = _split_heads(h_ref[:, 2 * RET_W:3 * RET_W], RET_H)
        do = do_ref[...]
        gam = g_ref[...]
        sg = _silu(gr)
        don = _split_heads(do * gam * sg, RET_H)
        ons, dqs, dks, dvs = [None] * nc, [None] * nc, [None] * nc, [None] * nc
        ds = dst[...]
        for c in reversed(range(nc)):
            sl = slice(c * RET_H, (c + 1) * RET_H)
            (on, _), vjp = jax.vjp(functools.partial(_ret_chunk, consts), qh[sl], kh[sl], vh[sl], st_ref[c])
            dqs[c], dks[c], dvs[c], ds = vjp((don[sl], ds))
            ons[c] = on
        dst[...] = ds
        _merge_heads(wide, jnp.concatenate(ons, axis=0), RET_H)
        onw = wide[...]
        dgam_ref[...] += jnp.sum(do * onw * sg, 0, keepdims=True)
        dh_ref[:, 3 * RET_W:4 * RET_W] = (do * onw * gam * _dsilu(gr)).astype(bf16)
        _merge_heads(wide, jnp.concatenate(dqs, axis=0), RET_H)
        u = wide[...] * 0.125
        dh_ref[:, 0:RET_W] = (u * cw + _partner(u * sw)).astype(bf16)
        _merge_heads(wide, jnp.concatenate(dks, axis=0), RET_H)
        u = wide[...]
        dh_ref[:, RET_W:2 * RET_W] = (u * cw + _partner(u * sw)).astype(bf16)
        _merge_heads(wide, jnp.concatenate(dvs, axis=0), RET_H)
        dh_ref[:, 2 * RET_W:3 * RET_W] = wide[...].astype(bf16)

    blk = pl.BlockSpec((R, RET_W), lambda i: (nb - 1 - i, 0))
    return pl.pallas_call(
        body, grid=(nb,),
        in_specs=[pl.BlockSpec((R, D), lambda i: (nb - 1 - i, 0)), blk, blk, _full_spec((1, RET_W)),
                  pl.BlockSpec((nc, RET_H, CH, CH), lambda i: (nb - 1 - i, 0, 0, 0)), blk],
        out_specs=[pl.BlockSpec((R, D), lambda i: (nb - 1 - i, 0)), _full_spec((1, RET_W))],
        out_shape=[jax.ShapeDtypeStruct((T, D), bf16), jax.ShapeDtypeStruct((1, RET_W), f32)],
        scratch_shapes=[pltpu.VMEM((RET_H, CH, CH), f32), pltpu.VMEM((R, RET_W), f32)],
        compiler_params=_cparams(("arbitrary",)), name="ret_bwd")(hr, cosw, sinw, gam, states, dout)


def _lru_ab(xc, wa, ba, wx, bx, lam):
    r = _sigmoid(_dot(xc, wa) + ba)
    i = _sigmoid(_dot(xc, wx) + bx)
    la = 8.0 * r * (-_softplus(-lam))
    a = jnp.exp(la)
    em = jnp.tanh(la) * (jnp.exp(2.0 * la) + 1.0)
    return a, jnp.sqrt(-em) * (i * xc)


def _lru_out(h, gate):
    return h * _gelu(gate)


def _scan_fwd(a, b):
    R = a.shape[0]
    row = lax.broadcasted_iota(jnp.int32, a.shape, 0)
    d = 1
    while d < R:
        m = row >= d
        b = jnp.where(m, a * pltpu.roll(b, d, 0) + b, b)
        a = jnp.where(m, a * pltpu.roll(a, d, 0), a)
        d *= 2
    return a, b


def _scan_bwd(a, b):
    R = a.shape[0]
    row = lax.broadcasted_iota(jnp.int32, a.shape, 0)
    d = 1
    while d < R:
        m = row < R - d
        b = jnp.where(m, a * pltpu.roll(b, R - d, 0) + b, b)
        a = jnp.where(m, a * pltpu.roll(a, R - d, 0), a)
        d *= 2
    return b


def _lru_fwd(hl, cw, cb, wa, ba, wx, bx, lam):
    T = hl.shape[0]
    R = RB_LRU
    W = LRU_W

    def body(h_ref, t_ref, cw_ref, cb_ref, wa_ref, ba_ref, wx_ref, bx_ref, lam_ref, o_ref, hs_ref, carry, ext):
        first = pl.program_id(0) == 0

        @pl.when(first)
        def _():
            carry[...] = jnp.zeros_like(carry)

        tail = jnp.where(first, 0.0, t_ref[:, 0:W])
        xc = _conv_fwd(ext, h_ref[:, 0:W], tail, cw_ref[...], R) + cb_ref[...]
        a, b = _lru_ab(xc, wa_ref[...], ba_ref[...], wx_ref[...], bx_ref[...], lam_ref[...])
        ap, hloc = _scan_fwd(a, b)
        h = hloc + ap * carry[0:1, :]
        carry[...] = jnp.broadcast_to(h[R - 1:R, :], carry.shape)
        hs_ref[...] = h
        o_ref[...] = _lru_out(h, h_ref[:, W:2 * W])

    vec = _full_spec((1, W))
    blk = pl.BlockSpec((R, W), lambda i: (i, 0))
    return pl.pallas_call(
        body, grid=(T // R,),
        in_specs=[pl.BlockSpec((R, 2 * W), lambda i: (i, 0)), _prev_tail_spec(R, 2 * W), _full_spec((4, W)), vec,
                  _full_spec((W, W)), vec, _full_spec((W, W)), vec, vec],
        out_specs=[blk, blk], out_shape=[jax.ShapeDtypeStruct((T, W), f32)] * 2,
        scratch_shapes=[pltpu.VMEM((8, W), f32), pltpu.VMEM((R + 8, W), f32)],
        compiler_params=_cparams(("arbitrary",)), name="lru_fwd")(hl, hl, cw, cb, wa, ba, wx, bx, lam)


def _lru_bwd(hl, hs, cw, cb, wa, ba, wx, bx, lam, dout):
    T = hl.shape[0]
    R = RB_LRU
    W = LRU_W
    nb = T // R

    def body(h_ref, t_ref, hs_ref, hst_ref, cw_ref, cb_ref, wa_ref, ba_ref, wx_ref, bx_ref, lam_ref, do_ref,
             dh_ref, dcw_ref, dcb_ref, dwa_ref, dba_ref, dwx_ref, dbx_ref, dlam_ref, carry_g, carry_dy, ext):
        i = pl.program_id(0)
        last_blk = i == 0
        first_blk = i == nb - 1

        @pl.when(last_blk)
        def _():
            carry_g[...] = jnp.zeros_like(carry_g)
            carry_dy[...] = jnp.zeros_like(carry_dy)
            for r in (dcw_ref, dcb_ref, dwa_ref, dba_ref, dwx_ref, dbx_ref, dlam_ref):
                r[...] = jnp.zeros_like(r)

        tail = jnp.where(first_blk, 0.0, t_ref[:, 0:W])
        xc = _conv_fwd(ext, h_ref[:, 0:W], tail, cw_ref[...], R) + cb_ref[...]
        (a, _), vjp_ab = jax.vjp(_lru_ab, xc, wa_ref[...], ba_ref[...], wx_ref[...], bx_ref[...], lam_ref[...])
        hs = hs_ref[...]
        _, vjp_out = jax.vjp(_lru_out, hs, h_ref[:, W:2 * W])
        dh, dgate = vjp_out(do_ref[...])
        row = lax.broadcasted_iota(jnp.int32, (R, W), 0)
        dh = jnp.where(row == R - 1, dh + carry_g[0:1, :], dh)
        a_up = jnp.where(row == R - 1, 0.0, pltpu.roll(a, R - 1, 0))
        g = _scan_bwd(a_up, dh)
        carry_g[...] = jnp.broadcast_to(a[0:1, :] * g[0:1, :], carry_g.shape)
        hprev0 = jnp.where(first_blk, 0.0, hst_ref[7:8, :])
        hprev = jnp.where(row == 0, hprev0, pltpu.roll(hs, 1, 0))
        dxc, dwa, dba, dwx, dbx, dlam = vjp_ab((g * hprev, g))
        dwa_ref[...] += dwa
        dba_ref[...] += dba
        dwx_ref[...] += dwx
        dbx_ref[...] += dbx
        dlam_ref[...] += dlam
        dcb_ref[...] += jnp.sum(dxc, 0, keepdims=True)
        dx, dcw = _conv_bwd(ext, dxc, carry_dy[...], cw_ref[...], R)
        carry_dy[...] = dxc[0:8, :]
        dcw_ref[...] += dcw
        dh_ref[:, 0:W] = dx.astype(bf16)
        dh_ref[:, W:2 * W] = dgate.astype(bf16)

    vec = _full_spec((1, W))
    mat = _full_spec((W, W))
    blk = pl.BlockSpec((R, W), lambda i: (nb - 1 - i, 0))
    blk2 = pl.BlockSpec((R, 2 * W), lambda i: (nb - 1 - i, 0))
    return pl.pallas_call(
        body, grid=(nb,),
        in_specs=[blk2, _prev_tail_spec_rev(R, 2 * W, nb), blk, _prev_tail_spec_rev(R, W, nb), _full_spec((4, W)), vec,
                  mat, vec, mat, vec, vec, blk],
        out_specs=[blk2, _full_spec((4, W)), vec, mat, vec, mat, vec, vec],
        out_shape=[jax.ShapeDtypeStruct((T, 2 * W), bf16), jax.ShapeDtypeStruct((4, W), f32),
                   jax.ShapeDtypeStruct((1, W), f32), jax.ShapeDtypeStruct((W, W), f32),
                   jax.ShapeDtypeStruct((1, W), f32), jax.ShapeDtypeStruct((W, W), f32),
                   jax.ShapeDtypeStruct((1, W), f32), jax.ShapeDtypeStruct((1, W), f32)],
        scratch_shapes=[pltpu.VMEM((8, W), f32), pltpu.VMEM((8, W), f32), pltpu.VMEM((R + 8, W), f32)],
        compiler_params=_cparams(("arbitrary",)), name="lru_bwd")(hl, hl, hs, hs, cw, cb, wa, ba, wx, bx, lam, dout)


def _head_ones():
    i = lax.broadcasted_iota(jnp.int32, (GDN_W, GDN_W), 0)
    j = lax.broadcasted_iota(jnp.int32, (GDN_W, GDN_W), 1)
    return jnp.where(jnp.bitwise_xor(i, j) < CH, 1.0, 0.0).astype(bf16)


def _head_sums(x, ones):
    return jnp.dot(x.astype(bf16), ones, preferred_element_type=f32)


def _l2n(y, ones):
    r = lax.rsqrt(_head_sums(y * y, ones) + 1e-6)
    return y * r, r


def _l2n_bwd(dn, n, r, ones):
    return r * (dn - n * _head_sums(dn * n, ones))


def _gdn_local(inverse, q, k, vs, gc, bb):
    B = q.shape[0]
    ii = lax.broadcasted_iota(jnp.int32, (B, CH, CH), 1)
    jj = lax.broadcasted_iota(jnp.int32, (B, CH, CH), 2)
    gct = jnp.swapaxes(gc, 1, 2)
    decay = jnp.where(ii >= jj, jnp.exp(jnp.minimum(gc - gct, 0.0)), 0.0)
    kk = _bmm('bid,bjd->bij', k, k)
    inv = inverse(-jnp.where(ii > jj, bb * kk * decay, 0.0))
    egc = jnp.exp(gc)
    u = _bmm('bij,bje->bie', inv, vs * bb)
    w = _bmm('bij,bje->bie', inv, k * (bb * egc))
    qk = _bmm('bid,bjd->bij', q, k) * (0.125 * decay)
    glast = gc[:, CH - 1:CH, :]
    return u, w, qk, q * (0.125 * egc), k * jnp.exp(glast - gc), jnp.exp(jnp.broadcast_to(glast, gc.shape))


def _gdn_step(st, u, w, qk, qd, kt, egl, z, gn):
    vnew = u - _bmm('hcd,hde->hce', w, st)
    o = _bmm('hcd,hde->hce', qd, st) + _bmm('hij,hje->hie', qk, vnew)
    st2 = st * egl + _bmm('hcd,hce->hde', kt, vnew)
    out = o * lax.rsqrt(_rowsum(o * o) * (1.0 / CH) + 1e-6) * gn * _silu(z)
    return out, st2


def _gdn_scalars(ab, alog, dtb):
    sp = _softplus(ab + dtb)
    return -jnp.exp(alog) * sp, _sigmoid(ab)


def _bcast_heads(blk, lane0, H):
    R = blk.shape[0]
    n = R // CH
    parts = [jnp.broadcast_to(blk[:, lane0 + h:lane0 + h + 1], (R, CH)).reshape(n, CH, CH) for h in range(H)]
    return jnp.stack(parts, axis=1).reshape(n * H, CH, CH)


def _unbcast_heads(x, lane0, H):
    n = x.shape[0] // H
    R = n * CH
    s = jnp.sum(x, axis=2, keepdims=True).reshape(n, H, CH, 1)
    lane = lax.broadcasted_iota(jnp.int32, (R, 128), 1)
    acc = jnp.zeros((R, 128), f32)
    for h in range(H):
        acc = acc + jnp.where(lane == lane0 + h, jnp.broadcast_to(s[:, h].reshape(R, 1), (R, 128)), 0.0)
    return acc


def _gdn_fwd(hg, cw, alog, dtb, gn):
    T = hg.shape[0]
    R = RB_GDN
    nc = R // CH
    W3 = 3 * GDN_W
    H = GDN_H

    def body(h_ref, t_ref, cw_ref, al_ref, dt_ref, gn_ref, o_ref, st_ref, inv_ref, st, ext):
        first = pl.program_id(0) == 0

        @pl.when(first)
        def _():
            st[...] = jnp.zeros_like(st)

        def inverse(m):
            inv = _neumann_inv(m)
            inv_ref[...] = inv
            return inv

        tail = jnp.where(first, 0.0, t_ref[:, 0:W3])
        y = _silu(_conv_fwd(ext, h_ref[:, 0:W3], tail, cw_ref[...], R))
        ones = _head_ones()
        qs = _split_heads(_l2n(y[:, 0:GDN_W], ones)[0], H)
        ks = _split_heads(_l2n(y[:, GDN_W:2 * GDN_W], ones)[0], H)
        vs = _split_heads(y[:, 2 * GDN_W:W3], H)
        zh = _split_heads(h_ref[:, W3:W3 + GDN_W], H)
        g, beta = _gdn_scalars(h_ref[:, W3 + GDN_W:GDN_IN], al_ref[...], dt_ref[...])
        loc = _gdn_local(inverse, qs, ks, vs, _bcast_heads(_chunk_cumsum(g), 0, H), _bcast_heads(beta, H, H))
        gnv = gn_ref[...]
        outs = []
        s_cur = st[...]
        for c in range(nc):
            sl = slice(c * H, (c + 1) * H)
            st_ref[c] = s_cur
            out, s_cur = _gdn_step(s_cur, *(t[sl] for t in loc), zh[sl], gnv)
            outs.append(out)
        st[...] = s_cur
        _merge_heads(o_ref, jnp.concatenate(outs, axis=0), H)

    return pl.pallas_call(
        body, grid=(T // R,),
        in_specs=[pl.BlockSpec((R, GDN_IN), lambda i: (i, 0)), _prev_tail_spec(R, GDN_IN), _full_spec((4, W3)),
                  _full_spec((1, 128)), _full_spec((1, 128)), _full_spec((1, CH))],
        out_specs=[pl.BlockSpec((R, GDN_W), lambda i: (i, 0)), pl.BlockSpec((nc, H, CH, CH), lambda i: (i, 0, 0, 0)),
                   pl.BlockSpec((nc * H, CH, CH), lambda i: (i, 0, 0))],
        out_shape=[jax.ShapeDtypeStruct((T, GDN_W), f32), jax.ShapeDtypeStruct((T // CH, H, CH, CH), f32),
                   jax.ShapeDtypeStruct((T // CH * H, CH, CH), f32)],
        scratch_shapes=[pltpu.VMEM((H, CH, CH), f32), pltpu.VMEM((R + 8, W3), f32)],
        compiler_params=_cparams(("arbitrary",)), name="gdn_fwd")(hg, hg, cw, alog, dtb, gn)


def _gdn_bwd(hg, cw, alog, dtb, gn, states, invs, dout):
    T = hg.shape[0]
    R = RB_GDN
    nc = R // CH
    nb = T // R
    W3 = 3 * GDN_W
    H = GDN_H

    def body(h_ref, t_ref, cw_ref, al_ref, dt_ref, gn_ref, st_ref, inv_ref, do_ref,
             dh_ref, dcw_ref, dal_ref, ddt_ref, dgn_ref, dst, carry_dy, ext, wide):
        i = pl.program_id(0)
        first_blk = i == nb - 1

        @pl.when(i == 0)
        def _():
            dst[...] = jnp.zeros_like(dst)
            carry_dy[...] = jnp.zeros_like(carry_dy)
            for r in (dcw_ref, dal_ref, ddt_ref, dgn_ref):
                r[...] = jnp.zeros_like(r)

        tail = jnp.where(first_blk, 0.0, t_ref[:, 0:W3])
        ypre = _conv_fwd(ext, h_ref[:, 0:W3], tail, cw_ref[...], R)
        y = _silu(ypre)
        ones = _head_ones()
        qn, rq = _l2n(y[:, 0:GDN_W], ones)
        kn, rk = _l2n(y[:, GDN_W:2 * GDN_W], ones)
        qs, ks, vs = _split_heads(qn, H), _split_heads(kn, H), _split_heads(y[:, 2 * GDN_W:W3], H)
        zh = _split_heads(h_ref[:, W3:W3 + GDN_W], H)
        ab = h_ref[:, W3 + GDN_W:GDN_IN]
        alog, dtb = al_ref[...], dt_ref[...]
        g, beta = _gdn_scalars(ab, alog, dtb)
        kept = inv_ref[...]
        loc, vjp_loc = jax.vjp(functools.partial(_gdn_local, lambda m: _known_inv(m, kept)), qs, ks, vs,
                               _bcast_heads(_chunk_cumsum(g), 0, H), _bcast_heads(beta, H, H))
        doh = _split_heads(do_ref[...], H)
        gnv = gn_ref[...]
        dloc = [[None] * nc for _ in range(6)]
        dzs = [None] * nc
        ds = dst[...]
        dgn = jnp.zeros((1, CH), f32)
        for c in reversed(range(nc)):
            sl = slice(c * H, (c + 1) * H)
            _, vjp = jax.vjp(_gdn_step, st_ref[c], *(t[sl] for t in loc), zh[sl], gnv)
            grads = vjp((doh[sl], ds))
            ds = grads[0]
            for j in range(6):
                dloc[j][c] = grads[1 + j]
            dzs[c] = grads[7]
            dgn = dgn + grads[8]
        dst[...] = ds
        dgn_ref[...] += dgn
        dqs, dks, dvs, dgb, dbb = vjp_loc(tuple(jnp.concatenate(d, axis=0) for d in dloc))
        lane = lax.broadcasted_iota(jnp.int32, (R, 128), 1)
        dg = _chunk_cumsum(_unbcast_heads(dgb, 0, H), reverse=True)
        dbeta = _unbcast_heads(dbb, H, H)
        da = dg * (-jnp.exp(alog)) * _sigmoid(ab + dtb)
        dh_ref[:, W3 + GDN_W:GDN_IN] = jnp.where(lane < H, da, dbeta * beta * (1.0 - beta)).astype(bf16)
        ddt_ref[...] += jnp.sum(jnp.where(lane < H, da, 0.0), 0, keepdims=True)
        dal_ref[...] += jnp.sum(jnp.where(lane < H, dg * g, 0.0), 0, keepdims=True)
        for j, dpart in enumerate((dqs, dks, dvs)):
            _merge_heads(wide, dpart, H, col0=j * GDN_W)
        wide[:, 0:GDN_W] = _l2n_bwd(wide[:, 0:GDN_W], qn, rq, ones)
        wide[:, GDN_W:2 * GDN_W] = _l2n_bwd(wide[:, GDN_W:2 * GDN_W], kn, rk, ones)
        dy = wide[...] * _dsilu(ypre)
        dx, dcw = _conv_bwd(ext, dy, carry_dy[...], cw_ref[...], R)
        carry_dy[...] = dy[0:8, :]
        dcw_ref[...] += dcw
        dh_ref[:, 0:W3] = dx.astype(bf16)
        _merge_heads(wide, jnp.concatenate(dzs, axis=0), H)
        dh_ref[:, W3:W3 + GDN_W] = wide[:, 0:GDN_W].astype(bf16)

    blk = pl.BlockSpec((R, GDN_IN), lambda i: (nb - 1 - i, 0))
    return pl.pallas_call(
        body, grid=(nb,),
        in_specs=[blk, _prev_tail_spec_rev(R, GDN_IN, nb), _full_spec((4, W3)), _full_spec((1, 128)),
                  _full_spec((1, 128)), _full_spec((1, CH)),
                  pl.BlockSpec((nc, H, CH, CH), lambda i: (nb - 1 - i, 0, 0, 0)),
                  pl.BlockSpec((nc * H, CH, CH), lambda i: (nb - 1 - i, 0, 0)),
                  pl.BlockSpec((R, GDN_W), lambda i: (nb - 1 - i, 0))],
        out_specs=[blk, _full_spec((4, W3)), _full_spec((1, 128)), _full_spec((1, 128)), _full_spec((1, CH))],
        out_shape=[jax.ShapeDtypeStruct((T, GDN_IN), bf16), jax.ShapeDtypeStruct((4, W3), f32),
                   jax.ShapeDtypeStruct((1, 128), f32), jax.ShapeDtypeStruct((1, 128), f32),
                   jax.ShapeDtypeStruct((1, CH), f32)],
        scratch_shapes=[pltpu.VMEM((H, CH, CH), f32), pltpu.VMEM((8, W3), f32), pltpu.VMEM((R + 8, W3), f32),
                        pltpu.VMEM((R, W3), f32)],
        compiler_params=_cparams(("arbitrary",)), name="gdn_bwd")(hg, hg, cw, alog, dtb, gn, states, invs, dout)


def _block_diag(w):
    out = jnp.zeros((LRU_W, LRU_W), w.dtype)
    for g in range(w.shape[0]):
        out = lax.dynamic_update_slice(out, w[g], (g * CH, g * CH))
    return out


def _block_diag_t(w):
    return jnp.stack([w[g * CH:(g + 1) * CH, g * CH:(g + 1) * CH] for g in range(LRU_W // CH)])


def _pad_lanes(v, n=128):
    return jnp.pad(v, (0, n - v.shape[0]))[None, :]


def _local_step(x, p, positions, target, fetch, emit, sm):
    cosw, sinw = _rope_tables(positions)
    cols = lambda w: jnp.transpose(w, (1, 2, 0, 3)).reshape(-1, D, NDEV * FSP)
    saved = []
    h = x
    for l in range(DEPTH):
        v = lambda n: sm[n][l][None, :]
        F1, tok = fetch(l, 'f1', h)
        p384a, pda = cols(F1['p384']), F1['pd']
        z1, x1, g1, u1, a1 = _ffn_fwd(h, p384a, pda, v('ln_ffn1_g') + tok, v('ln_ffn1_b'), 0, 0)
        G, _ = fetch(l, 'rest', x1)
        p384, pd, pr, pinl, ping, wpp = cols(G['p384']), G['pd'], G['pr'], G['pinl'], G['ping'], G['wpp']
        wts = (p384a, pda, p384, pd, pr, pinl, ping, wpp)
        hr, hl, hg = _proj_in(x1, pr, pinl, ping, 0)
        o_r, rst = _ret_fwd(hr, cosw, sinw, v('ret_norm_g'))
        lru_args = (sm['lru_conv_w'][l], v('lru_conv_b'), _block_diag(sm['lru_w_a'][l]), v('lru_b_a'),
                    _block_diag(sm['lru_w_x'][l]), v('lru_b_x'), v('lru_lambda'))
        o_l, hs = _lru_fwd(hl, *lru_args)
        gdn_args = (sm['gdn_conv_w'][l], _pad_lanes(sm['gdn_a_log'][l]), _pad_lanes(sm['gdn_dt_bias'][l]),
                    v('gdn_norm_g'))
        o_g, *gst = _gdn_fwd(hg, *gdn_args)
        z2, x2 = _mix_out(x1, o_r, o_l, o_g, pr, v('ln_mix_g'), v('ln_mix_b'), 0)
        z3, x3, g2, u2, a2 = _ffn_fwd(x2, p384, pd, v('ln_ffn2_g'), v('ln_ffn2_b'), 0, 1, ple=(p[l], pr, wpp))
        saved.append((h, z1, x1, hr, hl, hg, o_r, rst, o_l, hs, lru_args, o_g, gst, gdn_args, z2, x2, z3,
                      g1, u1, a1, g2, u2, a2, wts))
        h = x3
    d, loss = _loss_grad(h, target)

    small = {n: [None] * DEPTH for n in SMALL}
    tok = 0.0
    for l in reversed(range(DEPTH)):
        (x0, z1, x1, hr, hl, hg, o_r, rst, o_l, hs, lru_args, o_g, gst, gdn_args, z2, x2, z3,
         g1, u1, a1, g2, u2, a2, wts) = saved[l]
        p384a, pda, p384, pd, pr, pinl, ping, wpp = wts
        v = lambda n: sm[n][l][None, :]
        rows = lambda m: m.reshape(NDEV, m.shape[1] // NDEV, m.shape[2])
        d2, dg2, du2, dy2, small['ln_ffn2_g'][l], small['ln_ffn2_b'][l] = _ffn_bwd(
            z3, d, g2, u2, p384, pd, v('ln_ffn2_g') + tok, 0, 1)
        d2, dgp, dpj = _ple_bwd(x2, p[l], dy2, d2, pr, wpp, 0)
        dxb, dzb, do_r, do_l, do_g, small['ln_mix_g'][l], small['ln_mix_b'][l] = _mix_out_bwd(
            z2, d2, pr, v('ln_mix_g'), 0)
        dhr, small['ret_norm_g'][l] = _ret_bwd(hr, cosw, sinw, v('ret_norm_g'), rst, do_r)
        (dhl, small['lru_conv_w'][l], small['lru_conv_b'][l], dwa, small['lru_b_a'][l], dwx, small['lru_b_x'][l],
         small['lru_lambda'][l]) = _lru_bwd(hl, hs, *lru_args, do_l)
        small['lru_w_a'][l], small['lru_w_x'][l] = _block_diag_t(dwa), _block_diag_t(dwx)
        dhg, small['gdn_conv_w'][l], dal, ddt, small['gdn_norm_g'][l] = _gdn_bwd(hg, *gdn_args, *gst, do_g)
        small['gdn_a_log'][l], small['gdn_dt_bias'][l] = dal[:, 0:GDN_H], ddt[:, 0:GDN_H]
        d1 = _proj_in_bwd(dxb, dhr, dhl, dhg, pr, pinl, ping, 0)
        dwo = jnp.concatenate([_matmul_tn(o_r, dzb, D, "dw_out_r"), _matmul_tn(o_l, dzb, D, "dw_out_l"),
                               _matmul_tn(o_g, dzb, D, "dw_out_g")], axis=1)
        tok = emit(l, 'rest', {
            'p384': jnp.stack([_matmul_tn(x2, dg2, FSP, "dw_gate", NDEV), _matmul_tn(x2, du2, FSP, "dw_up", NDEV)],
                              axis=1),
            'pd': rows(_matmul_tn(a2, dy2, D, "dw_down"))[:, None],
            'pr': jnp.stack([rows(_matmul_tn(x1, dhr, D, "dw_in_r")), rows(dwo),
                             rows(_matmul_tn(x2, dgp, D, "dw_ple_gate"))], axis=1),
            'pinl': rows(_matmul_tn(x1, dhl, 2 * LRU_W, "dw_in_l"))[:, None],
            'ping': rows(_matmul_tn(x1, dhg, GDN_IN, "dw_in_g"))[:, None],
            'ppp': jnp.transpose(_matmul_tn(p[l], dpj, D, "dw_ple_proj").reshape(PLE, NDEV, 128), (1, 0, 2))[:, None]})
        d, dg1, du1, dy1, small['ln_ffn1_g'][l], small['ln_ffn1_b'][l] = _ffn_bwd(
            z1, d1, g1, u1, p384a, pda, v('ln_ffn1_g') + tok, 0, 0)
        if l == 0:
            emit(l, 'small', {n: jnp.stack([g.reshape(sm[n].shape[1:]) for g in gs]) for n, gs in small.items()})
        tok = emit(l, 'f1', {
            'p384': jnp.stack([_matmul_tn(x0, dg1, FSP, "dw_gate", NDEV), _matmul_tn(x0, du1, FSP, "dw_up", NDEV)],
                              axis=1),
            'pd': rows(_matmul_tn(a1, dy1, D, "dw_down"))[:, None]})
    return loss, d


def _pack_big(ws, dtype=bf16):
    padc = lambda a, n: jnp.pad(a, ((0, 0), (0, 0), (0, n - a.shape[2])))
    padr = lambda a, n: jnp.pad(a, ((0, 0), (0, n - a.shape[1]), (0, 0)))
    per_layer = lambda arrs: jnp.stack(arrs, axis=1).reshape((-1,) + arrs[0].shape[1:])
    w_in = ws['w_in']
    out = {
        'p384': per_layer([padc(ws[n], FSP) for n in ('ffn1_w_gate', 'ffn1_w_up', 'ffn2_w_gate', 'ffn2_w_up')]),
        'pd': per_layer([padr(ws[n], FSP) for n in ('ffn1_w_down', 'ffn2_w_down')]),
        'pr': per_layer([w_in[:, :, 0:D], ws['w_out'], ws['ple_w_gate']]),
        'pinl': w_in[:, :, D:D + 2 * LRU_W],
        'ping': padc(w_in[:, :, D + 2 * LRU_W:D_IN], GDN_IN),
        'ppp': ws['ple_w_proj'],
    }
    return {k: a.astype(dtype) for k, a in out.items()}


def _gather_two_level(arrays, name):
    n = len(arrays)

    def body(*refs):
        ins, outs = refs[:n], refs[n:2 * n]
        send_sems, recv_sems, local_sems = refs[2 * n:]
        x, y, c = lax.axis_index("x"), lax.axis_index("y"), lax.axis_index("c")
        me, sibling = (x, y, c), (x, y, 1 - c)
        chips = [(1 - x, y), (x, 1 - y), (1 - x, 1 - y)]
        slot = lambda d: 4 * d[0] + 2 * d[1] + d[2]

        def copy(i, k, block, to, src=None):
            return pltpu.make_async_remote_copy(
                src_ref=outs[i].at[slot(block)] if src is None else src, dst_ref=outs[i].at[slot(block)],
                send_sem=send_sems.at[i, k], recv_sem=recv_sems.at[i, k], device_id=to,
                device_id_type=pl.DeviceIdType.MESH)

        mine, first, passed = [], [], []
        for i in range(n):
            cp = pltpu.make_async_copy(ins[i], outs[i].at[slot(me)], local_sems.at[i])
            cp.start()
            mine.append(cp)
            first.append(copy(i, 0, me, sibling, src=ins[i]))
            first += [copy(i, 1 + j, me, (*chip, c), src=ins[i]) for j, chip in enumerate(chips)]
        for cp in first:
            cp.start()
        for i in range(n):
            for j, chip in enumerate(chips):
                copy(i, 1 + j, (*chip, c), me).wait_recv()
                cp = copy(i, 4 + j, (*chip, c), sibling)
                cp.start()
                passed.append(cp)
        for i in range(n):
            copy(i, 0, sibling, me).wait_recv()
            for j, chip in enumerate(chips):
                copy(i, 4 + j, (*chip, 1 - c), me).wait_recv()
        for cp in first + passed:
            cp.wait_send()
        for cp in mine:
            cp.wait()

    hbm = pl.BlockSpec(memory_space=pltpu.HBM)
    return pl.pallas_call(
        body, in_specs=[hbm] * n, out_specs=[hbm] * n,
        out_shape=[jax.ShapeDtypeStruct((NDEV,) + a.shape, a.dtype) for a in arrays],
        scratch_shapes=[pltpu.SemaphoreType.DMA((n, NDEV - 1)), pltpu.SemaphoreType.DMA((n, NDEV - 1)),
                        pltpu.SemaphoreType.DMA((n,))],
        compiler_params=pltpu.CompilerParams(has_side_effects=True), name=name)(*arrays)


def _scatter_pairs(arrays, name):
    n = len(arrays)

    def body(*refs):
        ins, gots = refs[:n], refs[n:2 * n]
        send_sems, recv_sems = refs[2 * n:]
        x, y, c = lax.axis_index("x"), lax.axis_index("y"), lax.axis_index("c")
        sends = []
        for i in range(n):
            for q in range(4):
                cp = pltpu.make_async_remote_copy(
                    src_ref=ins[i].at[2 * q + 1 - c], dst_ref=gots[i].at[q], send_sem=send_sems.at[i, q],
                    recv_sem=recv_sems.at[i, q], device_id=(x, y, 1 - c), device_id_type=pl.DeviceIdType.MESH)
                cp.start()
                sends.append(cp)
        for cp in sends:
            cp.wait_recv()
        for cp in sends:
            cp.wait_send()

    hbm = pl.BlockSpec(memory_space=pltpu.HBM)
    return pl.pallas_call(
        body, in_specs=[hbm] * n, out_specs=[hbm] * n,
        out_shape=[jax.ShapeDtypeStruct((4,) + a.shape[1:], a.dtype) for a in arrays],
        scratch_shapes=[pltpu.SemaphoreType.DMA((n, 4)), pltpu.SemaphoreType.DMA((n, 4))],
        compiler_params=pltpu.CompilerParams(has_side_effects=True), name=name)(*arrays)


def _pair_sum(own, got, name):
    def body(a_ref, b_ref, o_ref):
        o_ref[...] = (a_ref[...].astype(f32) + b_ref[...].astype(f32)).astype(bf16)

    spec = pl.BlockSpec((None, None) + own.shape[2:], lambda q, s: (q, s, 0, 0))
    return pl.pallas_call(
        body, grid=own.shape[:2], in_specs=[spec, spec], out_specs=spec,
        out_shape=jax.ShapeDtypeStruct(own.shape, bf16),
        compiler_params=_cparams(("arbitrary", "arbitrary")), name=name)(own, got)


def _gather_plan(srcs, lands, x, y, c):
    me = 4 * x + 2 * y + c
    sends, arrivals = [], []
    for j in range(1, NDEV):
        peer, source = (me + j) % NDEV, (me + NDEV - j) % NDEV
        for i in range(len(srcs)):
            k = i * (NDEV - 1) + j - 1
            sends.append((srcs[i], lands[i].at[me], (peer // 4, (peer // 2) % 2, peer % 2), k))
            arrivals.append((srcs[i], lands[i].at[source], (source // 4, (source // 2) % 2, source % 2), k))
    return sends, arrivals


def _chips_plan(srcs, lands, x, y, c):
    chip = 2 * x + y
    sends, arrivals = [], []
    for j in range(1, 4):
        peer, source = (chip + j) % 4, (chip + 4 - j) % 4
        for i in range(len(srcs)):
            k = i * 3 + j - 1
            sends.append((srcs[i].at[peer], lands[i].at[chip], (peer // 2, peer % 2, c), k))
            arrivals.append((srcs[i].at[chip], lands[i].at[source], (source // 2, source % 2, c), k))
    return sends, arrivals


def _remote(entry, send_sems, recv_sems):
    src, dst, dev, k = entry
    return pltpu.make_async_remote_copy(src_ref=src, dst_ref=dst, send_sem=send_sems.at[k], recv_sem=recv_sems.at[k],
                                        device_id=dev, device_id_type=pl.DeviceIdType.MESH)


_HBM = pl.BlockSpec(memory_space=pltpu.HBM)
_SEM = pl.BlockSpec(memory_space=pltpu.SEMAPHORE)


def _split_start(arrays, land_shapes, plan, npeer, name):
    n = len(arrays)

    def body(*refs):
        srcs, lands = refs[:n], refs[n:2 * n]
        send_sems, recv_sems, token = refs[2 * n], refs[2 * n + 1], refs[-1]
        sends, _ = plan(srcs, lands, lax.axis_index("x"), lax.axis_index("y"), lax.axis_index("c"))
        for entry in sends:
            _remote(entry, send_sems, recv_sems).start()
        token[...] = jnp.zeros_like(token)

    lands = [lax.empty(s, a.dtype) for s, a in zip(land_shapes, arrays)]
    thru = [pltpu.HBM(a.shape, a.dtype) for a in arrays + lands]
    out = pl.pallas_call(
        body, name=name, in_specs=[_HBM] * (2 * n),
        out_specs=(_SEM, _SEM, *([_HBM] * (2 * n)), pl.BlockSpec(memory_space=pltpu.VMEM)),
        out_shape=(pltpu.SemaphoreType.DMA((n * npeer,)), pltpu.SemaphoreType.DMA((n * npeer,)), *thru,
                   jax.ShapeDtypeStruct((8, 128), f32)),
        input_output_aliases={i: 2 + i for i in range(2 * n)},
        compiler_params=pltpu.CompilerParams(has_side_effects=pltpu.SideEffectType.DATAFLOW_SIDE_EFFECTING),
    )(*[pltpu.with_memory_space_constraint(a, pltpu.HBM) for a in arrays + lands])
    return out[0], out[1], list(out[2:2 + n]), list(out[2 + n:2 + 2 * n]), out[-1]


def _split_wait(send_sems, recv_sems, srcs, lands, after, plan, name):
    n = len(srcs)

    def body(*refs):
        s_refs, l_refs = refs[:n], refs[n:2 * n]
        ssem, rsem = refs[2 * n], refs[2 * n + 1]
        sends, arrivals = plan(s_refs, l_refs, lax.axis_index("x"), lax.axis_index("y"), lax.axis_index("c"))
        for entry in sends:
            _remote(entry, ssem, rsem).wait_send()
        for entry in arrivals:
            _remote(entry, ssem, rsem).wait_recv()

    out = pl.pallas_call(
        body, name=name, in_specs=[_HBM] * (2 * n) + [_SEM, _SEM, pl.BlockSpec(memory_space=pl.ANY)],
        out_specs=[_HBM] * (2 * n), out_shape=[pltpu.HBM(a.shape, a.dtype) for a in srcs + lands],
        input_output_aliases={i: i for i in range(2 * n)},
        compiler_params=pltpu.CompilerParams(has_side_effects=pltpu.SideEffectType.DATAFLOW_SIDE_EFFECTING),
    )(*srcs, *lands, send_sems, recv_sems, after)
    return list(out[:n]), list(out[n:])


def _adam_math(w, g, m, v):
    m2 = ADAM_B1 * m + (1.0 - ADAM_B1) * g
    v2 = ADAM_B2 * v + (1.0 - ADAM_B2) * (g * g)
    m_hat = m2 / (1.0 - ADAM_B1 ** ADAM_STEP)
    v_hat = v2 / (1.0 - ADAM_B2 ** ADAM_STEP)
    return -ADAM_LR * (m_hat / (jnp.sqrt(v_hat) + ADAM_EPS) + ADAM_WD * w), m2, v2


def _adam_big(parts, w, m, v, anchor, name):
    L, rows, cols = w.shape
    flat = [(a, slot) for layer_parts in parts for a, slot in layer_parts]
    per = len(parts[0])

    def body(*refs):
        prefs = refs[:len(flat)]
        w_ref, m_ref, v_ref, _, g_ref, d_ref, m2_ref, v2_ref = refs[len(flat):]
        for li in range(L):
            @pl.when(pl.program_id(0) == li)
            def _():
                c0 = 0
                for pref in prefs[li * per:(li + 1) * per]:
                    acc = pref[0].astype(f32)
                    for s in range(1, pref.shape[0]):
                        acc = acc + pref[s].astype(f32)
                    width = min(acc.shape[1], cols - c0)
                    g_ref[:, c0:c0 + width] = acc[0:rows, 0:width]
                    c0 += width

        d, m2, v2 = _adam_math(w_ref[...], g_ref[...], m_ref[...], v_ref[...])
        d_ref[...] = d
        m2_ref[...] = m2
        v2_ref[...] = v2

    wspec = pl.BlockSpec((None, rows, cols), lambda l: (l, 0, 0))
    in_specs = [pl.BlockSpec((a.shape[0], None) + a.shape[2:], functools.partial(lambda l, slot: (0, slot, 0, 0), slot=slot))
                for a, slot in flat]
    return pl.pallas_call(
        body, grid=(L,), in_specs=in_specs + [wspec] * 3 + [_full_spec((8, 128))], out_specs=[wspec] * 4,
        out_shape=[jax.ShapeDtypeStruct(w.shape, f32)] * 4,
        compiler_params=_cparams(("arbitrary",)), name=name)(*[a for a, _ in flat], w, m, v, anchor)


def _sum_sources(stacked):
    rows = stacked.shape[1]

    def body(s_ref, o_ref):
        acc = s_ref[0]
        for s in range(1, NDEV):
            acc = acc + s_ref[s]
        o_ref[...] = acc

    return pl.pallas_call(body, out_shape=jax.ShapeDtypeStruct((rows, 128), f32), name="sum_small_grads")(stacked)


def _adam_small(w, g, m, v):
    def body(w_ref, g_ref, m_ref, v_ref, d_ref, m2_ref, v2_ref):
        d, m2, v2 = _adam_math(w_ref[...], g_ref[...], m_ref[...], v_ref[...])
        d_ref[...] = d
        m2_ref[...] = m2
        v2_ref[...] = v2

    return pl.pallas_call(body, out_shape=[jax.ShapeDtypeStruct(w.shape, f32)] * 3, name="adam_small")(w, g, m, v)


def _pack_rows(arrs):
    flat = []
    for a in arrs:
        a = a.reshape(-1)
        flat.append(jnp.pad(a, (0, (-a.shape[0]) % 1024)))
    return jnp.concatenate(flat).reshape(-1, 128)


def _unpack_rows(packed, shapes):
    out, off = [], 0
    flat = packed.reshape(-1)
    for s in shapes:
        n = math.prod(s)
        out.append(flat[off:off + n].reshape(s))
        off += n + (-n) % 1024
    return out


def _gather_conv(gathered, shape):
    L, K, c = shape
    return jnp.transpose(gathered, (1, 2, 0, 3)).reshape(L, K, NDEV * c)


def kernel(x, p, positions, ln_ffn1_g, ln_ffn1_b, ffn1_w_gate, ffn1_w_up, ffn1_w_down, w_in, ret_norm_g, lru_conv_w, lru_conv_b, lru_w_a, lru_b_a, lru_w_x, lru_b_x, lru_lambda, gdn_conv_w, gdn_a_log, gdn_dt_bias, gdn_norm_g, w_out, ln_mix_g, ln_mix_b, ffn2_w_gate, ffn2_w_up, ffn2_w_down, ple_w_gate, ple_w_proj, ln_ffn2_g, ln_ffn2_b, loss_target, m_ln_ffn1_g, m_ln_ffn1_b, m_ffn1_w_gate, m_ffn1_w_up, m_ffn1_w_down, m_w_in, m_ret_norm_g, m_lru_conv_w, m_lru_conv_b, m_lru_w_a, m_lru_b_a, m_lru_w_x, m_lru_b_x, m_lru_lambda, m_gdn_conv_w, m_gdn_a_log, m_gdn_dt_bias, m_gdn_norm_g, m_w_out, m_ln_mix_g, m_ln_mix_b, m_ffn2_w_gate, m_ffn2_w_up, m_ffn2_w_down, m_ple_w_gate, m_ple_w_proj, m_ln_ffn2_g, m_ln_ffn2_b, v_ln_ffn1_g, v_ln_ffn1_b, v_ffn1_w_gate, v_ffn1_w_up, v_ffn1_w_down, v_w_in, v_ret_norm_g, v_lru_conv_w, v_lru_conv_b, v_lru_w_a, v_lru_b_a, v_lru_w_x, v_lru_b_x, v_lru_lambda, v_gdn_conv_w, v_gdn_a_log, v_gdn_dt_bias, v_gdn_norm_g, v_w_out, v_ln_mix_g, v_ln_mix_b, v_ffn2_w_gate, v_ffn2_w_up, v_ffn2_w_down, v_ple_w_gate, v_ple_w_proj, v_ln_ffn2_g, v_ln_ffn2_b):
    args = locals()
    W = {n: args[n] for n in WEIGHTS}
    M = {n: args['m_' + n] for n in WEIGHTS}
    V = {n: args['v_' + n] for n in WEIGHTS}
    me = 4 * lax.axis_index("x") + 2 * lax.axis_index("y") + lax.axis_index("c")

    core = lax.axis_index("c")
    chip = 2 * lax.axis_index("x") + lax.axis_index("y")

    packed = _pack_big(W)

    def group(l, name):
        per = {k: packed[k].shape[0] // DEPTH for k in PACKS}
        if name == 'f1':
            return [packed['p384'][l * per['p384']:l * per['p384'] + 2], packed['pd'][l * per['pd']:l * per['pd'] + 1]]
        return [packed['p384'][l * per['p384'] + 2:(l + 1) * per['p384']],
                packed['pd'][l * per['pd'] + 1:(l + 1) * per['pd']]] + [
                    packed[k][l * per[k]:(l + 1) * per[k]] for k in PACKS[2:]]

    def as_weights(arrs):
        G = dict(zip(PACKS, arrs))
        if 'ppp' in G:
            G['wpp'] = jnp.transpose(G.pop('ppp'), (1, 2, 0, 3)).reshape(PLE, D)
        return G

    conv_pack = _pack_rows([W[n] for n in CONV_SHARDED])
    g0 = _gather_two_level(group(0, 'f1') + [conv_pack], "gather_weights")
    g0, rest0 = lax.optimization_barrier((g0, group(0, 'rest')))
    start0 = _split_start(rest0, [(NDEV,) + a.shape for a in rest0], _gather_plan, NDEV - 1, "gather_start_0")
    tok0, all1 = lax.optimization_barrier((start0[4], group(1, 'f1') + group(1, 'rest')))
    start1 = _split_start(all1, [(NDEV,) + a.shape for a in all1], _gather_plan, NDEV - 1, "gather_start_1")
    arrived = {}

    def gather_done(started, after, name):
        srcs, lands = _split_wait(started[0], started[1], started[2], started[3], after, _gather_plan, name)
        return [lax.dynamic_update_slice_in_dim(ld, s[None], me, axis=0) for s, ld in zip(srcs, lands)]

    def fetch(l, name, after):
        if l == 0 and name == 'f1':
            return as_weights(g0[:-1]), tok0[0, 0] + start1[4][0, 0]
        if l == 0:
            return as_weights(gather_done(start0, after, "gather_wait_0")), 0.0
        if name == 'f1':
            arrived[1] = gather_done(start1, after, "gather_wait_1")
            return as_weights(arrived[1][:2]), 0.0
        return as_weights(arrived[1][2:]), 0.0

    conv_all = g0[-1]
    sm = {n: W[n] for n in SMALL}
    conv_shards = [_unpack_rows(conv_all[s], [W[n].shape for n in CONV_SHARDED]) for s in range(NDEV)]
    for i, n in enumerate(CONV_SHARDED):
        sm[n] = _gather_conv(jnp.stack([cs[i] for cs in conv_shards]), W[n].shape)

    received, started = {}, {}

    small_shapes = [sm[n].shape for n in SMALL]
    small_started = []

    def emit(l, name, grads):
        if name == 'small':
            pack = _pack_rows([grads[n] for n in SMALL])
            small_started.extend(_split_start([pack], [(NDEV,) + pack.shape], _gather_plan, NDEV - 1, "small_start"))
            return 0.0
        keys = list(grads)
        arrs = [grads[k] for k in keys]
        gots = _scatter_pairs(arrs, "scatter_pairs")
        owns = [lax.dynamic_index_in_dim(a.reshape((4, 2) + a.shape[1:]), core, axis=1, keepdims=False) for a in arrs]
        pair = [_pair_sum(o, g, "pair_sum_" + k) for k, o, g in zip(keys, owns, gots)]
        started[l, name] = (keys, _split_start(pair, [a.shape for a in pair], _chips_plan, 3,
                                               f"scatter_start_{l}_{name}"))
        return started[l, name][1][4][0, 0]

    def scatter_done(l, name, after):
        keys, st = started[l, name]
        srcs, lands = _split_wait(st[0], st[1], st[2], st[3], after, _chips_plan, f"scatter_wait_{l}_{name}")
        received[l, name] = dict(zip(keys, [
            lax.dynamic_update_slice_in_dim(ld, lax.dynamic_index_in_dim(s, chip, axis=0), chip, axis=0)
            for s, ld in zip(srcs, lands)]))

    loss, grad_x = _local_step(x[0], p[:, 0], positions.reshape(-1, 1), loss_target[0], fetch, emit, sm)
    loss = lax.psum(loss[0, 0], ("x", "y", "c"))
    last = (0, 'f1')
    for l, name in started:
        if (l, name) != last:
            scatter_done(l, name, grad_x)

    anchor = started[last][1][4]
    srcs, lands = _split_wait(small_started[0], small_started[1], small_started[2], small_started[3], anchor,
                              _gather_plan, "small_wait")
    small_all = lax.dynamic_update_slice_in_dim(lands[0], srcs[0][None], me, axis=0)
    small_sum = _unpack_rows(_sum_sources(small_all), small_shapes)
    grads, delta, new_m, new_v = {}, {}, {}, {}
    for n, g in zip(SMALL, small_sum):
        if n in CONV_SHARDED:
            c = W[n].shape[2]
            g = lax.dynamic_slice_in_dim(g, me * c, c, axis=2)
        grads[n] = g

    big_parts = {
        'ffn1_w_gate': [('f1', 'p384', 0)], 'ffn1_w_up': [('f1', 'p384', 1)], 'ffn1_w_down': [('f1', 'pd', 0)],
        'ffn2_w_gate': [('rest', 'p384', 0)], 'ffn2_w_up': [('rest', 'p384', 1)], 'ffn2_w_down': [('rest', 'pd', 0)],
        'w_in': [('rest', 'pr', 0), ('rest', 'pinl', 0), ('rest', 'ping', 0)], 'w_out': [('rest', 'pr', 1)],
        'ple_w_gate': [('rest', 'pr', 2)], 'ple_w_proj': [('rest', 'ppp', 0)],
    }
    def adam(n):
        parts = [[(received[l, grp][k], slot) for grp, k, slot in big_parts[n]] for l in range(DEPTH)]
        grads[n], delta[n], new_m[n], new_v[n] = _adam_big(parts, W[n], M[n], V[n], anchor, "adam_" + n)

    shapes = [W[n].shape for n in SMALL]
    d_s, m_s, v_s = _adam_small(*[_pack_rows([src[n] for n in SMALL]) for src in (W, grads, M, V)])
    for n, dd, mm, vv in zip(SMALL, _unpack_rows(d_s, shapes), _unpack_rows(m_s, shapes), _unpack_rows(v_s, shapes)):
        delta[n], new_m[n], new_v[n] = dd, mm, vv
    waits_last = [n for n in BIG if big_parts[n][0][0] == last[1]]
    for n in BIG:
        if n not in waits_last:
            adam(n)
    done = jnp.stack([d_s[0, 0]] + [delta[n][0, 0, 0] for n in BIG if n not in waits_last])
    scatter_done(*last, done)
    for n in waits_last:
        adam(n)

    return (loss, grad_x[None], *[grads[n] for n in WEIGHTS], *[delta[n] for n in WEIGHTS],
            *[new_m[n] for n in WEIGHTS], *[new_v[n] for n in WEIGHTS])
```

```python
import functools
import math

import jax
import jax.numpy as jnp
from jax import lax
from jax.experimental import pallas as pl
from jax.experimental.pallas import tpu as pltpu

f32 = jnp.float32
bf16 = jnp.bfloat16

NDEV = 8
DEPTH = 2
D = 1024
FSP = 384
FB = 2
NF = NDEV // FB
PLE = 256
CH = 64
RET_H, GDN_H = 4, 6
RET_W, LRU_W, GDN_W = 256, 384, 384
GDN_IN = 1664
D_IN = 3340
ALPHA = 4.0 ** 0.25
LN_EPS = 1e-5
ROPE_THETA = 10000.0
TM = 512
RB_RET, RB_LRU, RB_GDN = 512, 512, 256
VMEM_LIMIT = 56 * 1024 * 1024
ADAM_LR, ADAM_B1, ADAM_B2, ADAM_EPS, ADAM_WD, ADAM_STEP = 0.001, 0.9, 0.999, 1e-08, 0.01, 10

WEIGHTS = ['ln_ffn1_g', 'ln_ffn1_b', 'ffn1_w_gate', 'ffn1_w_up', 'ffn1_w_down', 'w_in', 'ret_norm_g', 'lru_conv_w',
           'lru_conv_b', 'lru_w_a', 'lru_b_a', 'lru_w_x', 'lru_b_x', 'lru_lambda', 'gdn_conv_w', 'gdn_a_log',
           'gdn_dt_bias', 'gdn_norm_g', 'w_out', 'ln_mix_g', 'ln_mix_b', 'ffn2_w_gate', 'ffn2_w_up', 'ffn2_w_down',
           'ple_w_gate', 'ple_w_proj', 'ln_ffn2_g', 'ln_ffn2_b']
BIG = ['ffn1_w_gate', 'ffn1_w_up', 'ffn1_w_down', 'w_in', 'w_out', 'ffn2_w_gate', 'ffn2_w_up', 'ffn2_w_down',
       'ple_w_gate', 'ple_w_proj']
SMALL = [n for n in WEIGHTS if n not in BIG]
PACKS = ('p384', 'pd', 'pr', 'pinl', 'ping', 'ppp')
CONV_SHARDED = {'lru_conv_w': LRU_W, 'gdn_conv_w': 3 * GDN_W}


def _cparams(sem=None):
    return pltpu.CompilerParams(dimension_semantics=sem, vmem_limit_bytes=VMEM_LIMIT)


def _sigmoid(x):
    return 1.0 / (1.0 + jnp.exp(-x))


def _silu(x):
    return x * _sigmoid(x)


def _dsilu(x):
    s = _sigmoid(x)
    return s * (1.0 + x * (1.0 - s))


def _softplus(x):
    return jnp.maximum(x, 0.0) + jnp.log(1.0 + jnp.exp(-jnp.abs(x)))


def _gelu(x):
    return 0.5 * x * (1.0 + jnp.tanh(0.7978845608028654 * (x + 0.044715 * x * x * x)))


def _dot(a, b):
    return jnp.dot(a.astype(bf16), b.astype(bf16), preferred_element_type=f32)


def _dot_nt(a, b):
    return lax.dot_general(a.astype(bf16), b.astype(bf16), (((1,), (1,)), ((), ())), preferred_element_type=f32)


def _dot_tn(a, b):
    return lax.dot_general(a.astype(bf16), b.astype(bf16), (((0,), (0,)), ((), ())), preferred_element_type=f32)


def _bmm(eq, a, b):
    return jnp.einsum(eq, a.astype(bf16), b.astype(bf16), preferred_element_type=f32)


def _split3(a):
    a1 = a.astype(bf16)
    r = a - a1.astype(f32)
    a2 = r.astype(bf16)
    return a1, a2, (r - a2.astype(f32)).astype(bf16)


def _bmm3(eq, a, b):
    a1, a2, _ = _split3(a)
    b1, b2, _ = _split3(b)
    e = lambda x, y: jnp.einsum(eq, x, y, preferred_element_type=f32)
    return e(a1, b1) + (e(a1, b2) + e(a2, b1))


def _rowsum(x):
    ones = jnp.ones((x.shape[0], CH, CH), bf16)
    return jnp.einsum('bij,bjk->bik', x.astype(bf16), ones, preferred_element_type=f32)


def _tri_ones(B, upper=False):
    ii = lax.broadcasted_iota(jnp.int32, (B, CH, CH), 1)
    jj = lax.broadcasted_iota(jnp.int32, (B, CH, CH), 2)
    return jnp.where((ii <= jj) if upper else (ii >= jj), 1.0, 0.0).astype(bf16)


def _cumsum_mm(t, x):
    x1, x2, x3 = _split3(x)
    e = lambda y: jnp.einsum('bij,bjk->bik', t, y, preferred_element_type=f32)
    return e(x1) + (e(x2) + e(x3))


def _chunk_cumsum(x, reverse=False):
    n = x.shape[0] // CH
    return _cumsum_mm(_tri_ones(n, upper=reverse), x.reshape(n, CH, 128)).reshape(x.shape)


@jax.custom_vjp
def _neumann_inv(m):
    ii = lax.broadcasted_iota(jnp.int32, m.shape, 1)
    jj = lax.broadcasted_iota(jnp.int32, m.shape, 2)
    inv = jnp.where(ii == jj, 1.0, 0.0).astype(f32) + m
    mp = m
    for _ in range(5):
        mp = _bmm3('bij,bjk->bik', mp, mp)
        inv = inv + _bmm3('bij,bjk->bik', inv, mp)
    return inv


def _neumann_inv_fwd(m):
    inv = _neumann_inv(m)
    return inv, inv


def _neumann_inv_bwd(inv, g):
    return (_bmm('bij,bkj->bik', _bmm('bji,bjk->bik', inv, g), inv),)


_neumann_inv.defvjp(_neumann_inv_fwd, _neumann_inv_bwd)


@jax.custom_vjp
def _known_inv(m, inv):
    return inv


def _known_inv_fwd(m, inv):
    return inv, inv


def _known_inv_bwd(inv, g):
    return _neumann_inv_bwd(inv, g)[0], jnp.zeros_like(inv)


_known_inv.defvjp(_known_inv_fwd, _known_inv_bwd)


def _ln_stats(z):
    mu = jnp.mean(z, -1, keepdims=True)
    zc = z - mu
    rstd = lax.rsqrt(jnp.mean(zc * zc, -1, keepdims=True) + LN_EPS)
    return zc * rstd, rstd


def _ln_bwd(z, g, dout):
    xh, rstd = _ln_stats(z)
    dxh = dout * g
    dz = rstd * (dxh - jnp.mean(dxh, -1, keepdims=True) - xh * jnp.mean(dxh * xh, -1, keepdims=True))
    return dz, jnp.sum(dout * xh, 0, keepdims=True), jnp.sum(dout, 0, keepdims=True)


def _full_spec(shape):
    nd = len(shape)
    return pl.BlockSpec(shape, lambda *_: (0,) * nd)


def _ffn_fwd(x, p384, pd, lg, lb, slot, which, ple=None):
    T = x.shape[0]
    sg, su, sd = 2 * slot, 2 * slot + 1, slot
    has_ple = ple is not None

    def body(*refs):
        if has_ple:
            (x_ref, wg_ref, wu_ref, wd_ref, lg_ref, lb_ref, p_ref, wpg_ref, wpp_ref,
             z_ref, o_ref, g_ref, u_ref, a_ref, acc, xb_s) = refs
        else:
            x_ref, wg_ref, wu_ref, wd_ref, lg_ref, lb_ref, z_ref, o_ref, g_ref, u_ref, a_ref, acc, xb_s = refs
        f = pl.program_id(1)

        @pl.when(f == 0)
        def _():
            x = x_ref[...]
            xb = x.astype(bf16)
            xb_s[...] = xb
            base = ALPHA * x
            if has_ple:
                gate = _sigmoid(_dot(xb, wpg_ref[...].reshape(D, D)))
                base = base + gate * _dot(p_ref[...], wpp_ref[...])
            acc[...] = base

        xb = xb_s[...]
        g = _dot(xb, wg_ref[...])
        u = _dot(xb, wu_ref[...])
        g_ref[...] = g.astype(bf16)
        u_ref[...] = u.astype(bf16)
        a = (_silu(g) * u).astype(bf16)
        a_ref[...] = a
        acc[...] += 0.5 * _dot(a, wd_ref[...].reshape(FB * FSP, D))

        @pl.when(f == NF - 1)
        def _():
            z = acc[...]
            z_ref[...] = z
            o_ref[...] = _ln_stats(z)[0] * lg_ref[...] + lb_ref[...]

    row = pl.BlockSpec((TM, D), lambda i, f: (i, 0))
    in_specs = [row,
                pl.BlockSpec((None, D, FB * FSP), lambda i, f: (sg, 0, f)),
                pl.BlockSpec((None, D, FB * FSP), lambda i, f: (su, 0, f)),
                pl.BlockSpec((FB, None, FSP, D), lambda i, f: (f, sd, 0, 0)),
                _full_spec((1, D)), _full_spec((1, D))]
    args = [x, p384, p384, pd, lg, lb]
    if has_ple:
        p, pr, wpp = ple
        in_specs += [pl.BlockSpec((TM, PLE), lambda i, f: (i, 0)),
                     pl.BlockSpec((NDEV, None, 128, D), lambda i, f: (0, 2, 0, 0)),
                     _full_spec((PLE, D))]
        args += [p, pr, wpp]
    hid = pl.BlockSpec((TM, FB * FSP), lambda i, f: (i, f))
    hshape = jax.ShapeDtypeStruct((T, NDEV * FSP), bf16)
    return pl.pallas_call(
        body, grid=(T // TM, NF), in_specs=in_specs, out_specs=[row, row, hid, hid, hid],
        out_shape=[jax.ShapeDtypeStruct((T, D), f32)] * 2 + [hshape, hshape, hshape],
        scratch_shapes=[pltpu.VMEM((TM, D), f32), pltpu.VMEM((TM, D), bf16)],
        compiler_params=_cparams(("arbitrary", "arbitrary")), name=f"ffn{which + 1}_fwd")(*args)


def _ffn_bwd(z, dout, gs, us, p384, pd, lg, slot, which):
    T = z.shape[0]
    TMB = TM
    sg, su, sd = 2 * slot, 2 * slot + 1, slot

    def body(z_ref, do_ref, g_ref, u_ref, wg_ref, wu_ref, wd_ref, lg_ref,
             dx_ref, dg_ref, du_ref, dy_ref, dlg_ref, dlb_ref, acc, dyb):
        i, f = pl.program_id(0), pl.program_id(1)

        @pl.when(jnp.logical_and(i == 0, f == 0))
        def _():
            dlg_ref[...] = jnp.zeros_like(dlg_ref)
            dlb_ref[...] = jnp.zeros_like(dlb_ref)

        @pl.when(f == 0)
        def _():
            dz, dlg, dlb = _ln_bwd(z_ref[...], lg_ref[...], do_ref[...])
            dlg_ref[...] += dlg
            dlb_ref[...] += dlb
            dy = (0.5 * dz).astype(bf16)
            dyb[...] = dy
            dy_ref[...] = dy
            acc[...] = ALPHA * dz

        g = g_ref[...].astype(f32)
        u = u_ref[...].astype(f32)
        da = _dot_nt(dyb[...], wd_ref[...].reshape(FB * FSP, D))
        sgm = _sigmoid(g)
        dg = (da * u * (sgm * (1.0 + g * (1.0 - sgm)))).astype(bf16)
        du = (da * (g * sgm)).astype(bf16)
        dg_ref[...] = dg
        du_ref[...] = du
        acc[...] += _dot_nt(dg, wg_ref[...]) + _dot_nt(du, wu_ref[...])

        @pl.when(f == NF - 1)
        def _():
            dx_ref[...] = acc[...]

    row = pl.BlockSpec((TMB, D), lambda i, f: (i, 0))
    hid = pl.BlockSpec((TMB, FB * FSP), lambda i, f: (i, f))
    vec = _full_spec((1, D))
    in_specs = [row, row, hid, hid,
                pl.BlockSpec((None, D, FB * FSP), lambda i, f: (sg, 0, f)),
                pl.BlockSpec((None, D, FB * FSP), lambda i, f: (su, 0, f)),
                pl.BlockSpec((FB, None, FSP, D), lambda i, f: (f, sd, 0, 0)),
                vec]
    args = [z, dout, gs, us, p384, p384, pd, lg]
    out_specs = [row, hid, hid, row, vec, vec]
    hshape = jax.ShapeDtypeStruct((T, NDEV * FSP), bf16)
    out_shape = [jax.ShapeDtypeStruct((T, D), f32), hshape, hshape, jax.ShapeDtypeStruct((T, D), bf16),
                 jax.ShapeDtypeStruct((1, D), f32), jax.ShapeDtypeStruct((1, D), f32)]
    return pl.pallas_call(
        body, grid=(T // TMB, NF), in_specs=in_specs, out_specs=out_specs, out_shape=out_shape,
        scratch_shapes=[pltpu.VMEM((TMB, D), f32), pltpu.VMEM((TMB, D), bf16)],
        compiler_params=_cparams(("arbitrary", "arbitrary")), name=f"ffn{which + 1}_bwd")(*args)


def _ple_bwd(x, p, dy, dx_ffn, pr, wpp, layer):
    T = x.shape[0]

    def body(x_ref, p_ref, dy_ref, dxf_ref, wpg_ref, wpp_ref, dx_ref, dgp_ref, dpj_ref):
        dz = 2.0 * dy_ref[...].astype(f32)
        wpg = wpg_ref[...].reshape(D, D)
        gate = _sigmoid(_dot(x_ref[...], wpg))
        proj = _dot(p_ref[...], wpp_ref[...])
        dgp = (dz * proj * gate * (1.0 - gate)).astype(bf16)
        dgp_ref[...] = dgp
        dpj_ref[...] = (dz * gate).astype(bf16)
        dx_ref[...] = dxf_ref[...] + _dot_nt(dgp, wpg)

    row = pl.BlockSpec((TM, D), lambda i: (i, 0))
    return pl.pallas_call(
        body, grid=(T // TM,),
        in_specs=[row, pl.BlockSpec((TM, PLE), lambda i: (i, 0)), row, row,
                  pl.BlockSpec((NDEV, None, 128, D), lambda i: (0, 3 * layer + 2, 0, 0)), _full_spec((PLE, D))],
        out_specs=[row, row, row],
        out_shape=[jax.ShapeDtypeStruct((T, D), f32), jax.ShapeDtypeStruct((T, D), bf16),
                   jax.ShapeDtypeStruct((T, D), bf16)],
        compiler_params=_cparams(("arbitrary",)), name="ple_bwd")(x, p, dy, dx_ffn, pr, wpp)


def _matmul_tn(a, b, nb, name, nsub=1):
    T, M = a.shape
    N = b.shape[1]
    wide = nsub * nb
    tk = min(T, 1024 if wide <= 2048 else 512)
    nk = T // tk

    def body(a_ref, b_ref, o_ref, acc):
        k = pl.program_id(1)

        @pl.when(k == 0)
        def _():
            acc[...] = jnp.zeros_like(acc)

        acc[...] += _dot_tn(a_ref[...], b_ref[...])

        @pl.when(k == nk - 1)
        def _():
            for j in range(nsub):
                o_ref[j] = acc[:, j * nb:(j + 1) * nb].astype(bf16)

    return pl.pallas_call(
        body, grid=(N // wide, nk),
        in_specs=[pl.BlockSpec((tk, M), lambda n, k: (k, 0)), pl.BlockSpec((tk, wide), lambda n, k: (k, n))],
        out_specs=pl.BlockSpec((nsub, M, nb), lambda n, k: (n, 0, 0)),
        out_shape=jax.ShapeDtypeStruct((N // nb, M, nb), bf16),
        scratch_shapes=[pltpu.VMEM((M, wide), f32)],
        compiler_params=_cparams(("arbitrary", "arbitrary")), name=name)(a, b)


def _proj_in(x, pr, pinl, ping, layer):
    T = x.shape[0]

    def body(x_ref, wr_ref, wl_ref, wg_ref, hr_ref, hl_ref, hg_ref):
        xb = x_ref[...].astype(bf16)
        hr_ref[...] = _dot(xb, wr_ref[...].reshape(D, D))
        hl_ref[...] = _dot(xb, wl_ref[...].reshape(D, 2 * LRU_W))
        hg_ref[...] = _dot(xb, wg_ref[...].reshape(D, GDN_IN))

    return pl.pallas_call(
        body, grid=(T // TM,),
        in_specs=[pl.BlockSpec((TM, D), lambda i: (i, 0)),
                  pl.BlockSpec((NDEV, None, 128, D), lambda i: (0, 3 * layer, 0, 0)),
                  pl.BlockSpec((NDEV, None, 128, 2 * LRU_W), lambda i: (0, layer, 0, 0)),
                  pl.BlockSpec((NDEV, None, 128, GDN_IN), lambda i: (0, layer, 0, 0))],
        out_specs=[pl.BlockSpec((TM, D), lambda i: (i, 0)), pl.BlockSpec((TM, 2 * LRU_W), lambda i: (i, 0)),
                   pl.BlockSpec((TM, GDN_IN), lambda i: (i, 0))],
        out_shape=[jax.ShapeDtypeStruct((T, D), f32), jax.ShapeDtypeStruct((T, 2 * LRU_W), f32),
                   jax.ShapeDtypeStruct((T, GDN_IN), f32)],
        compiler_params=_cparams(("arbitrary",)), name="proj_in")(x, pr, pinl, ping)


def _proj_in_bwd(base, dhr, dhl, dhg, pr, pinl, ping, layer):
    T = base.shape[0]

    def body(b_ref, dr_ref, dl_ref, dg_ref, wr_ref, wl_ref, wg_ref, o_ref):
        o_ref[...] = (b_ref[...] + _dot_nt(dr_ref[...], wr_ref[...].reshape(D, D))
                      + _dot_nt(dl_ref[...], wl_ref[...].reshape(D, 2 * LRU_W))
                      + _dot_nt(dg_ref[...], wg_ref[...].reshape(D, GDN_IN)))

    return pl.pallas_call(
        body, grid=(T // TM,),
        in_specs=[pl.BlockSpec((TM, D), lambda i: (i, 0)), pl.BlockSpec((TM, D), lambda i: (i, 0)),
                  pl.BlockSpec((TM, 2 * LRU_W), lambda i: (i, 0)), pl.BlockSpec((TM, GDN_IN), lambda i: (i, 0)),
                  pl.BlockSpec((NDEV, None, 128, D), lambda i: (0, 3 * layer, 0, 0)),
                  pl.BlockSpec((NDEV, None, 128, 2 * LRU_W), lambda i: (0, layer, 0, 0)),
                  pl.BlockSpec((NDEV, None, 128, GDN_IN), lambda i: (0, layer, 0, 0))],
        out_specs=pl.BlockSpec((TM, D), lambda i: (i, 0)),
        out_shape=jax.ShapeDtypeStruct((T, D), f32),
        compiler_params=_cparams(("arbitrary",)), name="proj_in_bwd")(base, dhr, dhl, dhg, pr, pinl, ping)


def _mix_out(x1, o_r, o_l, o_g, pr, lg, lb, layer):
    T = x1.shape[0]

    def body(x_ref, r_ref, l_ref, g_ref, w_ref, lg_ref, lb_ref, z_ref, o_ref):
        w = w_ref[...].reshape(D, D)
        z = (ALPHA * x_ref[...] + _dot(r_ref[...], w[0:RET_W]) + _dot(l_ref[...], w[RET_W:RET_W + LRU_W])
             + _dot(g_ref[...], w[RET_W + LRU_W:D]))
        z_ref[...] = z
        o_ref[...] = _ln_stats(z)[0] * lg_ref[...] + lb_ref[...]

    row = pl.BlockSpec((TM, D), lambda i: (i, 0))
    return pl.pallas_call(
        body, grid=(T // TM,),
        in_specs=[row, pl.BlockSpec((TM, RET_W), lambda i: (i, 0)), pl.BlockSpec((TM, LRU_W), lambda i: (i, 0)),
                  pl.BlockSpec((TM, GDN_W), lambda i: (i, 0)),
                  pl.BlockSpec((NDEV, None, 128, D), lambda i: (0, 3 * layer + 1, 0, 0)),
                  _full_spec((1, D)), _full_spec((1, D))],
        out_specs=[row, row], out_shape=[jax.ShapeDtypeStruct((T, D), f32)] * 2,
        compiler_params=_cparams(("arbitrary",)), name="mix_out")(x1, o_r, o_l, o_g, pr, lg, lb)


def _mix_out_bwd(z, dout, pr, lg, layer):
    T = z.shape[0]

    def body(z_ref, do_ref, w_ref, lg_ref, dxb_ref, dzb_ref, dr_ref, dl_ref, dg_ref, dlg_ref, dlb_ref):
        @pl.when(pl.program_id(0) == 0)
        def _():
            dlg_ref[...] = jnp.zeros_like(dlg_ref)
            dlb_ref[...] = jnp.zeros_like(dlb_ref)

        dz, dlg, dlb = _ln_bwd(z_ref[...], lg_ref[...], do_ref[...])
        dlg_ref[...] += dlg
        dlb_ref[...] += dlb
        dxb_ref[...] = ALPHA * dz
        dzb = dz.astype(bf16)
        dzb_ref[...] = dzb
        w = w_ref[...].reshape(D, D)
        dr_ref[...] = _dot_nt(dzb, w[0:RET_W])
        dl_ref[...] = _dot_nt(dzb, w[RET_W:RET_W + LRU_W])
        dg_ref[...] = _dot_nt(dzb, w[RET_W + LRU_W:D])

    row = pl.BlockSpec((TM, D), lambda i: (i, 0))
    vec = _full_spec((1, D))
    return pl.pallas_call(
        body, grid=(T // TM,),
        in_specs=[row, row, pl.BlockSpec((NDEV, None, 128, D), lambda i: (0, 3 * layer + 1, 0, 0)), vec],
        out_specs=[row, row, pl.BlockSpec((TM, RET_W), lambda i: (i, 0)), pl.BlockSpec((TM, LRU_W), lambda i: (i, 0)),
                   pl.BlockSpec((TM, GDN_W), lambda i: (i, 0)), vec, vec],
        out_shape=[jax.ShapeDtypeStruct((T, D), f32), jax.ShapeDtypeStruct((T, D), bf16),
                   jax.ShapeDtypeStruct((T, RET_W), f32), jax.ShapeDtypeStruct((T, LRU_W), f32),
                   jax.ShapeDtypeStruct((T, GDN_W), f32), jax.ShapeDtypeStruct((1, D), f32),
                   jax.ShapeDtypeStruct((1, D), f32)],
        compiler_params=_cparams(("arbitrary",)), name="mix_out_bwd")(z, dout, pr, lg)


def _loss_grad(y, target):
    T = y.shape[0]

    def body(y_ref, t_ref, dy_ref, l_ref):
        @pl.when(pl.program_id(0) == 0)
        def _():
            l_ref[...] = jnp.zeros_like(l_ref)

        e = y_ref[...] - t_ref[...]
        dy_ref[...] = e * (1.0 / D)
        l_ref[...] += 0.5 * jnp.sum(jnp.sum(e * e, -1, keepdims=True) * (1.0 / D), 0, keepdims=True)

    row = pl.BlockSpec((TM, D), lambda i: (i, 0))
    return pl.pallas_call(
        body, grid=(T // TM,), in_specs=[row, row], out_specs=[row, _full_spec((1, 1))],
        out_shape=[jax.ShapeDtypeStruct((T, D), f32), jax.ShapeDtypeStruct((1, 1), f32)],
        compiler_params=_cparams(("arbitrary",)), name="loss_grad")(y, target)


def _split_heads(x, H):
    n = x.shape[0] // CH
    parts = [x[:, h * CH:(h + 1) * CH].reshape(n, CH, CH) for h in range(H)]
    return jnp.stack(parts, axis=1).reshape(n * H, CH, CH)


def _merge_heads(ref, x, H, col0=0):
    n = x.shape[0] // H
    x4 = x.reshape(n, H, CH, CH)
    for h in range(H):
        ref[:, col0 + h * CH:col0 + (h + 1) * CH] = x4[:, h].reshape(n * CH, CH)


def _rows_down(x, before, s):
    r8 = lax.broadcasted_iota(jnp.int32, before.shape, 0)
    top = jnp.where(r8 < s, pltpu.roll(before, s, 0), pltpu.roll(x[0:8], s, 0))
    return jnp.concatenate([top, pltpu.roll(x, s, 0)[8:]], axis=0)


def _rows_up(x, after, s):
    R = x.shape[0]
    r8 = lax.broadcasted_iota(jnp.int32, after.shape, 0)
    bottom = jnp.where(r8 >= 8 - s, pltpu.roll(after, 8 - s, 0), pltpu.roll(x[R - 8:R], 8 - s, 0))
    return jnp.concatenate([pltpu.roll(x, R - s, 0)[0:R - 8], bottom], axis=0)


def _conv_fwd(ext, x, tail, w, R):
    ext[0:8, :] = tail
    ext[8:R + 8, :] = x
    y = w[3:4, :] * x
    for k in range(3):
        y = y + w[k:k + 1, :] * _rows_down(x, tail, 3 - k)
    return y


def _conv_bwd(ext, dy, dy_next, w, R):
    x, tail = ext[8:8 + R, :], ext[0:8, :]
    dx = w[3:4, :] * dy
    dws = []
    for k in range(3):
        dx = dx + w[k:k + 1, :] * _rows_up(dy, dy_next, 3 - k)
        dws.append(jnp.sum(dy * _rows_down(x, tail, 3 - k), 0, keepdims=True))
    dws.append(jnp.sum(dy * x, 0, keepdims=True))
    return dx, jnp.concatenate(dws, axis=0)


def _prev_tail_spec(R, W):
    return pl.BlockSpec((8, W), lambda i: (jnp.maximum(i * (R // 8) - 1, 0), 0))


def _prev_tail_spec_rev(R, W, nb):
    return pl.BlockSpec((8, W), lambda i: (jnp.maximum((nb - 1 - i) * (R // 8) - 1, 0), 0))


def _rope_tables(positions):
    T = positions.shape[0]

    def body(p_ref, c_ref, s_ref):
        lane = lax.broadcasted_iota(jnp.int32, (TM, RET_W), 1)
        fi = (lane % 32).astype(f32)
        inv = jnp.exp(fi * (-math.log(ROPE_THETA) / 32.0))
        ang = p_ref[...].astype(f32) * inv
        c_ref[...] = jnp.cos(ang)
        s_ref[...] = jnp.where(lane % CH < 32, -jnp.sin(ang), jnp.sin(ang))

    row = pl.BlockSpec((TM, RET_W), lambda i: (i, 0))
    return pl.pallas_call(
        body, grid=(T // TM,), in_specs=[pl.BlockSpec((TM, 1), lambda i: (i, 0))], out_specs=[row, row],
        out_shape=[jax.ShapeDtypeStruct((T, RET_W), f32)] * 2,
        compiler_params=_cparams(("arbitrary",)), name="rope_tables")(positions)


def _partner(x):
    lane = lax.broadcasted_iota(jnp.int32, x.shape, 1)
    return jnp.where(lane % CH < 32, pltpu.roll(x, RET_W - 32, 1), pltpu.roll(x, 32, 1))


def _ret_consts():
    ii = lax.broadcasted_iota(jnp.int32, (CH, CH), 0).astype(f32)
    jj = lax.broadcasted_iota(jnp.int32, (CH, CH), 1).astype(f32)
    intra, cross, tail, cd = [], [], [], []
    for h in range(RET_H):
        lg = math.log1p(-(2.0 ** (-5.0 - h)))
        intra.append(jnp.exp(jnp.abs(ii - jj) * lg))
        cross.append(jnp.exp((ii + 1.0) * lg))
        tail.append(jnp.exp((CH - 1.0 - ii) * lg))
        cd.append(jnp.full((CH, CH), math.exp(CH * lg), f32))
    return jnp.stack(intra), jnp.stack(cross), jnp.stack(tail), jnp.stack(cd)


def _ret_local(consts, q, k, v):
    n = q.shape[0] // RET_H
    intra, cross, tail = (jnp.tile(c, (n, 1, 1)) for c in consts[:3])
    s = _bmm('bid,bjd->bij', q, k) * intra
    return _bmm('bij,bje->bie', s, v), _bmm('bjd,bje->bde', k * tail, v), q * cross


def _ret_step(cd, st, o_intra, kv, qc):
    o = o_intra + _bmm('hid,hde->hie', qc, st)
    oc = o - jnp.mean(o, -1, keepdims=True)
    return oc * lax.rsqrt(jnp.mean(oc * oc, -1, keepdims=True) + 1e-5), st * cd + kv


def _ret_fwd(hr, cosw, sinw, gam):
    T = hr.shape[0]
    R = RB_RET
    nc = R // CH

    def body(h_ref, c_ref, s_ref, g_ref, o_ref, st_ref, st, wide):
        @pl.when(pl.program_id(0) == 0)
        def _():
            st[...] = jnp.zeros_like(st)

        consts = _ret_consts()
        cw, sw = c_ref[...], s_ref[...]
        q, k = h_ref[:, 0:RET_W], h_ref[:, RET_W:2 * RET_W]
        qh = _split_heads((q * cw + _partner(q) * sw) * 0.125, RET_H)
        kh = _split_heads(k * cw + _partner(k) * sw, RET_H)
        vh = _split_heads(h_ref[:, 2 * RET_W:3 * RET_W], RET_H)
        loc = _ret_local(consts, qh, kh, vh)
        outs = []
        s_cur = st[...]
        for c in range(nc):
            sl = slice(c * RET_H, (c + 1) * RET_H)
            st_ref[c] = s_cur
            on, s_cur = _ret_step(consts[3], s_cur, *(t[sl] for t in loc))
            outs.append(on)
        st[...] = s_cur
        _merge_heads(wide, jnp.concatenate(outs, axis=0), RET_H)
        o_ref[...] = wide[...] * g_ref[...] * _silu(h_ref[:, 3 * RET_W:4 * RET_W])

    blk = pl.BlockSpec((R, RET_W), lambda i: (i, 0))
    return pl.pallas_call(
        body, grid=(T // R,),
        in_specs=[pl.BlockSpec((R, D), lambda i: (i, 0)), blk, blk, _full_spec((1, RET_W))],
        out_specs=[blk, pl.BlockSpec((nc, RET_H, CH, CH), lambda i: (i, 0, 0, 0))],
        out_shape=[jax.ShapeDtypeStruct((T, RET_W), f32), jax.ShapeDtypeStruct((T // CH, RET_H, CH, CH), f32)],
        scratch_shapes=[pltpu.VMEM((RET_H, CH, CH), f32), pltpu.VMEM((R, RET_W), f32)],
        compiler_params=_cparams(("arbitrary",)), name="ret_fwd")(hr, cosw, sinw, gam)


def _ret_bwd(hr, cosw, sinw, gam, states, dout):
    T = hr.shape[0]
    R = RB_RET
    nc = R // CH
    nb = T // R

    def body(h_ref, c_ref, s_ref, g_ref, st_ref, do_ref, dh_ref, dgam_ref, dst, wide):
        @pl.when(pl.program_id(0) == 0)
        def _():
            dst[...] = jnp.zeros_like(dst)
            dgam_ref[...] = jnp.zeros_like(dgam_ref)

        consts = _ret_consts()
        cw, sw = c_ref[...], s_ref[...]
        q, k = h_ref[:, 0:RET_W], h_ref[:, RET_W:2 * RET_W]
        gr = h_ref[:, 3 * RET_W:4 * RET_W]
        qh = _split_heads((q * cw + _partner(q) * sw) * 0.125, RET_H)
        kh = _split_heads(k * cw + _partner(k) * sw, RET_H)
        vh = _split_heads(h_ref[:, 2 * RET_W:3 * RET_W], RET_H)
        do = do_ref[...]
        gam = g_ref[...]
        sg = _silu(gr)
        don = _split_heads(do * gam * sg, RET_H)
        loc, vjp_loc = jax.vjp(functools.partial(_ret_local, consts), qh, kh, vh)
        ons, dloc = [None] * nc, [[None] * nc for _ in range(3)]
        ds = dst[...]
        for c in reversed(range(nc)):
            sl = slice(c * RET_H, (c + 1) * RET_H)
            (on, _), vjp = jax.vjp(functools.partial(_ret_step, consts[3]), st_ref[c], *(t[sl] for t in loc))
            ds, dloc[0][c], dloc[1][c], dloc[2][c] = vjp((don[sl], ds))
            ons[c] = on
        dst[...] = ds
        dqs, dks, dvs = ([g] for g in vjp_loc(tuple(jnp.concatenate(d, axis=0) for d in dloc)))
        _merge_heads(wide, jnp.concatenate(ons, axis=0), RET_H)
        onw = wide[...]
        dgam_ref[...] += jnp.sum(do * onw * sg, 0, keepdims=True)
        dh_ref[:, 3 * RET_W:4 * RET_W] = (do * onw * gam * _dsilu(gr)).astype(bf16)
        _merge_heads(wide, jnp.concatenate(dqs, axis=0), RET_H)
        u = wide[...] * 0.125
        dh_ref[:, 0:RET_W] = (u * cw + _partner(u * sw)).astype(bf16)
        _merge_heads(wide, jnp.concatenate(dks, axis=0), RET_H)
        u = wide[...]
        dh_ref[:, RET_W:2 * RET_W] = (u * cw + _partner(u * sw)).astype(bf16)
        _merge_heads(wide, jnp.concatenate(dvs, axis=0), RET_H)
        dh_ref[:, 2 * RET_W:3 * RET_W] = wide[...].astype(bf16)

    blk = pl.BlockSpec((R, RET_W), lambda i: (nb - 1 - i, 0))
    return pl.pallas_call(
        body, grid=(nb,),
        in_specs=[pl.BlockSpec((R, D), lambda i: (nb - 1 - i, 0)), blk, blk, _full_spec((1, RET_W)),
                  pl.BlockSpec((nc, RET_H, CH, CH), lambda i: (nb - 1 - i, 0, 0, 0)), blk],
        out_specs=[pl.BlockSpec((R, D), lambda i: (nb - 1 - i, 0)), _full_spec((1, RET_W))],
        out_shape=[jax.ShapeDtypeStruct((T, D), bf16), jax.ShapeDtypeStruct((1, RET_W), f32)],
        scratch_shapes=[pltpu.VMEM((RET_H, CH, CH), f32), pltpu.VMEM((R, RET_W), f32)],
        compiler_params=_cparams(("arbitrary",)), name="ret_bwd")(hr, cosw, sinw, gam, states, dout)


def _lru_ab(xc, wa, ba, wx, bx, lam):
    r = _sigmoid(_dot(xc, wa) + ba)
    i = _sigmoid(_dot(xc, wx) + bx)
    la = 8.0 * r * (-_softplus(-lam))
    a = jnp.exp(la)
    em = jnp.tanh(la) * (jnp.exp(2.0 * la) + 1.0)
    return a, jnp.sqrt(-em) * (i * xc)


def _lru_out(h, gate):
    return h * _gelu(gate)


def _scan_fwd(a, b):
    R = a.shape[0]
    row = lax.broadcasted_iota(jnp.int32, a.shape, 0)
    d = 1
    while d < R:
        m = row >= d
        b = jnp.where(m, a * pltpu.roll(b, d, 0) + b, b)
        a = jnp.where(m, a * pltpu.roll(a, d, 0), a)
        d *= 2
    return a, b


def _scan_bwd(a, b):
    R = a.shape[0]
    row = lax.broadcasted_iota(jnp.int32, a.shape, 0)
    d = 1
    while d < R:
        m = row < R - d
        b = jnp.where(m, a * pltpu.roll(b, R - d, 0) + b, b)
        a = jnp.where(m, a * pltpu.roll(a, R - d, 0), a)
        d *= 2
    return b


def _lru_fwd(hl, cw, cb, wa, ba, wx, bx, lam):
    T = hl.shape[0]
    R = RB_LRU
    W = LRU_W

    def body(h_ref, t_ref, cw_ref, cb_ref, wa_ref, ba_ref, wx_ref, bx_ref, lam_ref, o_ref, hs_ref, carry, ext):
        first = pl.program_id(0) == 0

        @pl.when(first)
        def _():
            carry[...] = jnp.zeros_like(carry)

        tail = jnp.where(first, 0.0, t_ref[:, 0:W])
        xc = _conv_fwd(ext, h_ref[:, 0:W], tail, cw_ref[...], R) + cb_ref[...]
        a, b = _lru_ab(xc, wa_ref[...], ba_ref[...], wx_ref[...], bx_ref[...], lam_ref[...])
        ap, hloc = _scan_fwd(a, b)
        h = hloc + ap * carry[0:1, :]
        carry[...] = jnp.broadcast_to(h[R - 1:R, :], carry.shape)
        hs_ref[...] = h
        o_ref[...] = _lru_out(h, h_ref[:, W:2 * W])

    vec = _full_spec((1, W))
    blk = pl.BlockSpec((R, W), lambda i: (i, 0))
    return pl.pallas_call(
        body, grid=(T // R,),
        in_specs=[pl.BlockSpec((R, 2 * W), lambda i: (i, 0)), _prev_tail_spec(R, 2 * W), _full_spec((4, W)), vec,
                  _full_spec((W, W)), vec, _full_spec((W, W)), vec, vec],
        out_specs=[blk, blk], out_shape=[jax.ShapeDtypeStruct((T, W), f32)] * 2,
        scratch_shapes=[pltpu.VMEM((8, W), f32), pltpu.VMEM((R + 8, W), f32)],
        compiler_params=_cparams(("arbitrary",)), name="lru_fwd")(hl, hl, cw, cb, wa, ba, wx, bx, lam)


def _lru_bwd(hl, hs, cw, cb, wa, ba, wx, bx, lam, dout):
    T = hl.shape[0]
    R = RB_LRU
    W = LRU_W
    nb = T // R

    def body(h_ref, t_ref, hs_ref, hst_ref, cw_ref, cb_ref, wa_ref, ba_ref, wx_ref, bx_ref, lam_ref, do_ref,
             dh_ref, dcw_ref, dcb_ref, dwa_ref, dba_ref, dwx_ref, dbx_ref, dlam_ref, carry_g, carry_dy, ext):
        i = pl.program_id(0)
        last_blk = i == 0
        first_blk = i == nb - 1

        @pl.when(last_blk)
        def _():
            carry_g[...] = jnp.zeros_like(carry_g)
            carry_dy[...] = jnp.zeros_like(carry_dy)
            for r in (dcw_ref, dcb_ref, dwa_ref, dba_ref, dwx_ref, dbx_ref, dlam_ref):
                r[...] = jnp.zeros_like(r)

        tail = jnp.where(first_blk, 0.0, t_ref[:, 0:W])
        xc = _conv_fwd(ext, h_ref[:, 0:W], tail, cw_ref[...], R) + cb_ref[...]
        (a, _), vjp_ab = jax.vjp(_lru_ab, xc, wa_ref[...], ba_ref[...], wx_ref[...], bx_ref[...], lam_ref[...])
        hs = hs_ref[...]
        _, vjp_out = jax.vjp(_lru_out, hs, h_ref[:, W:2 * W])
        dh, dgate = vjp_out(do_ref[...])
        row = lax.broadcasted_iota(jnp.int32, (R, W), 0)
        dh = jnp.where(row == R - 1, dh + carry_g[0:1, :], dh)
        a_up = jnp.where(row == R - 1, 0.0, pltpu.roll(a, R - 1, 0))
        g = _scan_bwd(a_up, dh)
        carry_g[...] = jnp.broadcast_to(a[0:1, :] * g[0:1, :], carry_g.shape)
        hprev0 = jnp.where(first_blk, 0.0, hst_ref[7:8, :])
        hprev = jnp.where(row == 0, hprev0, pltpu.roll(hs, 1, 0))
        dxc, dwa, dba, dwx, dbx, dlam = vjp_ab((g * hprev, g))
        dwa_ref[...] += dwa
        dba_ref[...] += dba
        dwx_ref[...] += dwx
        dbx_ref[...] += dbx
        dlam_ref[...] += dlam
        dcb_ref[...] += jnp.sum(dxc, 0, keepdims=True)
        dx, dcw = _conv_bwd(ext, dxc, carry_dy[...], cw_ref[...], R)
        carry_dy[...] = dxc[0:8, :]
        dcw_ref[...] += dcw
        dh_ref[:, 0:W] = dx.astype(bf16)
        dh_ref[:, W:2 * W] = dgate.astype(bf16)

    vec = _full_spec((1, W))
    mat = _full_spec((W, W))
    blk = pl.BlockSpec((R, W), lambda i: (nb - 1 - i, 0))
    blk2 = pl.BlockSpec((R, 2 * W), lambda i: (nb - 1 - i, 0))
    return pl.pallas_call(
        body, grid=(nb,),
        in_specs=[blk2, _prev_tail_spec_rev(R, 2 * W, nb), blk, _prev_tail_spec_rev(R, W, nb), _full_spec((4, W)), vec,
                  mat, vec, mat, vec, vec, blk],
        out_specs=[blk2, _full_spec((4, W)), vec, mat, vec, mat, vec, vec],
        out_shape=[jax.ShapeDtypeStruct((T, 2 * W), bf16), jax.ShapeDtypeStruct((4, W), f32),
                   jax.ShapeDtypeStruct((1, W), f32), jax.ShapeDtypeStruct((W, W), f32),
                   jax.ShapeDtypeStruct((1, W), f32), jax.ShapeDtypeStruct((W, W), f32),
                   jax.ShapeDtypeStruct((1, W), f32), jax.ShapeDtypeStruct((1, W), f32)],
        scratch_shapes=[pltpu.VMEM((8, W), f32), pltpu.VMEM((8, W), f32), pltpu.VMEM((R + 8, W), f32)],
        compiler_params=_cparams(("arbitrary",)), name="lru_bwd")(hl, hl, hs, hs, cw, cb, wa, ba, wx, bx, lam, dout)


def _head_ones():
    i = lax.broadcasted_iota(jnp.int32, (GDN_W, GDN_W), 0)
    j = lax.broadcasted_iota(jnp.int32, (GDN_W, GDN_W), 1)
    return jnp.where(jnp.bitwise_xor(i, j) < CH, 1.0, 0.0).astype(bf16)


def _head_sums(x, ones):
    return jnp.dot(x.astype(bf16), ones, preferred_element_type=f32)


def _l2n(y, ones):
    r = lax.rsqrt(_head_sums(y * y, ones) + 1e-6)
    return y * r, r


def _l2n_bwd(dn, n, r, ones):
    return r * (dn - n * _head_sums(dn * n, ones))


def _gdn_local(inverse, q, k, vs, gc, bb):
    B = q.shape[0]
    ii = lax.broadcasted_iota(jnp.int32, (B, CH, CH), 1)
    jj = lax.broadcasted_iota(jnp.int32, (B, CH, CH), 2)
    gct = jnp.swapaxes(gc, 1, 2)
    decay = jnp.where(ii >= jj, jnp.exp(jnp.minimum(gc - gct, 0.0)), 0.0)
    kk = _bmm('bid,bjd->bij', k, k)
    inv = inverse(-jnp.where(ii > jj, bb * kk * decay, 0.0))
    egc = jnp.exp(gc)
    u = _bmm('bij,bje->bie', inv, vs * bb)
    w = _bmm('bij,bje->bie', inv, k * (bb * egc))
    qk = _bmm('bid,bjd->bij', q, k) * (0.125 * decay)
    glast = gc[:, CH - 1:CH, :]
    return u, w, qk, q * (0.125 * egc), k * jnp.exp(glast - gc), jnp.exp(jnp.broadcast_to(glast, gc.shape))


def _gdn_step(st, u, w, qk, qd, kt, egl, z, gn):
    vnew = u - _bmm('hcd,hde->hce', w, st)
    o = _bmm('hcd,hde->hce', qd, st) + _bmm('hij,hje->hie', qk, vnew)
    st2 = st * egl + _bmm('hcd,hce->hde', kt, vnew)
    out = o * lax.rsqrt(_rowsum(o * o) * (1.0 / CH) + 1e-6) * gn * _silu(z)
    return out, st2


def _gdn_scalars(ab, alog, dtb):
    sp = _softplus(ab + dtb)
    return -jnp.exp(alog) * sp, _sigmoid(ab)


def _bcast_heads(blk, lane0, H):
    R = blk.shape[0]
    n = R // CH
    parts = [jnp.broadcast_to(blk[:, lane0 + h:lane0 + h + 1], (R, CH)).reshape(n, CH, CH) for h in range(H)]
    return jnp.stack(parts, axis=1).reshape(n * H, CH, CH)


def _unbcast_heads(x, lane0, H):
    n = x.shape[0] // H
    R = n * CH
    s = jnp.sum(x, axis=2, keepdims=True).reshape(n, H, CH, 1)
    lane = lax.broadcasted_iota(jnp.int32, (R, 128), 1)
    acc = jnp.zeros((R, 128), f32)
    for h in range(H):
        acc = acc + jnp.where(lane == lane0 + h, jnp.broadcast_to(s[:, h].reshape(R, 1), (R, 128)), 0.0)
    return acc


def _gdn_fwd(hg, cw, alog, dtb, gn):
    T = hg.shape[0]
    R = RB_GDN
    nc = R // CH
    W3 = 3 * GDN_W
    H = GDN_H

    def body(h_ref, t_ref, cw_ref, al_ref, dt_ref, gn_ref, o_ref, st_ref, inv_ref, st, ext):
        first = pl.program_id(0) == 0

        @pl.when(first)
        def _():
            st[...] = jnp.zeros_like(st)

        def inverse(m):
            inv = _neumann_inv(m)
            inv_ref[...] = inv
            return inv

        tail = jnp.where(first, 0.0, t_ref[:, 0:W3])
        y = _silu(_conv_fwd(ext, h_ref[:, 0:W3], tail, cw_ref[...], R))
        ones = _head_ones()
        qs = _split_heads(_l2n(y[:, 0:GDN_W], ones)[0], H)
        ks = _split_heads(_l2n(y[:, GDN_W:2 * GDN_W], ones)[0], H)
        vs = _split_heads(y[:, 2 * GDN_W:W3], H)
        zh = _split_heads(h_ref[:, W3:W3 + GDN_W], H)
        g, beta = _gdn_scalars(h_ref[:, W3 + GDN_W:GDN_IN], al_ref[...], dt_ref[...])
        loc = _gdn_local(inverse, qs, ks, vs, _bcast_heads(_chunk_cumsum(g), 0, H), _bcast_heads(beta, H, H))
        gnv = gn_ref[...]
        outs = []
        s_cur = st[...]
        for c in range(nc):
            sl = slice(c * H, (c + 1) * H)
            st_ref[c] = s_cur
            out, s_cur = _gdn_step(s_cur, *(t[sl] for t in loc), zh[sl], gnv)
            outs.append(out)
        st[...] = s_cur
        _merge_heads(o_ref, jnp.concatenate(outs, axis=0), H)

    return pl.pallas_call(
        body, grid=(T // R,),
        in_specs=[pl.BlockSpec((R, GDN_IN), lambda i: (i, 0)), _prev_tail_spec(R, GDN_IN), _full_spec((4, W3)),
                  _full_spec((1, 128)), _full_spec((1, 128)), _full_spec((1, CH))],
        out_specs=[pl.BlockSpec((R, GDN_W), lambda i: (i, 0)), pl.BlockSpec((nc, H, CH, CH), lambda i: (i, 0, 0, 0)),
                   pl.BlockSpec((nc * H, CH, CH), lambda i: (i, 0, 0))],
        out_shape=[jax.ShapeDtypeStruct((T, GDN_W), f32), jax.ShapeDtypeStruct((T // CH, H, CH, CH), f32),
                   jax.ShapeDtypeStruct((T // CH * H, CH, CH), f32)],
        scratch_shapes=[pltpu.VMEM((H, CH, CH), f32), pltpu.VMEM((R + 8, W3), f32)],
        compiler_params=_cparams(("arbitrary",)), name="gdn_fwd")(hg, hg, cw, alog, dtb, gn)


def _gdn_bwd(hg, cw, alog, dtb, gn, states, invs, dout):
    T = hg.shape[0]
    R = RB_GDN
    nc = R // CH
    nb = T // R
    W3 = 3 * GDN_W
    H = GDN_H

    def body(h_ref, t_ref, cw_ref, al_ref, dt_ref, gn_ref, st_ref, inv_ref, do_ref,
             dh_ref, dcw_ref, dal_ref, ddt_ref, dgn_ref, dst, carry_dy, ext, wide):
        i = pl.program_id(0)
        first_blk = i == nb - 1

        @pl.when(i == 0)
        def _():
            dst[...] = jnp.zeros_like(dst)
            carry_dy[...] = jnp.zeros_like(carry_dy)
            for r in (dcw_ref, dal_ref, ddt_ref, dgn_ref):
                r[...] = jnp.zeros_like(r)

        tail = jnp.where(first_blk, 0.0, t_ref[:, 0:W3])
        ypre = _conv_fwd(ext, h_ref[:, 0:W3], tail, cw_ref[...], R)
        y = _silu(ypre)
        ones = _head_ones()
        qn, rq = _l2n(y[:, 0:GDN_W], ones)
        kn, rk = _l2n(y[:, GDN_W:2 * GDN_W], ones)
        qs, ks, vs = _split_heads(qn, H), _split_heads(kn, H), _split_heads(y[:, 2 * GDN_W:W3], H)
        zh = _split_heads(h_ref[:, W3:W3 + GDN_W], H)
        ab = h_ref[:, W3 + GDN_W:GDN_IN]
        alog, dtb = al_ref[...], dt_ref[...]
        g, beta = _gdn_scalars(ab, alog, dtb)
        kept = inv_ref[...]
        loc, vjp_loc = jax.vjp(functools.partial(_gdn_local, lambda m: _known_inv(m, kept)), qs, ks, vs,
                               _bcast_heads(_chunk_cumsum(g), 0, H), _bcast_heads(beta, H, H))
        doh = _split_heads(do_ref[...], H)
        gnv = gn_ref[...]
        dloc = [[None] * nc for _ in range(6)]
        dzs = [None] * nc
        ds = dst[...]
        dgn = jnp.zeros((1, CH), f32)
        for c in reversed(range(nc)):
            sl = slice(c * H, (c + 1) * H)
            _, vjp = jax.vjp(_gdn_step, st_ref[c], *(t[sl] for t in loc), zh[sl], gnv)
            grads = vjp((doh[sl], ds))
            ds = grads[0]
            for j in range(6):
                dloc[j][c] = grads[1 + j]
            dzs[c] = grads[7]
            dgn = dgn + grads[8]
        dst[...] = ds
        dgn_ref[...] += dgn
        dqs, dks, dvs, dgb, dbb = vjp_loc(tuple(jnp.concatenate(d, axis=0) for d in dloc))
        lane = lax.broadcasted_iota(jnp.int32, (R, 128), 1)
        dg = _chunk_cumsum(_unbcast_heads(dgb, 0, H), reverse=True)
        dbeta = _unbcast_heads(dbb, H, H)
        da = dg * (-jnp.exp(alog)) * _sigmoid(ab + dtb)
        dh_ref[:, W3 + GDN_W:GDN_IN] = jnp.where(lane < H, da, dbeta * beta * (1.0 - beta)).astype(bf16)
        ddt_ref[...] += jnp.sum(jnp.where(lane < H, da, 0.0), 0, keepdims=True)
        dal_ref[...] += jnp.sum(jnp.where(lane < H, dg * g, 0.0), 0, keepdims=True)
        for j, dpart in enumerate((dqs, dks, dvs)):
            _merge_heads(wide, dpart, H, col0=j * GDN_W)
        wide[:, 0:GDN_W] = _l2n_bwd(wide[:, 0:GDN_W], qn, rq, ones)
        wide[:, GDN_W:2 * GDN_W] = _l2n_bwd(wide[:, GDN_W:2 * GDN_W], kn, rk, ones)
        dy = wide[...] * _dsilu(ypre)
        dx, dcw = _conv_bwd(ext, dy, carry_dy[...], cw_ref[...], R)
        carry_dy[...] = dy[0:8, :]
        dcw_ref[...] += dcw
        dh_ref[:, 0:W3] = dx.astype(bf16)
        _merge_heads(wide, jnp.concatenate(dzs, axis=0), H)
        dh_ref[:, W3:W3 + GDN_W] = wide[:, 0:GDN_W].astype(bf16)

    blk = pl.BlockSpec((R, GDN_IN), lambda i: (nb - 1 - i, 0))
    return pl.pallas_call(
        body, grid=(nb,),
        in_specs=[blk, _prev_tail_spec_rev(R, GDN_IN, nb), _full_spec((4, W3)), _full_spec((1, 128)),
                  _full_spec((1, 128)), _full_spec((1, CH)),
                  pl.BlockSpec((nc, H, CH, CH), lambda i: (nb - 1 - i, 0, 0, 0)),
                  pl.BlockSpec((nc * H, CH, CH), lambda i: (nb - 1 - i, 0, 0)),
                  pl.BlockSpec((R, GDN_W), lambda i: (nb - 1 - i, 0))],
        out_specs=[blk, _full_spec((4, W3)), _full_spec((1, 128)), _full_spec((1, 128)), _full_spec((1, CH))],
        out_shape=[jax.ShapeDtypeStruct((T, GDN_IN), bf16), jax.ShapeDtypeStruct((4, W3), f32),
                   jax.ShapeDtypeStruct((1, 128), f32), jax.ShapeDtypeStruct((1, 128), f32),
                   jax.ShapeDtypeStruct((1, CH), f32)],
        scratch_shapes=[pltpu.VMEM((H, CH, CH), f32), pltpu.VMEM((8, W3), f32), pltpu.VMEM((R + 8, W3), f32),
                        pltpu.VMEM((R, W3), f32)],
        compiler_params=_cparams(("arbitrary",)), name="gdn_bwd")(hg, hg, cw, alog, dtb, gn, states, invs, dout)


def _block_diag(w):
    out = jnp.zeros((LRU_W, LRU_W), w.dtype)
    for g in range(w.shape[0]):
        out = lax.dynamic_update_slice(out, w[g], (g * CH, g * CH))
    return out


def _block_diag_t(w):
    return jnp.stack([w[g * CH:(g + 1) * CH, g * CH:(g + 1) * CH] for g in range(LRU_W // CH)])


def _pad_lanes(v, n=128):
    return jnp.pad(v, (0, n - v.shape[0]))[None, :]


def _local_step(x, p, positions, target, fetch, emit, sm):
    cosw, sinw = _rope_tables(positions)
    cols = lambda w: jnp.transpose(w, (1, 2, 0, 3)).reshape(-1, D, NDEV * FSP)
    saved = []
    h = x
    for l in range(DEPTH):
        v = lambda n: sm[n][l][None, :]
        F1, tok = fetch(l, 'f1', h)
        p384a, pda = cols(F1['p384']), F1['pd']
        z1, x1, g1, u1, a1 = _ffn_fwd(h, p384a, pda, v('ln_ffn1_g') + tok, v('ln_ffn1_b'), 0, 0)
        G, _ = fetch(l, 'rest', x1)
        p384, pd, pr, pinl, ping, wpp = cols(G['p384']), G['pd'], G['pr'], G['pinl'], G['ping'], G['wpp']
        wts = (p384a, pda, p384, pd, pr, pinl, ping, wpp)
        hr, hl, hg = _proj_in(x1, pr, pinl, ping, 0)
        o_r, rst = _ret_fwd(hr, cosw, sinw, v('ret_norm_g'))
        lru_args = (sm['lru_conv_w'][l], v('lru_conv_b'), _block_diag(sm['lru_w_a'][l]), v('lru_b_a'),
                    _block_diag(sm['lru_w_x'][l]), v('lru_b_x'), v('lru_lambda'))
        o_l, hs = _lru_fwd(hl, *lru_args)
        gdn_args = (sm['gdn_conv_w'][l], _pad_lanes(sm['gdn_a_log'][l]), _pad_lanes(sm['gdn_dt_bias'][l]),
                    v('gdn_norm_g'))
        o_g, *gst = _gdn_fwd(hg, *gdn_args)
        z2, x2 = _mix_out(x1, o_r, o_l, o_g, pr, v('ln_mix_g'), v('ln_mix_b'), 0)
        z3, x3, g2, u2, a2 = _ffn_fwd(x2, p384, pd, v('ln_ffn2_g'), v('ln_ffn2_b'), 0, 1, ple=(p[l], pr, wpp))
        saved.append((h, z1, x1, hr, hl, hg, o_r, rst, o_l, hs, lru_args, o_g, gst, gdn_args, z2, x2, z3,
                      g1, u1, a1, g2, u2, a2, wts))
        h = x3
    d, loss = _loss_grad(h, target)

    small = {n: [None] * DEPTH for n in SMALL}
    tok = 0.0
    for l in reversed(range(DEPTH)):
        (x0, z1, x1, hr, hl, hg, o_r, rst, o_l, hs, lru_args, o_g, gst, gdn_args, z2, x2, z3,
         g1, u1, a1, g2, u2, a2, wts) = saved[l]
        p384a, pda, p384, pd, pr, pinl, ping, wpp = wts
        v = lambda n: sm[n][l][None, :]
        rows = lambda m: m.reshape(NDEV, m.shape[1] // NDEV, m.shape[2])
        d2, dg2, du2, dy2, small['ln_ffn2_g'][l], small['ln_ffn2_b'][l] = _ffn_bwd(
            z3, d, g2, u2, p384, pd, v('ln_ffn2_g') + tok, 0, 1)
        d2, dgp, dpj = _ple_bwd(x2, p[l], dy2, d2, pr, wpp, 0)
        dxb, dzb, do_r, do_l, do_g, small['ln_mix_g'][l], small['ln_mix_b'][l] = _mix_out_bwd(
            z2, d2, pr, v('ln_mix_g'), 0)
        dhr, small['ret_norm_g'][l] = _ret_bwd(hr, cosw, sinw, v('ret_norm_g'), rst, do_r)
        (dhl, small['lru_conv_w'][l], small['lru_conv_b'][l], dwa, small['lru_b_a'][l], dwx, small['lru_b_x'][l],
         small['lru_lambda'][l]) = _lru_bwd(hl, hs, *lru_args, do_l)
        small['lru_w_a'][l], small['lru_w_x'][l] = _block_diag_t(dwa), _block_diag_t(dwx)
        dhg, small['gdn_conv_w'][l], dal, ddt, small['gdn_norm_g'][l] = _gdn_bwd(hg, *gdn_args, *gst, do_g)
        small['gdn_a_log'][l], small['gdn_dt_bias'][l] = dal[:, 0:GDN_H], ddt[:, 0:GDN_H]
        d1 = _proj_in_bwd(dxb, dhr, dhl, dhg, pr, pinl, ping, 0)
        dwo = jnp.concatenate([_matmul_tn(o_r, dzb, D, "dw_out_r"), _matmul_tn(o_l, dzb, D, "dw_out_l"),
                               _matmul_tn(o_g, dzb, D, "dw_out_g")], axis=1)
        tok = emit(l, 'rest', {
            'p384': jnp.stack([_matmul_tn(x2, dg2, FSP, "dw_gate", NDEV), _matmul_tn(x2, du2, FSP, "dw_up", NDEV)],
                              axis=1),
            'pd': rows(_matmul_tn(a2, dy2, D, "dw_down"))[:, None],
            'pr': jnp.stack([rows(_matmul_tn(x1, dhr, D, "dw_in_r")), rows(dwo),
                             rows(_matmul_tn(x2, dgp, D, "dw_ple_gate"))], axis=1),
            'pinl': rows(_matmul_tn(x1, dhl, 2 * LRU_W, "dw_in_l"))[:, None],
            'ping': rows(_matmul_tn(x1, dhg, GDN_IN, "dw_in_g"))[:, None],
            'ppp': jnp.transpose(_matmul_tn(p[l], dpj, D, "dw_ple_proj").reshape(PLE, NDEV, 128), (1, 0, 2))[:, None]})
        d, dg1, du1, dy1, small['ln_ffn1_g'][l], small['ln_ffn1_b'][l] = _ffn_bwd(
            z1, d1, g1, u1, p384a, pda, v('ln_ffn1_g') + tok, 0, 0)
        if l == 0:
            emit(l, 'small', {n: jnp.stack([g.reshape(sm[n].shape[1:]) for g in gs]) for n, gs in small.items()})
        tok = emit(l, 'f1', {
            'p384': jnp.stack([_matmul_tn(x0, dg1, FSP, "dw_gate", NDEV), _matmul_tn(x0, du1, FSP, "dw_up", NDEV)],
                              axis=1),
            'pd': rows(_matmul_tn(a1, dy1, D, "dw_down"))[:, None]})
    return loss, d


def _pack_big(ws, dtype=bf16):
    padc = lambda a, n: jnp.pad(a, ((0, 0), (0, 0), (0, n - a.shape[2])))
    padr = lambda a, n: jnp.pad(a, ((0, 0), (0, n - a.shape[1]), (0, 0)))
    per_layer = lambda arrs: jnp.stack(arrs, axis=1).reshape((-1,) + arrs[0].shape[1:])
    w_in = ws['w_in']
    out = {
        'p384': per_layer([padc(ws[n], FSP) for n in ('ffn1_w_gate', 'ffn1_w_up', 'ffn2_w_gate', 'ffn2_w_up')]),
        'pd': per_layer([padr(ws[n], FSP) for n in ('ffn1_w_down', 'ffn2_w_down')]),
        'pr': per_layer([w_in[:, :, 0:D], ws['w_out'], ws['ple_w_gate']]),
        'pinl': w_in[:, :, D:D + 2 * LRU_W],
        'ping': padc(w_in[:, :, D + 2 * LRU_W:D_IN], GDN_IN),
        'ppp': ws['ple_w_proj'],
    }
    return {k: a.astype(dtype) for k, a in out.items()}


def _gather_two_level(arrays, name):
    n = len(arrays)

    def body(*refs):
        ins, outs = refs[:n], refs[n:2 * n]
        send_sems, recv_sems, local_sems = refs[2 * n:]
        x, y, c = lax.axis_index("x"), lax.axis_index("y"), lax.axis_index("c")
        me, sibling = (x, y, c), (x, y, 1 - c)
        chips = [(1 - x, y), (x, 1 - y), (1 - x, 1 - y)]
        slot = lambda d: 4 * d[0] + 2 * d[1] + d[2]

        def copy(i, k, block, to, src=None):
            return pltpu.make_async_remote_copy(
                src_ref=outs[i].at[slot(block)] if src is None else src, dst_ref=outs[i].at[slot(block)],
                send_sem=send_sems.at[i, k], recv_sem=recv_sems.at[i, k], device_id=to,
                device_id_type=pl.DeviceIdType.MESH)

        mine, first, passed = [], [], []
        for i in range(n):
            cp = pltpu.make_async_copy(ins[i], outs[i].at[slot(me)], local_sems.at[i])
            cp.start()
            mine.append(cp)
            first.append(copy(i, 0, me, sibling, src=ins[i]))
            first += [copy(i, 1 + j, me, (*chip, c), src=ins[i]) for j, chip in enumerate(chips)]
        for cp in first:
            cp.start()
        for i in range(n):
            for j, chip in enumerate(chips):
                copy(i, 1 + j, (*chip, c), me).wait_recv()
                cp = copy(i, 4 + j, (*chip, c), sibling)
                cp.start()
                passed.append(cp)
        for i in range(n):
            copy(i, 0, sibling, me).wait_recv()
            for j, chip in enumerate(chips):
                copy(i, 4 + j, (*chip, 1 - c), me).wait_recv()
        for cp in first + passed:
            cp.wait_send()
        for cp in mine:
            cp.wait()

    hbm = pl.BlockSpec(memory_space=pltpu.HBM)
    return pl.pallas_call(
        body, in_specs=[hbm] * n, out_specs=[hbm] * n,
        out_shape=[jax.ShapeDtypeStruct((NDEV,) + a.shape, a.dtype) for a in arrays],
        scratch_shapes=[pltpu.SemaphoreType.DMA((n, NDEV - 1)), pltpu.SemaphoreType.DMA((n, NDEV - 1)),
                        pltpu.SemaphoreType.DMA((n,))],
        compiler_params=pltpu.CompilerParams(has_side_effects=True), name=name)(*arrays)


def _scatter_pairs(arrays, name):
    n = len(arrays)

    def body(*refs):
        ins, gots = refs[:n], refs[n:2 * n]
        send_sems, recv_sems = refs[2 * n:]
        x, y, c = lax.axis_index("x"), lax.axis_index("y"), lax.axis_index("c")
        sends = []
        for i in range(n):
            for q in range(4):
                cp = pltpu.make_async_remote_copy(
                    src_ref=ins[i].at[2 * q + 1 - c], dst_ref=gots[i].at[q], send_sem=send_sems.at[i, q],
                    recv_sem=recv_sems.at[i, q], device_id=(x, y, 1 - c), device_id_type=pl.DeviceIdType.MESH)
                cp.start()
                sends.append(cp)
        for cp in sends:
            cp.wait_recv()
        for cp in sends:
            cp.wait_send()

    hbm = pl.BlockSpec(memory_space=pltpu.HBM)
    return pl.pallas_call(
        body, in_specs=[hbm] * n, out_specs=[hbm] * n,
        out_shape=[jax.ShapeDtypeStruct((4,) + a.shape[1:], a.dtype) for a in arrays],
        scratch_shapes=[pltpu.SemaphoreType.DMA((n, 4)), pltpu.SemaphoreType.DMA((n, 4))],
        compiler_params=pltpu.CompilerParams(has_side_effects=True), name=name)(*arrays)


def _pair_sum(own, got, name):
    def body(a_ref, b_ref, o_ref):
        o_ref[...] = (a_ref[...].astype(f32) + b_ref[...].astype(f32)).astype(bf16)

    spec = pl.BlockSpec((None, None) + own.shape[2:], lambda q, s: (q, s, 0, 0))
    return pl.pallas_call(
        body, grid=own.shape[:2], in_specs=[spec, spec], out_specs=spec,
        out_shape=jax.ShapeDtypeStruct(own.shape, bf16),
        compiler_params=_cparams(("arbitrary", "arbitrary")), name=name)(own, got)


def _gather_plan(srcs, lands, x, y, c):
    me = 4 * x + 2 * y + c
    sends, arrivals = [], []
    for j in range(1, NDEV):
        peer, source = (me + j) % NDEV, (me + NDEV - j) % NDEV
        for i in range(len(srcs)):
            k = i * (NDEV - 1) + j - 1
            sends.append((srcs[i], lands[i].at[me], (peer // 4, (peer // 2) % 2, peer % 2), k))
            arrivals.append((srcs[i], lands[i].at[source], (source // 4, (source // 2) % 2, source % 2), k))
    return sends, arrivals


def _chips_plan(srcs, lands, x, y, c):
    chip = 2 * x + y
    sends, arrivals = [], []
    for j in range(1, 4):
        peer, source = (chip + j) % 4, (chip + 4 - j) % 4
        for i in range(len(srcs)):
            k = i * 3 + j - 1
            sends.append((srcs[i].at[peer], lands[i].at[chip], (peer // 2, peer % 2, c), k))
            arrivals.append((srcs[i].at[chip], lands[i].at[source], (source // 2, source % 2, c), k))
    return sends, arrivals


def _remote(entry, send_sems, recv_sems):
    src, dst, dev, k = entry
    return pltpu.make_async_remote_copy(src_ref=src, dst_ref=dst, send_sem=send_sems.at[k], recv_sem=recv_sems.at[k],
                                        device_id=dev, device_id_type=pl.DeviceIdType.MESH)


_HBM = pl.BlockSpec(memory_space=pltpu.HBM)
_SEM = pl.BlockSpec(memory_space=pltpu.SEMAPHORE)


def _split_start(arrays, land_shapes, plan, npeer, name):
    n = len(arrays)

    def body(*refs):
        srcs, lands = refs[:n], refs[n:2 * n]
        send_sems, recv_sems, token = refs[2 * n], refs[2 * n + 1], refs[-1]
        sends, _ = plan(srcs, lands, lax.axis_index("x"), lax.axis_index("y"), lax.axis_index("c"))
        for entry in sends:
            _remote(entry, send_sems, recv_sems).start()
        token[...] = jnp.zeros_like(token)

    lands = [lax.empty(s, a.dtype) for s, a in zip(land_shapes, arrays)]
    thru = [pltpu.HBM(a.shape, a.dtype) for a in arrays + lands]
    out = pl.pallas_call(
        body, name=name, in_specs=[_HBM] * (2 * n),
        out_specs=(_SEM, _SEM, *([_HBM] * (2 * n)), pl.BlockSpec(memory_space=pltpu.VMEM)),
        out_shape=(pltpu.SemaphoreType.DMA((n * npeer,)), pltpu.SemaphoreType.DMA((n * npeer,)), *thru,
                   jax.ShapeDtypeStruct((8, 128), f32)),
        input_output_aliases={i: 2 + i for i in range(2 * n)},
        compiler_params=pltpu.CompilerParams(has_side_effects=pltpu.SideEffectType.DATAFLOW_SIDE_EFFECTING),
    )(*[pltpu.with_memory_space_constraint(a, pltpu.HBM) for a in arrays + lands])
    return out[0], out[1], list(out[2:2 + n]), list(out[2 + n:2 + 2 * n]), out[-1]


def _split_wait(send_sems, recv_sems, srcs, lands, after, plan, name):
    n = len(srcs)

    def body(*refs):
        s_refs, l_refs = refs[:n], refs[n:2 * n]
        ssem, rsem = refs[2 * n], refs[2 * n + 1]
        sends, arrivals = plan(s_refs, l_refs, lax.axis_index("x"), lax.axis_index("y"), lax.axis_index("c"))
        for entry in sends:
            _remote(entry, ssem, rsem).wait_send()
        for entry in arrivals:
            _remote(entry, ssem, rsem).wait_recv()

    out = pl.pallas_call(
        body, name=name, in_specs=[_HBM] * (2 * n) + [_SEM, _SEM, pl.BlockSpec(memory_space=pl.ANY)],
        out_specs=[_HBM] * (2 * n), out_shape=[pltpu.HBM(a.shape, a.dtype) for a in srcs + lands],
        input_output_aliases={i: i for i in range(2 * n)},
        compiler_params=pltpu.CompilerParams(has_side_effects=pltpu.SideEffectType.DATAFLOW_SIDE_EFFECTING),
    )(*srcs, *lands, send_sems, recv_sems, after)
    return list(out[:n]), list(out[n:])


def _adam_math(w, g, m, v):
    m2 = ADAM_B1 * m + (1.0 - ADAM_B1) * g
    v2 = ADAM_B2 * v + (1.0 - ADAM_B2) * (g * g)
    m_hat = m2 / (1.0 - ADAM_B1 ** ADAM_STEP)
    v_hat = v2 / (1.0 - ADAM_B2 ** ADAM_STEP)
    return -ADAM_LR * (m_hat / (jnp.sqrt(v_hat) + ADAM_EPS) + ADAM_WD * w), m2, v2


def _adam_big(parts, w, m, v, anchor, name):
    L, rows, cols = w.shape
    flat = [(a, slot) for layer_parts in parts for a, slot in layer_parts]
    per = len(parts[0])

    def body(*refs):
        prefs = refs[:len(flat)]
        w_ref, m_ref, v_ref, _, g_ref, d_ref, m2_ref, v2_ref = refs[len(flat):]
        for li in range(L):
            @pl.when(pl.program_id(0) == li)
            def _():
                c0 = 0
                for pref in prefs[li * per:(li + 1) * per]:
                    acc = pref[0].astype(f32)
                    for s in range(1, pref.shape[0]):
                        acc = acc + pref[s].astype(f32)
                    width = min(acc.shape[1], cols - c0)
                    g_ref[:, c0:c0 + width] = acc[0:rows, 0:width]
                    c0 += width

        d, m2, v2 = _adam_math(w_ref[...], g_ref[...], m_ref[...], v_ref[...])
        d_ref[...] = d
        m2_ref[...] = m2
        v2_ref[...] = v2

    wspec = pl.BlockSpec((None, rows, cols), lambda l: (l, 0, 0))
    in_specs = [pl.BlockSpec((a.shape[0], None) + a.shape[2:], functools.partial(lambda l, slot: (0, slot, 0, 0), slot=slot))
                for a, slot in flat]
    return pl.pallas_call(
        body, grid=(L,), in_specs=in_specs + [wspec] * 3 + [_full_spec((8, 128))], out_specs=[wspec] * 4,
        out_shape=[jax.ShapeDtypeStruct(w.shape, f32)] * 4,
        compiler_params=_cparams(("arbitrary",)), name=name)(*[a for a, _ in flat], w, m, v, anchor)


def _sum_sources(stacked):
    rows = stacked.shape[1]

    def body(s_ref, o_ref):
        acc = s_ref[0]
        for s in range(1, NDEV):
            acc = acc + s_ref[s]
        o_ref[...] = acc

    return pl.pallas_call(body, out_shape=jax.ShapeDtypeStruct((rows, 128), f32), name="sum_small_grads")(stacked)


def _adam_small(w, g, m, v):
    def body(w_ref, g_ref, m_ref, v_ref, d_ref, m2_ref, v2_ref):
        d, m2, v2 = _adam_math(w_ref[...], g_ref[...], m_ref[...], v_ref[...])
        d_ref[...] = d
        m2_ref[...] = m2
        v2_ref[...] = v2

    return pl.pallas_call(body, out_shape=[jax.ShapeDtypeStruct(w.shape, f32)] * 3, name="adam_small")(w, g, m, v)


def _pack_rows(arrs):
    flat = []
    for a in arrs:
        a = a.reshape(-1)
        flat.append(jnp.pad(a, (0, (-a.shape[0]) % 1024)))
    return jnp.concatenate(flat).reshape(-1, 128)


def _unpack_rows(packed, shapes):
    out, off = [], 0
    flat = packed.reshape(-1)
    for s in shapes:
        n = math.prod(s)
        out.append(flat[off:off + n].reshape(s))
        off += n + (-n) % 1024
    return out


def _gather_conv(gathered, shape):
    L, K, c = shape
    return jnp.transpose(gathered, (1, 2, 0, 3)).reshape(L, K, NDEV * c)


def kernel(x, p, positions, ln_ffn1_g, ln_ffn1_b, ffn1_w_gate, ffn1_w_up, ffn1_w_down, w_in, ret_norm_g, lru_conv_w, lru_conv_b, lru_w_a, lru_b_a, lru_w_x, lru_b_x, lru_lambda, gdn_conv_w, gdn_a_log, gdn_dt_bias, gdn_norm_g, w_out, ln_mix_g, ln_mix_b, ffn2_w_gate, ffn2_w_up, ffn2_w_down, ple_w_gate, ple_w_proj, ln_ffn2_g, ln_ffn2_b, loss_target, m_ln_ffn1_g, m_ln_ffn1_b, m_ffn1_w_gate, m_ffn1_w_up, m_ffn1_w_down, m_w_in, m_ret_norm_g, m_lru_conv_w, m_lru_conv_b, m_lru_w_a, m_lru_b_a, m_lru_w_x, m_lru_b_x, m_lru_lambda, m_gdn_conv_w, m_gdn_a_log, m_gdn_dt_bias, m_gdn_norm_g, m_w_out, m_ln_mix_g, m_ln_mix_b, m_ffn2_w_gate, m_ffn2_w_up, m_ffn2_w_down, m_ple_w_gate, m_ple_w_proj, m_ln_ffn2_g, m_ln_ffn2_b, v_ln_ffn1_g, v_ln_ffn1_b, v_ffn1_w_gate, v_ffn1_w_up, v_ffn1_w_down, v_w_in, v_ret_norm_g, v_lru_conv_w, v_lru_conv_b, v_lru_w_a, v_lru_b_a, v_lru_w_x, v_lru_b_x, v_lru_lambda, v_gdn_conv_w, v_gdn_a_log, v_gdn_dt_bias, v_gdn_norm_g, v_w_out, v_ln_mix_g, v_ln_mix_b, v_ffn2_w_gate, v_ffn2_w_up, v_ffn2_w_down, v_ple_w_gate, v_ple_w_proj, v_ln_ffn2_g, v_ln_ffn2_b):
    args = locals()
    W = {n: args[n] for n in WEIGHTS}
    M = {n: args['m_' + n] for n in WEIGHTS}
    V = {n: args['v_' + n] for n in WEIGHTS}
    me = 4 * lax.axis_index("x") + 2 * lax.axis_index("y") + lax.axis_index("c")

    core = lax.axis_index("c")
    chip = 2 * lax.axis_index("x") + lax.axis_index("y")

    packed = _pack_big(W)

    def group(l, name):
        per = {k: packed[k].shape[0] // DEPTH for k in PACKS}
        if name == 'f1':
            return [packed['p384'][l * per['p384']:l * per['p384'] + 2], packed['pd'][l * per['pd']:l * per['pd'] + 1]]
        return [packed['p384'][l * per['p384'] + 2:(l + 1) * per['p384']],
                packed['pd'][l * per['pd'] + 1:(l + 1) * per['pd']]] + [
                    packed[k][l * per[k]:(l + 1) * per[k]] for k in PACKS[2:]]

    def as_weights(arrs):
        G = dict(zip(PACKS, arrs))
        if 'ppp' in G:
            G['wpp'] = jnp.transpose(G.pop('ppp'), (1, 2, 0, 3)).reshape(PLE, D)
        return G

    conv_pack = _pack_rows([W[n] for n in CONV_SHARDED])
    g0 = _gather_two_level(group(0, 'f1') + [conv_pack], "gather_weights")
    g0, rest0 = lax.optimization_barrier((g0, group(0, 'rest')))
    start0 = _split_start(rest0, [(NDEV,) + a.shape for a in rest0], _gather_plan, NDEV - 1, "gather_start_0")
    tok0, all1 = lax.optimization_barrier((start0[4], group(1, 'f1') + group(1, 'rest')))
    start1 = _split_start(all1, [(NDEV,) + a.shape for a in all1], _gather_plan, NDEV - 1, "gather_start_1")
    arrived = {}

    def gather_done(started, after, name):
        srcs, lands = _split_wait(started[0], started[1], started[2], started[3], after, _gather_plan, name)
        return [lax.dynamic_update_slice_in_dim(ld, s[None], me, axis=0) for s, ld in zip(srcs, lands)]

    def fetch(l, name, after):
        if l == 0 and name == 'f1':
            return as_weights(g0[:-1]), tok0[0, 0] + start1[4][0, 0]
        if l == 0:
            return as_weights(gather_done(start0, after, "gather_wait_0")), 0.0
        if name == 'f1':
            arrived[1] = gather_done(start1, after, "gather_wait_1")
            return as_weights(arrived[1][:2]), 0.0
        return as_weights(arrived[1][2:]), 0.0

    conv_all = g0[-1]
    sm = {n: W[n] for n in SMALL}
    conv_shards = [_unpack_rows(conv_all[s], [W[n].shape for n in CONV_SHARDED]) for s in range(NDEV)]
    for i, n in enumerate(CONV_SHARDED):
        sm[n] = _gather_conv(jnp.stack([cs[i] for cs in conv_shards]), W[n].shape)

    received, started = {}, {}

    small_shapes = [sm[n].shape for n in SMALL]
    small_started = []

    def emit(l, name, grads):
        if name == 'small':
            pack = _pack_rows([grads[n] for n in SMALL])
            small_started.extend(_split_start([pack], [(NDEV,) + pack.shape], _gather_plan, NDEV - 1, "small_start"))
            return 0.0
        keys = list(grads)
        arrs = [grads[k] for k in keys]
        gots = _scatter_pairs(arrs, "scatter_pairs")
        owns = [lax.dynamic_index_in_dim(a.reshape((4, 2) + a.shape[1:]), core, axis=1, keepdims=False) for a in arrs]
        pair = [_pair_sum(o, g, "pair_sum_" + k) for k, o, g in zip(keys, owns, gots)]
        started[l, name] = (keys, _split_start(pair, [a.shape for a in pair], _chips_plan, 3,
                                               f"scatter_start_{l}_{name}"))
        return started[l, name][1][4][0, 0]

    def scatter_done(l, name, after):
        keys, st = started[l, name]
        srcs, lands = _split_wait(st[0], st[1], st[2], st[3], after, _chips_plan, f"scatter_wait_{l}_{name}")
        received[l, name] = dict(zip(keys, [
            lax.dynamic_update_slice_in_dim(ld, lax.dynamic_index_in_dim(s, chip, axis=0), chip, axis=0)
            for s, ld in zip(srcs, lands)]))

    loss, grad_x = _local_step(x[0], p[:, 0], positions.reshape(-1, 1), loss_target[0], fetch, emit, sm)
    loss = lax.psum(loss[0, 0], ("x", "y", "c"))
    last = (0, 'f1')
    for l, name in started:
        if (l, name) != last:
            scatter_done(l, name, grad_x)

    anchor = started[last][1][4]
    srcs, lands = _split_wait(small_started[0], small_started[1], small_started[2], small_started[3], anchor,
                              _gather_plan, "small_wait")
    small_all = lax.dynamic_update_slice_in_dim(lands[0], srcs[0][None], me, axis=0)
    small_sum = _unpack_rows(_sum_sources(small_all), small_shapes)
    grads, delta, new_m, new_v = {}, {}, {}, {}
    for n, g in zip(SMALL, small_sum):
        if n in CONV_SHARDED:
            c = W[n].shape[2]
            g = lax.dynamic_slice_in_dim(g, me * c, c, axis=2)
        grads[n] = g

    big_parts = {
        'ffn1_w_gate': [('f1', 'p384', 0)], 'ffn1_w_up': [('f1', 'p384', 1)], 'ffn1_w_down': [('f1', 'pd', 0)],
        'ffn2_w_gate': [('rest', 'p384', 0)], 'ffn2_w_up': [('rest', 'p384', 1)], 'ffn2_w_down': [('rest', 'pd', 0)],
        'w_in': [('rest', 'pr', 0), ('rest', 'pinl', 0), ('rest', 'ping', 0)], 'w_out': [('rest', 'pr', 1)],
        'ple_w_gate': [('rest', 'pr', 2)], 'ple_w_proj': [('rest', 'ppp', 0)],
    }
    def adam(n):
        parts = [[(received[l, grp][k], slot) for grp, k, slot in big_parts[n]] for l in range(DEPTH)]
        grads[n], delta[n], new_m[n], new_v[n] = _adam_big(parts, W[n], M[n], V[n], anchor, "adam_" + n)

    shapes = [W[n].shape for n in SMALL]
    d_s, m_s, v_s = _adam_small(*[_pack_rows([src[n] for n in SMALL]) for src in (W, grads, M, V)])
    for n, dd, mm, vv in zip(SMALL, _unpack_rows(d_s, shapes), _unpack_rows(m_s, shapes), _unpack_rows(v_s, shapes)):
        delta[n], new_m[n], new_v[n] = dd, mm, vv
    waits_last = [n for n in BIG if big_parts[n][0][0] == last[1]]
    for n in BIG:
        if n not in waits_last:
            adam(n)
    done = jnp.stack([d_s[0, 0]] + [delta[n][0, 0, 0] for n in BIG if n not in waits_last])
    scatter_done(*last, done)
    for n in waits_last:
        adam(n)

    return (loss, grad_x[None], *[grads[n] for n in WEIGHTS], *[delta[n] for n in WEIGHTS],
            *[new_m[n] for n in WEIGHTS], *[new_v[n] for n in WEIGHTS])
```

```python
import functools
import math

import jax
import jax.numpy as jnp
from jax import lax
from jax.experimental import pallas as pl
from jax.experimental.pallas import tpu as pltpu

f32 = jnp.float32
bf16 = jnp.bfloat16

NDEV = 8
DEPTH = 2
D = 1024
FSP = 384
FB = 2
NF = NDEV // FB
PLE = 256
CH = 64
RET_H, GDN_H = 4, 6
RET_W, LRU_W, GDN_W = 256, 384, 384
GDN_IN = 1664
D_IN = 3340
ALPHA = 4.0 ** 0.25
LN_EPS = 1e-5
ROPE_THETA = 10000.0
TM = 512
RB_RET, RB_LRU, RB_GDN = 512, 512, 256
VMEM_LIMIT = 56 * 1024 * 1024
ADAM_LR, ADAM_B1, ADAM_B2, ADAM_EPS, ADAM_WD, ADAM_STEP = 0.001, 0.9, 0.999, 1e-08, 0.01, 10

WEIGHTS = ['ln_ffn1_g', 'ln_ffn1_b', 'ffn1_w_gate', 'ffn1_w_up', 'ffn1_w_down', 'w_in', 'ret_norm_g', 'lru_conv_w',
           'lru_conv_b', 'lru_w_a', 'lru_b_a', 'lru_w_x', 'lru_b_x', 'lru_lambda', 'gdn_conv_w', 'gdn_a_log',
           'gdn_dt_bias', 'gdn_norm_g', 'w_out', 'ln_mix_g', 'ln_mix_b', 'ffn2_w_gate', 'ffn2_w_up', 'ffn2_w_down',
           'ple_w_gate', 'ple_w_proj', 'ln_ffn2_g', 'ln_ffn2_b']
BIG = ['ffn1_w_gate', 'ffn1_w_up', 'ffn1_w_down', 'w_in', 'w_out', 'ffn2_w_gate', 'ffn2_w_up', 'ffn2_w_down',
       'ple_w_gate', 'ple_w_proj']
SMALL = [n for n in WEIGHTS if n not in BIG]
PACKS = ('p384', 'pd', 'pr', 'pinl', 'ping', 'ppp')
CONV_SHARDED = {'lru_conv_w': LRU_W, 'gdn_conv_w': 3 * GDN_W}


def _cparams(sem=None):
    return pltpu.CompilerParams(dimension_semantics=sem, vmem_limit_bytes=VMEM_LIMIT)


def _sigmoid(x):
    return 1.0 / (1.0 + jnp.exp(-x))


def _silu(x):
    return x * _sigmoid(x)


def _dsilu(x):
    s = _sigmoid(x)
    return s * (1.0 + x * (1.0 - s))


def _softplus(x):
    return jnp.maximum(x, 0.0) + jnp.log(1.0 + jnp.exp(-jnp.abs(x)))


def _gelu(x):
    return 0.5 * x * (1.0 + jnp.tanh(0.7978845608028654 * (x + 0.044715 * x * x * x)))


def _dot(a, b):
    return jnp.dot(a.astype(bf16), b.astype(bf16), preferred_element_type=f32)


def _dot_nt(a, b):
    return lax.dot_general(a.astype(bf16), b.astype(bf16), (((1,), (1,)), ((), ())), preferred_element_type=f32)


def _dot_tn(a, b):
    return lax.dot_general(a.astype(bf16), b.astype(bf16), (((0,), (0,)), ((), ())), preferred_element_type=f32)


def _bmm(eq, a, b):
    return jnp.einsum(eq, a.astype(bf16), b.astype(bf16), preferred_element_type=f32)


def _split3(a):
    a1 = a.astype(bf16)
    r = a - a1.astype(f32)
    a2 = r.astype(bf16)
    return a1, a2, (r - a2.astype(f32)).astype(bf16)


def _bmm3(eq, a, b):
    a1, a2, _ = _split3(a)
    b1, b2, _ = _split3(b)
    e = lambda x, y: jnp.einsum(eq, x, y, preferred_element_type=f32)
    return e(a1, b1) + (e(a1, b2) + e(a2, b1))


def _rowsum(x):
    ones = jnp.ones((x.shape[0], CH, CH), bf16)
    return jnp.einsum('bij,bjk->bik', x.astype(bf16), ones, preferred_element_type=f32)


def _tri_ones(B, upper=False):
    ii = lax.broadcasted_iota(jnp.int32, (B, CH, CH), 1)
    jj = lax.broadcasted_iota(jnp.int32, (B, CH, CH), 2)
    return jnp.where((ii <= jj) if upper else (ii >= jj), 1.0, 0.0).astype(bf16)


def _cumsum_mm(t, x):
    x1, x2, x3 = _split3(x)
    e = lambda y: jnp.einsum('bij,bjk->bik', t, y, preferred_element_type=f32)
    return e(x1) + (e(x2) + e(x3))


def _chunk_cumsum(x, reverse=False):
    n = x.shape[0] // CH
    return _cumsum_mm(_tri_ones(n, upper=reverse), x.reshape(n, CH, 128)).reshape(x.shape)


@jax.custom_vjp
def _neumann_inv(m):
    ii = lax.broadcasted_iota(jnp.int32, m.shape, 1)
    jj = lax.broadcasted_iota(jnp.int32, m.shape, 2)
    inv = jnp.where(ii == jj, 1.0, 0.0).astype(f32) + m
    mp = m
    for _ in range(5):
        mp = _bmm3('bij,bjk->bik', mp, mp)
        inv = inv + _bmm3('bij,bjk->bik', inv, mp)
    return inv


def _neumann_inv_fwd(m):
    inv = _neumann_inv(m)
    return inv, inv


def _neumann_inv_bwd(inv, g):
    return (_bmm('bij,bkj->bik', _bmm('bji,bjk->bik', inv, g), inv),)


_neumann_inv.defvjp(_neumann_inv_fwd, _neumann_inv_bwd)


@jax.custom_vjp
def _known_inv(m, inv):
    return inv


def _known_inv_fwd(m, inv):
    return inv, inv


def _known_inv_bwd(inv, g):
    return _neumann_inv_bwd(inv, g)[0], jnp.zeros_like(inv)


_known_inv.defvjp(_known_inv_fwd, _known_inv_bwd)


def _ln_stats(z):
    mu = jnp.mean(z, -1, keepdims=True)
    zc = z - mu
    rstd = lax.rsqrt(jnp.mean(zc * zc, -1, keepdims=True) + LN_EPS)
    return zc * rstd, rstd


def _ln_bwd(z, g, dout):
    xh, rstd = _ln_stats(z)
    dxh = dout * g
    dz = rstd * (dxh - jnp.mean(dxh, -1, keepdims=True) - xh * jnp.mean(dxh * xh, -1, keepdims=True))
    return dz, jnp.sum(dout * xh, 0, keepdims=True), jnp.sum(dout, 0, keepdims=True)


def _full_spec(shape):
    nd = len(shape)
    return pl.BlockSpec(shape, lambda *_: (0,) * nd)


def _ffn_fwd(x, p384, pd, lg, lb, slot, which, ple=None):
    T = x.shape[0]
    sg, su, sd = 2 * slot, 2 * slot + 1, slot
    has_ple = ple is not None

    def body(*refs):
        if has_ple:
            (x_ref, wg_ref, wu_ref, wd_ref, lg_ref, lb_ref, p_ref, wpg_ref, wpp_ref,
             z_ref, o_ref, g_ref, u_ref, a_ref, acc, xb_s) = refs
        else:
            x_ref, wg_ref, wu_ref, wd_ref, lg_ref, lb_ref, z_ref, o_ref, g_ref, u_ref, a_ref, acc, xb_s = refs
        f = pl.program_id(1)

        @pl.when(f == 0)
        def _():
            x = x_ref[...]
            xb = x.astype(bf16)
            xb_s[...] = xb
            base = ALPHA * x
            if has_ple:
                gate = _sigmoid(_dot(xb, wpg_ref[...].reshape(D, D)))
                base = base + gate * _dot(p_ref[...], wpp_ref[...])
            acc[...] = base

        xb = xb_s[...]
        g = _dot(xb, wg_ref[...])
        u = _dot(xb, wu_ref[...])
        g_ref[...] = g.astype(bf16)
        u_ref[...] = u.astype(bf16)
        a = (_silu(g) * u).astype(bf16)
        a_ref[...] = a
        acc[...] += 0.5 * _dot(a, wd_ref[...].reshape(FB * FSP, D))

        @pl.when(f == NF - 1)
        def _():
            z = acc[...]
            z_ref[...] = z
            o_ref[...] = _ln_stats(z)[0] * lg_ref[...] + lb_ref[...]

    row = pl.BlockSpec((TM, D), lambda i, f: (i, 0))
    in_specs = [row,
                pl.BlockSpec((None, D, FB * FSP), lambda i, f: (sg, 0, f)),
                pl.BlockSpec((None, D, FB * FSP), lambda i, f: (su, 0, f)),
                pl.BlockSpec((FB, None, FSP, D), lambda i, f: (f, sd, 0, 0)),
                _full_spec((1, D)), _full_spec((1, D))]
    args = [x, p384, p384, pd, lg, lb]
    if has_ple:
        p, pr, wpp = ple
        in_specs += [pl.BlockSpec((TM, PLE), lambda i, f: (i, 0)),
                     pl.BlockSpec((NDEV, None, 128, D), lambda i, f: (0, 2, 0, 0)),
                     _full_spec((PLE, D))]
        args += [p, pr, wpp]
    hid = pl.BlockSpec((TM, FB * FSP), lambda i, f: (i, f))
    hshape = jax.ShapeDtypeStruct((T, NDEV * FSP), bf16)
    return pl.pallas_call(
        body, grid=(T // TM, NF), in_specs=in_specs, out_specs=[row, row, hid, hid, hid],
        out_shape=[jax.ShapeDtypeStruct((T, D), f32)] * 2 + [hshape, hshape, hshape],
        scratch_shapes=[pltpu.VMEM((TM, D), f32), pltpu.VMEM((TM, D), bf16)],
        compiler_params=_cparams(("arbitrary", "arbitrary")), name=f"ffn{which + 1}_fwd")(*args)


def _ffn_bwd(z, dout, gs, us, p384, pd, lg, slot, which):
    T = z.shape[0]
    TMB = TM
    sg, su, sd = 2 * slot, 2 * slot + 1, slot

    def body(z_ref, do_ref, g_ref, u_ref, wg_ref, wu_ref, wd_ref, lg_ref,
             dx_ref, dg_ref, du_ref, dy_ref, dlg_ref, dlb_ref, acc, dyb):
        i, f = pl.program_id(0), pl.program_id(1)

        @pl.when(jnp.logical_and(i == 0, f == 0))
        def _():
            dlg_ref[...] = jnp.zeros_like(dlg_ref)
            dlb_ref[...] = jnp.zeros_like(dlb_ref)

        @pl.when(f == 0)
        def _():
            dz, dlg, dlb = _ln_bwd(z_ref[...], lg_ref[...], do_ref[...])
            dlg_ref[...] += dlg
            dlb_ref[...] += dlb
            dy = (0.5 * dz).astype(bf16)
            dyb[...] = dy
            dy_ref[...] = dy
            acc[...] = ALPHA * dz

        g = g_ref[...].astype(f32)
        u = u_ref[...].astype(f32)
        da = _dot_nt(dyb[...], wd_ref[...].reshape(FB * FSP, D))
        sgm = _sigmoid(g)
        dg = (da * u * (sgm * (1.0 + g * (1.0 - sgm)))).astype(bf16)
        du = (da * (g * sgm)).astype(bf16)
        dg_ref[...] = dg
        du_ref[...] = du
        acc[...] += _dot_nt(dg, wg_ref[...]) + _dot_nt(du, wu_ref[...])

        @pl.when(f == NF - 1)
        def _():
            dx_ref[...] = acc[...]

    row = pl.BlockSpec((TMB, D), lambda i, f: (i, 0))
    hid = pl.BlockSpec((TMB, FB * FSP), lambda i, f: (i, f))
    vec = _full_spec((1, D))
    in_specs = [row, row, hid, hid,
                pl.BlockSpec((None, D, FB * FSP), lambda i, f: (sg, 0, f)),
                pl.BlockSpec((None, D, FB * FSP), lambda i, f: (su, 0, f)),
                pl.BlockSpec((FB, None, FSP, D), lambda i, f: (f, sd, 0, 0)),
                vec]
    args = [z, dout, gs, us, p384, p384, pd, lg]
    out_specs = [row, hid, hid, row, vec, vec]
    hshape = jax.ShapeDtypeStruct((T, NDEV * FSP), bf16)
    out_shape = [jax.ShapeDtypeStruct((T, D), f32), hshape, hshape, jax.ShapeDtypeStruct((T, D), bf16),
                 jax.ShapeDtypeStruct((1, D), f32), jax.ShapeDtypeStruct((1, D), f32)]
    return pl.pallas_call(
        body, grid=(T // TMB, NF), in_specs=in_specs, out_specs=out_specs, out_shape=out_shape,
        scratch_shapes=[pltpu.VMEM((TMB, D), f32), pltpu.VMEM((TMB, D), bf16)],
        compiler_params=_cparams(("arbitrary", "arbitrary")), name=f"ffn{which + 1}_bwd")(*args)


def _ple_bwd(x, p, dy, dx_ffn, pr, wpp, layer):
    T = x.shape[0]

    def body(x_ref, p_ref, dy_ref, dxf_ref, wpg_ref, wpp_ref, dx_ref, dgp_ref, dpj_ref):
        dz = 2.0 * dy_ref[...].astype(f32)
        wpg = wpg_ref[...].reshape(D, D)
        gate = _sigmoid(_dot(x_ref[...], wpg))
        proj = _dot(p_ref[...], wpp_ref[...])
        dgp = (dz * proj * gate * (1.0 - gate)).astype(bf16)
        dgp_ref[...] = dgp
        dpj_ref[...] = (dz * gate).astype(bf16)
        dx_ref[...] = dxf_ref[...] + _dot_nt(dgp, wpg)

    row = pl.BlockSpec((TM, D), lambda i: (i, 0))
    return pl.pallas_call(
        body, grid=(T // TM,),
        in_specs=[row, pl.BlockSpec((TM, PLE), lambda i: (i, 0)), row, row,
                  pl.BlockSpec((NDEV, None, 128, D), lambda i: (0, 3 * layer + 2, 0, 0)), _full_spec((PLE, D))],
        out_specs=[row, row, row],
        out_shape=[jax.ShapeDtypeStruct((T, D), f32), jax.ShapeDtypeStruct((T, D), bf16),
                   jax.ShapeDtypeStruct((T, D), bf16)],
        compiler_params=_cparams(("arbitrary",)), name="ple_bwd")(x, p, dy, dx_ffn, pr, wpp)


def _matmul_tn(a, b, nb, name, nsub=1):
    T, M = a.shape
    N = b.shape[1]
    wide = nsub * nb
    tk = min(T, 1024 if wide <= 2048 else 512)
    nk = T // tk

    def body(a_ref, b_ref, o_ref, acc):
        k = pl.program_id(1)

        @pl.when(k == 0)
        def _():
            acc[...] = jnp.zeros_like(acc)

        acc[...] += _dot_tn(a_ref[...], b_ref[...])

        @pl.when(k == nk - 1)
        def _():
            for j in range(nsub):
                o_ref[j] = acc[:, j * nb:(j + 1) * nb].astype(bf16)

    return pl.pallas_call(
        body, grid=(N // wide, nk),
        in_specs=[pl.BlockSpec((tk, M), lambda n, k: (k, 0)), pl.BlockSpec((tk, wide), lambda n, k: (k, n))],
        out_specs=pl.BlockSpec((nsub, M, nb), lambda n, k: (n, 0, 0)),
        out_shape=jax.ShapeDtypeStruct((N // nb, M, nb), bf16),
        scratch_shapes=[pltpu.VMEM((M, wide), f32)],
        compiler_params=_cparams(("arbitrary", "arbitrary")), name=name)(a, b)


def _proj_in(x, pr, pinl, ping, layer):
    T = x.shape[0]

    def body(x_ref, wr_ref, wl_ref, wg_ref, hr_ref, hl_ref, hg_ref):
        xb = x_ref[...].astype(bf16)
        hr_ref[...] = _dot(xb, wr_ref[...].reshape(D, D))
        hl_ref[...] = _dot(xb, wl_ref[...].reshape(D, 2 * LRU_W))
        hg_ref[...] = _dot(xb, wg_ref[...].reshape(D, GDN_IN))

    return pl.pallas_call(
        body, grid=(T // TM,),
        in_specs=[pl.BlockSpec((TM, D), lambda i: (i, 0)),
                  pl.BlockSpec((NDEV, None, 128, D), lambda i: (0, 3 * layer, 0, 0)),
                  pl.BlockSpec((NDEV, None, 128, 2 * LRU_W), lambda i: (0, layer, 0, 0)),
                  pl.BlockSpec((NDEV, None, 128, GDN_IN), lambda i: (0, layer, 0, 0))],
        out_specs=[pl.BlockSpec((TM, D), lambda i: (i, 0)), pl.BlockSpec((TM, 2 * LRU_W), lambda i: (i, 0)),
                   pl.BlockSpec((TM, GDN_IN), lambda i: (i, 0))],
        out_shape=[jax.ShapeDtypeStruct((T, D), f32), jax.ShapeDtypeStruct((T, 2 * LRU_W), f32),
                   jax.ShapeDtypeStruct((T, GDN_IN), f32)],
        compiler_params=_cparams(("arbitrary",)), name="proj_in")(x, pr, pinl, ping)


def _proj_in_bwd(base, dhr, dhl, dhg, pr, pinl, ping, layer):
    T = base.shape[0]

    def body(b_ref, dr_ref, dl_ref, dg_ref, wr_ref, wl_ref, wg_ref, o_ref):
        o_ref[...] = (b_ref[...] + _dot_nt(dr_ref[...], wr_ref[...].reshape(D, D))
                      + _dot_nt(dl_ref[...], wl_ref[...].reshape(D, 2 * LRU_W))
                      + _dot_nt(dg_ref[...], wg_ref[...].reshape(D, GDN_IN)))

    return pl.pallas_call(
        body, grid=(T // TM,),
        in_specs=[pl.BlockSpec((TM, D), lambda i: (i, 0)), pl.BlockSpec((TM, D), lambda i: (i, 0)),
                  pl.BlockSpec((TM, 2 * LRU_W), lambda i: (i, 0)), pl.BlockSpec((TM, GDN_IN), lambda i: (i, 0)),
                  pl.BlockSpec((NDEV, None, 128, D), lambda i: (0, 3 * layer, 0, 0)),
                  pl.BlockSpec((NDEV, None, 128, 2 * LRU_W), lambda i: (0, layer, 0, 0)),
                  pl.BlockSpec((NDEV, None, 128, GDN_IN), lambda i: (0, layer, 0, 0))],
        out_specs=pl.BlockSpec((TM, D), lambda i: (i, 0)),
        out_shape=jax.ShapeDtypeStruct((T, D), f32),
        compiler_params=_cparams(("arbitrary",)), name="proj_in_bwd")(base, dhr, dhl, dhg, pr, pinl, ping)


def _mix_out(x1, o_r, o_l, o_g, pr, lg, lb, layer):
    T = x1.shape[0]

    def body(x_ref, r_ref, l_ref, g_ref, w_ref, lg_ref, lb_ref, z_ref, o_ref):
        w = w_ref[...].reshape(D, D)
        z = (ALPHA * x_ref[...] + _dot(r_ref[...], w[0:RET_W]) + _dot(l_ref[...], w[RET_W:RET_W + LRU_W])
             + _dot(g_ref[...], w[RET_W + LRU_W:D]))
        z_ref[...] = z
        o_ref[...] = _ln_stats(z)[0] * lg_ref[...] + lb_ref[...]

    row = pl.BlockSpec((TM, D), lambda i: (i, 0))
    return pl.pallas_call(
        body, grid=(T // TM,),
        in_specs=[row, pl.BlockSpec((TM, RET_W), lambda i: (i, 0)), pl.BlockSpec((TM, LRU_W), lambda i: (i, 0)),
                  pl.BlockSpec((TM, GDN_W), lambda i: (i, 0)),
                  pl.BlockSpec((NDEV, None, 128, D), lambda i: (0, 3 * layer + 1, 0, 0)),
                  _full_spec((1, D)), _full_spec((1, D))],
        out_specs=[row, row], out_shape=[jax.ShapeDtypeStruct((T, D), f32)] * 2,
        compiler_params=_cparams(("arbitrary",)), name="mix_out")(x1, o_r, o_l, o_g, pr, lg, lb)


def _ple_mix_bwd(x, p, dy, dx_ffn, z, pr, wpp, lg):
    T = x.shape[0]

    def body(x_ref, p_ref, dy_ref, dxf_ref, z_ref, wpg_ref, wo_ref, wpp_ref, lg_ref,
             dxb_ref, dzb_ref, dr_ref, dl_ref, dg_ref, dlg_ref, dlb_ref, dgp_ref, dpj_ref):
        @pl.when(pl.program_id(0) == 0)
        def _():
            dlg_ref[...] = jnp.zeros_like(dlg_ref)
            dlb_ref[...] = jnp.zeros_like(dlb_ref)

        dz3 = 2.0 * dy_ref[...].astype(f32)
        wpg = wpg_ref[...].reshape(D, D)
        gate = _sigmoid(_dot(x_ref[...], wpg))
        proj = _dot(p_ref[...], wpp_ref[...])
        dgp = (dz3 * proj * gate * (1.0 - gate)).astype(bf16)
        dgp_ref[...] = dgp
        dpj_ref[...] = (dz3 * gate).astype(bf16)
        dz, dlg, dlb = _ln_bwd(z_ref[...], lg_ref[...], dxf_ref[...] + _dot_nt(dgp, wpg))
        dlg_ref[...] += dlg
        dlb_ref[...] += dlb
        dxb_ref[...] = ALPHA * dz
        dzb = dz.astype(bf16)
        dzb_ref[...] = dzb
        w = wo_ref[...].reshape(D, D)
        dr_ref[...] = _dot_nt(dzb, w[0:RET_W])
        dl_ref[...] = _dot_nt(dzb, w[RET_W:RET_W + LRU_W])
        dg_ref[...] = _dot_nt(dzb, w[RET_W + LRU_W:D])

    row = pl.BlockSpec((TM, D), lambda i: (i, 0))
    vec = _full_spec((1, D))
    return pl.pallas_call(
        body, grid=(T // TM,),
        in_specs=[row, pl.BlockSpec((TM, PLE), lambda i: (i, 0)), row, row, row,
                  pl.BlockSpec((NDEV, None, 128, D), lambda i: (0, 2, 0, 0)),
                  pl.BlockSpec((NDEV, None, 128, D), lambda i: (0, 1, 0, 0)), _full_spec((PLE, D)), vec],
        out_specs=[row, row, pl.BlockSpec((TM, RET_W), lambda i: (i, 0)), pl.BlockSpec((TM, LRU_W), lambda i: (i, 0)),
                   pl.BlockSpec((TM, GDN_W), lambda i: (i, 0)), vec, vec, row, row],
        out_shape=[jax.ShapeDtypeStruct((T, D), f32), jax.ShapeDtypeStruct((T, D), bf16),
                   jax.ShapeDtypeStruct((T, RET_W), f32), jax.ShapeDtypeStruct((T, LRU_W), f32),
                   jax.ShapeDtypeStruct((T, GDN_W), f32), jax.ShapeDtypeStruct((1, D), f32),
                   jax.ShapeDtypeStruct((1, D), f32), jax.ShapeDtypeStruct((T, D), bf16),
                   jax.ShapeDtypeStruct((T, D), bf16)],
        compiler_params=_cparams(("arbitrary",)), name="ple_mix_bwd")(x, p, dy, dx_ffn, z, pr, pr, wpp, lg)


def _loss_grad(y, target):
    T = y.shape[0]

    def body(y_ref, t_ref, dy_ref, l_ref):
        @pl.when(pl.program_id(0) == 0)
        def _():
            l_ref[...] = jnp.zeros_like(l_ref)

        e = y_ref[...] - t_ref[...]
        dy_ref[...] = e * (1.0 / D)
        l_ref[...] += 0.5 * jnp.sum(jnp.sum(e * e, -1, keepdims=True) * (1.0 / D), 0, keepdims=True)

    row = pl.BlockSpec((TM, D), lambda i: (i, 0))
    return pl.pallas_call(
        body, grid=(T // TM,), in_specs=[row, row], out_specs=[row, _full_spec((1, 1))],
        out_shape=[jax.ShapeDtypeStruct((T, D), f32), jax.ShapeDtypeStruct((1, 1), f32)],
        compiler_params=_cparams(("arbitrary",)), name="loss_grad")(y, target)


def _split_heads(x, H):
    n = x.shape[0] // CH
    parts = [x[:, h * CH:(h + 1) * CH].reshape(n, CH, CH) for h in range(H)]
    return jnp.stack(parts, axis=1).reshape(n * H, CH, CH)


def _merge_heads(ref, x, H, col0=0):
    n = x.shape[0] // H
    x4 = x.reshape(n, H, CH, CH)
    for h in range(H):
        ref[:, col0 + h * CH:col0 + (h + 1) * CH] = x4[:, h].reshape(n * CH, CH)


def _rows_down(x, before, s):
    r8 = lax.broadcasted_iota(jnp.int32, before.shape, 0)
    top = jnp.where(r8 < s, pltpu.roll(before, s, 0), pltpu.roll(x[0:8], s, 0))
    return jnp.concatenate([top, pltpu.roll(x, s, 0)[8:]], axis=0)


def _rows_up(x, after, s):
    R = x.shape[0]
    r8 = lax.broadcasted_iota(jnp.int32, after.shape, 0)
    bottom = jnp.where(r8 >= 8 - s, pltpu.roll(after, 8 - s, 0), pltpu.roll(x[R - 8:R], 8 - s, 0))
    return jnp.concatenate([pltpu.roll(x, R - s, 0)[0:R - 8], bottom], axis=0)


def _conv_fwd(ext, x, tail, w, R):
    ext[0:8, :] = tail
    ext[8:R + 8, :] = x
    y = w[3:4, :] * x
    for k in range(3):
        y = y + w[k:k + 1, :] * _rows_down(x, tail, 3 - k)
    return y


def _conv_bwd(ext, dy, dy_next, w, R):
    x, tail = ext[8:8 + R, :], ext[0:8, :]
    dx = w[3:4, :] * dy
    dws = []
    for k in range(3):
        dx = dx + w[k:k + 1, :] * _rows_up(dy, dy_next, 3 - k)
        dws.append(jnp.sum(dy * _rows_down(x, tail, 3 - k), 0, keepdims=True))
    dws.append(jnp.sum(dy * x, 0, keepdims=True))
    return dx, jnp.concatenate(dws, axis=0)


def _prev_tail_spec(R, W):
    return pl.BlockSpec((8, W), lambda i: (jnp.maximum(i * (R // 8) - 1, 0), 0))


def _prev_tail_spec_rev(R, W, nb):
    return pl.BlockSpec((8, W), lambda i: (jnp.maximum((nb - 1 - i) * (R // 8) - 1, 0), 0))


def _rope_tables(positions):
    T = positions.shape[0]

    def body(p_ref, c_ref, s_ref):
        lane = lax.broadcasted_iota(jnp.int32, (TM, RET_W), 1)
        fi = (lane % 32).astype(f32)
        inv = jnp.exp(fi * (-math.log(ROPE_THETA) / 32.0))
        ang = p_ref[...].astype(f32) * inv
        c_ref[...] = jnp.cos(ang)
        s_ref[...] = jnp.where(lane % CH < 32, -jnp.sin(ang), jnp.sin(ang))

    row = pl.BlockSpec((TM, RET_W), lambda i: (i, 0))
    return pl.pallas_call(
        body, grid=(T // TM,), in_specs=[pl.BlockSpec((TM, 1), lambda i: (i, 0))], out_specs=[row, row],
        out_shape=[jax.ShapeDtypeStruct((T, RET_W), f32)] * 2,
        compiler_params=_cparams(("arbitrary",)), name="rope_tables")(positions)


def _partner(x):
    lane = lax.broadcasted_iota(jnp.int32, x.shape, 1)
    return jnp.where(lane % CH < 32, pltpu.roll(x, RET_W - 32, 1), pltpu.roll(x, 32, 1))


def _ret_consts():
    ii = lax.broadcasted_iota(jnp.int32, (CH, CH), 0).astype(f32)
    jj = lax.broadcasted_iota(jnp.int32, (CH, CH), 1).astype(f32)
    intra, cross, tail, cd = [], [], [], []
    for h in range(RET_H):
        lg = math.log1p(-(2.0 ** (-5.0 - h)))
        intra.append(jnp.exp(jnp.abs(ii - jj) * lg))
        cross.append(jnp.exp((ii + 1.0) * lg))
        tail.append(jnp.exp((CH - 1.0 - ii) * lg))
        cd.append(jnp.full((CH, CH), math.exp(CH * lg), f32))
    return jnp.stack(intra), jnp.stack(cross), jnp.stack(tail), jnp.stack(cd)


def _ret_local(consts, q, k, v):
    n = q.shape[0] // RET_H
    intra, cross, tail = (jnp.tile(c, (n, 1, 1)) for c in consts[:3])
    s = _bmm('bid,bjd->bij', q, k) * intra
    return _bmm('bij,bje->bie', s, v), _bmm('bjd,bje->bde', k * tail, v), q * cross


def _ret_step(cd, st, o_intra, kv, qc):
    o = o_intra + _bmm('hid,hde->hie', qc, st)
    oc = o - jnp.mean(o, -1, keepdims=True)
    return oc * lax.rsqrt(jnp.mean(oc * oc, -1, keepdims=True) + 1e-5), st * cd + kv


def _ret_fwd(hr, cosw, sinw, gam):
    T = hr.shape[0]
    R = RB_RET
    nc = R // CH

    def body(h_ref, c_ref, s_ref, g_ref, o_ref, st_ref, st, wide):
        @pl.when(pl.program_id(0) == 0)
        def _():
            st[...] = jnp.zeros_like(st)

        consts = _ret_consts()
        cw, sw = c_ref[...], s_ref[...]
        q, k = h_ref[:, 0:RET_W], h_ref[:, RET_W:2 * RET_W]
        qh = _split_heads((q * cw + _partner(q) * sw) * 0.125, RET_H)
        kh = _split_heads(k * cw + _partner(k) * sw, RET_H)
        vh = _split_heads(h_ref[:, 2 * RET_W:3 * RET_W], RET_H)
        loc = _ret_local(consts, qh, kh, vh)
        outs = []
        s_cur = st[...]
        for c in range(nc):
            sl = slice(c * RET_H, (c + 1) * RET_H)
            st_ref[c] = s_cur
            on, s_cur = _ret_step(consts[3], s_cur, *(t[sl] for t in loc))
            outs.append(on)
        st[...] = s_cur
        _merge_heads(wide, jnp.concatenate(outs, axis=0), RET_H)
        o_ref[...] = wide[...] * g_ref[...] * _silu(h_ref[:, 3 * RET_W:4 * RET_W])

    blk = pl.BlockSpec((R, RET_W), lambda i: (i, 0))
    return pl.pallas_call(
        body, grid=(T // R,),
        in_specs=[pl.BlockSpec((R, D), lambda i: (i, 0)), blk, blk, _full_spec((1, RET_W))],
        out_specs=[blk, pl.BlockSpec((nc, RET_H, CH, CH), lambda i: (i, 0, 0, 0))],
        out_shape=[jax.ShapeDtypeStruct((T, RET_W), f32), jax.ShapeDtypeStruct((T // CH, RET_H, CH, CH), f32)],
        scratch_shapes=[pltpu.VMEM((RET_H, CH, CH), f32), pltpu.VMEM((R, RET_W), f32)],
        compiler_params=_cparams(("arbitrary",)), name="ret_fwd")(hr, cosw, sinw, gam)


def _ret_bwd(hr, cosw, sinw, gam, states, dout):
    T = hr.shape[0]
    R = RB_RET
    nc = R // CH
    nb = T // R

    def body(h_ref, c_ref, s_ref, g_ref, st_ref, do_ref, dh_ref, dgam_ref, dst, wide):
        @pl.when(pl.program_id(0) == 0)
        def _():
            dst[...] = jnp.zeros_like(dst)
            dgam_ref[...] = jnp.zeros_like(dgam_ref)

        consts = _ret_consts()
        cw, sw = c_ref[...], s_ref[...]
        q, k = h_ref[:, 0:RET_W], h_ref[:, RET_W:2 * RET_W]
        gr = h_ref[:, 3 * RET_W:4 * RET_W]
        qh = _split_heads((q * cw + _partner(q) * sw) * 0.125, RET_H)
        kh = _split_heads(k * cw + _partner(k) * sw, RET_H)
        vh = _split_heads(h_ref[:, 2 * RET_W:3 * RET_W], RET_H)
        do = do_ref[...]
        gam = g_ref[...]
        sg = _silu(gr)
        don = _split_heads(do * gam * sg, RET_H)
        loc, vjp_loc = jax.vjp(functools.partial(_ret_local, consts), qh, kh, vh)
        ons, dloc = [None] * nc, [[None] * nc for _ in range(3)]
        ds = dst[...]
        for c in reversed(range(nc)):
            sl = slice(c * RET_H, (c + 1) * RET_H)
            (on, _), vjp = jax.vjp(functools.partial(_ret_step, consts[3]), st_ref[c], *(t[sl] for t in loc))
            ds, dloc[0][c], dloc[1][c], dloc[2][c] = vjp((don[sl], ds))
            ons[c] = on
        dst[...] = ds
        dqs, dks, dvs = ([g] for g in vjp_loc(tuple(jnp.concatenate(d, axis=0) for d in dloc)))
        _merge_heads(wide, jnp.concatenate(ons, axis=0), RET_H)
        onw = wide[...]
        dgam_ref[...] += jnp.sum(do * onw * sg, 0, keepdims=True)
        dh_ref[:, 3 * RET_W:4 * RET_W] = (do * onw * gam * _dsilu(gr)).astype(bf16)
        _merge_heads(wide, jnp.concatenate(dqs, axis=0), RET_H)
        u = wide[...] * 0.125
        dh_ref[:, 0:RET_W] = (u * cw + _partner(u * sw)).astype(bf16)
        _merge_heads(wide, jnp.concatenate(dks, axis=0), RET_H)
        u = wide[...]
        dh_ref[:, RET_W:2 * RET_W] = (u * cw + _partner(u * sw)).astype(bf16)
        _merge_heads(wide, jnp.concatenate(dvs, axis=0), RET_H)
        dh_ref[:, 2 * RET_W:3 * RET_W] = wide[...].astype(bf16)

    blk = pl.BlockSpec((R, RET_W), lambda i: (nb - 1 - i, 0))
    return pl.pallas_call(
        body, grid=(nb,),
        in_specs=[pl.BlockSpec((R, D), lambda i: (nb - 1 - i, 0)), blk, blk, _full_spec((1, RET_W)),
                  pl.BlockSpec((nc, RET_H, CH, CH), lambda i: (nb - 1 - i, 0, 0, 0)), blk],
        out_specs=[pl.BlockSpec((R, D), lambda i: (nb - 1 - i, 0)), _full_spec((1, RET_W))],
        out_shape=[jax.ShapeDtypeStruct((T, D), bf16), jax.ShapeDtypeStruct((1, RET_W), f32)],
        scratch_shapes=[pltpu.VMEM((RET_H, CH, CH), f32), pltpu.VMEM((R, RET_W), f32)],
        compiler_params=_cparams(("arbitrary",)), name="ret_bwd")(hr, cosw, sinw, gam, states, dout)


def _lru_ab(xc, wa, ba, wx, bx, lam):
    r = _sigmoid(_dot(xc, wa) + ba)
    i = _sigmoid(_dot(xc, wx) + bx)
    la = 8.0 * r * (-_softplus(-lam))
    a = jnp.exp(la)
    em = jnp.tanh(la) * (jnp.exp(2.0 * la) + 1.0)
    return a, jnp.sqrt(-em) * (i * xc)


def _lru_out(h, gate):
    return h * _gelu(gate)


def _scan_fwd(a, b):
    R = a.shape[0]
    row = lax.broadcasted_iota(jnp.int32, a.shape, 0)
    d = 1
    while d < R:
        m = row >= d
        b = jnp.where(m, a * pltpu.roll(b, d, 0) + b, b)
        a = jnp.where(m, a * pltpu.roll(a, d, 0), a)
        d *= 2
    return a, b


def _scan_bwd(a, b):
    R = a.shape[0]
    row = lax.broadcasted_iota(jnp.int32, a.shape, 0)
    d = 1
    while d < R:
        m = row < R - d
        b = jnp.where(m, a * pltpu.roll(b, R - d, 0) + b, b)
        a = jnp.where(m, a * pltpu.roll(a, R - d, 0), a)
        d *= 2
    return b


def _lru_fwd(hl, cw, cb, wa, ba, wx, bx, lam):
    T = hl.shape[0]
    R = RB_LRU
    W = LRU_W

    def body(h_ref, t_ref, cw_ref, cb_ref, wa_ref, ba_ref, wx_ref, bx_ref, lam_ref, o_ref, hs_ref, carry, ext):
        first = pl.program_id(0) == 0

        @pl.when(first)
        def _():
            carry[...] = jnp.zeros_like(carry)

        tail = jnp.where(first, 0.0, t_ref[:, 0:W])
        xc = _conv_fwd(ext, h_ref[:, 0:W], tail, cw_ref[...], R) + cb_ref[...]
        a, b = _lru_ab(xc, wa_ref[...], ba_ref[...], wx_ref[...], bx_ref[...], lam_ref[...])
        ap, hloc = _scan_fwd(a, b)
        h = hloc + ap * carry[0:1, :]
        carry[...] = jnp.broadcast_to(h[R - 1:R, :], carry.shape)
        hs_ref[...] = h
        o_ref[...] = _lru_out(h, h_ref[:, W:2 * W])

    vec = _full_spec((1, W))
    blk = pl.BlockSpec((R, W), lambda i: (i, 0))
    return pl.pallas_call(
        body, grid=(T // R,),
        in_specs=[pl.BlockSpec((R, 2 * W), lambda i: (i, 0)), _prev_tail_spec(R, 2 * W), _full_spec((4, W)), vec,
                  _full_spec((W, W)), vec, _full_spec((W, W)), vec, vec],
        out_specs=[blk, blk], out_shape=[jax.ShapeDtypeStruct((T, W), f32)] * 2,
        scratch_shapes=[pltpu.VMEM((8, W), f32), pltpu.VMEM((R + 8, W), f32)],
        compiler_params=_cparams(("arbitrary",)), name="lru_fwd")(hl, hl, cw, cb, wa, ba, wx, bx, lam)


def _lru_bwd(hl, hs, cw, cb, wa, ba, wx, bx, lam, dout):
    T = hl.shape[0]
    R = RB_LRU
    W = LRU_W
    nb = T // R

    def body(h_ref, t_ref, hs_ref, hst_ref, cw_ref, cb_ref, wa_ref, ba_ref, wx_ref, bx_ref, lam_ref, do_ref,
             dh_ref, dcw_ref, dcb_ref, dwa_ref, dba_ref, dwx_ref, dbx_ref, dlam_ref, carry_g, carry_dy, ext):
        i = pl.program_id(0)
        last_blk = i == 0
        first_blk = i == nb - 1

        @pl.when(last_blk)
        def _():
            carry_g[...] = jnp.zeros_like(carry_g)
            carry_dy[...] = jnp.zeros_like(carry_dy)
            for r in (dcw_ref, dcb_ref, dwa_ref, dba_ref, dwx_ref, dbx_ref, dlam_ref):
                r[...] = jnp.zeros_like(r)

        tail = jnp.where(first_blk, 0.0, t_ref[:, 0:W])
        xc = _conv_fwd(ext, h_ref[:, 0:W], tail, cw_ref[...], R) + cb_ref[...]
        (a, _), vjp_ab = jax.vjp(_lru_ab, xc, wa_ref[...], ba_ref[...], wx_ref[...], bx_ref[...], lam_ref[...])
        hs = hs_ref[...]
        _, vjp_out = jax.vjp(_lru_out, hs, h_ref[:, W:2 * W])
        dh, dgate = vjp_out(do_ref[...])
        row = lax.broadcasted_iota(jnp.int32, (R, W), 0)
        dh = jnp.where(row == R - 1, dh + carry_g[0:1, :], dh)
        a_up = jnp.where(row == R - 1, 0.0, pltpu.roll(a, R - 1, 0))
        g = _scan_bwd(a_up, dh)
        carry_g[...] = jnp.broadcast_to(a[0:1, :] * g[0:1, :], carry_g.shape)
        hprev0 = jnp.where(first_blk, 0.0, hst_ref[7:8, :])
        hprev = jnp.where(row == 0, hprev0, pltpu.roll(hs, 1, 0))
        dxc, dwa, dba, dwx, dbx, dlam = vjp_ab((g * hprev, g))
        dwa_ref[...] += dwa
        dba_ref[...] += dba
        dwx_ref[...] += dwx
        dbx_ref[...] += dbx
        dlam_ref[...] += dlam
        dcb_ref[...] += jnp.sum(dxc, 0, keepdims=True)
        dx, dcw = _conv_bwd(ext, dxc, carry_dy[...], cw_ref[...], R)
        carry_dy[...] = dxc[0:8, :]
        dcw_ref[...] += dcw
        dh_ref[:, 0:W] = dx.astype(bf16)
        dh_ref[:, W:2 * W] = dgate.astype(bf16)

    vec = _full_spec((1, W))
    mat = _full_spec((W, W))
    blk = pl.BlockSpec((R, W), lambda i: (nb - 1 - i, 0))
    blk2 = pl.BlockSpec((R, 2 * W), lambda i: (nb - 1 - i, 0))
    return pl.pallas_call(
        body, grid=(nb,),
        in_specs=[blk2, _prev_tail_spec_rev(R, 2 * W, nb), blk, _prev_tail_spec_rev(R, W, nb), _full_spec((4, W)), vec,
                  mat, vec, mat, vec, vec, blk],
        out_specs=[blk2, _full_spec((4, W)), vec, mat, vec, mat, vec, vec],
        out_shape=[jax.ShapeDtypeStruct((T, 2 * W), bf16), jax.ShapeDtypeStruct((4, W), f32),
                   jax.ShapeDtypeStruct((1, W), f32), jax.ShapeDtypeStruct((W, W), f32),
                   jax.ShapeDtypeStruct((1, W), f32), jax.ShapeDtypeStruct((W, W), f32),
                   jax.ShapeDtypeStruct((1, W), f32), jax.ShapeDtypeStruct((1, W), f32)],
        scratch_shapes=[pltpu.VMEM((8, W), f32), pltpu.VMEM((8, W), f32), pltpu.VMEM((R + 8, W), f32)],
        compiler_params=_cparams(("arbitrary",)), name="lru_bwd")(hl, hl, hs, hs, cw, cb, wa, ba, wx, bx, lam, dout)


def _head_ones():
    i = lax.broadcasted_iota(jnp.int32, (GDN_W, GDN_W), 0)
    j = lax.broadcasted_iota(jnp.int32, (GDN_W, GDN_W), 1)
    return jnp.where(jnp.bitwise_xor(i, j) < CH, 1.0, 0.0).astype(bf16)


def _head_sums(x, ones):
    return jnp.dot(x.astype(bf16), ones, preferred_element_type=f32)


def _l2n(y, ones):
    r = lax.rsqrt(_head_sums(y * y, ones) + 1e-6)
    return y * r, r


def _l2n_bwd(dn, n, r, ones):
    return r * (dn - n * _head_sums(dn * n, ones))


def _gdn_local(inverse, q, k, vs, gc, bb):
    B = q.shape[0]
    ii = lax.broadcasted_iota(jnp.int32, (B, CH, CH), 1)
    jj = lax.broadcasted_iota(jnp.int32, (B, CH, CH), 2)
    gct = jnp.swapaxes(gc, 1, 2)
    decay = jnp.where(ii >= jj, jnp.exp(jnp.minimum(gc - gct, 0.0)), 0.0)
    kk = _bmm('bid,bjd->bij', k, k)
    inv = inverse(-jnp.where(ii > jj, bb * kk * decay, 0.0))
    egc = jnp.exp(gc)
    u = _bmm('bij,bje->bie', inv, vs * bb)
    w = _bmm('bij,bje->bie', inv, k * (bb * egc))
    qk = _bmm('bid,bjd->bij', q, k) * (0.125 * decay)
    glast = gc[:, CH - 1:CH, :]
    return u, w, qk, q * (0.125 * egc), k * jnp.exp(glast - gc), jnp.exp(jnp.broadcast_to(glast, gc.shape))


def _gdn_step(st, u, w, qk, qd, kt, egl, z, gn):
    vnew = u - _bmm('hcd,hde->hce', w, st)
    o = _bmm('hcd,hde->hce', qd, st) + _bmm('hij,hje->hie', qk, vnew)
    st2 = st * egl + _bmm('hcd,hce->hde', kt, vnew)
    out = o * lax.rsqrt(_rowsum(o * o) * (1.0 / CH) + 1e-6) * gn * _silu(z)
    return out, st2


def _gdn_scalars(ab, alog, dtb):
    sp = _softplus(ab + dtb)
    return -jnp.exp(alog) * sp, _sigmoid(ab)


def _bcast_heads(blk, lane0, H):
    R = blk.shape[0]
    n = R // CH
    parts = [jnp.broadcast_to(blk[:, lane0 + h:lane0 + h + 1], (R, CH)).reshape(n, CH, CH) for h in range(H)]
    return jnp.stack(parts, axis=1).reshape(n * H, CH, CH)


def _unbcast_heads(x, lane0, H):
    n = x.shape[0] // H
    R = n * CH
    s = jnp.sum(x, axis=2, keepdims=True).reshape(n, H, CH, 1)
    lane = lax.broadcasted_iota(jnp.int32, (R, 128), 1)
    acc = jnp.zeros((R, 128), f32)
    for h in range(H):
        acc = acc + jnp.where(lane == lane0 + h, jnp.broadcast_to(s[:, h].reshape(R, 1), (R, 128)), 0.0)
    return acc


def _gdn_fwd(hg, cw, alog, dtb, gn):
    T = hg.shape[0]
    R = RB_GDN
    nc = R // CH
    W3 = 3 * GDN_W
    H = GDN_H

    def body(h_ref, t_ref, cw_ref, al_ref, dt_ref, gn_ref, o_ref, st_ref, inv_ref, st, ext):
        first = pl.program_id(0) == 0

        @pl.when(first)
        def _():
            st[...] = jnp.zeros_like(st)

        def inverse(m):
            inv = _neumann_inv(m)
            inv_ref[...] = inv
            return inv

        tail = jnp.where(first, 0.0, t_ref[:, 0:W3])
        y = _silu(_conv_fwd(ext, h_ref[:, 0:W3], tail, cw_ref[...], R))
        ones = _head_ones()
        qs = _split_heads(_l2n(y[:, 0:GDN_W], ones)[0], H)
        ks = _split_heads(_l2n(y[:, GDN_W:2 * GDN_W], ones)[0], H)
        vs = _split_heads(y[:, 2 * GDN_W:W3], H)
        zh = _split_heads(h_ref[:, W3:W3 + GDN_W], H)
        g, beta = _gdn_scalars(h_ref[:, W3 + GDN_W:GDN_IN], al_ref[...], dt_ref[...])
        loc = _gdn_local(inverse, qs, ks, vs, _bcast_heads(_chunk_cumsum(g), 0, H), _bcast_heads(beta, H, H))
        gnv = gn_ref[...]
        outs = []
        s_cur = st[...]
        for c in range(nc):
            sl = slice(c * H, (c + 1) * H)
            st_ref[c] = s_cur
            out, s_cur = _gdn_step(s_cur, *(t[sl] for t in loc), zh[sl], gnv)
            outs.append(out)
        st[...] = s_cur
        _merge_heads(o_ref, jnp.concatenate(outs, axis=0), H)

    return pl.pallas_call(
        body, grid=(T // R,),
        in_specs=[pl.BlockSpec((R, GDN_IN), lambda i: (i, 0)), _prev_tail_spec(R, GDN_IN), _full_spec((4, W3)),
                  _full_spec((1, 128)), _full_spec((1, 128)), _full_spec((1, CH))],
        out_specs=[pl.BlockSpec((R, GDN_W), lambda i: (i, 0)), pl.BlockSpec((nc, H, CH, CH), lambda i: (i, 0, 0, 0)),
                   pl.BlockSpec((nc * H, CH, CH), lambda i: (i, 0, 0))],
        out_shape=[jax.ShapeDtypeStruct((T, GDN_W), f32), jax.ShapeDtypeStruct((T // CH, H, CH, CH), f32),
                   jax.ShapeDtypeStruct((T // CH * H, CH, CH), f32)],
        scratch_shapes=[pltpu.VMEM((H, CH, CH), f32), pltpu.VMEM((R + 8, W3), f32)],
        compiler_params=_cparams(("arbitrary",)), name="gdn_fwd")(hg, hg, cw, alog, dtb, gn)


def _gdn_bwd(hg, cw, alog, dtb, gn, states, invs, dout):
    T = hg.shape[0]
    R = RB_GDN
    nc = R // CH
    nb = T // R
    W3 = 3 * GDN_W
    H = GDN_H

    def body(h_ref, t_ref, cw_ref, al_ref, dt_ref, gn_ref, st_ref, inv_ref, do_ref,
             dh_ref, dcw_ref, dal_ref, ddt_ref, dgn_ref, dst, carry_dy, ext, wide):
        i = pl.program_id(0)
        first_blk = i == nb - 1

        @pl.when(i == 0)
        def _():
            dst[...] = jnp.zeros_like(dst)
            carry_dy[...] = jnp.zeros_like(carry_dy)
            for r in (dcw_ref, dal_ref, ddt_ref, dgn_ref):
                r[...] = jnp.zeros_like(r)

        tail = jnp.where(first_blk, 0.0, t_ref[:, 0:W3])
        ypre = _conv_fwd(ext, h_ref[:, 0:W3], tail, cw_ref[...], R)
        y = _silu(ypre)
        ones = _head_ones()
        qn, rq = _l2n(y[:, 0:GDN_W], ones)
        kn, rk = _l2n(y[:, GDN_W:2 * GDN_W], ones)
        qs, ks, vs = _split_heads(qn, H), _split_heads(kn, H), _split_heads(y[:, 2 * GDN_W:W3], H)
        zh = _split_heads(h_ref[:, W3:W3 + GDN_W], H)
        ab = h_ref[:, W3 + GDN_W:GDN_IN]
        alog, dtb = al_ref[...], dt_ref[...]
        g, beta = _gdn_scalars(ab, alog, dtb)
        kept = inv_ref[...]
        loc, vjp_loc = jax.vjp(functools.partial(_gdn_local, lambda m: _known_inv(m, kept)), qs, ks, vs,
                               _bcast_heads(_chunk_cumsum(g), 0, H), _bcast_heads(beta, H, H))
        doh = _split_heads(do_ref[...], H)
        gnv = gn_ref[...]
        dloc = [[None] * nc for _ in range(6)]
        dzs = [None] * nc
        ds = dst[...]
        dgn = jnp.zeros((1, CH), f32)
        for c in reversed(range(nc)):
            sl = slice(c * H, (c + 1) * H)
            _, vjp = jax.vjp(_gdn_step, st_ref[c], *(t[sl] for t in loc), zh[sl], gnv)
            grads = vjp((doh[sl], ds))
            ds = grads[0]
            for j in range(6):
                dloc[j][c] = grads[1 + j]
            dzs[c] = grads[7]
            dgn = dgn + grads[8]
        dst[...] = ds
        dgn_ref[...] += dgn
        dqs, dks, dvs, dgb, dbb = vjp_loc(tuple(jnp.concatenate(d, axis=0) for d in dloc))
        lane = lax.broadcasted_iota(jnp.int32, (R, 128), 1)
        dg = _chunk_cumsum(_unbcast_heads(dgb, 0, H), reverse=True)
        dbeta = _unbcast_heads(dbb, H, H)
        da = dg * (-jnp.exp(alog)) * _sigmoid(ab + dtb)
        dh_ref[:, W3 + GDN_W:GDN_IN] = jnp.where(lane < H, da, dbeta * beta * (1.0 - beta)).astype(bf16)
        ddt_ref[...] += jnp.sum(jnp.where(lane < H, da, 0.0), 0, keepdims=True)
        dal_ref[...] += jnp.sum(jnp.where(lane < H, dg * g, 0.0), 0, keepdims=True)
        for j, dpart in enumerate((dqs, dks, dvs)):
            _merge_heads(wide, dpart, H, col0=j * GDN_W)
        wide[:, 0:GDN_W] = _l2n_bwd(wide[:, 0:GDN_W], qn, rq, ones)
        wide[:, GDN_W:2 * GDN_W] = _l2n_bwd(wide[:, GDN_W:2 * GDN_W], kn, rk, ones)
        dy = wide[...] * _dsilu(ypre)
        dx, dcw = _conv_bwd(ext, dy, carry_dy[...], cw_ref[...], R)
        carry_dy[...] = dy[0:8, :]
        dcw_ref[...] += dcw
        dh_ref[:, 0:W3] = dx.astype(bf16)
        _merge_heads(wide, jnp.concatenate(dzs, axis=0), H)
        dh_ref[:, W3:W3 + GDN_W] = wide[:, 0:GDN_W].astype(bf16)

    blk = pl.BlockSpec((R, GDN_IN), lambda i: (nb - 1 - i, 0))
    return pl.pallas_call(
        body, grid=(nb,),
        in_specs=[blk, _prev_tail_spec_rev(R, GDN_IN, nb), _full_spec((4, W3)), _full_spec((1, 128)),
                  _full_spec((1, 128)), _full_spec((1, CH)),
                  pl.BlockSpec((nc, H, CH, CH), lambda i: (nb - 1 - i, 0, 0, 0)),
                  pl.BlockSpec((nc * H, CH, CH), lambda i: (nb - 1 - i, 0, 0)),
                  pl.BlockSpec((R, GDN_W), lambda i: (nb - 1 - i, 0))],
        out_specs=[blk, _full_spec((4, W3)), _full_spec((1, 128)), _full_spec((1, 128)), _full_spec((1, CH))],
        out_shape=[jax.ShapeDtypeStruct((T, GDN_IN), bf16), jax.ShapeDtypeStruct((4, W3), f32),
                   jax.ShapeDtypeStruct((1, 128), f32), jax.ShapeDtypeStruct((1, 128), f32),
                   jax.ShapeDtypeStruct((1, CH), f32)],
        scratch_shapes=[pltpu.VMEM((H, CH, CH), f32), pltpu.VMEM((8, W3), f32), pltpu.VMEM((R + 8, W3), f32),
                        pltpu.VMEM((R, W3), f32)],
        compiler_params=_cparams(("arbitrary",)), name="gdn_bwd")(hg, hg, cw, alog, dtb, gn, states, invs, dout)


def _block_diag(w):
    out = jnp.zeros((LRU_W, LRU_W), w.dtype)
    for g in range(w.shape[0]):
        out = lax.dynamic_update_slice(out, w[g], (g * CH, g * CH))
    return out


def _block_diag_t(w):
    return jnp.stack([w[g * CH:(g + 1) * CH, g * CH:(g + 1) * CH] for g in range(LRU_W // CH)])


def _pad_lanes(v, n=128):
    return jnp.pad(v, (0, n - v.shape[0]))[None, :]


def _local_step(x, p, positions, target, fetch, emit, sm):
    cosw, sinw = _rope_tables(positions)
    cols = lambda w: jnp.transpose(w, (1, 2, 0, 3)).reshape(-1, D, NDEV * FSP)
    saved = []
    h = x
    for l in range(DEPTH):
        v = lambda n: sm[n][l][None, :]
        F1, tok = fetch(l, 'f1', h)
        p384a, pda = cols(F1['p384']), F1['pd']
        z1, x1, g1, u1, a1 = _ffn_fwd(h, p384a, pda, v('ln_ffn1_g') + tok, v('ln_ffn1_b'), 0, 0)
        G, _ = fetch(l, 'rest', x1)
        p384, pd, pr, pinl, ping, wpp = cols(G['p384']), G['pd'], G['pr'], G['pinl'], G['ping'], G['wpp']
        wts = (p384a, pda, p384, pd, pr, pinl, ping, wpp)
        hr, hl, hg = _proj_in(x1, pr, pinl, ping, 0)
        o_r, rst = _ret_fwd(hr, cosw, sinw, v('ret_norm_g'))
        lru_args = (sm['lru_conv_w'][l], v('lru_conv_b'), _block_diag(sm['lru_w_a'][l]), v('lru_b_a'),
                    _block_diag(sm['lru_w_x'][l]), v('lru_b_x'), v('lru_lambda'))
        o_l, hs = _lru_fwd(hl, *lru_args)
        gdn_args = (sm['gdn_conv_w'][l], _pad_lanes(sm['gdn_a_log'][l]), _pad_lanes(sm['gdn_dt_bias'][l]),
                    v('gdn_norm_g'))
        o_g, *gst = _gdn_fwd(hg, *gdn_args)
        z2, x2 = _mix_out(x1, o_r, o_l, o_g, pr, v('ln_mix_g'), v('ln_mix_b'), 0)
        z3, x3, g2, u2, a2 = _ffn_fwd(x2, p384, pd, v('ln_ffn2_g'), v('ln_ffn2_b'), 0, 1, ple=(p[l], pr, wpp))
        saved.append((h, z1, x1, hr, hl, hg, o_r, rst, o_l, hs, lru_args, o_g, gst, gdn_args, z2, x2, z3,
                      g1, u1, a1, g2, u2, a2, wts))
        h = x3
    d, loss = _loss_grad(h, target)

    small = {n: [None] * DEPTH for n in SMALL}
    tok = 0.0
    for l in reversed(range(DEPTH)):
        (x0, z1, x1, hr, hl, hg, o_r, rst, o_l, hs, lru_args, o_g, gst, gdn_args, z2, x2, z3,
         g1, u1, a1, g2, u2, a2, wts) = saved[l]
        p384a, pda, p384, pd, pr, pinl, ping, wpp = wts
        v = lambda n: sm[n][l][None, :]
        rows = lambda m: m.reshape(NDEV, m.shape[1] // NDEV, m.shape[2])
        d2, dg2, du2, dy2, small['ln_ffn2_g'][l], small['ln_ffn2_b'][l] = _ffn_bwd(
            z3, d, g2, u2, p384, pd, v('ln_ffn2_g') + tok, 0, 1)
        dxb, dzb, do_r, do_l, do_g, small['ln_mix_g'][l], small['ln_mix_b'][l], dgp, dpj = _ple_mix_bwd(
            x2, p[l], dy2, d2, z2, pr, wpp, v('ln_mix_g'))
        dhr, small['ret_norm_g'][l] = _ret_bwd(hr, cosw, sinw, v('ret_norm_g'), rst, do_r)
        (dhl, small['lru_conv_w'][l], small['lru_conv_b'][l], dwa, small['lru_b_a'][l], dwx, small['lru_b_x'][l],
         small['lru_lambda'][l]) = _lru_bwd(hl, hs, *lru_args, do_l)
        small['lru_w_a'][l], small['lru_w_x'][l] = _block_diag_t(dwa), _block_diag_t(dwx)
        dhg, small['gdn_conv_w'][l], dal, ddt, small['gdn_norm_g'][l] = _gdn_bwd(hg, *gdn_args, *gst, do_g)
        small['gdn_a_log'][l], small['gdn_dt_bias'][l] = dal[:, 0:GDN_H], ddt[:, 0:GDN_H]
        d1 = _proj_in_bwd(dxb, dhr, dhl, dhg, pr, pinl, ping, 0)
        dwo = jnp.concatenate([_matmul_tn(o_r, dzb, D, "dw_out_r"), _matmul_tn(o_l, dzb, D, "dw_out_l"),
                               _matmul_tn(o_g, dzb, D, "dw_out_g")], axis=1)
        tok = emit(l, 'rest', {
            'p384': jnp.stack([_matmul_tn(x2, dg2, FSP, "dw_gate", NDEV), _matmul_tn(x2, du2, FSP, "dw_up", NDEV)],
                              axis=1),
            'pd': rows(_matmul_tn(a2, dy2, D, "dw_down"))[:, None],
            'pr': jnp.stack([rows(_matmul_tn(x1, dhr, D, "dw_in_r")), rows(dwo),
                             rows(_matmul_tn(x2, dgp, D, "dw_ple_gate"))], axis=1),
            'pinl': rows(_matmul_tn(x1, dhl, 2 * LRU_W, "dw_in_l"))[:, None],
            'ping': rows(_matmul_tn(x1, dhg, GDN_IN, "dw_in_g"))[:, None],
            'ppp': jnp.transpose(_matmul_tn(p[l], dpj, D, "dw_ple_proj").reshape(PLE, NDEV, 128), (1, 0, 2))[:, None]})
        d, dg1, du1, dy1, small['ln_ffn1_g'][l], small['ln_ffn1_b'][l] = _ffn_bwd(
            z1, d1, g1, u1, p384a, pda, v('ln_ffn1_g') + tok, 0, 0)
        if l == 0:
            emit(l, 'small', {n: jnp.stack([g.reshape(sm[n].shape[1:]) for g in gs]) for n, gs in small.items()})
        tok = emit(l, 'f1', {
            'p384': jnp.stack([_matmul_tn(x0, dg1, FSP, "dw_gate", NDEV), _matmul_tn(x0, du1, FSP, "dw_up", NDEV)],
                              axis=1),
            'pd': rows(_matmul_tn(a1, dy1, D, "dw_down"))[:, None]})
    return loss, d


def _pack_big(ws, dtype=bf16):
    padc = lambda a, n: jnp.pad(a, ((0, 0), (0, 0), (0, n - a.shape[2])))
    padr = lambda a, n: jnp.pad(a, ((0, 0), (0, n - a.shape[1]), (0, 0)))
    per_layer = lambda arrs: jnp.stack(arrs, axis=1).reshape((-1,) + arrs[0].shape[1:])
    w_in = ws['w_in']
    out = {
        'p384': per_layer([padc(ws[n], FSP) for n in ('ffn1_w_gate', 'ffn1_w_up', 'ffn2_w_gate', 'ffn2_w_up')]),
        'pd': per_layer([padr(ws[n], FSP) for n in ('ffn1_w_down', 'ffn2_w_down')]),
        'pr': per_layer([w_in[:, :, 0:D], ws['w_out'], ws['ple_w_gate']]),
        'pinl': w_in[:, :, D:D + 2 * LRU_W],
        'ping': padc(w_in[:, :, D + 2 * LRU_W:D_IN], GDN_IN),
        'ppp': ws['ple_w_proj'],
    }
    return {k: a.astype(dtype) for k, a in out.items()}


def _gather_two_level(arrays, name):
    n = len(arrays)

    def body(*refs):
        ins, outs = refs[:n], refs[n:2 * n]
        send_sems, recv_sems, local_sems = refs[2 * n:]
        x, y, c = lax.axis_index("x"), lax.axis_index("y"), lax.axis_index("c")
        me, sibling = (x, y, c), (x, y, 1 - c)
        chips = [(1 - x, y), (x, 1 - y), (1 - x, 1 - y)]
        slot = lambda d: 4 * d[0] + 2 * d[1] + d[2]

        def copy(i, k, block, to, src=None):
            return pltpu.make_async_remote_copy(
                src_ref=outs[i].at[slot(block)] if src is None else src, dst_ref=outs[i].at[slot(block)],
                send_sem=send_sems.at[i, k], recv_sem=recv_sems.at[i, k], device_id=to,
                device_id_type=pl.DeviceIdType.MESH)

        mine, first, passed = [], [], []
        for i in range(n):
            cp = pltpu.make_async_copy(ins[i], outs[i].at[slot(me)], local_sems.at[i])
            cp.start()
            mine.append(cp)
            first.append(copy(i, 0, me, sibling, src=ins[i]))
            first += [copy(i, 1 + j, me, (*chip, c), src=ins[i]) for j, chip in enumerate(chips)]
        for cp in first:
            cp.start()
        for i in range(n):
            for j, chip in enumerate(chips):
                copy(i, 1 + j, (*chip, c), me).wait_recv()
                cp = copy(i, 4 + j, (*chip, c), sibling)
                cp.start()
                passed.append(cp)
        for i in range(n):
            copy(i, 0, sibling, me).wait_recv()
            for j, chip in enumerate(chips):
                copy(i, 4 + j, (*chip, 1 - c), me).wait_recv()
        for cp in first + passed:
            cp.wait_send()
        for cp in mine:
            cp.wait()

    hbm = pl.BlockSpec(memory_space=pltpu.HBM)
    return pl.pallas_call(
        body, in_specs=[hbm] * n, out_specs=[hbm] * n,
        out_shape=[jax.ShapeDtypeStruct((NDEV,) + a.shape, a.dtype) for a in arrays],
        scratch_shapes=[pltpu.SemaphoreType.DMA((n, NDEV - 1)), pltpu.SemaphoreType.DMA((n, NDEV - 1)),
                        pltpu.SemaphoreType.DMA((n,))],
        compiler_params=pltpu.CompilerParams(has_side_effects=True), name=name)(*arrays)


def _scatter_pairs(arrays, name):
    n = len(arrays)

    def body(*refs):
        ins, gots = refs[:n], refs[n:2 * n]
        send_sems, recv_sems = refs[2 * n:]
        x, y, c = lax.axis_index("x"), lax.axis_index("y"), lax.axis_index("c")
        sends = []
        for i in range(n):
            for q in range(4):
                cp = pltpu.make_async_remote_copy(
                    src_ref=ins[i].at[2 * q + 1 - c], dst_ref=gots[i].at[q], send_sem=send_sems.at[i, q],
                    recv_sem=recv_sems.at[i, q], device_id=(x, y, 1 - c), device_id_type=pl.DeviceIdType.MESH)
                cp.start()
                sends.append(cp)
        for cp in sends:
            cp.wait_recv()
        for cp in sends:
            cp.wait_send()

    hbm = pl.BlockSpec(memory_space=pltpu.HBM)
    return pl.pallas_call(
        body, in_specs=[hbm] * n, out_specs=[hbm] * n,
        out_shape=[jax.ShapeDtypeStruct((4,) + a.shape[1:], a.dtype) for a in arrays],
        scratch_shapes=[pltpu.SemaphoreType.DMA((n, 4)), pltpu.SemaphoreType.DMA((n, 4))],
        compiler_params=pltpu.CompilerParams(has_side_effects=True), name=name)(*arrays)


def _pair_sum(own, got, name):
    def body(a_ref, b_ref, o_ref):
        o_ref[...] = (a_ref[...].astype(f32) + b_ref[...].astype(f32)).astype(bf16)

    spec = pl.BlockSpec((None, None) + own.shape[2:], lambda q, s: (q, s, 0, 0))
    return pl.pallas_call(
        body, grid=own.shape[:2], in_specs=[spec, spec], out_specs=spec,
        out_shape=jax.ShapeDtypeStruct(own.shape, bf16),
        compiler_params=_cparams(("arbitrary", "arbitrary")), name=name)(own, got)


def _gather_plan(srcs, lands, x, y, c):
    me = 4 * x + 2 * y + c
    sends, arrivals = [], []
    for j in range(1, NDEV):
        peer, source = (me + j) % NDEV, (me + NDEV - j) % NDEV
        for i in range(len(srcs)):
            k = i * (NDEV - 1) + j - 1
            sends.append((srcs[i], lands[i].at[me], (peer // 4, (peer // 2) % 2, peer % 2), k))
            arrivals.append((srcs[i], lands[i].at[source], (source // 4, (source // 2) % 2, source % 2), k))
    return sends, arrivals


def _chips_plan(srcs, lands, x, y, c):
    chip = 2 * x + y
    sends, arrivals = [], []
    for j in range(1, 4):
        peer, source = (chip + j) % 4, (chip + 4 - j) % 4
        for i in range(len(srcs)):
            k = i * 3 + j - 1
            sends.append((srcs[i].at[peer], lands[i].at[chip], (peer // 2, peer % 2, c), k))
            arrivals.append((srcs[i].at[chip], lands[i].at[source], (source // 2, source % 2, c), k))
    return sends, arrivals


def _remote(entry, send_sems, recv_sems):
    src, dst, dev, k = entry
    return pltpu.make_async_remote_copy(src_ref=src, dst_ref=dst, send_sem=send_sems.at[k], recv_sem=recv_sems.at[k],
                                        device_id=dev, device_id_type=pl.DeviceIdType.MESH)


_HBM = pl.BlockSpec(memory_space=pltpu.HBM)
_SEM = pl.BlockSpec(memory_space=pltpu.SEMAPHORE)


def _split_start(arrays, land_shapes, plan, npeer, name):
    n = len(arrays)

    def body(*refs):
        srcs, lands = refs[:n], refs[n:2 * n]
        send_sems, recv_sems, token = refs[2 * n], refs[2 * n + 1], refs[-1]
        sends, _ = plan(srcs, lands, lax.axis_index("x"), lax.axis_index("y"), lax.axis_index("c"))
        for entry in sends:
            _remote(entry, send_sems, recv_sems).start()
        token[...] = jnp.zeros_like(token)

    lands = [lax.empty(s, a.dtype) for s, a in zip(land_shapes, arrays)]
    thru = [pltpu.HBM(a.shape, a.dtype) for a in arrays + lands]
    out = pl.pallas_call(
        body, name=name, in_specs=[_HBM] * (2 * n),
        out_specs=(_SEM, _SEM, *([_HBM] * (2 * n)), pl.BlockSpec(memory_space=pltpu.VMEM)),
        out_shape=(pltpu.SemaphoreType.DMA((n * npeer,)), pltpu.SemaphoreType.DMA((n * npeer,)), *thru,
                   jax.ShapeDtypeStruct((8, 128), f32)),
        input_output_aliases={i: 2 + i for i in range(2 * n)},
        compiler_params=pltpu.CompilerParams(has_side_effects=pltpu.SideEffectType.DATAFLOW_SIDE_EFFECTING),
    )(*[pltpu.with_memory_space_constraint(a, pltpu.HBM) for a in arrays + lands])
    return out[0], out[1], list(out[2:2 + n]), list(out[2 + n:2 + 2 * n]), out[-1]


def _split_wait(send_sems, recv_sems, srcs, lands, after, plan, name):
    n = len(srcs)

    def body(*refs):
        s_refs, l_refs = refs[:n], refs[n:2 * n]
        ssem, rsem = refs[2 * n], refs[2 * n + 1]
        sends, arrivals = plan(s_refs, l_refs, lax.axis_index("x"), lax.axis_index("y"), lax.axis_index("c"))
        for entry in sends:
            _remote(entry, ssem, rsem).wait_send()
        for entry in arrivals:
            _remote(entry, ssem, rsem).wait_recv()

    out = pl.pallas_call(
        body, name=name, in_specs=[_HBM] * (2 * n) + [_SEM, _SEM, pl.BlockSpec(memory_space=pl.ANY)],
        out_specs=[_HBM] * (2 * n), out_shape=[pltpu.HBM(a.shape, a.dtype) for a in srcs + lands],
        input_output_aliases={i: i for i in range(2 * n)},
        compiler_params=pltpu.CompilerParams(has_side_effects=pltpu.SideEffectType.DATAFLOW_SIDE_EFFECTING),
    )(*srcs, *lands, send_sems, recv_sems, after)
    return list(out[:n]), list(out[n:])


def _adam_math(w, g, m, v):
    m2 = ADAM_B1 * m + (1.0 - ADAM_B1) * g
    v2 = ADAM_B2 * v + (1.0 - ADAM_B2) * (g * g)
    m_hat = m2 / (1.0 - ADAM_B1 ** ADAM_STEP)
    v_hat = v2 / (1.0 - ADAM_B2 ** ADAM_STEP)
    return -ADAM_LR * (m_hat / (jnp.sqrt(v_hat) + ADAM_EPS) + ADAM_WD * w), m2, v2


def _adam_big(parts, w, m, v, anchor, name):
    L, rows, cols = w.shape
    flat = [(a, slot) for layer_parts in parts for a, slot in layer_parts]
    per = len(parts[0])

    def body(*refs):
        prefs = refs[:len(flat)]
        w_ref, m_ref, v_ref, _, g_ref, d_ref, m2_ref, v2_ref = refs[len(flat):]
        for li in range(L):
            @pl.when(pl.program_id(0) == li)
            def _():
                c0 = 0
                for pref in prefs[li * per:(li + 1) * per]:
                    acc = pref[0].astype(f32)
                    for s in range(1, pref.shape[0]):
                        acc = acc + pref[s].astype(f32)
                    width = min(acc.shape[1], cols - c0)
                    g_ref[:, c0:c0 + width] = acc[0:rows, 0:width]
                    c0 += width

        d, m2, v2 = _adam_math(w_ref[...], g_ref[...], m_ref[...], v_ref[...])
        d_ref[...] = d
        m2_ref[...] = m2
        v2_ref[...] = v2

    wspec = pl.BlockSpec((None, rows, cols), lambda l: (l, 0, 0))
    in_specs = [pl.BlockSpec((a.shape[0], None) + a.shape[2:], functools.partial(lambda l, slot: (0, slot, 0, 0), slot=slot))
                for a, slot in flat]
    return pl.pallas_call(
        body, grid=(L,), in_specs=in_specs + [wspec] * 3 + [_full_spec((8, 128))], out_specs=[wspec] * 4,
        out_shape=[jax.ShapeDtypeStruct(w.shape, f32)] * 4,
        compiler_params=_cparams(("arbitrary",)), name=name)(*[a for a, _ in flat], w, m, v, anchor)


def _sum_sources(stacked):
    rows = stacked.shape[1]

    def body(s_ref, o_ref):
        acc = s_ref[0]
        for s in range(1, NDEV):
            acc = acc + s_ref[s]
        o_ref[...] = acc

    return pl.pallas_call(body, out_shape=jax.ShapeDtypeStruct((rows, 128), f32), name="sum_small_grads")(stacked)


def _adam_small(w, g, m, v):
    def body(w_ref, g_ref, m_ref, v_ref, d_ref, m2_ref, v2_ref):
        d, m2, v2 = _adam_math(w_ref[...], g_ref[...], m_ref[...], v_ref[...])
        d_ref[...] = d
        m2_ref[...] = m2
        v2_ref[...] = v2

    return pl.pallas_call(body, out_shape=[jax.ShapeDtypeStruct(w.shape, f32)] * 3, name="adam_small")(w, g, m, v)


def _pack_rows(arrs):
    flat = []
    for a in arrs:
        a = a.reshape(-1)
        flat.append(jnp.pad(a, (0, (-a.shape[0]) % 1024)))
    return jnp.concatenate(flat).reshape(-1, 128)


def _unpack_rows(packed, shapes):
    out, off = [], 0
    flat = packed.reshape(-1)
    for s in shapes:
        n = math.prod(s)
        out.append(flat[off:off + n].reshape(s))
        off += n + (-n) % 1024
    return out


def _gather_conv(gathered, shape):
    L, K, c = shape
    return jnp.transpose(gathered, (1, 2, 0, 3)).reshape(L, K, NDEV * c)


def kernel(x, p, positions, ln_ffn1_g, ln_ffn1_b, ffn1_w_gate, ffn1_w_up, ffn1_w_down, w_in, ret_norm_g, lru_conv_w, lru_conv_b, lru_w_a, lru_b_a, lru_w_x, lru_b_x, lru_lambda, gdn_conv_w, gdn_a_log, gdn_dt_bias, gdn_norm_g, w_out, ln_mix_g, ln_mix_b, ffn2_w_gate, ffn2_w_up, ffn2_w_down, ple_w_gate, ple_w_proj, ln_ffn2_g, ln_ffn2_b, loss_target, m_ln_ffn1_g, m_ln_ffn1_b, m_ffn1_w_gate, m_ffn1_w_up, m_ffn1_w_down, m_w_in, m_ret_norm_g, m_lru_conv_w, m_lru_conv_b, m_lru_w_a, m_lru_b_a, m_lru_w_x, m_lru_b_x, m_lru_lambda, m_gdn_conv_w, m_gdn_a_log, m_gdn_dt_bias, m_gdn_norm_g, m_w_out, m_ln_mix_g, m_ln_mix_b, m_ffn2_w_gate, m_ffn2_w_up, m_ffn2_w_down, m_ple_w_gate, m_ple_w_proj, m_ln_ffn2_g, m_ln_ffn2_b, v_ln_ffn1_g, v_ln_ffn1_b, v_ffn1_w_gate, v_ffn1_w_up, v_ffn1_w_down, v_w_in, v_ret_norm_g, v_lru_conv_w, v_lru_conv_b, v_lru_w_a, v_lru_b_a, v_lru_w_x, v_lru_b_x, v_lru_lambda, v_gdn_conv_w, v_gdn_a_log, v_gdn_dt_bias, v_gdn_norm_g, v_w_out, v_ln_mix_g, v_ln_mix_b, v_ffn2_w_gate, v_ffn2_w_up, v_ffn2_w_down, v_ple_w_gate, v_ple_w_proj, v_ln_ffn2_g, v_ln_ffn2_b):
    args = locals()
    W = {n: args[n] for n in WEIGHTS}
    M = {n: args['m_' + n] for n in WEIGHTS}
    V = {n: args['v_' + n] for n in WEIGHTS}
    me = 4 * lax.axis_index("x") + 2 * lax.axis_index("y") + lax.axis_index("c")

    core = lax.axis_index("c")
    chip = 2 * lax.axis_index("x") + lax.axis_index("y")

    packed = _pack_big(W)

    def group(l, name):
        per = {k: packed[k].shape[0] // DEPTH for k in PACKS}
        if name == 'f1':
            return [packed['p384'][l * per['p384']:l * per['p384'] + 2], packed['pd'][l * per['pd']:l * per['pd'] + 1]]
        return [packed['p384'][l * per['p384'] + 2:(l + 1) * per['p384']],
                packed['pd'][l * per['pd'] + 1:(l + 1) * per['pd']]] + [
                    packed[k][l * per[k]:(l + 1) * per[k]] for k in PACKS[2:]]

    def as_weights(arrs):
        G = dict(zip(PACKS, arrs))
        if 'ppp' in G:
            G['wpp'] = jnp.transpose(G.pop('ppp'), (1, 2, 0, 3)).reshape(PLE, D)
        return G

    conv_pack = _pack_rows([W[n] for n in CONV_SHARDED])
    g0 = _gather_two_level(group(0, 'f1') + [conv_pack], "gather_weights")
    g0, rest0 = lax.optimization_barrier((g0, group(0, 'rest')))
    start0 = _split_start(rest0, [(NDEV,) + a.shape for a in rest0], _gather_plan, NDEV - 1, "gather_start_0")
    tok0, all1 = lax.optimization_barrier((start0[4], group(1, 'f1') + group(1, 'rest')))
    start1 = _split_start(all1, [(NDEV,) + a.shape for a in all1], _gather_plan, NDEV - 1, "gather_start_1")
    arrived = {}

    def gather_done(started, after, name):
        srcs, lands = _split_wait(started[0], started[1], started[2], started[3], after, _gather_plan, name)
        return [lax.dynamic_update_slice_in_dim(ld, s[None], me, axis=0) for s, ld in zip(srcs, lands)]

    def fetch(l, name, after):
        if l == 0 and name == 'f1':
            return as_weights(g0[:-1]), tok0[0, 0] + start1[4][0, 0]
        if l == 0:
            return as_weights(gather_done(start0, after, "gather_wait_0")), 0.0
        if name == 'f1':
            arrived[1] = gather_done(start1, after, "gather_wait_1")
            return as_weights(arrived[1][:2]), 0.0
        return as_weights(arrived[1][2:]), 0.0

    conv_all = g0[-1]
    sm = {n: W[n] for n in SMALL}
    conv_shards = [_unpack_rows(conv_all[s], [W[n].shape for n in CONV_SHARDED]) for s in range(NDEV)]
    for i, n in enumerate(CONV_SHARDED):
        sm[n] = _gather_conv(jnp.stack([cs[i] for cs in conv_shards]), W[n].shape)

    received, started = {}, {}

    small_shapes = [sm[n].shape for n in SMALL]
    small_started = []

    def emit(l, name, grads):
        if name == 'small':
            pack = _pack_rows([grads[n] for n in SMALL])
            small_started.extend(_split_start([pack], [(NDEV,) + pack.shape], _gather_plan, NDEV - 1, "small_start"))
            return 0.0
        keys = list(grads)
        arrs = [grads[k] for k in keys]
        gots = _scatter_pairs(arrs, "scatter_pairs")
        owns = [lax.dynamic_index_in_dim(a.reshape((4, 2) + a.shape[1:]), core, axis=1, keepdims=False) for a in arrs]
        pair = [_pair_sum(o, g, "pair_sum_" + k) for k, o, g in zip(keys, owns, gots)]
        started[l, name] = (keys, _split_start(pair, [a.shape for a in pair], _chips_plan, 3,
                                               f"scatter_start_{l}_{name}"))
        return started[l, name][1][4][0, 0]

    def scatter_done(l, name, after):
        keys, st = started[l, name]
        srcs, lands = _split_wait(st[0], st[1], st[2], st[3], after, _chips_plan, f"scatter_wait_{l}_{name}")
        received[l, name] = dict(zip(keys, [
            lax.dynamic_update_slice_in_dim(ld, lax.dynamic_index_in_dim(s, chip, axis=0), chip, axis=0)
            for s, ld in zip(srcs, lands)]))

    loss, grad_x = _local_step(x[0], p[:, 0], positions.reshape(-1, 1), loss_target[0], fetch, emit, sm)
    loss = lax.psum(loss[0, 0], ("x", "y", "c"))
    last = (0, 'f1')
    for l, name in started:
        if (l, name) != last:
            scatter_done(l, name, grad_x)

    anchor = started[last][1][4]
    srcs, lands = _split_wait(small_started[0], small_started[1], small_started[2], small_started[3], anchor,
                              _gather_plan, "small_wait")
    small_all = lax.dynamic_update_slice_in_dim(lands[0], srcs[0][None], me, axis=0)
    small_sum = _unpack_rows(_sum_sources(small_all), small_shapes)
    grads, delta, new_m, new_v = {}, {}, {}, {}
    for n, g in zip(SMALL, small_sum):
        if n in CONV_SHARDED:
            c = W[n].shape[2]
            g = lax.dynamic_slice_in_dim(g, me * c, c, axis=2)
        grads[n] = g

    big_parts = {
        'ffn1_w_gate': [('f1', 'p384', 0)], 'ffn1_w_up': [('f1', 'p384', 1)], 'ffn1_w_down': [('f1', 'pd', 0)],
        'ffn2_w_gate': [('rest', 'p384', 0)], 'ffn2_w_up': [('rest', 'p384', 1)], 'ffn2_w_down': [('rest', 'pd', 0)],
        'w_in': [('rest', 'pr', 0), ('rest', 'pinl', 0), ('rest', 'ping', 0)], 'w_out': [('rest', 'pr', 1)],
        'ple_w_gate': [('rest', 'pr', 2)], 'ple_w_proj': [('rest', 'ppp', 0)],
    }
    def adam(n):
        parts = [[(received[l, grp][k], slot) for grp, k, slot in big_parts[n]] for l in range(DEPTH)]
        grads[n], delta[n], new_m[n], new_v[n] = _adam_big(parts, W[n], M[n], V[n], anchor, "adam_" + n)

    shapes = [W[n].shape for n in SMALL]
    d_s, m_s, v_s = _adam_small(*[_pack_rows([src[n] for n in SMALL]) for src in (W, grads, M, V)])
    for n, dd, mm, vv in zip(SMALL, _unpack_rows(d_s, shapes), _unpack_rows(m_s, shapes), _unpack_rows(v_s, shapes)):
        delta[n], new_m[n], new_v[n] = dd, mm, vv
    waits_last = [n for n in BIG if big_parts[n][0][0] == last[1]]
    for n in BIG:
        if n not in waits_last:
            adam(n)
    done = jnp.stack([d_s[0, 0]] + [delta[n][0, 0, 0] for n in BIG if n not in waits_last])
    scatter_done(*last, done)
    for n in waits_last:
        adam(n)

    return (loss, grad_x[None], *[grads[n] for n in WEIGHTS], *[delta[n] for n in WEIGHTS],
            *[new_m[n] for n in WEIGHTS], *[new_v[n] for n in WEIGHTS])
```
